```python
import math
import jax, jax.numpy as jnp
from jax import lax
import numpy as np

D_MODEL = 1024
BATCH = 8
SEQ = 4096
DEPTH = 4

CONV_WIDTH = D_MODEL
CONV_K = 3
POOL_WIDTH = D_MODEL
POOL_GROUPS = 4
POOL_WINDOWS = (2, 4, 8, 16)
N_HEADS = 16
N_KV_HEADS = 4
HEAD_DIM = D_MODEL // N_HEADS
WINDOW = 128
BLOCK = 128
N_BUCKETS = 32
MAX_DISTANCE = 128
N_BRANCHES = 3
D_FF = -(-8 * D_MODEL // (3 * 256)) * 256

EPS = 1e-6
NEG_INF = -1e30

Q_WIDTH = N_HEADS * HEAD_DIM
KV_WIDTH = N_KV_HEADS * HEAD_DIM
IN_SIZES = (CONV_WIDTH, CONV_WIDTH, CONV_WIDTH, POOL_WIDTH, Q_WIDTH, KV_WIDTH, KV_WIDTH,
            D_MODEL, D_MODEL, D_MODEL)
IN_TOTAL = sum(IN_SIZES)

kernel_name = "hybrid_conv_pool_swa_encoder"


def rms_norm(x, g):
    xf = x.astype(jnp.float32)
    y = xf * lax.rsqrt(jnp.mean(xf * xf, axis=-1, keepdims=True) + EPS)
    return (y * g.astype(jnp.float32)).astype(x.dtype)


def split_points():
    pts, acc = [], 0
    for s in IN_SIZES[:-1]:
        acc += s
        pts.append(acc)
    return pts


def t5_bucket(rel):
    half = N_BUCKETS // 2
    max_exact = half // 2
    ret = jnp.where(rel > 0, half, 0)
    n = jnp.abs(rel)
    nf = jnp.maximum(n, 1).astype(jnp.float32)
    large = max_exact + (jnp.log(nf / max_exact) / math.log(MAX_DISTANCE / max_exact)
                         * (half - max_exact)).astype(jnp.int32)
    large = jnp.minimum(large, half - 1)
    return ret + jnp.where(n < max_exact, n, large)


def short_conv_mixer(b_gate, c_gate, xin, conv_w, w_out):
    u = c_gate * xin
    y = lax.conv_general_dilated(u, conv_w, window_strides=(1,),
                                 padding=[(CONV_K // 2, CONV_K // 2)],
                                 dimension_numbers=("NWC", "WIO", "NWC"),
                                 feature_group_count=u.shape[-1])
    return (b_gate * y) @ w_out


def multiscale_pool_mixer(u, w_pool, pool_scale):
    B, S, W = u.shape
    cg = W // POOL_GROUPS
    uf = u.astype(jnp.float32).reshape(B, S, POOL_GROUPS, cg)
    cs = jnp.pad(jnp.cumsum(uf, axis=1), ((0, 0), (1, 0), (0, 0), (0, 0)))
    t = jnp.arange(S)
    outs = []
    for gi, w in enumerate(POOL_WINDOWS):
        lo = jnp.maximum(t - w // 2, 0)
        hi = jnp.minimum(t + (w - 1 - w // 2), S - 1)
        csg = cs[:, :, gi]
        s = jnp.take(csg, hi + 1, axis=1) - jnp.take(csg, lo, axis=1)
        cnt = (hi - lo + 1).astype(jnp.float32)[None, :, None]
        outs.append(s / cnt - uf[:, :, gi])
    p = jnp.stack(outs, axis=2).astype(u.dtype)
    y = jnp.einsum("bsgc,gcd->bsgd", p, w_pool).reshape(B, S, W)
    return y * pool_scale


def windowed_gqa(q, k, v, rel_bias, sink):
    B, S, _ = q.shape
    nb = S // BLOCK
    G = N_HEADS // N_KV_HEADS
    qb = q.reshape(B, nb, BLOCK, N_KV_HEADS, G, HEAD_DIM)
    pad = ((0, 0), (BLOCK, BLOCK), (0, 0))
    kp = jnp.pad(k, pad).reshape(B, nb + 2, BLOCK, N_KV_HEADS, HEAD_DIM)
    vp = jnp.pad(v, pad).reshape(B, nb + 2, BLOCK, N_KV_HEADS, HEAD_DIM)
    kb = jnp.concatenate([kp[:, :-2], kp[:, 1:-1], kp[:, 2:]], axis=2)
    vb = jnp.concatenate([vp[:, :-2], vp[:, 1:-1], vp[:, 2:]], axis=2)
    scores = jnp.einsum("bnqhgd,bnkhd->bhgnqk", qb, kb,
                        preferred_element_type=jnp.float32) * (HEAD_DIM ** -0.5)
    qi = jnp.arange(BLOCK)[:, None]
    kj = jnp.arange(3 * BLOCK)[None, :]
    rel = kj - BLOCK - qi
    bias = rel_bias[t5_bucket(rel)].astype(jnp.float32)
    bias = jnp.transpose(bias, (2, 0, 1)).reshape(N_KV_HEADS, G, 1, BLOCK, 3 * BLOCK)
    kabs = jnp.arange(nb)[:, None, None] * BLOCK + kj[None] - BLOCK
    valid = (jnp.abs(rel)[None] <= WINDOW) & (kabs >= 0) & (kabs < S)
    scores = jnp.where(valid, scores + bias, NEG_INF)
    sink_l = sink.astype(jnp.float32).reshape(N_KV_HEADS, G, 1, 1, 1)
    m = jnp.maximum(jnp.max(scores, axis=-1, keepdims=True), sink_l)
    p = jnp.exp(scores - m)
    denom = jnp.sum(p, axis=-1, keepdims=True) + jnp.exp(sink_l - m)
    p = (p / denom).astype(v.dtype)
    out = jnp.einsum("bhgnqk,bnkhd->bnqhgd", p, vb)
    return out.reshape(B, S, N_HEADS * HEAD_DIM)


def hybrid_layer(x, w_in, conv_w, w_a_out, w_pool, pool_scale, w_attn_out, sink, w_o,
                 g_mix, g_ffn, w_gu, w_down, rel_bias):
    h = rms_norm(x, g_mix)
    proj = h @ w_in
    b_a, c_a, x_a, u_p, q, k, v, ga, gp, gt = jnp.split(proj, split_points(), axis=-1)
    y_a = short_conv_mixer(b_a, c_a, x_a, conv_w, w_a_out)
    y_p = multiscale_pool_mixer(u_p, w_pool, pool_scale)
    y_t = windowed_gqa(q, k, v, rel_bias, sink) @ w_attn_out
    merged = jax.nn.sigmoid(ga) * y_a + jax.nn.sigmoid(gp) * y_p + jax.nn.sigmoid(gt) * y_t
    x = x + merged @ w_o
    h2 = rms_norm(x, g_ffn)
    gate, up = jnp.split(h2 @ w_gu, [D_FF], axis=-1)
    return x + (jax.nn.silu(gate) * up) @ w_down


def _fwd_setup_inputs(seed: int = 0) -> dict:
    key = jax.random.key(seed)
    ks = jax.random.split(key, 16)
    nrm = lambda k, shape, scale: jax.random.normal(k, shape, jnp.float32) * scale
    cg = POOL_WIDTH // POOL_GROUPS
    return {
        "x": nrm(ks[0], (BATCH, SEQ, D_MODEL), 1.0),
        "w_in": nrm(ks[1], (DEPTH, D_MODEL, IN_TOTAL), D_MODEL ** -0.5),
        "conv_w": nrm(ks[2], (DEPTH, CONV_K, 1, CONV_WIDTH), CONV_K ** -0.5),
        "w_a_out": nrm(ks[3], (DEPTH, CONV_WIDTH, D_MODEL), CONV_WIDTH ** -0.5),
        "w_pool": nrm(ks[4], (DEPTH, POOL_GROUPS, cg, cg), cg ** -0.5),
        "pool_scale": 1.0 + nrm(ks[5], (DEPTH, POOL_WIDTH), 0.02),
        "w_attn_out": nrm(ks[6], (DEPTH, Q_WIDTH, D_MODEL), Q_WIDTH ** -0.5),
        "attn_sink": nrm(ks[7], (DEPTH, N_HEADS), 0.5),
        "w_o": nrm(ks[8], (DEPTH, D_MODEL, D_MODEL), D_MODEL ** -0.5),
        "g_mix": 1.0 + nrm(ks[9], (DEPTH, D_MODEL), 0.02),
        "g_ffn": 1.0 + nrm(ks[10], (DEPTH, D_MODEL), 0.02),
        "w_gu": nrm(ks[11], (DEPTH, D_MODEL, 2 * D_FF), D_MODEL ** -0.5),
        "w_down": nrm(ks[12], (DEPTH, D_FF, D_MODEL), D_FF ** -0.5),
        "rel_bias": nrm(ks[13], (N_BUCKETS, N_HEADS), 0.5),
        "g_final": 1.0 + nrm(ks[14], (D_MODEL,), 0.02),
    }


def _fwd_reference(x, w_in, conv_w, w_a_out, w_pool, pool_scale, w_attn_out, attn_sink, w_o,
              g_mix, g_ffn, w_gu, w_down, rel_bias, g_final):
    for layer in range(DEPTH):
        x = hybrid_layer(x, w_in[layer], conv_w[layer], w_a_out[layer], w_pool[layer],
                         pool_scale[layer], w_attn_out[layer], attn_sink[layer], w_o[layer],
                         g_mix[layer], g_ffn[layer], w_gu[layer], w_down[layer], rel_bias)
    return rms_norm(x, g_final)


import jax as _jax
import jax.numpy as _jnp

TWIN_FORMAT = 'train_step'
FWD_PARAMS = ['x', 'w_in', 'conv_w', 'w_a_out', 'w_pool', 'pool_scale', 'w_attn_out', 'attn_sink', 'w_o', 'g_mix', 'g_ffn', 'w_gu', 'w_down', 'rel_bias', 'g_final']
TWIN_WEIGHTS = ['w_in', 'conv_w', 'w_a_out', 'w_pool', 'pool_scale', 'w_attn_out', 'attn_sink', 'w_o', 'g_mix', 'g_ffn', 'w_gu', 'w_down', 'rel_bias', 'g_final']
TWIN_DIFF_INPUT = 'x'
TWIN_INPUTS = ['x', 'w_in', 'conv_w', 'w_a_out', 'w_pool', 'pool_scale', 'w_attn_out', 'attn_sink', 'w_o', 'g_mix', 'g_ffn', 'w_gu', 'w_down', 'rel_bias', 'g_final', 'loss_target', 'm_w_in', 'm_conv_w', 'm_w_a_out', 'm_w_pool', 'm_pool_scale', 'm_w_attn_out', 'm_attn_sink', 'm_w_o', 'm_g_mix', 'm_g_ffn', 'm_w_gu', 'm_w_down', 'm_rel_bias', 'm_g_final', 'v_w_in', 'v_conv_w', 'v_w_a_out', 'v_w_pool', 'v_pool_scale', 'v_w_attn_out', 'v_attn_sink', 'v_w_o', 'v_g_mix', 'v_g_ffn', 'v_w_gu', 'v_w_down', 'v_rel_bias', 'v_g_final']
TWIN_OUTPUTS = ['loss', 'grad_x', 'grad_w_in', 'grad_conv_w', 'grad_w_a_out', 'grad_w_pool', 'grad_pool_scale', 'grad_w_attn_out', 'grad_attn_sink', 'grad_w_o', 'grad_g_mix', 'grad_g_ffn', 'grad_w_gu', 'grad_w_down', 'grad_rel_bias', 'grad_g_final', 'delta_w_in', 'delta_conv_w', 'delta_w_a_out', 'delta_w_pool', 'delta_pool_scale', 'delta_w_attn_out', 'delta_attn_sink', 'delta_w_o', 'delta_g_mix', 'delta_g_ffn', 'delta_w_gu', 'delta_w_down', 'delta_rel_bias', 'delta_g_final', 'new_m_w_in', 'new_m_conv_w', 'new_m_w_a_out', 'new_m_w_pool', 'new_m_pool_scale', 'new_m_w_attn_out', 'new_m_attn_sink', 'new_m_w_o', 'new_m_g_mix', 'new_m_g_ffn', 'new_m_w_gu', 'new_m_w_down', 'new_m_rel_bias', 'new_m_g_final', 'new_v_w_in', 'new_v_conv_w', 'new_v_w_a_out', 'new_v_w_pool', 'new_v_pool_scale', 'new_v_w_attn_out', 'new_v_attn_sink', 'new_v_w_o', 'new_v_g_mix', 'new_v_g_ffn', 'new_v_w_gu', 'new_v_w_down', 'new_v_rel_bias', 'new_v_g_final']
TWIN_LEAF_KINDS = {'loss': 'loss', 'grad_x': 'grad_x', 'grad_w_in': 'grad_w', 'grad_conv_w': 'grad_w', 'grad_w_a_out': 'grad_w', 'grad_w_pool': 'grad_w', 'grad_pool_scale': 'grad_w', 'grad_w_attn_out': 'grad_w', 'grad_attn_sink': 'grad_w', 'grad_w_o': 'grad_w', 'grad_g_mix': 'grad_w', 'grad_g_ffn': 'grad_w', 'grad_w_gu': 'grad_w', 'grad_w_down': 'grad_w', 'grad_rel_bias': 'grad_w', 'grad_g_final': 'grad_w', 'delta_w_in': 'delta_w', 'delta_conv_w': 'delta_w', 'delta_w_a_out': 'delta_w', 'delta_w_pool': 'delta_w', 'delta_pool_scale': 'delta_w', 'delta_w_attn_out': 'delta_w', 'delta_attn_sink': 'delta_w', 'delta_w_o': 'delta_w', 'delta_g_mix': 'delta_w', 'delta_g_ffn': 'delta_w', 'delta_w_gu': 'delta_w', 'delta_w_down': 'delta_w', 'delta_rel_bias': 'delta_w', 'delta_g_final': 'delta_w', 'new_m_w_in': 'new_m', 'new_m_conv_w': 'new_m', 'new_m_w_a_out': 'new_m', 'new_m_w_pool': 'new_m', 'new_m_pool_scale': 'new_m', 'new_m_w_attn_out': 'new_m', 'new_m_attn_sink': 'new_m', 'new_m_w_o': 'new_m', 'new_m_g_mix': 'new_m', 'new_m_g_ffn': 'new_m', 'new_m_w_gu': 'new_m', 'new_m_w_down': 'new_m', 'new_m_rel_bias': 'new_m', 'new_m_g_final': 'new_m', 'new_v_w_in': 'new_v', 'new_v_conv_w': 'new_v', 'new_v_w_a_out': 'new_v', 'new_v_w_pool': 'new_v', 'new_v_pool_scale': 'new_v', 'new_v_w_attn_out': 'new_v', 'new_v_attn_sink': 'new_v', 'new_v_w_o': 'new_v', 'new_v_g_mix': 'new_v', 'new_v_g_ffn': 'new_v', 'new_v_w_gu': 'new_v', 'new_v_w_down': 'new_v', 'new_v_rel_bias': 'new_v', 'new_v_g_final': 'new_v'}


def _forward(args):
    return _fwd_reference(*[args[k] for k in FWD_PARAMS])


def _output_shape():
    def fwd():
        inp = _fwd_setup_inputs(0)
        return _fwd_reference(*[inp[k] for k in FWD_PARAMS])
    out = _jax.eval_shape(fwd)
    return out.shape, out.dtype

N_MICROBATCH = 1
ADAM_LR = 0.001
ADAM_B1 = 0.9
ADAM_B2 = 0.999
ADAM_EPS = 1e-08
ADAM_WD = 0.01
ADAM_STEP = 10
PER_EXAMPLE_BATCH_AXIS = {'x': 0, 'loss_target': 0}
SHARED_INPUTS = []
_WEIGHT_DTYPES = {'w_in': _jnp.float32, 'conv_w': _jnp.float32, 'w_a_out': _jnp.float32, 'w_pool': _jnp.float32, 'pool_scale': _jnp.float32, 'w_attn_out': _jnp.float32, 'attn_sink': _jnp.float32, 'w_o': _jnp.float32, 'g_mix': _jnp.float32, 'g_ffn': _jnp.float32, 'w_gu': _jnp.float32, 'w_down': _jnp.float32, 'rel_bias': _jnp.float32, 'g_final': _jnp.float32}
MOMENT_SCALE = {'w_in': 6.183810e-02, 'conv_w': 8.846165e-02, 'w_a_out': 8.909512e-02, 'w_pool': 7.899286e-02, 'pool_scale': 7.777825e-02, 'w_attn_out': 1.186500e-02, 'attn_sink': 4.567150e-04, 'w_o': 1.194164e-01, 'g_mix': 1.799731e-01, 'g_ffn': 1.202743e-01, 'w_gu': 5.107218e-02, 'w_down': 8.338691e-02, 'rel_bias': 2.840694e-02, 'g_final': 3.200687e+01}


def _to_microbatches(a, axis):
    t = _jnp.moveaxis(a, axis, 0)
    t = t.reshape((N_MICROBATCH, t.shape[0] // N_MICROBATCH) + t.shape[1:])
    return _jnp.moveaxis(t, 1, axis + 1)


def setup_inputs(seed: int = 0) -> dict:
    inp = _fwd_setup_inputs(seed)
    key = _jax.random.fold_in(_jax.random.key(seed), 7919)
    shape, _ = _output_shape()
    out = dict(inp)
    out["loss_target"] = _jax.random.normal(_jax.random.fold_in(key, 0), shape, _jnp.float32)
    for i, name in enumerate(TWIN_WEIGHTS):
        w = inp[name].astype(_jnp.float32)
        if MOMENT_SCALE is None:
            s = _jnp.sqrt(_jnp.mean(_jnp.square(w)) + 1e-30)
        else:
            s = MOMENT_SCALE[name]
        km, kv = _jax.random.split(_jax.random.fold_in(key, i + 1))
        out[name] = w
        out["m_" + name] = s * _jax.random.normal(km, w.shape, _jnp.float32)
        out["v_" + name] = (s * s) * _jax.random.uniform(kv, w.shape, _jnp.float32, 0.5, 1.5)
    if N_MICROBATCH > 1:
        for name, axis in PER_EXAMPLE_BATCH_AXIS.items():
            out[name] = _to_microbatches(out[name], axis)
    return {'x': out['x'], 'w_in': out['w_in'], 'conv_w': out['conv_w'], 'w_a_out': out['w_a_out'], 'w_pool': out['w_pool'], 'pool_scale': out['pool_scale'], 'w_attn_out': out['w_attn_out'], 'attn_sink': out['attn_sink'], 'w_o': out['w_o'], 'g_mix': out['g_mix'], 'g_ffn': out['g_ffn'], 'w_gu': out['w_gu'], 'w_down': out['w_down'], 'rel_bias': out['rel_bias'], 'g_final': out['g_final'], 'loss_target': out['loss_target'], 'm_w_in': out['m_w_in'], 'm_conv_w': out['m_conv_w'], 'm_w_a_out': out['m_w_a_out'], 'm_w_pool': out['m_w_pool'], 'm_pool_scale': out['m_pool_scale'], 'm_w_attn_out': out['m_w_attn_out'], 'm_attn_sink': out['m_attn_sink'], 'm_w_o': out['m_w_o'], 'm_g_mix': out['m_g_mix'], 'm_g_ffn': out['m_g_ffn'], 'm_w_gu': out['m_w_gu'], 'm_w_down': out['m_w_down'], 'm_rel_bias': out['m_rel_bias'], 'm_g_final': out['m_g_final'], 'v_w_in': out['v_w_in'], 'v_conv_w': out['v_conv_w'], 'v_w_a_out': out['v_w_a_out'], 'v_w_pool': out['v_w_pool'], 'v_pool_scale': out['v_pool_scale'], 'v_w_attn_out': out['v_w_attn_out'], 'v_attn_sink': out['v_attn_sink'], 'v_w_o': out['v_w_o'], 'v_g_mix': out['v_g_mix'], 'v_g_ffn': out['v_g_ffn'], 'v_w_gu': out['v_w_gu'], 'v_w_down': out['v_w_down'], 'v_rel_bias': out['v_rel_bias'], 'v_g_final': out['v_g_final']}


def _loss(weights, diff, rest, loss_target):
    with _jax.named_scope("forward"):
        args = {**rest, TWIN_DIFF_INPUT: diff, **{k: w.astype(_WEIGHT_DTYPES[k]) for k, w in weights.items()}}
        y = _forward(args)
    with _jax.named_scope("loss_head"):
        err = _jnp.square(y.astype(_jnp.float32) - loss_target)
        return 0.5 * _jnp.sum(_jnp.mean(err, axis=-1)) if err.ndim else 0.5 * err


def _adamw(w, g, m, v):
    m = ADAM_B1 * m + (1.0 - ADAM_B1) * g
    v = ADAM_B2 * v + (1.0 - ADAM_B2) * _jnp.square(g)
    m_hat = m / (1.0 - ADAM_B1 ** ADAM_STEP)
    v_hat = v / (1.0 - ADAM_B2 ** ADAM_STEP)
    delta = -ADAM_LR * (m_hat / (_jnp.sqrt(v_hat) + ADAM_EPS) + ADAM_WD * w)
    return delta, m, v


def reference(x, w_in, conv_w, w_a_out, w_pool, pool_scale, w_attn_out, attn_sink, w_o, g_mix, g_ffn, w_gu, w_down, rel_bias, g_final, loss_target, m_w_in, m_conv_w, m_w_a_out, m_w_pool, m_pool_scale, m_w_attn_out, m_attn_sink, m_w_o, m_g_mix, m_g_ffn, m_w_gu, m_w_down, m_rel_bias, m_g_final, v_w_in, v_conv_w, v_w_a_out, v_w_pool, v_pool_scale, v_w_attn_out, v_attn_sink, v_w_o, v_g_mix, v_g_ffn, v_w_gu, v_w_down, v_rel_bias, v_g_final):
    given = dict(x=x, w_in=w_in, conv_w=conv_w, w_a_out=w_a_out, w_pool=w_pool, pool_scale=pool_scale, w_attn_out=w_attn_out, attn_sink=attn_sink, w_o=w_o, g_mix=g_mix, g_ffn=g_ffn, w_gu=w_gu, w_down=w_down, rel_bias=rel_bias, g_final=g_final, loss_target=loss_target, m_w_in=m_w_in, m_conv_w=m_conv_w, m_w_a_out=m_w_a_out, m_w_pool=m_w_pool, m_pool_scale=m_pool_scale, m_w_attn_out=m_w_attn_out, m_attn_sink=m_attn_sink, m_w_o=m_w_o, m_g_mix=m_g_mix, m_g_ffn=m_g_ffn, m_w_gu=m_w_gu, m_w_down=m_w_down, m_rel_bias=m_rel_bias, m_g_final=m_g_final, v_w_in=v_w_in, v_conv_w=v_conv_w, v_w_a_out=v_w_a_out, v_w_pool=v_w_pool, v_pool_scale=v_pool_scale, v_w_attn_out=v_w_attn_out, v_attn_sink=v_attn_sink, v_w_o=v_w_o, v_g_mix=v_g_mix, v_g_ffn=v_g_ffn, v_w_gu=v_w_gu, v_w_down=v_w_down, v_rel_bias=v_rel_bias, v_g_final=v_g_final)
    weights = {n: given[n] for n in TWIN_WEIGHTS}
    shared = {n: given[n] for n in SHARED_INPUTS}
    per_example = {n: given[n] for n in ['x']}
    grad_fn = _jax.value_and_grad(_loss, argnums=(0, 1))

    def one_microbatch(ex, loss_target):
        ex = dict(ex)
        diff = ex.pop(TWIN_DIFF_INPUT)
        return grad_fn(weights, diff, {**shared, **ex}, loss_target)

    if N_MICROBATCH == 1:
        loss, (grad_w, grad_x) = one_microbatch(per_example, given["loss_target"])
    else:
        def body(carry, xs):
            loss_sum, grad_sum = carry
            l_k, (gw_k, gx_k) = one_microbatch(xs[0], xs[1])
            with _jax.named_scope("update"):
                return (loss_sum + l_k, _jax.tree.map(_jnp.add, grad_sum, gw_k)), gx_k

        init = (_jnp.zeros((), _jnp.float32), _jax.tree.map(_jnp.zeros_like, weights))
        (loss, grad_w), grad_x = _jax.lax.scan(body, init, (per_example, given["loss_target"]))
    with _jax.named_scope("update"):
        delta_w, new_m, new_v = {}, {}, {}
        for n in TWIN_WEIGHTS:
            delta_w[n], new_m[n], new_v[n] = _adamw(weights[n], grad_w[n], given["m_" + n], given["v_" + n])
    return (loss, grad_x, *[grad_w[n] for n in TWIN_WEIGHTS], *[delta_w[n] for n in TWIN_WEIGHTS],
            *[new_m[n] for n in TWIN_WEIGHTS], *[new_v[n] for n in TWIN_WEIGHTS])
```

```python
import functools
import math

import jax
import jax.numpy as jnp
from jax import lax
from jax.experimental import pallas as pl
from jax.experimental.pallas import tpu as pltpu

F32 = jnp.float32
BF16 = jnp.bfloat16
MESH = pl.DeviceIdType.MESH

N_DEV = 8
N_HEADS = 16
N_KV_HEADS = 4
HEAD_DIM = 64
GROUP = N_HEADS // N_KV_HEADS
BLOCK = 128
WINDOW = 128
N_BUCKETS = 32
MAX_DISTANCE = 128
POOL_WINDOWS = (2, 4, 8, 16)
POOL_GROUPS = 4
HALO = 8
EPS = 1e-6
NEG_INF = -1e30

ADAM_LR = 0.001
ADAM_B1 = 0.9
ADAM_B2 = 0.999
ADAM_EPS = 1e-08
ADAM_WD = 0.01
ADAM_STEP = 10

LANES = 1024
VMEM_LIMIT_BYTES = 48 * 1024 * 1024


def _params(*sem):
    return pltpu.CompilerParams(dimension_semantics=sem, vmem_limit_bytes=VMEM_LIMIT_BYTES)


def _tile(n, cap):
    if n <= cap:
        return n
    for t in range(cap - cap % 128, 0, -128):
        if n % t == 0:
            return t
    raise ValueError(f"no tile for {n}")


_DIMS = {"nn": (((1,), (0,)), ((), ())), "nt": (((1,), (1,)), ((), ())), "tn": (((0,), (0,)), ((), ()))}


def _matmul(a, b, mode, out_dtype, name, res=None, tm_cap=1024, tn_cap=512, tk_cap=1024):
    if mode == "tn":
        K, M = a.shape
    else:
        M, K = a.shape
    N = b.shape[0] if mode == "nt" else b.shape[1]
    tm, tn, tk = _tile(M, tm_cap), _tile(N, tn_cap), _tile(K, tk_cap)
    nk = K // tk
    a_spec = pl.BlockSpec((tk, tm), lambda i, j, k: (k, i)) if mode == "tn" else pl.BlockSpec((tm, tk), lambda i, j, k: (i, k))
    b_spec = pl.BlockSpec((tn, tk), lambda i, j, k: (j, k)) if mode == "nt" else pl.BlockSpec((tk, tn), lambda i, j, k: (k, j))
    o_spec = pl.BlockSpec((tm, tn), lambda i, j, k: (i, j))
    dims = _DIMS[mode]
    has_res = res is not None

    def body(*refs):
        a_ref, b_ref = refs[0], refs[1]
        res_ref = refs[2] if has_res else None
        o_ref, acc_ref = refs[-2], refs[-1]
        k = pl.program_id(2)
        part = lax.dot_general(a_ref[...], b_ref[...], dims, preferred_element_type=F32)

        @pl.when(k == 0)
        def _():
            acc_ref[...] = part

        @pl.when(k > 0)
        def _():
            acc_ref[...] += part

        @pl.when(k == nk - 1)
        def _():
            out = acc_ref[...]
            if has_res:
                out = out + res_ref[...]
            o_ref[...] = out.astype(out_dtype)

    in_specs = [a_spec, b_spec] + ([o_spec] if has_res else [])
    args = (a, b) + ((res,) if has_res else ())
    return pl.pallas_call(
        body, name=name, grid=(M // tm, N // tn, nk),
        in_specs=in_specs, out_specs=o_spec,
        out_shape=jax.ShapeDtypeStruct((M, N), out_dtype),
        scratch_shapes=[pltpu.VMEM((tm, tn), F32)],
        compiler_params=_params("parallel", "parallel", "arbitrary"),
    )(*args)


def _pool_mm(a, w, mode, out_dtype, name):
    T = a.shape[0]
    G = POOL_GROUPS
    cg = a.shape[1] // G
    tm = _tile(T, 1024)
    nt = T // tm
    dims = _DIMS[mode]
    if mode == "tn":
        def body(a_ref, d_ref, o_ref):
            part = lax.dot_general(a_ref[...], d_ref[...], dims, preferred_element_type=F32)

            @pl.when(pl.program_id(1) == 0)
            def _():
                o_ref[...] = part

            @pl.when(pl.program_id(1) > 0)
            def _():
                o_ref[...] += part

        return pl.pallas_call(
            body, name=name, grid=(G, nt),
            in_specs=[pl.BlockSpec((tm, cg), lambda g, i: (i, g)), pl.BlockSpec((tm, cg), lambda g, i: (i, g))],
            out_specs=pl.BlockSpec((None, cg, cg), lambda g, i: (g, 0, 0)),
            out_shape=jax.ShapeDtypeStruct((G, cg, cg), F32),
            compiler_params=_params("parallel", "arbitrary"),
        )(a, w)

    def body(a_ref, w_ref, o_ref):
        o_ref[...] = lax.dot_general(a_ref[...], w_ref[...], dims, preferred_element_type=F32).astype(out_dtype)

    return pl.pallas_call(
        body, name=name, grid=(G, nt),
        in_specs=[pl.BlockSpec((tm, cg), lambda g, i: (i, g)), pl.BlockSpec((None, cg, cg), lambda g, i: (g, 0, 0))],
        out_specs=pl.BlockSpec((tm, cg), lambda g, i: (i, g)),
        out_shape=jax.ShapeDtypeStruct((T, G * cg), out_dtype),
        compiler_params=_params("parallel", "parallel"),
    )(a, w)


ROWS = 256


def _row_spec(d, col=0, rows=ROWS):
    return pl.BlockSpec((rows, d), lambda i, col=col: (i, col))


def _const_spec(shape):
    return pl.BlockSpec(shape, lambda *_: (0,) * len(shape))


def _rms_fwd(x, g, name):
    T, D = x.shape

    def body(x_ref, g_ref, h_ref):
        xv = x_ref[...]
        r = lax.rsqrt(jnp.mean(xv * xv, axis=-1, keepdims=True) + EPS)
        h_ref[...] = (xv * r * g_ref[...]).astype(BF16)

    return pl.pallas_call(
        body, name=name, grid=(T // ROWS,),
        in_specs=[_row_spec(D), _const_spec((1, D))], out_specs=_row_spec(D),
        out_shape=jax.ShapeDtypeStruct((T, D), BF16), compiler_params=_params("parallel"),
    )(x, g)


def _accumulate(ref, part):
    first = pl.program_id(0) == 0

    @pl.when(first)
    def _():
        ref[...] = part

    @pl.when(jnp.logical_not(first))
    def _():
        ref[...] += part


def _rms_bwd(x, g, dh, dres, name):
    T, D = x.shape

    def body(x_ref, g_ref, dh_ref, dres_ref, dx_ref, dxb_ref, dg_ref):
        xv = x_ref[...]
        r = lax.rsqrt(jnp.mean(xv * xv, axis=-1, keepdims=True) + EPS)
        xhat = xv * r
        dh_v = dh_ref[...]
        dxhat = dh_v * g_ref[...]
        dx = dres_ref[...] + r * (dxhat - xhat * jnp.mean(dxhat * xhat, axis=-1, keepdims=True))
        dx_ref[...] = dx
        dxb_ref[...] = dx.astype(BF16)
        _accumulate(dg_ref, jnp.sum(dh_v * xhat, axis=0, keepdims=True))

    return pl.pallas_call(
        body, name=name, grid=(T // ROWS,),
        in_specs=[_row_spec(D), _const_spec((1, D)), _row_spec(D), _row_spec(D)],
        out_specs=[_row_spec(D), _row_spec(D), _const_spec((1, D))],
        out_shape=[jax.ShapeDtypeStruct((T, D), F32), jax.ShapeDtypeStruct((T, D), BF16),
                   jax.ShapeDtypeStruct((1, D), F32)],
        compiler_params=_params("arbitrary"),
    )(x, g, dh, dres)


def _loss_head(x, g, target, name):
    T, D = x.shape

    def body(x_ref, g_ref, t_ref, loss_ref, dx_ref, dxb_ref, dg_ref):
        xv = x_ref[...]
        gv = g_ref[...]
        r = lax.rsqrt(jnp.mean(xv * xv, axis=-1, keepdims=True) + EPS)
        xhat = xv * r
        err = xhat * gv - t_ref[...]
        loss = 0.5 * jnp.sum(jnp.mean(err * err, axis=-1, keepdims=True), axis=0, keepdims=True)
        dy = err * (1.0 / D)
        dxhat = dy * gv
        dx = r * (dxhat - xhat * jnp.mean(dxhat * xhat, axis=-1, keepdims=True))
        dx_ref[...] = dx
        dxb_ref[...] = dx.astype(BF16)
        _accumulate(loss_ref, loss)
        _accumulate(dg_ref, jnp.sum(dy * xhat, axis=0, keepdims=True))

    return pl.pallas_call(
        body, name=name, grid=(T // ROWS,),
        in_specs=[_row_spec(D), _const_spec((1, D)), _row_spec(D)],
        out_specs=[_const_spec((1, 1)), _row_spec(D), _row_spec(D), _const_spec((1, D))],
        out_shape=[jax.ShapeDtypeStruct((1, 1), F32), jax.ShapeDtypeStruct((T, D), F32),
                   jax.ShapeDtypeStruct((T, D), BF16), jax.ShapeDtypeStruct((1, D), F32)],
        compiler_params=_params("arbitrary"),
    )(x, g, target)


def _halo_specs(d, col, n_blocks):
    per = ROWS // HALO
    last = n_blocks * per - 1
    prev = pl.BlockSpec((HALO, d), lambda i, col=col: (jnp.maximum(i * per - 1, 0), col))
    nxt = pl.BlockSpec((HALO, d), lambda i, col=col: (jnp.minimum((i + 1) * per, last), col))
    return prev, nxt


def _with_halo(prev, cur, nxt, n_blocks):
    i = pl.program_id(0)
    prev = jnp.where(i > 0, prev, 0.0)
    nxt = jnp.where(i < n_blocks - 1, nxt, 0.0)
    return jnp.concatenate([prev, cur, nxt], axis=0)


def _shift(ext, k):
    n = ext.shape[0]
    v = ext if k == 0 else pltpu.roll(ext, (-k) % n, 0)
    return v[HALO:HALO + ROWS]


def _shift_full(ext, k):
    n = ext.shape[0]
    return pltpu.roll(ext, (-k) % n, 0)


def _pool_counts(T):
    n = ROWS + 2 * HALO
    t = pl.program_id(0) * ROWS - HALO + lax.broadcasted_iota(jnp.int32, (n, 1), 0)
    out = []
    for w in POOL_WINDOWS:
        lo = jnp.maximum(t - w // 2, 0)
        hi = jnp.minimum(t + (w - 1 - w // 2), T - 1)
        out.append(jnp.maximum(hi - lo + 1, 1).astype(F32))
    return out


def _window_sums(e, sign):
    s2 = e + _shift_full(e, -sign)
    s4 = _shift_full(s2, -1) + _shift_full(s2, 1)
    s8 = _shift_full(s4, -2) + _shift_full(s4, 2)
    s16 = _shift_full(s8, -4) + _shift_full(s8, 4)
    return s2, s4, s8, s16


def _mixer_fwd(proj, conv_w, name):
    T = proj.shape[0]
    W = conv_w.shape[1]
    nb = T // ROWS
    cg = W // POOL_GROUPS

    def body(b_ref, c_ref, x_ref, u_ref, cp_ref, cn_ref, xp_ref, xn_ref, up_ref, un_ref, w_ref, z_ref, p_ref):
        uc = _with_halo(cp_ref[...] * xp_ref[...], c_ref[...] * x_ref[...], cn_ref[...] * xn_ref[...], nb)
        w0, w1, w2 = w_ref[0:1, :], w_ref[1:2, :], w_ref[2:3, :]
        y = w0 * _shift(uc, -1) + w1 * _shift(uc, 0) + w2 * _shift(uc, 1)
        z_ref[...] = (b_ref[...] * y).astype(BF16)
        e = _with_halo(up_ref[...], u_ref[...], un_ref[...], nb)
        counts = _pool_counts(T)
        for gi in range(POOL_GROUPS):
            eg = e[:, gi * cg:(gi + 1) * cg]
            s = _window_sums(eg, 1)[gi]
            p = s[HALO:HALO + ROWS] / counts[gi][HALO:HALO + ROWS] - eg[HALO:HALO + ROWS]
            p_ref[:, gi * cg:(gi + 1) * cg] = p.astype(BF16)

    halo = [s for col in (1, 2, 3) for s in _halo_specs(W, col, nb)]
    return pl.pallas_call(
        body, name=name, grid=(nb,),
        in_specs=[_row_spec(W, 0), _row_spec(W, 1), _row_spec(W, 2), _row_spec(W, 3)] + halo + [_const_spec((8, W))],
        out_specs=[_row_spec(W), _row_spec(W)],
        out_shape=[jax.ShapeDtypeStruct((T, W), BF16), jax.ShapeDtypeStruct((T, W), BF16)],
        compiler_params=_params("parallel"),
    )(proj, proj, proj, proj, proj, proj, proj, proj, proj, proj, conv_w)


def _mixer_bwd(proj, conv_w, dz, dp, name):
    T = proj.shape[0]
    W = conv_w.shape[1]
    nb = T // ROWS
    cg = W // POOL_GROUPS

    def body(b_ref, c_ref, x_ref, dz_ref, dp_ref,
             bp_ref, bn_ref, cp_ref, cn_ref, xp_ref, xn_ref, dzp_ref, dzn_ref, dpp_ref, dpn_ref, w_ref,
             db_ref, dc_ref, dx_ref, du_ref, dw_ref):
        cv, xv = c_ref[...], x_ref[...]
        uc = _with_halo(cp_ref[...] * xp_ref[...], cv * xv, cn_ref[...] * xn_ref[...], nb)
        dy = _with_halo(dzp_ref[...] * bp_ref[...], dz_ref[...] * b_ref[...], dzn_ref[...] * bn_ref[...], nb)
        w0, w1, w2 = w_ref[0:1, :], w_ref[1:2, :], w_ref[2:3, :]
        um, u0, up = _shift(uc, -1), _shift(uc, 0), _shift(uc, 1)
        db_ref[...] = (dz_ref[...] * (w0 * um + w1 * u0 + w2 * up)).astype(BF16)
        dy0 = _shift(dy, 0)
        duc = w0 * _shift(dy, 1) + w1 * dy0 + w2 * _shift(dy, -1)
        dc_ref[...] = (duc * xv).astype(BF16)
        dx_ref[...] = (duc * cv).astype(BF16)
        row = lax.broadcasted_iota(jnp.int32, (8, W), 0)
        dw = jnp.where(row == 0, jnp.sum(dy0 * um, axis=0, keepdims=True),
                       jnp.where(row == 1, jnp.sum(dy0 * u0, axis=0, keepdims=True),
                                 jnp.where(row == 2, jnp.sum(dy0 * up, axis=0, keepdims=True), 0.0)))
        _accumulate(dw_ref, dw)
        d = _with_halo(dpp_ref[...], dp_ref[...], dpn_ref[...], nb)
        counts = _pool_counts(T)
        for gi in range(POOL_GROUPS):
            dg = d[:, gi * cg:(gi + 1) * cg]
            s = _window_sums(dg / counts[gi], -1)[gi]
            du_ref[:, gi * cg:(gi + 1) * cg] = (s[HALO:HALO + ROWS] - dg[HALO:HALO + ROWS]).astype(BF16)

    def halo(col):
        return list(_halo_specs(W, col, nb))

    out = jax.ShapeDtypeStruct((T, W), BF16)
    return pl.pallas_call(
        body, name=name, grid=(nb,),
        in_specs=[_row_spec(W, 0), _row_spec(W, 1), _row_spec(W, 2), _row_spec(W), _row_spec(W)]
        + halo(0) + halo(1) + halo(2) + halo(0) + halo(0) + [_const_spec((8, W))],
        out_specs=[_row_spec(W)] * 4 + [_const_spec((8, W))],
        out_shape=[out, out, out, out, jax.ShapeDtypeStruct((8, W), F32)],
        compiler_params=_params("arbitrary"),
    )(proj, proj, proj, dz, dp, proj, proj, proj, proj, proj, proj, dz, dz, dp, dp, conv_w)


def _t5_bucket(rel):
    half = N_BUCKETS // 2
    max_exact = half // 2
    ret = jnp.where(rel > 0, half, 0)
    n = jnp.abs(rel)
    nf = jnp.maximum(n, 1).astype(jnp.float32)
    large = max_exact + (jnp.log(nf / max_exact) / math.log(MAX_DISTANCE / max_exact)
                         * (half - max_exact)).astype(jnp.int32)
    large = jnp.minimum(large, half - 1)
    return ret + jnp.where(n < max_exact, n, large)


def _bucket_table():
    qi = jnp.arange(BLOCK)[:, None]
    kj = jnp.arange(3 * BLOCK)[None, :]
    rel = kj - BLOCK - qi
    return jnp.where(jnp.abs(rel) <= WINDOW, _t5_bucket(rel), -1).astype(jnp.int32)


def _bias_table(rel_bias, bucket, name):
    def body(rb_ref, bucket_ref, o_ref):
        h = pl.program_id(0)
        bk = bucket_ref[...]
        acc = jnp.full(bk.shape, NEG_INF, F32)
        for b in range(N_BUCKETS):
            acc = jnp.where(bk == b, rb_ref[b, h], acc)
        o_ref[...] = acc

    return pl.pallas_call(
        body, name=name, grid=(N_HEADS,),
        in_specs=[pl.BlockSpec(memory_space=pltpu.SMEM), _const_spec((BLOCK, 3 * BLOCK))],
        out_specs=pl.BlockSpec((None, BLOCK, 3 * BLOCK), lambda h: (h, 0, 0)),
        out_shape=jax.ShapeDtypeStruct((N_HEADS, BLOCK, 3 * BLOCK), F32),
        compiler_params=_params("parallel"),
    )(rel_bias, bucket)


def _bias_grad(ds_sum, bucket, name):
    def body(ds_ref, bucket_ref, o_ref):
        bk = bucket_ref[...]
        ds = ds_ref[...]
        row = lax.broadcasted_iota(jnp.int32, (N_BUCKETS, 128), 0)
        acc = jnp.zeros((N_BUCKETS, 128), F32)
        for b in range(N_BUCKETS):
            s = jnp.sum(jnp.sum(jnp.where(bk == b, ds, 0.0), axis=1, keepdims=True), axis=0, keepdims=True)
            acc = jnp.where(row == b, s, acc)
        o_ref[...] = acc

    return pl.pallas_call(
        body, name=name, grid=(N_HEADS,),
        in_specs=[pl.BlockSpec((None, BLOCK, 3 * BLOCK), lambda h: (h, 0, 0)), _const_spec((BLOCK, 3 * BLOCK))],
        out_specs=pl.BlockSpec((None, N_BUCKETS, 128), lambda h: (h, 0, 0)),
        out_shape=jax.ShapeDtypeStruct((N_HEADS, N_BUCKETS, 128), F32),
        compiler_params=_params("parallel"),
    )(ds_sum, bucket)


def _scores(q_ref, k_ref, bias_ref, nb):
    i = pl.program_id(1)
    q = q_ref[...].reshape(GROUP * BLOCK, HEAD_DIM)
    k3 = k_ref[pl.ds(pl.multiple_of(i * BLOCK, BLOCK), 3 * BLOCK), :]
    s = lax.dot_general(q, k3, _DIMS["nt"], preferred_element_type=F32) * (HEAD_DIM ** -0.5)
    s = s + bias_ref[...].reshape(GROUP * BLOCK, 3 * BLOCK)
    kj = lax.broadcasted_iota(jnp.int32, (1, 3 * BLOCK), 1)
    outside = jnp.logical_or(jnp.logical_and(i == 0, kj < BLOCK), jnp.logical_and(i == nb - 1, kj >= 2 * BLOCK))
    return q, k3, jnp.where(outside, NEG_INF, s)


def _attn_specs(T):
    qspec = pl.BlockSpec((GROUP, BLOCK, HEAD_DIM), lambda j, i: (j, i, 0))
    kspec = pl.BlockSpec((None, T + 2 * BLOCK, HEAD_DIM), lambda j, i: (j, 0, 0))
    bspec = pl.BlockSpec((GROUP, BLOCK, 3 * BLOCK), lambda j, i: (j, 0, 0))
    cspec = pl.BlockSpec((GROUP, BLOCK, 1), lambda j, i: (j, i, 0))
    sspec = pl.BlockSpec((GROUP, BLOCK, 1), lambda j, i: (j, 0, 0))
    return qspec, kspec, bspec, cspec, sspec


def _attn_fwd(q, k, v, bias, sink, name):
    T = q.shape[1]
    nb = T // BLOCK
    qspec, kspec, bspec, cspec, sspec = _attn_specs(T)

    def body(q_ref, k_ref, v_ref, bias_ref, sink_ref, o_ref, lse_ref):
        i = pl.program_id(1)
        _, _, s = _scores(q_ref, k_ref, bias_ref, nb)
        sk = sink_ref[...].reshape(GROUP * BLOCK, 1)
        m = jnp.maximum(jnp.max(s, axis=-1, keepdims=True), sk)
        p = jnp.exp(s - m)
        denom = jnp.sum(p, axis=-1, keepdims=True) + jnp.exp(sk - m)
        p = (p / denom).astype(BF16)
        v3 = v_ref[pl.ds(pl.multiple_of(i * BLOCK, BLOCK), 3 * BLOCK), :]
        out = lax.dot_general(p, v3, _DIMS["nn"], preferred_element_type=F32)
        o_ref[...] = out.reshape(GROUP, BLOCK, HEAD_DIM).astype(BF16)
        lse_ref[...] = (m + jnp.log(denom)).reshape(GROUP, BLOCK, 1)

    return pl.pallas_call(
        body, name=name, grid=(N_KV_HEADS, nb),
        in_specs=[qspec, kspec, kspec, bspec, sspec], out_specs=[qspec, cspec],
        out_shape=[jax.ShapeDtypeStruct((N_HEADS, T, HEAD_DIM), BF16), jax.ShapeDtypeStruct((N_HEADS, T, 1), F32)],
        compiler_params=_params("parallel", "parallel"),
    )(q, k, v, bias, sink)


def _attn_bwd(q, k, v, bias, sink, out, lse, dout, name):
    T = q.shape[1]
    nb = T // BLOCK
    qspec, kspec, bspec, cspec, sspec = _attn_specs(T)
    scale = HEAD_DIM ** -0.5

    def body(q_ref, k_ref, v_ref, bias_ref, sink_ref, o_ref, lse_ref, do_ref, dq_ref, dk_ref, dv_ref, ds_ref, dsink_ref):
        i = pl.program_id(1)
        first = i == 0
        q4, k3, s = _scores(q_ref, k_ref, bias_ref, nb)
        lse = lse_ref[...].reshape(GROUP * BLOCK, 1)
        p = jnp.exp(s - lse)
        rows = pl.ds(pl.multiple_of(i * BLOCK, BLOCK), 3 * BLOCK)
        v3 = v_ref[rows, :]
        do4 = do_ref[...].reshape(GROUP * BLOCK, HEAD_DIM)
        dp = lax.dot_general(do4, v3, _DIMS["nt"], preferred_element_type=F32)
        delta = jnp.sum(do4.astype(F32) * o_ref[...].reshape(GROUP * BLOCK, HEAD_DIM).astype(F32), axis=-1, keepdims=True)
        ds = p * (dp - delta)
        dsb = ds.astype(BF16)
        dq = lax.dot_general(dsb, k3, _DIMS["nn"], preferred_element_type=F32) * scale
        dq_ref[...] = dq.reshape(GROUP, BLOCK, HEAD_DIM).astype(BF16)
        dk3 = lax.dot_general(dsb, q4, _DIMS["tn"], preferred_element_type=F32) * scale
        dv3 = lax.dot_general(p.astype(BF16), do4, _DIMS["tn"], preferred_element_type=F32)
        p_sink = jnp.exp(sink_ref[...].reshape(GROUP * BLOCK, 1) - lse)
        dsink = jnp.sum((-p_sink * delta).reshape(GROUP, BLOCK, 1), axis=1, keepdims=True)

        @pl.when(first)
        def _():
            dk_ref[...] = jnp.zeros(dk_ref.shape, F32)
            dv_ref[...] = jnp.zeros(dv_ref.shape, F32)
            ds_ref[...] = jnp.zeros(ds_ref.shape, F32)
            dsink_ref[...] = jnp.zeros(dsink_ref.shape, F32)

        dk_ref[rows, :] += dk3
        dv_ref[rows, :] += dv3
        ds_ref[...] += ds.reshape(GROUP, BLOCK, 3 * BLOCK)
        dsink_ref[...] += dsink

    kv_out = jax.ShapeDtypeStruct((N_KV_HEADS, T + 2 * BLOCK, HEAD_DIM), F32)
    return pl.pallas_call(
        body, name=name, grid=(N_KV_HEADS, nb),
        in_specs=[qspec, kspec, kspec, bspec, sspec, qspec, cspec, qspec],
        out_specs=[qspec, kspec, kspec, bspec, pl.BlockSpec((GROUP, 1, 1), lambda j, i: (j, 0, 0))],
        out_shape=[jax.ShapeDtypeStruct((N_HEADS, T, HEAD_DIM), BF16), kv_out, kv_out,
                   jax.ShapeDtypeStruct((N_HEADS, BLOCK, 3 * BLOCK), F32), jax.ShapeDtypeStruct((N_HEADS, 1, 1), F32)],
        compiler_params=_params("parallel", "arbitrary"),
    )(q, k, v, bias, sink, out, lse, dout)


def _to_heads(a, n_heads, pad):
    T = a.shape[0]
    a = jnp.transpose(a.astype(BF16).reshape(T, n_heads, HEAD_DIM), (1, 0, 2))
    return jnp.pad(a, ((0, 0), (BLOCK, BLOCK), (0, 0))) if pad else a


def _from_heads(a):
    return jnp.transpose(a, (1, 0, 2)).reshape(a.shape[1], a.shape[0] * HEAD_DIM)


GATE_COLS = 512


def _sigmoid(x):
    return 1.0 / (1.0 + jnp.exp(-x))


def _merge_specs(D, gate_off):
    nc = D // GATE_COLS
    base = gate_off // GATE_COLS

    def spec(off):
        return pl.BlockSpec((ROWS, GATE_COLS), lambda j, i, off=off: (i, off + j))

    gates = [spec(base), spec(base + nc), spec(base + 2 * nc)]
    return nc, gates, spec(0), pl.BlockSpec((1, GATE_COLS), lambda j, i: (0, j))


def _merge_fwd(proj, gate_off, ya, yp, yt, scale, name):
    T, D = ya.shape
    nc, gates, yspec, sspec = _merge_specs(D, gate_off)

    def body(ga_ref, gp_ref, gt_ref, ya_ref, yp_ref, yt_ref, s_ref, o_ref):
        merged = (_sigmoid(ga_ref[...]) * ya_ref[...] + _sigmoid(gp_ref[...]) * (yp_ref[...] * s_ref[...])
                  + _sigmoid(gt_ref[...]) * yt_ref[...])
        o_ref[...] = merged.astype(BF16)

    return pl.pallas_call(
        body, name=name, grid=(nc, T // ROWS),
        in_specs=gates + [yspec, yspec, yspec, sspec], out_specs=yspec,
        out_shape=jax.ShapeDtypeStruct((T, D), BF16), compiler_params=_params("parallel", "parallel"),
    )(proj, proj, proj, ya, yp, yt, scale)


def _merge_bwd(proj, gate_off, ya, yp, yt, scale, dm, name):
    T, D = ya.shape
    nc, gates, yspec, sspec = _merge_specs(D, gate_off)

    def body(ga_ref, gp_ref, gt_ref, ya_ref, yp_ref, yt_ref, s_ref, dm_ref,
             dga_ref, dgp_ref, dgt_ref, dya_ref, dyp_ref, dyt_ref, ds_ref):
        dm_v = dm_ref[...]
        sa, sp, st = _sigmoid(ga_ref[...]), _sigmoid(gp_ref[...]), _sigmoid(gt_ref[...])
        yp_v, s_v = yp_ref[...], s_ref[...]
        dga_ref[...] = (dm_v * ya_ref[...] * sa * (1.0 - sa)).astype(BF16)
        dgp_ref[...] = (dm_v * (yp_v * s_v) * sp * (1.0 - sp)).astype(BF16)
        dgt_ref[...] = (dm_v * yt_ref[...] * st * (1.0 - st)).astype(BF16)
        dya_ref[...] = (dm_v * sa).astype(BF16)
        dyps = dm_v * sp
        dyp_ref[...] = (dyps * s_v).astype(BF16)
        dyt_ref[...] = (dm_v * st).astype(BF16)
        part = jnp.sum(dyps * yp_v, axis=0, keepdims=True)
        first = pl.program_id(1) == 0

        @pl.when(first)
        def _():
            ds_ref[...] = part

        @pl.when(jnp.logical_not(first))
        def _():
            ds_ref[...] += part

    out = jax.ShapeDtypeStruct((T, D), BF16)
    return pl.pallas_call(
        body, name=name, grid=(nc, T // ROWS),
        in_specs=gates + [yspec, yspec, yspec, sspec, yspec], out_specs=[yspec] * 6 + [sspec],
        out_shape=[out] * 6 + [jax.ShapeDtypeStruct((1, D), F32)],
        compiler_params=_params("parallel", "arbitrary"),
    )(proj, proj, proj, ya, yp, yt, scale, dm)


FF_COLS = 256


def _swiglu_fwd(gu, name):
    T = gu.shape[0]
    F = gu.shape[1] // 2
    nc = F // FF_COLS

    def body(g_ref, u_ref, o_ref):
        g = g_ref[...]
        o_ref[...] = (g * _sigmoid(g) * u_ref[...]).astype(BF16)

    spec = pl.BlockSpec((ROWS, FF_COLS), lambda i, j: (i, j))
    return pl.pallas_call(
        body, name=name, grid=(T // ROWS, nc),
        in_specs=[spec, pl.BlockSpec((ROWS, FF_COLS), lambda i, j: (i, nc + j))], out_specs=spec,
        out_shape=jax.ShapeDtypeStruct((T, F), BF16), compiler_params=_params("parallel", "parallel"),
    )(gu, gu)


def _swiglu_bwd(gu, dact, name):
    T = gu.shape[0]
    F = gu.shape[1] // 2
    nc = F // FF_COLS

    def body(g_ref, u_ref, d_ref, dg_ref, du_ref):
        g, d = g_ref[...], d_ref[...]
        sg = _sigmoid(g)
        dg_ref[...] = (d * u_ref[...] * sg * (1.0 + g * (1.0 - sg))).astype(BF16)
        du_ref[...] = (d * g * sg).astype(BF16)

    spec = pl.BlockSpec((ROWS, FF_COLS), lambda i, j: (i, j))
    out = jax.ShapeDtypeStruct((T, F), BF16)
    return pl.pallas_call(
        body, name=name, grid=(T // ROWS, nc),
        in_specs=[spec, pl.BlockSpec((ROWS, FF_COLS), lambda i, j: (i, nc + j)), spec], out_specs=[spec, spec],
        out_shape=[out, out], compiler_params=_params("parallel", "parallel"),
    )(gu, gu, dact)


def _local_step(x, target, wts, small):
    T, D = x.shape
    depth = wts["w_in"].shape[0]
    gate_off = wts["w_in"].shape[2] - 3 * D
    q_off = 4 * D
    kv_w = N_KV_HEADS * HEAD_DIM
    bucket = _bucket_table()
    bias = _bias_table(small["rel_bias"], bucket, "bias_table")

    saved = []
    for l in range(depth):
        n = f"l{l}_"
        h = _rms_fwd(x, small["g_mix"][l], n + "rms_mix")
        proj = _matmul(h, wts["w_in"][l], "nn", F32, n + "proj")
        z, p = _mixer_fwd(proj, small["conv_w"][l], n + "mixer")
        q = _to_heads(proj[:, q_off:q_off + D], N_HEADS, False)
        k = _to_heads(proj[:, q_off + D:q_off + D + kv_w], N_KV_HEADS, True)
        v = _to_heads(proj[:, q_off + D + kv_w:q_off + D + 2 * kv_w], N_KV_HEADS, True)
        sink = jnp.broadcast_to(small["attn_sink"][l][:, None, None], (N_HEADS, BLOCK, 1))
        att_h, lse = _attn_fwd(q, k, v, bias, sink, n + "attn")
        att = _from_heads(att_h)
        ya = _matmul(z, wts["w_a_out"][l], "nn", F32, n + "ya")
        yp = _pool_mm(p, wts["w_pool"][l], "nn", F32, n + "yp")
        yt = _matmul(att, wts["w_attn_out"][l], "nn", F32, n + "yt")
        merged = _merge_fwd(proj, gate_off, ya, yp, yt, small["pool_scale"][l], n + "merge")
        x1 = _matmul(merged, wts["w_o"][l], "nn", F32, n + "x1", res=x)
        h2 = _rms_fwd(x1, small["g_ffn"][l], n + "rms_ffn")
        gu = _matmul(h2, wts["w_gu"][l], "nn", F32, n + "gu")
        act = _swiglu_fwd(gu, n + "swiglu")
        x2 = _matmul(act, wts["w_down"][l], "nn", F32, n + "x2", res=x1)
        saved.append(dict(x=x, h=h, proj=proj, z=z, p=p, q=q, k=k, v=v, sink=sink, att_h=att_h, lse=lse, att=att,
                          ya=ya, yp=yp, yt=yt, merged=merged, x1=x1, h2=h2, gu=gu, act=act))
        x = x2

    loss, dx, dxb, dg_final = _loss_head(x, small["g_final"], target, "loss_head")

    gw = {k_: [None] * depth for k_ in wts}
    gs = {k_: [None] * depth for k_ in ("conv_w", "pool_scale", "g_mix", "g_ffn", "attn_sink")}
    ds_total = None
    for l in reversed(range(depth)):
        n = f"l{l}_b_"
        s = saved[l]
        gw["w_down"][l] = _matmul(s["act"], dxb, "tn", F32, n + "dw_down", tm_cap=512)
        dact = _matmul(dxb, wts["w_down"][l], "nt", F32, n + "dact")
        dgate, dup = _swiglu_bwd(s["gu"], dact, n + "swiglu")
        dgu = jnp.concatenate([dgate, dup], axis=1)
        gw["w_gu"][l] = _matmul(s["h2"], dgu, "tn", F32, n + "dw_gu")
        dh2 = _matmul(dgu, wts["w_gu"][l], "nt", F32, n + "dh2")
        dx1, dx1b, gs["g_ffn"][l] = _rms_bwd(s["x1"], small["g_ffn"][l], dh2, dx, n + "rms_ffn")
        gw["w_o"][l] = _matmul(s["merged"], dx1b, "tn", F32, n + "dw_o")
        dm = _matmul(dx1b, wts["w_o"][l], "nt", F32, n + "dmerged")
        dga, dgp, dgt, dya, dyp, dyt, gs["pool_scale"][l] = _merge_bwd(
            s["proj"], gate_off, s["ya"], s["yp"], s["yt"], small["pool_scale"][l], dm, n + "merge")
        gw["w_a_out"][l] = _matmul(s["z"], dya, "tn", F32, n + "dw_a_out")
        dz = _matmul(dya, wts["w_a_out"][l], "nt", F32, n + "dz")
        gw["w_pool"][l] = _pool_mm(s["p"], dyp, "tn", F32, n + "dw_pool")
        dp = _pool_mm(dyp, wts["w_pool"][l], "nt", F32, n + "dp")
        gw["w_attn_out"][l] = _matmul(s["att"], dyt, "tn", F32, n + "dw_attn_out")
        datt = _matmul(dyt, wts["w_attn_out"][l], "nt", BF16, n + "datt")
        db, dc, dxa, du, gs["conv_w"][l] = _mixer_bwd(s["proj"], small["conv_w"][l], dz, dp, n + "mixer")
        dq, dk, dv, ds_sum, dsink = _attn_bwd(s["q"], s["k"], s["v"], bias, s["sink"], s["att_h"], s["lse"],
                                              _to_heads(datt, N_HEADS, False), n + "attn")
        gs["attn_sink"][l] = dsink.reshape(N_HEADS)
        ds_total = ds_sum if ds_total is None else ds_total + ds_sum
        dkv = [_from_heads(a[:, BLOCK:BLOCK + T].astype(BF16)) for a in (dk, dv)]
        dproj = jnp.concatenate([db, dc, dxa, du, _from_heads(dq)] + dkv + [dga, dgp, dgt], axis=1)
        gw["w_in"][l] = _matmul(s["h"], dproj, "tn", F32, n + "dw_in")
        dh = _matmul(dproj, wts["w_in"][l], "nt", F32, n + "dh")
        dx, dxb, gs["g_mix"][l] = _rms_bwd(s["x"], small["g_mix"][l], dh, dx1, n + "rms_mix")

    d_rel = _bias_grad(ds_total, bucket, "bias_grad")[:, :, 0].T
    gw = {k_: jnp.stack(v_) for k_, v_ in gw.items()}
    gs = {k_: jnp.stack(v_) for k_, v_ in gs.items()}
    gs["rel_bias"] = d_rel
    gs["g_final"] = dg_final
    return loss, dx, gw, gs


HBM_SPEC = pl.BlockSpec(memory_space=pltpu.HBM)


def _place():
    return lax.axis_index("x"), lax.axis_index("y"), lax.axis_index("c")


def _all_gather(v, name):
    def body(x_ref, out_ref, send_sems, recv_sems, local_sem):
        x, y, c = _place()
        me, sibling = (x, y, c), (x, y, 1 - c)
        chips = [(1 - x, y), (x, 1 - y), (1 - x, 1 - y)]

        def rows(px, py, pc):
            return out_ref.at[4 * px + 2 * py + pc]

        def copy(k, block, to, src=None):
            return pltpu.make_async_remote_copy(
                src_ref=rows(*block) if src is None else src, dst_ref=rows(*block),
                send_sem=send_sems.at[k], recv_sem=recv_sems.at[k], device_id=to, device_id_type=MESH)

        mine = pltpu.make_async_copy(x_ref, rows(*me), local_sem)
        mine.start()
        first = [copy(0, me, sibling, src=x_ref)]
        first += [copy(1 + j, me, (*chip, c), src=x_ref) for j, chip in enumerate(chips)]
        for cp in first:
            cp.start()
        passed = [copy(4 + j, (*chip, c), sibling) for j, chip in enumerate(chips)]
        for j, chip in enumerate(chips):
            copy(1 + j, (*chip, c), me).wait_recv()
            passed[j].start()
        copy(0, sibling, me).wait_recv()
        for j, chip in enumerate(chips):
            copy(4 + j, (*chip, 1 - c), me).wait_recv()
        for cp in first + passed:
            cp.wait_send()
        mine.wait()

    return pl.pallas_call(
        body, name=name, in_specs=[HBM_SPEC], out_specs=HBM_SPEC,
        out_shape=jax.ShapeDtypeStruct((N_DEV,) + v.shape, v.dtype),
        scratch_shapes=[pltpu.SemaphoreType.DMA((7,)), pltpu.SemaphoreType.DMA((7,)), pltpu.SemaphoreType.DMA],
    )(v)


def _swap_with_sibling(g, name):
    def body(g_ref, out_ref, send_sem, recv_sem):
        x, y, c = _place()
        cp = pltpu.make_async_remote_copy(src_ref=g_ref.at[1 - c], dst_ref=out_ref, send_sem=send_sem,
                                          recv_sem=recv_sem, device_id=(x, y, 1 - c), device_id_type=MESH)
        cp.start()
        cp.wait()

    return pl.pallas_call(
        body, name=name, in_specs=[HBM_SPEC], out_specs=HBM_SPEC,
        out_shape=jax.ShapeDtypeStruct(g.shape[1:], g.dtype),
        scratch_shapes=[pltpu.SemaphoreType.DMA, pltpu.SemaphoreType.DMA],
    )(g)


def _chip_exchange(p, name):
    def body(p_ref, out_ref, send_sems, recv_sems):
        x, y, c = _place()
        chips = [(1 - x, y), (x, 1 - y), (1 - x, 1 - y)]
        copies = [pltpu.make_async_remote_copy(
            src_ref=p_ref.at[2 * px + py], dst_ref=out_ref.at[k], send_sem=send_sems.at[k], recv_sem=recv_sems.at[k],
            device_id=(px, py, c), device_id_type=MESH) for k, (px, py) in enumerate(chips)]
        for cp in copies:
            cp.start()
        for cp in copies:
            cp.wait()

    return pl.pallas_call(
        body, name=name, in_specs=[HBM_SPEC], out_specs=HBM_SPEC,
        out_shape=jax.ShapeDtypeStruct((3,) + p.shape[1:], p.dtype),
        scratch_shapes=[pltpu.SemaphoreType.DMA((3,)), pltpu.SemaphoreType.DMA((3,))],
    )(p)


def _sum_parts(parts, out_dtype, name):
    R = parts[0].shape[0]
    rows = _tile(R, 512)
    n = len(parts)

    def body(*refs):
        acc = refs[0][...].astype(F32)
        for r in refs[1:n]:
            acc = acc + r[...].astype(F32)
        refs[n][...] = acc.astype(out_dtype)

    spec = pl.BlockSpec((rows, LANES), lambda i: (i, 0))
    return pl.pallas_call(
        body, name=name, grid=(R // rows,), in_specs=[spec] * n, out_specs=spec,
        out_shape=jax.ShapeDtypeStruct((R, LANES), out_dtype), compiler_params=_params("parallel"),
    )(*parts)


def _adamw(w, g, m, v, name):
    shape = w.shape
    cols = shape[-1]
    rows_total = w.size // cols
    w2, g2, m2, v2 = (a.reshape(rows_total, cols) for a in (w, g, m, v))
    rows = rows_total
    if rows_total > ROWS:
        rows = next(r for r in range(ROWS, 0, -8) if rows_total % r == 0)

    def body(w_ref, g_ref, m_ref, v_ref, d_ref, nm_ref, nv_ref):
        gv = g_ref[...]
        nm = ADAM_B1 * m_ref[...] + (1.0 - ADAM_B1) * gv
        nv = ADAM_B2 * v_ref[...] + (1.0 - ADAM_B2) * (gv * gv)
        m_hat = nm / (1.0 - ADAM_B1 ** ADAM_STEP)
        v_hat = nv / (1.0 - ADAM_B2 ** ADAM_STEP)
        d_ref[...] = -ADAM_LR * (m_hat / (jnp.sqrt(v_hat) + ADAM_EPS) + ADAM_WD * w_ref[...])
        nm_ref[...] = nm
        nv_ref[...] = nv

    spec = pl.BlockSpec((rows, cols), lambda i: (i, 0))
    out = jax.ShapeDtypeStruct((rows_total, cols), F32)
    d, nm, nv = pl.pallas_call(
        body, name=name, grid=(rows_total // rows,), in_specs=[spec] * 4, out_specs=[spec] * 3,
        out_shape=[out, out, out], compiler_params=_params("parallel"),
    )(w2, g2, m2, v2)
    return d.reshape(shape), nm.reshape(shape), nv.reshape(shape)


BIG = ("w_in", "w_a_out", "w_pool", "w_attn_out", "w_o", "w_gu", "w_down")


def _pack_shards(shards):
    L = shards["w_in"].shape[0]
    parts, spans, at = [], {}, 0
    for name in BIG:
        a = shards[name].reshape(L, -1, LANES)
        spans[name] = (at, at + a.shape[1])
        at += a.shape[1]
        parts.append(a)
    return jnp.concatenate(parts, axis=1), spans


def _unpack_full(gathered, spans, shard_shapes):
    out = {}
    for name in BIG:
        lo, hi = spans[name]
        sh = shard_shapes[name]
        a = gathered[:, :, lo:hi].reshape((N_DEV,) + sh)
        L = sh[0]
        if name in ("w_in", "w_gu"):
            out[name] = jnp.transpose(a, (1, 2, 0, 3)).reshape(L, sh[1], N_DEV * sh[2])
        elif name == "w_pool":
            out[name] = jnp.transpose(a, (1, 2, 0, 3, 4)).reshape(L, sh[1], N_DEV * sh[2], sh[3])
        else:
            out[name] = jnp.transpose(a, (1, 0, 2, 3)).reshape(L, N_DEV * sh[1], sh[2])
    return out


def _split_full(grads, shard_shapes):
    parts = []
    for name in BIG:
        sh = shard_shapes[name]
        L = sh[0]
        g = grads[name]
        if name in ("w_in", "w_gu"):
            a = jnp.transpose(g.reshape(L, sh[1], N_DEV, sh[2]), (2, 0, 1, 3))
        elif name == "w_pool":
            a = jnp.transpose(g.reshape(L, sh[1], N_DEV, sh[2], sh[3]), (2, 0, 1, 3, 4))
        else:
            a = jnp.transpose(g.reshape(L, N_DEV, sh[1], sh[2]), (1, 0, 2, 3))
        parts.append(a.reshape(N_DEV, L, -1, LANES))
    return jnp.concatenate(parts, axis=2)


SMALL_ROWS = 32


def _pack_small(gs, L, D):
    rows = [gs["pool_scale"].reshape(L, D), gs["g_mix"].reshape(L, D), gs["g_ffn"].reshape(L, D),
            gs["g_final"].reshape(1, D), gs["conv_w"][:, :3].reshape(3 * L, D),
            jnp.pad(gs["attn_sink"].reshape(1, -1), ((0, 0), (0, D - L * N_HEADS))),
            jnp.pad(gs["rel_bias"].reshape(1, -1), ((0, 0), (0, D - N_BUCKETS * N_HEADS)))]
    a = jnp.concatenate(rows, axis=0)
    return jnp.pad(a, ((0, SMALL_ROWS - a.shape[0]), (0, 0)))


def _unpack_small(a, L, D):
    g = {}
    g["pool_scale"] = a[0:L]
    g["g_mix"] = a[L:2 * L]
    g["g_ffn"] = a[2 * L:3 * L]
    g["g_final"] = a[3 * L]
    g["conv_w"] = a[3 * L + 1:6 * L + 1].reshape(L, 3, 1, D)
    g["attn_sink"] = a[6 * L + 1, :L * N_HEADS].reshape(L, N_HEADS)
    g["rel_bias"] = a[6 * L + 2, :N_BUCKETS * N_HEADS].reshape(N_BUCKETS, N_HEADS)
    return g


WEIGHTS = ("w_in", "conv_w", "w_a_out", "w_pool", "pool_scale", "w_attn_out", "attn_sink", "w_o", "g_mix", "g_ffn",
           "w_gu", "w_down", "rel_bias", "g_final")


def kernel(x, w_in, conv_w, w_a_out, w_pool, pool_scale, w_attn_out, attn_sink, w_o, g_mix, g_ffn, w_gu, w_down, rel_bias, g_final, loss_target, m_w_in, m_conv_w, m_w_a_out, m_w_pool, m_pool_scale, m_w_attn_out, m_attn_sink, m_w_o, m_g_mix, m_g_ffn, m_w_gu, m_w_down, m_rel_bias, m_g_final, v_w_in, v_conv_w, v_w_a_out, v_w_pool, v_pool_scale, v_w_attn_out, v_attn_sink, v_w_o, v_g_mix, v_g_ffn, v_w_gu, v_w_down, v_rel_bias, v_g_final):
    w = dict(w_in=w_in, conv_w=conv_w, w_a_out=w_a_out, w_pool=w_pool, pool_scale=pool_scale, w_attn_out=w_attn_out,
             attn_sink=attn_sink, w_o=w_o, g_mix=g_mix, g_ffn=g_ffn, w_gu=w_gu, w_down=w_down, rel_bias=rel_bias,
             g_final=g_final)
    m = dict(w_in=m_w_in, conv_w=m_conv_w, w_a_out=m_w_a_out, w_pool=m_w_pool, pool_scale=m_pool_scale,
             w_attn_out=m_w_attn_out, attn_sink=m_attn_sink, w_o=m_w_o, g_mix=m_g_mix, g_ffn=m_g_ffn, w_gu=m_w_gu,
             w_down=m_w_down, rel_bias=m_rel_bias, g_final=m_g_final)
    v = dict(w_in=v_w_in, conv_w=v_conv_w, w_a_out=v_w_a_out, w_pool=v_w_pool, pool_scale=v_pool_scale,
             w_attn_out=v_w_attn_out, attn_sink=v_attn_sink, w_o=v_w_o, g_mix=v_g_mix, g_ffn=v_g_ffn, w_gu=v_w_gu,
             w_down=v_w_down, rel_bias=v_rel_bias, g_final=v_g_final)
    T, D = x.shape[1], x.shape[2]
    L = w_in.shape[0]
    cx, cy, cc = _place()

    shard_shapes = {name: w[name].shape for name in BIG}
    packed, spans = _pack_shards({name: w[name].astype(BF16) for name in BIG})
    rows = packed.shape[1]
    gathered = _all_gather(packed.reshape(L * rows, LANES), "gather_weights").reshape(N_DEV, L, rows, LANES)
    wts = _unpack_full(gathered, spans, shard_shapes)
    cw = jnp.pad(conv_w.reshape(L * 3, -1), ((0, 16 - L * 3), (0, 0)))
    cw = _all_gather(cw, "gather_conv_w")
    cw = jnp.transpose(cw, (1, 0, 2)).reshape(16, -1)[:L * 3].reshape(L, 3, -1)
    small = dict(conv_w=jnp.pad(cw, ((0, 0), (0, 5), (0, 0))), pool_scale=pool_scale.reshape(L, 1, D),
                 g_mix=g_mix.reshape(L, 1, D), g_ffn=g_ffn.reshape(L, 1, D), attn_sink=attn_sink,
                 rel_bias=rel_bias, g_final=g_final.reshape(1, D))

    loss, dx, gw, gs = _local_step(x[0], loss_target[0], wts, small)
    loss = lax.psum(loss[0, 0], ("x", "y", "c"))

    split = _split_full(gw, shard_shapes).astype(BF16)
    split = jnp.transpose(split.reshape(4, 2, L * rows, LANES), (1, 0, 2, 3))
    from_sibling = _swap_with_sibling(split, "reduce_pair")
    mine = lax.dynamic_index_in_dim(split, cc, 0, keepdims=False)
    pair = _sum_parts([mine.reshape(4 * L * rows, LANES), from_sibling.reshape(4 * L * rows, LANES)], BF16, "pair_sum")
    pair = pair.reshape(4, L * rows, LANES)
    from_chips = _chip_exchange(pair, "reduce_chips")
    own = lax.dynamic_index_in_dim(pair, 2 * cx + cy, 0, keepdims=False)
    g_packed = _sum_parts([own, from_chips[0], from_chips[1], from_chips[2]], F32, "chip_sum").reshape(L, rows, LANES)
    grads = {name: g_packed[:, spans[name][0]:spans[name][1]].reshape(shard_shapes[name]) for name in BIG}

    small_all = _all_gather(_pack_small(gs, L, D), "gather_small")
    small_sum = _sum_parts([small_all[d] for d in range(N_DEV)], F32, "small_sum")
    gsm = _unpack_small(small_sum, L, D)
    W8 = D // N_DEV
    dev = 4 * cx + 2 * cy + cc
    gsm["conv_w"] = lax.dynamic_slice_in_dim(gsm["conv_w"], dev * W8, W8, axis=3)
    grads.update(gsm)

    deltas, new_m, new_v = {}, {}, {}
    for name in WEIGHTS:
        deltas[name], new_m[name], new_v[name] = _adamw(w[name], grads[name], m[name], v[name], "adamw_" + name)

    return (loss, dx[None], *[grads[n] for n in WEIGHTS], *[deltas[n] for n in WEIGHTS],
            *[new_m[n] for n in WEIGHTS], *[new_v[n] for n in WEIGHTS])
```

```python
import functools
import math

import jax
import jax.numpy as jnp
from jax import lax
from jax.experimental import pallas as pl
from jax.experimental.pallas import tpu as pltpu

F32 = jnp.float32
BF16 = jnp.bfloat16
MESH = pl.DeviceIdType.MESH

N_DEV = 8
N_HEADS = 16
N_KV_HEADS = 4
HEAD_DIM = 64
GROUP = N_HEADS // N_KV_HEADS
BLOCK = 128
WINDOW = 128
N_BUCKETS = 32
MAX_DISTANCE = 128
POOL_WINDOWS = (2, 4, 8, 16)
POOL_GROUPS = 4
HALO = 8
EPS = 1e-6
NEG_INF = -1e30

ADAM_LR = 0.001
ADAM_B1 = 0.9
ADAM_B2 = 0.999
ADAM_EPS = 1e-08
ADAM_WD = 0.01
ADAM_STEP = 10

LANES = 1024
VMEM_LIMIT_BYTES = 48 * 1024 * 1024


def _params(*sem):
    return pltpu.CompilerParams(dimension_semantics=sem, vmem_limit_bytes=VMEM_LIMIT_BYTES)


def _tile(n, cap):
    if n <= cap:
        return n
    for t in range(cap - cap % 128, 0, -128):
        if n % t == 0:
            return t
    raise ValueError(f"no tile for {n}")


_DIMS = {"nn": (((1,), (0,)), ((), ())), "nt": (((1,), (1,)), ((), ())), "tn": (((0,), (0,)), ((), ()))}


def _matmul(a, b, mode, out_dtype, name, res=None, tm_cap=1024, tn_cap=1024, tk_cap=1024):
    if mode == "tn":
        K, M = a.shape
    else:
        M, K = a.shape
    N = b.shape[0] if mode == "nt" else b.shape[1]
    tm, tn, tk = _tile(M, tm_cap), _tile(N, tn_cap), _tile(K, tk_cap)
    nk = K // tk
    a_spec = pl.BlockSpec((tk, tm), lambda i, j, k: (k, i)) if mode == "tn" else pl.BlockSpec((tm, tk), lambda i, j, k: (i, k))
    b_spec = pl.BlockSpec((tn, tk), lambda i, j, k: (j, k)) if mode == "nt" else pl.BlockSpec((tk, tn), lambda i, j, k: (k, j))
    o_spec = pl.BlockSpec((tm, tn), lambda i, j, k: (i, j))
    dims = _DIMS[mode]
    has_res = res is not None

    def body(*refs):
        a_ref, b_ref = refs[0], refs[1]
        res_ref = refs[2] if has_res else None
        o_ref, acc_ref = refs[-2], refs[-1]
        k = pl.program_id(2)
        part = lax.dot_general(a_ref[...], b_ref[...], dims, preferred_element_type=F32)

        @pl.when(k == 0)
        def _():
            acc_ref[...] = part

        @pl.when(k > 0)
        def _():
            acc_ref[...] += part

        @pl.when(k == nk - 1)
        def _():
            out = acc_ref[...]
            if has_res:
                out = out + res_ref[...]
            o_ref[...] = out.astype(out_dtype)

    in_specs = [a_spec, b_spec] + ([o_spec] if has_res else [])
    args = (a, b) + ((res,) if has_res else ())
    return pl.pallas_call(
        body, name=name, grid=(M // tm, N // tn, nk),
        in_specs=in_specs, out_specs=o_spec,
        out_shape=jax.ShapeDtypeStruct((M, N), out_dtype),
        scratch_shapes=[pltpu.VMEM((tm, tn), F32)],
        compiler_params=_params("parallel", "parallel", "arbitrary"),
    )(*args)


def _pool_mm(a, w, mode, out_dtype, name):
    T = a.shape[0]
    G = POOL_GROUPS
    cg = a.shape[1] // G
    tm = _tile(T, 1024)
    nt = T // tm
    dims = _DIMS[mode]
    if mode == "tn":
        def body(a_ref, d_ref, o_ref):
            part = lax.dot_general(a_ref[...], d_ref[...], dims, preferred_element_type=F32)

            @pl.when(pl.program_id(1) == 0)
            def _():
                o_ref[...] = part

            @pl.when(pl.program_id(1) > 0)
            def _():
                o_ref[...] += part

        return pl.pallas_call(
            body, name=name, grid=(G, nt),
            in_specs=[pl.BlockSpec((tm, cg), lambda g, i: (i, g)), pl.BlockSpec((tm, cg), lambda g, i: (i, g))],
            out_specs=pl.BlockSpec((None, cg, cg), lambda g, i: (g, 0, 0)),
            out_shape=jax.ShapeDtypeStruct((G, cg, cg), F32),
            compiler_params=_params("parallel", "arbitrary"),
        )(a, w)

    def body(a_ref, w_ref, o_ref):
        o_ref[...] = lax.dot_general(a_ref[...], w_ref[...], dims, preferred_element_type=F32).astype(out_dtype)

    return pl.pallas_call(
        body, name=name, grid=(G, nt),
        in_specs=[pl.BlockSpec((tm, cg), lambda g, i: (i, g)), pl.BlockSpec((None, cg, cg), lambda g, i: (g, 0, 0))],
        out_specs=pl.BlockSpec((tm, cg), lambda g, i: (i, g)),
        out_shape=jax.ShapeDtypeStruct((T, G * cg), out_dtype),
        compiler_params=_params("parallel", "parallel"),
    )(a, w)


ROWS = 256
HALO_BLOCK = 16


def _row_spec(d, col=0, rows=ROWS):
    return pl.BlockSpec((rows, d), lambda i, col=col: (i, col))


def _const_spec(shape):
    return pl.BlockSpec(shape, lambda *_: (0,) * len(shape))


def _rms_fwd(x, g, name):
    T, D = x.shape

    def body(x_ref, g_ref, h_ref):
        xv = x_ref[...]
        r = lax.rsqrt(jnp.mean(xv * xv, axis=-1, keepdims=True) + EPS)
        h_ref[...] = (xv * r * g_ref[...]).astype(BF16)

    return pl.pallas_call(
        body, name=name, grid=(T // ROWS,),
        in_specs=[_row_spec(D), _const_spec((1, D))], out_specs=_row_spec(D),
        out_shape=jax.ShapeDtypeStruct((T, D), BF16), compiler_params=_params("parallel"),
    )(x, g)


def _accumulate(ref, part):
    first = pl.program_id(0) == 0

    @pl.when(first)
    def _():
        ref[...] = part

    @pl.when(jnp.logical_not(first))
    def _():
        ref[...] += part


def _rms_bwd(x, g, dh, dres, name):
    T, D = x.shape

    def body(x_ref, g_ref, dh_ref, dres_ref, dx_ref, dxb_ref, dg_ref):
        xv = x_ref[...]
        r = lax.rsqrt(jnp.mean(xv * xv, axis=-1, keepdims=True) + EPS)
        xhat = xv * r
        dh_v = dh_ref[...]
        dxhat = dh_v * g_ref[...]
        dx = dres_ref[...] + r * (dxhat - xhat * jnp.mean(dxhat * xhat, axis=-1, keepdims=True))
        dx_ref[...] = dx
        dxb_ref[...] = dx.astype(BF16)
        _accumulate(dg_ref, jnp.sum(dh_v * xhat, axis=0, keepdims=True))

    return pl.pallas_call(
        body, name=name, grid=(T // ROWS,),
        in_specs=[_row_spec(D), _const_spec((1, D)), _row_spec(D), _row_spec(D)],
        out_specs=[_row_spec(D), _row_spec(D), _const_spec((1, D))],
        out_shape=[jax.ShapeDtypeStruct((T, D), F32), jax.ShapeDtypeStruct((T, D), BF16),
                   jax.ShapeDtypeStruct((1, D), F32)],
        compiler_params=_params("arbitrary"),
    )(x, g, dh, dres)


def _loss_head(x, g, target, name):
    T, D = x.shape

    def body(x_ref, g_ref, t_ref, loss_ref, dx_ref, dxb_ref, dg_ref):
        xv = x_ref[...]
        gv = g_ref[...]
        r = lax.rsqrt(jnp.mean(xv * xv, axis=-1, keepdims=True) + EPS)
        xhat = xv * r
        err = xhat * gv - t_ref[...]
        loss = 0.5 * jnp.sum(jnp.mean(err * err, axis=-1, keepdims=True), axis=0, keepdims=True)
        dy = err * (1.0 / D)
        dxhat = dy * gv
        dx = r * (dxhat - xhat * jnp.mean(dxhat * xhat, axis=-1, keepdims=True))
        dx_ref[...] = dx
        dxb_ref[...] = dx.astype(BF16)
        _accumulate(loss_ref, loss)
        _accumulate(dg_ref, jnp.sum(dy * xhat, axis=0, keepdims=True))

    return pl.pallas_call(
        body, name=name, grid=(T // ROWS,),
        in_specs=[_row_spec(D), _const_spec((1, D)), _row_spec(D)],
        out_specs=[_const_spec((1, 1)), _row_spec(D), _row_spec(D), _const_spec((1, D))],
        out_shape=[jax.ShapeDtypeStruct((1, 1), F32), jax.ShapeDtypeStruct((T, D), F32),
                   jax.ShapeDtypeStruct((T, D), BF16), jax.ShapeDtypeStruct((1, D), F32)],
        compiler_params=_params("arbitrary"),
    )(x, g, target)


def _halo_specs(d, col, n_blocks):
    per = ROWS // HALO_BLOCK
    last = n_blocks * per - 1
    prev = pl.BlockSpec((HALO_BLOCK, d), lambda i, col=col: (jnp.maximum(i * per - 1, 0), col))
    nxt = pl.BlockSpec((HALO_BLOCK, d), lambda i, col=col: (jnp.minimum((i + 1) * per, last), col))
    return prev, nxt


def _with_halo(prev, cur, nxt, n_blocks):
    i = pl.program_id(0)
    prev = jnp.where(i > 0, prev[HALO_BLOCK - HALO:], 0.0)
    nxt = jnp.where(i < n_blocks - 1, nxt[:HALO], 0.0)
    return jnp.concatenate([prev, cur, nxt], axis=0)


def _f32(ref):
    return ref[...].astype(F32)


def _shift(ext, k):
    n = ext.shape[0]
    v = ext if k == 0 else pltpu.roll(ext, (-k) % n, 0)
    return v[HALO:HALO + ROWS]


def _shift_full(ext, k):
    n = ext.shape[0]
    return pltpu.roll(ext, (-k) % n, 0)


def _pool_counts(T):
    n = ROWS + 2 * HALO
    t = pl.program_id(0) * ROWS - HALO + lax.broadcasted_iota(jnp.int32, (n, 1), 0)
    out = []
    for w in POOL_WINDOWS:
        lo = jnp.maximum(t - w // 2, 0)
        hi = jnp.minimum(t + (w - 1 - w // 2), T - 1)
        out.append(jnp.maximum(hi - lo + 1, 1).astype(F32))
    return out


def _window_sums(e, sign):
    s2 = e + _shift_full(e, -sign)
    s4 = _shift_full(s2, -1) + _shift_full(s2, 1)
    s8 = _shift_full(s4, -2) + _shift_full(s4, 2)
    s16 = _shift_full(s8, -4) + _shift_full(s8, 4)
    return s2, s4, s8, s16


def _mixer_fwd(proj, conv_w, name):
    T = proj.shape[0]
    W = conv_w.shape[1]
    nb = T // ROWS
    cg = W // POOL_GROUPS

    def body(b_ref, c_ref, x_ref, u_ref, cp_ref, cn_ref, xp_ref, xn_ref, up_ref, un_ref, w_ref, z_ref, p_ref):
        uc = _with_halo(_f32(cp_ref) * _f32(xp_ref), _f32(c_ref) * _f32(x_ref), _f32(cn_ref) * _f32(xn_ref), nb)
        w0, w1, w2 = w_ref[0:1, :], w_ref[1:2, :], w_ref[2:3, :]
        y = w0 * _shift(uc, -1) + w1 * _shift(uc, 0) + w2 * _shift(uc, 1)
        z_ref[...] = (_f32(b_ref) * y).astype(BF16)
        e = _with_halo(_f32(up_ref), _f32(u_ref), _f32(un_ref), nb)
        counts = _pool_counts(T)
        for gi in range(POOL_GROUPS):
            eg = e[:, gi * cg:(gi + 1) * cg]
            s = _window_sums(eg, 1)[gi]
            p = s[HALO:HALO + ROWS] / counts[gi][HALO:HALO + ROWS] - eg[HALO:HALO + ROWS]
            p_ref[:, gi * cg:(gi + 1) * cg] = p.astype(BF16)

    halo = [s for col in (1, 2, 3) for s in _halo_specs(W, col, nb)]
    return pl.pallas_call(
        body, name=name, grid=(nb,),
        in_specs=[_row_spec(W, 0), _row_spec(W, 1), _row_spec(W, 2), _row_spec(W, 3)] + halo + [_const_spec((8, W))],
        out_specs=[_row_spec(W), _row_spec(W)],
        out_shape=[jax.ShapeDtypeStruct((T, W), BF16), jax.ShapeDtypeStruct((T, W), BF16)],
        compiler_params=_params("parallel"),
    )(proj, proj, proj, proj, proj, proj, proj, proj, proj, proj, conv_w)


def _mixer_bwd(proj, conv_w, dz, dp, name):
    T = proj.shape[0]
    W = conv_w.shape[1]
    nb = T // ROWS
    cg = W // POOL_GROUPS

    def body(b_ref, c_ref, x_ref, dz_ref, dp_ref,
             bp_ref, bn_ref, cp_ref, cn_ref, xp_ref, xn_ref, dzp_ref, dzn_ref, dpp_ref, dpn_ref, w_ref,
             o_ref, dw_ref):
        cv, xv, dzv = _f32(c_ref), _f32(x_ref), _f32(dz_ref)
        uc = _with_halo(_f32(cp_ref) * _f32(xp_ref), cv * xv, _f32(cn_ref) * _f32(xn_ref), nb)
        dy = _with_halo(_f32(dzp_ref) * _f32(bp_ref), dzv * _f32(b_ref), _f32(dzn_ref) * _f32(bn_ref), nb)
        w0, w1, w2 = w_ref[0:1, :], w_ref[1:2, :], w_ref[2:3, :]
        um, u0, up = _shift(uc, -1), _shift(uc, 0), _shift(uc, 1)
        o_ref[:, 0:W] = (dzv * (w0 * um + w1 * u0 + w2 * up)).astype(BF16)
        dy0 = _shift(dy, 0)
        duc = w0 * _shift(dy, 1) + w1 * dy0 + w2 * _shift(dy, -1)
        o_ref[:, W:2 * W] = (duc * xv).astype(BF16)
        o_ref[:, 2 * W:3 * W] = (duc * cv).astype(BF16)
        row = lax.broadcasted_iota(jnp.int32, (8, W), 0)
        dw = jnp.where(row == 0, jnp.sum(dy0 * um, axis=0, keepdims=True),
                       jnp.where(row == 1, jnp.sum(dy0 * u0, axis=0, keepdims=True),
                                 jnp.where(row == 2, jnp.sum(dy0 * up, axis=0, keepdims=True), 0.0)))
        _accumulate(dw_ref, dw)
        d = _with_halo(_f32(dpp_ref), _f32(dp_ref), _f32(dpn_ref), nb)
        counts = _pool_counts(T)
        for gi in range(POOL_GROUPS):
            dg = d[:, gi * cg:(gi + 1) * cg]
            s = _window_sums(dg / counts[gi], -1)[gi]
            o_ref[:, 3 * W + gi * cg:3 * W + (gi + 1) * cg] = (s[HALO:HALO + ROWS] - dg[HALO:HALO + ROWS]).astype(BF16)

    def halo(col):
        return list(_halo_specs(W, col, nb))

    return pl.pallas_call(
        body, name=name, grid=(nb,),
        in_specs=[_row_spec(W, 0), _row_spec(W, 1), _row_spec(W, 2), _row_spec(W), _row_spec(W)]
        + halo(0) + halo(1) + halo(2) + halo(0) + halo(0) + [_const_spec((8, W))],
        out_specs=[_row_spec(4 * W), _const_spec((8, W))],
        out_shape=[jax.ShapeDtypeStruct((T, 4 * W), BF16), jax.ShapeDtypeStruct((8, W), F32)],
        compiler_params=_params("arbitrary"),
    )(proj, proj, proj, dz, dp, proj, proj, proj, proj, proj, proj, dz, dz, dp, dp, conv_w)


def _t5_bucket(rel):
    half = N_BUCKETS // 2
    max_exact = half // 2
    ret = jnp.where(rel > 0, half, 0)
    n = jnp.abs(rel)
    nf = jnp.maximum(n, 1).astype(jnp.float32)
    large = max_exact + (jnp.log(nf / max_exact) / math.log(MAX_DISTANCE / max_exact)
                         * (half - max_exact)).astype(jnp.int32)
    large = jnp.minimum(large, half - 1)
    return ret + jnp.where(n < max_exact, n, large)


def _bucket_table():
    qi = jnp.arange(BLOCK)[:, None]
    kj = jnp.arange(3 * BLOCK)[None, :]
    rel = kj - BLOCK - qi
    return jnp.where(jnp.abs(rel) <= WINDOW, _t5_bucket(rel), -1).astype(jnp.int32)


def _bias_table(rel_bias, bucket, name):
    def body(rb_ref, bucket_ref, o_ref):
        h = pl.program_id(0)
        bk = bucket_ref[...]
        acc = jnp.full(bk.shape, NEG_INF, F32)
        for b in range(N_BUCKETS):
            acc = jnp.where(bk == b, rb_ref[b, h], acc)
        o_ref[...] = acc

    return pl.pallas_call(
        body, name=name, grid=(N_HEADS,),
        in_specs=[pl.BlockSpec(memory_space=pltpu.SMEM), _const_spec((BLOCK, 3 * BLOCK))],
        out_specs=pl.BlockSpec((None, BLOCK, 3 * BLOCK), lambda h: (h, 0, 0)),
        out_shape=jax.ShapeDtypeStruct((N_HEADS, BLOCK, 3 * BLOCK), F32),
        compiler_params=_params("parallel"),
    )(rel_bias, bucket)


def _bias_grad(ds_sum, bucket, name):
    def body(ds_ref, bucket_ref, o_ref):
        bk = bucket_ref[...]
        ds = ds_ref[...]
        row = lax.broadcasted_iota(jnp.int32, (N_BUCKETS, 128), 0)
        acc = jnp.zeros((N_BUCKETS, 128), F32)
        for b in range(N_BUCKETS):
            s = jnp.sum(jnp.sum(jnp.where(bk == b, ds, 0.0), axis=1, keepdims=True), axis=0, keepdims=True)
            acc = jnp.where(row == b, s, acc)
        o_ref[...] = acc

    return pl.pallas_call(
        body, name=name, grid=(N_HEADS,),
        in_specs=[pl.BlockSpec((None, BLOCK, 3 * BLOCK), lambda h: (h, 0, 0)), _const_spec((BLOCK, 3 * BLOCK))],
        out_specs=pl.BlockSpec((None, N_BUCKETS, 128), lambda h: (h, 0, 0)),
        out_shape=jax.ShapeDtypeStruct((N_HEADS, N_BUCKETS, 128), F32),
        compiler_params=_params("parallel"),
    )(ds_sum, bucket)


def _scores(q_ref, k_ref, bias_ref, nb):
    i = pl.program_id(1)
    q = q_ref[...].reshape(GROUP * BLOCK, HEAD_DIM)
    k3 = k_ref[pl.ds(pl.multiple_of(i * BLOCK, BLOCK), 3 * BLOCK), :]
    s = lax.dot_general(q, k3, _DIMS["nt"], preferred_element_type=F32) * (HEAD_DIM ** -0.5)
    s = s + bias_ref[...].reshape(GROUP * BLOCK, 3 * BLOCK)
    kj = lax.broadcasted_iota(jnp.int32, (1, 3 * BLOCK), 1)
    outside = jnp.logical_or(jnp.logical_and(i == 0, kj < BLOCK), jnp.logical_and(i == nb - 1, kj >= 2 * BLOCK))
    return q, k3, jnp.where(outside, NEG_INF, s)


def _attn_specs(T):
    qspec = pl.BlockSpec((GROUP, BLOCK, HEAD_DIM), lambda j, i: (j, i, 0))
    kspec = pl.BlockSpec((None, T + 2 * BLOCK, HEAD_DIM), lambda j, i: (j, 0, 0))
    bspec = pl.BlockSpec((GROUP, BLOCK, 3 * BLOCK), lambda j, i: (j, 0, 0))
    cspec = pl.BlockSpec((GROUP, BLOCK, 1), lambda j, i: (j, i, 0))
    sspec = pl.BlockSpec((GROUP, BLOCK, 1), lambda j, i: (j, 0, 0))
    return qspec, kspec, bspec, cspec, sspec


def _attn_fwd(q, k, v, bias, sink, name):
    T = q.shape[1]
    nb = T // BLOCK
    qspec, kspec, bspec, cspec, sspec = _attn_specs(T)

    def body(q_ref, k_ref, v_ref, bias_ref, sink_ref, o_ref, lse_ref):
        i = pl.program_id(1)
        _, _, s = _scores(q_ref, k_ref, bias_ref, nb)
        sk = sink_ref[...].reshape(GROUP * BLOCK, 1)
        m = jnp.maximum(jnp.max(s, axis=-1, keepdims=True), sk)
        p = jnp.exp(s - m)
        denom = jnp.sum(p, axis=-1, keepdims=True) + jnp.exp(sk - m)
        p = (p / denom).astype(BF16)
        v3 = v_ref[pl.ds(pl.multiple_of(i * BLOCK, BLOCK), 3 * BLOCK), :]
        out = lax.dot_general(p, v3, _DIMS["nn"], preferred_element_type=F32)
        o_ref[...] = out.reshape(GROUP, BLOCK, HEAD_DIM).astype(BF16)
        lse_ref[...] = (m + jnp.log(denom)).reshape(GROUP, BLOCK, 1)

    return pl.pallas_call(
        body, name=name, grid=(N_KV_HEADS, nb),
        in_specs=[qspec, kspec, kspec, bspec, sspec], out_specs=[qspec, cspec],
        out_shape=[jax.ShapeDtypeStruct((N_HEADS, T, HEAD_DIM), BF16), jax.ShapeDtypeStruct((N_HEADS, T, 1), F32)],
        compiler_params=_params("parallel", "parallel"),
    )(q, k, v, bias, sink)


def _attn_bwd(q, k, v, bias, sink, out, lse, dout, name):
    T = q.shape[1]
    nb = T // BLOCK
    qspec, kspec, bspec, cspec, sspec = _attn_specs(T)
    scale = HEAD_DIM ** -0.5

    def body(q_ref, k_ref, v_ref, bias_ref, sink_ref, o_ref, lse_ref, do_ref, dq_ref, dk_ref, dv_ref, ds_ref, dsink_ref):
        i = pl.program_id(1)
        first = i == 0
        q4, k3, s = _scores(q_ref, k_ref, bias_ref, nb)
        lse = lse_ref[...].reshape(GROUP * BLOCK, 1)
        p = jnp.exp(s - lse)
        rows = pl.ds(pl.multiple_of(i * BLOCK, BLOCK), 3 * BLOCK)
        v3 = v_ref[rows, :]
        do4 = do_ref[...].reshape(GROUP * BLOCK, HEAD_DIM)
        dp = lax.dot_general(do4, v3, _DIMS["nt"], preferred_element_type=F32)
        delta = jnp.sum(do4.astype(F32) * o_ref[...].reshape(GROUP * BLOCK, HEAD_DIM).astype(F32), axis=-1, keepdims=True)
        ds = p * (dp - delta)
        dsb = ds.astype(BF16)
        dq = lax.dot_general(dsb, k3, _DIMS["nn"], preferred_element_type=F32) * scale
        dq_ref[...] = dq.reshape(GROUP, BLOCK, HEAD_DIM).astype(BF16)
        dk3 = lax.dot_general(dsb, q4, _DIMS["tn"], preferred_element_type=F32) * scale
        dv3 = lax.dot_general(p.astype(BF16), do4, _DIMS["tn"], preferred_element_type=F32)
        p_sink = jnp.exp(sink_ref[...].reshape(GROUP * BLOCK, 1) - lse)
        dsink = jnp.sum((-p_sink * delta).reshape(GROUP, BLOCK, 1), axis=1, keepdims=True)

        @pl.when(first)
        def _():
            dk_ref[...] = jnp.zeros(dk_ref.shape, F32)
            dv_ref[...] = jnp.zeros(dv_ref.shape, F32)
            ds_ref[...] = jnp.zeros(ds_ref.shape, F32)
            dsink_ref[...] = jnp.zeros(dsink_ref.shape, F32)

        dk_ref[rows, :] += dk3
        dv_ref[rows, :] += dv3
        ds_ref[...] += ds.reshape(GROUP, BLOCK, 3 * BLOCK)
        dsink_ref[...] += dsink

    kv_out = jax.ShapeDtypeStruct((N_KV_HEADS, T + 2 * BLOCK, HEAD_DIM), F32)
    return pl.pallas_call(
        body, name=name, grid=(N_KV_HEADS, nb),
        in_specs=[qspec, kspec, kspec, bspec, sspec, qspec, cspec, qspec],
        out_specs=[qspec, kspec, kspec, bspec, pl.BlockSpec((GROUP, 1, 1), lambda j, i: (j, 0, 0))],
        out_shape=[jax.ShapeDtypeStruct((N_HEADS, T, HEAD_DIM), BF16), kv_out, kv_out,
                   jax.ShapeDtypeStruct((N_HEADS, BLOCK, 3 * BLOCK), F32), jax.ShapeDtypeStruct((N_HEADS, 1, 1), F32)],
        compiler_params=_params("parallel", "arbitrary"),
    )(q, k, v, bias, sink, out, lse, dout)


def _to_heads(a, n_heads, pad):
    T = a.shape[0]
    a = jnp.transpose(a.astype(BF16).reshape(T, n_heads, HEAD_DIM), (1, 0, 2))
    return jnp.pad(a, ((0, 0), (BLOCK, BLOCK), (0, 0))) if pad else a


def _from_heads(a):
    return jnp.transpose(a, (1, 0, 2)).reshape(a.shape[1], a.shape[0] * HEAD_DIM)


GATE_COLS = 512


def _sigmoid(x):
    return 1.0 / (1.0 + jnp.exp(-x))


def _gate_specs(D, gate_off):
    nc = D // GATE_COLS
    base = gate_off // GATE_COLS
    return [pl.BlockSpec((ROWS, GATE_COLS), lambda i, c=base + g * nc + h: (i, c)) for g in range(3) for h in range(nc)]


def _merge_fwd(proj, gate_off, ya, yp, yt, scale, name):
    T, D = ya.shape
    nc = D // GATE_COLS

    def body(*refs):
        gates = refs[:3 * nc]
        ya_ref, yp_ref, yt_ref, s_ref, o_ref = refs[3 * nc:]
        for h in range(nc):
            cols = slice(h * GATE_COLS, (h + 1) * GATE_COLS)
            merged = (_sigmoid(_f32(gates[h])) * ya_ref[:, cols].astype(F32)
                      + _sigmoid(_f32(gates[nc + h])) * (yp_ref[:, cols].astype(F32) * s_ref[:, cols])
                      + _sigmoid(_f32(gates[2 * nc + h])) * yt_ref[:, cols].astype(F32))
            o_ref[:, cols] = merged.astype(BF16)

    yspec = _row_spec(D)
    return pl.pallas_call(
        body, name=name, grid=(T // ROWS,),
        in_specs=_gate_specs(D, gate_off) + [yspec, yspec, yspec, _const_spec((1, D))], out_specs=yspec,
        out_shape=jax.ShapeDtypeStruct((T, D), BF16), compiler_params=_params("parallel"),
    )(*([proj] * (3 * nc)), ya, yp, yt, scale)


def _merge_bwd(proj, gate_off, ya, yp, yt, scale, dm, name):
    T, D = ya.shape
    nc = D // GATE_COLS

    def body(*refs):
        gates = refs[:3 * nc]
        ya_ref, yp_ref, yt_ref, s_ref, dm_ref, dg_ref, dya_ref, dyp_ref, dyt_ref, ds_ref = refs[3 * nc:]
        parts = []
        for h in range(nc):
            cols = slice(h * GATE_COLS, (h + 1) * GATE_COLS)
            dm_v = dm_ref[:, cols].astype(F32)
            sa, sp, st = _sigmoid(_f32(gates[h])), _sigmoid(_f32(gates[nc + h])), _sigmoid(_f32(gates[2 * nc + h]))
            yp_v, s_v = yp_ref[:, cols].astype(F32), s_ref[:, cols]
            dg_ref[:, h * GATE_COLS:(h + 1) * GATE_COLS] = (dm_v * ya_ref[:, cols].astype(F32) * sa * (1.0 - sa)).astype(BF16)
            dg_ref[:, D + h * GATE_COLS:D + (h + 1) * GATE_COLS] = (dm_v * (yp_v * s_v) * sp * (1.0 - sp)).astype(BF16)
            dg_ref[:, 2 * D + h * GATE_COLS:2 * D + (h + 1) * GATE_COLS] = (
                dm_v * yt_ref[:, cols].astype(F32) * st * (1.0 - st)).astype(BF16)
            dya_ref[:, cols] = (dm_v * sa).astype(BF16)
            dyps = dm_v * sp
            dyp_ref[:, cols] = (dyps * s_v).astype(BF16)
            dyt_ref[:, cols] = (dm_v * st).astype(BF16)
            parts.append(jnp.sum(dyps * yp_v, axis=0, keepdims=True))
        _accumulate(ds_ref, jnp.concatenate(parts, axis=1))

    yspec = _row_spec(D)
    out = jax.ShapeDtypeStruct((T, D), BF16)
    return pl.pallas_call(
        body, name=name, grid=(T // ROWS,),
        in_specs=_gate_specs(D, gate_off) + [yspec, yspec, yspec, _const_spec((1, D)), yspec],
        out_specs=[_row_spec(3 * D), yspec, yspec, yspec, _const_spec((1, D))],
        out_shape=[jax.ShapeDtypeStruct((T, 3 * D), BF16), out, out, out, jax.ShapeDtypeStruct((1, D), F32)],
        compiler_params=_params("arbitrary"),
    )(*([proj] * (3 * nc)), ya, yp, yt, scale, dm)


def _swiglu_fwd(gu, name):
    T = gu.shape[0]
    F = gu.shape[1] // 2

    def body(gu_ref, o_ref):
        g = gu_ref[:, 0:F].astype(F32)
        o_ref[...] = (g * _sigmoid(g) * gu_ref[:, F:2 * F].astype(F32)).astype(BF16)

    return pl.pallas_call(
        body, name=name, grid=(T // ROWS,), in_specs=[_row_spec(2 * F)], out_specs=_row_spec(F),
        out_shape=jax.ShapeDtypeStruct((T, F), BF16), compiler_params=_params("parallel"),
    )(gu)


def _swiglu_bwd(gu, dact, name):
    T = gu.shape[0]
    F = gu.shape[1] // 2

    def body(gu_ref, d_ref, o_ref):
        g, d = gu_ref[:, 0:F].astype(F32), d_ref[...].astype(F32)
        sg = _sigmoid(g)
        o_ref[:, 0:F] = (d * gu_ref[:, F:2 * F].astype(F32) * sg * (1.0 + g * (1.0 - sg))).astype(BF16)
        o_ref[:, F:2 * F] = (d * g * sg).astype(BF16)

    return pl.pallas_call(
        body, name=name, grid=(T // ROWS,), in_specs=[_row_spec(2 * F), _row_spec(F)], out_specs=_row_spec(2 * F),
        out_shape=jax.ShapeDtypeStruct((T, 2 * F), BF16), compiler_params=_params("parallel"),
    )(gu, dact)


def _local_step(x, target, wts, small):
    T, D = x.shape
    depth = len(wts)
    gate_off = wts[0]["w_inT"].shape[0] - 3 * D
    q_off = 4 * D
    kv_w = N_KV_HEADS * HEAD_DIM
    ff2 = wts[0]["w_guT"].shape[0]
    bucket = _bucket_table()
    bias = _bias_table(small["rel_bias"], bucket, "bias_table")

    saved = []
    for l in range(depth):
        n = f"l{l}_"
        w = wts[l]
        h = _rms_fwd(x, small["g_mix"][l], n + "rms_mix")
        proj = _matmul(h, w["w_inT"], "nt", BF16, n + "proj")
        z, p = _mixer_fwd(proj, small["conv_w"][l], n + "mixer")
        q = _to_heads(proj[:, q_off:q_off + D], N_HEADS, False)
        k = _to_heads(proj[:, q_off + D:q_off + D + kv_w], N_KV_HEADS, True)
        v = _to_heads(proj[:, q_off + D + kv_w:q_off + D + 2 * kv_w], N_KV_HEADS, True)
        sink = jnp.broadcast_to(small["attn_sink"][l][:, None, None], (N_HEADS, BLOCK, 1))
        att_h, lse = _attn_fwd(q, k, v, bias, sink, n + "attn")
        att = _from_heads(att_h)
        ya = _matmul(z, w["w_a_out"], "nn", BF16, n + "ya")
        yp = _pool_mm(p, w["w_pool"], "nn", BF16, n + "yp")
        yt = _matmul(att, w["w_attn_out"], "nn", BF16, n + "yt")
        merged = _merge_fwd(proj, gate_off, ya, yp, yt, small["pool_scale"][l], n + "merge")
        x1 = _matmul(merged, w["w_o"], "nn", F32, n + "x1", res=x)
        h2 = _rms_fwd(x1, small["g_ffn"][l], n + "rms_ffn")
        gu = _matmul(h2, w["w_guT"], "nt", BF16, n + "gu")
        act = _swiglu_fwd(gu, n + "swiglu")
        x2 = _matmul(act, w["w_down"], "nn", F32, n + "x2", res=x1, tn_cap=512, tk_cap=ff2 // 2)
        saved.append(dict(x=x, h=h, proj=proj, z=z, p=p, q=q, k=k, v=v, sink=sink, att_h=att_h, lse=lse, att=att,
                          ya=ya, yp=yp, yt=yt, merged=merged, x1=x1, h2=h2, gu=gu, act=act))
        x = x2

    loss, dx, dxb, dg_final = _loss_head(x, small["g_final"], target, "loss_head")

    gw = [None] * depth
    gs = {k_: [None] * depth for k_ in ("conv_w", "pool_scale", "g_mix", "g_ffn", "attn_sink")}
    ds_total = None
    for l in reversed(range(depth)):
        n = f"l{l}_b_"
        s, w, g = saved[l], wts[l], {}
        g["w_down"] = _matmul(s["act"], dxb, "tn", BF16, n + "dw_down", tm_cap=ff2 // 2, tk_cap=512)
        dact = _matmul(dxb, w["w_down"], "nt", BF16, n + "dact", tm_cap=512, tn_cap=ff2 // 2)
        dgu = _swiglu_bwd(s["gu"], dact, n + "swiglu")
        g["w_guT"] = _matmul(dgu, s["h2"], "tn", BF16, n + "dw_gu", tm_cap=512)
        dh2 = _matmul(dgu, w["w_guT"], "nn", F32, n + "dh2", tn_cap=512, tk_cap=ff2 // 2)
        dx1, dx1b, gs["g_ffn"][l] = _rms_bwd(s["x1"], small["g_ffn"][l], dh2, dx, n + "rms_ffn")
        g["w_o"] = _matmul(s["merged"], dx1b, "tn", BF16, n + "dw_o")
        dm = _matmul(dx1b, w["w_o"], "nt", BF16, n + "dmerged")
        dgates, dya, dyp, dyt, gs["pool_scale"][l] = _merge_bwd(
            s["proj"], gate_off, s["ya"], s["yp"], s["yt"], small["pool_scale"][l], dm, n + "merge")
        g["w_a_out"] = _matmul(s["z"], dya, "tn", BF16, n + "dw_a_out")
        dz = _matmul(dya, w["w_a_out"], "nt", BF16, n + "dz")
        g["w_pool"] = _pool_mm(s["p"], dyp, "tn", F32, n + "dw_pool")
        dp = _pool_mm(dyp, w["w_pool"], "nt", BF16, n + "dp")
        g["w_attn_out"] = _matmul(s["att"], dyt, "tn", BF16, n + "dw_attn_out")
        datt = _matmul(dyt, w["w_attn_out"], "nt", BF16, n + "datt")
        dmix, gs["conv_w"][l] = _mixer_bwd(s["proj"], small["conv_w"][l], dz, dp, n + "mixer")
        dq, dk, dv, ds_sum, dsink = _attn_bwd(s["q"], s["k"], s["v"], bias, s["sink"], s["att_h"], s["lse"],
                                              _to_heads(datt, N_HEADS, False), n + "attn")
        gs["attn_sink"][l] = dsink.reshape(N_HEADS)
        ds_total = ds_sum if ds_total is None else ds_total + ds_sum
        dkv = [_from_heads(a[:, BLOCK:BLOCK + T].astype(BF16)) for a in (dk, dv)]
        dproj = jnp.concatenate([dmix, _from_heads(dq)] + dkv + [dgates], axis=1)
        g["w_inT"] = _matmul(dproj, s["h"], "tn", BF16, n + "dw_in", tm_cap=512)
        dh = _matmul(dproj, w["w_inT"], "nn", F32, n + "dh", tk_cap=2816)
        dx, dxb, gs["g_mix"][l] = _rms_bwd(s["x"], small["g_mix"][l], dh, dx1, n + "rms_mix")
        gw[l] = g

    d_rel = _bias_grad(ds_total, bucket, "bias_grad")[:, :, 0].T
    gs = {k_: jnp.stack(v_) for k_, v_ in gs.items()}
    gs["rel_bias"] = d_rel
    gs["g_final"] = dg_final
    return loss, dx, gw, gs


HBM_SPEC = pl.BlockSpec(memory_space=pltpu.HBM)


def _place():
    return lax.axis_index("x"), lax.axis_index("y"), lax.axis_index("c")


def _all_gather(v, name):
    def body(x_ref, out_ref, send_sems, recv_sems, local_sem):
        x, y, c = _place()
        me, sibling = (x, y, c), (x, y, 1 - c)
        chips = [(1 - x, y), (x, 1 - y), (1 - x, 1 - y)]

        def rows(px, py, pc):
            return out_ref.at[4 * px + 2 * py + pc]

        def copy(k, block, to, src=None):
            return pltpu.make_async_remote_copy(
                src_ref=rows(*block) if src is None else src, dst_ref=rows(*block),
                send_sem=send_sems.at[k], recv_sem=recv_sems.at[k], device_id=to, device_id_type=MESH)

        mine = pltpu.make_async_copy(x_ref, rows(*me), local_sem)
        mine.start()
        first = [copy(0, me, sibling, src=x_ref)]
        first += [copy(1 + j, me, (*chip, c), src=x_ref) for j, chip in enumerate(chips)]
        for cp in first:
            cp.start()
        passed = [copy(4 + j, (*chip, c), sibling) for j, chip in enumerate(chips)]
        for j, chip in enumerate(chips):
            copy(1 + j, (*chip, c), me).wait_recv()
            passed[j].start()
        copy(0, sibling, me).wait_recv()
        for j, chip in enumerate(chips):
            copy(4 + j, (*chip, 1 - c), me).wait_recv()
        for cp in first + passed:
            cp.wait_send()
        mine.wait()

    return pl.pallas_call(
        body, name=name, in_specs=[HBM_SPEC], out_specs=HBM_SPEC,
        out_shape=jax.ShapeDtypeStruct((N_DEV,) + v.shape, v.dtype),
        scratch_shapes=[pltpu.SemaphoreType.DMA((7,)), pltpu.SemaphoreType.DMA((7,)), pltpu.SemaphoreType.DMA],
    )(v)


def _all_gather_many(parts, name):
    n = len(parts)

    def body(*refs):
        ins, outs = refs[:n], refs[n:2 * n]
        send_sems, recv_sems, local_sems = refs[2 * n:]
        x, y, c = _place()
        me, sibling = (x, y, c), (x, y, 1 - c)
        chips = [(1 - x, y), (x, 1 - y), (1 - x, 1 - y)]

        def rows(t, px, py, pc):
            return outs[t].at[4 * px + 2 * py + pc]

        def copy(t, k, block, to, src=None):
            return pltpu.make_async_remote_copy(
                src_ref=rows(t, *block) if src is None else src, dst_ref=rows(t, *block),
                send_sem=send_sems.at[7 * t + k], recv_sem=recv_sems.at[7 * t + k], device_id=to, device_id_type=MESH)

        mine = [pltpu.make_async_copy(ins[t], rows(t, *me), local_sems.at[t]) for t in range(n)]
        sends = []
        for t in range(n):
            mine[t].start()
            sends.append(copy(t, 0, me, sibling, src=ins[t]))
            sends += [copy(t, 1 + j, me, (*chip, c), src=ins[t]) for j, chip in enumerate(chips)]
        for cp in sends:
            cp.start()
        for j, chip in enumerate(chips):
            for t in range(n):
                copy(t, 1 + j, (*chip, c), me).wait_recv()
                passed = copy(t, 4 + j, (*chip, c), sibling)
                passed.start()
                sends.append(passed)
        for t in range(n):
            copy(t, 0, sibling, me).wait_recv()
            for j, chip in enumerate(chips):
                copy(t, 4 + j, (*chip, 1 - c), me).wait_recv()
        for cp in sends:
            cp.wait_send()
        for cp in mine:
            cp.wait()

    return pl.pallas_call(
        body, name=name, in_specs=[HBM_SPEC] * n, out_specs=[HBM_SPEC] * n,
        out_shape=[jax.ShapeDtypeStruct((N_DEV,) + p.shape, p.dtype) for p in parts],
        scratch_shapes=[pltpu.SemaphoreType.DMA((7 * n,)), pltpu.SemaphoreType.DMA((7 * n,)),
                        pltpu.SemaphoreType.DMA((n,))],
    )(*parts)


def _swap_with_sibling(g, name):
    def body(g_ref, out_ref, send_sem, recv_sem):
        x, y, c = _place()
        cp = pltpu.make_async_remote_copy(src_ref=g_ref.at[1 - c], dst_ref=out_ref, send_sem=send_sem,
                                          recv_sem=recv_sem, device_id=(x, y, 1 - c), device_id_type=MESH)
        cp.start()
        cp.wait()

    return pl.pallas_call(
        body, name=name, in_specs=[HBM_SPEC], out_specs=HBM_SPEC,
        out_shape=jax.ShapeDtypeStruct(g.shape[1:], g.dtype),
        scratch_shapes=[pltpu.SemaphoreType.DMA, pltpu.SemaphoreType.DMA],
    )(g)


def _chip_exchange(p, name):
    def body(p_ref, out_ref, send_sems, recv_sems):
        x, y, c = _place()
        chips = [(1 - x, y), (x, 1 - y), (1 - x, 1 - y)]
        copies = [pltpu.make_async_remote_copy(
            src_ref=p_ref.at[2 * px + py], dst_ref=out_ref.at[k], send_sem=send_sems.at[k], recv_sem=recv_sems.at[k],
            device_id=(px, py, c), device_id_type=MESH) for k, (px, py) in enumerate(chips)]
        for cp in copies:
            cp.start()
        for cp in copies:
            cp.wait()

    return pl.pallas_call(
        body, name=name, in_specs=[HBM_SPEC], out_specs=HBM_SPEC,
        out_shape=jax.ShapeDtypeStruct((3,) + p.shape[1:], p.dtype),
        scratch_shapes=[pltpu.SemaphoreType.DMA((3,)), pltpu.SemaphoreType.DMA((3,))],
    )(p)


def _sum_parts(own, index, others, out_dtype, name):
    R = own.shape[1]
    rows = _tile(R, 512)
    k = others.shape[0]

    def body(idx_ref, own_ref, *refs):
        del idx_ref
        acc = own_ref[...].astype(F32)
        for r in refs[:k]:
            acc = acc + r[...].astype(F32)
        refs[k][...] = acc.astype(out_dtype)

    grid_spec = pltpu.PrefetchScalarGridSpec(
        num_scalar_prefetch=1, grid=(R // rows,),
        in_specs=[pl.BlockSpec((None, rows, LANES), lambda i, idx: (idx[0], i, 0))]
        + [pl.BlockSpec((None, rows, LANES), lambda i, idx, j=j: (j, i, 0)) for j in range(k)],
        out_specs=pl.BlockSpec((rows, LANES), lambda i, idx: (i, 0)))
    return pl.pallas_call(
        body, name=name, grid_spec=grid_spec,
        out_shape=jax.ShapeDtypeStruct((R, LANES), out_dtype), compiler_params=_params("parallel"),
    )(jnp.reshape(index, (1,)).astype(jnp.int32), own, *([others] * k))


def _adamw(w, g, m, v, name):
    shape = w.shape
    cols = shape[-1]
    rows_total = w.size // cols
    w2, g2, m2, v2 = (a.reshape(rows_total, cols) for a in (w, g, m, v))
    rows = rows_total
    if rows_total > ROWS:
        rows = next(r for r in range(ROWS, 0, -8) if rows_total % r == 0)

    def body(w_ref, g_ref, m_ref, v_ref, d_ref, nm_ref, nv_ref):
        gv = g_ref[...]
        nm = ADAM_B1 * m_ref[...] + (1.0 - ADAM_B1) * gv
        nv = ADAM_B2 * v_ref[...] + (1.0 - ADAM_B2) * (gv * gv)
        m_hat = nm / (1.0 - ADAM_B1 ** ADAM_STEP)
        v_hat = nv / (1.0 - ADAM_B2 ** ADAM_STEP)
        d_ref[...] = -ADAM_LR * (m_hat / (jnp.sqrt(v_hat) + ADAM_EPS) + ADAM_WD * w_ref[...])
        nm_ref[...] = nm
        nv_ref[...] = nv

    spec = pl.BlockSpec((rows, cols), lambda i: (i, 0))
    out = jax.ShapeDtypeStruct((rows_total, cols), F32)
    d, nm, nv = pl.pallas_call(
        body, name=name, grid=(rows_total // rows,), in_specs=[spec] * 4, out_specs=[spec] * 3,
        out_shape=[out, out, out], compiler_params=_params("parallel"),
    )(w2, g2, m2, v2)
    return d.reshape(shape), nm.reshape(shape), nv.reshape(shape)


BIG = ("w_in", "w_a_out", "w_pool", "w_attn_out", "w_o", "w_gu", "w_down")


LOCAL = dict(w_in="w_inT", w_a_out="w_a_out", w_pool="w_pool", w_attn_out="w_attn_out", w_o="w_o", w_gu="w_guT",
             w_down="w_down")


def _shard_rows(w, l):
    out = []
    for name in BIG:
        a = w[name][l]
        if name in ("w_in", "w_gu"):
            a = a.T
        elif name == "w_pool":
            a = a.reshape(-1, a.shape[-1])
        out.append(a.astype(BF16))
    return out


def _full_weights(gathered, w):
    out = {}
    for name, g in zip(BIG, gathered):
        if name == "w_pool":
            G, rg, cg = w[name].shape[1:]
            out[name] = jnp.transpose(g.reshape(N_DEV, G, rg, cg), (1, 0, 2, 3)).reshape(G, N_DEV * rg, cg)
        else:
            out[LOCAL[name]] = g.reshape(N_DEV * g.shape[1], g.shape[2])
    return out


def _split_grads(gw, w):
    layers, spans = [], {}
    for l, g in enumerate(gw):
        parts, at = [], 0
        for name in BIG:
            a = g[LOCAL[name]].astype(BF16)
            if name == "w_pool":
                G, rg, cg = w[name].shape[1:]
                a = jnp.transpose(a.reshape(G, N_DEV, rg, cg), (1, 0, 2, 3))
            a = jnp.transpose(a.reshape(4, 2, -1, LANES), (1, 0, 2, 3))
            spans[name] = (at, at + a.shape[2])
            at += a.shape[2]
            parts.append(a)
        layers.append(jnp.concatenate(parts, axis=2))
    return jnp.concatenate(layers, axis=2), spans, at


def _own_grads(g_packed, spans, rows, w):
    L = w["w_in"].shape[0]
    g3 = g_packed.reshape(L, rows, LANES)
    out = {}
    for name in BIG:
        a = g3[:, spans[name][0]:spans[name][1]]
        if name in ("w_in", "w_gu"):
            sh = w[name].shape
            a = jnp.swapaxes(a.reshape(L, sh[2], sh[1]), 1, 2)
        out[name] = a.reshape(w[name].shape)
    return out


SMALL_ROWS = 32


def _pack_small(gs, L, D):
    rows = [gs["pool_scale"].reshape(L, D), gs["g_mix"].reshape(L, D), gs["g_ffn"].reshape(L, D),
            gs["g_final"].reshape(1, D), gs["conv_w"][:, :3].reshape(3 * L, D),
            jnp.pad(gs["attn_sink"].reshape(1, -1), ((0, 0), (0, D - L * N_HEADS))),
            jnp.pad(gs["rel_bias"].reshape(1, -1), ((0, 0), (0, D - N_BUCKETS * N_HEADS)))]
    a = jnp.concatenate(rows, axis=0)
    return jnp.pad(a, ((0, SMALL_ROWS - a.shape[0]), (0, 0)))


def _unpack_small(a, L, D):
    g = {}
    g["pool_scale"] = a[0:L]
    g["g_mix"] = a[L:2 * L]
    g["g_ffn"] = a[2 * L:3 * L]
    g["g_final"] = a[3 * L]
    g["conv_w"] = a[3 * L + 1:6 * L + 1].reshape(L, 3, 1, D)
    g["attn_sink"] = a[6 * L + 1, :L * N_HEADS].reshape(L, N_HEADS)
    g["rel_bias"] = a[6 * L + 2, :N_BUCKETS * N_HEADS].reshape(N_BUCKETS, N_HEADS)
    return g


WEIGHTS = ("w_in", "conv_w", "w_a_out", "w_pool", "pool_scale", "w_attn_out", "attn_sink", "w_o", "g_mix", "g_ffn",
           "w_gu", "w_down", "rel_bias", "g_final")


def kernel(x, w_in, conv_w, w_a_out, w_pool, pool_scale, w_attn_out, attn_sink, w_o, g_mix, g_ffn, w_gu, w_down, rel_bias, g_final, loss_target, m_w_in, m_conv_w, m_w_a_out, m_w_pool, m_pool_scale, m_w_attn_out, m_attn_sink, m_w_o, m_g_mix, m_g_ffn, m_w_gu, m_w_down, m_rel_bias, m_g_final, v_w_in, v_conv_w, v_w_a_out, v_w_pool, v_pool_scale, v_w_attn_out, v_attn_sink, v_w_o, v_g_mix, v_g_ffn, v_w_gu, v_w_down, v_rel_bias, v_g_final):
    w = dict(w_in=w_in, conv_w=conv_w, w_a_out=w_a_out, w_pool=w_pool, pool_scale=pool_scale, w_attn_out=w_attn_out,
             attn_sink=attn_sink, w_o=w_o, g_mix=g_mix, g_ffn=g_ffn, w_gu=w_gu, w_down=w_down, rel_bias=rel_bias,
             g_final=g_final)
    m = dict(w_in=m_w_in, conv_w=m_conv_w, w_a_out=m_w_a_out, w_pool=m_w_pool, pool_scale=m_pool_scale,
             w_attn_out=m_w_attn_out, attn_sink=m_attn_sink, w_o=m_w_o, g_mix=m_g_mix, g_ffn=m_g_ffn, w_gu=m_w_gu,
             w_down=m_w_down, rel_bias=m_rel_bias, g_final=m_g_final)
    v = dict(w_in=v_w_in, conv_w=v_conv_w, w_a_out=v_w_a_out, w_pool=v_w_pool, pool_scale=v_pool_scale,
             w_attn_out=v_w_attn_out, attn_sink=v_attn_sink, w_o=v_w_o, g_mix=v_g_mix, g_ffn=v_g_ffn, w_gu=v_w_gu,
             w_down=v_w_down, rel_bias=v_rel_bias, g_final=v_g_final)
    T, D = x.shape[1], x.shape[2]
    L = w_in.shape[0]
    cx, cy, cc = _place()

    wts = [_full_weights(_all_gather_many(_shard_rows(w, l), f"gather_weights_l{l}"), w) for l in range(L)]
    cw = jnp.pad(conv_w.reshape(L * 3, -1), ((0, 16 - L * 3), (0, 0)))
    cw = _all_gather(cw, "gather_conv_w")
    cw = jnp.transpose(cw, (1, 0, 2)).reshape(16, -1)[:L * 3].reshape(L, 3, -1)
    small = dict(conv_w=jnp.pad(cw, ((0, 0), (0, 5), (0, 0))), pool_scale=pool_scale.reshape(L, 1, D),
                 g_mix=g_mix.reshape(L, 1, D), g_ffn=g_ffn.reshape(L, 1, D), attn_sink=attn_sink,
                 rel_bias=rel_bias, g_final=g_final.reshape(1, D))

    loss, dx, gw, gs = _local_step(x[0], loss_target[0], wts, small)
    loss = lax.psum(loss[0, 0], ("x", "y", "c"))

    split, spans, rows = _split_grads(gw, w)
    from_sibling = _swap_with_sibling(split, "reduce_pair")
    pair = _sum_parts(split.reshape(2, 4 * L * rows, LANES), cc, from_sibling.reshape(1, 4 * L * rows, LANES),
                      BF16, "pair_sum").reshape(4, L * rows, LANES)
    from_chips = _chip_exchange(pair, "reduce_chips")
    g_packed = _sum_parts(pair, 2 * cx + cy, from_chips, F32, "chip_sum")
    grads = _own_grads(g_packed, spans, rows, w)

    small_all = _all_gather(_pack_small(gs, L, D), "gather_small")
    small_sum = _sum_parts(small_all, jnp.int32(0), small_all[1:], F32, "small_sum")
    gsm = _unpack_small(small_sum, L, D)
    W8 = D // N_DEV
    dev = 4 * cx + 2 * cy + cc
    gsm["conv_w"] = lax.dynamic_slice_in_dim(gsm["conv_w"], dev * W8, W8, axis=3)
    grads.update(gsm)

    deltas, new_m, new_v = {}, {}, {}
    for name in WEIGHTS:
        deltas[name], new_m[name], new_v[name] = _adamw(w[name], grads[name], m[name], v[name], "adamw_" + name)

    return (loss, dx[None], *[grads[n] for n in WEIGHTS], *[deltas[n] for n in WEIGHTS],
            *[new_m[n] for n in WEIGHTS], *[new_v[n] for n in WEIGHTS])
```

```python
import functools
import math

import jax
import jax.numpy as jnp
from jax import lax
from jax.experimental import pallas as pl
from jax.experimental.pallas import tpu as pltpu

F32 = jnp.float32
BF16 = jnp.bfloat16
MESH = pl.DeviceIdType.MESH

N_DEV = 8
N_HEADS = 16
N_KV_HEADS = 4
HEAD_DIM = 64
GROUP = N_HEADS // N_KV_HEADS
BLOCK = 128
WINDOW = 128
N_BUCKETS = 32
MAX_DISTANCE = 128
POOL_WINDOWS = (2, 4, 8, 16)
POOL_GROUPS = 4
HALO = 8
EPS = 1e-6
NEG_INF = -1e30

ADAM_LR = 0.001
ADAM_B1 = 0.9
ADAM_B2 = 0.999
ADAM_EPS = 1e-08
ADAM_WD = 0.01
ADAM_STEP = 10

LANES = 1024
VMEM_LIMIT_BYTES = 48 * 1024 * 1024


def _params(*sem):
    return pltpu.CompilerParams(dimension_semantics=sem, vmem_limit_bytes=VMEM_LIMIT_BYTES)


def _tile(n, cap):
    if n <= cap:
        return n
    for t in range(cap - cap % 128, 0, -128):
        if n % t == 0:
            return t
    raise ValueError(f"no tile for {n}")


_DIMS = {"nn": (((1,), (0,)), ((), ())), "nt": (((1,), (1,)), ((), ())), "tn": (((0,), (0,)), ((), ()))}


def _matmul(a, b, mode, out_dtype, name, res=None, tm_cap=1024, tn_cap=1024, tk_cap=1024):
    if mode == "tn":
        K, M = a.shape
    else:
        M, K = a.shape
    N = b.shape[0] if mode == "nt" else b.shape[1]
    tm, tn, tk = _tile(M, tm_cap), _tile(N, tn_cap), _tile(K, tk_cap)
    nk = K // tk
    a_spec = pl.BlockSpec((tk, tm), lambda i, j, k: (k, i)) if mode == "tn" else pl.BlockSpec((tm, tk), lambda i, j, k: (i, k))
    b_spec = pl.BlockSpec((tn, tk), lambda i, j, k: (j, k)) if mode == "nt" else pl.BlockSpec((tk, tn), lambda i, j, k: (k, j))
    o_spec = pl.BlockSpec((tm, tn), lambda i, j, k: (i, j))
    dims = _DIMS[mode]
    has_res = res is not None

    def body(*refs):
        a_ref, b_ref = refs[0], refs[1]
        res_ref = refs[2] if has_res else None
        o_ref, acc_ref = refs[-2], refs[-1]
        k = pl.program_id(2)
        part = lax.dot_general(a_ref[...], b_ref[...], dims, preferred_element_type=F32)

        @pl.when(k == 0)
        def _():
            acc_ref[...] = part

        @pl.when(k > 0)
        def _():
            acc_ref[...] += part

        @pl.when(k == nk - 1)
        def _():
            out = acc_ref[...]
            if has_res:
                out = out + res_ref[...]
            o_ref[...] = out.astype(out_dtype)

    in_specs = [a_spec, b_spec] + ([o_spec] if has_res else [])
    args = (a, b) + ((res,) if has_res else ())
    return pl.pallas_call(
        body, name=name, grid=(M // tm, N // tn, nk),
        in_specs=in_specs, out_specs=o_spec,
        out_shape=jax.ShapeDtypeStruct((M, N), out_dtype),
        scratch_shapes=[pltpu.VMEM((tm, tn), F32)],
        compiler_params=_params("parallel", "parallel", "arbitrary"),
    )(*args)


def _pool_mm(a, w, mode, out_dtype, name):
    T = a.shape[0]
    G = POOL_GROUPS
    cg = a.shape[1] // G
    tm = _tile(T, 1024)
    nt = T // tm
    dims = _DIMS[mode]
    if mode == "tn":
        def body(a_ref, d_ref, o_ref):
            part = lax.dot_general(a_ref[...], d_ref[...], dims, preferred_element_type=F32)

            @pl.when(pl.program_id(1) == 0)
            def _():
                o_ref[...] = part

            @pl.when(pl.program_id(1) > 0)
            def _():
                o_ref[...] += part

        return pl.pallas_call(
            body, name=name, grid=(G, nt),
            in_specs=[pl.BlockSpec((tm, cg), lambda g, i: (i, g)), pl.BlockSpec((tm, cg), lambda g, i: (i, g))],
            out_specs=pl.BlockSpec((None, cg, cg), lambda g, i: (g, 0, 0)),
            out_shape=jax.ShapeDtypeStruct((G, cg, cg), F32),
            compiler_params=_params("parallel", "arbitrary"),
        )(a, w)

    def body(a_ref, w_ref, o_ref):
        o_ref[...] = lax.dot_general(a_ref[...], w_ref[...], dims, preferred_element_type=F32).astype(out_dtype)

    return pl.pallas_call(
        body, name=name, grid=(G, nt),
        in_specs=[pl.BlockSpec((tm, cg), lambda g, i: (i, g)), pl.BlockSpec((None, cg, cg), lambda g, i: (g, 0, 0))],
        out_specs=pl.BlockSpec((tm, cg), lambda g, i: (i, g)),
        out_shape=jax.ShapeDtypeStruct((T, G * cg), out_dtype),
        compiler_params=_params("parallel", "parallel"),
    )(a, w)


ROWS = 256
HALO_BLOCK = 16


def _row_spec(d, col=0, rows=ROWS):
    return pl.BlockSpec((rows, d), lambda i, col=col: (i, col))


def _const_spec(shape):
    return pl.BlockSpec(shape, lambda *_: (0,) * len(shape))


def _rms_fwd(x, g, name):
    T, D = x.shape

    def body(x_ref, g_ref, h_ref):
        xv = x_ref[...]
        r = lax.rsqrt(jnp.mean(xv * xv, axis=-1, keepdims=True) + EPS)
        h_ref[...] = (xv * r * g_ref[...]).astype(BF16)

    return pl.pallas_call(
        body, name=name, grid=(T // ROWS,),
        in_specs=[_row_spec(D), _const_spec((1, D))], out_specs=_row_spec(D),
        out_shape=jax.ShapeDtypeStruct((T, D), BF16), compiler_params=_params("parallel"),
    )(x, g)


def _accumulate(ref, part):
    first = pl.program_id(0) == 0

    @pl.when(first)
    def _():
        ref[...] = part

    @pl.when(jnp.logical_not(first))
    def _():
        ref[...] += part


def _rms_bwd(x, g, dh, dres, name):
    T, D = x.shape

    def body(x_ref, g_ref, dh_ref, dres_ref, dx_ref, dxb_ref, dg_ref):
        xv = x_ref[...]
        r = lax.rsqrt(jnp.mean(xv * xv, axis=-1, keepdims=True) + EPS)
        xhat = xv * r
        dh_v = dh_ref[...]
        dxhat = dh_v * g_ref[...]
        dx = dres_ref[...] + r * (dxhat - xhat * jnp.mean(dxhat * xhat, axis=-1, keepdims=True))
        dx_ref[...] = dx
        dxb_ref[...] = dx.astype(BF16)
        _accumulate(dg_ref, jnp.sum(dh_v * xhat, axis=0, keepdims=True))

    return pl.pallas_call(
        body, name=name, grid=(T // ROWS,),
        in_specs=[_row_spec(D), _const_spec((1, D)), _row_spec(D), _row_spec(D)],
        out_specs=[_row_spec(D), _row_spec(D), _const_spec((1, D))],
        out_shape=[jax.ShapeDtypeStruct((T, D), F32), jax.ShapeDtypeStruct((T, D), BF16),
                   jax.ShapeDtypeStruct((1, D), F32)],
        compiler_params=_params("arbitrary"),
    )(x, g, dh, dres)


def _loss_head(x, g, target, name):
    T, D = x.shape

    def body(x_ref, g_ref, t_ref, loss_ref, dx_ref, dxb_ref, dg_ref):
        xv = x_ref[...]
        gv = g_ref[...]
        r = lax.rsqrt(jnp.mean(xv * xv, axis=-1, keepdims=True) + EPS)
        xhat = xv * r
        err = xhat * gv - t_ref[...]
        loss = 0.5 * jnp.sum(jnp.mean(err * err, axis=-1, keepdims=True), axis=0, keepdims=True)
        dy = err * (1.0 / D)
        dxhat = dy * gv
        dx = r * (dxhat - xhat * jnp.mean(dxhat * xhat, axis=-1, keepdims=True))
        dx_ref[...] = dx
        dxb_ref[...] = dx.astype(BF16)
        _accumulate(loss_ref, loss)
        _accumulate(dg_ref, jnp.sum(dy * xhat, axis=0, keepdims=True))

    return pl.pallas_call(
        body, name=name, grid=(T // ROWS,),
        in_specs=[_row_spec(D), _const_spec((1, D)), _row_spec(D)],
        out_specs=[_const_spec((1, 1)), _row_spec(D), _row_spec(D), _const_spec((1, D))],
        out_shape=[jax.ShapeDtypeStruct((1, 1), F32), jax.ShapeDtypeStruct((T, D), F32),
                   jax.ShapeDtypeStruct((T, D), BF16), jax.ShapeDtypeStruct((1, D), F32)],
        compiler_params=_params("arbitrary"),
    )(x, g, target)


def _halo_specs(d, col, n_blocks):
    per = ROWS // HALO_BLOCK
    last = n_blocks * per - 1
    prev = pl.BlockSpec((HALO_BLOCK, d), lambda i, col=col: (jnp.maximum(i * per - 1, 0), col))
    nxt = pl.BlockSpec((HALO_BLOCK, d), lambda i, col=col: (jnp.minimum((i + 1) * per, last), col))
    return prev, nxt


def _with_halo(prev, cur, nxt, n_blocks):
    i = pl.program_id(0)
    prev = jnp.where(i > 0, prev[HALO_BLOCK - HALO:], 0.0)
    nxt = jnp.where(i < n_blocks - 1, nxt[:HALO], 0.0)
    return jnp.concatenate([prev, cur, nxt], axis=0)


def _f32(ref):
    return ref[...].astype(F32)


def _shift(ext, k):
    n = ext.shape[0]
    v = ext if k == 0 else pltpu.roll(ext, (-k) % n, 0)
    return v[HALO:HALO + ROWS]


def _shift_full(ext, k):
    n = ext.shape[0]
    return pltpu.roll(ext, (-k) % n, 0)


def _pool_counts(T):
    n = ROWS + 2 * HALO
    t = pl.program_id(0) * ROWS - HALO + lax.broadcasted_iota(jnp.int32, (n, 1), 0)
    out = []
    for w in POOL_WINDOWS:
        lo = jnp.maximum(t - w // 2, 0)
        hi = jnp.minimum(t + (w - 1 - w // 2), T - 1)
        out.append(jnp.maximum(hi - lo + 1, 1).astype(F32))
    return out


def _window_sums(e, sign):
    s2 = e + _shift_full(e, -sign)
    s4 = _shift_full(s2, -1) + _shift_full(s2, 1)
    s8 = _shift_full(s4, -2) + _shift_full(s4, 2)
    s16 = _shift_full(s8, -4) + _shift_full(s8, 4)
    return s2, s4, s8, s16


def _mixer_fwd(proj, conv_w, name):
    T = proj.shape[0]
    W = conv_w.shape[1]
    nb = T // ROWS
    cg = W // POOL_GROUPS

    def body(b_ref, c_ref, x_ref, u_ref, cp_ref, cn_ref, xp_ref, xn_ref, up_ref, un_ref, w_ref, z_ref, p_ref):
        uc = _with_halo(_f32(cp_ref) * _f32(xp_ref), _f32(c_ref) * _f32(x_ref), _f32(cn_ref) * _f32(xn_ref), nb)
        w0, w1, w2 = w_ref[0:1, :], w_ref[1:2, :], w_ref[2:3, :]
        y = w0 * _shift(uc, -1) + w1 * _shift(uc, 0) + w2 * _shift(uc, 1)
        z_ref[...] = (_f32(b_ref) * y).astype(BF16)
        e = _with_halo(_f32(up_ref), _f32(u_ref), _f32(un_ref), nb)
        counts = _pool_counts(T)
        for gi in range(POOL_GROUPS):
            eg = e[:, gi * cg:(gi + 1) * cg]
            s = _window_sums(eg, 1)[gi]
            p = s[HALO:HALO + ROWS] / counts[gi][HALO:HALO + ROWS] - eg[HALO:HALO + ROWS]
            p_ref[:, gi * cg:(gi + 1) * cg] = p.astype(BF16)

    halo = [s for col in (1, 2, 3) for s in _halo_specs(W, col, nb)]
    return pl.pallas_call(
        body, name=name, grid=(nb,),
        in_specs=[_row_spec(W, 0), _row_spec(W, 1), _row_spec(W, 2), _row_spec(W, 3)] + halo + [_const_spec((8, W))],
        out_specs=[_row_spec(W), _row_spec(W)],
        out_shape=[jax.ShapeDtypeStruct((T, W), BF16), jax.ShapeDtypeStruct((T, W), BF16)],
        compiler_params=_params("parallel"),
    )(proj, proj, proj, proj, proj, proj, proj, proj, proj, proj, conv_w)


def _mixer_bwd(proj, conv_w, dz, dp, name):
    T = proj.shape[0]
    W = conv_w.shape[1]
    nb = T // ROWS
    cg = W // POOL_GROUPS

    def body(b_ref, c_ref, x_ref, dz_ref, dp_ref,
             bp_ref, bn_ref, cp_ref, cn_ref, xp_ref, xn_ref, dzp_ref, dzn_ref, dpp_ref, dpn_ref, w_ref,
             o_ref, dw_ref):
        cv, xv, dzv = _f32(c_ref), _f32(x_ref), _f32(dz_ref)
        uc = _with_halo(_f32(cp_ref) * _f32(xp_ref), cv * xv, _f32(cn_ref) * _f32(xn_ref), nb)
        dy = _with_halo(_f32(dzp_ref) * _f32(bp_ref), dzv * _f32(b_ref), _f32(dzn_ref) * _f32(bn_ref), nb)
        w0, w1, w2 = w_ref[0:1, :], w_ref[1:2, :], w_ref[2:3, :]
        um, u0, up = _shift(uc, -1), _shift(uc, 0), _shift(uc, 1)
        o_ref[:, 0:W] = (dzv * (w0 * um + w1 * u0 + w2 * up)).astype(BF16)
        dy0 = _shift(dy, 0)
        duc = w0 * _shift(dy, 1) + w1 * dy0 + w2 * _shift(dy, -1)
        o_ref[:, W:2 * W] = (duc * xv).astype(BF16)
        o_ref[:, 2 * W:3 * W] = (duc * cv).astype(BF16)
        row = lax.broadcasted_iota(jnp.int32, (8, W), 0)
        dw = jnp.where(row == 0, jnp.sum(dy0 * um, axis=0, keepdims=True),
                       jnp.where(row == 1, jnp.sum(dy0 * u0, axis=0, keepdims=True),
                                 jnp.where(row == 2, jnp.sum(dy0 * up, axis=0, keepdims=True), 0.0)))
        _accumulate(dw_ref, dw)
        d = _with_halo(_f32(dpp_ref), _f32(dp_ref), _f32(dpn_ref), nb)
        counts = _pool_counts(T)
        for gi in range(POOL_GROUPS):
            dg = d[:, gi * cg:(gi + 1) * cg]
            s = _window_sums(dg / counts[gi], -1)[gi]
            o_ref[:, 3 * W + gi * cg:3 * W + (gi + 1) * cg] = (s[HALO:HALO + ROWS] - dg[HALO:HALO + ROWS]).astype(BF16)

    def halo(col):
        return list(_halo_specs(W, col, nb))

    return pl.pallas_call(
        body, name=name, grid=(nb,),
        in_specs=[_row_spec(W, 0), _row_spec(W, 1), _row_spec(W, 2), _row_spec(W), _row_spec(W)]
        + halo(0) + halo(1) + halo(2) + halo(0) + halo(0) + [_const_spec((8, W))],
        out_specs=[_row_spec(4 * W), _const_spec((8, W))],
        out_shape=[jax.ShapeDtypeStruct((T, 4 * W), BF16), jax.ShapeDtypeStruct((8, W), F32)],
        compiler_params=_params("arbitrary"),
    )(proj, proj, proj, dz, dp, proj, proj, proj, proj, proj, proj, dz, dz, dp, dp, conv_w)


def _t5_bucket(rel):
    half = N_BUCKETS // 2
    max_exact = half // 2
    ret = jnp.where(rel > 0, half, 0)
    n = jnp.abs(rel)
    nf = jnp.maximum(n, 1).astype(jnp.float32)
    large = max_exact + (jnp.log(nf / max_exact) / math.log(MAX_DISTANCE / max_exact)
                         * (half - max_exact)).astype(jnp.int32)
    large = jnp.minimum(large, half - 1)
    return ret + jnp.where(n < max_exact, n, large)


def _bucket_table():
    qi = jnp.arange(BLOCK)[:, None]
    kj = jnp.arange(3 * BLOCK)[None, :]
    rel = kj - BLOCK - qi
    return jnp.where(jnp.abs(rel) <= WINDOW, _t5_bucket(rel), -1).astype(jnp.int32)


def _bias_table(rel_bias, bucket, name):
    def body(rb_ref, bucket_ref, o_ref):
        h = pl.program_id(0)
        bk = bucket_ref[...]
        acc = jnp.full(bk.shape, NEG_INF, F32)
        for b in range(N_BUCKETS):
            acc = jnp.where(bk == b, rb_ref[b, h], acc)
        o_ref[...] = acc

    return pl.pallas_call(
        body, name=name, grid=(N_HEADS,),
        in_specs=[pl.BlockSpec(memory_space=pltpu.SMEM), _const_spec((BLOCK, 3 * BLOCK))],
        out_specs=pl.BlockSpec((None, BLOCK, 3 * BLOCK), lambda h: (h, 0, 0)),
        out_shape=jax.ShapeDtypeStruct((N_HEADS, BLOCK, 3 * BLOCK), F32),
        compiler_params=_params("parallel"),
    )(rel_bias, bucket)


def _bias_grad(ds_sum, bucket, name):
    def body(ds_ref, bucket_ref, o_ref):
        bk = bucket_ref[...]
        ds = ds_ref[...]
        row = lax.broadcasted_iota(jnp.int32, (N_BUCKETS, 128), 0)
        acc = jnp.zeros((N_BUCKETS, 128), F32)
        for b in range(N_BUCKETS):
            s = jnp.sum(jnp.sum(jnp.where(bk == b, ds, 0.0), axis=1, keepdims=True), axis=0, keepdims=True)
            acc = jnp.where(row == b, s, acc)
        o_ref[...] = acc

    return pl.pallas_call(
        body, name=name, grid=(N_HEADS,),
        in_specs=[pl.BlockSpec((None, BLOCK, 3 * BLOCK), lambda h: (h, 0, 0)), _const_spec((BLOCK, 3 * BLOCK))],
        out_specs=pl.BlockSpec((None, N_BUCKETS, 128), lambda h: (h, 0, 0)),
        out_shape=jax.ShapeDtypeStruct((N_HEADS, N_BUCKETS, 128), F32),
        compiler_params=_params("parallel"),
    )(ds_sum, bucket)


PAIR = 2 * HEAD_DIM


def _low_half(shape):
    return lax.broadcasted_iota(jnp.int32, shape, len(shape) - 1) % PAIR < HEAD_DIM


def _split_pair(a):
    low = _low_half(a.shape)
    zero = jnp.zeros_like(a)
    return jnp.concatenate([jnp.where(low, a, zero), jnp.where(low, zero, a)], axis=0)


def _kv_expand(proj, kv_off, name):
    T = proj.shape[0]
    kv_w = N_KV_HEADS * HEAD_DIM
    rows = _tile(T, 512)

    def body(k_ref, v_ref, ke_ref, ve_ref):
        for src, dst in ((k_ref, ke_ref), (v_ref, ve_ref)):
            for g in range(N_KV_HEADS // 2):
                x = src[:, g * PAIR:(g + 1) * PAIR].astype(F32)
                swapped = pltpu.roll(x, HEAD_DIM, 1)
                low = _low_half(x.shape)
                dst[:, 2 * g * PAIR:(2 * g + 1) * PAIR] = jnp.where(low, x, swapped).astype(BF16)
                dst[:, (2 * g + 1) * PAIR:(2 * g + 2) * PAIR] = jnp.where(low, swapped, x).astype(BF16)

    out = jax.ShapeDtypeStruct((T, N_KV_HEADS * PAIR), BF16)
    ospec = pl.BlockSpec((rows, N_KV_HEADS * PAIR), lambda i: (i, 0))
    return pl.pallas_call(
        body, name=name, grid=(T // rows,),
        in_specs=[pl.BlockSpec((rows, kv_w), lambda i: (i, kv_off // kv_w)),
                  pl.BlockSpec((rows, kv_w), lambda i: (i, kv_off // kv_w + 1))],
        out_specs=[ospec, ospec], out_shape=[out, out], compiler_params=_params("parallel"),
    )(proj, proj)


def _kv_fold(dke, dve, name):
    T = dke.shape[0]
    kv_w = N_KV_HEADS * HEAD_DIM
    rows = _tile(T, 512)

    def body(dk_ref, dv_ref, o_ref):
        for n, src in enumerate((dk_ref, dv_ref)):
            for g in range(N_KV_HEADS // 2):
                a = src[:, 2 * g * PAIR:(2 * g + 1) * PAIR]
                b = src[:, (2 * g + 1) * PAIR:(2 * g + 2) * PAIR]
                a = a + pltpu.roll(a, HEAD_DIM, 1)
                b = b + pltpu.roll(b, HEAD_DIM, 1)
                o_ref[:, n * kv_w + g * PAIR:n * kv_w + (g + 1) * PAIR] = jnp.where(_low_half(a.shape), a, b).astype(BF16)

    ispec = pl.BlockSpec((rows, N_KV_HEADS * PAIR), lambda i: (i, 0))
    return pl.pallas_call(
        body, name=name, grid=(T // rows,), in_specs=[ispec, ispec],
        out_specs=pl.BlockSpec((rows, 2 * kv_w), lambda i: (i, 0)),
        out_shape=jax.ShapeDtypeStruct((T, 2 * kv_w), BF16), compiler_params=_params("parallel"),
    )(dke, dve)


def _key_blocks(i, nb):
    return [pl.multiple_of(n * BLOCK, BLOCK) for n in (jnp.maximum(i - 1, 0), i, jnp.minimum(i + 1, nb - 1))]


def _three_blocks(ref, starts):
    return jnp.concatenate([ref[pl.ds(s, BLOCK), :] for s in starts], axis=0)


def _pair_scores(q2, kd, bias_ref, pr, i, nb):
    qq = _split_pair(q2)
    s = lax.dot_general(qq, kd, _DIMS["nt"], preferred_element_type=F32) * (HEAD_DIM ** -0.5)
    s = s + bias_ref[2 * pr:2 * pr + 2].reshape(2 * BLOCK, 3 * BLOCK)
    kj = lax.broadcasted_iota(jnp.int32, (1, 3 * BLOCK), 1)
    outside = jnp.logical_or(jnp.logical_and(i == 0, kj < BLOCK), jnp.logical_and(i == nb - 1, kj >= 2 * BLOCK))
    return qq, jnp.where(outside, NEG_INF, s)


def _attn_specs(T, q_off):
    gw = GROUP * HEAD_DIM
    return dict(
        sink=pl.BlockSpec(memory_space=pltpu.SMEM),
        q=pl.BlockSpec((BLOCK, gw), lambda j, i: (i, q_off // gw + j)),
        kv=pl.BlockSpec((T, PAIR), lambda j, i: (0, j)),
        bias=pl.BlockSpec((GROUP, BLOCK, 3 * BLOCK), lambda j, i: (j, 0, 0)),
        o=pl.BlockSpec((BLOCK, gw), lambda j, i: (i, j)))


def _attn_fwd(proj, q_off, kexp, vexp, bias, sink, name):
    T = proj.shape[0]
    nb = T // BLOCK
    sp = _attn_specs(T, q_off)

    def body(sink_ref, q_ref, ke_ref, ve_ref, bias_ref, o_ref, lse_ref):
        j, i = pl.program_id(0), pl.program_id(1)
        starts = _key_blocks(i, nb)
        kd = _three_blocks(ke_ref, starts)
        vv = _split_pair(_three_blocks(ve_ref, starts))
        first_rows = lax.broadcasted_iota(jnp.int32, (2 * BLOCK, 1), 0) < BLOCK
        low = _low_half((BLOCK, PAIR))
        for pr in range(GROUP // 2):
            lanes = slice(pr * PAIR, (pr + 1) * PAIR)
            _, s = _pair_scores(q_ref[:, lanes], kd, bias_ref, pr, i, nb)
            head = GROUP * j + 2 * pr
            sk = jnp.where(first_rows, sink_ref[head], sink_ref[head + 1])
            m = jnp.maximum(jnp.max(s, axis=-1, keepdims=True), sk)
            p = jnp.exp(s - m)
            denom = jnp.sum(p, axis=-1, keepdims=True) + jnp.exp(sk - m)
            p = (p / denom).astype(BF16)
            pp = jnp.concatenate([p[:BLOCK], p[BLOCK:]], axis=1)
            o_ref[:, lanes] = lax.dot_general(pp, vv, _DIMS["nn"], preferred_element_type=F32).astype(BF16)
            lse = m + jnp.log(denom)
            lse_ref[:, lanes] = jnp.where(low, lse[:BLOCK], lse[BLOCK:])

    return pl.pallas_call(
        body, name=name, grid=(N_KV_HEADS, nb),
        in_specs=[sp["sink"], sp["q"], sp["kv"], sp["kv"], sp["bias"]], out_specs=[sp["o"], sp["o"]],
        out_shape=[jax.ShapeDtypeStruct((T, N_HEADS * HEAD_DIM), BF16), jax.ShapeDtypeStruct((T, N_HEADS * HEAD_DIM), F32)],
        compiler_params=_params("parallel", "parallel"),
    )(sink, proj, kexp, vexp, bias)


def _attn_bwd(proj, q_off, kexp, vexp, bias, sink, out, lse, dout, name):
    T = proj.shape[0]
    nb = T // BLOCK
    sp = _attn_specs(T, q_off)
    scale = HEAD_DIM ** -0.5

    def body(sink_ref, q_ref, ke_ref, ve_ref, bias_ref, o_ref, lse_ref, do_ref,
             dq_ref, dke_ref, dve_ref, ds_ref, dsink_ref):
        j, i = pl.program_id(0), pl.program_id(1)

        @pl.when(i == 0)
        def _():
            dke_ref[...] = jnp.zeros(dke_ref.shape, F32)
            dve_ref[...] = jnp.zeros(dve_ref.shape, F32)
            ds_ref[...] = jnp.zeros(ds_ref.shape, F32)
            dsink_ref[...] = jnp.zeros(dsink_ref.shape, F32)

        starts = _key_blocks(i, nb)
        kd = _three_blocks(ke_ref, starts)
        vd = _three_blocks(ve_ref, starts)
        kk = _split_pair(kd)
        low = _low_half((BLOCK, PAIR))
        dk_acc = jnp.zeros((3 * BLOCK, PAIR), F32)
        dv_acc = jnp.zeros((3 * BLOCK, PAIR), F32)
        for pr in range(GROUP // 2):
            lanes = slice(pr * PAIR, (pr + 1) * PAIR)
            qq, s = _pair_scores(q_ref[:, lanes], kd, bias_ref, pr, i, nb)
            l2 = lse_ref[:, lanes]
            l2s = pltpu.roll(l2, HEAD_DIM, 1)
            lse_a, lse_b = jnp.where(low, l2, l2s), jnp.where(low, l2s, l2)
            p = jnp.exp(s - jnp.concatenate([jnp.concatenate([lse_a] * 3, axis=1),
                                             jnp.concatenate([lse_b] * 3, axis=1)], axis=0))
            do2 = do_ref[:, lanes]
            prod = do2.astype(F32) * o_ref[:, lanes].astype(F32)
            delta_a = jnp.sum(jnp.where(low, prod, 0.0), axis=-1, keepdims=True)
            delta_b = jnp.sum(jnp.where(low, 0.0, prod), axis=-1, keepdims=True)
            dd = _split_pair(do2)
            dp = lax.dot_general(dd, vd, _DIMS["nt"], preferred_element_type=F32)
            ds = p * (dp - jnp.concatenate([delta_a, delta_b], axis=0))
            dsb = ds.astype(BF16)
            dq = lax.dot_general(jnp.concatenate([dsb[:BLOCK], dsb[BLOCK:]], axis=1), kk, _DIMS["nn"],
                                 preferred_element_type=F32) * scale
            dq_ref[:, lanes] = dq.astype(BF16)
            dk_acc += lax.dot_general(dsb, qq, _DIMS["tn"], preferred_element_type=F32) * scale
            dv_acc += lax.dot_general(p.astype(BF16), dd, _DIMS["tn"], preferred_element_type=F32)
            ds_ref[2 * pr:2 * pr + 2] += ds.reshape(2, BLOCK, 3 * BLOCK)
            head = GROUP * j + 2 * pr
            p_sink = jnp.exp(jnp.where(low, sink_ref[head], sink_ref[head + 1]) - l2)
            dsink_ref[:, lanes] += jnp.sum(-p_sink * jnp.where(low, delta_a, delta_b), axis=0, keepdims=True)
        for t, start in enumerate(starts):
            dke_ref[pl.ds(start, BLOCK), :] += dk_acc[t * BLOCK:(t + 1) * BLOCK]
            dve_ref[pl.ds(start, BLOCK), :] += dv_acc[t * BLOCK:(t + 1) * BLOCK]

    kv_out = jax.ShapeDtypeStruct((T, N_KV_HEADS * PAIR), F32)
    return pl.pallas_call(
        body, name=name, grid=(N_KV_HEADS, nb),
        in_specs=[sp["sink"], sp["q"], sp["kv"], sp["kv"], sp["bias"], sp["o"], sp["o"], sp["o"]],
        out_specs=[sp["o"], sp["kv"], sp["kv"], sp["bias"],
                   pl.BlockSpec((1, GROUP * HEAD_DIM), lambda j, i: (0, j))],
        out_shape=[jax.ShapeDtypeStruct((T, N_HEADS * HEAD_DIM), BF16), kv_out, kv_out,
                   jax.ShapeDtypeStruct((N_HEADS, BLOCK, 3 * BLOCK), F32),
                   jax.ShapeDtypeStruct((1, N_HEADS * HEAD_DIM), F32)],
        compiler_params=_params("parallel", "arbitrary"),
    )(sink, proj, kexp, vexp, bias, out, lse, dout)


GATE_COLS = 512


def _sigmoid(x):
    return 1.0 / (1.0 + jnp.exp(-x))


def _gate_specs(D, gate_off):
    nc = D // GATE_COLS
    base = gate_off // GATE_COLS
    return [pl.BlockSpec((ROWS, GATE_COLS), lambda i, c=base + g * nc + h: (i, c)) for g in range(3) for h in range(nc)]


def _merge_fwd(proj, gate_off, ya, yp, yt, scale, name):
    T, D = ya.shape
    nc = D // GATE_COLS

    def body(*refs):
        gates = refs[:3 * nc]
        ya_ref, yp_ref, yt_ref, s_ref, o_ref = refs[3 * nc:]
        for h in range(nc):
            cols = slice(h * GATE_COLS, (h + 1) * GATE_COLS)
            merged = (_sigmoid(_f32(gates[h])) * ya_ref[:, cols].astype(F32)
                      + _sigmoid(_f32(gates[nc + h])) * (yp_ref[:, cols].astype(F32) * s_ref[:, cols])
                      + _sigmoid(_f32(gates[2 * nc + h])) * yt_ref[:, cols].astype(F32))
            o_ref[:, cols] = merged.astype(BF16)

    yspec = _row_spec(D)
    return pl.pallas_call(
        body, name=name, grid=(T // ROWS,),
        in_specs=_gate_specs(D, gate_off) + [yspec, yspec, yspec, _const_spec((1, D))], out_specs=yspec,
        out_shape=jax.ShapeDtypeStruct((T, D), BF16), compiler_params=_params("parallel"),
    )(*([proj] * (3 * nc)), ya, yp, yt, scale)


def _merge_bwd(proj, gate_off, ya, yp, yt, scale, dm, name):
    T, D = ya.shape
    nc = D // GATE_COLS

    def body(*refs):
        gates = refs[:3 * nc]
        ya_ref, yp_ref, yt_ref, s_ref, dm_ref, dg_ref, dya_ref, dyp_ref, dyt_ref, ds_ref = refs[3 * nc:]
        parts = []
        for h in range(nc):
            cols = slice(h * GATE_COLS, (h + 1) * GATE_COLS)
            dm_v = dm_ref[:, cols].astype(F32)
            sa, sp, st = _sigmoid(_f32(gates[h])), _sigmoid(_f32(gates[nc + h])), _sigmoid(_f32(gates[2 * nc + h]))
            yp_v, s_v = yp_ref[:, cols].astype(F32), s_ref[:, cols]
            dg_ref[:, h * GATE_COLS:(h + 1) * GATE_COLS] = (dm_v * ya_ref[:, cols].astype(F32) * sa * (1.0 - sa)).astype(BF16)
            dg_ref[:, D + h * GATE_COLS:D + (h + 1) * GATE_COLS] = (dm_v * (yp_v * s_v) * sp * (1.0 - sp)).astype(BF16)
            dg_ref[:, 2 * D + h * GATE_COLS:2 * D + (h + 1) * GATE_COLS] = (
                dm_v * yt_ref[:, cols].astype(F32) * st * (1.0 - st)).astype(BF16)
            dya_ref[:, cols] = (dm_v * sa).astype(BF16)
            dyps = dm_v * sp
            dyp_ref[:, cols] = (dyps * s_v).astype(BF16)
            dyt_ref[:, cols] = (dm_v * st).astype(BF16)
            parts.append(jnp.sum(dyps * yp_v, axis=0, keepdims=True))
        _accumulate(ds_ref, jnp.concatenate(parts, axis=1))

    yspec = _row_spec(D)
    out = jax.ShapeDtypeStruct((T, D), BF16)
    return pl.pallas_call(
        body, name=name, grid=(T // ROWS,),
        in_specs=_gate_specs(D, gate_off) + [yspec, yspec, yspec, _const_spec((1, D)), yspec],
        out_specs=[_row_spec(3 * D), yspec, yspec, yspec, _const_spec((1, D))],
        out_shape=[jax.ShapeDtypeStruct((T, 3 * D), BF16), out, out, out, jax.ShapeDtypeStruct((1, D), F32)],
        compiler_params=_params("arbitrary"),
    )(*([proj] * (3 * nc)), ya, yp, yt, scale, dm)


def _swiglu_fwd(gu, name):
    T = gu.shape[0]
    F = gu.shape[1] // 2

    def body(gu_ref, o_ref):
        g = gu_ref[:, 0:F].astype(F32)
        o_ref[...] = (g * _sigmoid(g) * gu_ref[:, F:2 * F].astype(F32)).astype(BF16)

    return pl.pallas_call(
        body, name=name, grid=(T // ROWS,), in_specs=[_row_spec(2 * F)], out_specs=_row_spec(F),
        out_shape=jax.ShapeDtypeStruct((T, F), BF16), compiler_params=_params("parallel"),
    )(gu)


def _swiglu_bwd(gu, dact, name):
    T = gu.shape[0]
    F = gu.shape[1] // 2

    def body(gu_ref, d_ref, o_ref):
        g, d = gu_ref[:, 0:F].astype(F32), d_ref[...].astype(F32)
        sg = _sigmoid(g)
        o_ref[:, 0:F] = (d * gu_ref[:, F:2 * F].astype(F32) * sg * (1.0 + g * (1.0 - sg))).astype(BF16)
        o_ref[:, F:2 * F] = (d * g * sg).astype(BF16)

    return pl.pallas_call(
        body, name=name, grid=(T // ROWS,), in_specs=[_row_spec(2 * F), _row_spec(F)], out_specs=_row_spec(2 * F),
        out_shape=jax.ShapeDtypeStruct((T, 2 * F), BF16), compiler_params=_params("parallel"),
    )(gu, dact)


def _local_step(x, target, wts, small):
    T, D = x.shape
    depth = len(wts)
    gate_off = wts[0]["w_inT"].shape[0] - 3 * D
    q_off = 4 * D
    kv_w = N_KV_HEADS * HEAD_DIM
    ff2 = wts[0]["w_guT"].shape[0]
    bucket = _bucket_table()
    bias = _bias_table(small["rel_bias"], bucket, "bias_table")

    saved = []
    for l in range(depth):
        n = f"l{l}_"
        w = wts[l]
        h = _rms_fwd(x, small["g_mix"][l], n + "rms_mix")
        proj = _matmul(h, w["w_inT"], "nt", BF16, n + "proj")
        z, p = _mixer_fwd(proj, small["conv_w"][l], n + "mixer")
        kexp, vexp = _kv_expand(proj, q_off + D, n + "kv_expand")
        sink = small["attn_sink"][l]
        att, lse = _attn_fwd(proj, q_off, kexp, vexp, bias, sink, n + "attn")
        ya =_matmul(z, w["w_a_out"], "nn", BF16, n + "ya")
        yp = _pool_mm(p, w["w_pool"], "nn", BF16, n + "yp")
        yt = _matmul(att, w["w_attn_out"], "nn", BF16, n + "yt")
        merged = _merge_fwd(proj, gate_off, ya, yp, yt, small["pool_scale"][l], n + "merge")
        x1 = _matmul(merged, w["w_o"], "nn", F32, n + "x1", res=x)
        h2 = _rms_fwd(x1, small["g_ffn"][l], n + "rms_ffn")
        gu = _matmul(h2, w["w_guT"], "nt", BF16, n + "gu")
        act = _swiglu_fwd(gu, n + "swiglu")
        x2 = _matmul(act, w["w_down"], "nn", F32, n + "x2", res=x1, tn_cap=512, tk_cap=ff2 // 2)
        saved.append(dict(x=x, h=h, proj=proj, z=z, p=p, kexp=kexp, vexp=vexp, sink=sink, lse=lse, att=att,
                          ya=ya, yp=yp, yt=yt, merged=merged, x1=x1, h2=h2, gu=gu, act=act))
        x = x2

    loss, dx, dxb, dg_final = _loss_head(x, small["g_final"], target, "loss_head")

    gw = [None] * depth
    gs = {k_: [None] * depth for k_ in ("conv_w", "pool_scale", "g_mix", "g_ffn", "attn_sink")}
    ds_total = None
    for l in reversed(range(depth)):
        n = f"l{l}_b_"
        s, w, g = saved[l], wts[l], {}
        g["w_down"] = _matmul(s["act"], dxb, "tn", BF16, n + "dw_down", tm_cap=ff2 // 2, tk_cap=512)
        dact = _matmul(dxb, w["w_down"], "nt", BF16, n + "dact", tm_cap=512, tn_cap=ff2 // 2)
        dgu = _swiglu_bwd(s["gu"], dact, n + "swiglu")
        g["w_guT"] = _matmul(dgu, s["h2"], "tn", BF16, n + "dw_gu", tm_cap=512)
        dh2 = _matmul(dgu, w["w_guT"], "nn", F32, n + "dh2", tn_cap=512, tk_cap=ff2 // 2)
        dx1, dx1b, gs["g_ffn"][l] = _rms_bwd(s["x1"], small["g_ffn"][l], dh2, dx, n + "rms_ffn")
        g["w_o"] = _matmul(s["merged"], dx1b, "tn", BF16, n + "dw_o")
        dm = _matmul(dx1b, w["w_o"], "nt", BF16, n + "dmerged")
        dgates, dya, dyp, dyt, gs["pool_scale"][l] = _merge_bwd(
            s["proj"], gate_off, s["ya"], s["yp"], s["yt"], small["pool_scale"][l], dm, n + "merge")
        g["w_a_out"] = _matmul(s["z"], dya, "tn", BF16, n + "dw_a_out")
        dz = _matmul(dya, w["w_a_out"], "nt", BF16, n + "dz")
        g["w_pool"] = _pool_mm(s["p"], dyp, "tn", F32, n + "dw_pool")
        dp = _pool_mm(dyp, w["w_pool"], "nt", BF16, n + "dp")
        g["w_attn_out"] = _matmul(s["att"], dyt, "tn", BF16, n + "dw_attn_out")
        datt = _matmul(dyt, w["w_attn_out"], "nt", BF16, n + "datt")
        dmix, gs["conv_w"][l] = _mixer_bwd(s["proj"], small["conv_w"][l], dz, dp, n + "mixer")
        dq, dke, dve, ds_sum, dsink = _attn_bwd(s["proj"], q_off, s["kexp"], s["vexp"], bias, s["sink"], s["att"],
                                                s["lse"], datt, n + "attn")
        gs["attn_sink"][l] = dsink.reshape(N_HEADS, HEAD_DIM)[:, 0]
        ds_total = ds_sum if ds_total is None else ds_total + ds_sum
        dproj = jnp.concatenate([dmix, dq, _kv_fold(dke, dve, n + "kv_fold"), dgates], axis=1)
        g["w_inT"] = _matmul(dproj, s["h"], "tn", BF16, n + "dw_in", tm_cap=512)
        dh = _matmul(dproj, w["w_inT"], "nn", F32, n + "dh", tk_cap=2816)
        dx, dxb, gs["g_mix"][l] = _rms_bwd(s["x"], small["g_mix"][l], dh, dx1, n + "rms_mix")
        gw[l] = g

    d_rel = _bias_grad(ds_total, bucket, "bias_grad")[:, :, 0].T
    gs = {k_: jnp.stack(v_) for k_, v_ in gs.items()}
    gs["rel_bias"] = d_rel
    gs["g_final"] = dg_final
    return loss, dx, gw, gs


HBM_SPEC = pl.BlockSpec(memory_space=pltpu.HBM)


def _place():
    return lax.axis_index("x"), lax.axis_index("y"), lax.axis_index("c")


def _all_gather(v, name):
    def body(x_ref, out_ref, send_sems, recv_sems, local_sem):
        x, y, c = _place()
        me, sibling = (x, y, c), (x, y, 1 - c)
        chips = [(1 - x, y), (x, 1 - y), (1 - x, 1 - y)]

        def rows(px, py, pc):
            return out_ref.at[4 * px + 2 * py + pc]

        def copy(k, block, to, src=None):
            return pltpu.make_async_remote_copy(
                src_ref=rows(*block) if src is None else src, dst_ref=rows(*block),
                send_sem=send_sems.at[k], recv_sem=recv_sems.at[k], device_id=to, device_id_type=MESH)

        mine = pltpu.make_async_copy(x_ref, rows(*me), local_sem)
        mine.start()
        first = [copy(0, me, sibling, src=x_ref)]
        first += [copy(1 + j, me, (*chip, c), src=x_ref) for j, chip in enumerate(chips)]
        for cp in first:
            cp.start()
        passed = [copy(4 + j, (*chip, c), sibling) for j, chip in enumerate(chips)]
        for j, chip in enumerate(chips):
            copy(1 + j, (*chip, c), me).wait_recv()
            passed[j].start()
        copy(0, sibling, me).wait_recv()
        for j, chip in enumerate(chips):
            copy(4 + j, (*chip, 1 - c), me).wait_recv()
        for cp in first + passed:
            cp.wait_send()
        mine.wait()

    return pl.pallas_call(
        body, name=name, in_specs=[HBM_SPEC], out_specs=HBM_SPEC,
        out_shape=jax.ShapeDtypeStruct((N_DEV,) + v.shape, v.dtype),
        scratch_shapes=[pltpu.SemaphoreType.DMA((7,)), pltpu.SemaphoreType.DMA((7,)), pltpu.SemaphoreType.DMA],
    )(v)


def _all_gather_many(parts, name):
    n = len(parts)

    def body(*refs):
        ins, outs = refs[:n], refs[n:2 * n]
        send_sems, recv_sems, local_sems = refs[2 * n:]
        x, y, c = _place()
        me, sibling = (x, y, c), (x, y, 1 - c)
        chips = [(1 - x, y), (x, 1 - y), (1 - x, 1 - y)]

        def rows(t, px, py, pc):
            return outs[t].at[4 * px + 2 * py + pc]

        def copy(t, k, block, to, src=None):
            return pltpu.make_async_remote_copy(
                src_ref=rows(t, *block) if src is None else src, dst_ref=rows(t, *block),
                send_sem=send_sems.at[7 * t + k], recv_sem=recv_sems.at[7 * t + k], device_id=to, device_id_type=MESH)

        mine = [pltpu.make_async_copy(ins[t], rows(t, *me), local_sems.at[t]) for t in range(n)]
        sends = []
        for t in range(n):
            mine[t].start()
            sends.append(copy(t, 0, me, sibling, src=ins[t]))
            sends += [copy(t, 1 + j, me, (*chip, c), src=ins[t]) for j, chip in enumerate(chips)]
        for cp in sends:
            cp.start()
        for j, chip in enumerate(chips):
            for t in range(n):
                copy(t, 1 + j, (*chip, c), me).wait_recv()
                passed = copy(t, 4 + j, (*chip, c), sibling)
                passed.start()
                sends.append(passed)
        for t in range(n):
            copy(t, 0, sibling, me).wait_recv()
            for j, chip in enumerate(chips):
                copy(t, 4 + j, (*chip, 1 - c), me).wait_recv()
        for cp in sends:
            cp.wait_send()
        for cp in mine:
            cp.wait()

    return pl.pallas_call(
        body, name=name, in_specs=[HBM_SPEC] * n, out_specs=[HBM_SPEC] * n,
        out_shape=[jax.ShapeDtypeStruct((N_DEV,) + p.shape, p.dtype) for p in parts],
        scratch_shapes=[pltpu.SemaphoreType.DMA((7 * n,)), pltpu.SemaphoreType.DMA((7 * n,)),
                        pltpu.SemaphoreType.DMA((n,))],
    )(*parts)


def _swap_with_sibling(g, name):
    def body(g_ref, out_ref, send_sem, recv_sem):
        x, y, c = _place()
        cp = pltpu.make_async_remote_copy(src_ref=g_ref.at[1 - c], dst_ref=out_ref, send_sem=send_sem,
                                          recv_sem=recv_sem, device_id=(x, y, 1 - c), device_id_type=MESH)
        cp.start()
        cp.wait()

    return pl.pallas_call(
        body, name=name, in_specs=[HBM_SPEC], out_specs=HBM_SPEC,
        out_shape=jax.ShapeDtypeStruct(g.shape[1:], g.dtype),
        scratch_shapes=[pltpu.SemaphoreType.DMA, pltpu.SemaphoreType.DMA],
    )(g)


def _chip_exchange(p, name):
    def body(p_ref, out_ref, send_sems, recv_sems):
        x, y, c = _place()
        chips = [(1 - x, y), (x, 1 - y), (1 - x, 1 - y)]
        copies = [pltpu.make_async_remote_copy(
            src_ref=p_ref.at[2 * px + py], dst_ref=out_ref.at[k], send_sem=send_sems.at[k], recv_sem=recv_sems.at[k],
            device_id=(px, py, c), device_id_type=MESH) for k, (px, py) in enumerate(chips)]
        for cp in copies:
            cp.start()
        for cp in copies:
            cp.wait()

    return pl.pallas_call(
        body, name=name, in_specs=[HBM_SPEC], out_specs=HBM_SPEC,
        out_shape=jax.ShapeDtypeStruct((3,) + p.shape[1:], p.dtype),
        scratch_shapes=[pltpu.SemaphoreType.DMA((3,)), pltpu.SemaphoreType.DMA((3,))],
    )(p)


def _sum_parts(own, index, others, out_dtype, name):
    R = own.shape[1]
    rows = _tile(R, 512)
    k = others.shape[0]

    def body(idx_ref, own_ref, *refs):
        del idx_ref
        acc = own_ref[...].astype(F32)
        for r in refs[:k]:
            acc = acc + r[...].astype(F32)
        refs[k][...] = acc.astype(out_dtype)

    grid_spec = pltpu.PrefetchScalarGridSpec(
        num_scalar_prefetch=1, grid=(R // rows,),
        in_specs=[pl.BlockSpec((None, rows, LANES), lambda i, idx: (idx[0], i, 0))]
        + [pl.BlockSpec((None, rows, LANES), lambda i, idx, j=j: (j, i, 0)) for j in range(k)],
        out_specs=pl.BlockSpec((rows, LANES), lambda i, idx: (i, 0)))
    return pl.pallas_call(
        body, name=name, grid_spec=grid_spec,
        out_shape=jax.ShapeDtypeStruct((R, LANES), out_dtype), compiler_params=_params("parallel"),
    )(jnp.reshape(index, (1,)).astype(jnp.int32), own, *([others] * k))


def _adamw(w, g, m, v, name):
    shape = w.shape
    cols = shape[-1]
    rows_total = w.size // cols
    w2, g2, m2, v2 = (a.reshape(rows_total, cols) for a in (w, g, m, v))
    rows = rows_total
    if rows_total > ROWS:
        rows = next(r for r in range(ROWS, 0, -8) if rows_total % r == 0)

    def body(w_ref, g_ref, m_ref, v_ref, d_ref, nm_ref, nv_ref):
        gv = g_ref[...]
        nm = ADAM_B1 * m_ref[...] + (1.0 - ADAM_B1) * gv
        nv = ADAM_B2 * v_ref[...] + (1.0 - ADAM_B2) * (gv * gv)
        m_hat = nm / (1.0 - ADAM_B1 ** ADAM_STEP)
        v_hat = nv / (1.0 - ADAM_B2 ** ADAM_STEP)
        d_ref[...] = -ADAM_LR * (m_hat / (jnp.sqrt(v_hat) + ADAM_EPS) + ADAM_WD * w_ref[...])
        nm_ref[...] = nm
        nv_ref[...] = nv

    spec = pl.BlockSpec((rows, cols), lambda i: (i, 0))
    out = jax.ShapeDtypeStruct((rows_total, cols), F32)
    d, nm, nv = pl.pallas_call(
        body, name=name, grid=(rows_total // rows,), in_specs=[spec] * 4, out_specs=[spec] * 3,
        out_shape=[out, out, out], compiler_params=_params("parallel"),
    )(w2, g2, m2, v2)
    return d.reshape(shape), nm.reshape(shape), nv.reshape(shape)


BIG = ("w_in", "w_a_out", "w_pool", "w_attn_out", "w_o", "w_gu", "w_down")


LOCAL = dict(w_in="w_inT", w_a_out="w_a_out", w_pool="w_pool", w_attn_out="w_attn_out", w_o="w_o", w_gu="w_guT",
             w_down="w_down")


def _shard_rows(w, l):
    out = []
    for name in BIG:
        a = w[name][l]
        if name in ("w_in", "w_gu"):
            a = a.T
        elif name == "w_pool":
            a = a.reshape(-1, a.shape[-1])
        out.append(a.astype(BF16))
    return out


def _full_weights(gathered, w):
    out = {}
    for name, g in zip(BIG, gathered):
        if name == "w_pool":
            G, rg, cg = w[name].shape[1:]
            out[name] = jnp.transpose(g.reshape(N_DEV, G, rg, cg), (1, 0, 2, 3)).reshape(G, N_DEV * rg, cg)
        else:
            out[LOCAL[name]] = g.reshape(N_DEV * g.shape[1], g.shape[2])
    return out


def _split_grads(gw, w):
    layers, spans = [], {}
    for l, g in enumerate(gw):
        parts, at = [], 0
        for name in BIG:
            a = g[LOCAL[name]].astype(BF16)
            if name == "w_pool":
                G, rg, cg = w[name].shape[1:]
                a = jnp.transpose(a.reshape(G, N_DEV, rg, cg), (1, 0, 2, 3))
            a = jnp.transpose(a.reshape(4, 2, -1, LANES), (1, 0, 2, 3))
            spans[name] = (at, at + a.shape[2])
            at += a.shape[2]
            parts.append(a)
        layers.append(jnp.concatenate(parts, axis=2))
    return jnp.concatenate(layers, axis=2), spans, at


def _own_grads(g_packed, spans, rows, w):
    L = w["w_in"].shape[0]
    g3 = g_packed.reshape(L, rows, LANES)
    out = {}
    for name in BIG:
        a = g3[:, spans[name][0]:spans[name][1]]
        if name in ("w_in", "w_gu"):
            sh = w[name].shape
            a = jnp.swapaxes(a.reshape(L, sh[2], sh[1]), 1, 2)
        out[name] = a.reshape(w[name].shape)
    return out


SMALL_ROWS = 32


def _pack_small(gs, L, D):
    rows = [gs["pool_scale"].reshape(L, D), gs["g_mix"].reshape(L, D), gs["g_ffn"].reshape(L, D),
            gs["g_final"].reshape(1, D), gs["conv_w"][:, :3].reshape(3 * L, D),
            jnp.pad(gs["attn_sink"].reshape(1, -1), ((0, 0), (0, D - L * N_HEADS))),
            jnp.pad(gs["rel_bias"].reshape(1, -1), ((0, 0), (0, D - N_BUCKETS * N_HEADS)))]
    a = jnp.concatenate(rows, axis=0)
    return jnp.pad(a, ((0, SMALL_ROWS - a.shape[0]), (0, 0)))


def _unpack_small(a, L, D):
    g = {}
    g["pool_scale"] = a[0:L]
    g["g_mix"] = a[L:2 * L]
    g["g_ffn"] = a[2 * L:3 * L]
    g["g_final"] = a[3 * L]
    g["conv_w"] = a[3 * L + 1:6 * L + 1].reshape(L, 3, 1, D)
    g["attn_sink"] = a[6 * L + 1, :L * N_HEADS].reshape(L, N_HEADS)
    g["rel_bias"] = a[6 * L + 2, :N_BUCKETS * N_HEADS].reshape(N_BUCKETS, N_HEADS)
    return g


WEIGHTS = ("w_in", "conv_w", "w_a_out", "w_pool", "pool_scale", "w_attn_out", "attn_sink", "w_o", "g_mix", "g_ffn",
           "w_gu", "w_down", "rel_bias", "g_final")


def kernel(x, w_in, conv_w, w_a_out, w_pool, pool_scale, w_attn_out, attn_sink, w_o, g_mix, g_ffn, w_gu, w_down, rel_bias, g_final, loss_target, m_w_in, m_conv_w, m_w_a_out, m_w_pool, m_pool_scale, m_w_attn_out, m_attn_sink, m_w_o, m_g_mix, m_g_ffn, m_w_gu, m_w_down, m_rel_bias, m_g_final, v_w_in, v_conv_w, v_w_a_out, v_w_pool, v_pool_scale, v_w_attn_out, v_attn_sink, v_w_o, v_g_mix, v_g_ffn, v_w_gu, v_w_down, v_rel_bias, v_g_final):
    w = dict(w_in=w_in, conv_w=conv_w, w_a_out=w_a_out, w_pool=w_pool, pool_scale=pool_scale, w_attn_out=w_attn_out,
             attn_sink=attn_sink, w_o=w_o, g_mix=g_mix, g_ffn=g_ffn, w_gu=w_gu, w_down=w_down, rel_bias=rel_bias,
             g_final=g_final)
    m = dict(w_in=m_w_in, conv_w=m_conv_w, w_a_out=m_w_a_out, w_pool=m_w_pool, pool_scale=m_pool_scale,
             w_attn_out=m_w_attn_out, attn_sink=m_attn_sink, w_o=m_w_o, g_mix=m_g_mix, g_ffn=m_g_ffn, w_gu=m_w_gu,
             w_down=m_w_down, rel_bias=m_rel_bias, g_final=m_g_final)
    v = dict(w_in=v_w_in, conv_w=v_conv_w, w_a_out=v_w_a_out, w_pool=v_w_pool, pool_scale=v_pool_scale,
             w_attn_out=v_w_attn_out, attn_sink=v_attn_sink, w_o=v_w_o, g_mix=v_g_mix, g_ffn=v_g_ffn, w_gu=v_w_gu,
             w_down=v_w_down, rel_bias=v_rel_bias, g_final=v_g_final)
    T, D = x.shape[1], x.shape[2]
    L = w_in.shape[0]
    cx, cy, cc = _place()

    wts = [_full_weights(_all_gather_many(_shard_rows(w, l), f"gather_weights_l{l}"), w) for l in range(L)]
    cw = jnp.pad(conv_w.reshape(L * 3, -1), ((0, 16 - L * 3), (0, 0)))
    cw = _all_gather(cw, "gather_conv_w")
    cw = jnp.transpose(cw, (1, 0, 2)).reshape(16, -1)[:L * 3].reshape(L, 3, -1)
    small = dict(conv_w=jnp.pad(cw, ((0, 0), (0, 5), (0, 0))), pool_scale=pool_scale.reshape(L, 1, D),
                 g_mix=g_mix.reshape(L, 1, D), g_ffn=g_ffn.reshape(L, 1, D), attn_sink=attn_sink,
                 rel_bias=rel_bias, g_final=g_final.reshape(1, D))

    loss, dx, gw, gs = _local_step(x[0], loss_target[0], wts, small)
    loss = lax.psum(loss[0, 0], ("x", "y", "c"))

    split, spans, rows = _split_grads(gw, w)
    from_sibling = _swap_with_sibling(split, "reduce_pair")
    pair = _sum_parts(split.reshape(2, 4 * L * rows, LANES), cc, from_sibling.reshape(1, 4 * L * rows, LANES),
                      BF16, "pair_sum").reshape(4, L * rows, LANES)
    from_chips = _chip_exchange(pair, "reduce_chips")
    g_packed = _sum_parts(pair, 2 * cx + cy, from_chips, F32, "chip_sum")
    grads = _own_grads(g_packed, spans, rows, w)

    small_all = _all_gather(_pack_small(gs, L, D), "gather_small")
    small_sum = _sum_parts(small_all, jnp.int32(0), small_all[1:], F32, "small_sum")
    gsm = _unpack_small(small_sum, L, D)
    W8 = D // N_DEV
    dev = 4 * cx + 2 * cy + cc
    gsm["conv_w"] = lax.dynamic_slice_in_dim(gsm["conv_w"], dev * W8, W8, axis=3)
    grads.update(gsm)

    deltas, new_m, new_v = {}, {}, {}
    for name in WEIGHTS:
        deltas[name], new_m[name], new_v[name] = _adamw(w[name], grads[name], m[name], v[name], "adamw_" + name)

    return (loss, dx[None], *[grads[n] for n in WEIGHTS], *[deltas[n] for n in WEIGHTS],
            *[new_m[n] for n in WEIGHTS], *[new_v[n] for n in WEIGHTS])
```

```python
import functools
import math

import jax
import jax.numpy as jnp
from jax import lax
from jax.experimental import pallas as pl
from jax.experimental.pallas import tpu as pltpu

F32 = jnp.float32
BF16 = jnp.bfloat16
MESH = pl.DeviceIdType.MESH

N_DEV = 8
N_HEADS = 16
N_KV_HEADS = 4
HEAD_DIM = 64
GROUP = N_HEADS // N_KV_HEADS
BLOCK = 128
WINDOW = 128
N_BUCKETS = 32
MAX_DISTANCE = 128
POOL_WINDOWS = (2, 4, 8, 16)
POOL_GROUPS = 4
HALO = 8
EPS = 1e-6
NEG_INF = -1e30

ADAM_LR = 0.001
ADAM_B1 = 0.9
ADAM_B2 = 0.999
ADAM_EPS = 1e-08
ADAM_WD = 0.01
ADAM_STEP = 10

LANES = 1024
VMEM_LIMIT_BYTES = 48 * 1024 * 1024


def _params(*sem):
    return pltpu.CompilerParams(dimension_semantics=sem, vmem_limit_bytes=VMEM_LIMIT_BYTES)


def _tile(n, cap):
    if n <= cap:
        return n
    for t in range(cap - cap % 128, 0, -128):
        if n % t == 0:
            return t
    raise ValueError(f"no tile for {n}")


_DIMS = {"nn": (((1,), (0,)), ((), ())), "nt": (((1,), (1,)), ((), ())), "tn": (((0,), (0,)), ((), ()))}


HBM_SPEC = pl.BlockSpec(memory_space=pltpu.HBM)


def _matmul(a, b, mode, out_dtype, name, res=None, tm_cap=1024, tn_cap=1024, tk_cap=1024, comm=None):
    if mode == "tn":
        K, M = a.shape
    else:
        M, K = a.shape
    N = b.shape[0] if mode == "nt" else b.shape[1]
    tm, tn, tk = _tile(M, tm_cap), _tile(N, tn_cap), _tile(K, tk_cap)
    nk = K // tk
    a_spec = pl.BlockSpec((tk, tm), lambda i, j, k: (k, i)) if mode == "tn" else pl.BlockSpec((tm, tk), lambda i, j, k: (i, k))
    b_spec = pl.BlockSpec((tn, tk), lambda i, j, k: (j, k)) if mode == "nt" else pl.BlockSpec((tk, tn), lambda i, j, k: (k, j))
    o_spec = pl.BlockSpec((tm, tn), lambda i, j, k: (i, j))
    dims = _DIMS[mode]
    has_res = res is not None
    gm, gn = M // tm, N // tn
    steps = gm * gn * nk
    n_in = 2 + has_res
    n_ci = len(comm.ins) if comm is not None else 0
    n_co = len(comm.outs) if comm is not None else 0

    def body(*refs):
        a_ref, b_ref = refs[0], refs[1]
        res_ref = refs[2] if has_res else None
        comm_in = refs[n_in:n_in + n_ci]
        o_ref = refs[n_in + n_ci]
        comm_out = refs[n_in + n_ci + 1:n_in + n_ci + 1 + n_co]
        acc_ref = refs[n_in + n_ci + 1 + n_co]
        sems = refs[n_in + n_ci + 2 + n_co:]
        k = pl.program_id(2)
        step = (pl.program_id(0) * gn + pl.program_id(1)) * nk + k
        if comm is not None:
            @pl.when(step == 0)
            def _():
                comm.start(comm_in, comm_out, sems)

        part = lax.dot_general(a_ref[...], b_ref[...], dims, preferred_element_type=F32)

        @pl.when(k == 0)
        def _():
            acc_ref[...] = part

        @pl.when(k > 0)
        def _():
            acc_ref[...] += part

        @pl.when(k == nk - 1)
        def _():
            out = acc_ref[...]
            if has_res:
                out = out + res_ref[...]
            o_ref[...] = out.astype(out_dtype)

        if comm is not None:
            @pl.when(step == (3 * steps) // 4)
            def _():
                comm.mid(comm_in, comm_out, sems)

            @pl.when(step == steps - 1)
            def _():
                comm.finish(comm_in, comm_out, sems)

    in_specs = [a_spec, b_spec] + ([o_spec] if has_res else [])
    args = (a, b) + ((res,) if has_res else ())
    out_shape = jax.ShapeDtypeStruct((M, N), out_dtype)
    if comm is None:
        return pl.pallas_call(
            body, name=name, grid=(gm, gn, nk), in_specs=in_specs, out_specs=o_spec, out_shape=out_shape,
            scratch_shapes=[pltpu.VMEM((tm, tn), F32)],
            compiler_params=_params("parallel", "parallel", "arbitrary"),
        )(*args)
    outs = pl.pallas_call(
        body, name=name, grid=(gm, gn, nk),
        in_specs=in_specs + [HBM_SPEC] * n_ci, out_specs=[o_spec] + [HBM_SPEC] * n_co,
        out_shape=[out_shape] + list(comm.outs),
        scratch_shapes=[pltpu.VMEM((tm, tn), F32)] + list(comm.sems),
        compiler_params=_params("arbitrary", "arbitrary", "arbitrary"),
    )(*args, *comm.ins)
    return outs[0], outs[1:]


def _pool_mm(a, w, mode, out_dtype, name):
    T = a.shape[0]
    G = POOL_GROUPS
    cg = a.shape[1] // G
    tm = _tile(T, 1024)
    nt = T // tm
    dims = _DIMS[mode]
    if mode == "tn":
        def body(a_ref, d_ref, o_ref):
            part = lax.dot_general(a_ref[...], d_ref[...], dims, preferred_element_type=F32)

            @pl.when(pl.program_id(1) == 0)
            def _():
                o_ref[...] = part

            @pl.when(pl.program_id(1) > 0)
            def _():
                o_ref[...] += part

        return pl.pallas_call(
            body, name=name, grid=(G, nt),
            in_specs=[pl.BlockSpec((tm, cg), lambda g, i: (i, g)), pl.BlockSpec((tm, cg), lambda g, i: (i, g))],
            out_specs=pl.BlockSpec((None, cg, cg), lambda g, i: (g, 0, 0)),
            out_shape=jax.ShapeDtypeStruct((G, cg, cg), F32),
            compiler_params=_params("parallel", "arbitrary"),
        )(a, w)

    def body(a_ref, w_ref, o_ref):
        o_ref[...] = lax.dot_general(a_ref[...], w_ref[...], dims, preferred_element_type=F32).astype(out_dtype)

    return pl.pallas_call(
        body, name=name, grid=(G, nt),
        in_specs=[pl.BlockSpec((tm, cg), lambda g, i: (i, g)), pl.BlockSpec((None, cg, cg), lambda g, i: (g, 0, 0))],
        out_specs=pl.BlockSpec((tm, cg), lambda g, i: (i, g)),
        out_shape=jax.ShapeDtypeStruct((T, G * cg), out_dtype),
        compiler_params=_params("parallel", "parallel"),
    )(a, w)


ROWS = 256
HALO_BLOCK = 16


def _row_spec(d, col=0, rows=ROWS):
    return pl.BlockSpec((rows, d), lambda i, col=col: (i, col))


def _const_spec(shape):
    return pl.BlockSpec(shape, lambda *_: (0,) * len(shape))


def _rms_fwd(x, g, name):
    T, D = x.shape

    def body(x_ref, g_ref, h_ref):
        xv = x_ref[...]
        r = lax.rsqrt(jnp.mean(xv * xv, axis=-1, keepdims=True) + EPS)
        h_ref[...] = (xv * r * g_ref[...]).astype(BF16)

    return pl.pallas_call(
        body, name=name, grid=(T // ROWS,),
        in_specs=[_row_spec(D), _const_spec((1, D))], out_specs=_row_spec(D),
        out_shape=jax.ShapeDtypeStruct((T, D), BF16), compiler_params=_params("parallel"),
    )(x, g)


def _accumulate(ref, part):
    first = pl.program_id(0) == 0

    @pl.when(first)
    def _():
        ref[...] = part

    @pl.when(jnp.logical_not(first))
    def _():
        ref[...] += part


def _rms_bwd(x, g, dh, dres, name):
    T, D = x.shape

    def body(x_ref, g_ref, dh_ref, dres_ref, dx_ref, dxb_ref, dg_ref):
        xv = x_ref[...]
        r = lax.rsqrt(jnp.mean(xv * xv, axis=-1, keepdims=True) + EPS)
        xhat = xv * r
        dh_v = dh_ref[...]
        dxhat = dh_v * g_ref[...]
        dx = dres_ref[...] + r * (dxhat - xhat * jnp.mean(dxhat * xhat, axis=-1, keepdims=True))
        dx_ref[...] = dx
        dxb_ref[...] = dx.astype(BF16)
        _accumulate(dg_ref, jnp.sum(dh_v * xhat, axis=0, keepdims=True))

    return pl.pallas_call(
        body, name=name, grid=(T // ROWS,),
        in_specs=[_row_spec(D), _const_spec((1, D)), _row_spec(D), _row_spec(D)],
        out_specs=[_row_spec(D), _row_spec(D), _const_spec((1, D))],
        out_shape=[jax.ShapeDtypeStruct((T, D), F32), jax.ShapeDtypeStruct((T, D), BF16),
                   jax.ShapeDtypeStruct((1, D), F32)],
        compiler_params=_params("arbitrary"),
    )(x, g, dh, dres)


def _loss_head(x, g, target, name):
    T, D = x.shape

    def body(x_ref, g_ref, t_ref, loss_ref, dx_ref, dxb_ref, dg_ref):
        xv = x_ref[...]
        gv = g_ref[...]
        r = lax.rsqrt(jnp.mean(xv * xv, axis=-1, keepdims=True) + EPS)
        xhat = xv * r
        err = xhat * gv - t_ref[...]
        loss = 0.5 * jnp.sum(jnp.mean(err * err, axis=-1, keepdims=True), axis=0, keepdims=True)
        dy = err * (1.0 / D)
        dxhat = dy * gv
        dx = r * (dxhat - xhat * jnp.mean(dxhat * xhat, axis=-1, keepdims=True))
        dx_ref[...] = dx
        dxb_ref[...] = dx.astype(BF16)
        _accumulate(loss_ref, loss)
        _accumulate(dg_ref, jnp.sum(dy * xhat, axis=0, keepdims=True))

    return pl.pallas_call(
        body, name=name, grid=(T // ROWS,),
        in_specs=[_row_spec(D), _const_spec((1, D)), _row_spec(D)],
        out_specs=[_const_spec((1, 1)), _row_spec(D), _row_spec(D), _const_spec((1, D))],
        out_shape=[jax.ShapeDtypeStruct((1, 1), F32), jax.ShapeDtypeStruct((T, D), F32),
                   jax.ShapeDtypeStruct((T, D), BF16), jax.ShapeDtypeStruct((1, D), F32)],
        compiler_params=_params("arbitrary"),
    )(x, g, target)


def _halo_specs(d, col, n_blocks):
    per = ROWS // HALO_BLOCK
    last = n_blocks * per - 1
    prev = pl.BlockSpec((HALO_BLOCK, d), lambda i, col=col: (jnp.maximum(i * per - 1, 0), col))
    nxt = pl.BlockSpec((HALO_BLOCK, d), lambda i, col=col: (jnp.minimum((i + 1) * per, last), col))
    return prev, nxt


def _with_halo(prev, cur, nxt, n_blocks):
    i = pl.program_id(0)
    prev = jnp.where(i > 0, prev[HALO_BLOCK - HALO:], 0.0)
    nxt = jnp.where(i < n_blocks - 1, nxt[:HALO], 0.0)
    return jnp.concatenate([prev, cur, nxt], axis=0)


def _f32(ref):
    return ref[...].astype(F32)


def _shift(ext, k):
    n = ext.shape[0]
    v = ext if k == 0 else pltpu.roll(ext, (-k) % n, 0)
    return v[HALO:HALO + ROWS]


def _shift_full(ext, k):
    n = ext.shape[0]
    return pltpu.roll(ext, (-k) % n, 0)


def _pool_counts(T):
    n = ROWS + 2 * HALO
    t = pl.program_id(0) * ROWS - HALO + lax.broadcasted_iota(jnp.int32, (n, 1), 0)
    out = []
    for w in POOL_WINDOWS:
        lo = jnp.maximum(t - w // 2, 0)
        hi = jnp.minimum(t + (w - 1 - w // 2), T - 1)
        out.append(jnp.maximum(hi - lo + 1, 1).astype(F32))
    return out


def _window_sums(e, sign):
    s2 = e + _shift_full(e, -sign)
    s4 = _shift_full(s2, -1) + _shift_full(s2, 1)
    s8 = _shift_full(s4, -2) + _shift_full(s4, 2)
    s16 = _shift_full(s8, -4) + _shift_full(s8, 4)
    return s2, s4, s8, s16


def _mixer_fwd(proj, conv_w, name):
    T = proj.shape[0]
    W = conv_w.shape[1]
    nb = T // ROWS
    cg = W // POOL_GROUPS

    def body(b_ref, c_ref, x_ref, u_ref, cp_ref, cn_ref, xp_ref, xn_ref, up_ref, un_ref, w_ref, z_ref, p_ref):
        uc = _with_halo(_f32(cp_ref) * _f32(xp_ref), _f32(c_ref) * _f32(x_ref), _f32(cn_ref) * _f32(xn_ref), nb)
        w0, w1, w2 = w_ref[0:1, :], w_ref[1:2, :], w_ref[2:3, :]
        y = w0 * _shift(uc, -1) + w1 * _shift(uc, 0) + w2 * _shift(uc, 1)
        z_ref[...] = (_f32(b_ref) * y).astype(BF16)
        e = _with_halo(_f32(up_ref), _f32(u_ref), _f32(un_ref), nb)
        counts = _pool_counts(T)
        for gi in range(POOL_GROUPS):
            eg = e[:, gi * cg:(gi + 1) * cg]
            s = _window_sums(eg, 1)[gi]
            p = s[HALO:HALO + ROWS] / counts[gi][HALO:HALO + ROWS] - eg[HALO:HALO + ROWS]
            p_ref[:, gi * cg:(gi + 1) * cg] = p.astype(BF16)

    halo = [s for col in (1, 2, 3) for s in _halo_specs(W, col, nb)]
    return pl.pallas_call(
        body, name=name, grid=(nb,),
        in_specs=[_row_spec(W, 0), _row_spec(W, 1), _row_spec(W, 2), _row_spec(W, 3)] + halo + [_const_spec((8, W))],
        out_specs=[_row_spec(W), _row_spec(W)],
        out_shape=[jax.ShapeDtypeStruct((T, W), BF16), jax.ShapeDtypeStruct((T, W), BF16)],
        compiler_params=_params("parallel"),
    )(proj, proj, proj, proj, proj, proj, proj, proj, proj, proj, conv_w)


def _mixer_bwd(proj, conv_w, dz, dp, name):
    T = proj.shape[0]
    W = conv_w.shape[1]
    nb = T // ROWS
    cg = W // POOL_GROUPS

    def body(b_ref, c_ref, x_ref, dz_ref, dp_ref,
             bp_ref, bn_ref, cp_ref, cn_ref, xp_ref, xn_ref, dzp_ref, dzn_ref, dpp_ref, dpn_ref, w_ref,
             o_ref, dw_ref):
        cv, xv, dzv = _f32(c_ref), _f32(x_ref), _f32(dz_ref)
        uc = _with_halo(_f32(cp_ref) * _f32(xp_ref), cv * xv, _f32(cn_ref) * _f32(xn_ref), nb)
        dy = _with_halo(_f32(dzp_ref) * _f32(bp_ref), dzv * _f32(b_ref), _f32(dzn_ref) * _f32(bn_ref), nb)
        w0, w1, w2 = w_ref[0:1, :], w_ref[1:2, :], w_ref[2:3, :]
        um, u0, up = _shift(uc, -1), _shift(uc, 0), _shift(uc, 1)
        o_ref[:, 0:W] = (dzv * (w0 * um + w1 * u0 + w2 * up)).astype(BF16)
        dy0 = _shift(dy, 0)
        duc = w0 * _shift(dy, 1) + w1 * dy0 + w2 * _shift(dy, -1)
        o_ref[:, W:2 * W] = (duc * xv).astype(BF16)
        o_ref[:, 2 * W:3 * W] = (duc * cv).astype(BF16)
        row = lax.broadcasted_iota(jnp.int32, (8, W), 0)
        dw = jnp.where(row == 0, jnp.sum(dy0 * um, axis=0, keepdims=True),
                       jnp.where(row == 1, jnp.sum(dy0 * u0, axis=0, keepdims=True),
                                 jnp.where(row == 2, jnp.sum(dy0 * up, axis=0, keepdims=True), 0.0)))
        _accumulate(dw_ref, dw)
        d = _with_halo(_f32(dpp_ref), _f32(dp_ref), _f32(dpn_ref), nb)
        counts = _pool_counts(T)
        for gi in range(POOL_GROUPS):
            dg = d[:, gi * cg:(gi + 1) * cg]
            s = _window_sums(dg / counts[gi], -1)[gi]
            o_ref[:, 3 * W + gi * cg:3 * W + (gi + 1) * cg] = (s[HALO:HALO + ROWS] - dg[HALO:HALO + ROWS]).astype(BF16)

    def halo(col):
        return list(_halo_specs(W, col, nb))

    return pl.pallas_call(
        body, name=name, grid=(nb,),
        in_specs=[_row_spec(W, 0), _row_spec(W, 1), _row_spec(W, 2), _row_spec(W), _row_spec(W)]
        + halo(0) + halo(1) + halo(2) + halo(0) + halo(0) + [_const_spec((8, W))],
        out_specs=[_row_spec(4 * W), _const_spec((8, W))],
        out_shape=[jax.ShapeDtypeStruct((T, 4 * W), BF16), jax.ShapeDtypeStruct((8, W), F32)],
        compiler_params=_params("arbitrary"),
    )(proj, proj, proj, dz, dp, proj, proj, proj, proj, proj, proj, dz, dz, dp, dp, conv_w)


def _t5_bucket(rel):
    half = N_BUCKETS // 2
    max_exact = half // 2
    ret = jnp.where(rel > 0, half, 0)
    n = jnp.abs(rel)
    nf = jnp.maximum(n, 1).astype(jnp.float32)
    large = max_exact + (jnp.log(nf / max_exact) / math.log(MAX_DISTANCE / max_exact)
                         * (half - max_exact)).astype(jnp.int32)
    large = jnp.minimum(large, half - 1)
    return ret + jnp.where(n < max_exact, n, large)


def _bucket_table():
    qi = jnp.arange(BLOCK)[:, None]
    kj = jnp.arange(3 * BLOCK)[None, :]
    rel = kj - BLOCK - qi
    return jnp.where(jnp.abs(rel) <= WINDOW, _t5_bucket(rel), -1).astype(jnp.int32)


def _bias_table(rel_bias, bucket, name):
    def body(rb_ref, bucket_ref, o_ref):
        h = pl.program_id(0)
        bk = bucket_ref[...]
        acc = jnp.full(bk.shape, NEG_INF, F32)
        for b in range(N_BUCKETS):
            acc = jnp.where(bk == b, rb_ref[b, h], acc)
        o_ref[...] = acc

    return pl.pallas_call(
        body, name=name, grid=(N_HEADS,),
        in_specs=[pl.BlockSpec(memory_space=pltpu.SMEM), _const_spec((BLOCK, 3 * BLOCK))],
        out_specs=pl.BlockSpec((None, BLOCK, 3 * BLOCK), lambda h: (h, 0, 0)),
        out_shape=jax.ShapeDtypeStruct((N_HEADS, BLOCK, 3 * BLOCK), F32),
        compiler_params=_params("parallel"),
    )(rel_bias, bucket)


def _bias_grad(ds_sum, bucket, name):
    def body(ds_ref, bucket_ref, o_ref):
        bk = bucket_ref[...]
        ds = ds_ref[...]
        row = lax.broadcasted_iota(jnp.int32, (N_BUCKETS, 128), 0)
        acc = jnp.zeros((N_BUCKETS, 128), F32)
        for b in range(N_BUCKETS):
            s = jnp.sum(jnp.sum(jnp.where(bk == b, ds, 0.0), axis=1, keepdims=True), axis=0, keepdims=True)
            acc = jnp.where(row == b, s, acc)
        o_ref[...] = acc

    return pl.pallas_call(
        body, name=name, grid=(N_HEADS,),
        in_specs=[pl.BlockSpec((None, BLOCK, 3 * BLOCK), lambda h: (h, 0, 0)), _const_spec((BLOCK, 3 * BLOCK))],
        out_specs=pl.BlockSpec((None, N_BUCKETS, 128), lambda h: (h, 0, 0)),
        out_shape=jax.ShapeDtypeStruct((N_HEADS, N_BUCKETS, 128), F32),
        compiler_params=_params("parallel"),
    )(ds_sum, bucket)


PAIR = 2 * HEAD_DIM


def _low_half(shape):
    return lax.broadcasted_iota(jnp.int32, shape, len(shape) - 1) % PAIR < HEAD_DIM


def _split_pair(a):
    low = _low_half(a.shape)
    zero = jnp.zeros_like(a)
    return jnp.concatenate([jnp.where(low, a, zero), jnp.where(low, zero, a)], axis=0)


def _kv_expand(proj, kv_off, name):
    T = proj.shape[0]
    kv_w = N_KV_HEADS * HEAD_DIM
    rows = _tile(T, 512)

    def body(k_ref, v_ref, ke_ref, ve_ref):
        for src, dst in ((k_ref, ke_ref), (v_ref, ve_ref)):
            for g in range(N_KV_HEADS // 2):
                x = src[:, g * PAIR:(g + 1) * PAIR].astype(F32)
                swapped = pltpu.roll(x, HEAD_DIM, 1)
                low = _low_half(x.shape)
                dst[:, 2 * g * PAIR:(2 * g + 1) * PAIR] = jnp.where(low, x, swapped).astype(BF16)
                dst[:, (2 * g + 1) * PAIR:(2 * g + 2) * PAIR] = jnp.where(low, swapped, x).astype(BF16)

    out = jax.ShapeDtypeStruct((T, N_KV_HEADS * PAIR), BF16)
    ospec = pl.BlockSpec((rows, N_KV_HEADS * PAIR), lambda i: (i, 0))
    return pl.pallas_call(
        body, name=name, grid=(T // rows,),
        in_specs=[pl.BlockSpec((rows, kv_w), lambda i: (i, kv_off // kv_w)),
                  pl.BlockSpec((rows, kv_w), lambda i: (i, kv_off // kv_w + 1))],
        out_specs=[ospec, ospec], out_shape=[out, out], compiler_params=_params("parallel"),
    )(proj, proj)


def _kv_fold(dke, dve, name):
    T = dke.shape[0]
    kv_w = N_KV_HEADS * HEAD_DIM
    rows = _tile(T, 512)

    def body(dk_ref, dv_ref, o_ref):
        for n, src in enumerate((dk_ref, dv_ref)):
            for g in range(N_KV_HEADS // 2):
                a = src[:, 2 * g * PAIR:(2 * g + 1) * PAIR]
                b = src[:, (2 * g + 1) * PAIR:(2 * g + 2) * PAIR]
                a = a + pltpu.roll(a, HEAD_DIM, 1)
                b = b + pltpu.roll(b, HEAD_DIM, 1)
                o_ref[:, n * kv_w + g * PAIR:n * kv_w + (g + 1) * PAIR] = jnp.where(_low_half(a.shape), a, b).astype(BF16)

    ispec = pl.BlockSpec((rows, N_KV_HEADS * PAIR), lambda i: (i, 0))
    return pl.pallas_call(
        body, name=name, grid=(T // rows,), in_specs=[ispec, ispec],
        out_specs=pl.BlockSpec((rows, 2 * kv_w), lambda i: (i, 0)),
        out_shape=jax.ShapeDtypeStruct((T, 2 * kv_w), BF16), compiler_params=_params("parallel"),
    )(dke, dve)


def _key_blocks(i, nb):
    return [pl.multiple_of(n * BLOCK, BLOCK) for n in (jnp.maximum(i - 1, 0), i, jnp.minimum(i + 1, nb - 1))]


def _three_blocks(ref, starts):
    return jnp.concatenate([ref[pl.ds(s, BLOCK), :] for s in starts], axis=0)


def _pair_scores(q2, kd, bias_ref, pr, i, nb):
    qq = _split_pair(q2)
    s = lax.dot_general(qq, kd, _DIMS["nt"], preferred_element_type=F32) * (HEAD_DIM ** -0.5)
    s = s + bias_ref[2 * pr:2 * pr + 2].reshape(2 * BLOCK, 3 * BLOCK)
    kj = lax.broadcasted_iota(jnp.int32, (1, 3 * BLOCK), 1)
    outside = jnp.logical_or(jnp.logical_and(i == 0, kj < BLOCK), jnp.logical_and(i == nb - 1, kj >= 2 * BLOCK))
    return qq, jnp.where(outside, NEG_INF, s)


def _attn_specs(T, q_off):
    gw = GROUP * HEAD_DIM
    return dict(
        sink=pl.BlockSpec(memory_space=pltpu.SMEM),
        q=pl.BlockSpec((BLOCK, gw), lambda j, i: (i, q_off // gw + j)),
        kv=pl.BlockSpec((T, PAIR), lambda j, i: (0, j)),
        bias=pl.BlockSpec((GROUP, BLOCK, 3 * BLOCK), lambda j, i: (j, 0, 0)),
        o=pl.BlockSpec((BLOCK, gw), lambda j, i: (i, j)))


def _attn_fwd(proj, q_off, kexp, vexp, bias, sink, name):
    T = proj.shape[0]
    nb = T // BLOCK
    sp = _attn_specs(T, q_off)

    def body(sink_ref, q_ref, ke_ref, ve_ref, bias_ref, o_ref, lse_ref):
        j, i = pl.program_id(0), pl.program_id(1)
        starts = _key_blocks(i, nb)
        kd = _three_blocks(ke_ref, starts)
        vv = _split_pair(_three_blocks(ve_ref, starts))
        first_rows = lax.broadcasted_iota(jnp.int32, (2 * BLOCK, 1), 0) < BLOCK
        low = _low_half((BLOCK, PAIR))
        for pr in range(GROUP // 2):
            lanes = slice(pr * PAIR, (pr + 1) * PAIR)
            _, s = _pair_scores(q_ref[:, lanes], kd, bias_ref, pr, i, nb)
            head = GROUP * j + 2 * pr
            sk = jnp.where(first_rows, sink_ref[head], sink_ref[head + 1])
            m = jnp.maximum(jnp.max(s, axis=-1, keepdims=True), sk)
            p = jnp.exp(s - m)
            denom = jnp.sum(p, axis=-1, keepdims=True) + jnp.exp(sk - m)
            p = (p / denom).astype(BF16)
            pp = jnp.concatenate([p[:BLOCK], p[BLOCK:]], axis=1)
            o_ref[:, lanes] = lax.dot_general(pp, vv, _DIMS["nn"], preferred_element_type=F32).astype(BF16)
            lse = m + jnp.log(denom)
            lse_ref[:, lanes] = jnp.where(low, lse[:BLOCK], lse[BLOCK:])

    return pl.pallas_call(
        body, name=name, grid=(N_KV_HEADS, nb),
        in_specs=[sp["sink"], sp["q"], sp["kv"], sp["kv"], sp["bias"]], out_specs=[sp["o"], sp["o"]],
        out_shape=[jax.ShapeDtypeStruct((T, N_HEADS * HEAD_DIM), BF16), jax.ShapeDtypeStruct((T, N_HEADS * HEAD_DIM), F32)],
        compiler_params=_params("parallel", "parallel"),
    )(sink, proj, kexp, vexp, bias)


def _attn_bwd(proj, q_off, kexp, vexp, bias, sink, out, lse, dout, name):
    T = proj.shape[0]
    nb = T // BLOCK
    sp = _attn_specs(T, q_off)
    scale = HEAD_DIM ** -0.5

    def body(sink_ref, q_ref, ke_ref, ve_ref, bias_ref, o_ref, lse_ref, do_ref,
             dq_ref, dke_ref, dve_ref, ds_ref, dsink_ref):
        j, i = pl.program_id(0), pl.program_id(1)

        @pl.when(i == 0)
        def _():
            dke_ref[...] = jnp.zeros(dke_ref.shape, F32)
            dve_ref[...] = jnp.zeros(dve_ref.shape, F32)
            ds_ref[...] = jnp.zeros(ds_ref.shape, F32)
            dsink_ref[...] = jnp.zeros(dsink_ref.shape, F32)

        starts = _key_blocks(i, nb)
        kd = _three_blocks(ke_ref, starts)
        vd = _three_blocks(ve_ref, starts)
        kk = _split_pair(kd)
        low = _low_half((BLOCK, PAIR))
        dk_acc = jnp.zeros((3 * BLOCK, PAIR), F32)
        dv_acc = jnp.zeros((3 * BLOCK, PAIR), F32)
        for pr in range(GROUP // 2):
            lanes = slice(pr * PAIR, (pr + 1) * PAIR)
            qq, s = _pair_scores(q_ref[:, lanes], kd, bias_ref, pr, i, nb)
            l2 = lse_ref[:, lanes]
            l2s = pltpu.roll(l2, HEAD_DIM, 1)
            lse_a, lse_b = jnp.where(low, l2, l2s), jnp.where(low, l2s, l2)
            p = jnp.exp(s - jnp.concatenate([jnp.concatenate([lse_a] * 3, axis=1),
                                             jnp.concatenate([lse_b] * 3, axis=1)], axis=0))
            do2 = do_ref[:, lanes]
            prod = do2.astype(F32) * o_ref[:, lanes].astype(F32)
            delta_a = jnp.sum(jnp.where(low, prod, 0.0), axis=-1, keepdims=True)
            delta_b = jnp.sum(jnp.where(low, 0.0, prod), axis=-1, keepdims=True)
            dd = _split_pair(do2)
            dp = lax.dot_general(dd, vd, _DIMS["nt"], preferred_element_type=F32)
            ds = p * (dp - jnp.concatenate([delta_a, delta_b], axis=0))
            dsb = ds.astype(BF16)
            dq = lax.dot_general(jnp.concatenate([dsb[:BLOCK], dsb[BLOCK:]], axis=1), kk, _DIMS["nn"],
                                 preferred_element_type=F32) * scale
            dq_ref[:, lanes] = dq.astype(BF16)
            dk_acc += lax.dot_general(dsb, qq, _DIMS["tn"], preferred_element_type=F32) * scale
            dv_acc += lax.dot_general(p.astype(BF16), dd, _DIMS["tn"], preferred_element_type=F32)
            ds_ref[2 * pr:2 * pr + 2] += ds.reshape(2, BLOCK, 3 * BLOCK)
            head = GROUP * j + 2 * pr
            p_sink = jnp.exp(jnp.where(low, sink_ref[head], sink_ref[head + 1]) - l2)
            dsink_ref[:, lanes] += jnp.sum(-p_sink * jnp.where(low, delta_a, delta_b), axis=0, keepdims=True)
        for t, start in enumerate(starts):
            dke_ref[pl.ds(start, BLOCK), :] += dk_acc[t * BLOCK:(t + 1) * BLOCK]
            dve_ref[pl.ds(start, BLOCK), :] += dv_acc[t * BLOCK:(t + 1) * BLOCK]

    kv_out = jax.ShapeDtypeStruct((T, N_KV_HEADS * PAIR), F32)
    return pl.pallas_call(
        body, name=name, grid=(N_KV_HEADS, nb),
        in_specs=[sp["sink"], sp["q"], sp["kv"], sp["kv"], sp["bias"], sp["o"], sp["o"], sp["o"]],
        out_specs=[sp["o"], sp["kv"], sp["kv"], sp["bias"],
                   pl.BlockSpec((1, GROUP * HEAD_DIM), lambda j, i: (0, j))],
        out_shape=[jax.ShapeDtypeStruct((T, N_HEADS * HEAD_DIM), BF16), kv_out, kv_out,
                   jax.ShapeDtypeStruct((N_HEADS, BLOCK, 3 * BLOCK), F32),
                   jax.ShapeDtypeStruct((1, N_HEADS * HEAD_DIM), F32)],
        compiler_params=_params("parallel", "arbitrary"),
    )(sink, proj, kexp, vexp, bias, out, lse, dout)


GATE_COLS = 512


def _sigmoid(x):
    return 1.0 / (1.0 + jnp.exp(-x))


def _gate_specs(D, gate_off):
    nc = D // GATE_COLS
    base = gate_off // GATE_COLS
    return [pl.BlockSpec((ROWS, GATE_COLS), lambda i, c=base + g * nc + h: (i, c)) for g in range(3) for h in range(nc)]


def _merge_fwd(proj, gate_off, ya, yp, yt, scale, name):
    T, D = ya.shape
    nc = D // GATE_COLS

    def body(*refs):
        gates = refs[:3 * nc]
        ya_ref, yp_ref, yt_ref, s_ref, o_ref = refs[3 * nc:]
        for h in range(nc):
            cols = slice(h * GATE_COLS, (h + 1) * GATE_COLS)
            merged = (_sigmoid(_f32(gates[h])) * ya_ref[:, cols].astype(F32)
                      + _sigmoid(_f32(gates[nc + h])) * (yp_ref[:, cols].astype(F32) * s_ref[:, cols])
                      + _sigmoid(_f32(gates[2 * nc + h])) * yt_ref[:, cols].astype(F32))
            o_ref[:, cols] = merged.astype(BF16)

    yspec = _row_spec(D)
    return pl.pallas_call(
        body, name=name, grid=(T // ROWS,),
        in_specs=_gate_specs(D, gate_off) + [yspec, yspec, yspec, _const_spec((1, D))], out_specs=yspec,
        out_shape=jax.ShapeDtypeStruct((T, D), BF16), compiler_params=_params("parallel"),
    )(*([proj] * (3 * nc)), ya, yp, yt, scale)


def _merge_bwd(proj, gate_off, ya, yp, yt, scale, dm, name):
    T, D = ya.shape
    nc = D // GATE_COLS

    def body(*refs):
        gates = refs[:3 * nc]
        ya_ref, yp_ref, yt_ref, s_ref, dm_ref, dg_ref, dya_ref, dyp_ref, dyt_ref, ds_ref = refs[3 * nc:]
        parts = []
        for h in range(nc):
            cols = slice(h * GATE_COLS, (h + 1) * GATE_COLS)
            dm_v = dm_ref[:, cols].astype(F32)
            sa, sp, st = _sigmoid(_f32(gates[h])), _sigmoid(_f32(gates[nc + h])), _sigmoid(_f32(gates[2 * nc + h]))
            yp_v, s_v = yp_ref[:, cols].astype(F32), s_ref[:, cols]
            dg_ref[:, h * GATE_COLS:(h + 1) * GATE_COLS] = (dm_v * ya_ref[:, cols].astype(F32) * sa * (1.0 - sa)).astype(BF16)
            dg_ref[:, D + h * GATE_COLS:D + (h + 1) * GATE_COLS] = (dm_v * (yp_v * s_v) * sp * (1.0 - sp)).astype(BF16)
            dg_ref[:, 2 * D + h * GATE_COLS:2 * D + (h + 1) * GATE_COLS] = (
                dm_v * yt_ref[:, cols].astype(F32) * st * (1.0 - st)).astype(BF16)
            dya_ref[:, cols] = (dm_v * sa).astype(BF16)
            dyps = dm_v * sp
            dyp_ref[:, cols] = (dyps * s_v).astype(BF16)
            dyt_ref[:, cols] = (dm_v * st).astype(BF16)
            parts.append(jnp.sum(dyps * yp_v, axis=0, keepdims=True))
        _accumulate(ds_ref, jnp.concatenate(parts, axis=1))

    yspec = _row_spec(D)
    out = jax.ShapeDtypeStruct((T, D), BF16)
    return pl.pallas_call(
        body, name=name, grid=(T // ROWS,),
        in_specs=_gate_specs(D, gate_off) + [yspec, yspec, yspec, _const_spec((1, D)), yspec],
        out_specs=[_row_spec(3 * D), yspec, yspec, yspec, _const_spec((1, D))],
        out_shape=[jax.ShapeDtypeStruct((T, 3 * D), BF16), out, out, out, jax.ShapeDtypeStruct((1, D), F32)],
        compiler_params=_params("arbitrary"),
    )(*([proj] * (3 * nc)), ya, yp, yt, scale, dm)


def _swiglu_fwd(gu, name):
    T = gu.shape[0]
    F = gu.shape[1] // 2

    def body(gu_ref, o_ref):
        g = gu_ref[:, 0:F].astype(F32)
        o_ref[...] = (g * _sigmoid(g) * gu_ref[:, F:2 * F].astype(F32)).astype(BF16)

    return pl.pallas_call(
        body, name=name, grid=(T // ROWS,), in_specs=[_row_spec(2 * F)], out_specs=_row_spec(F),
        out_shape=jax.ShapeDtypeStruct((T, F), BF16), compiler_params=_params("parallel"),
    )(gu)


def _swiglu_bwd(gu, dact, name):
    T = gu.shape[0]
    F = gu.shape[1] // 2

    def body(gu_ref, d_ref, o_ref):
        g, d = gu_ref[:, 0:F].astype(F32), d_ref[...].astype(F32)
        sg = _sigmoid(g)
        o_ref[:, 0:F] = (d * gu_ref[:, F:2 * F].astype(F32) * sg * (1.0 + g * (1.0 - sg))).astype(BF16)
        o_ref[:, F:2 * F] = (d * g * sg).astype(BF16)

    return pl.pallas_call(
        body, name=name, grid=(T // ROWS,), in_specs=[_row_spec(2 * F), _row_spec(F)], out_specs=_row_spec(2 * F),
        out_shape=jax.ShapeDtypeStruct((T, 2 * F), BF16), compiler_params=_params("parallel"),
    )(gu, dact)


def _carried(plan, key, *args, **kwargs):
    job = plan.job(key) if plan is not None else None
    if job is None:
        return _matmul(*args, **kwargs)
    out, extra = _matmul(*args, comm=job, **kwargs)
    plan.done(key, extra)
    return out


def _local_step(x, target, wts, small, hooks=None):
    T, D = x.shape
    depth = small["g_mix"].shape[0]
    wts = list(wts) + [None] * (depth - len(wts))
    gate_off = wts[0]["w_inT"].shape[0] - 3 * D
    q_off = 4 * D
    ff2 = wts[0]["w_guT"].shape[0]
    bucket = _bucket_table()
    bias = _bias_table(small["rel_bias"], bucket, "bias_table")

    saved = []
    for l in range(depth):
        n = f"l{l}_"
        if hooks is not None and l > 0:
            wts[l] = hooks.weights(l)
        w = wts[l]
        plan = hooks.plan_fwd(l) if hooks is not None else None
        h = _rms_fwd(x, small["g_mix"][l], n + "rms_mix")
        proj = _carried(plan, "proj", h, w["w_inT"], "nt", BF16, n + "proj")
        z, p = _mixer_fwd(proj, small["conv_w"][l], n + "mixer")
        kexp, vexp = _kv_expand(proj, q_off + D, n + "kv_expand")
        sink = small["attn_sink"][l]
        att, lse = _attn_fwd(proj, q_off, kexp, vexp, bias, sink, n + "attn")
        ya =_matmul(z, w["w_a_out"], "nn", BF16, n + "ya")
        yp = _pool_mm(p, w["w_pool"], "nn", BF16, n + "yp")
        yt = _matmul(att, w["w_attn_out"], "nn", BF16, n + "yt")
        merged = _merge_fwd(proj, gate_off, ya, yp, yt, small["pool_scale"][l], n + "merge")
        x1 = _matmul(merged, w["w_o"], "nn", F32, n + "x1", res=x)
        h2 = _rms_fwd(x1, small["g_ffn"][l], n + "rms_ffn")
        gu = _carried(plan, "gu", h2, w["w_guT"], "nt", BF16, n + "gu")
        act = _swiglu_fwd(gu, n + "swiglu")
        x2 = _carried(plan, "x2", act, w["w_down"], "nn", F32, n + "x2", res=x1, tn_cap=512, tk_cap=ff2 // 2)
        saved.append(dict(x=x, h=h, proj=proj, z=z, p=p, kexp=kexp, vexp=vexp, sink=sink, lse=lse, att=att,
                          ya=ya, yp=yp, yt=yt, merged=merged, x1=x1, h2=h2, gu=gu, act=act))
        x = x2

    loss, dx, dxb, dg_final = _loss_head(x, small["g_final"], target, "loss_head")

    gw = [None] * depth
    gs = {k_: [None] * depth for k_ in ("conv_w", "pool_scale", "g_mix", "g_ffn", "attn_sink")}
    ds_total = None
    for l in reversed(range(depth)):
        n = f"l{l}_b_"
        s, w, g = saved[l], wts[l], {}
        plan = hooks.plan_bwd(l) if hooks is not None else None
        g["w_down"] = _carried(plan, "dw_down", s["act"], dxb, "tn", BF16, n + "dw_down", tm_cap=ff2 // 2, tk_cap=512)
        dact = _matmul(dxb, w["w_down"], "nt", BF16, n + "dact", tm_cap=512, tn_cap=ff2 // 2)
        dgu = _swiglu_bwd(s["gu"], dact, n + "swiglu")
        g["w_guT"] = _carried(plan, "dw_gu", dgu, s["h2"], "tn", BF16, n + "dw_gu", tm_cap=512)
        dh2 = _carried(plan, "dh2", dgu, w["w_guT"], "nn", F32, n + "dh2", tn_cap=512, tk_cap=ff2 // 2)
        dx1, dx1b, gs["g_ffn"][l] = _rms_bwd(s["x1"], small["g_ffn"][l], dh2, dx, n + "rms_ffn")
        g["w_o"] = _matmul(s["merged"], dx1b, "tn", BF16, n + "dw_o")
        dm = _matmul(dx1b, w["w_o"], "nt", BF16, n + "dmerged")
        dgates, dya, dyp, dyt, gs["pool_scale"][l] = _merge_bwd(
            s["proj"], gate_off, s["ya"], s["yp"], s["yt"], small["pool_scale"][l], dm, n + "merge")
        g["w_a_out"] = _matmul(s["z"], dya, "tn", BF16, n + "dw_a_out")
        dz = _matmul(dya, w["w_a_out"], "nt", BF16, n + "dz")
        g["w_pool"] = _pool_mm(s["p"], dyp, "tn", F32, n + "dw_pool")
        dp = _pool_mm(dyp, w["w_pool"], "nt", BF16, n + "dp")
        g["w_attn_out"] = _matmul(s["att"], dyt, "tn", BF16, n + "dw_attn_out")
        datt = _matmul(dyt, w["w_attn_out"], "nt", BF16, n + "datt")
        dmix, gs["conv_w"][l] = _mixer_bwd(s["proj"], small["conv_w"][l], dz, dp, n + "mixer")
        dq, dke, dve, ds_sum, dsink = _attn_bwd(s["proj"], q_off, s["kexp"], s["vexp"], bias, s["sink"], s["att"],
                                                s["lse"], datt, n + "attn")
        gs["attn_sink"][l] = dsink.reshape(N_HEADS, HEAD_DIM)[:, 0]
        ds_total = ds_sum if ds_total is None else ds_total + ds_sum
        dproj = jnp.concatenate([dmix, dq, _kv_fold(dke, dve, n + "kv_fold"), dgates], axis=1)
        g["w_inT"] = _carried(plan, "dw_in", dproj, s["h"], "tn", BF16, n + "dw_in", tm_cap=512)
        dh = _matmul(dproj, w["w_inT"], "nn", F32, n + "dh", tk_cap=2816)
        dx, dxb, gs["g_mix"][l] = _rms_bwd(s["x"], small["g_mix"][l], dh, dx1, n + "rms_mix")
        gw[l] = g
        if hooks is not None:
            hooks.grads(l, g)

    d_rel =_bias_grad(ds_total, bucket, "bias_grad")[:, :, 0].T
    gs = {k_: jnp.stack(v_) for k_, v_ in gs.items()}
    gs["rel_bias"] = d_rel
    gs["g_final"] = dg_final
    return loss, dx, gw, gs


def _place():
    return lax.axis_index("x"), lax.axis_index("y"), lax.axis_index("c")


class _GatherJob:
    def __init__(self, parts):
        n = len(parts)
        self.n = n
        self.ins = list(parts)
        self.outs = [jax.ShapeDtypeStruct((N_DEV,) + p.shape, p.dtype) for p in parts]
        self.sems = [pltpu.SemaphoreType.DMA((7 * n,)), pltpu.SemaphoreType.DMA((7 * n,)), pltpu.SemaphoreType.DMA((n,))]

    def _copies(self, ins, outs, sems):
        send_sems, recv_sems, local_sems = sems
        x, y, c = _place()
        me, sibling = (x, y, c), (x, y, 1 - c)
        chips = [(1 - x, y), (x, 1 - y), (1 - x, 1 - y)]

        def rows(t, px, py, pc):
            return outs[t].at[4 * px + 2 * py + pc]

        def copy(t, k, block, to, src=None):
            return pltpu.make_async_remote_copy(
                src_ref=rows(t, *block) if src is None else src, dst_ref=rows(t, *block),
                send_sem=send_sems.at[7 * t + k], recv_sem=recv_sems.at[7 * t + k], device_id=to, device_id_type=MESH)

        ts = range(self.n)
        own = [pltpu.make_async_copy(ins[t], rows(t, *me), local_sems.at[t]) for t in ts]
        first = [copy(t, 0, me, sibling, src=ins[t]) for t in ts]
        first += [copy(t, 1 + j, me, (*chip, c), src=ins[t]) for t in ts for j, chip in enumerate(chips)]
        landed = [copy(t, 1 + j, (*chip, c), me) for j, chip in enumerate(chips) for t in ts]
        passed = [copy(t, 4 + j, (*chip, c), sibling) for j, chip in enumerate(chips) for t in ts]
        last = [copy(t, 0, sibling, me) for t in ts]
        last += [copy(t, 4 + j, (*chip, 1 - c), me) for t in ts for j, chip in enumerate(chips)]
        return own, first, landed, passed, last

    def start(self, ins, outs, sems):
        own, first, _, _, _ = self._copies(ins, outs, sems)
        for cp in own + first:
            cp.start()

    def mid(self, ins, outs, sems):
        _, _, landed, passed, _ = self._copies(ins, outs, sems)
        for arrived, onward in zip(landed, passed):
            arrived.wait_recv()
            onward.start()

    def finish(self, ins, outs, sems):
        own, first, _, passed, last = self._copies(ins, outs, sems)
        for cp in last:
            cp.wait_recv()
        for cp in first + passed:
            cp.wait_send()
        for cp in own:
            cp.wait()


class _SwapJob:
    def __init__(self, g):
        self.ins = [g]
        self.outs = [jax.ShapeDtypeStruct(g.shape[1:], g.dtype)]
        self.sems = [pltpu.SemaphoreType.DMA, pltpu.SemaphoreType.DMA]

    def _copy(self, ins, outs, sems):
        x, y, c = _place()
        return pltpu.make_async_remote_copy(src_ref=ins[0].at[1 - c], dst_ref=outs[0], send_sem=sems[0],
                                            recv_sem=sems[1], device_id=(x, y, 1 - c), device_id_type=MESH)

    def start(self, ins, outs, sems):
        self._copy(ins, outs, sems).start()

    def mid(self, ins, outs, sems):
        pass

    def finish(self, ins, outs, sems):
        self._copy(ins, outs, sems).wait()


class _ExchangeJob:
    def __init__(self, p, row0, rows):
        self.row0, self.rows = row0, rows
        self.ins = [p]
        self.outs = [jax.ShapeDtypeStruct((3, rows) + p.shape[2:], p.dtype)]
        self.sems = [pltpu.SemaphoreType.DMA((3,)), pltpu.SemaphoreType.DMA((3,))]

    def _copies(self, ins, outs, sems):
        x, y, c = _place()
        chips = [(1 - x, y), (x, 1 - y), (1 - x, 1 - y)]
        return [pltpu.make_async_remote_copy(
            src_ref=ins[0].at[2 * px + py, pl.ds(self.row0, self.rows)], dst_ref=outs[0].at[k],
            send_sem=sems[0].at[k], recv_sem=sems[1].at[k], device_id=(px, py, c), device_id_type=MESH)
            for k, (px, py) in enumerate(chips)]

    def start(self, ins, outs, sems):
        for cp in self._copies(ins, outs, sems):
            cp.start()

    def mid(self, ins, outs, sems):
        pass

    def finish(self, ins, outs, sems):
        for cp in self._copies(ins, outs, sems):
            cp.wait()


def _all_gather(v, name):
    def body(x_ref, out_ref, send_sems, recv_sems, local_sem):
        x, y, c = _place()
        me, sibling = (x, y, c), (x, y, 1 - c)
        chips = [(1 - x, y), (x, 1 - y), (1 - x, 1 - y)]

        def rows(px, py, pc):
            return out_ref.at[4 * px + 2 * py + pc]

        def copy(k, block, to, src=None):
            return pltpu.make_async_remote_copy(
                src_ref=rows(*block) if src is None else src, dst_ref=rows(*block),
                send_sem=send_sems.at[k], recv_sem=recv_sems.at[k], device_id=to, device_id_type=MESH)

        mine = pltpu.make_async_copy(x_ref, rows(*me), local_sem)
        mine.start()
        first = [copy(0, me, sibling, src=x_ref)]
        first += [copy(1 + j, me, (*chip, c), src=x_ref) for j, chip in enumerate(chips)]
        for cp in first:
            cp.start()
        passed = [copy(4 + j, (*chip, c), sibling) for j, chip in enumerate(chips)]
        for j, chip in enumerate(chips):
            copy(1 + j, (*chip, c), me).wait_recv()
            passed[j].start()
        copy(0, sibling, me).wait_recv()
        for j, chip in enumerate(chips):
            copy(4 + j, (*chip, 1 - c), me).wait_recv()
        for cp in first + passed:
            cp.wait_send()
        mine.wait()

    return pl.pallas_call(
        body, name=name, in_specs=[HBM_SPEC], out_specs=HBM_SPEC,
        out_shape=jax.ShapeDtypeStruct((N_DEV,) + v.shape, v.dtype),
        scratch_shapes=[pltpu.SemaphoreType.DMA((7,)), pltpu.SemaphoreType.DMA((7,)), pltpu.SemaphoreType.DMA],
    )(v)


def _all_gather_many(parts, name):
    n = len(parts)

    def body(*refs):
        ins, outs = refs[:n], refs[n:2 * n]
        send_sems, recv_sems, local_sems = refs[2 * n:]
        x, y, c = _place()
        me, sibling = (x, y, c), (x, y, 1 - c)
        chips = [(1 - x, y), (x, 1 - y), (1 - x, 1 - y)]

        def rows(t, px, py, pc):
            return outs[t].at[4 * px + 2 * py + pc]

        def copy(t, k, block, to, src=None):
            return pltpu.make_async_remote_copy(
                src_ref=rows(t, *block) if src is None else src, dst_ref=rows(t, *block),
                send_sem=send_sems.at[7 * t + k], recv_sem=recv_sems.at[7 * t + k], device_id=to, device_id_type=MESH)

        mine = [pltpu.make_async_copy(ins[t], rows(t, *me), local_sems.at[t]) for t in range(n)]
        sends = []
        for t in range(n):
            mine[t].start()
            sends.append(copy(t, 0, me, sibling, src=ins[t]))
            sends += [copy(t, 1 + j, me, (*chip, c), src=ins[t]) for j, chip in enumerate(chips)]
        for cp in sends:
            cp.start()
        for j, chip in enumerate(chips):
            for t in range(n):
                copy(t, 1 + j, (*chip, c), me).wait_recv()
                passed = copy(t, 4 + j, (*chip, c), sibling)
                passed.start()
                sends.append(passed)
        for t in range(n):
            copy(t, 0, sibling, me).wait_recv()
            for j, chip in enumerate(chips):
                copy(t, 4 + j, (*chip, 1 - c), me).wait_recv()
        for cp in sends:
            cp.wait_send()
        for cp in mine:
            cp.wait()

    return pl.pallas_call(
        body, name=name, in_specs=[HBM_SPEC] * n, out_specs=[HBM_SPEC] * n,
        out_shape=[jax.ShapeDtypeStruct((N_DEV,) + p.shape, p.dtype) for p in parts],
        scratch_shapes=[pltpu.SemaphoreType.DMA((7 * n,)), pltpu.SemaphoreType.DMA((7 * n,)),
                        pltpu.SemaphoreType.DMA((n,))],
    )(*parts)


def _swap_with_sibling(g, name):
    def body(g_ref, out_ref, send_sem, recv_sem):
        x, y, c = _place()
        cp = pltpu.make_async_remote_copy(src_ref=g_ref.at[1 - c], dst_ref=out_ref, send_sem=send_sem,
                                          recv_sem=recv_sem, device_id=(x, y, 1 - c), device_id_type=MESH)
        cp.start()
        cp.wait()

    return pl.pallas_call(
        body, name=name, in_specs=[HBM_SPEC], out_specs=HBM_SPEC,
        out_shape=jax.ShapeDtypeStruct(g.shape[1:], g.dtype),
        scratch_shapes=[pltpu.SemaphoreType.DMA, pltpu.SemaphoreType.DMA],
    )(g)


def _chip_exchange(p, name):
    def body(p_ref, out_ref, send_sems, recv_sems):
        x, y, c = _place()
        chips = [(1 - x, y), (x, 1 - y), (1 - x, 1 - y)]
        copies = [pltpu.make_async_remote_copy(
            src_ref=p_ref.at[2 * px + py], dst_ref=out_ref.at[k], send_sem=send_sems.at[k], recv_sem=recv_sems.at[k],
            device_id=(px, py, c), device_id_type=MESH) for k, (px, py) in enumerate(chips)]
        for cp in copies:
            cp.start()
        for cp in copies:
            cp.wait()

    return pl.pallas_call(
        body, name=name, in_specs=[HBM_SPEC], out_specs=HBM_SPEC,
        out_shape=jax.ShapeDtypeStruct((3,) + p.shape[1:], p.dtype),
        scratch_shapes=[pltpu.SemaphoreType.DMA((3,)), pltpu.SemaphoreType.DMA((3,))],
    )(p)


def _sum_parts(own, index, others, out_dtype, name, own_row0=0):
    R = others.shape[1]
    rows = _tile(R, 512)
    k = others.shape[0]
    assert own_row0 % rows == 0
    blk0 = own_row0 // rows

    def body(idx_ref, own_ref, *refs):
        del idx_ref
        acc = own_ref[...].astype(F32)
        for r in refs[:k]:
            acc = acc + r[...].astype(F32)
        refs[k][...] = acc.astype(out_dtype)

    grid_spec = pltpu.PrefetchScalarGridSpec(
        num_scalar_prefetch=1, grid=(R // rows,),
        in_specs=[pl.BlockSpec((None, rows, LANES), lambda i, idx: (idx[0], blk0 + i, 0))]
        + [pl.BlockSpec((None, rows, LANES), lambda i, idx, j=j: (j, i, 0)) for j in range(k)],
        out_specs=pl.BlockSpec((rows, LANES), lambda i, idx: (i, 0)))
    return pl.pallas_call(
        body, name=name, grid_spec=grid_spec,
        out_shape=jax.ShapeDtypeStruct((R, LANES), out_dtype), compiler_params=_params("parallel"),
    )(jnp.reshape(index, (1,)).astype(jnp.int32), own, *([others] * k))


def _adamw(w, g, m, v, name):
    shape = w.shape
    cols = shape[-1]
    rows_total = w.size // cols
    w2, g2, m2, v2 = (a.reshape(rows_total, cols) for a in (w, g, m, v))
    rows = rows_total
    if rows_total > ROWS:
        rows = next(r for r in range(ROWS, 0, -8) if rows_total % r == 0)

    def body(w_ref, g_ref, m_ref, v_ref, d_ref, nm_ref, nv_ref):
        gv = g_ref[...]
        nm = ADAM_B1 * m_ref[...] + (1.0 - ADAM_B1) * gv
        nv = ADAM_B2 * v_ref[...] + (1.0 - ADAM_B2) * (gv * gv)
        m_hat = nm / (1.0 - ADAM_B1 ** ADAM_STEP)
        v_hat = nv / (1.0 - ADAM_B2 ** ADAM_STEP)
        d_ref[...] = -ADAM_LR * (m_hat / (jnp.sqrt(v_hat) + ADAM_EPS) + ADAM_WD * w_ref[...])
        nm_ref[...] = nm
        nv_ref[...] = nv

    spec = pl.BlockSpec((rows, cols), lambda i: (i, 0))
    out = jax.ShapeDtypeStruct((rows_total, cols), F32)
    d, nm, nv = pl.pallas_call(
        body, name=name, grid=(rows_total // rows,), in_specs=[spec] * 4, out_specs=[spec] * 3,
        out_shape=[out, out, out], compiler_params=_params("parallel"),
    )(w2, g2, m2, v2)
    return d.reshape(shape), nm.reshape(shape), nv.reshape(shape)


BIG = ("w_in", "w_a_out", "w_pool", "w_attn_out", "w_o", "w_gu", "w_down")


LOCAL = dict(w_in="w_inT", w_a_out="w_a_out", w_pool="w_pool", w_attn_out="w_attn_out", w_o="w_o", w_gu="w_guT",
             w_down="w_down")


def _shard_rows(w, l):
    out = []
    for name in BIG:
        a = w[name][l]
        if name in ("w_in", "w_gu"):
            a = a.T
        elif name == "w_pool":
            a = a.reshape(-1, a.shape[-1])
        out.append(a.astype(BF16))
    return out


def _full_weights(gathered, w):
    out = {}
    for name, g in zip(BIG, gathered):
        if name == "w_pool":
            G, rg, cg = w[name].shape[1:]
            out[name] = jnp.transpose(g.reshape(N_DEV, G, rg, cg), (1, 0, 2, 3)).reshape(G, N_DEV * rg, cg)
        else:
            out[LOCAL[name]] = g.reshape(N_DEV * g.shape[1], g.shape[2])
    return out


def _split_grads(g, w):
    parts, spans, at = [], {}, 0
    for name in BIG:
        a = g[LOCAL[name]].astype(BF16)
        if name == "w_pool":
            G, rg, cg = w[name].shape[1:]
            a = jnp.transpose(a.reshape(G, N_DEV, rg, cg), (1, 0, 2, 3))
        a = jnp.transpose(a.reshape(4, 2, -1, LANES), (1, 0, 2, 3))
        spans[name] = (at, at + a.shape[2])
        at += a.shape[2]
        parts.append(a)
    return jnp.concatenate(parts, axis=2), spans


def _own_grads(g_layers, spans, w):
    L = len(g_layers)
    g3 = jnp.stack(g_layers)
    out = {}
    for name in BIG:
        a = g3[:, spans[name][0]:spans[name][1]]
        if name in ("w_in", "w_gu"):
            sh = w[name].shape
            a = jnp.swapaxes(a.reshape(L, sh[2], sh[1]), 1, 2)
        out[name] = a.reshape(w[name].shape)
    return out


class _Prefetch:
    GROUPS = dict(proj=("w_in",), gu=("w_gu", "w_down"), x2=("w_a_out", "w_pool", "w_attn_out", "w_o"))

    def __init__(self, shards):
        self.shards = dict(zip(BIG, shards))
        self.gathered = {}

    def job(self, key):
        return _GatherJob([self.shards[name] for name in self.GROUPS[key]]) if key in self.GROUPS else None

    def done(self, key, outs):
        self.gathered.update(zip(self.GROUPS[key], outs))

    def result(self):
        return [self.gathered[name] for name in BIG]


class _Reduce:
    CHUNK_ROWS = 1024
    CARRIERS = ("dw_gu", "dw_in", "dh2")

    def __init__(self, split, place, tag):
        self.split, self.tag = split, tag
        self.core, self.chip = place[2], 2 * place[0] + place[1]
        rows = split.shape[2]
        self.chunks = [(r, min(self.CHUNK_ROWS, rows - r)) for r in range(0, rows, self.CHUNK_ROWS)]
        assert len(self.chunks) <= len(self.CARRIERS)
        self.sums = [None] * len(self.chunks)

    def _pair_sum(self, from_sibling):
        rows4 = 4 * self.split.shape[2]
        pair = _sum_parts(self.split.reshape(2, rows4, LANES), self.core, from_sibling.reshape(1, rows4, LANES),
                          BF16, self.tag + "pair_sum")
        self.pair = pair.reshape(4, rows4 // 4, LANES)

    def _chip_sum(self, n, from_chips):
        self.sums[n] = _sum_parts(self.pair, self.chip, from_chips, F32, f"{self.tag}chip_sum{n}",
                                  own_row0=self.chunks[n][0])

    def job(self, key):
        if key == "dw_down":
            return _SwapJob(self.split)
        if key in self.CARRIERS[:len(self.chunks)]:
            return _ExchangeJob(self.pair, *self.chunks[self.CARRIERS.index(key)])
        return None

    def done(self, key, outs):
        if key == "dw_down":
            self._pair_sum(outs[0])
        else:
            self._chip_sum(self.CARRIERS.index(key), outs[0])

    def run(self):
        self._pair_sum(_swap_with_sibling(self.split, self.tag + "reduce_pair"))
        self._chip_sum(0, _chip_exchange(self.pair, self.tag + "reduce_chips"))
        self.chunks, self.sums = self.chunks[:1], self.sums[:1]
        return self.result()

    def result(self):
        return self.sums[0] if len(self.sums) == 1 else jnp.concatenate(self.sums, axis=0)


class _Schedule:
    def __init__(self, w, place):
        self.w, self.place = w, place
        self.depth = w["w_in"].shape[0]
        self.prefetch = {}
        self.reduce = {}
        self.g_layers = [None] * self.depth
        self.spans = None

    def plan_fwd(self, l):
        if l + 1 < self.depth:
            self.prefetch[l + 1] = _Prefetch(_shard_rows(self.w, l + 1))
            return self.prefetch[l + 1]
        return None

    def weights(self, l):
        return _full_weights(self.prefetch[l].result(), self.w)

    def plan_bwd(self, l):
        return self.reduce.get(l + 1)

    def grads(self, l, g):
        if l + 1 in self.reduce:
            self.g_layers[l + 1] = self.reduce[l + 1].result()
        split, self.spans = _split_grads(g, self.w)
        self.reduce[l] = _Reduce(split, self.place, f"l{l}_")
        if l == 0:
            self.g_layers[0] = self.reduce[0].run()


SMALL_ROWS = 32


def _pack_small(gs, L, D):
    rows = [gs["pool_scale"].reshape(L, D), gs["g_mix"].reshape(L, D), gs["g_ffn"].reshape(L, D),
            gs["g_final"].reshape(1, D), gs["conv_w"][:, :3].reshape(3 * L, D),
            jnp.pad(gs["attn_sink"].reshape(1, -1), ((0, 0), (0, D - L * N_HEADS))),
            jnp.pad(gs["rel_bias"].reshape(1, -1), ((0, 0), (0, D - N_BUCKETS * N_HEADS)))]
    a = jnp.concatenate(rows, axis=0)
    return jnp.pad(a, ((0, SMALL_ROWS - a.shape[0]), (0, 0)))


def _unpack_small(a, L, D):
    g = {}
    g["pool_scale"] = a[0:L]
    g["g_mix"] = a[L:2 * L]
    g["g_ffn"] = a[2 * L:3 * L]
    g["g_final"] = a[3 * L]
    g["conv_w"] = a[3 * L + 1:6 * L + 1].reshape(L, 3, 1, D)
    g["attn_sink"] = a[6 * L + 1, :L * N_HEADS].reshape(L, N_HEADS)
    g["rel_bias"] = a[6 * L + 2, :N_BUCKETS * N_HEADS].reshape(N_BUCKETS, N_HEADS)
    return g


WEIGHTS = ("w_in", "conv_w", "w_a_out", "w_pool", "pool_scale", "w_attn_out", "attn_sink", "w_o", "g_mix", "g_ffn",
           "w_gu", "w_down", "rel_bias", "g_final")


def kernel(x, w_in, conv_w, w_a_out, w_pool, pool_scale, w_attn_out, attn_sink, w_o, g_mix, g_ffn, w_gu, w_down, rel_bias, g_final, loss_target, m_w_in, m_conv_w, m_w_a_out, m_w_pool, m_pool_scale, m_w_attn_out, m_attn_sink, m_w_o, m_g_mix, m_g_ffn, m_w_gu, m_w_down, m_rel_bias, m_g_final, v_w_in, v_conv_w, v_w_a_out, v_w_pool, v_pool_scale, v_w_attn_out, v_attn_sink, v_w_o, v_g_mix, v_g_ffn, v_w_gu, v_w_down, v_rel_bias, v_g_final):
    w = dict(w_in=w_in, conv_w=conv_w, w_a_out=w_a_out, w_pool=w_pool, pool_scale=pool_scale, w_attn_out=w_attn_out,
             attn_sink=attn_sink, w_o=w_o, g_mix=g_mix, g_ffn=g_ffn, w_gu=w_gu, w_down=w_down, rel_bias=rel_bias,
             g_final=g_final)
    m = dict(w_in=m_w_in, conv_w=m_conv_w, w_a_out=m_w_a_out, w_pool=m_w_pool, pool_scale=m_pool_scale,
             w_attn_out=m_w_attn_out, attn_sink=m_attn_sink, w_o=m_w_o, g_mix=m_g_mix, g_ffn=m_g_ffn, w_gu=m_w_gu,
             w_down=m_w_down, rel_bias=m_rel_bias, g_final=m_g_final)
    v = dict(w_in=v_w_in, conv_w=v_conv_w, w_a_out=v_w_a_out, w_pool=v_w_pool, pool_scale=v_pool_scale,
             w_attn_out=v_w_attn_out, attn_sink=v_attn_sink, w_o=v_w_o, g_mix=v_g_mix, g_ffn=v_g_ffn, w_gu=v_w_gu,
             w_down=v_w_down, rel_bias=v_rel_bias, g_final=v_g_final)
    T, D = x.shape[1], x.shape[2]
    L = w_in.shape[0]
    cx, cy, cc = _place()

    wts = [_full_weights(_all_gather_many(_shard_rows(w, 0), "gather_weights_l0"), w)]
    cw = jnp.pad(conv_w.reshape(L * 3, -1), ((0, 16 - L * 3), (0, 0)))
    cw = _all_gather(cw, "gather_conv_w")
    cw = jnp.transpose(cw, (1, 0, 2)).reshape(16, -1)[:L * 3].reshape(L, 3, -1)
    small = dict(conv_w=jnp.pad(cw, ((0, 0), (0, 5), (0, 0))), pool_scale=pool_scale.reshape(L, 1, D),
                 g_mix=g_mix.reshape(L, 1, D), g_ffn=g_ffn.reshape(L, 1, D), attn_sink=attn_sink,
                 rel_bias=rel_bias, g_final=g_final.reshape(1, D))

    schedule = _Schedule(w, (cx, cy, cc))
    loss, dx, _, gs = _local_step(x[0], loss_target[0], wts, small, schedule)
    loss = lax.psum(loss[0, 0], ("x", "y", "c"))
    grads = _own_grads(schedule.g_layers, schedule.spans, w)

    small_all = _all_gather(_pack_small(gs, L, D), "gather_small")
    small_sum = _sum_parts(small_all, jnp.int32(0), small_all[1:], F32, "small_sum")
    gsm = _unpack_small(small_sum, L, D)
    W8 = D // N_DEV
    dev = 4 * cx + 2 * cy + cc
    gsm["conv_w"] = lax.dynamic_slice_in_dim(gsm["conv_w"], dev * W8, W8, axis=3)
    grads.update(gsm)

    deltas, new_m, new_v = {}, {}, {}
    for name in WEIGHTS:
        deltas[name], new_m[name], new_v[name] = _adamw(w[name], grads[name], m[name], v[name], "adamw_" + name)

    return (loss, dx[None], *[grads[n] for n in WEIGHTS], *[deltas[n] for n in WEIGHTS],
            *[new_m[n] for n in WEIGHTS], *[new_v[n] for n in WEIGHTS])
```

```python
import functools
import math

import jax
import jax.numpy as jnp
from jax import lax
from jax.experimental import pallas as pl
from jax.experimental.pallas import tpu as pltpu

F32 = jnp.float32
BF16 = jnp.bfloat16
MESH = pl.DeviceIdType.MESH

N_DEV = 8
N_HEADS = 16
N_KV_HEADS = 4
HEAD_DIM = 64
GROUP = N_HEADS // N_KV_HEADS
BLOCK = 128
WINDOW = 128
N_BUCKETS = 32
MAX_DISTANCE = 128
POOL_WINDOWS = (2, 4, 8, 16)
POOL_GROUPS = 4
HALO = 8
EPS = 1e-6
NEG_INF = -1e30

ADAM_LR = 0.001
ADAM_B1 = 0.9
ADAM_B2 = 0.999
ADAM_EPS = 1e-08
ADAM_WD = 0.01
ADAM_STEP = 10

LANES = 1024
VMEM_LIMIT_BYTES = 48 * 1024 * 1024


def _params(*sem):
    return pltpu.CompilerParams(dimension_semantics=sem, vmem_limit_bytes=VMEM_LIMIT_BYTES)


def _tile(n, cap):
    if n <= cap:
        return n
    for t in range(cap - cap % 128, 0, -128):
        if n % t == 0:
            return t
    raise ValueError(f"no tile for {n}")


_DIMS = {"nn": (((1,), (0,)), ((), ())), "nt": (((1,), (1,)), ((), ())), "tn": (((0,), (0,)), ((), ()))}


HBM_SPEC = pl.BlockSpec(memory_space=pltpu.HBM)


def _matmul(a, b, mode, out_dtype, name, res=None, tm_cap=1024, tn_cap=1024, tk_cap=1024, comm=None):
    if mode == "tn":
        K, M = a.shape
    else:
        M, K = a.shape
    N = b.shape[0] if mode == "nt" else b.shape[1]
    tm, tn, tk = _tile(M, tm_cap), _tile(N, tn_cap), _tile(K, tk_cap)
    nk = K // tk
    a_spec = pl.BlockSpec((tk, tm), lambda i, j, k: (k, i)) if mode == "tn" else pl.BlockSpec((tm, tk), lambda i, j, k: (i, k))
    b_spec = pl.BlockSpec((tn, tk), lambda i, j, k: (j, k)) if mode == "nt" else pl.BlockSpec((tk, tn), lambda i, j, k: (k, j))
    o_spec = pl.BlockSpec((tm, tn), lambda i, j, k: (i, j))
    dims = _DIMS[mode]
    has_res = res is not None
    gm, gn = M // tm, N // tn
    steps = gm * gn * nk
    n_in = 2 + has_res
    n_ci = len(comm.ins) if comm is not None else 0
    n_co = len(comm.outs) if comm is not None else 0

    def body(*refs):
        a_ref, b_ref = refs[0], refs[1]
        res_ref = refs[2] if has_res else None
        comm_in = refs[n_in:n_in + n_ci]
        o_ref = refs[n_in + n_ci]
        comm_out = refs[n_in + n_ci + 1:n_in + n_ci + 1 + n_co]
        acc_ref = refs[n_in + n_ci + 1 + n_co]
        sems = refs[n_in + n_ci + 2 + n_co:]
        k = pl.program_id(2)
        step = (pl.program_id(0) * gn + pl.program_id(1)) * nk + k
        if comm is not None:
            @pl.when(step == 0)
            def _():
                comm.start(comm_in, comm_out, sems)

        part = lax.dot_general(a_ref[...], b_ref[...], dims, preferred_element_type=F32)

        @pl.when(k == 0)
        def _():
            acc_ref[...] = part

        @pl.when(k > 0)
        def _():
            acc_ref[...] += part

        @pl.when(k == nk - 1)
        def _():
            out = acc_ref[...]
            if has_res:
                out = out + res_ref[...]
            o_ref[...] = out.astype(out_dtype)

        if comm is not None:
            @pl.when(step == (3 * steps) // 4)
            def _():
                comm.mid(comm_in, comm_out, sems)

            @pl.when(step == steps - 1)
            def _():
                comm.finish(comm_in, comm_out, sems)

    in_specs = [a_spec, b_spec] + ([o_spec] if has_res else [])
    args = (a, b) + ((res,) if has_res else ())
    out_shape = jax.ShapeDtypeStruct((M, N), out_dtype)
    if comm is None:
        return pl.pallas_call(
            body, name=name, grid=(gm, gn, nk), in_specs=in_specs, out_specs=o_spec, out_shape=out_shape,
            scratch_shapes=[pltpu.VMEM((tm, tn), F32)],
            compiler_params=_params("parallel", "parallel", "arbitrary"),
        )(*args)
    outs = pl.pallas_call(
        body, name=name, grid=(gm, gn, nk),
        in_specs=in_specs + [HBM_SPEC] * n_ci, out_specs=[o_spec] + [HBM_SPEC] * n_co,
        out_shape=[out_shape] + list(comm.outs),
        scratch_shapes=[pltpu.VMEM((tm, tn), F32)] + list(comm.sems),
        compiler_params=_params("arbitrary", "arbitrary", "arbitrary"),
    )(*args, *comm.ins)
    return outs[0], outs[1:]


def _pool_mm(a, w, mode, out_dtype, name):
    T = a.shape[0]
    G = POOL_GROUPS
    cg = a.shape[1] // G
    tm = _tile(T, 1024)
    nt = T // tm
    dims = _DIMS[mode]
    if mode == "tn":
        def body(a_ref, d_ref, o_ref):
            part = lax.dot_general(a_ref[...], d_ref[...], dims, preferred_element_type=F32)

            @pl.when(pl.program_id(1) == 0)
            def _():
                o_ref[...] = part

            @pl.when(pl.program_id(1) > 0)
            def _():
                o_ref[...] += part

        return pl.pallas_call(
            body, name=name, grid=(G, nt),
            in_specs=[pl.BlockSpec((tm, cg), lambda g, i: (i, g)), pl.BlockSpec((tm, cg), lambda g, i: (i, g))],
            out_specs=pl.BlockSpec((None, cg, cg), lambda g, i: (g, 0, 0)),
            out_shape=jax.ShapeDtypeStruct((G, cg, cg), F32),
            compiler_params=_params("parallel", "arbitrary"),
        )(a, w)

    def body(a_ref, w_ref, o_ref):
        o_ref[...] = lax.dot_general(a_ref[...], w_ref[...], dims, preferred_element_type=F32).astype(out_dtype)

    return pl.pallas_call(
        body, name=name, grid=(G, nt),
        in_specs=[pl.BlockSpec((tm, cg), lambda g, i: (i, g)), pl.BlockSpec((None, cg, cg), lambda g, i: (g, 0, 0))],
        out_specs=pl.BlockSpec((tm, cg), lambda g, i: (i, g)),
        out_shape=jax.ShapeDtypeStruct((T, G * cg), out_dtype),
        compiler_params=_params("parallel", "parallel"),
    )(a, w)


ROWS = 256
HALO_BLOCK = 16


def _row_spec(d, col=0, rows=ROWS):
    return pl.BlockSpec((rows, d), lambda i, col=col: (i, col))


def _const_spec(shape):
    return pl.BlockSpec(shape, lambda *_: (0,) * len(shape))


def _rms_fwd(x, g, name):
    T, D = x.shape

    def body(x_ref, g_ref, h_ref):
        xv = x_ref[...]
        r = lax.rsqrt(jnp.mean(xv * xv, axis=-1, keepdims=True) + EPS)
        h_ref[...] = (xv * r * g_ref[...]).astype(BF16)

    return pl.pallas_call(
        body, name=name, grid=(T // ROWS,),
        in_specs=[_row_spec(D), _const_spec((1, D))], out_specs=_row_spec(D),
        out_shape=jax.ShapeDtypeStruct((T, D), BF16), compiler_params=_params("parallel"),
    )(x, g)


def _accumulate(ref, part):
    first = pl.program_id(0) == 0

    @pl.when(first)
    def _():
        ref[...] = part

    @pl.when(jnp.logical_not(first))
    def _():
        ref[...] += part


def _rms_bwd(x, g, dh, dres, name):
    T, D = x.shape

    def body(x_ref, g_ref, dh_ref, dres_ref, dx_ref, dxb_ref, dg_ref):
        xv = x_ref[...]
        r = lax.rsqrt(jnp.mean(xv * xv, axis=-1, keepdims=True) + EPS)
        xhat = xv * r
        dh_v = dh_ref[...]
        dxhat = dh_v * g_ref[...]
        dx = dres_ref[...] + r * (dxhat - xhat * jnp.mean(dxhat * xhat, axis=-1, keepdims=True))
        dx_ref[...] = dx
        dxb_ref[...] = dx.astype(BF16)
        _accumulate(dg_ref, jnp.sum(dh_v * xhat, axis=0, keepdims=True))

    return pl.pallas_call(
        body, name=name, grid=(T // ROWS,),
        in_specs=[_row_spec(D), _const_spec((1, D)), _row_spec(D), _row_spec(D)],
        out_specs=[_row_spec(D), _row_spec(D), _const_spec((1, D))],
        out_shape=[jax.ShapeDtypeStruct((T, D), F32), jax.ShapeDtypeStruct((T, D), BF16),
                   jax.ShapeDtypeStruct((1, D), F32)],
        compiler_params=_params("arbitrary"),
    )(x, g, dh, dres)


def _loss_head(x, g, target, name):
    T, D = x.shape

    def body(x_ref, g_ref, t_ref, loss_ref, dx_ref, dxb_ref, dg_ref):
        xv = x_ref[...]
        gv = g_ref[...]
        r = lax.rsqrt(jnp.mean(xv * xv, axis=-1, keepdims=True) + EPS)
        xhat = xv * r
        err = xhat * gv - t_ref[...]
        loss = 0.5 * jnp.sum(jnp.mean(err * err, axis=-1, keepdims=True), axis=0, keepdims=True)
        dy = err * (1.0 / D)
        dxhat = dy * gv
        dx = r * (dxhat - xhat * jnp.mean(dxhat * xhat, axis=-1, keepdims=True))
        dx_ref[...] = dx
        dxb_ref[...] = dx.astype(BF16)
        _accumulate(loss_ref, loss)
        _accumulate(dg_ref, jnp.sum(dy * xhat, axis=0, keepdims=True))

    return pl.pallas_call(
        body, name=name, grid=(T // ROWS,),
        in_specs=[_row_spec(D), _const_spec((1, D)), _row_spec(D)],
        out_specs=[_const_spec((1, 1)), _row_spec(D), _row_spec(D), _const_spec((1, D))],
        out_shape=[jax.ShapeDtypeStruct((1, 1), F32), jax.ShapeDtypeStruct((T, D), F32),
                   jax.ShapeDtypeStruct((T, D), BF16), jax.ShapeDtypeStruct((1, D), F32)],
        compiler_params=_params("arbitrary"),
    )(x, g, target)


def _halo_specs(d, col, n_blocks):
    per = ROWS // HALO_BLOCK
    last = n_blocks * per - 1
    prev = pl.BlockSpec((HALO_BLOCK, d), lambda i, col=col: (jnp.maximum(i * per - 1, 0), col))
    nxt = pl.BlockSpec((HALO_BLOCK, d), lambda i, col=col: (jnp.minimum((i + 1) * per, last), col))
    return prev, nxt


def _with_halo(prev, cur, nxt, n_blocks):
    i = pl.program_id(0)
    prev = jnp.where(i > 0, prev[HALO_BLOCK - HALO:], 0.0)
    nxt = jnp.where(i < n_blocks - 1, nxt[:HALO], 0.0)
    return jnp.concatenate([prev, cur, nxt], axis=0)


def _f32(ref):
    return ref[...].astype(F32)


def _shift(ext, k):
    n = ext.shape[0]
    v = ext if k == 0 else pltpu.roll(ext, (-k) % n, 0)
    return v[HALO:HALO + ROWS]


def _shift_full(ext, k):
    n = ext.shape[0]
    return pltpu.roll(ext, (-k) % n, 0)


def _pool_counts(T):
    n = ROWS + 2 * HALO
    t = pl.program_id(0) * ROWS - HALO + lax.broadcasted_iota(jnp.int32, (n, 1), 0)
    out = []
    for w in POOL_WINDOWS:
        lo = jnp.maximum(t - w // 2, 0)
        hi = jnp.minimum(t + (w - 1 - w // 2), T - 1)
        out.append(jnp.maximum(hi - lo + 1, 1).astype(F32))
    return out


def _window_sums(e, sign):
    s2 = e + _shift_full(e, -sign)
    s4 = _shift_full(s2, -1) + _shift_full(s2, 1)
    s8 = _shift_full(s4, -2) + _shift_full(s4, 2)
    s16 = _shift_full(s8, -4) + _shift_full(s8, 4)
    return s2, s4, s8, s16


def _mixer_fwd(proj, conv_w, name):
    T = proj.shape[0]
    W = conv_w.shape[1]
    nb = T // ROWS
    cg = W // POOL_GROUPS

    def body(b_ref, c_ref, x_ref, u_ref, cp_ref, cn_ref, xp_ref, xn_ref, up_ref, un_ref, w_ref, z_ref, p_ref):
        uc = _with_halo(_f32(cp_ref) * _f32(xp_ref), _f32(c_ref) * _f32(x_ref), _f32(cn_ref) * _f32(xn_ref), nb)
        w0, w1, w2 = w_ref[0:1, :], w_ref[1:2, :], w_ref[2:3, :]
        y = w0 * _shift(uc, -1) + w1 * _shift(uc, 0) + w2 * _shift(uc, 1)
        z_ref[...] = (_f32(b_ref) * y).astype(BF16)
        e = _with_halo(_f32(up_ref), _f32(u_ref), _f32(un_ref), nb)
        counts = _pool_counts(T)
        for gi in range(POOL_GROUPS):
            eg = e[:, gi * cg:(gi + 1) * cg]
            s = _window_sums(eg, 1)[gi]
            p = s[HALO:HALO + ROWS] / counts[gi][HALO:HALO + ROWS] - eg[HALO:HALO + ROWS]
            p_ref[:, gi * cg:(gi + 1) * cg] = p.astype(BF16)

    halo = [s for col in (1, 2, 3) for s in _halo_specs(W, col, nb)]
    return pl.pallas_call(
        body, name=name, grid=(nb,),
        in_specs=[_row_spec(W, 0), _row_spec(W, 1), _row_spec(W, 2), _row_spec(W, 3)] + halo + [_const_spec((8, W))],
        out_specs=[_row_spec(W), _row_spec(W)],
        out_shape=[jax.ShapeDtypeStruct((T, W), BF16), jax.ShapeDtypeStruct((T, W), BF16)],
        compiler_params=_params("parallel"),
    )(proj, proj, proj, proj, proj, proj, proj, proj, proj, proj, conv_w)


def _mixer_bwd(proj, conv_w, dz, dp, name):
    T = proj.shape[0]
    W = conv_w.shape[1]
    nb = T // ROWS
    cg = W // POOL_GROUPS

    def body(b_ref, c_ref, x_ref, dz_ref, dp_ref,
             bp_ref, bn_ref, cp_ref, cn_ref, xp_ref, xn_ref, dzp_ref, dzn_ref, dpp_ref, dpn_ref, w_ref,
             o_ref, dw_ref):
        cv, xv, dzv = _f32(c_ref), _f32(x_ref), _f32(dz_ref)
        uc = _with_halo(_f32(cp_ref) * _f32(xp_ref), cv * xv, _f32(cn_ref) * _f32(xn_ref), nb)
        dy = _with_halo(_f32(dzp_ref) * _f32(bp_ref), dzv * _f32(b_ref), _f32(dzn_ref) * _f32(bn_ref), nb)
        w0, w1, w2 = w_ref[0:1, :], w_ref[1:2, :], w_ref[2:3, :]
        um, u0, up = _shift(uc, -1), _shift(uc, 0), _shift(uc, 1)
        o_ref[:, 0:W] = (dzv * (w0 * um + w1 * u0 + w2 * up)).astype(BF16)
        dy0 = _shift(dy, 0)
        duc = w0 * _shift(dy, 1) + w1 * dy0 + w2 * _shift(dy, -1)
        o_ref[:, W:2 * W] = (duc * xv).astype(BF16)
        o_ref[:, 2 * W:3 * W] = (duc * cv).astype(BF16)
        row = lax.broadcasted_iota(jnp.int32, (8, W), 0)
        dw = jnp.where(row == 0, jnp.sum(dy0 * um, axis=0, keepdims=True),
                       jnp.where(row == 1, jnp.sum(dy0 * u0, axis=0, keepdims=True),
                                 jnp.where(row == 2, jnp.sum(dy0 * up, axis=0, keepdims=True), 0.0)))
        _accumulate(dw_ref, dw)
        d = _with_halo(_f32(dpp_ref), _f32(dp_ref), _f32(dpn_ref), nb)
        counts = _pool_counts(T)
        for gi in range(POOL_GROUPS):
            dg = d[:, gi * cg:(gi + 1) * cg]
            s = _window_sums(dg / counts[gi], -1)[gi]
            o_ref[:, 3 * W + gi * cg:3 * W + (gi + 1) * cg] = (s[HALO:HALO + ROWS] - dg[HALO:HALO + ROWS]).astype(BF16)

    def halo(col):
        return list(_halo_specs(W, col, nb))

    return pl.pallas_call(
        body, name=name, grid=(nb,),
        in_specs=[_row_spec(W, 0), _row_spec(W, 1), _row_spec(W, 2), _row_spec(W), _row_spec(W)]
        + halo(0) + halo(1) + halo(2) + halo(0) + halo(0) + [_const_spec((8, W))],
        out_specs=[_row_spec(4 * W), _const_spec((8, W))],
        out_shape=[jax.ShapeDtypeStruct((T, 4 * W), BF16), jax.ShapeDtypeStruct((8, W), F32)],
        compiler_params=_params("arbitrary"),
    )(proj, proj, proj, dz, dp, proj, proj, proj, proj, proj, proj, dz, dz, dp, dp, conv_w)


def _t5_bucket(rel):
    half = N_BUCKETS // 2
    max_exact = half // 2
    ret = jnp.where(rel > 0, half, 0)
    n = jnp.abs(rel)
    nf = jnp.maximum(n, 1).astype(jnp.float32)
    large = max_exact + (jnp.log(nf / max_exact) / math.log(MAX_DISTANCE / max_exact)
                         * (half - max_exact)).astype(jnp.int32)
    large = jnp.minimum(large, half - 1)
    return ret + jnp.where(n < max_exact, n, large)


def _bucket_table():
    qi = jnp.arange(BLOCK)[:, None]
    kj = jnp.arange(3 * BLOCK)[None, :]
    rel = kj - BLOCK - qi
    return jnp.where(jnp.abs(rel) <= WINDOW, _t5_bucket(rel), -1).astype(jnp.int32)


def _bias_table(rel_bias, bucket, name):
    def body(rb_ref, bucket_ref, o_ref):
        h = pl.program_id(0)
        bk = bucket_ref[...]
        acc = jnp.full(bk.shape, NEG_INF, F32)
        for b in range(N_BUCKETS):
            acc = jnp.where(bk == b, rb_ref[b, h], acc)
        o_ref[...] = acc

    return pl.pallas_call(
        body, name=name, grid=(N_HEADS,),
        in_specs=[pl.BlockSpec(memory_space=pltpu.SMEM), _const_spec((BLOCK, 3 * BLOCK))],
        out_specs=pl.BlockSpec((None, BLOCK, 3 * BLOCK), lambda h: (h, 0, 0)),
        out_shape=jax.ShapeDtypeStruct((N_HEADS, BLOCK, 3 * BLOCK), F32),
        compiler_params=_params("parallel"),
    )(rel_bias, bucket)


def _bias_grad(ds_sum, bucket, name):
    def body(ds_ref, bucket_ref, o_ref):
        bk = bucket_ref[...]
        ds = ds_ref[...]
        row = lax.broadcasted_iota(jnp.int32, (N_BUCKETS, 128), 0)
        acc = jnp.zeros((N_BUCKETS, 128), F32)
        for b in range(N_BUCKETS):
            s = jnp.sum(jnp.sum(jnp.where(bk == b, ds, 0.0), axis=1, keepdims=True), axis=0, keepdims=True)
            acc = jnp.where(row == b, s, acc)
        o_ref[...] = acc

    return pl.pallas_call(
        body, name=name, grid=(N_HEADS,),
        in_specs=[pl.BlockSpec((None, BLOCK, 3 * BLOCK), lambda h: (h, 0, 0)), _const_spec((BLOCK, 3 * BLOCK))],
        out_specs=pl.BlockSpec((None, N_BUCKETS, 128), lambda h: (h, 0, 0)),
        out_shape=jax.ShapeDtypeStruct((N_HEADS, N_BUCKETS, 128), F32),
        compiler_params=_params("parallel"),
    )(ds_sum, bucket)


PAIR = 2 * HEAD_DIM


def _low_half(shape):
    return lax.broadcasted_iota(jnp.int32, shape, len(shape) - 1) % PAIR < HEAD_DIM


def _split_pair(a):
    low = _low_half(a.shape)
    zero = jnp.zeros_like(a)
    return jnp.concatenate([jnp.where(low, a, zero), jnp.where(low, zero, a)], axis=0)


def _kv_expand(proj, kv_off, name):
    T = proj.shape[0]
    kv_w = N_KV_HEADS * HEAD_DIM
    rows = _tile(T, 512)

    def body(k_ref, v_ref, ke_ref, ve_ref):
        for src, dst in ((k_ref, ke_ref), (v_ref, ve_ref)):
            for g in range(N_KV_HEADS // 2):
                x = src[:, g * PAIR:(g + 1) * PAIR].astype(F32)
                swapped = pltpu.roll(x, HEAD_DIM, 1)
                low = _low_half(x.shape)
                dst[:, 2 * g * PAIR:(2 * g + 1) * PAIR] = jnp.where(low, x, swapped).astype(BF16)
                dst[:, (2 * g + 1) * PAIR:(2 * g + 2) * PAIR] = jnp.where(low, swapped, x).astype(BF16)

    out = jax.ShapeDtypeStruct((T, N_KV_HEADS * PAIR), BF16)
    ospec = pl.BlockSpec((rows, N_KV_HEADS * PAIR), lambda i: (i, 0))
    return pl.pallas_call(
        body, name=name, grid=(T // rows,),
        in_specs=[pl.BlockSpec((rows, kv_w), lambda i: (i, kv_off // kv_w)),
                  pl.BlockSpec((rows, kv_w), lambda i: (i, kv_off // kv_w + 1))],
        out_specs=[ospec, ospec], out_shape=[out, out], compiler_params=_params("parallel"),
    )(proj, proj)


def _kv_fold(dke, dve, name):
    T = dke.shape[0]
    kv_w = N_KV_HEADS * HEAD_DIM
    rows = _tile(T, 512)

    def body(dk_ref, dv_ref, o_ref):
        for n, src in enumerate((dk_ref, dv_ref)):
            for g in range(N_KV_HEADS // 2):
                a = src[:, 2 * g * PAIR:(2 * g + 1) * PAIR]
                b = src[:, (2 * g + 1) * PAIR:(2 * g + 2) * PAIR]
                a = a + pltpu.roll(a, HEAD_DIM, 1)
                b = b + pltpu.roll(b, HEAD_DIM, 1)
                o_ref[:, n * kv_w + g * PAIR:n * kv_w + (g + 1) * PAIR] = jnp.where(_low_half(a.shape), a, b).astype(BF16)

    ispec = pl.BlockSpec((rows, N_KV_HEADS * PAIR), lambda i: (i, 0))
    return pl.pallas_call(
        body, name=name, grid=(T // rows,), in_specs=[ispec, ispec],
        out_specs=pl.BlockSpec((rows, 2 * kv_w), lambda i: (i, 0)),
        out_shape=jax.ShapeDtypeStruct((T, 2 * kv_w), BF16), compiler_params=_params("parallel"),
    )(dke, dve)


def _key_blocks(i, nb):
    return [pl.multiple_of(n * BLOCK, BLOCK) for n in (jnp.maximum(i - 1, 0), i, jnp.minimum(i + 1, nb - 1))]


def _three_blocks(ref, starts):
    return jnp.concatenate([ref[pl.ds(s, BLOCK), :] for s in starts], axis=0)


def _pair_scores(q2, kd, bias_ref, pr, i, nb):
    qq = _split_pair(q2)
    s = lax.dot_general(qq, kd, _DIMS["nt"], preferred_element_type=F32) * (HEAD_DIM ** -0.5)
    s = s + bias_ref[2 * pr:2 * pr + 2].reshape(2 * BLOCK, 3 * BLOCK)
    kj = lax.broadcasted_iota(jnp.int32, (1, 3 * BLOCK), 1)
    outside = jnp.logical_or(jnp.logical_and(i == 0, kj < BLOCK), jnp.logical_and(i == nb - 1, kj >= 2 * BLOCK))
    return qq, jnp.where(outside, NEG_INF, s)


def _attn_specs(T, q_off):
    gw = GROUP * HEAD_DIM
    return dict(
        sink=pl.BlockSpec(memory_space=pltpu.SMEM),
        q=pl.BlockSpec((BLOCK, gw), lambda j, i: (i, q_off // gw + j)),
        kv=pl.BlockSpec((T, PAIR), lambda j, i: (0, j)),
        bias=pl.BlockSpec((GROUP, BLOCK, 3 * BLOCK), lambda j, i: (j, 0, 0)),
        o=pl.BlockSpec((BLOCK, gw), lambda j, i: (i, j)))


def _attn_fwd(proj, q_off, kexp, vexp, bias, sink, name, comm=None):
    T = proj.shape[0]
    nb = T // BLOCK
    sp = _attn_specs(T, q_off)
    steps = N_KV_HEADS * nb
    n_ci = len(comm.ins) if comm is not None else 0
    n_co = len(comm.outs) if comm is not None else 0

    def body(*refs):
        sink_ref, q_ref, ke_ref, ve_ref, bias_ref = refs[:5]
        comm_in = refs[5:5 + n_ci]
        o_ref, lse_ref = refs[5 + n_ci:7 + n_ci]
        comm_out = refs[7 + n_ci:7 + n_ci + n_co]
        sems = refs[7 + n_ci + n_co:]
        j, i = pl.program_id(0), pl.program_id(1)
        step = j * nb + i
        if comm is not None:
            @pl.when(step == 0)
            def _():
                comm.start(comm_in, comm_out, sems)

        starts = _key_blocks(i, nb)
        kd = _three_blocks(ke_ref, starts)
        vv = _split_pair(_three_blocks(ve_ref, starts))
        first_rows = lax.broadcasted_iota(jnp.int32, (2 * BLOCK, 1), 0) < BLOCK
        low = _low_half((BLOCK, PAIR))
        for pr in range(GROUP // 2):
            lanes = slice(pr * PAIR, (pr + 1) * PAIR)
            _, s = _pair_scores(q_ref[:, lanes], kd, bias_ref, pr, i, nb)
            head = GROUP * j + 2 * pr
            sk = jnp.where(first_rows, sink_ref[head], sink_ref[head + 1])
            m = jnp.maximum(jnp.max(s, axis=-1, keepdims=True), sk)
            p = jnp.exp(s - m)
            denom = jnp.sum(p, axis=-1, keepdims=True) + jnp.exp(sk - m)
            p = (p / denom).astype(BF16)
            pp = jnp.concatenate([p[:BLOCK], p[BLOCK:]], axis=1)
            o_ref[:, lanes] = lax.dot_general(pp, vv, _DIMS["nn"], preferred_element_type=F32).astype(BF16)
            lse = m + jnp.log(denom)
            lse_ref[:, lanes] = jnp.where(low, lse[:BLOCK], lse[BLOCK:])

        if comm is not None:
            @pl.when(step == (3 * steps) // 4)
            def _():
                comm.mid(comm_in, comm_out, sems)

            @pl.when(step == steps - 1)
            def _():
                comm.finish(comm_in, comm_out, sems)

    out_shape = [jax.ShapeDtypeStruct((T, N_HEADS * HEAD_DIM), BF16), jax.ShapeDtypeStruct((T, N_HEADS * HEAD_DIM), F32)]
    in_specs = [sp["sink"], sp["q"], sp["kv"], sp["kv"], sp["bias"]]
    if comm is None:
        att, lse = pl.pallas_call(
            body, name=name, grid=(N_KV_HEADS, nb), in_specs=in_specs, out_specs=[sp["o"], sp["o"]],
            out_shape=out_shape, compiler_params=_params("parallel", "parallel"),
        )(sink, proj, kexp, vexp, bias)
        return att, lse, []
    outs = pl.pallas_call(
        body, name=name, grid=(N_KV_HEADS, nb),
        in_specs=in_specs + [HBM_SPEC] * n_ci, out_specs=[sp["o"], sp["o"]] + [HBM_SPEC] * n_co,
        out_shape=out_shape + list(comm.outs), scratch_shapes=list(comm.sems),
        compiler_params=_params("arbitrary", "arbitrary"),
    )(sink, proj, kexp, vexp, bias, *comm.ins)
    return outs[0], outs[1], outs[2:]


def _attn_bwd(proj, q_off, kexp, vexp, bias, sink, out, lse, dout, name):
    T = proj.shape[0]
    nb = T // BLOCK
    sp = _attn_specs(T, q_off)
    scale = HEAD_DIM ** -0.5

    def body(sink_ref, q_ref, ke_ref, ve_ref, bias_ref, o_ref, lse_ref, do_ref,
             dq_ref, dke_ref, dve_ref, ds_ref, dsink_ref):
        j, i = pl.program_id(0), pl.program_id(1)

        @pl.when(i == 0)
        def _():
            dke_ref[...] = jnp.zeros(dke_ref.shape, F32)
            dve_ref[...] = jnp.zeros(dve_ref.shape, F32)
            ds_ref[...] = jnp.zeros(ds_ref.shape, F32)
            dsink_ref[...] = jnp.zeros(dsink_ref.shape, F32)

        starts = _key_blocks(i, nb)
        kd = _three_blocks(ke_ref, starts)
        vd = _three_blocks(ve_ref, starts)
        kk = _split_pair(kd)
        low = _low_half((BLOCK, PAIR))
        dk_acc = jnp.zeros((3 * BLOCK, PAIR), F32)
        dv_acc = jnp.zeros((3 * BLOCK, PAIR), F32)
        for pr in range(GROUP // 2):
            lanes = slice(pr * PAIR, (pr + 1) * PAIR)
            qq, s = _pair_scores(q_ref[:, lanes], kd, bias_ref, pr, i, nb)
            l2 = lse_ref[:, lanes]
            l2s = pltpu.roll(l2, HEAD_DIM, 1)
            lse_a, lse_b = jnp.where(low, l2, l2s), jnp.where(low, l2s, l2)
            p = jnp.exp(s - jnp.concatenate([jnp.concatenate([lse_a] * 3, axis=1),
                                             jnp.concatenate([lse_b] * 3, axis=1)], axis=0))
            do2 = do_ref[:, lanes]
            prod = do2.astype(F32) * o_ref[:, lanes].astype(F32)
            delta_a = jnp.sum(jnp.where(low, prod, 0.0), axis=-1, keepdims=True)
            delta_b = jnp.sum(jnp.where(low, 0.0, prod), axis=-1, keepdims=True)
            dd = _split_pair(do2)
            dp = lax.dot_general(dd, vd, _DIMS["nt"], preferred_element_type=F32)
            ds = p * (dp - jnp.concatenate([delta_a, delta_b], axis=0))
            dsb = ds.astype(BF16)
            dq = lax.dot_general(jnp.concatenate([dsb[:BLOCK], dsb[BLOCK:]], axis=1), kk, _DIMS["nn"],
                                 preferred_element_type=F32) * scale
            dq_ref[:, lanes] = dq.astype(BF16)
            dk_acc += lax.dot_general(dsb, qq, _DIMS["tn"], preferred_element_type=F32) * scale
            dv_acc += lax.dot_general(p.astype(BF16), dd, _DIMS["tn"], preferred_element_type=F32)
            ds_ref[2 * pr:2 * pr + 2] += ds.reshape(2, BLOCK, 3 * BLOCK)
            head = GROUP * j + 2 * pr
            p_sink = jnp.exp(jnp.where(low, sink_ref[head], sink_ref[head + 1]) - l2)
            dsink_ref[:, lanes] += jnp.sum(-p_sink * jnp.where(low, delta_a, delta_b), axis=0, keepdims=True)
        for t, start in enumerate(starts):
            dke_ref[pl.ds(start, BLOCK), :] += dk_acc[t * BLOCK:(t + 1) * BLOCK]
            dve_ref[pl.ds(start, BLOCK), :] += dv_acc[t * BLOCK:(t + 1) * BLOCK]

    kv_out = jax.ShapeDtypeStruct((T, N_KV_HEADS * PAIR), F32)
    return pl.pallas_call(
        body, name=name, grid=(N_KV_HEADS, nb),
        in_specs=[sp["sink"], sp["q"], sp["kv"], sp["kv"], sp["bias"], sp["o"], sp["o"], sp["o"]],
        out_specs=[sp["o"], sp["kv"], sp["kv"], sp["bias"],
                   pl.BlockSpec((1, GROUP * HEAD_DIM), lambda j, i: (0, j))],
        out_shape=[jax.ShapeDtypeStruct((T, N_HEADS * HEAD_DIM), BF16), kv_out, kv_out,
                   jax.ShapeDtypeStruct((N_HEADS, BLOCK, 3 * BLOCK), F32),
                   jax.ShapeDtypeStruct((1, N_HEADS * HEAD_DIM), F32)],
        compiler_params=_params("parallel", "arbitrary"),
    )(sink, proj, kexp, vexp, bias, out, lse, dout)


GATE_COLS = 512


def _sigmoid(x):
    return 1.0 / (1.0 + jnp.exp(-x))


def _gate_specs(D, gate_off):
    nc = D // GATE_COLS
    base = gate_off // GATE_COLS
    return [pl.BlockSpec((ROWS, GATE_COLS), lambda i, c=base + g * nc + h: (i, c)) for g in range(3) for h in range(nc)]


def _merge_fwd(proj, gate_off, ya, yp, yt, scale, name):
    T, D = ya.shape
    nc = D // GATE_COLS

    def body(*refs):
        gates = refs[:3 * nc]
        ya_ref, yp_ref, yt_ref, s_ref, o_ref = refs[3 * nc:]
        for h in range(nc):
            cols = slice(h * GATE_COLS, (h + 1) * GATE_COLS)
            merged = (_sigmoid(_f32(gates[h])) * ya_ref[:, cols].astype(F32)
                      + _sigmoid(_f32(gates[nc + h])) * (yp_ref[:, cols].astype(F32) * s_ref[:, cols])
                      + _sigmoid(_f32(gates[2 * nc + h])) * yt_ref[:, cols].astype(F32))
            o_ref[:, cols] = merged.astype(BF16)

    yspec = _row_spec(D)
    return pl.pallas_call(
        body, name=name, grid=(T // ROWS,),
        in_specs=_gate_specs(D, gate_off) + [yspec, yspec, yspec, _const_spec((1, D))], out_specs=yspec,
        out_shape=jax.ShapeDtypeStruct((T, D), BF16), compiler_params=_params("parallel"),
    )(*([proj] * (3 * nc)), ya, yp, yt, scale)


def _merge_bwd(proj, gate_off, ya, yp, yt, scale, dm, name):
    T, D = ya.shape
    nc = D // GATE_COLS

    def body(*refs):
        gates = refs[:3 * nc]
        ya_ref, yp_ref, yt_ref, s_ref, dm_ref, dg_ref, dya_ref, dyp_ref, dyt_ref, ds_ref = refs[3 * nc:]
        parts = []
        for h in range(nc):
            cols = slice(h * GATE_COLS, (h + 1) * GATE_COLS)
            dm_v = dm_ref[:, cols].astype(F32)
            sa, sp, st = _sigmoid(_f32(gates[h])), _sigmoid(_f32(gates[nc + h])), _sigmoid(_f32(gates[2 * nc + h]))
            yp_v, s_v = yp_ref[:, cols].astype(F32), s_ref[:, cols]
            dg_ref[:, h * GATE_COLS:(h + 1) * GATE_COLS] = (dm_v * ya_ref[:, cols].astype(F32) * sa * (1.0 - sa)).astype(BF16)
            dg_ref[:, D + h * GATE_COLS:D + (h + 1) * GATE_COLS] = (dm_v * (yp_v * s_v) * sp * (1.0 - sp)).astype(BF16)
            dg_ref[:, 2 * D + h * GATE_COLS:2 * D + (h + 1) * GATE_COLS] = (
                dm_v * yt_ref[:, cols].astype(F32) * st * (1.0 - st)).astype(BF16)
            dya_ref[:, cols] = (dm_v * sa).astype(BF16)
            dyps = dm_v * sp
            dyp_ref[:, cols] = (dyps * s_v).astype(BF16)
            dyt_ref[:, cols] = (dm_v * st).astype(BF16)
            parts.append(jnp.sum(dyps * yp_v, axis=0, keepdims=True))
        _accumulate(ds_ref, jnp.concatenate(parts, axis=1))

    yspec = _row_spec(D)
    out = jax.ShapeDtypeStruct((T, D), BF16)
    return pl.pallas_call(
        body, name=name, grid=(T // ROWS,),
        in_specs=_gate_specs(D, gate_off) + [yspec, yspec, yspec, _const_spec((1, D)), yspec],
        out_specs=[_row_spec(3 * D), yspec, yspec, yspec, _const_spec((1, D))],
        out_shape=[jax.ShapeDtypeStruct((T, 3 * D), BF16), out, out, out, jax.ShapeDtypeStruct((1, D), F32)],
        compiler_params=_params("arbitrary"),
    )(*([proj] * (3 * nc)), ya, yp, yt, scale, dm)


def _swiglu_fwd(gu, name):
    T = gu.shape[0]
    F = gu.shape[1] // 2

    def body(gu_ref, o_ref):
        g = gu_ref[:, 0:F].astype(F32)
        o_ref[...] = (g * _sigmoid(g) * gu_ref[:, F:2 * F].astype(F32)).astype(BF16)

    return pl.pallas_call(
        body, name=name, grid=(T // ROWS,), in_specs=[_row_spec(2 * F)], out_specs=_row_spec(F),
        out_shape=jax.ShapeDtypeStruct((T, F), BF16), compiler_params=_params("parallel"),
    )(gu)


def _swiglu_bwd(gu, dact, name):
    T = gu.shape[0]
    F = gu.shape[1] // 2

    def body(gu_ref, d_ref, o_ref):
        g, d = gu_ref[:, 0:F].astype(F32), d_ref[...].astype(F32)
        sg = _sigmoid(g)
        o_ref[:, 0:F] = (d * gu_ref[:, F:2 * F].astype(F32) * sg * (1.0 + g * (1.0 - sg))).astype(BF16)
        o_ref[:, F:2 * F] = (d * g * sg).astype(BF16)

    return pl.pallas_call(
        body, name=name, grid=(T // ROWS,), in_specs=[_row_spec(2 * F), _row_spec(F)], out_specs=_row_spec(2 * F),
        out_shape=jax.ShapeDtypeStruct((T, 2 * F), BF16), compiler_params=_params("parallel"),
    )(gu, dact)


def _carried(plan, key, *args, **kwargs):
    job = plan.job(key) if plan is not None else None
    if job is None:
        return _matmul(*args, **kwargs)
    out, extra = _matmul(*args, comm=job, **kwargs)
    plan.done(key, extra)
    return out


def _local_step(x, target, wts, small, hooks=None):
    T, D = x.shape
    depth = small["g_mix"].shape[0]
    wts = list(wts) + [None] * (depth - len(wts))
    gate_off = wts[0]["w_inT"].shape[0] - 3 * D
    q_off = 4 * D
    bucket = _bucket_table()
    bias = _bias_table(small["rel_bias"], bucket, "bias_table")

    saved = []
    for l in range(depth):
        n = f"l{l}_"
        if hooks is not None and l > 0:
            wts[l] = hooks.weights(l)
        w = wts[l]
        plan = hooks.plan_fwd(l) if hooks is not None else None
        h = _rms_fwd(x, small["g_mix"][l], n + "rms_mix")
        proj = _carried(plan, "proj", h, w["w_inT"], "nt", BF16, n + "proj")
        z, p = _mixer_fwd(proj, small["conv_w"][l], n + "mixer")
        kexp, vexp = _kv_expand(proj, q_off + D, n + "kv_expand")
        sink = small["attn_sink"][l]
        job = plan.job("attn") if plan is not None else None
        att, lse, extra = _attn_fwd(proj, q_off, kexp, vexp, bias, sink, n + "attn", comm=job)
        if job is not None:
            plan.done("attn", extra)
        ya =_matmul(z, w["w_a_out"], "nn", BF16, n + "ya")
        yp = _pool_mm(p, w["w_pool"], "nn", BF16, n + "yp")
        yt = _matmul(att, w["w_attn_out"], "nn", BF16, n + "yt")
        merged = _merge_fwd(proj, gate_off, ya, yp, yt, small["pool_scale"][l], n + "merge")
        x1 = _matmul(merged, w["w_o"], "nn", F32, n + "x1", res=x)
        h2 = _rms_fwd(x1, small["g_ffn"][l], n + "rms_ffn")
        gu = _carried(plan, "gu", h2, w["w_guT"], "nt", BF16, n + "gu")
        act = _swiglu_fwd(gu, n + "swiglu")
        ff = w["w_down"].shape[0]
        x2 = _carried(plan, "x2", act, w["w_down"], "nn", F32, n + "x2", res=x1, tn_cap=512, tk_cap=ff)
        saved.append(dict(x=x, h=h, proj=proj, z=z, p=p, kexp=kexp, vexp=vexp, sink=sink, lse=lse, att=att,
                          ya=ya, yp=yp, yt=yt, merged=merged, x1=x1, h2=h2, gu=gu, act=act))
        x = x2

    loss, dx, dxb, dg_final = _loss_head(x, small["g_final"], target, "loss_head")

    gw = [None] * depth
    gs = {k_: [None] * depth for k_ in ("conv_w", "pool_scale", "g_mix", "g_ffn", "attn_sink")}
    ds_total = None
    for l in reversed(range(depth)):
        n = f"l{l}_b_"
        s, w, g = saved[l], wts[l], {}
        plan = hooks.plan_bwd(l) if hooks is not None else None
        ff = w["w_down"].shape[0]
        g["w_down"] = _carried(plan, "dw_down", s["act"], dxb, "tn", BF16, n + "dw_down", tm_cap=ff, tk_cap=512)
        dact = _matmul(dxb, w["w_down"], "nt", BF16, n + "dact", tm_cap=512, tn_cap=ff)
        dgu = _swiglu_bwd(s["gu"], dact, n + "swiglu")
        g["w_guT"] = _carried(plan, "dw_gu", dgu, s["h2"], "tn", BF16, n + "dw_gu", tm_cap=512)
        dh2 = _carried(plan, "dh2", dgu, w["w_guT"], "nn", F32, n + "dh2", tn_cap=512, tk_cap=ff)
        dx1, dx1b, gs["g_ffn"][l] = _rms_bwd(s["x1"], small["g_ffn"][l], dh2, dx, n + "rms_ffn")
        g["w_o"] = _matmul(s["merged"], dx1b, "tn", BF16, n + "dw_o")
        dm = _matmul(dx1b, w["w_o"], "nt", BF16, n + "dmerged")
        dgates, dya, dyp, dyt, gs["pool_scale"][l] = _merge_bwd(
            s["proj"], gate_off, s["ya"], s["yp"], s["yt"], small["pool_scale"][l], dm, n + "merge")
        g["w_a_out"] = _matmul(s["z"], dya, "tn", BF16, n + "dw_a_out")
        dz = _matmul(dya, w["w_a_out"], "nt", BF16, n + "dz")
        g["w_pool"] = _pool_mm(s["p"], dyp, "tn", F32, n + "dw_pool")
        dp = _pool_mm(dyp, w["w_pool"], "nt", BF16, n + "dp")
        g["w_attn_out"] = _matmul(s["att"], dyt, "tn", BF16, n + "dw_attn_out")
        datt = _matmul(dyt, w["w_attn_out"], "nt", BF16, n + "datt")
        dmix, gs["conv_w"][l] = _mixer_bwd(s["proj"], small["conv_w"][l], dz, dp, n + "mixer")
        dq, dke, dve, ds_sum, dsink = _attn_bwd(s["proj"], q_off, s["kexp"], s["vexp"], bias, s["sink"], s["att"],
                                                s["lse"], datt, n + "attn")
        gs["attn_sink"][l] = dsink.reshape(N_HEADS, HEAD_DIM)[:, 0]
        ds_total = ds_sum if ds_total is None else ds_total + ds_sum
        dproj = jnp.concatenate([dmix, dq, _kv_fold(dke, dve, n + "kv_fold"), dgates], axis=1)
        g["w_inT"] = _carried(plan, "dw_in", dproj, s["h"], "tn", BF16, n + "dw_in", tm_cap=512)
        dh = _matmul(dproj, w["w_inT"], "nn", F32, n + "dh", tk_cap=2816)
        dx, dxb, gs["g_mix"][l] = _rms_bwd(s["x"], small["g_mix"][l], dh, dx1, n + "rms_mix")
        gw[l] = g
        if hooks is not None:
            hooks.grads(l, g)

    d_rel =_bias_grad(ds_total, bucket, "bias_grad")[:, :, 0].T
    gs = {k_: jnp.stack(v_) for k_, v_ in gs.items()}
    gs["rel_bias"] = d_rel
    gs["g_final"] = dg_final
    return loss, dx, gw, gs


def _place():
    return lax.axis_index("x"), lax.axis_index("y"), lax.axis_index("c")


class _GatherJob:
    def __init__(self, parts):
        n = len(parts)
        self.n = n
        self.ins = list(parts)
        self.outs = [jax.ShapeDtypeStruct((N_DEV,) + p.shape, p.dtype) for p in parts]
        self.sems = [pltpu.SemaphoreType.DMA((7 * n,)), pltpu.SemaphoreType.DMA((7 * n,)), pltpu.SemaphoreType.DMA((n,))]

    def _copies(self, ins, outs, sems):
        send_sems, recv_sems, local_sems = sems
        x, y, c = _place()
        me, sibling = (x, y, c), (x, y, 1 - c)
        chips = [(1 - x, y), (x, 1 - y), (1 - x, 1 - y)]

        def rows(t, px, py, pc):
            return outs[t].at[4 * px + 2 * py + pc]

        def copy(t, k, block, to, src=None):
            return pltpu.make_async_remote_copy(
                src_ref=rows(t, *block) if src is None else src, dst_ref=rows(t, *block),
                send_sem=send_sems.at[7 * t + k], recv_sem=recv_sems.at[7 * t + k], device_id=to, device_id_type=MESH)

        ts = range(self.n)
        own = [pltpu.make_async_copy(ins[t], rows(t, *me), local_sems.at[t]) for t in ts]
        first = [copy(t, 0, me, sibling, src=ins[t]) for t in ts]
        first += [copy(t, 1 + j, me, (*chip, c), src=ins[t]) for t in ts for j, chip in enumerate(chips)]
        landed = [copy(t, 1 + j, (*chip, c), me) for j, chip in enumerate(chips) for t in ts]
        passed = [copy(t, 4 + j, (*chip, c), sibling) for j, chip in enumerate(chips) for t in ts]
        last = [copy(t, 0, sibling, me) for t in ts]
        last += [copy(t, 4 + j, (*chip, 1 - c), me) for t in ts for j, chip in enumerate(chips)]
        return own, first, landed, passed, last

    def start(self, ins, outs, sems):
        own, first, _, _, _ = self._copies(ins, outs, sems)
        for cp in own + first:
            cp.start()

    def mid(self, ins, outs, sems):
        _, _, landed, passed, _ = self._copies(ins, outs, sems)
        for arrived, onward in zip(landed, passed):
            arrived.wait_recv()
            onward.start()

    def finish(self, ins, outs, sems):
        own, first, _, passed, last = self._copies(ins, outs, sems)
        for cp in last:
            cp.wait_recv()
        for cp in first + passed:
            cp.wait_send()
        for cp in own:
            cp.wait()


class _SwapJob:
    def __init__(self, g):
        self.ins = [g]
        self.outs = [jax.ShapeDtypeStruct(g.shape[:1] + g.shape[2:], g.dtype)]
        self.sems = [pltpu.SemaphoreType.DMA, pltpu.SemaphoreType.DMA]

    def _copy(self, ins, outs, sems):
        x, y, c = _place()
        return pltpu.make_async_remote_copy(src_ref=ins[0].at[pl.ds(0, ins[0].shape[0]), 1 - c], dst_ref=outs[0],
                                            send_sem=sems[0], recv_sem=sems[1], device_id=(x, y, 1 - c),
                                            device_id_type=MESH)

    def start(self, ins, outs, sems):
        self._copy(ins, outs, sems).start()

    def mid(self, ins, outs, sems):
        pass

    def finish(self, ins, outs, sems):
        self._copy(ins, outs, sems).wait()


class _ExchangeJob:
    def __init__(self, p, row0, rows):
        self.row0, self.rows = row0, rows
        self.ins = [p]
        self.outs = [jax.ShapeDtypeStruct((3, rows) + p.shape[2:], p.dtype)]
        self.sems = [pltpu.SemaphoreType.DMA((3,)), pltpu.SemaphoreType.DMA((3,))]

    def _copies(self, ins, outs, sems):
        x, y, c = _place()
        chips = [(1 - x, y), (x, 1 - y), (1 - x, 1 - y)]
        return [pltpu.make_async_remote_copy(
            src_ref=ins[0].at[2 * px + py, pl.ds(self.row0, self.rows)], dst_ref=outs[0].at[k],
            send_sem=sems[0].at[k], recv_sem=sems[1].at[k], device_id=(px, py, c), device_id_type=MESH)
            for k, (px, py) in enumerate(chips)]

    def start(self, ins, outs, sems):
        for cp in self._copies(ins, outs, sems):
            cp.start()

    def mid(self, ins, outs, sems):
        pass

    def finish(self, ins, outs, sems):
        for cp in self._copies(ins, outs, sems):
            cp.wait()


def _all_gather(v, name):
    def body(x_ref, out_ref, send_sems, recv_sems, local_sem):
        x, y, c = _place()
        me, sibling = (x, y, c), (x, y, 1 - c)
        chips = [(1 - x, y), (x, 1 - y), (1 - x, 1 - y)]

        def rows(px, py, pc):
            return out_ref.at[4 * px + 2 * py + pc]

        def copy(k, block, to, src=None):
            return pltpu.make_async_remote_copy(
                src_ref=rows(*block) if src is None else src, dst_ref=rows(*block),
                send_sem=send_sems.at[k], recv_sem=recv_sems.at[k], device_id=to, device_id_type=MESH)

        mine = pltpu.make_async_copy(x_ref, rows(*me), local_sem)
        mine.start()
        first = [copy(0, me, sibling, src=x_ref)]
        first += [copy(1 + j, me, (*chip, c), src=x_ref) for j, chip in enumerate(chips)]
        for cp in first:
            cp.start()
        passed = [copy(4 + j, (*chip, c), sibling) for j, chip in enumerate(chips)]
        for j, chip in enumerate(chips):
            copy(1 + j, (*chip, c), me).wait_recv()
            passed[j].start()
        copy(0, sibling, me).wait_recv()
        for j, chip in enumerate(chips):
            copy(4 + j, (*chip, 1 - c), me).wait_recv()
        for cp in first + passed:
            cp.wait_send()
        mine.wait()

    return pl.pallas_call(
        body, name=name, in_specs=[HBM_SPEC], out_specs=HBM_SPEC,
        out_shape=jax.ShapeDtypeStruct((N_DEV,) + v.shape, v.dtype),
        scratch_shapes=[pltpu.SemaphoreType.DMA((7,)), pltpu.SemaphoreType.DMA((7,)), pltpu.SemaphoreType.DMA],
    )(v)


def _all_gather_many(parts, name):
    n = len(parts)

    def body(*refs):
        ins, outs = refs[:n], refs[n:2 * n]
        send_sems, recv_sems, local_sems = refs[2 * n:]
        x, y, c = _place()
        me, sibling = (x, y, c), (x, y, 1 - c)
        chips = [(1 - x, y), (x, 1 - y), (1 - x, 1 - y)]

        def rows(t, px, py, pc):
            return outs[t].at[4 * px + 2 * py + pc]

        def copy(t, k, block, to, src=None):
            return pltpu.make_async_remote_copy(
                src_ref=rows(t, *block) if src is None else src, dst_ref=rows(t, *block),
                send_sem=send_sems.at[7 * t + k], recv_sem=recv_sems.at[7 * t + k], device_id=to, device_id_type=MESH)

        mine = [pltpu.make_async_copy(ins[t], rows(t, *me), local_sems.at[t]) for t in range(n)]
        sends = []
        for t in range(n):
            mine[t].start()
            sends.append(copy(t, 0, me, sibling, src=ins[t]))
            sends += [copy(t, 1 + j, me, (*chip, c), src=ins[t]) for j, chip in enumerate(chips)]
        for cp in sends:
            cp.start()
        for j, chip in enumerate(chips):
            for t in range(n):
                copy(t, 1 + j, (*chip, c), me).wait_recv()
                passed = copy(t, 4 + j, (*chip, c), sibling)
                passed.start()
                sends.append(passed)
        for t in range(n):
            copy(t, 0, sibling, me).wait_recv()
            for j, chip in enumerate(chips):
                copy(t, 4 + j, (*chip, 1 - c), me).wait_recv()
        for cp in sends:
            cp.wait_send()
        for cp in mine:
            cp.wait()

    return pl.pallas_call(
        body, name=name, in_specs=[HBM_SPEC] * n, out_specs=[HBM_SPEC] * n,
        out_shape=[jax.ShapeDtypeStruct((N_DEV,) + p.shape, p.dtype) for p in parts],
        scratch_shapes=[pltpu.SemaphoreType.DMA((7 * n,)), pltpu.SemaphoreType.DMA((7 * n,)),
                        pltpu.SemaphoreType.DMA((n,))],
    )(*parts)


def _run_job(job, name):
    n_in, n_out = len(job.ins), len(job.outs)

    def body(*refs):
        ins, outs, sems = refs[:n_in], refs[n_in:n_in + n_out], refs[n_in + n_out:]
        job.start(ins, outs, sems)
        job.mid(ins, outs, sems)
        job.finish(ins, outs, sems)

    return pl.pallas_call(
        body, name=name, in_specs=[HBM_SPEC] * n_in, out_specs=[HBM_SPEC] * n_out, out_shape=list(job.outs),
        scratch_shapes=list(job.sems),
    )(*job.ins)


def _chip_exchange(p, name):
    def body(p_ref, out_ref, send_sems, recv_sems):
        x, y, c = _place()
        chips = [(1 - x, y), (x, 1 - y), (1 - x, 1 - y)]
        copies = [pltpu.make_async_remote_copy(
            src_ref=p_ref.at[2 * px + py], dst_ref=out_ref.at[k], send_sem=send_sems.at[k], recv_sem=recv_sems.at[k],
            device_id=(px, py, c), device_id_type=MESH) for k, (px, py) in enumerate(chips)]
        for cp in copies:
            cp.start()
        for cp in copies:
            cp.wait()

    return pl.pallas_call(
        body, name=name, in_specs=[HBM_SPEC], out_specs=HBM_SPEC,
        out_shape=jax.ShapeDtypeStruct((3,) + p.shape[1:], p.dtype),
        scratch_shapes=[pltpu.SemaphoreType.DMA((3,)), pltpu.SemaphoreType.DMA((3,))],
    )(p)


def _sum_parts(own, index, others, out_dtype, name, own_row0=0, own_step=0):
    R = others.shape[1]
    rows = _tile(R, 512)
    k = others.shape[0]
    assert own_row0 % rows == 0 and own.shape[1] % rows == 0
    blk0 = own_row0 // rows
    per_own = own.shape[1] // rows

    def own_block(i, idx):
        if own_step:
            return (idx[0] + own_step * (i // per_own), i % per_own, 0)
        return (idx[0], blk0 + i, 0)

    def body(idx_ref, own_ref, *refs):
        del idx_ref
        acc = own_ref[...].astype(F32)
        for r in refs[:k]:
            acc = acc + r[...].astype(F32)
        refs[k][...] = acc.astype(out_dtype)

    grid_spec = pltpu.PrefetchScalarGridSpec(
        num_scalar_prefetch=1, grid=(R // rows,),
        in_specs=[pl.BlockSpec((None, rows, LANES), own_block)]
        + [pl.BlockSpec((None, rows, LANES), lambda i, idx, j=j: (j, i, 0)) for j in range(k)],
        out_specs=pl.BlockSpec((rows, LANES), lambda i, idx: (i, 0)))
    return pl.pallas_call(
        body, name=name, grid_spec=grid_spec,
        out_shape=jax.ShapeDtypeStruct((R, LANES), out_dtype), compiler_params=_params("parallel"),
    )(jnp.reshape(index, (1,)).astype(jnp.int32), own, *([others] * k))


def _adamw(w, g, m, v, name):
    shape = w.shape
    cols = shape[-1]
    rows_total = w.size // cols
    w2, g2, m2, v2 = (a.reshape(rows_total, cols) for a in (w, g, m, v))
    rows = rows_total
    if rows_total > ROWS:
        rows = next(r for r in range(ROWS, 0, -8) if rows_total % r == 0)

    def body(w_ref, g_ref, m_ref, v_ref, d_ref, nm_ref, nv_ref):
        gv = g_ref[...]
        nm = ADAM_B1 * m_ref[...] + (1.0 - ADAM_B1) * gv
        nv = ADAM_B2 * v_ref[...] + (1.0 - ADAM_B2) * (gv * gv)
        m_hat = nm / (1.0 - ADAM_B1 ** ADAM_STEP)
        v_hat = nv / (1.0 - ADAM_B2 ** ADAM_STEP)
        d_ref[...] = -ADAM_LR * (m_hat / (jnp.sqrt(v_hat) + ADAM_EPS) + ADAM_WD * w_ref[...])
        nm_ref[...] = nm
        nv_ref[...] = nv

    spec = pl.BlockSpec((rows, cols), lambda i: (i, 0))
    out = jax.ShapeDtypeStruct((rows_total, cols), F32)
    d, nm, nv = pl.pallas_call(
        body, name=name, grid=(rows_total // rows,), in_specs=[spec] * 4, out_specs=[spec] * 3,
        out_shape=[out, out, out], compiler_params=_params("parallel"),
    )(w2, g2, m2, v2)
    return d.reshape(shape), nm.reshape(shape), nv.reshape(shape)


BIG = ("w_in", "w_a_out", "w_pool", "w_attn_out", "w_o", "w_gu", "w_down")


LOCAL = dict(w_in="w_inT", w_a_out="w_a_out", w_pool="w_pool", w_attn_out="w_attn_out", w_o="w_o", w_gu="w_guT",
             w_down="w_down")


def _shard_rows(w, l):
    out = []
    for name in BIG:
        a = w[name][l]
        if name in ("w_in", "w_gu"):
            a = a.T
        elif name == "w_pool":
            a = a.reshape(-1, a.shape[-1])
        out.append(a.astype(BF16))
    return out


def _full_weights(names, gathered, w):
    out = {}
    for name, g in zip(names, gathered):
        if name == "w_pool":
            G, rg, cg = w[name].shape[1:]
            out[name] = jnp.transpose(g.reshape(N_DEV, G, rg, cg), (1, 0, 2, 3)).reshape(G, N_DEV * rg, cg)
        else:
            out[LOCAL[name]] = g.reshape(N_DEV * g.shape[1], g.shape[2])
    return out


def _split_grads(g, w):
    parts, spans, at = [], {}, 0
    for name in BIG:
        a = g[LOCAL[name]].astype(BF16)
        if name == "w_pool":
            G, rg, cg = w[name].shape[1:]
            a = jnp.transpose(a.reshape(G, N_DEV, rg, cg), (1, 0, 2, 3))
        a = a.reshape(N_DEV, -1, LANES)
        spans[name] = (at, at + a.shape[1])
        at += a.shape[1]
        parts.append(a)
    return jnp.concatenate(parts, axis=1), spans


def _own_grads(g_layers, spans, w):
    L = len(g_layers)
    g3 = jnp.stack(g_layers)
    out = {}
    for name in BIG:
        a = g3[:, spans[name][0]:spans[name][1]]
        if name in ("w_in", "w_gu"):
            sh = w[name].shape
            a = jnp.swapaxes(a.reshape(L, sh[2], sh[1]), 1, 2)
        out[name] = a.reshape(w[name].shape)
    return out


SQUARE = ("w_a_out", "w_pool", "w_attn_out", "w_o")


class _Prefetch:
    def __init__(self, schedule, assign):
        self.schedule, self.assign = schedule, assign

    def job(self, key):
        if key not in self.assign:
            return None
        layer, names = self.assign[key]
        shards = dict(zip(BIG, _shard_rows(self.schedule.w, layer)))
        return _GatherJob([shards[name] for name in names])

    def done(self, key, outs):
        layer, names = self.assign[key]
        self.schedule.arrived(layer, names, outs)


class _Reduce:
    CHUNK_ROWS = 1024
    CARRIERS = ("dw_gu", "dw_in", "dh2")

    def __init__(self, split, place, tag):
        self.split, self.tag = split, tag
        self.core, self.chip = place[2], 2 * place[0] + place[1]
        self.rows = split.shape[1]
        self.chunks = [(r, min(self.CHUNK_ROWS, self.rows - r)) for r in range(0, self.rows, self.CHUNK_ROWS)]
        assert len(self.chunks) <= len(self.CARRIERS)
        self.sums = [None] * len(self.chunks)

    def _swap_job(self):
        return _SwapJob(self.split.reshape(4, 2, self.rows, LANES))

    def _pair_sum(self, from_sibling):
        pair = _sum_parts(self.split, self.core, from_sibling.reshape(1, 4 * self.rows, LANES), BF16,
                          self.tag + "pair_sum", own_step=2)
        self.pair = pair.reshape(4, self.rows, LANES)

    def _chip_sum(self, n, from_chips):
        self.sums[n] = _sum_parts(self.pair, self.chip, from_chips, F32, f"{self.tag}chip_sum{n}",
                                  own_row0=self.chunks[n][0])

    def job(self, key):
        if key == "dw_down":
            return self._swap_job()
        if key in self.CARRIERS[:len(self.chunks)]:
            return _ExchangeJob(self.pair, *self.chunks[self.CARRIERS.index(key)])
        return None

    def done(self, key, outs):
        if key == "dw_down":
            self._pair_sum(outs[0])
        else:
            self._chip_sum(self.CARRIERS.index(key), outs[0])

    def run(self):
        self._pair_sum(_run_job(self._swap_job(), self.tag + "reduce_pair")[0])
        self.chunks, self.sums = [(0, self.rows)], [None]
        self._chip_sum(0, _run_job(_ExchangeJob(self.pair, 0, self.rows), self.tag + "reduce_chips")[0])
        return self.result()

    def result(self):
        return self.sums[0] if len(self.sums) == 1 else jnp.concatenate(self.sums, axis=0)


class _Schedule:
    def __init__(self, w, place):
        self.w, self.place = w, place
        self.depth = w["w_in"].shape[0]
        self.full = [{} for _ in range(self.depth)]
        self.reduce = {}
        self.g_layers = [None] * self.depth
        self.spans = None

    def arrived(self, layer, names, gathered):
        self.full[layer].update(_full_weights(names, gathered, self.w))

    def plan_fwd(self, l):
        nxt = l + 1
        if l == 0:
            assign = dict(proj=(0, tuple(n for n in BIG if n != "w_in")))
            if nxt < self.depth:
                assign.update(attn=(nxt, ("w_in",) + SQUARE), gu=(nxt, ("w_gu",)), x2=(nxt, ("w_down",)))
        elif nxt < self.depth:
            assign = dict(proj=(nxt, ("w_in",)), attn=(nxt, ("w_down",) + SQUARE), gu=(nxt, ("w_gu",)))
        else:
            return None
        return _Prefetch(self, assign)

    def weights(self, l):
        return self.full[l]

    def plan_bwd(self, l):
        return self.reduce.get(l + 1)

    def grads(self, l, g):
        if l + 1 in self.reduce:
            self.g_layers[l + 1] = self.reduce[l + 1].result()
        split, self.spans = _split_grads(g, self.w)
        self.reduce[l] = _Reduce(split, self.place, f"l{l}_")
        if l == 0:
            self.g_layers[0] = self.reduce[0].run()


SMALL_ROWS = 32


def _pack_small(gs, L, D):
    rows = [gs["pool_scale"].reshape(L, D), gs["g_mix"].reshape(L, D), gs["g_ffn"].reshape(L, D),
            gs["g_final"].reshape(1, D), gs["conv_w"][:, :3].reshape(3 * L, D),
            jnp.pad(gs["attn_sink"].reshape(1, -1), ((0, 0), (0, D - L * N_HEADS))),
            jnp.pad(gs["rel_bias"].reshape(1, -1), ((0, 0), (0, D - N_BUCKETS * N_HEADS)))]
    a = jnp.concatenate(rows, axis=0)
    return jnp.pad(a, ((0, SMALL_ROWS - a.shape[0]), (0, 0)))


def _unpack_small(a, L, D):
    g = {}
    g["pool_scale"] = a[0:L]
    g["g_mix"] = a[L:2 * L]
    g["g_ffn"] = a[2 * L:3 * L]
    g["g_final"] = a[3 * L]
    g["conv_w"] = a[3 * L + 1:6 * L + 1].reshape(L, 3, 1, D)
    g["attn_sink"] = a[6 * L + 1, :L * N_HEADS].reshape(L, N_HEADS)
    g["rel_bias"] = a[6 * L + 2, :N_BUCKETS * N_HEADS].reshape(N_BUCKETS, N_HEADS)
    return g


WEIGHTS = ("w_in", "conv_w", "w_a_out", "w_pool", "pool_scale", "w_attn_out", "attn_sink", "w_o", "g_mix", "g_ffn",
           "w_gu", "w_down", "rel_bias", "g_final")


def kernel(x, w_in, conv_w, w_a_out, w_pool, pool_scale, w_attn_out, attn_sink, w_o, g_mix, g_ffn, w_gu, w_down, rel_bias, g_final, loss_target, m_w_in, m_conv_w, m_w_a_out, m_w_pool, m_pool_scale, m_w_attn_out, m_attn_sink, m_w_o, m_g_mix, m_g_ffn, m_w_gu, m_w_down, m_rel_bias, m_g_final, v_w_in, v_conv_w, v_w_a_out, v_w_pool, v_pool_scale, v_w_attn_out, v_attn_sink, v_w_o, v_g_mix, v_g_ffn, v_w_gu, v_w_down, v_rel_bias, v_g_final):
    w = dict(w_in=w_in, conv_w=conv_w, w_a_out=w_a_out, w_pool=w_pool, pool_scale=pool_scale, w_attn_out=w_attn_out,
             attn_sink=attn_sink, w_o=w_o, g_mix=g_mix, g_ffn=g_ffn, w_gu=w_gu, w_down=w_down, rel_bias=rel_bias,
             g_final=g_final)
    m = dict(w_in=m_w_in, conv_w=m_conv_w, w_a_out=m_w_a_out, w_pool=m_w_pool, pool_scale=m_pool_scale,
             w_attn_out=m_w_attn_out, attn_sink=m_attn_sink, w_o=m_w_o, g_mix=m_g_mix, g_ffn=m_g_ffn, w_gu=m_w_gu,
             w_down=m_w_down, rel_bias=m_rel_bias, g_final=m_g_final)
    v = dict(w_in=v_w_in, conv_w=v_conv_w, w_a_out=v_w_a_out, w_pool=v_w_pool, pool_scale=v_pool_scale,
             w_attn_out=v_w_attn_out, attn_sink=v_attn_sink, w_o=v_w_o, g_mix=v_g_mix, g_ffn=v_g_ffn, w_gu=v_w_gu,
             w_down=v_w_down, rel_bias=v_rel_bias, g_final=v_g_final)
    T, D = x.shape[1], x.shape[2]
    L = w_in.shape[0]
    cx, cy, cc = _place()

    schedule = _Schedule(w, (cx, cy, cc))
    schedule.arrived(0, ("w_in",), _run_job(_GatherJob(_shard_rows(w, 0)[:1]), "gather_w_in_l0"))
    cw = jnp.pad(conv_w.reshape(L * 3, -1), ((0, 16 - L * 3), (0, 0)))
    cw = _all_gather(cw, "gather_conv_w")
    cw = jnp.transpose(cw, (1, 0, 2)).reshape(16, -1)[:L * 3].reshape(L, 3, -1)
    small = dict(conv_w=jnp.pad(cw, ((0, 0), (0, 5), (0, 0))), pool_scale=pool_scale.reshape(L, 1, D),
                 g_mix=g_mix.reshape(L, 1, D), g_ffn=g_ffn.reshape(L, 1, D), attn_sink=attn_sink,
                 rel_bias=rel_bias, g_final=g_final.reshape(1, D))

    loss, dx, _, gs = _local_step(x[0], loss_target[0], [schedule.full[0]], small, schedule)
    loss = lax.psum(loss[0, 0], ("x", "y", "c"))
    grads = _own_grads(schedule.g_layers, schedule.spans, w)

    small_all = _all_gather(_pack_small(gs, L, D), "gather_small")
    small_sum = _sum_parts(small_all, jnp.int32(0), small_all[1:], F32, "small_sum")
    gsm = _unpack_small(small_sum, L, D)
    W8 = D // N_DEV
    dev = 4 * cx + 2 * cy + cc
    gsm["conv_w"] = lax.dynamic_slice_in_dim(gsm["conv_w"], dev * W8, W8, axis=3)
    grads.update(gsm)

    deltas, new_m, new_v = {}, {}, {}
    for name in WEIGHTS:
        deltas[name], new_m[name], new_v[name] = _adamw(w[name], grads[name], m[name], v[name], "adamw_" + name)

    return (loss, dx[None], *[grads[n] for n in WEIGHTS], *[deltas[n] for n in WEIGHTS],
            *[new_m[n] for n in WEIGHTS], *[new_v[n] for n in WEIGHTS])
```

```python
import functools
import math

import jax
import jax.numpy as jnp
from jax import lax
from jax.experimental import pallas as pl
from jax.experimental.pallas import tpu as pltpu

F32 = jnp.float32
BF16 = jnp.bfloat16
MESH = pl.DeviceIdType.MESH

N_DEV = 8
N_HEADS = 16
N_KV_HEADS = 4
HEAD_DIM = 64
GROUP = N_HEADS // N_KV_HEADS
BLOCK = 128
WINDOW = 128
N_BUCKETS = 32
MAX_DISTANCE = 128
POOL_WINDOWS = (2, 4, 8, 16)
POOL_GROUPS = 4
HALO = 8
EPS = 1e-6
NEG_INF = -1e30

ADAM_LR = 0.001
ADAM_B1 = 0.9
ADAM_B2 = 0.999
ADAM_EPS = 1e-08
ADAM_WD = 0.01
ADAM_STEP = 10

LANES = 1024
VMEM_LIMIT_BYTES = 48 * 1024 * 1024


def _params(*sem):
    return pltpu.CompilerParams(dimension_semantics=sem, vmem_limit_bytes=VMEM_LIMIT_BYTES)


def _tile(n, cap):
    if n <= cap:
        return n
    for t in range(cap - cap % 128, 0, -128):
        if n % t == 0:
            return t
    raise ValueError(f"no tile for {n}")


_DIMS = {"nn": (((1,), (0,)), ((), ())), "nt": (((1,), (1,)), ((), ())), "tn": (((0,), (0,)), ((), ()))}


HBM_SPEC = pl.BlockSpec(memory_space=pltpu.HBM)
ANY_SPEC = pl.BlockSpec(memory_space=pl.ANY)


def _matmul(a, b, mode, out_dtype, name, res=None, tm_cap=1024, tn_cap=1024, tk_cap=1024, comm=None):
    if mode == "tn":
        K, M = a.shape
    else:
        M, K = a.shape
    N = b.shape[0] if mode == "nt" else b.shape[1]
    tm, tn, tk = _tile(M, tm_cap), _tile(N, tn_cap), _tile(K, tk_cap)
    nk = K // tk
    a_spec = pl.BlockSpec((tk, tm), lambda i, j, k: (k, i)) if mode == "tn" else pl.BlockSpec((tm, tk), lambda i, j, k: (i, k))
    b_spec = pl.BlockSpec((tn, tk), lambda i, j, k: (j, k)) if mode == "nt" else pl.BlockSpec((tk, tn), lambda i, j, k: (k, j))
    o_spec = pl.BlockSpec((tm, tn), lambda i, j, k: (i, j))
    dims = _DIMS[mode]
    has_res = res is not None
    gm, gn = M // tm, N // tn
    steps = gm * gn * nk
    n_in = 2 + has_res
    n_ci = len(comm.ins) if comm is not None else 0
    n_co = len(comm.outs) if comm is not None else 0

    def body(*refs):
        a_ref, b_ref = refs[0], refs[1]
        res_ref = refs[2] if has_res else None
        comm_in = refs[n_in:n_in + n_ci]
        o_ref = refs[n_in + n_ci]
        comm_out = refs[n_in + n_ci + 1:n_in + n_ci + 1 + n_co]
        acc_ref = refs[n_in + n_ci + 1 + n_co]
        sems = refs[n_in + n_ci + 2 + n_co:]
        k = pl.program_id(2)
        step = (pl.program_id(0) * gn + pl.program_id(1)) * nk + k
        if comm is not None:
            @pl.when(step == 0)
            def _():
                comm.start(comm_in, comm_out, sems)

        part = lax.dot_general(a_ref[...], b_ref[...], dims, preferred_element_type=F32)

        @pl.when(k == 0)
        def _():
            acc_ref[...] = part

        @pl.when(k > 0)
        def _():
            acc_ref[...] += part

        @pl.when(k == nk - 1)
        def _():
            out = acc_ref[...]
            if has_res:
                out = out + res_ref[...]
            o_ref[...] = out.astype(out_dtype)

        if comm is not None:
            @pl.when(step == (3 * steps) // 4)
            def _():
                comm.mid(comm_in, comm_out, sems)

            @pl.when(step == steps - 1)
            def _():
                comm.finish(comm_in, comm_out, sems)

    in_specs = [a_spec, b_spec] + ([o_spec] if has_res else [])
    args = (a, b) + ((res,) if has_res else ())
    out_shape = jax.ShapeDtypeStruct((M, N), out_dtype)
    if comm is None:
        return pl.pallas_call(
            body, name=name, grid=(gm, gn, nk), in_specs=in_specs, out_specs=o_spec, out_shape=out_shape,
            scratch_shapes=[pltpu.VMEM((tm, tn), F32)],
            compiler_params=_params("parallel", "parallel", "arbitrary"),
        )(*args)
    outs = pl.pallas_call(
        body, name=name, grid=(gm, gn, nk),
        in_specs=in_specs + [HBM_SPEC] * n_ci, out_specs=[o_spec] + [HBM_SPEC] * n_co,
        out_shape=[out_shape] + list(comm.outs),
        scratch_shapes=[pltpu.VMEM((tm, tn), F32)] + list(comm.sems),
        compiler_params=_params("arbitrary", "arbitrary", "arbitrary"),
    )(*args, *comm.ins)
    return outs[0], outs[1:]


def _pool_mm(a, w, mode, out_dtype, name):
    T = a.shape[0]
    G = POOL_GROUPS
    cg = a.shape[1] // G
    tm = _tile(T, 1024)
    nt = T // tm
    dims = _DIMS[mode]
    if mode == "tn":
        def body(a_ref, d_ref, o_ref):
            part = lax.dot_general(a_ref[...], d_ref[...], dims, preferred_element_type=F32)

            @pl.when(pl.program_id(1) == 0)
            def _():
                o_ref[...] = part

            @pl.when(pl.program_id(1) > 0)
            def _():
                o_ref[...] += part

        return pl.pallas_call(
            body, name=name, grid=(G, nt),
            in_specs=[pl.BlockSpec((tm, cg), lambda g, i: (i, g)), pl.BlockSpec((tm, cg), lambda g, i: (i, g))],
            out_specs=pl.BlockSpec((None, cg, cg), lambda g, i: (g, 0, 0)),
            out_shape=jax.ShapeDtypeStruct((G, cg, cg), F32),
            compiler_params=_params("parallel", "arbitrary"),
        )(a, w)

    def body(a_ref, w_ref, o_ref):
        o_ref[...] = lax.dot_general(a_ref[...], w_ref[...], dims, preferred_element_type=F32).astype(out_dtype)

    return pl.pallas_call(
        body, name=name, grid=(G, nt),
        in_specs=[pl.BlockSpec((tm, cg), lambda g, i: (i, g)), pl.BlockSpec((None, cg, cg), lambda g, i: (g, 0, 0))],
        out_specs=pl.BlockSpec((tm, cg), lambda g, i: (i, g)),
        out_shape=jax.ShapeDtypeStruct((T, G * cg), out_dtype),
        compiler_params=_params("parallel", "parallel"),
    )(a, w)


ROWS = 256
HALO_BLOCK = 16


def _row_spec(d, col=0, rows=ROWS):
    return pl.BlockSpec((rows, d), lambda i, col=col: (i, col))


def _const_spec(shape):
    return pl.BlockSpec(shape, lambda *_: (0,) * len(shape))


def _rms_fwd(x, g, name):
    T, D = x.shape

    def body(x_ref, g_ref, h_ref):
        xv = x_ref[...]
        r = lax.rsqrt(jnp.mean(xv * xv, axis=-1, keepdims=True) + EPS)
        h_ref[...] = (xv * r * g_ref[...]).astype(BF16)

    return pl.pallas_call(
        body, name=name, grid=(T // ROWS,),
        in_specs=[_row_spec(D), _const_spec((1, D))], out_specs=_row_spec(D),
        out_shape=jax.ShapeDtypeStruct((T, D), BF16), compiler_params=_params("parallel"),
    )(x, g)


def _accumulate(ref, part):
    first = pl.program_id(0) == 0

    @pl.when(first)
    def _():
        ref[...] = part

    @pl.when(jnp.logical_not(first))
    def _():
        ref[...] += part


def _rms_bwd(x, g, dh, dres, name):
    T, D = x.shape

    def body(x_ref, g_ref, dh_ref, dres_ref, dx_ref, dxb_ref, dg_ref):
        xv = x_ref[...]
        r = lax.rsqrt(jnp.mean(xv * xv, axis=-1, keepdims=True) + EPS)
        xhat = xv * r
        dh_v = dh_ref[...]
        dxhat = dh_v * g_ref[...]
        dx = dres_ref[...] + r * (dxhat - xhat * jnp.mean(dxhat * xhat, axis=-1, keepdims=True))
        dx_ref[...] = dx
        dxb_ref[...] = dx.astype(BF16)
        _accumulate(dg_ref, jnp.sum(dh_v * xhat, axis=0, keepdims=True))

    return pl.pallas_call(
        body, name=name, grid=(T // ROWS,),
        in_specs=[_row_spec(D), _const_spec((1, D)), _row_spec(D), _row_spec(D)],
        out_specs=[_row_spec(D), _row_spec(D), _const_spec((1, D))],
        out_shape=[jax.ShapeDtypeStruct((T, D), F32), jax.ShapeDtypeStruct((T, D), BF16),
                   jax.ShapeDtypeStruct((1, D), F32)],
        compiler_params=_params("arbitrary"),
    )(x, g, dh, dres)


def _loss_head(x, g, target, name):
    T, D = x.shape

    def body(x_ref, g_ref, t_ref, loss_ref, dx_ref, dxb_ref, dg_ref):
        xv = x_ref[...]
        gv = g_ref[...]
        r = lax.rsqrt(jnp.mean(xv * xv, axis=-1, keepdims=True) + EPS)
        xhat = xv * r
        err = xhat * gv - t_ref[...]
        loss = 0.5 * jnp.sum(jnp.mean(err * err, axis=-1, keepdims=True), axis=0, keepdims=True)
        dy = err * (1.0 / D)
        dxhat = dy * gv
        dx = r * (dxhat - xhat * jnp.mean(dxhat * xhat, axis=-1, keepdims=True))
        dx_ref[...] = dx
        dxb_ref[...] = dx.astype(BF16)
        _accumulate(loss_ref, loss)
        _accumulate(dg_ref, jnp.sum(dy * xhat, axis=0, keepdims=True))

    return pl.pallas_call(
        body, name=name, grid=(T // ROWS,),
        in_specs=[_row_spec(D), _const_spec((1, D)), _row_spec(D)],
        out_specs=[_const_spec((1, 1)), _row_spec(D), _row_spec(D), _const_spec((1, D))],
        out_shape=[jax.ShapeDtypeStruct((1, 1), F32), jax.ShapeDtypeStruct((T, D), F32),
                   jax.ShapeDtypeStruct((T, D), BF16), jax.ShapeDtypeStruct((1, D), F32)],
        compiler_params=_params("arbitrary"),
    )(x, g, target)


def _halo_specs(d, col, n_blocks):
    per = ROWS // HALO_BLOCK
    last = n_blocks * per - 1
    prev = pl.BlockSpec((HALO_BLOCK, d), lambda i, col=col: (jnp.maximum(i * per - 1, 0), col))
    nxt = pl.BlockSpec((HALO_BLOCK, d), lambda i, col=col: (jnp.minimum((i + 1) * per, last), col))
    return prev, nxt


def _with_halo(prev, cur, nxt, n_blocks):
    i = pl.program_id(0)
    prev = jnp.where(i > 0, prev[HALO_BLOCK - HALO:], 0.0)
    nxt = jnp.where(i < n_blocks - 1, nxt[:HALO], 0.0)
    return jnp.concatenate([prev, cur, nxt], axis=0)


def _f32(ref):
    return ref[...].astype(F32)


def _shift(ext, k):
    n = ext.shape[0]
    v = ext if k == 0 else pltpu.roll(ext, (-k) % n, 0)
    return v[HALO:HALO + ROWS]


def _shift_full(ext, k):
    n = ext.shape[0]
    return pltpu.roll(ext, (-k) % n, 0)


def _pool_counts(T):
    n = ROWS + 2 * HALO
    t = pl.program_id(0) * ROWS - HALO + lax.broadcasted_iota(jnp.int32, (n, 1), 0)
    out = []
    for w in POOL_WINDOWS:
        lo = jnp.maximum(t - w // 2, 0)
        hi = jnp.minimum(t + (w - 1 - w // 2), T - 1)
        out.append(jnp.maximum(hi - lo + 1, 1).astype(F32))
    return out


def _window_sums(e, sign):
    s2 = e + _shift_full(e, -sign)
    s4 = _shift_full(s2, -1) + _shift_full(s2, 1)
    s8 = _shift_full(s4, -2) + _shift_full(s4, 2)
    s16 = _shift_full(s8, -4) + _shift_full(s8, 4)
    return s2, s4, s8, s16


def _mixer_fwd(proj, conv_w, name):
    T = proj.shape[0]
    W = conv_w.shape[1]
    nb = T // ROWS
    cg = W // POOL_GROUPS

    def body(b_ref, c_ref, x_ref, u_ref, cp_ref, cn_ref, xp_ref, xn_ref, up_ref, un_ref, w_ref, z_ref, p_ref):
        uc = _with_halo(_f32(cp_ref) * _f32(xp_ref), _f32(c_ref) * _f32(x_ref), _f32(cn_ref) * _f32(xn_ref), nb)
        w0, w1, w2 = w_ref[0:1, :], w_ref[1:2, :], w_ref[2:3, :]
        y = w0 * _shift(uc, -1) + w1 * _shift(uc, 0) + w2 * _shift(uc, 1)
        z_ref[...] = (_f32(b_ref) * y).astype(BF16)
        e = _with_halo(_f32(up_ref), _f32(u_ref), _f32(un_ref), nb)
        counts = _pool_counts(T)
        for gi in range(POOL_GROUPS):
            eg = e[:, gi * cg:(gi + 1) * cg]
            s = _window_sums(eg, 1)[gi]
            p = s[HALO:HALO + ROWS] / counts[gi][HALO:HALO + ROWS] - eg[HALO:HALO + ROWS]
            p_ref[:, gi * cg:(gi + 1) * cg] = p.astype(BF16)

    halo = [s for col in (1, 2, 3) for s in _halo_specs(W, col, nb)]
    return pl.pallas_call(
        body, name=name, grid=(nb,),
        in_specs=[_row_spec(W, 0), _row_spec(W, 1), _row_spec(W, 2), _row_spec(W, 3)] + halo + [_const_spec((8, W))],
        out_specs=[_row_spec(W), _row_spec(W)],
        out_shape=[jax.ShapeDtypeStruct((T, W), BF16), jax.ShapeDtypeStruct((T, W), BF16)],
        compiler_params=_params("parallel"),
    )(proj, proj, proj, proj, proj, proj, proj, proj, proj, proj, conv_w)


def _mixer_bwd(proj, conv_w, dz, dp, dproj, name):
    T = proj.shape[0]
    W = conv_w.shape[1]
    nb = T // ROWS
    cg = W // POOL_GROUPS

    def body(b_ref, c_ref, x_ref, dz_ref, dp_ref,
             bp_ref, bn_ref, cp_ref, cn_ref, xp_ref, xn_ref, dzp_ref, dzn_ref, dpp_ref, dpn_ref, w_ref, _,
             o_ref, dw_ref):
        cv, xv, dzv = _f32(c_ref), _f32(x_ref), _f32(dz_ref)
        uc = _with_halo(_f32(cp_ref) * _f32(xp_ref), cv * xv, _f32(cn_ref) * _f32(xn_ref), nb)
        dy = _with_halo(_f32(dzp_ref) * _f32(bp_ref), dzv * _f32(b_ref), _f32(dzn_ref) * _f32(bn_ref), nb)
        w0, w1, w2 = w_ref[0:1, :], w_ref[1:2, :], w_ref[2:3, :]
        um, u0, up = _shift(uc, -1), _shift(uc, 0), _shift(uc, 1)
        o_ref[:, 0:W] = (dzv * (w0 * um + w1 * u0 + w2 * up)).astype(BF16)
        dy0 = _shift(dy, 0)
        duc = w0 * _shift(dy, 1) + w1 * dy0 + w2 * _shift(dy, -1)
        o_ref[:, W:2 * W] = (duc * xv).astype(BF16)
        o_ref[:, 2 * W:3 * W] = (duc * cv).astype(BF16)
        row = lax.broadcasted_iota(jnp.int32, (8, W), 0)
        dw = jnp.where(row == 0, jnp.sum(dy0 * um, axis=0, keepdims=True),
                       jnp.where(row == 1, jnp.sum(dy0 * u0, axis=0, keepdims=True),
                                 jnp.where(row == 2, jnp.sum(dy0 * up, axis=0, keepdims=True), 0.0)))
        _accumulate(dw_ref, dw)
        d = _with_halo(_f32(dpp_ref), _f32(dp_ref), _f32(dpn_ref), nb)
        counts = _pool_counts(T)
        for gi in range(POOL_GROUPS):
            dg = d[:, gi * cg:(gi + 1) * cg]
            s = _window_sums(dg / counts[gi], -1)[gi]
            o_ref[:, 3 * W + gi * cg:3 * W + (gi + 1) * cg] = (s[HALO:HALO + ROWS] - dg[HALO:HALO + ROWS]).astype(BF16)

    def halo(col):
        return list(_halo_specs(W, col, nb))

    return pl.pallas_call(
        body, name=name, grid=(nb,),
        in_specs=[_row_spec(W, 0), _row_spec(W, 1), _row_spec(W, 2), _row_spec(W), _row_spec(W)]
        + halo(0) + halo(1) + halo(2) + halo(0) + halo(0) + [_const_spec((8, W)), ANY_SPEC],
        out_specs=[_row_spec(4 * W), _const_spec((8, W))],
        out_shape=[jax.ShapeDtypeStruct(dproj.shape, BF16), jax.ShapeDtypeStruct((8, W), F32)],
        input_output_aliases={16: 0}, compiler_params=_params("arbitrary"),
    )(proj, proj, proj, dz, dp, proj, proj, proj, proj, proj, proj, dz, dz, dp, dp, conv_w, dproj)


def _t5_bucket(rel):
    half = N_BUCKETS // 2
    max_exact = half // 2
    ret = jnp.where(rel > 0, half, 0)
    n = jnp.abs(rel)
    nf = jnp.maximum(n, 1).astype(jnp.float32)
    large = max_exact + (jnp.log(nf / max_exact) / math.log(MAX_DISTANCE / max_exact)
                         * (half - max_exact)).astype(jnp.int32)
    large = jnp.minimum(large, half - 1)
    return ret + jnp.where(n < max_exact, n, large)


def _bucket_table():
    qi = jnp.arange(BLOCK)[:, None]
    kj = jnp.arange(3 * BLOCK)[None, :]
    rel = kj - BLOCK - qi
    return jnp.where(jnp.abs(rel) <= WINDOW, _t5_bucket(rel), -1).astype(jnp.int32)


def _bias_table(rel_bias, bucket, name):
    def body(rb_ref, bucket_ref, o_ref):
        h = pl.program_id(0)
        bk = bucket_ref[...]
        acc = jnp.full(bk.shape, NEG_INF, F32)
        for b in range(N_BUCKETS):
            acc = jnp.where(bk == b, rb_ref[b, h], acc)
        o_ref[...] = acc

    return pl.pallas_call(
        body, name=name, grid=(N_HEADS,),
        in_specs=[pl.BlockSpec(memory_space=pltpu.SMEM), _const_spec((BLOCK, 3 * BLOCK))],
        out_specs=pl.BlockSpec((None, BLOCK, 3 * BLOCK), lambda h: (h, 0, 0)),
        out_shape=jax.ShapeDtypeStruct((N_HEADS, BLOCK, 3 * BLOCK), F32),
        compiler_params=_params("parallel"),
    )(rel_bias, bucket)


def _bias_grad(ds_sum, bucket, name):
    def body(ds_ref, bucket_ref, o_ref):
        bk = bucket_ref[...]
        ds = ds_ref[...]
        row = lax.broadcasted_iota(jnp.int32, (N_BUCKETS, 128), 0)
        acc = jnp.zeros((N_BUCKETS, 128), F32)
        for b in range(N_BUCKETS):
            s = jnp.sum(jnp.sum(jnp.where(bk == b, ds, 0.0), axis=1, keepdims=True), axis=0, keepdims=True)
            acc = jnp.where(row == b, s, acc)
        o_ref[...] = acc

    return pl.pallas_call(
        body, name=name, grid=(N_HEADS,),
        in_specs=[pl.BlockSpec((None, BLOCK, 3 * BLOCK), lambda h: (h, 0, 0)), _const_spec((BLOCK, 3 * BLOCK))],
        out_specs=pl.BlockSpec((None, N_BUCKETS, 128), lambda h: (h, 0, 0)),
        out_shape=jax.ShapeDtypeStruct((N_HEADS, N_BUCKETS, 128), F32),
        compiler_params=_params("parallel"),
    )(ds_sum, bucket)


PAIR = 2 * HEAD_DIM
Q_BLOCKS = 2


def _low_half(shape):
    return lax.broadcasted_iota(jnp.int32, shape, len(shape) - 1) % PAIR < HEAD_DIM


def _split_pair(a):
    low = _low_half(a.shape)
    zero = jnp.zeros_like(a)
    return jnp.concatenate([jnp.where(low, a, zero), jnp.where(low, zero, a)], axis=0)


def _kv_expand(proj, kv_off, name):
    T = proj.shape[0]
    kv_w = N_KV_HEADS * HEAD_DIM
    rows = _tile(T, 512)

    def body(k_ref, v_ref, ke_ref, ve_ref):
        for src, dst in ((k_ref, ke_ref), (v_ref, ve_ref)):
            for g in range(N_KV_HEADS // 2):
                x = src[:, g * PAIR:(g + 1) * PAIR].astype(F32)
                swapped = pltpu.roll(x, HEAD_DIM, 1)
                low = _low_half(x.shape)
                dst[:, 2 * g * PAIR:(2 * g + 1) * PAIR] = jnp.where(low, x, swapped).astype(BF16)
                dst[:, (2 * g + 1) * PAIR:(2 * g + 2) * PAIR] = jnp.where(low, swapped, x).astype(BF16)

    out = jax.ShapeDtypeStruct((T, N_KV_HEADS * PAIR), BF16)
    ospec = pl.BlockSpec((rows, N_KV_HEADS * PAIR), lambda i: (i, 0))
    return pl.pallas_call(
        body, name=name, grid=(T // rows,),
        in_specs=[pl.BlockSpec((rows, kv_w), lambda i: (i, kv_off // kv_w)),
                  pl.BlockSpec((rows, kv_w), lambda i: (i, kv_off // kv_w + 1))],
        out_specs=[ospec, ospec], out_shape=[out, out], compiler_params=_params("parallel"),
    )(proj, proj)


def _kv_fold(dke, dve, dproj, kv_off, name):
    T = dke.shape[0]
    kv_w = N_KV_HEADS * HEAD_DIM
    rows = _tile(T, 512)

    def body(dk_ref, dv_ref, _, o_ref):
        for n, src in enumerate((dk_ref, dv_ref)):
            for g in range(N_KV_HEADS // 2):
                a = src[:, 2 * g * PAIR:(2 * g + 1) * PAIR]
                b = src[:, (2 * g + 1) * PAIR:(2 * g + 2) * PAIR]
                a = a + pltpu.roll(a, HEAD_DIM, 1)
                b = b + pltpu.roll(b, HEAD_DIM, 1)
                o_ref[:, n * kv_w + g * PAIR:n * kv_w + (g + 1) * PAIR] = jnp.where(_low_half(a.shape), a, b).astype(BF16)

    ispec = pl.BlockSpec((rows, N_KV_HEADS * PAIR), lambda i: (i, 0))
    return pl.pallas_call(
        body, name=name, grid=(T // rows,), in_specs=[ispec, ispec, ANY_SPEC],
        out_specs=pl.BlockSpec((rows, 2 * kv_w), lambda i: (i, kv_off // (2 * kv_w))),
        out_shape=jax.ShapeDtypeStruct(dproj.shape, BF16), input_output_aliases={2: 0},
        compiler_params=_params("parallel"),
    )(dke, dve, dproj)


def _key_blocks(i, nb):
    return [pl.multiple_of(n * BLOCK, BLOCK) for n in (jnp.maximum(i - 1, 0), i, jnp.minimum(i + 1, nb - 1))]


def _three_blocks(ref, starts):
    return jnp.concatenate([ref[pl.ds(s, BLOCK), :] for s in starts], axis=0)


def _pair_scores(q2, kd, bias_ref, pr, i, nb):
    qq = _split_pair(q2)
    s = lax.dot_general(qq, kd, _DIMS["nt"], preferred_element_type=F32) * (HEAD_DIM ** -0.5)
    s = s + bias_ref[2 * pr:2 * pr + 2].reshape(2 * BLOCK, 3 * BLOCK)
    kj = lax.broadcasted_iota(jnp.int32, (1, 3 * BLOCK), 1)
    outside = jnp.logical_or(jnp.logical_and(i == 0, kj < BLOCK), jnp.logical_and(i == nb - 1, kj >= 2 * BLOCK))
    return qq, jnp.where(outside, NEG_INF, s)


def _attn_specs(T, q_off):
    gw = GROUP * HEAD_DIM
    return dict(
        sink=pl.BlockSpec(memory_space=pltpu.SMEM),
        q=pl.BlockSpec((Q_BLOCKS * BLOCK, gw), lambda j, i: (i, q_off // gw + j)),
        kv=pl.BlockSpec((T, PAIR), lambda j, i: (0, j)),
        bias=pl.BlockSpec((GROUP, BLOCK, 3 * BLOCK), lambda j, i: (j, 0, 0)),
        o=pl.BlockSpec((Q_BLOCKS * BLOCK, gw), lambda j, i: (i, j)))


def _attn_fwd(proj, q_off, kexp, vexp, bias, sink, name, comm=None):
    T = proj.shape[0]
    nb = T // BLOCK
    sp = _attn_specs(T, q_off)
    steps = N_KV_HEADS * (nb // Q_BLOCKS)
    n_ci = len(comm.ins) if comm is not None else 0
    n_co = len(comm.outs) if comm is not None else 0

    def body(*refs):
        sink_ref, q_ref, ke_ref, ve_ref, bias_ref = refs[:5]
        comm_in = refs[5:5 + n_ci]
        o_ref, lse_ref = refs[5 + n_ci:7 + n_ci]
        comm_out = refs[7 + n_ci:7 + n_ci + n_co]
        sems = refs[7 + n_ci + n_co:]
        j, i = pl.program_id(0), pl.program_id(1)
        step = j * (nb // Q_BLOCKS) + i
        if comm is not None:
            @pl.when(step == 0)
            def _():
                comm.start(comm_in, comm_out, sems)

        first_rows = lax.broadcasted_iota(jnp.int32, (2 * BLOCK, 1), 0) < BLOCK
        low = _low_half((BLOCK, PAIR))
        for b in range(Q_BLOCKS):
            blk = i * Q_BLOCKS + b
            rows = slice(b * BLOCK, (b + 1) * BLOCK)
            starts = _key_blocks(blk, nb)
            kd = _three_blocks(ke_ref, starts)
            vv = _split_pair(_three_blocks(ve_ref, starts))
            for pr in range(GROUP // 2):
                lanes = slice(pr * PAIR, (pr + 1) * PAIR)
                _, s = _pair_scores(q_ref[rows, lanes], kd, bias_ref, pr, blk, nb)
                head = GROUP * j + 2 * pr
                sk = jnp.where(first_rows, sink_ref[head], sink_ref[head + 1])
                m = jnp.maximum(jnp.max(s, axis=-1, keepdims=True), sk)
                p = jnp.exp(s - m)
                denom = jnp.sum(p, axis=-1, keepdims=True) + jnp.exp(sk - m)
                p = (p / denom).astype(BF16)
                pp = jnp.concatenate([p[:BLOCK], p[BLOCK:]], axis=1)
                o_ref[rows, lanes] = lax.dot_general(pp, vv, _DIMS["nn"], preferred_element_type=F32).astype(BF16)
                lse = m + jnp.log(denom)
                lse_ref[rows, lanes] = jnp.where(low, lse[:BLOCK], lse[BLOCK:])

        if comm is not None:
            @pl.when(step == (3 * steps) // 4)
            def _():
                comm.mid(comm_in, comm_out, sems)

            @pl.when(step == steps - 1)
            def _():
                comm.finish(comm_in, comm_out, sems)

    out_shape = [jax.ShapeDtypeStruct((T, N_HEADS * HEAD_DIM), BF16), jax.ShapeDtypeStruct((T, N_HEADS * HEAD_DIM), F32)]
    in_specs = [sp["sink"], sp["q"], sp["kv"], sp["kv"], sp["bias"]]
    if comm is None:
        att, lse = pl.pallas_call(
            body, name=name, grid=(N_KV_HEADS, nb // Q_BLOCKS), in_specs=in_specs, out_specs=[sp["o"], sp["o"]],
            out_shape=out_shape, compiler_params=_params("parallel", "parallel"),
        )(sink, proj, kexp, vexp, bias)
        return att, lse, []
    outs = pl.pallas_call(
        body, name=name, grid=(N_KV_HEADS, nb // Q_BLOCKS),
        in_specs=in_specs + [HBM_SPEC] * n_ci, out_specs=[sp["o"], sp["o"]] + [HBM_SPEC] * n_co,
        out_shape=out_shape + list(comm.outs), scratch_shapes=list(comm.sems),
        compiler_params=_params("arbitrary", "arbitrary"),
    )(sink, proj, kexp, vexp, bias, *comm.ins)
    return outs[0], outs[1], outs[2:]


def _attn_bwd(proj, q_off, kexp, vexp, bias, sink, out, lse, dout, dproj, name):
    T = proj.shape[0]
    nb = T // BLOCK
    sp = _attn_specs(T, q_off)
    scale = HEAD_DIM ** -0.5

    def body(sink_ref, q_ref, ke_ref, ve_ref, bias_ref, o_ref, lse_ref, do_ref, _,
             dq_ref, dke_ref, dve_ref, ds_ref, dsink_ref):
        j, i = pl.program_id(0), pl.program_id(1)

        @pl.when(i == 0)
        def _():
            dke_ref[...] = jnp.zeros(dke_ref.shape, F32)
            dve_ref[...] = jnp.zeros(dve_ref.shape, F32)
            ds_ref[...] = jnp.zeros(ds_ref.shape, F32)
            dsink_ref[...] = jnp.zeros(dsink_ref.shape, F32)

        low = _low_half((BLOCK, PAIR))
        for b in range(Q_BLOCKS):
            blk = i * Q_BLOCKS + b
            rows = slice(b * BLOCK, (b + 1) * BLOCK)
            starts = _key_blocks(blk, nb)
            kd = _three_blocks(ke_ref, starts)
            vd = _three_blocks(ve_ref, starts)
            kk = _split_pair(kd)
            dk_acc = jnp.zeros((3 * BLOCK, PAIR), F32)
            dv_acc = jnp.zeros((3 * BLOCK, PAIR), F32)
            for pr in range(GROUP // 2):
                lanes = slice(pr * PAIR, (pr + 1) * PAIR)
                qq, s = _pair_scores(q_ref[rows, lanes], kd, bias_ref, pr, blk, nb)
                l2 = lse_ref[rows, lanes]
                l2s = pltpu.roll(l2, HEAD_DIM, 1)
                lse_a, lse_b = jnp.where(low, l2, l2s), jnp.where(low, l2s, l2)
                p = jnp.exp(s - jnp.concatenate([jnp.concatenate([lse_a] * 3, axis=1),
                                                 jnp.concatenate([lse_b] * 3, axis=1)], axis=0))
                do2 = do_ref[rows, lanes]
                prod = do2.astype(F32) * o_ref[rows, lanes].astype(F32)
                delta_a = jnp.sum(jnp.where(low, prod, 0.0), axis=-1, keepdims=True)
                delta_b = jnp.sum(jnp.where(low, 0.0, prod), axis=-1, keepdims=True)
                dd = _split_pair(do2)
                dp = lax.dot_general(dd, vd, _DIMS["nt"], preferred_element_type=F32)
                ds = p * (dp - jnp.concatenate([delta_a, delta_b], axis=0))
                dsb = ds.astype(BF16)
                dq = lax.dot_general(jnp.concatenate([dsb[:BLOCK], dsb[BLOCK:]], axis=1), kk, _DIMS["nn"],
                                     preferred_element_type=F32) * scale
                dq_ref[rows, lanes] = dq.astype(BF16)
                dk_acc += lax.dot_general(dsb, qq, _DIMS["tn"], preferred_element_type=F32) * scale
                dv_acc += lax.dot_general(p.astype(BF16), dd, _DIMS["tn"], preferred_element_type=F32)
                ds_ref[2 * pr:2 * pr + 2] += ds.reshape(2, BLOCK, 3 * BLOCK)
                head = GROUP * j + 2 * pr
                p_sink = jnp.exp(jnp.where(low, sink_ref[head], sink_ref[head + 1]) - l2)
                dsink_ref[:, lanes] += jnp.sum(-p_sink * jnp.where(low, delta_a, delta_b), axis=0, keepdims=True)
            for t, start in enumerate(starts):
                dke_ref[pl.ds(start, BLOCK), :] += dk_acc[t * BLOCK:(t + 1) * BLOCK]
                dve_ref[pl.ds(start, BLOCK), :] += dv_acc[t * BLOCK:(t + 1) * BLOCK]

    kv_out = jax.ShapeDtypeStruct((T, N_KV_HEADS * PAIR), F32)
    return pl.pallas_call(
        body, name=name, grid=(N_KV_HEADS, nb // Q_BLOCKS),
        in_specs=[sp["sink"], sp["q"], sp["kv"], sp["kv"], sp["bias"], sp["o"], sp["o"], sp["o"], ANY_SPEC],
        out_specs=[sp["q"], sp["kv"], sp["kv"], sp["bias"],
                   pl.BlockSpec((1, GROUP * HEAD_DIM), lambda j, i: (0, j))],
        out_shape=[jax.ShapeDtypeStruct(dproj.shape, BF16), kv_out, kv_out,
                   jax.ShapeDtypeStruct((N_HEADS, BLOCK, 3 * BLOCK), F32),
                   jax.ShapeDtypeStruct((1, N_HEADS * HEAD_DIM), F32)],
        input_output_aliases={8: 0}, compiler_params=_params("parallel", "arbitrary"),
    )(sink, proj, kexp, vexp, bias, out, lse, dout, dproj)


GATE_COLS = 512


def _sigmoid(x):
    return 1.0 / (1.0 + jnp.exp(-x))


def _gate_specs(D, gate_off):
    nc = D // GATE_COLS
    base = gate_off // GATE_COLS
    return [pl.BlockSpec((ROWS, GATE_COLS), lambda i, c=base + g * nc + h: (i, c)) for g in range(3) for h in range(nc)]


def _merge_fwd(proj, gate_off, ya, yp, yt, scale, name):
    T, D = ya.shape
    nc = D // GATE_COLS

    def body(*refs):
        gates = refs[:3 * nc]
        ya_ref, yp_ref, yt_ref, s_ref, o_ref = refs[3 * nc:]
        for h in range(nc):
            cols = slice(h * GATE_COLS, (h + 1) * GATE_COLS)
            merged = (_sigmoid(_f32(gates[h])) * ya_ref[:, cols].astype(F32)
                      + _sigmoid(_f32(gates[nc + h])) * (yp_ref[:, cols].astype(F32) * s_ref[:, cols])
                      + _sigmoid(_f32(gates[2 * nc + h])) * yt_ref[:, cols].astype(F32))
            o_ref[:, cols] = merged.astype(BF16)

    yspec = _row_spec(D)
    return pl.pallas_call(
        body, name=name, grid=(T // ROWS,),
        in_specs=_gate_specs(D, gate_off) + [yspec, yspec, yspec, _const_spec((1, D))], out_specs=yspec,
        out_shape=jax.ShapeDtypeStruct((T, D), BF16), compiler_params=_params("parallel"),
    )(*([proj] * (3 * nc)), ya, yp, yt, scale)


def _merge_bwd(proj, gate_off, ya, yp, yt, scale, dm, name):
    T, D = ya.shape
    nc = D // GATE_COLS
    base = gate_off // GATE_COLS

    def body(gate_ref, ya_ref, yp_ref, yt_ref, s_ref, dm_ref, dg_ref, dya_ref, dyp_ref, dyt_ref, ds_ref):
        i, n = pl.program_id(0), pl.program_id(1)
        sg = _sigmoid(_f32(gate_ref))
        for g, (y_ref, dy_ref) in enumerate(((ya_ref, dya_ref), (yp_ref, dyp_ref), (yt_ref, dyt_ref))):
            for h in range(nc):
                @pl.when(n == g * nc + h)
                def _(g=g, h=h, y_ref=y_ref, dy_ref=dy_ref):
                    cols = slice(h * GATE_COLS, (h + 1) * GATE_COLS)
                    dy = dm_ref[:, cols].astype(F32) * sg
                    y = y_ref[:, cols].astype(F32)
                    if g == 1:
                        s_v = s_ref[:, cols]
                        part = jnp.sum(dy * y, axis=0, keepdims=True)

                        @pl.when(i == 0)
                        def _():
                            ds_ref[:, cols] = part

                        @pl.when(i > 0)
                        def _():
                            ds_ref[:, cols] += part

                        y = y * s_v
                        dy_ref[:, cols] = (dy * s_v).astype(BF16)
                    else:
                        dy_ref[:, cols] = dy.astype(BF16)
                    dg_ref[...] = (dy * y * (1.0 - sg)).astype(BF16)

    yspec = pl.BlockSpec((ROWS, D), lambda i, n: (i, 0))
    gspec = pl.BlockSpec((ROWS, GATE_COLS), lambda i, n: (i, base + n))
    sspec = pl.BlockSpec((1, D), lambda i, n: (0, 0))
    out = jax.ShapeDtypeStruct((T, D), BF16)
    return pl.pallas_call(
        body, name=name, grid=(T // ROWS, 3 * nc),
        in_specs=[gspec, yspec, yspec, yspec, sspec, yspec],
        out_specs=[gspec, yspec, yspec, yspec, sspec],
        out_shape=[jax.ShapeDtypeStruct(proj.shape, BF16), out, out, out, jax.ShapeDtypeStruct((1, D), F32)],
        compiler_params=_params("arbitrary", "arbitrary"),
    )(proj, ya, yp, yt, scale, dm)


def _swiglu_fwd(gu, name):
    T = gu.shape[0]
    F = gu.shape[1] // 2

    def body(gu_ref, o_ref):
        g = gu_ref[:, 0:F].astype(F32)
        o_ref[...] = (g * _sigmoid(g) * gu_ref[:, F:2 * F].astype(F32)).astype(BF16)

    return pl.pallas_call(
        body, name=name, grid=(T // ROWS,), in_specs=[_row_spec(2 * F)], out_specs=_row_spec(F),
        out_shape=jax.ShapeDtypeStruct((T, F), BF16), compiler_params=_params("parallel"),
    )(gu)


def _swiglu_bwd(gu, dact, name):
    T = gu.shape[0]
    F = gu.shape[1] // 2

    def body(gu_ref, d_ref, o_ref):
        g, d = gu_ref[:, 0:F].astype(F32), d_ref[...].astype(F32)
        sg = _sigmoid(g)
        o_ref[:, 0:F] = (d * gu_ref[:, F:2 * F].astype(F32) * sg * (1.0 + g * (1.0 - sg))).astype(BF16)
        o_ref[:, F:2 * F] = (d * g * sg).astype(BF16)

    return pl.pallas_call(
        body, name=name, grid=(T // ROWS,), in_specs=[_row_spec(2 * F), _row_spec(F)], out_specs=_row_spec(2 * F),
        out_shape=jax.ShapeDtypeStruct((T, 2 * F), BF16), compiler_params=_params("parallel"),
    )(gu, dact)


def _carried(plan, key, *args, **kwargs):
    job = plan.job(key) if plan is not None else None
    if job is None:
        return _matmul(*args, **kwargs)
    out, extra = _matmul(*args, comm=job, **kwargs)
    plan.done(key, extra)
    return out


def _local_step(x, target, wts, small, hooks=None):
    T, D = x.shape
    depth = small["g_mix"].shape[0]
    wts = list(wts) + [None] * (depth - len(wts))
    gate_off = wts[0]["w_inT"].shape[0] - 3 * D
    q_off = 4 * D
    bucket = _bucket_table()
    bias = _bias_table(small["rel_bias"], bucket, "bias_table")

    saved = []
    for l in range(depth):
        n = f"l{l}_"
        if hooks is not None and l > 0:
            wts[l] = hooks.weights(l)
        w = wts[l]
        plan = hooks.plan_fwd(l) if hooks is not None else None
        h = _rms_fwd(x, small["g_mix"][l], n + "rms_mix")
        proj = _carried(plan, "proj", h, w["w_inT"], "nt", BF16, n + "proj")
        z, p = _mixer_fwd(proj, small["conv_w"][l], n + "mixer")
        kexp, vexp = _kv_expand(proj, q_off + D, n + "kv_expand")
        sink = small["attn_sink"][l]
        job = plan.job("attn") if plan is not None else None
        att, lse, extra = _attn_fwd(proj, q_off, kexp, vexp, bias, sink, n + "attn", comm=job)
        if job is not None:
            plan.done("attn", extra)
        ya =_matmul(z, w["w_a_out"], "nn", BF16, n + "ya")
        yp = _pool_mm(p, w["w_pool"], "nn", BF16, n + "yp")
        yt = _matmul(att, w["w_attn_out"], "nn", BF16, n + "yt")
        merged = _merge_fwd(proj, gate_off, ya, yp, yt, small["pool_scale"][l], n + "merge")
        x1 = _matmul(merged, w["w_o"], "nn", F32, n + "x1", res=x)
        h2 = _rms_fwd(x1, small["g_ffn"][l], n + "rms_ffn")
        gu = _carried(plan, "gu", h2, w["w_guT"], "nt", BF16, n + "gu")
        act = _swiglu_fwd(gu, n + "swiglu")
        ff = w["w_down"].shape[0]
        x2 = _carried(plan, "x2", act, w["w_down"], "nn", F32, n + "x2", res=x1, tn_cap=512, tk_cap=ff)
        saved.append(dict(x=x, h=h, proj=proj, z=z, p=p, kexp=kexp, vexp=vexp, sink=sink, lse=lse, att=att,
                          ya=ya, yp=yp, yt=yt, merged=merged, x1=x1, h2=h2, gu=gu, act=act))
        x = x2

    loss, dx, dxb, dg_final = _loss_head(x, small["g_final"], target, "loss_head")

    gw = [None] * depth
    gs = {k_: [None] * depth for k_ in ("conv_w", "pool_scale", "g_mix", "g_ffn", "attn_sink")}
    ds_total = None
    for l in reversed(range(depth)):
        n = f"l{l}_b_"
        s, w, g = saved[l], wts[l], {}
        plan = hooks.plan_bwd(l) if hooks is not None else None
        ff = w["w_down"].shape[0]
        g["w_down"] = _carried(plan, "dw_down", s["act"], dxb, "tn", BF16, n + "dw_down", tm_cap=ff, tk_cap=512)
        dact = _matmul(dxb, w["w_down"], "nt", BF16, n + "dact", tm_cap=512, tn_cap=ff)
        dgu = _swiglu_bwd(s["gu"], dact, n + "swiglu")
        g["w_guT"] = _carried(plan, "dw_gu", dgu, s["h2"], "tn", BF16, n + "dw_gu", tm_cap=512)
        dh2 = _carried(plan, "dh2", dgu, w["w_guT"], "nn", F32, n + "dh2", tn_cap=512, tk_cap=ff)
        dx1, dx1b, gs["g_ffn"][l] = _rms_bwd(s["x1"], small["g_ffn"][l], dh2, dx, n + "rms_ffn")
        g["w_o"] = _matmul(s["merged"], dx1b, "tn", BF16, n + "dw_o")
        dm = _matmul(dx1b, w["w_o"], "nt", BF16, n + "dmerged")
        dproj, dya, dyp, dyt, gs["pool_scale"][l] = _merge_bwd(
            s["proj"], gate_off, s["ya"], s["yp"], s["yt"], small["pool_scale"][l], dm, n + "merge")
        g["w_a_out"] = _matmul(s["z"], dya, "tn", BF16, n + "dw_a_out")
        dz = _matmul(dya, w["w_a_out"], "nt", BF16, n + "dz")
        g["w_pool"] = _pool_mm(s["p"], dyp, "tn", F32, n + "dw_pool")
        dp = _pool_mm(dyp, w["w_pool"], "nt", BF16, n + "dp")
        g["w_attn_out"] = _matmul(s["att"], dyt, "tn", BF16, n + "dw_attn_out")
        datt = _matmul(dyt, w["w_attn_out"], "nt", BF16, n + "datt")
        dproj, gs["conv_w"][l] = _mixer_bwd(s["proj"], small["conv_w"][l], dz, dp, dproj, n + "mixer")
        dproj, dke, dve, ds_sum, dsink = _attn_bwd(s["proj"], q_off, s["kexp"], s["vexp"], bias, s["sink"], s["att"],
                                                   s["lse"], datt, dproj, n + "attn")
        gs["attn_sink"][l] = dsink.reshape(N_HEADS, HEAD_DIM)[:, 0]
        ds_total = ds_sum if ds_total is None else ds_total + ds_sum
        dproj = _kv_fold(dke, dve, dproj, q_off + D, n + "kv_fold")
        g["w_inT"] = _carried(plan, "dw_in", dproj, s["h"], "tn", BF16, n + "dw_in", tm_cap=512)
        dh = _matmul(dproj, w["w_inT"], "nn", F32, n + "dh", tk_cap=2816)
        dx, dxb, gs["g_mix"][l] = _rms_bwd(s["x"], small["g_mix"][l], dh, dx1, n + "rms_mix")
        gw[l] = g
        if hooks is not None:
            hooks.grads(l, g)

    d_rel =_bias_grad(ds_total, bucket, "bias_grad")[:, :, 0].T
    gs = {k_: jnp.stack(v_) for k_, v_ in gs.items()}
    gs["rel_bias"] = d_rel
    gs["g_final"] = dg_final
    return loss, dx, gw, gs


def _place():
    return lax.axis_index("x"), lax.axis_index("y"), lax.axis_index("c")


class _GatherJob:
    def __init__(self, parts):
        n = len(parts)
        self.n = n
        self.ins = list(parts)
        self.outs = [jax.ShapeDtypeStruct((N_DEV,) + p.shape, p.dtype) for p in parts]
        self.sems = [pltpu.SemaphoreType.DMA((7 * n,)), pltpu.SemaphoreType.DMA((7 * n,)), pltpu.SemaphoreType.DMA((n,))]

    def _copies(self, ins, outs, sems):
        send_sems, recv_sems, local_sems = sems
        x, y, c = _place()
        me, sibling = (x, y, c), (x, y, 1 - c)
        chips = [(1 - x, y), (x, 1 - y), (1 - x, 1 - y)]

        def rows(t, px, py, pc):
            return outs[t].at[4 * px + 2 * py + pc]

        def copy(t, k, block, to, src=None):
            return pltpu.make_async_remote_copy(
                src_ref=rows(t, *block) if src is None else src, dst_ref=rows(t, *block),
                send_sem=send_sems.at[7 * t + k], recv_sem=recv_sems.at[7 * t + k], device_id=to, device_id_type=MESH)

        ts = range(self.n)
        own = [pltpu.make_async_copy(ins[t], rows(t, *me), local_sems.at[t]) for t in ts]
        first = [copy(t, 0, me, sibling, src=ins[t]) for t in ts]
        first += [copy(t, 1 + j, me, (*chip, c), src=ins[t]) for t in ts for j, chip in enumerate(chips)]
        landed = [copy(t, 1 + j, (*chip, c), me) for j, chip in enumerate(chips) for t in ts]
        passed = [copy(t, 4 + j, (*chip, c), sibling) for j, chip in enumerate(chips) for t in ts]
        last = [copy(t, 0, sibling, me) for t in ts]
        last += [copy(t, 4 + j, (*chip, 1 - c), me) for t in ts for j, chip in enumerate(chips)]
        return own, first, landed, passed, last

    def start(self, ins, outs, sems):
        own, first, _, _, _ = self._copies(ins, outs, sems)
        for cp in own + first:
            cp.start()

    def mid(self, ins, outs, sems):
        _, _, landed, passed, _ = self._copies(ins, outs, sems)
        for arrived, onward in zip(landed, passed):
            arrived.wait_recv()
            onward.start()

    def finish(self, ins, outs, sems):
        own, first, _, passed, last = self._copies(ins, outs, sems)
        for cp in last:
            cp.wait_recv()
        for cp in first + passed:
            cp.wait_send()
        for cp in own:
            cp.wait()


class _SwapJob:
    def __init__(self, g):
        self.ins = [g]
        self.outs = [jax.ShapeDtypeStruct(g.shape[:1] + g.shape[2:], g.dtype)]
        self.sems = [pltpu.SemaphoreType.DMA, pltpu.SemaphoreType.DMA]

    def _copy(self, ins, outs, sems):
        x, y, c = _place()
        return pltpu.make_async_remote_copy(src_ref=ins[0].at[pl.ds(0, ins[0].shape[0]), 1 - c], dst_ref=outs[0],
                                            send_sem=sems[0], recv_sem=sems[1], device_id=(x, y, 1 - c),
                                            device_id_type=MESH)

    def start(self, ins, outs, sems):
        self._copy(ins, outs, sems).start()

    def mid(self, ins, outs, sems):
        pass

    def finish(self, ins, outs, sems):
        self._copy(ins, outs, sems).wait()


class _ExchangeJob:
    def __init__(self, p, row0, rows):
        self.row0, self.rows = row0, rows
        self.ins = [p]
        self.outs = [jax.ShapeDtypeStruct((3, rows) + p.shape[2:], p.dtype)]
        self.sems = [pltpu.SemaphoreType.DMA((3,)), pltpu.SemaphoreType.DMA((3,))]

    def _copies(self, ins, outs, sems):
        x, y, c = _place()
        chips = [(1 - x, y), (x, 1 - y), (1 - x, 1 - y)]
        return [pltpu.make_async_remote_copy(
            src_ref=ins[0].at[2 * px + py, pl.ds(self.row0, self.rows)], dst_ref=outs[0].at[k],
            send_sem=sems[0].at[k], recv_sem=sems[1].at[k], device_id=(px, py, c), device_id_type=MESH)
            for k, (px, py) in enumerate(chips)]

    def start(self, ins, outs, sems):
        for cp in self._copies(ins, outs, sems):
            cp.start()

    def mid(self, ins, outs, sems):
        pass

    def finish(self, ins, outs, sems):
        for cp in self._copies(ins, outs, sems):
            cp.wait()


def _all_gather(v, name):
    def body(x_ref, out_ref, send_sems, recv_sems, local_sem):
        x, y, c = _place()
        me, sibling = (x, y, c), (x, y, 1 - c)
        chips = [(1 - x, y), (x, 1 - y), (1 - x, 1 - y)]

        def rows(px, py, pc):
            return out_ref.at[4 * px + 2 * py + pc]

        def copy(k, block, to, src=None):
            return pltpu.make_async_remote_copy(
                src_ref=rows(*block) if src is None else src, dst_ref=rows(*block),
                send_sem=send_sems.at[k], recv_sem=recv_sems.at[k], device_id=to, device_id_type=MESH)

        mine = pltpu.make_async_copy(x_ref, rows(*me), local_sem)
        mine.start()
        first = [copy(0, me, sibling, src=x_ref)]
        first += [copy(1 + j, me, (*chip, c), src=x_ref) for j, chip in enumerate(chips)]
        for cp in first:
            cp.start()
        passed = [copy(4 + j, (*chip, c), sibling) for j, chip in enumerate(chips)]
        for j, chip in enumerate(chips):
            copy(1 + j, (*chip, c), me).wait_recv()
            passed[j].start()
        copy(0, sibling, me).wait_recv()
        for j, chip in enumerate(chips):
            copy(4 + j, (*chip, 1 - c), me).wait_recv()
        for cp in first + passed:
            cp.wait_send()
        mine.wait()

    return pl.pallas_call(
        body, name=name, in_specs=[HBM_SPEC], out_specs=HBM_SPEC,
        out_shape=jax.ShapeDtypeStruct((N_DEV,) + v.shape, v.dtype),
        scratch_shapes=[pltpu.SemaphoreType.DMA((7,)), pltpu.SemaphoreType.DMA((7,)), pltpu.SemaphoreType.DMA],
    )(v)


def _all_gather_many(parts, name):
    n = len(parts)

    def body(*refs):
        ins, outs = refs[:n], refs[n:2 * n]
        send_sems, recv_sems, local_sems = refs[2 * n:]
        x, y, c = _place()
        me, sibling = (x, y, c), (x, y, 1 - c)
        chips = [(1 - x, y), (x, 1 - y), (1 - x, 1 - y)]

        def rows(t, px, py, pc):
            return outs[t].at[4 * px + 2 * py + pc]

        def copy(t, k, block, to, src=None):
            return pltpu.make_async_remote_copy(
                src_ref=rows(t, *block) if src is None else src, dst_ref=rows(t, *block),
                send_sem=send_sems.at[7 * t + k], recv_sem=recv_sems.at[7 * t + k], device_id=to, device_id_type=MESH)

        mine = [pltpu.make_async_copy(ins[t], rows(t, *me), local_sems.at[t]) for t in range(n)]
        sends = []
        for t in range(n):
            mine[t].start()
            sends.append(copy(t, 0, me, sibling, src=ins[t]))
            sends += [copy(t, 1 + j, me, (*chip, c), src=ins[t]) for j, chip in enumerate(chips)]
        for cp in sends:
            cp.start()
        for j, chip in enumerate(chips):
            for t in range(n):
                copy(t, 1 + j, (*chip, c), me).wait_recv()
                passed = copy(t, 4 + j, (*chip, c), sibling)
                passed.start()
                sends.append(passed)
        for t in range(n):
            copy(t, 0, sibling, me).wait_recv()
            for j, chip in enumerate(chips):
                copy(t, 4 + j, (*chip, 1 - c), me).wait_recv()
        for cp in sends:
            cp.wait_send()
        for cp in mine:
            cp.wait()

    return pl.pallas_call(
        body, name=name, in_specs=[HBM_SPEC] * n, out_specs=[HBM_SPEC] * n,
        out_shape=[jax.ShapeDtypeStruct((N_DEV,) + p.shape, p.dtype) for p in parts],
        scratch_shapes=[pltpu.SemaphoreType.DMA((7 * n,)), pltpu.SemaphoreType.DMA((7 * n,)),
                        pltpu.SemaphoreType.DMA((n,))],
    )(*parts)


def _run_job(job, name):
    n_in, n_out = len(job.ins), len(job.outs)

    def body(*refs):
        ins, outs, sems = refs[:n_in], refs[n_in:n_in + n_out], refs[n_in + n_out:]
        job.start(ins, outs, sems)
        job.mid(ins, outs, sems)
        job.finish(ins, outs, sems)

    return pl.pallas_call(
        body, name=name, in_specs=[HBM_SPEC] * n_in, out_specs=[HBM_SPEC] * n_out, out_shape=list(job.outs),
        scratch_shapes=list(job.sems),
    )(*job.ins)


def _chip_exchange(p, name):
    def body(p_ref, out_ref, send_sems, recv_sems):
        x, y, c = _place()
        chips = [(1 - x, y), (x, 1 - y), (1 - x, 1 - y)]
        copies = [pltpu.make_async_remote_copy(
            src_ref=p_ref.at[2 * px + py], dst_ref=out_ref.at[k], send_sem=send_sems.at[k], recv_sem=recv_sems.at[k],
            device_id=(px, py, c), device_id_type=MESH) for k, (px, py) in enumerate(chips)]
        for cp in copies:
            cp.start()
        for cp in copies:
            cp.wait()

    return pl.pallas_call(
        body, name=name, in_specs=[HBM_SPEC], out_specs=HBM_SPEC,
        out_shape=jax.ShapeDtypeStruct((3,) + p.shape[1:], p.dtype),
        scratch_shapes=[pltpu.SemaphoreType.DMA((3,)), pltpu.SemaphoreType.DMA((3,))],
    )(p)


def _sum_parts(own, index, others, out_dtype, name, own_row0=0, own_step=0):
    R = others.shape[1]
    rows = _tile(R, 512)
    k = others.shape[0]
    assert own_row0 % rows == 0 and own.shape[1] % rows == 0
    blk0 = own_row0 // rows
    per_own = own.shape[1] // rows

    def own_block(i, idx):
        if own_step:
            return (idx[0] + own_step * (i // per_own), i % per_own, 0)
        return (idx[0], blk0 + i, 0)

    def body(idx_ref, own_ref, *refs):
        del idx_ref
        acc = own_ref[...].astype(F32)
        for r in refs[:k]:
            acc = acc + r[...].astype(F32)
        refs[k][...] = acc.astype(out_dtype)

    grid_spec = pltpu.PrefetchScalarGridSpec(
        num_scalar_prefetch=1, grid=(R // rows,),
        in_specs=[pl.BlockSpec((None, rows, LANES), own_block)]
        + [pl.BlockSpec((None, rows, LANES), lambda i, idx, j=j: (j, i, 0)) for j in range(k)],
        out_specs=pl.BlockSpec((rows, LANES), lambda i, idx: (i, 0)))
    return pl.pallas_call(
        body, name=name, grid_spec=grid_spec,
        out_shape=jax.ShapeDtypeStruct((R, LANES), out_dtype), compiler_params=_params("parallel"),
    )(jnp.reshape(index, (1,)).astype(jnp.int32), own, *([others] * k))


def _adamw(w, g, m, v, name):
    shape = w.shape
    cols = shape[-1]
    rows_total = w.size // cols
    w2, g2, m2, v2 = (a.reshape(rows_total, cols) for a in (w, g, m, v))
    rows = rows_total
    if rows_total > ROWS:
        rows = next(r for r in range(ROWS, 0, -8) if rows_total % r == 0)

    def body(w_ref, g_ref, m_ref, v_ref, d_ref, nm_ref, nv_ref):
        gv = g_ref[...]
        nm = ADAM_B1 * m_ref[...] + (1.0 - ADAM_B1) * gv
        nv = ADAM_B2 * v_ref[...] + (1.0 - ADAM_B2) * (gv * gv)
        m_hat = nm / (1.0 - ADAM_B1 ** ADAM_STEP)
        v_hat = nv / (1.0 - ADAM_B2 ** ADAM_STEP)
        d_ref[...] = -ADAM_LR * (m_hat / (jnp.sqrt(v_hat) + ADAM_EPS) + ADAM_WD * w_ref[...])
        nm_ref[...] = nm
        nv_ref[...] = nv

    spec = pl.BlockSpec((rows, cols), lambda i: (i, 0))
    out = jax.ShapeDtypeStruct((rows_total, cols), F32)
    d, nm, nv = pl.pallas_call(
        body, name=name, grid=(rows_total // rows,), in_specs=[spec] * 4, out_specs=[spec] * 3,
        out_shape=[out, out, out], compiler_params=_params("parallel"),
    )(w2, g2, m2, v2)
    return d.reshape(shape), nm.reshape(shape), nv.reshape(shape)


BIG = ("w_in", "w_a_out", "w_pool", "w_attn_out", "w_o", "w_gu", "w_down")


LOCAL = dict(w_in="w_inT", w_a_out="w_a_out", w_pool="w_pool", w_attn_out="w_attn_out", w_o="w_o", w_gu="w_guT",
             w_down="w_down")


def _shard_rows(w, l):
    out = []
    for name in BIG:
        a = w[name][l]
        if name in ("w_in", "w_gu"):
            a = a.T
        elif name == "w_pool":
            a = a.reshape(-1, a.shape[-1])
        out.append(a.astype(BF16))
    return out


def _full_weights(names, gathered, w):
    out = {}
    for name, g in zip(names, gathered):
        if name == "w_pool":
            G, rg, cg = w[name].shape[1:]
            out[name] = jnp.transpose(g.reshape(N_DEV, G, rg, cg), (1, 0, 2, 3)).reshape(G, N_DEV * rg, cg)
        else:
            out[LOCAL[name]] = g.reshape(N_DEV * g.shape[1], g.shape[2])
    return out


def _split_grads(g, w):
    parts, spans, at = [], {}, 0
    for name in BIG:
        a = g[LOCAL[name]].astype(BF16)
        if name == "w_pool":
            G, rg, cg = w[name].shape[1:]
            a = jnp.transpose(a.reshape(G, N_DEV, rg, cg), (1, 0, 2, 3))
        a = a.reshape(N_DEV, -1, LANES)
        spans[name] = (at, at + a.shape[1])
        at += a.shape[1]
        parts.append(a)
    return jnp.concatenate(parts, axis=1), spans


def _own_grads(g_layers, spans, w):
    L = len(g_layers)
    g3 = jnp.stack(g_layers)
    out = {}
    for name in BIG:
        a = g3[:, spans[name][0]:spans[name][1]]
        if name in ("w_in", "w_gu"):
            sh = w[name].shape
            a = jnp.swapaxes(a.reshape(L, sh[2], sh[1]), 1, 2)
        out[name] = a.reshape(w[name].shape)
    return out


SQUARE = ("w_a_out", "w_pool", "w_attn_out", "w_o")


class _Prefetch:
    def __init__(self, schedule, assign):
        self.schedule, self.assign = schedule, assign

    def job(self, key):
        if key not in self.assign:
            return None
        layer, names = self.assign[key]
        shards = dict(zip(BIG, _shard_rows(self.schedule.w, layer)))
        return _GatherJob([shards[name] for name in names])

    def done(self, key, outs):
        layer, names = self.assign[key]
        self.schedule.arrived(layer, names, outs)


class _Reduce:
    CHUNK_ROWS = 1024
    CARRIERS = ("dw_gu", "dw_in", "dh2")

    def __init__(self, split, place, tag):
        self.split, self.tag = split, tag
        self.core, self.chip = place[2], 2 * place[0] + place[1]
        self.rows = split.shape[1]
        self.chunks = [(r, min(self.CHUNK_ROWS, self.rows - r)) for r in range(0, self.rows, self.CHUNK_ROWS)]
        assert len(self.chunks) <= len(self.CARRIERS)
        self.sums = [None] * len(self.chunks)

    def _swap_job(self):
        return _SwapJob(self.split.reshape(4, 2, self.rows, LANES))

    def _pair_sum(self, from_sibling):
        pair = _sum_parts(self.split, self.core, from_sibling.reshape(1, 4 * self.rows, LANES), BF16,
                          self.tag + "pair_sum", own_step=2)
        self.pair = pair.reshape(4, self.rows, LANES)

    def _chip_sum(self, n, from_chips):
        self.sums[n] = _sum_parts(self.pair, self.chip, from_chips, F32, f"{self.tag}chip_sum{n}",
                                  own_row0=self.chunks[n][0])

    def job(self, key):
        if key == "dw_down":
            return self._swap_job()
        if key in self.CARRIERS[:len(self.chunks)]:
            return _ExchangeJob(self.pair, *self.chunks[self.CARRIERS.index(key)])
        return None

    def done(self, key, outs):
        if key == "dw_down":
            self._pair_sum(outs[0])
        else:
            self._chip_sum(self.CARRIERS.index(key), outs[0])

    def run(self):
        self._pair_sum(_run_job(self._swap_job(), self.tag + "reduce_pair")[0])
        self.chunks, self.sums = [(0, self.rows)], [None]
        self._chip_sum(0, _run_job(_ExchangeJob(self.pair, 0, self.rows), self.tag + "reduce_chips")[0])
        return self.result()

    def result(self):
        return self.sums[0] if len(self.sums) == 1 else jnp.concatenate(self.sums, axis=0)


class _Schedule:
    def __init__(self, w, place):
        self.w, self.place = w, place
        self.depth = w["w_in"].shape[0]
        self.full = [{} for _ in range(self.depth)]
        self.reduce = {}
        self.g_layers = [None] * self.depth
        self.spans = None

    def arrived(self, layer, names, gathered):
        self.full[layer].update(_full_weights(names, gathered, self.w))

    def plan_fwd(self, l):
        nxt = l + 1
        if l == 0:
            assign = dict(proj=(0, tuple(n for n in BIG if n != "w_in")))
            if nxt < self.depth:
                assign.update(attn=(nxt, ("w_in",) + SQUARE), gu=(nxt, ("w_gu",)), x2=(nxt, ("w_down",)))
        elif nxt < self.depth:
            assign = dict(proj=(nxt, ("w_in",)), attn=(nxt, ("w_down",) + SQUARE), gu=(nxt, ("w_gu",)))
        else:
            return None
        return _Prefetch(self, assign)

    def weights(self, l):
        return self.full[l]

    def plan_bwd(self, l):
        return self.reduce.get(l + 1)

    def grads(self, l, g):
        if l + 1 in self.reduce:
            self.g_layers[l + 1] = self.reduce[l + 1].result()
        split, self.spans = _split_grads(g, self.w)
        self.reduce[l] = _Reduce(split, self.place, f"l{l}_")
        if l == 0:
            self.g_layers[0] = self.reduce[0].run()


SMALL_ROWS = 32


def _pack_small(gs, L, D):
    rows = [gs["pool_scale"].reshape(L, D), gs["g_mix"].reshape(L, D), gs["g_ffn"].reshape(L, D),
            gs["g_final"].reshape(1, D), gs["conv_w"][:, :3].reshape(3 * L, D),
            jnp.pad(gs["attn_sink"].reshape(1, -1), ((0, 0), (0, D - L * N_HEADS))),
            jnp.pad(gs["rel_bias"].reshape(1, -1), ((0, 0), (0, D - N_BUCKETS * N_HEADS)))]
    a = jnp.concatenate(rows, axis=0)
    return jnp.pad(a, ((0, SMALL_ROWS - a.shape[0]), (0, 0)))


def _unpack_small(a, L, D):
    g = {}
    g["pool_scale"] = a[0:L]
    g["g_mix"] = a[L:2 * L]
    g["g_ffn"] = a[2 * L:3 * L]
    g["g_final"] = a[3 * L]
    g["conv_w"] = a[3 * L + 1:6 * L + 1].reshape(L, 3, 1, D)
    g["attn_sink"] = a[6 * L + 1, :L * N_HEADS].reshape(L, N_HEADS)
    g["rel_bias"] = a[6 * L + 2, :N_BUCKETS * N_HEADS].reshape(N_BUCKETS, N_HEADS)
    return g


WEIGHTS = ("w_in", "conv_w", "w_a_out", "w_pool", "pool_scale", "w_attn_out", "attn_sink", "w_o", "g_mix", "g_ffn",
           "w_gu", "w_down", "rel_bias", "g_final")


def kernel(x, w_in, conv_w, w_a_out, w_pool, pool_scale, w_attn_out, attn_sink, w_o, g_mix, g_ffn, w_gu, w_down, rel_bias, g_final, loss_target, m_w_in, m_conv_w, m_w_a_out, m_w_pool, m_pool_scale, m_w_attn_out, m_attn_sink, m_w_o, m_g_mix, m_g_ffn, m_w_gu, m_w_down, m_rel_bias, m_g_final, v_w_in, v_conv_w, v_w_a_out, v_w_pool, v_pool_scale, v_w_attn_out, v_attn_sink, v_w_o, v_g_mix, v_g_ffn, v_w_gu, v_w_down, v_rel_bias, v_g_final):
    w = dict(w_in=w_in, conv_w=conv_w, w_a_out=w_a_out, w_pool=w_pool, pool_scale=pool_scale, w_attn_out=w_attn_out,
             attn_sink=attn_sink, w_o=w_o, g_mix=g_mix, g_ffn=g_ffn, w_gu=w_gu, w_down=w_down, rel_bias=rel_bias,
             g_final=g_final)
    m = dict(w_in=m_w_in, conv_w=m_conv_w, w_a_out=m_w_a_out, w_pool=m_w_pool, pool_scale=m_pool_scale,
             w_attn_out=m_w_attn_out, attn_sink=m_attn_sink, w_o=m_w_o, g_mix=m_g_mix, g_ffn=m_g_ffn, w_gu=m_w_gu,
             w_down=m_w_down, rel_bias=m_rel_bias, g_final=m_g_final)
    v = dict(w_in=v_w_in, conv_w=v_conv_w, w_a_out=v_w_a_out, w_pool=v_w_pool, pool_scale=v_pool_scale,
             w_attn_out=v_w_attn_out, attn_sink=v_attn_sink, w_o=v_w_o, g_mix=v_g_mix, g_ffn=v_g_ffn, w_gu=v_w_gu,
             w_down=v_w_down, rel_bias=v_rel_bias, g_final=v_g_final)
    T, D = x.shape[1], x.shape[2]
    L = w_in.shape[0]
    cx, cy, cc = _place()

    schedule = _Schedule(w, (cx, cy, cc))
    schedule.arrived(0, ("w_in",), _run_job(_GatherJob(_shard_rows(w, 0)[:1]), "gather_w_in_l0"))
    cw = jnp.pad(conv_w.reshape(L * 3, -1), ((0, 16 - L * 3), (0, 0)))
    cw = _all_gather(cw, "gather_conv_w")
    cw = jnp.transpose(cw, (1, 0, 2)).reshape(16, -1)[:L * 3].reshape(L, 3, -1)
    small = dict(conv_w=jnp.pad(cw, ((0, 0), (0, 5), (0, 0))), pool_scale=pool_scale.reshape(L, 1, D),
                 g_mix=g_mix.reshape(L, 1, D), g_ffn=g_ffn.reshape(L, 1, D), attn_sink=attn_sink,
                 rel_bias=rel_bias, g_final=g_final.reshape(1, D))

    loss, dx, _, gs = _local_step(x[0], loss_target[0], [schedule.full[0]], small, schedule)
    loss = lax.psum(loss[0, 0], ("x", "y", "c"))
    grads = _own_grads(schedule.g_layers, schedule.spans, w)

    small_all = _all_gather(_pack_small(gs, L, D), "gather_small")
    small_sum = _sum_parts(small_all, jnp.int32(0), small_all[1:], F32, "small_sum")
    gsm = _unpack_small(small_sum, L, D)
    W8 = D // N_DEV
    dev = 4 * cx + 2 * cy + cc
    gsm["conv_w"] = lax.dynamic_slice_in_dim(gsm["conv_w"], dev * W8, W8, axis=3)
    grads.update(gsm)

    deltas, new_m, new_v = {}, {}, {}
    for name in WEIGHTS:
        deltas[name], new_m[name], new_v[name] = _adamw(w[name], grads[name], m[name], v[name], "adamw_" + name)

    return (loss, dx[None], *[grads[n] for n in WEIGHTS], *[deltas[n] for n in WEIGHTS],
            *[new_m[n] for n in WEIGHTS], *[new_v[n] for n in WEIGHTS])
```

```python
import functools
import math

import jax
import jax.numpy as jnp
from jax import lax
from jax.experimental import pallas as pl
from jax.experimental.pallas import tpu as pltpu

F32 = jnp.float32
BF16 = jnp.bfloat16
MESH = pl.DeviceIdType.MESH

N_DEV = 8
N_HEADS = 16
N_KV_HEADS = 4
HEAD_DIM = 64
GROUP = N_HEADS // N_KV_HEADS
BLOCK = 128
WINDOW = 128
N_BUCKETS = 32
MAX_DISTANCE = 128
POOL_WINDOWS = (2, 4, 8, 16)
POOL_GROUPS = 4
HALO = 8
EPS = 1e-6
NEG_INF = -1e30

ADAM_LR = 0.001
ADAM_B1 = 0.9
ADAM_B2 = 0.999
ADAM_EPS = 1e-08
ADAM_WD = 0.01
ADAM_STEP = 10

LANES = 1024
VMEM_LIMIT_BYTES = 48 * 1024 * 1024


def _params(*sem):
    return pltpu.CompilerParams(dimension_semantics=sem, vmem_limit_bytes=VMEM_LIMIT_BYTES)


def _tile(n, cap):
    if n <= cap:
        return n
    for t in range(cap - cap % 128, 0, -128):
        if n % t == 0:
            return t
    raise ValueError(f"no tile for {n}")


_DIMS = {"nn": (((1,), (0,)), ((), ())), "nt": (((1,), (1,)), ((), ())), "tn": (((0,), (0,)), ((), ()))}


HBM_SPEC = pl.BlockSpec(memory_space=pltpu.HBM)
ANY_SPEC = pl.BlockSpec(memory_space=pl.ANY)


def _matmul(a, b, mode, out_dtype, name, res=None, tm_cap=1024, tn_cap=1024, tk_cap=1024, comm=None):
    if mode == "tn":
        K, M = a.shape
    else:
        M, K = a.shape
    N = b.shape[0] if mode == "nt" else b.shape[1]
    tm, tn, tk = _tile(M, tm_cap), _tile(N, tn_cap), _tile(K, tk_cap)
    nk = K // tk
    a_spec = pl.BlockSpec((tk, tm), lambda i, j, k: (k, i)) if mode == "tn" else pl.BlockSpec((tm, tk), lambda i, j, k: (i, k))
    b_spec = pl.BlockSpec((tn, tk), lambda i, j, k: (j, k)) if mode == "nt" else pl.BlockSpec((tk, tn), lambda i, j, k: (k, j))
    o_spec = pl.BlockSpec((tm, tn), lambda i, j, k: (i, j))
    dims = _DIMS[mode]
    has_res = res is not None
    gm, gn = M // tm, N // tn
    steps = gm * gn * nk
    n_in = 2 + has_res
    n_ci = len(comm.ins) if comm is not None else 0
    n_co = len(comm.outs) if comm is not None else 0

    def body(*refs):
        a_ref, b_ref = refs[0], refs[1]
        res_ref = refs[2] if has_res else None
        comm_in = refs[n_in:n_in + n_ci]
        o_ref = refs[n_in + n_ci]
        comm_out = refs[n_in + n_ci + 1:n_in + n_ci + 1 + n_co]
        acc_ref = refs[n_in + n_ci + 1 + n_co]
        sems = refs[n_in + n_ci + 2 + n_co:]
        k = pl.program_id(2)
        step = (pl.program_id(0) * gn + pl.program_id(1)) * nk + k
        if comm is not None:
            @pl.when(step == 0)
            def _():
                comm.start(comm_in, comm_out, sems)

        part = lax.dot_general(a_ref[...], b_ref[...], dims, preferred_element_type=F32)

        @pl.when(k == 0)
        def _():
            acc_ref[...] = part

        @pl.when(k > 0)
        def _():
            acc_ref[...] += part

        @pl.when(k == nk - 1)
        def _():
            out = acc_ref[...]
            if has_res:
                out = out + res_ref[...]
            o_ref[...] = out.astype(out_dtype)

        if comm is not None:
            @pl.when(step == (3 * steps) // 4)
            def _():
                comm.mid(comm_in, comm_out, sems)

            @pl.when(step == steps - 1)
            def _():
                comm.finish(comm_in, comm_out, sems)

    in_specs = [a_spec, b_spec] + ([o_spec] if has_res else [])
    args = (a, b) + ((res,) if has_res else ())
    out_shape = jax.ShapeDtypeStruct((M, N), out_dtype)
    if comm is None:
        return pl.pallas_call(
            body, name=name, grid=(gm, gn, nk), in_specs=in_specs, out_specs=o_spec, out_shape=out_shape,
            scratch_shapes=[pltpu.VMEM((tm, tn), F32)],
            compiler_params=_params("parallel", "parallel", "arbitrary"),
        )(*args)
    outs = pl.pallas_call(
        body, name=name, grid=(gm, gn, nk),
        in_specs=in_specs + [HBM_SPEC] * n_ci, out_specs=[o_spec] + [HBM_SPEC] * n_co,
        out_shape=[out_shape] + list(comm.outs),
        scratch_shapes=[pltpu.VMEM((tm, tn), F32)] + list(comm.sems),
        compiler_params=_params("arbitrary", "arbitrary", "arbitrary"),
    )(*args, *comm.ins)
    return outs[0], outs[1:]


def _pool_mm(a, w, mode, out_dtype, name):
    T = a.shape[0]
    G = POOL_GROUPS
    cg = a.shape[1] // G
    tm = _tile(T, 1024)
    nt = T // tm
    dims = _DIMS[mode]
    if mode == "tn":
        def body(a_ref, d_ref, o_ref):
            part = lax.dot_general(a_ref[...], d_ref[...], dims, preferred_element_type=F32)

            @pl.when(pl.program_id(1) == 0)
            def _():
                o_ref[...] = part

            @pl.when(pl.program_id(1) > 0)
            def _():
                o_ref[...] += part

        return pl.pallas_call(
            body, name=name, grid=(G, nt),
            in_specs=[pl.BlockSpec((tm, cg), lambda g, i: (i, g)), pl.BlockSpec((tm, cg), lambda g, i: (i, g))],
            out_specs=pl.BlockSpec((None, cg, cg), lambda g, i: (g, 0, 0)),
            out_shape=jax.ShapeDtypeStruct((G, cg, cg), F32),
            compiler_params=_params("parallel", "arbitrary"),
        )(a, w)

    def body(a_ref, w_ref, o_ref):
        o_ref[...] = lax.dot_general(a_ref[...], w_ref[...], dims, preferred_element_type=F32).astype(out_dtype)

    return pl.pallas_call(
        body, name=name, grid=(G, nt),
        in_specs=[pl.BlockSpec((tm, cg), lambda g, i: (i, g)), pl.BlockSpec((None, cg, cg), lambda g, i: (g, 0, 0))],
        out_specs=pl.BlockSpec((tm, cg), lambda g, i: (i, g)),
        out_shape=jax.ShapeDtypeStruct((T, G * cg), out_dtype),
        compiler_params=_params("parallel", "parallel"),
    )(a, w)


ROWS = 256
HALO_BLOCK = 16


def _row_spec(d, col=0, rows=ROWS):
    return pl.BlockSpec((rows, d), lambda i, col=col: (i, col))


def _const_spec(shape):
    return pl.BlockSpec(shape, lambda *_: (0,) * len(shape))


def _rms_fwd(x, g, name):
    T, D = x.shape

    def body(x_ref, g_ref, h_ref):
        xv = x_ref[...]
        r = lax.rsqrt(jnp.mean(xv * xv, axis=-1, keepdims=True) + EPS)
        h_ref[...] = (xv * r * g_ref[...]).astype(BF16)

    return pl.pallas_call(
        body, name=name, grid=(T // ROWS,),
        in_specs=[_row_spec(D), _const_spec((1, D))], out_specs=_row_spec(D),
        out_shape=jax.ShapeDtypeStruct((T, D), BF16), compiler_params=_params("parallel"),
    )(x, g)


def _accumulate(ref, part):
    first = pl.program_id(0) == 0

    @pl.when(first)
    def _():
        ref[...] = part

    @pl.when(jnp.logical_not(first))
    def _():
        ref[...] += part


def _rms_bwd(x, g, dh, dres, name):
    T, D = x.shape

    def body(x_ref, g_ref, dh_ref, dres_ref, dx_ref, dxb_ref, dg_ref):
        xv = x_ref[...]
        r = lax.rsqrt(jnp.mean(xv * xv, axis=-1, keepdims=True) + EPS)
        xhat = xv * r
        dh_v = dh_ref[...]
        dxhat = dh_v * g_ref[...]
        dx = dres_ref[...] + r * (dxhat - xhat * jnp.mean(dxhat * xhat, axis=-1, keepdims=True))
        dx_ref[...] = dx
        dxb_ref[...] = dx.astype(BF16)
        _accumulate(dg_ref, jnp.sum(dh_v * xhat, axis=0, keepdims=True))

    return pl.pallas_call(
        body, name=name, grid=(T // ROWS,),
        in_specs=[_row_spec(D), _const_spec((1, D)), _row_spec(D), _row_spec(D)],
        out_specs=[_row_spec(D), _row_spec(D), _const_spec((1, D))],
        out_shape=[jax.ShapeDtypeStruct((T, D), F32), jax.ShapeDtypeStruct((T, D), BF16),
                   jax.ShapeDtypeStruct((1, D), F32)],
        compiler_params=_params("arbitrary"),
    )(x, g, dh, dres)


def _loss_head(x, g, target, name):
    T, D = x.shape

    def body(x_ref, g_ref, t_ref, loss_ref, dx_ref, dxb_ref, dg_ref):
        xv = x_ref[...]
        gv = g_ref[...]
        r = lax.rsqrt(jnp.mean(xv * xv, axis=-1, keepdims=True) + EPS)
        xhat = xv * r
        err = xhat * gv - t_ref[...]
        loss = 0.5 * jnp.sum(jnp.mean(err * err, axis=-1, keepdims=True), axis=0, keepdims=True)
        dy = err * (1.0 / D)
        dxhat = dy * gv
        dx = r * (dxhat - xhat * jnp.mean(dxhat * xhat, axis=-1, keepdims=True))
        dx_ref[...] = dx
        dxb_ref[...] = dx.astype(BF16)
        _accumulate(loss_ref, loss)
        _accumulate(dg_ref, jnp.sum(dy * xhat, axis=0, keepdims=True))

    return pl.pallas_call(
        body, name=name, grid=(T // ROWS,),
        in_specs=[_row_spec(D), _const_spec((1, D)), _row_spec(D)],
        out_specs=[_const_spec((1, 1)), _row_spec(D), _row_spec(D), _const_spec((1, D))],
        out_shape=[jax.ShapeDtypeStruct((1, 1), F32), jax.ShapeDtypeStruct((T, D), F32),
                   jax.ShapeDtypeStruct((T, D), BF16), jax.ShapeDtypeStruct((1, D), F32)],
        compiler_params=_params("arbitrary"),
    )(x, g, target)


def _halo_specs(d, col, n_blocks):
    per = ROWS // HALO_BLOCK
    last = n_blocks * per - 1
    prev = pl.BlockSpec((HALO_BLOCK, d), lambda i, col=col: (jnp.maximum(i * per - 1, 0), col))
    nxt = pl.BlockSpec((HALO_BLOCK, d), lambda i, col=col: (jnp.minimum((i + 1) * per, last), col))
    return prev, nxt


def _with_halo(prev, cur, nxt, n_blocks):
    i = pl.program_id(0)
    prev = jnp.where(i > 0, prev[HALO_BLOCK - HALO:], 0.0)
    nxt = jnp.where(i < n_blocks - 1, nxt[:HALO], 0.0)
    return jnp.concatenate([prev, cur, nxt], axis=0)


def _f32(ref):
    return ref[...].astype(F32)


def _shift(ext, k):
    n = ext.shape[0]
    v = ext if k == 0 else pltpu.roll(ext, (-k) % n, 0)
    return v[HALO:HALO + ROWS]


def _shift_full(ext, k):
    n = ext.shape[0]
    return pltpu.roll(ext, (-k) % n, 0)


def _pool_counts(T):
    n = ROWS + 2 * HALO
    t = pl.program_id(0) * ROWS - HALO + lax.broadcasted_iota(jnp.int32, (n, 1), 0)
    out = []
    for w in POOL_WINDOWS:
        lo = jnp.maximum(t - w // 2, 0)
        hi = jnp.minimum(t + (w - 1 - w // 2), T - 1)
        out.append(jnp.maximum(hi - lo + 1, 1).astype(F32))
    return out


def _window_sums(e, sign):
    s2 = e + _shift_full(e, -sign)
    s4 = _shift_full(s2, -1) + _shift_full(s2, 1)
    s8 = _shift_full(s4, -2) + _shift_full(s4, 2)
    s16 = _shift_full(s8, -4) + _shift_full(s8, 4)
    return s2, s4, s8, s16


def _mixer_fwd(proj, conv_w, name):
    T = proj.shape[0]
    W = conv_w.shape[1]
    nb = T // ROWS
    cg = W // POOL_GROUPS

    def body(b_ref, c_ref, x_ref, u_ref, cp_ref, cn_ref, xp_ref, xn_ref, up_ref, un_ref, w_ref, z_ref, p_ref):
        uc = _with_halo(_f32(cp_ref) * _f32(xp_ref), _f32(c_ref) * _f32(x_ref), _f32(cn_ref) * _f32(xn_ref), nb)
        w0, w1, w2 = w_ref[0:1, :], w_ref[1:2, :], w_ref[2:3, :]
        y = w0 * _shift(uc, -1) + w1 * _shift(uc, 0) + w2 * _shift(uc, 1)
        z_ref[...] = (_f32(b_ref) * y).astype(BF16)
        e = _with_halo(_f32(up_ref), _f32(u_ref), _f32(un_ref), nb)
        counts = _pool_counts(T)
        for gi in range(POOL_GROUPS):
            eg = e[:, gi * cg:(gi + 1) * cg]
            s = _window_sums(eg, 1)[gi]
            p = s[HALO:HALO + ROWS] / counts[gi][HALO:HALO + ROWS] - eg[HALO:HALO + ROWS]
            p_ref[:, gi * cg:(gi + 1) * cg] = p.astype(BF16)

    halo = [s for col in (1, 2, 3) for s in _halo_specs(W, col, nb)]
    return pl.pallas_call(
        body, name=name, grid=(nb,),
        in_specs=[_row_spec(W, 0), _row_spec(W, 1), _row_spec(W, 2), _row_spec(W, 3)] + halo + [_const_spec((8, W))],
        out_specs=[_row_spec(W), _row_spec(W)],
        out_shape=[jax.ShapeDtypeStruct((T, W), BF16), jax.ShapeDtypeStruct((T, W), BF16)],
        compiler_params=_params("parallel"),
    )(proj, proj, proj, proj, proj, proj, proj, proj, proj, proj, conv_w)


def _mixer_bwd(proj, conv_w, dz, dp, dproj, name):
    T = proj.shape[0]
    W = conv_w.shape[1]
    nb = T // ROWS
    cg = W // POOL_GROUPS

    def body(b_ref, c_ref, x_ref, dz_ref, dp_ref,
             bp_ref, bn_ref, cp_ref, cn_ref, xp_ref, xn_ref, dzp_ref, dzn_ref, dpp_ref, dpn_ref, w_ref, _,
             o_ref, dw_ref):
        cv, xv, dzv = _f32(c_ref), _f32(x_ref), _f32(dz_ref)
        uc = _with_halo(_f32(cp_ref) * _f32(xp_ref), cv * xv, _f32(cn_ref) * _f32(xn_ref), nb)
        dy = _with_halo(_f32(dzp_ref) * _f32(bp_ref), dzv * _f32(b_ref), _f32(dzn_ref) * _f32(bn_ref), nb)
        w0, w1, w2 = w_ref[0:1, :], w_ref[1:2, :], w_ref[2:3, :]
        um, u0, up = _shift(uc, -1), _shift(uc, 0), _shift(uc, 1)
        o_ref[:, 0:W] = (dzv * (w0 * um + w1 * u0 + w2 * up)).astype(BF16)
        dy0 = _shift(dy, 0)
        duc = w0 * _shift(dy, 1) + w1 * dy0 + w2 * _shift(dy, -1)
        o_ref[:, W:2 * W] = (duc * xv).astype(BF16)
        o_ref[:, 2 * W:3 * W] = (duc * cv).astype(BF16)
        row = lax.broadcasted_iota(jnp.int32, (8, W), 0)
        dw = jnp.where(row == 0, jnp.sum(dy0 * um, axis=0, keepdims=True),
                       jnp.where(row == 1, jnp.sum(dy0 * u0, axis=0, keepdims=True),
                                 jnp.where(row == 2, jnp.sum(dy0 * up, axis=0, keepdims=True), 0.0)))
        _accumulate(dw_ref, dw)
        d = _with_halo(_f32(dpp_ref), _f32(dp_ref), _f32(dpn_ref), nb)
        counts = _pool_counts(T)
        for gi in range(POOL_GROUPS):
            dg = d[:, gi * cg:(gi + 1) * cg]
            s = _window_sums(dg / counts[gi], -1)[gi]
            o_ref[:, 3 * W + gi * cg:3 * W + (gi + 1) * cg] = (s[HALO:HALO + ROWS] - dg[HALO:HALO + ROWS]).astype(BF16)

    def halo(col):
        return list(_halo_specs(W, col, nb))

    return pl.pallas_call(
        body, name=name, grid=(nb,),
        in_specs=[_row_spec(W, 0), _row_spec(W, 1), _row_spec(W, 2), _row_spec(W), _row_spec(W)]
        + halo(0) + halo(1) + halo(2) + halo(0) + halo(0) + [_const_spec((8, W)), ANY_SPEC],
        out_specs=[_row_spec(4 * W), _const_spec((8, W))],
        out_shape=[jax.ShapeDtypeStruct(dproj.shape, BF16), jax.ShapeDtypeStruct((8, W), F32)],
        input_output_aliases={16: 0}, compiler_params=_params("arbitrary"),
    )(proj, proj, proj, dz, dp, proj, proj, proj, proj, proj, proj, dz, dz, dp, dp, conv_w, dproj)


def _t5_bucket(rel):
    half = N_BUCKETS // 2
    max_exact = half // 2
    ret = jnp.where(rel > 0, half, 0)
    n = jnp.abs(rel)
    nf = jnp.maximum(n, 1).astype(jnp.float32)
    large = max_exact + (jnp.log(nf / max_exact) / math.log(MAX_DISTANCE / max_exact)
                         * (half - max_exact)).astype(jnp.int32)
    large = jnp.minimum(large, half - 1)
    return ret + jnp.where(n < max_exact, n, large)


def _bucket_table():
    qi = jnp.arange(BLOCK)[:, None]
    kj = jnp.arange(3 * BLOCK)[None, :]
    rel = kj - BLOCK - qi
    return jnp.where(jnp.abs(rel) <= WINDOW, _t5_bucket(rel), -1).astype(jnp.int32)


def _bias_table(rel_bias, bucket, name):
    def body(rb_ref, bucket_ref, o_ref):
        h = pl.program_id(0)
        bk = bucket_ref[...]
        acc = jnp.full(bk.shape, NEG_INF, F32)
        for b in range(N_BUCKETS):
            acc = jnp.where(bk == b, rb_ref[b, h], acc)
        o_ref[...] = acc

    return pl.pallas_call(
        body, name=name, grid=(N_HEADS,),
        in_specs=[pl.BlockSpec(memory_space=pltpu.SMEM), _const_spec((BLOCK, 3 * BLOCK))],
        out_specs=pl.BlockSpec((None, BLOCK, 3 * BLOCK), lambda h: (h, 0, 0)),
        out_shape=jax.ShapeDtypeStruct((N_HEADS, BLOCK, 3 * BLOCK), F32),
        compiler_params=_params("parallel"),
    )(rel_bias, bucket)


def _bias_grad(ds_sum, bucket, name):
    def body(ds_ref, bucket_ref, o_ref):
        bk = bucket_ref[...]
        ds = ds_ref[...]
        row = lax.broadcasted_iota(jnp.int32, (N_BUCKETS, 128), 0)
        acc = jnp.zeros((N_BUCKETS, 128), F32)
        for b in range(N_BUCKETS):
            s = jnp.sum(jnp.sum(jnp.where(bk == b, ds, 0.0), axis=1, keepdims=True), axis=0, keepdims=True)
            acc = jnp.where(row == b, s, acc)
        o_ref[...] = acc

    return pl.pallas_call(
        body, name=name, grid=(N_HEADS,),
        in_specs=[pl.BlockSpec((None, BLOCK, 3 * BLOCK), lambda h: (h, 0, 0)), _const_spec((BLOCK, 3 * BLOCK))],
        out_specs=pl.BlockSpec((None, N_BUCKETS, 128), lambda h: (h, 0, 0)),
        out_shape=jax.ShapeDtypeStruct((N_HEADS, N_BUCKETS, 128), F32),
        compiler_params=_params("parallel"),
    )(ds_sum, bucket)


PAIR = 2 * HEAD_DIM
Q_BLOCKS = 2


def _low_half(shape):
    return lax.broadcasted_iota(jnp.int32, shape, len(shape) - 1) % PAIR < HEAD_DIM


def _split_pair(a):
    low = _low_half(a.shape)
    zero = jnp.zeros_like(a)
    return jnp.concatenate([jnp.where(low, a, zero), jnp.where(low, zero, a)], axis=0)


def _kv_expand(proj, kv_off, name):
    T = proj.shape[0]
    kv_w = N_KV_HEADS * HEAD_DIM
    rows = _tile(T, 512)

    def body(k_ref, v_ref, ke_ref, ve_ref):
        for src, dst in ((k_ref, ke_ref), (v_ref, ve_ref)):
            for g in range(N_KV_HEADS // 2):
                x = src[:, g * PAIR:(g + 1) * PAIR].astype(F32)
                swapped = pltpu.roll(x, HEAD_DIM, 1)
                low = _low_half(x.shape)
                dst[:, 2 * g * PAIR:(2 * g + 1) * PAIR] = jnp.where(low, x, swapped).astype(BF16)
                dst[:, (2 * g + 1) * PAIR:(2 * g + 2) * PAIR] = jnp.where(low, swapped, x).astype(BF16)

    out = jax.ShapeDtypeStruct((T, N_KV_HEADS * PAIR), BF16)
    ospec = pl.BlockSpec((rows, N_KV_HEADS * PAIR), lambda i: (i, 0))
    return pl.pallas_call(
        body, name=name, grid=(T // rows,),
        in_specs=[pl.BlockSpec((rows, kv_w), lambda i: (i, kv_off // kv_w)),
                  pl.BlockSpec((rows, kv_w), lambda i: (i, kv_off // kv_w + 1))],
        out_specs=[ospec, ospec], out_shape=[out, out], compiler_params=_params("parallel"),
    )(proj, proj)


def _kv_fold(dke, dve, dproj, kv_off, name):
    T = dke.shape[0]
    kv_w = N_KV_HEADS * HEAD_DIM
    rows = _tile(T, 512)

    def body(dk_ref, dv_ref, _, o_ref):
        for n, src in enumerate((dk_ref, dv_ref)):
            for g in range(N_KV_HEADS // 2):
                a = src[:, 2 * g * PAIR:(2 * g + 1) * PAIR]
                b = src[:, (2 * g + 1) * PAIR:(2 * g + 2) * PAIR]
                a = a + pltpu.roll(a, HEAD_DIM, 1)
                b = b + pltpu.roll(b, HEAD_DIM, 1)
                o_ref[:, n * kv_w + g * PAIR:n * kv_w + (g + 1) * PAIR] = jnp.where(_low_half(a.shape), a, b).astype(BF16)

    ispec = pl.BlockSpec((rows, N_KV_HEADS * PAIR), lambda i: (i, 0))
    return pl.pallas_call(
        body, name=name, grid=(T // rows,), in_specs=[ispec, ispec, ANY_SPEC],
        out_specs=pl.BlockSpec((rows, 2 * kv_w), lambda i: (i, kv_off // (2 * kv_w))),
        out_shape=jax.ShapeDtypeStruct(dproj.shape, BF16), input_output_aliases={2: 0},
        compiler_params=_params("parallel"),
    )(dke, dve, dproj)


def _key_blocks(i, nb):
    return [pl.multiple_of(n * BLOCK, BLOCK) for n in (jnp.maximum(i - 1, 0), i, jnp.minimum(i + 1, nb - 1))]


def _three_blocks(ref, starts):
    return jnp.concatenate([ref[pl.ds(s, BLOCK), :] for s in starts], axis=0)


def _pair_scores(q2, kd, bias_ref, pr, i, nb):
    qq = _split_pair(q2)
    s = lax.dot_general(qq, kd, _DIMS["nt"], preferred_element_type=F32) * (HEAD_DIM ** -0.5)
    s = s + bias_ref[2 * pr:2 * pr + 2].reshape(2 * BLOCK, 3 * BLOCK)
    kj = lax.broadcasted_iota(jnp.int32, (1, 3 * BLOCK), 1)
    outside = jnp.logical_or(jnp.logical_and(i == 0, kj < BLOCK), jnp.logical_and(i == nb - 1, kj >= 2 * BLOCK))
    return qq, jnp.where(outside, NEG_INF, s)


def _attn_specs(T, q_off):
    gw = GROUP * HEAD_DIM
    return dict(
        sink=pl.BlockSpec(memory_space=pltpu.SMEM),
        q=pl.BlockSpec((Q_BLOCKS * BLOCK, gw), lambda j, i: (i, q_off // gw + j)),
        kv=pl.BlockSpec((T, PAIR), lambda j, i: (0, j)),
        bias=pl.BlockSpec((GROUP, BLOCK, 3 * BLOCK), lambda j, i: (j, 0, 0)),
        o=pl.BlockSpec((Q_BLOCKS * BLOCK, gw), lambda j, i: (i, j)))


def _attn_fwd(proj, q_off, kexp, vexp, bias, sink, name, comm=None):
    T = proj.shape[0]
    nb = T // BLOCK
    sp = _attn_specs(T, q_off)
    steps = N_KV_HEADS * (nb // Q_BLOCKS)
    n_ci = len(comm.ins) if comm is not None else 0
    n_co = len(comm.outs) if comm is not None else 0

    def body(*refs):
        sink_ref, q_ref, ke_ref, ve_ref, bias_ref = refs[:5]
        comm_in = refs[5:5 + n_ci]
        o_ref, lse_ref = refs[5 + n_ci:7 + n_ci]
        comm_out = refs[7 + n_ci:7 + n_ci + n_co]
        sems = refs[7 + n_ci + n_co:]
        j, i = pl.program_id(0), pl.program_id(1)
        step = j * (nb // Q_BLOCKS) + i
        if comm is not None:
            @pl.when(step == 0)
            def _():
                comm.start(comm_in, comm_out, sems)

        first_rows = lax.broadcasted_iota(jnp.int32, (2 * BLOCK, 1), 0) < BLOCK
        low = _low_half((BLOCK, PAIR))
        for b in range(Q_BLOCKS):
            blk = i * Q_BLOCKS + b
            rows = slice(b * BLOCK, (b + 1) * BLOCK)
            starts = _key_blocks(blk, nb)
            kd = _three_blocks(ke_ref, starts)
            vv = _split_pair(_three_blocks(ve_ref, starts))
            for pr in range(GROUP // 2):
                lanes = slice(pr * PAIR, (pr + 1) * PAIR)
                _, s = _pair_scores(q_ref[rows, lanes], kd, bias_ref, pr, blk, nb)
                head = GROUP * j + 2 * pr
                sk = jnp.where(first_rows, sink_ref[head], sink_ref[head + 1])
                m = jnp.maximum(jnp.max(s, axis=-1, keepdims=True), sk)
                p = jnp.exp(s - m)
                denom = jnp.sum(p, axis=-1, keepdims=True) + jnp.exp(sk - m)
                p = (p / denom).astype(BF16)
                pp = jnp.concatenate([p[:BLOCK], p[BLOCK:]], axis=1)
                o_ref[rows, lanes] = lax.dot_general(pp, vv, _DIMS["nn"], preferred_element_type=F32).astype(BF16)
                lse = m + jnp.log(denom)
                lse_ref[rows, lanes] = jnp.where(low, lse[:BLOCK], lse[BLOCK:])

        if comm is not None:
            @pl.when(step == (3 * steps) // 4)
            def _():
                comm.mid(comm_in, comm_out, sems)

            @pl.when(step == steps - 1)
            def _():
                comm.finish(comm_in, comm_out, sems)

    out_shape = [jax.ShapeDtypeStruct((T, N_HEADS * HEAD_DIM), BF16), jax.ShapeDtypeStruct((T, N_HEADS * HEAD_DIM), F32)]
    in_specs = [sp["sink"], sp["q"], sp["kv"], sp["kv"], sp["bias"]]
    if comm is None:
        att, lse = pl.pallas_call(
            body, name=name, grid=(N_KV_HEADS, nb // Q_BLOCKS), in_specs=in_specs, out_specs=[sp["o"], sp["o"]],
            out_shape=out_shape, compiler_params=_params("parallel", "parallel"),
        )(sink, proj, kexp, vexp, bias)
        return att, lse, []
    outs = pl.pallas_call(
        body, name=name, grid=(N_KV_HEADS, nb // Q_BLOCKS),
        in_specs=in_specs + [HBM_SPEC] * n_ci, out_specs=[sp["o"], sp["o"]] + [HBM_SPEC] * n_co,
        out_shape=out_shape + list(comm.outs), scratch_shapes=list(comm.sems),
        compiler_params=_params("arbitrary", "arbitrary"),
    )(sink, proj, kexp, vexp, bias, *comm.ins)
    return outs[0], outs[1], outs[2:]


def _attn_bwd(proj, q_off, kexp, vexp, bias, sink, out, lse, dout, dproj, name, comm=None):
    T = proj.shape[0]
    nb = T // BLOCK
    sp = _attn_specs(T, q_off)
    scale = HEAD_DIM ** -0.5
    steps = N_KV_HEADS * (nb // Q_BLOCKS)
    n_ci = len(comm.ins) if comm is not None else 0
    n_co = len(comm.outs) if comm is not None else 0

    def body(*refs):
        sink_ref, q_ref, ke_ref, ve_ref, bias_ref, o_ref, lse_ref, do_ref = refs[:8]
        comm_in = refs[9:9 + n_ci]
        dq_ref, dke_ref, dve_ref, ds_ref, dsink_ref = refs[9 + n_ci:14 + n_ci]
        comm_out = refs[14 + n_ci:14 + n_ci + n_co]
        sems = refs[14 + n_ci + n_co:]
        j, i = pl.program_id(0), pl.program_id(1)
        step = j * (nb // Q_BLOCKS) + i
        if comm is not None:
            @pl.when(step == 0)
            def _():
                comm.start(comm_in, comm_out, sems)

        @pl.when(i == 0)
        def _():
            dke_ref[...] = jnp.zeros(dke_ref.shape, F32)
            dve_ref[...] = jnp.zeros(dve_ref.shape, F32)
            ds_ref[...] = jnp.zeros(ds_ref.shape, F32)
            dsink_ref[...] = jnp.zeros(dsink_ref.shape, F32)

        low = _low_half((BLOCK, PAIR))
        for b in range(Q_BLOCKS):
            blk = i * Q_BLOCKS + b
            rows = slice(b * BLOCK, (b + 1) * BLOCK)
            starts = _key_blocks(blk, nb)
            kd = _three_blocks(ke_ref, starts)
            vd = _three_blocks(ve_ref, starts)
            kk = _split_pair(kd)
            dk_acc = jnp.zeros((3 * BLOCK, PAIR), F32)
            dv_acc = jnp.zeros((3 * BLOCK, PAIR), F32)
            for pr in range(GROUP // 2):
                lanes = slice(pr * PAIR, (pr + 1) * PAIR)
                qq, s = _pair_scores(q_ref[rows, lanes], kd, bias_ref, pr, blk, nb)
                l2 = lse_ref[rows, lanes]
                l2s = pltpu.roll(l2, HEAD_DIM, 1)
                lse_a, lse_b = jnp.where(low, l2, l2s), jnp.where(low, l2s, l2)
                p = jnp.exp(s - jnp.concatenate([jnp.concatenate([lse_a] * 3, axis=1),
                                                 jnp.concatenate([lse_b] * 3, axis=1)], axis=0))
                do2 = do_ref[rows, lanes]
                prod = do2.astype(F32) * o_ref[rows, lanes].astype(F32)
                delta_a = jnp.sum(jnp.where(low, prod, 0.0), axis=-1, keepdims=True)
                delta_b = jnp.sum(jnp.where(low, 0.0, prod), axis=-1, keepdims=True)
                dd = _split_pair(do2)
                dp = lax.dot_general(dd, vd, _DIMS["nt"], preferred_element_type=F32)
                ds = p * (dp - jnp.concatenate([delta_a, delta_b], axis=0))
                dsb = ds.astype(BF16)
                dq = lax.dot_general(jnp.concatenate([dsb[:BLOCK], dsb[BLOCK:]], axis=1), kk, _DIMS["nn"],
                                     preferred_element_type=F32) * scale
                dq_ref[rows, lanes] = dq.astype(BF16)
                dk_acc += lax.dot_general(dsb, qq, _DIMS["tn"], preferred_element_type=F32) * scale
                dv_acc += lax.dot_general(p.astype(BF16), dd, _DIMS["tn"], preferred_element_type=F32)
                ds_ref[2 * pr:2 * pr + 2] += ds.reshape(2, BLOCK, 3 * BLOCK)
                head = GROUP * j + 2 * pr
                p_sink = jnp.exp(jnp.where(low, sink_ref[head], sink_ref[head + 1]) - l2)
                dsink_ref[:, lanes] += jnp.sum(-p_sink * jnp.where(low, delta_a, delta_b), axis=0, keepdims=True)
            for t, start in enumerate(starts):
                dke_ref[pl.ds(start, BLOCK), :] += dk_acc[t * BLOCK:(t + 1) * BLOCK]
                dve_ref[pl.ds(start, BLOCK), :] += dv_acc[t * BLOCK:(t + 1) * BLOCK]

        if comm is not None:
            @pl.when(step == (3 * steps) // 4)
            def _():
                comm.mid(comm_in, comm_out, sems)

            @pl.when(step == steps - 1)
            def _():
                comm.finish(comm_in, comm_out, sems)

    kv_out = jax.ShapeDtypeStruct((T, N_KV_HEADS * PAIR), F32)
    job_ins, job_outs, job_sems = (comm.ins, comm.outs, comm.sems) if comm is not None else ([], [], [])
    outs = pl.pallas_call(
        body, name=name, grid=(N_KV_HEADS, nb // Q_BLOCKS),
        in_specs=[sp["sink"], sp["q"], sp["kv"], sp["kv"], sp["bias"], sp["o"], sp["o"], sp["o"], ANY_SPEC]
        + [HBM_SPEC] * n_ci,
        out_specs=[sp["q"], sp["kv"], sp["kv"], sp["bias"],
                   pl.BlockSpec((1, GROUP * HEAD_DIM), lambda j, i: (0, j))] + [HBM_SPEC] * n_co,
        out_shape=[jax.ShapeDtypeStruct(dproj.shape, BF16), kv_out, kv_out,
                   jax.ShapeDtypeStruct((N_HEADS, BLOCK, 3 * BLOCK), F32),
                   jax.ShapeDtypeStruct((1, N_HEADS * HEAD_DIM), F32)] + list(job_outs),
        scratch_shapes=list(job_sems), input_output_aliases={8: 0},
        compiler_params=_params("arbitrary" if comm is not None else "parallel", "arbitrary"),
    )(sink, proj, kexp, vexp, bias, out, lse, dout, dproj, *job_ins)
    return outs[:5], outs[5:]


GATE_COLS = 512


def _sigmoid(x):
    return 1.0 / (1.0 + jnp.exp(-x))


def _gate_specs(D, gate_off):
    nc = D // GATE_COLS
    base = gate_off // GATE_COLS
    return [pl.BlockSpec((ROWS, GATE_COLS), lambda i, c=base + g * nc + h: (i, c)) for g in range(3) for h in range(nc)]


def _merge_fwd(proj, gate_off, ya, yp, yt, scale, name):
    T, D = ya.shape
    nc = D // GATE_COLS

    def body(*refs):
        gates = refs[:3 * nc]
        ya_ref, yp_ref, yt_ref, s_ref, o_ref = refs[3 * nc:]
        for h in range(nc):
            cols = slice(h * GATE_COLS, (h + 1) * GATE_COLS)
            merged = (_sigmoid(_f32(gates[h])) * ya_ref[:, cols].astype(F32)
                      + _sigmoid(_f32(gates[nc + h])) * (yp_ref[:, cols].astype(F32) * s_ref[:, cols])
                      + _sigmoid(_f32(gates[2 * nc + h])) * yt_ref[:, cols].astype(F32))
            o_ref[:, cols] = merged.astype(BF16)

    yspec = _row_spec(D)
    return pl.pallas_call(
        body, name=name, grid=(T // ROWS,),
        in_specs=_gate_specs(D, gate_off) + [yspec, yspec, yspec, _const_spec((1, D))], out_specs=yspec,
        out_shape=jax.ShapeDtypeStruct((T, D), BF16), compiler_params=_params("parallel"),
    )(*([proj] * (3 * nc)), ya, yp, yt, scale)


def _merge_bwd(proj, gate_off, ya, yp, yt, scale, dm, name):
    T, D = ya.shape
    nc = D // GATE_COLS
    base = gate_off // GATE_COLS

    def body(gate_ref, ya_ref, yp_ref, yt_ref, s_ref, dm_ref, dg_ref, dya_ref, dyp_ref, dyt_ref, ds_ref):
        i, n = pl.program_id(0), pl.program_id(1)
        sg = _sigmoid(_f32(gate_ref))
        for g, (y_ref, dy_ref) in enumerate(((ya_ref, dya_ref), (yp_ref, dyp_ref), (yt_ref, dyt_ref))):
            for h in range(nc):
                @pl.when(n == g * nc + h)
                def _(g=g, h=h, y_ref=y_ref, dy_ref=dy_ref):
                    cols = slice(h * GATE_COLS, (h + 1) * GATE_COLS)
                    dy = dm_ref[:, cols].astype(F32) * sg
                    y = y_ref[:, cols].astype(F32)
                    if g == 1:
                        s_v = s_ref[:, cols]
                        part = jnp.sum(dy * y, axis=0, keepdims=True)

                        @pl.when(i == 0)
                        def _():
                            ds_ref[:, cols] = part

                        @pl.when(i > 0)
                        def _():
                            ds_ref[:, cols] += part

                        y = y * s_v
                        dy_ref[:, cols] = (dy * s_v).astype(BF16)
                    else:
                        dy_ref[:, cols] = dy.astype(BF16)
                    dg_ref[...] = (dy * y * (1.0 - sg)).astype(BF16)

    rows = _tile(T, 4 * ROWS)
    yspec = pl.BlockSpec((rows, D), lambda i, n: (i, 0))
    gspec = pl.BlockSpec((rows, GATE_COLS), lambda i, n: (i, base + n))
    sspec = pl.BlockSpec((1, D), lambda i, n: (0, 0))
    out = jax.ShapeDtypeStruct((T, D), BF16)
    return pl.pallas_call(
        body, name=name, grid=(T // rows, 3 * nc),
        in_specs=[gspec, yspec, yspec, yspec, sspec, yspec],
        out_specs=[gspec, yspec, yspec, yspec, sspec],
        out_shape=[jax.ShapeDtypeStruct(proj.shape, BF16), out, out, out, jax.ShapeDtypeStruct((1, D), F32)],
        compiler_params=_params("arbitrary", "arbitrary"),
    )(proj, ya, yp, yt, scale, dm)


def _swiglu_fwd(gu, name):
    T = gu.shape[0]
    F = gu.shape[1] // 2

    def body(gu_ref, o_ref):
        g = gu_ref[:, 0:F].astype(F32)
        o_ref[...] = (g * _sigmoid(g) * gu_ref[:, F:2 * F].astype(F32)).astype(BF16)

    return pl.pallas_call(
        body, name=name, grid=(T // ROWS,), in_specs=[_row_spec(2 * F)], out_specs=_row_spec(F),
        out_shape=jax.ShapeDtypeStruct((T, F), BF16), compiler_params=_params("parallel"),
    )(gu)


def _swiglu_bwd(gu, dact, name):
    T = gu.shape[0]
    F = gu.shape[1] // 2

    def body(gu_ref, d_ref, o_ref):
        g, d = gu_ref[:, 0:F].astype(F32), d_ref[...].astype(F32)
        sg = _sigmoid(g)
        o_ref[:, 0:F] = (d * gu_ref[:, F:2 * F].astype(F32) * sg * (1.0 + g * (1.0 - sg))).astype(BF16)
        o_ref[:, F:2 * F] = (d * g * sg).astype(BF16)

    return pl.pallas_call(
        body, name=name, grid=(T // ROWS,), in_specs=[_row_spec(2 * F), _row_spec(F)], out_specs=_row_spec(2 * F),
        out_shape=jax.ShapeDtypeStruct((T, 2 * F), BF16), compiler_params=_params("parallel"),
    )(gu, dact)


def _carried(plan, key, *args, **kwargs):
    job = plan.job(key) if plan is not None else None
    if job is None:
        return _matmul(*args, **kwargs)
    out, extra = _matmul(*args, comm=job, **kwargs)
    plan.done(key, extra)
    return out


def _local_step(x, target, wts, small, hooks=None):
    T, D = x.shape
    depth = small["g_mix"].shape[0]
    wts = list(wts) + [None] * (depth - len(wts))
    gate_off = wts[0]["w_inT"].shape[0] - 3 * D
    q_off = 4 * D
    bucket = _bucket_table()
    bias = _bias_table(small["rel_bias"], bucket, "bias_table")

    saved = []
    for l in range(depth):
        n = f"l{l}_"
        if hooks is not None and l > 0:
            wts[l] = hooks.weights(l)
        w = wts[l]
        plan = hooks.plan_fwd(l) if hooks is not None else None
        h = _rms_fwd(x, small["g_mix"][l], n + "rms_mix")
        proj = _carried(plan, "proj", h, w["w_inT"], "nt", BF16, n + "proj")
        z, p = _mixer_fwd(proj, small["conv_w"][l], n + "mixer")
        kexp, vexp = _kv_expand(proj, q_off + D, n + "kv_expand")
        sink = small["attn_sink"][l]
        job = plan.job("attn") if plan is not None else None
        att, lse, extra = _attn_fwd(proj, q_off, kexp, vexp, bias, sink, n + "attn", comm=job)
        if job is not None:
            plan.done("attn", extra)
        ya =_matmul(z, w["w_a_out"], "nn", BF16, n + "ya")
        yp = _pool_mm(p, w["w_pool"], "nn", BF16, n + "yp")
        yt = _matmul(att, w["w_attn_out"], "nn", BF16, n + "yt")
        merged = _merge_fwd(proj, gate_off, ya, yp, yt, small["pool_scale"][l], n + "merge")
        x1 = _matmul(merged, w["w_o"], "nn", F32, n + "x1", res=x)
        h2 = _rms_fwd(x1, small["g_ffn"][l], n + "rms_ffn")
        gu = _carried(plan, "gu", h2, w["w_guT"], "nt", BF16, n + "gu")
        act = _swiglu_fwd(gu, n + "swiglu")
        ff = w["w_down"].shape[0]
        x2 = _carried(plan, "x2", act, w["w_down"], "nn", F32, n + "x2", res=x1, tn_cap=512, tk_cap=ff)
        saved.append(dict(x=x, h=h, proj=proj, z=z, p=p, kexp=kexp, vexp=vexp, sink=sink, lse=lse, att=att,
                          ya=ya, yp=yp, yt=yt, merged=merged, x1=x1, h2=h2, gu=gu, act=act))
        x = x2

    loss, dx, dxb, dg_final = _loss_head(x, small["g_final"], target, "loss_head")

    gw = [None] * depth
    gs = {k_: [None] * depth for k_ in ("conv_w", "pool_scale", "g_mix", "g_ffn", "attn_sink")}
    ds_total = None
    for l in reversed(range(depth)):
        n = f"l{l}_b_"
        s, w, g = saved[l], wts[l], {}
        plan = hooks.plan_bwd(l) if hooks is not None else None
        ff = w["w_down"].shape[0]
        g["w_down"] = _carried(plan, "dw_down", s["act"], dxb, "tn", BF16, n + "dw_down", tm_cap=ff, tk_cap=512)
        dact = _matmul(dxb, w["w_down"], "nt", BF16, n + "dact", tm_cap=512, tn_cap=ff)
        dgu = _swiglu_bwd(s["gu"], dact, n + "swiglu")
        g["w_guT"] = _carried(plan, "dw_gu", dgu, s["h2"], "tn", BF16, n + "dw_gu", tm_cap=512)
        dh2 = _carried(plan, "dh2", dgu, w["w_guT"], "nn", F32, n + "dh2", tn_cap=512, tk_cap=ff)
        dx1, dx1b, gs["g_ffn"][l] = _rms_bwd(s["x1"], small["g_ffn"][l], dh2, dx, n + "rms_ffn")
        g["w_o"] = _matmul(s["merged"], dx1b, "tn", BF16, n + "dw_o")
        dm = _matmul(dx1b, w["w_o"], "nt", BF16, n + "dmerged")
        dproj, dya, dyp, dyt, gs["pool_scale"][l] = _merge_bwd(
            s["proj"], gate_off, s["ya"], s["yp"], s["yt"], small["pool_scale"][l], dm, n + "merge")
        g["w_a_out"] = _matmul(s["z"], dya, "tn", BF16, n + "dw_a_out")
        dz = _matmul(dya, w["w_a_out"], "nt", BF16, n + "dz")
        g["w_pool"] = _pool_mm(s["p"], dyp, "tn", F32, n + "dw_pool")
        dp = _pool_mm(dyp, w["w_pool"], "nt", BF16, n + "dp")
        g["w_attn_out"] = _matmul(s["att"], dyt, "tn", BF16, n + "dw_attn_out")
        if hooks is not None:
            hooks.early_grads(l, g)
        datt = _carried(plan, "datt", dyt, w["w_attn_out"], "nt", BF16, n + "datt")
        dproj, gs["conv_w"][l] = _mixer_bwd(s["proj"], small["conv_w"][l], dz, dp, dproj, n + "mixer")
        job = plan.job("attn_b") if plan is not None else None
        (dproj, dke, dve, ds_sum, dsink), extra = _attn_bwd(
            s["proj"], q_off, s["kexp"], s["vexp"], bias, s["sink"], s["att"], s["lse"], datt, dproj, n + "attn",
            comm=job)
        if job is not None:
            plan.done("attn_b", extra)
        gs["attn_sink"][l] = dsink.reshape(N_HEADS, HEAD_DIM)[:, 0]
        ds_total = ds_sum if ds_total is None else ds_total + ds_sum
        dproj = _kv_fold(dke, dve, dproj, q_off + D, n + "kv_fold")
        g["w_inT"] = _carried(plan, "dw_in", dproj, s["h"], "tn", BF16, n + "dw_in", tm_cap=512)
        dh = _matmul(dproj, w["w_inT"], "nn", F32, n + "dh", tk_cap=2816)
        dx, dxb, gs["g_mix"][l] = _rms_bwd(s["x"], small["g_mix"][l], dh, dx1, n + "rms_mix")
        gw[l] = g
        if hooks is not None:
            hooks.grads(l, g)

    d_rel =_bias_grad(ds_total, bucket, "bias_grad")[:, :, 0].T
    gs = {k_: jnp.stack(v_) for k_, v_ in gs.items()}
    gs["rel_bias"] = d_rel
    gs["g_final"] = dg_final
    return loss, dx, gw, gs


def _place():
    return lax.axis_index("x"), lax.axis_index("y"), lax.axis_index("c")


class _GatherJob:
    def __init__(self, parts):
        n = len(parts)
        self.n = n
        self.ins = list(parts)
        self.outs = [jax.ShapeDtypeStruct((N_DEV,) + p.shape, p.dtype) for p in parts]
        self.sems = [pltpu.SemaphoreType.DMA((7 * n,)), pltpu.SemaphoreType.DMA((7 * n,)), pltpu.SemaphoreType.DMA((n,))]

    def _copies(self, ins, outs, sems):
        send_sems, recv_sems, local_sems = sems
        x, y, c = _place()
        me, sibling = (x, y, c), (x, y, 1 - c)
        chips = [(1 - x, y), (x, 1 - y), (1 - x, 1 - y)]

        def rows(t, px, py, pc):
            return outs[t].at[4 * px + 2 * py + pc]

        def copy(t, k, block, to, src=None):
            return pltpu.make_async_remote_copy(
                src_ref=rows(t, *block) if src is None else src, dst_ref=rows(t, *block),
                send_sem=send_sems.at[7 * t + k], recv_sem=recv_sems.at[7 * t + k], device_id=to, device_id_type=MESH)

        ts = range(self.n)
        own = [pltpu.make_async_copy(ins[t], rows(t, *me), local_sems.at[t]) for t in ts]
        first = [copy(t, 0, me, sibling, src=ins[t]) for t in ts]
        first += [copy(t, 1 + j, me, (*chip, c), src=ins[t]) for t in ts for j, chip in enumerate(chips)]
        landed = [copy(t, 1 + j, (*chip, c), me) for j, chip in enumerate(chips) for t in ts]
        passed = [copy(t, 4 + j, (*chip, c), sibling) for j, chip in enumerate(chips) for t in ts]
        last = [copy(t, 0, sibling, me) for t in ts]
        last += [copy(t, 4 + j, (*chip, 1 - c), me) for t in ts for j, chip in enumerate(chips)]
        return own, first, landed, passed, last

    def start(self, ins, outs, sems):
        own, first, _, _, _ = self._copies(ins, outs, sems)
        for cp in own + first:
            cp.start()

    def mid(self, ins, outs, sems):
        _, _, landed, passed, _ = self._copies(ins, outs, sems)
        for arrived, onward in zip(landed, passed):
            arrived.wait_recv()
            onward.start()

    def finish(self, ins, outs, sems):
        own, first, _, passed, last = self._copies(ins, outs, sems)
        for cp in last:
            cp.wait_recv()
        for cp in first + passed:
            cp.wait_send()
        for cp in own:
            cp.wait()


class _SwapJob:
    def __init__(self, g):
        self.ins = [g]
        self.outs = [jax.ShapeDtypeStruct(g.shape[:1] + g.shape[2:], g.dtype)]
        self.sems = [pltpu.SemaphoreType.DMA, pltpu.SemaphoreType.DMA]

    def _copy(self, ins, outs, sems):
        x, y, c = _place()
        return pltpu.make_async_remote_copy(src_ref=ins[0].at[pl.ds(0, ins[0].shape[0]), 1 - c], dst_ref=outs[0],
                                            send_sem=sems[0], recv_sem=sems[1], device_id=(x, y, 1 - c),
                                            device_id_type=MESH)

    def start(self, ins, outs, sems):
        self._copy(ins, outs, sems).start()

    def mid(self, ins, outs, sems):
        pass

    def finish(self, ins, outs, sems):
        self._copy(ins, outs, sems).wait()


class _ExchangeJob:
    def __init__(self, p, row0, rows):
        self.row0, self.rows = row0, rows
        self.ins = [p]
        self.outs = [jax.ShapeDtypeStruct((3, rows) + p.shape[2:], p.dtype)]
        self.sems = [pltpu.SemaphoreType.DMA((3,)), pltpu.SemaphoreType.DMA((3,))]

    def _copies(self, ins, outs, sems):
        x, y, c = _place()
        chips = [(1 - x, y), (x, 1 - y), (1 - x, 1 - y)]
        return [pltpu.make_async_remote_copy(
            src_ref=ins[0].at[2 * px + py, pl.ds(self.row0, self.rows)], dst_ref=outs[0].at[k],
            send_sem=sems[0].at[k], recv_sem=sems[1].at[k], device_id=(px, py, c), device_id_type=MESH)
            for k, (px, py) in enumerate(chips)]

    def start(self, ins, outs, sems):
        for cp in self._copies(ins, outs, sems):
            cp.start()

    def mid(self, ins, outs, sems):
        pass

    def finish(self, ins, outs, sems):
        for cp in self._copies(ins, outs, sems):
            cp.wait()


def _all_gather(v, name):
    def body(x_ref, out_ref, send_sems, recv_sems, local_sem):
        x, y, c = _place()
        me, sibling = (x, y, c), (x, y, 1 - c)
        chips = [(1 - x, y), (x, 1 - y), (1 - x, 1 - y)]

        def rows(px, py, pc):
            return out_ref.at[4 * px + 2 * py + pc]

        def copy(k, block, to, src=None):
            return pltpu.make_async_remote_copy(
                src_ref=rows(*block) if src is None else src, dst_ref=rows(*block),
                send_sem=send_sems.at[k], recv_sem=recv_sems.at[k], device_id=to, device_id_type=MESH)

        mine = pltpu.make_async_copy(x_ref, rows(*me), local_sem)
        mine.start()
        first = [copy(0, me, sibling, src=x_ref)]
        first += [copy(1 + j, me, (*chip, c), src=x_ref) for j, chip in enumerate(chips)]
        for cp in first:
            cp.start()
        passed = [copy(4 + j, (*chip, c), sibling) for j, chip in enumerate(chips)]
        for j, chip in enumerate(chips):
            copy(1 + j, (*chip, c), me).wait_recv()
            passed[j].start()
        copy(0, sibling, me).wait_recv()
        for j, chip in enumerate(chips):
            copy(4 + j, (*chip, 1 - c), me).wait_recv()
        for cp in first + passed:
            cp.wait_send()
        mine.wait()

    return pl.pallas_call(
        body, name=name, in_specs=[HBM_SPEC], out_specs=HBM_SPEC,
        out_shape=jax.ShapeDtypeStruct((N_DEV,) + v.shape, v.dtype),
        scratch_shapes=[pltpu.SemaphoreType.DMA((7,)), pltpu.SemaphoreType.DMA((7,)), pltpu.SemaphoreType.DMA],
    )(v)


def _all_gather_many(parts, name):
    n = len(parts)

    def body(*refs):
        ins, outs = refs[:n], refs[n:2 * n]
        send_sems, recv_sems, local_sems = refs[2 * n:]
        x, y, c = _place()
        me, sibling = (x, y, c), (x, y, 1 - c)
        chips = [(1 - x, y), (x, 1 - y), (1 - x, 1 - y)]

        def rows(t, px, py, pc):
            return outs[t].at[4 * px + 2 * py + pc]

        def copy(t, k, block, to, src=None):
            return pltpu.make_async_remote_copy(
                src_ref=rows(t, *block) if src is None else src, dst_ref=rows(t, *block),
                send_sem=send_sems.at[7 * t + k], recv_sem=recv_sems.at[7 * t + k], device_id=to, device_id_type=MESH)

        mine = [pltpu.make_async_copy(ins[t], rows(t, *me), local_sems.at[t]) for t in range(n)]
        sends = []
        for t in range(n):
            mine[t].start()
            sends.append(copy(t, 0, me, sibling, src=ins[t]))
            sends += [copy(t, 1 + j, me, (*chip, c), src=ins[t]) for j, chip in enumerate(chips)]
        for cp in sends:
            cp.start()
        for j, chip in enumerate(chips):
            for t in range(n):
                copy(t, 1 + j, (*chip, c), me).wait_recv()
                passed = copy(t, 4 + j, (*chip, c), sibling)
                passed.start()
                sends.append(passed)
        for t in range(n):
            copy(t, 0, sibling, me).wait_recv()
            for j, chip in enumerate(chips):
                copy(t, 4 + j, (*chip, 1 - c), me).wait_recv()
        for cp in sends:
            cp.wait_send()
        for cp in mine:
            cp.wait()

    return pl.pallas_call(
        body, name=name, in_specs=[HBM_SPEC] * n, out_specs=[HBM_SPEC] * n,
        out_shape=[jax.ShapeDtypeStruct((N_DEV,) + p.shape, p.dtype) for p in parts],
        scratch_shapes=[pltpu.SemaphoreType.DMA((7 * n,)), pltpu.SemaphoreType.DMA((7 * n,)),
                        pltpu.SemaphoreType.DMA((n,))],
    )(*parts)


def _run_job(job, name):
    n_in, n_out = len(job.ins), len(job.outs)

    def body(*refs):
        ins, outs, sems = refs[:n_in], refs[n_in:n_in + n_out], refs[n_in + n_out:]
        job.start(ins, outs, sems)
        job.mid(ins, outs, sems)
        job.finish(ins, outs, sems)

    return pl.pallas_call(
        body, name=name, in_specs=[HBM_SPEC] * n_in, out_specs=[HBM_SPEC] * n_out, out_shape=list(job.outs),
        scratch_shapes=list(job.sems),
    )(*job.ins)


def _chip_exchange(p, name):
    def body(p_ref, out_ref, send_sems, recv_sems):
        x, y, c = _place()
        chips = [(1 - x, y), (x, 1 - y), (1 - x, 1 - y)]
        copies = [pltpu.make_async_remote_copy(
            src_ref=p_ref.at[2 * px + py], dst_ref=out_ref.at[k], send_sem=send_sems.at[k], recv_sem=recv_sems.at[k],
            device_id=(px, py, c), device_id_type=MESH) for k, (px, py) in enumerate(chips)]
        for cp in copies:
            cp.start()
        for cp in copies:
            cp.wait()

    return pl.pallas_call(
        body, name=name, in_specs=[HBM_SPEC], out_specs=HBM_SPEC,
        out_shape=jax.ShapeDtypeStruct((3,) + p.shape[1:], p.dtype),
        scratch_shapes=[pltpu.SemaphoreType.DMA((3,)), pltpu.SemaphoreType.DMA((3,))],
    )(p)


SUM_ROWS_CAP = 576


def _sum_parts(own, index, others, out_dtype, name, own_row0=0, own_step=0):
    R = others.shape[1]
    common = math.gcd(R, own.shape[1], own_row0 or R)
    rows = next(t for t in range(min(common, SUM_ROWS_CAP) // 16 * 16, 0, -16) if common % t == 0)
    k = others.shape[0]
    assert own_row0 % rows == 0 and own.shape[1] % rows == 0
    blk0 = own_row0 // rows
    per_own = own.shape[1] // rows

    def own_block(i, idx):
        if own_step:
            return (idx[0] + own_step * (i // per_own), i % per_own, 0)
        return (idx[0], blk0 + i, 0)

    def body(idx_ref, own_ref, *refs):
        del idx_ref
        acc = own_ref[...].astype(F32)
        for r in refs[:k]:
            acc = acc + r[...].astype(F32)
        refs[k][...] = acc.astype(out_dtype)

    grid_spec = pltpu.PrefetchScalarGridSpec(
        num_scalar_prefetch=1, grid=(R // rows,),
        in_specs=[pl.BlockSpec((None, rows, LANES), own_block)]
        + [pl.BlockSpec((None, rows, LANES), lambda i, idx, j=j: (j, i, 0)) for j in range(k)],
        out_specs=pl.BlockSpec((rows, LANES), lambda i, idx: (i, 0)))
    return pl.pallas_call(
        body, name=name, grid_spec=grid_spec,
        out_shape=jax.ShapeDtypeStruct((R, LANES), out_dtype), compiler_params=_params("parallel"),
    )(jnp.reshape(index, (1,)).astype(jnp.int32), own, *([others] * k))


def _adamw(w, g, m, v, name):
    shape = w.shape
    cols = shape[-1]
    rows_total = w.size // cols
    w2, g2, m2, v2 = (a.reshape(rows_total, cols) for a in (w, g, m, v))
    rows = rows_total
    if rows_total > ROWS:
        rows = next(r for r in range(ROWS, 0, -8) if rows_total % r == 0)

    def body(w_ref, g_ref, m_ref, v_ref, d_ref, nm_ref, nv_ref):
        gv = g_ref[...]
        nm = ADAM_B1 * m_ref[...] + (1.0 - ADAM_B1) * gv
        nv = ADAM_B2 * v_ref[...] + (1.0 - ADAM_B2) * (gv * gv)
        m_hat = nm / (1.0 - ADAM_B1 ** ADAM_STEP)
        v_hat = nv / (1.0 - ADAM_B2 ** ADAM_STEP)
        d_ref[...] = -ADAM_LR * (m_hat / (jnp.sqrt(v_hat) + ADAM_EPS) + ADAM_WD * w_ref[...])
        nm_ref[...] = nm
        nv_ref[...] = nv

    spec = pl.BlockSpec((rows, cols), lambda i: (i, 0))
    out = jax.ShapeDtypeStruct((rows_total, cols), F32)
    d, nm, nv = pl.pallas_call(
        body, name=name, grid=(rows_total // rows,), in_specs=[spec] * 4, out_specs=[spec] * 3,
        out_shape=[out, out, out], compiler_params=_params("parallel"),
    )(w2, g2, m2, v2)
    return d.reshape(shape), nm.reshape(shape), nv.reshape(shape)


BIG = ("w_in", "w_a_out", "w_pool", "w_attn_out", "w_o", "w_gu", "w_down")


LOCAL = dict(w_in="w_inT", w_a_out="w_a_out", w_pool="w_pool", w_attn_out="w_attn_out", w_o="w_o", w_gu="w_guT",
             w_down="w_down")


def _shard_rows(w, l):
    out = []
    for name in BIG:
        a = w[name][l]
        if name in ("w_in", "w_gu"):
            a = a.T
        elif name == "w_pool":
            a = a.reshape(-1, a.shape[-1])
        out.append(a.astype(BF16))
    return out


def _full_weights(names, gathered, w):
    out = {}
    for name, g in zip(names, gathered):
        if name == "w_pool":
            G, rg, cg = w[name].shape[1:]
            out[name] = jnp.transpose(g.reshape(N_DEV, G, rg, cg), (1, 0, 2, 3)).reshape(G, N_DEV * rg, cg)
        else:
            out[LOCAL[name]] = g.reshape(N_DEV * g.shape[1], g.shape[2])
    return out


def _split_grads(g, w, names=BIG, multiple=1):
    parts, spans, at = [], {}, 0
    for name in names:
        a = g[LOCAL[name]].astype(BF16)
        if name == "w_pool":
            G, rg, cg = w[name].shape[1:]
            a = jnp.transpose(a.reshape(G, N_DEV, rg, cg), (1, 0, 2, 3))
        a = a.reshape(N_DEV, -1, LANES)
        spans[name] = (at, at + a.shape[1])
        at += a.shape[1]
        parts.append(a)
    if at % multiple:
        parts.append(jnp.zeros((N_DEV, multiple - at % multiple, LANES), BF16))
    return jnp.concatenate(parts, axis=1), spans


def _own_grads(pieces, w):
    out = {}
    for name in BIG:
        per_layer = []
        for layer in pieces:
            packed, spans = next((p, s) for p, s in layer if name in s)
            per_layer.append(packed[spans[name][0]:spans[name][1]])
        a = jnp.stack(per_layer)
        if name in ("w_in", "w_gu"):
            sh = w[name].shape
            a = jnp.swapaxes(a.reshape(sh[0], sh[2], sh[1]), 1, 2)
        out[name] = a.reshape(w[name].shape)
    return out


SQUARE = ("w_a_out", "w_pool", "w_attn_out", "w_o")


class _Prefetch:
    def __init__(self, schedule, assign):
        self.schedule, self.assign = schedule, assign

    def job(self, key):
        if key not in self.assign:
            return None
        layer, names = self.assign[key]
        shards = dict(zip(BIG, _shard_rows(self.schedule.w, layer)))
        return _GatherJob([shards[name] for name in names])

    def done(self, key, outs):
        layer, names = self.assign[key]
        self.schedule.arrived(layer, names, outs)


class _Reduce:
    def __init__(self, split, spans, place, tag, swap_key="dw_down", carriers=("dw_gu", "dw_in", "dh2"),
                 chunk_rows=1024):
        self.split, self.spans, self.tag = split, spans, tag
        self.swap_key, self.carriers = swap_key, carriers
        self.core, self.chip = place[2], 2 * place[0] + place[1]
        self.rows = split.shape[1]
        self.chunks = [(r, min(chunk_rows, self.rows - r)) for r in range(0, self.rows, chunk_rows)]
        assert len(self.chunks) <= len(carriers)
        self.sums = [None] * len(self.chunks)

    def _swap_job(self):
        return _SwapJob(self.split.reshape(4, 2, self.rows, LANES))

    def _pair_sum(self, from_sibling):
        pair = _sum_parts(self.split, self.core, from_sibling.reshape(1, 4 * self.rows, LANES), BF16,
                          self.tag + "pair_sum", own_step=2)
        self.pair = pair.reshape(4, self.rows, LANES)

    def _chip_sum(self, n, from_chips):
        self.sums[n] = _sum_parts(self.pair, self.chip, from_chips, F32, f"{self.tag}chip_sum{n}",
                                  own_row0=self.chunks[n][0])

    def job(self, key):
        if key == self.swap_key:
            return self._swap_job()
        if key in self.carriers[:len(self.chunks)]:
            return _ExchangeJob(self.pair, *self.chunks[self.carriers.index(key)])
        return None

    def done(self, key, outs):
        if key == self.swap_key:
            self._pair_sum(outs[0])
        else:
            self._chip_sum(self.carriers.index(key), outs[0])

    def run(self):
        self._pair_sum(_run_job(self._swap_job(), self.tag + "reduce_pair")[0])
        self.chunks, self.sums = [(0, self.rows)], [None]
        self._chip_sum(0, _run_job(_ExchangeJob(self.pair, 0, self.rows), self.tag + "reduce_chips")[0])
        return self.result()

    def result(self):
        return (self.sums[0] if len(self.sums) == 1 else jnp.concatenate(self.sums, axis=0)), self.spans


class _Plans:
    def __init__(self, plans):
        self.plans = plans

    def job(self, key):
        self.owner = next((p for p in self.plans if p.job(key) is not None), None)
        return self.owner.job(key) if self.owner is not None else None

    def done(self, key, outs):
        self.owner.done(key, outs)


class _Schedule:
    EARLY = ("w_gu", "w_down") + SQUARE
    EARLY_PAD = 512

    def __init__(self, w, place):
        self.w, self.place = w, place
        self.depth = w["w_in"].shape[0]
        self.full = [{} for _ in range(self.depth)]
        self.reduce = {}
        self.pieces = [None] * self.depth
        self.active = []

    def arrived(self, layer, names, gathered):
        self.full[layer].update(_full_weights(names, gathered, self.w))

    def plan_fwd(self, l):
        nxt = l + 1
        if l == 0:
            assign = dict(proj=(0, tuple(n for n in BIG if n != "w_in")))
            if nxt < self.depth:
                assign.update(attn=(nxt, ("w_in",) + SQUARE), gu=(nxt, ("w_gu",)), x2=(nxt, ("w_down",)))
        elif nxt < self.depth:
            assign = dict(proj=(nxt, ("w_in",)), attn=(nxt, ("w_down",) + SQUARE), gu=(nxt, ("w_gu",)))
        else:
            return None
        return _Prefetch(self, assign)

    def weights(self, l):
        return self.full[l]

    def plan_bwd(self, l):
        self.active = [self.reduce[l + 1]] if l + 1 in self.reduce else []
        return _Plans(self.active)

    def early_grads(self, l, g):
        if l == 0:
            split, spans = _split_grads(g, self.w, self.EARLY, self.EARLY_PAD)
            self.early = _Reduce(split, spans, self.place, "l0_early_", swap_key="datt", carriers=("attn_b",),
                                 chunk_rows=split.shape[1])
            self.active.append(self.early)

    def grads(self, l, g):
        if l + 1 in self.reduce:
            self.pieces[l + 1] = [self.reduce[l + 1].result()]
        if l == 0:
            last = _Reduce(*_split_grads(g, self.w, ("w_in",)), self.place, "l0_")
            self.pieces[0] = [self.early.result(), last.run()]
        else:
            self.reduce[l] = _Reduce(*_split_grads(g, self.w), self.place, f"l{l}_")


SMALL_ROWS = 32


def _pack_small(gs, L, D):
    rows = [gs["pool_scale"].reshape(L, D), gs["g_mix"].reshape(L, D), gs["g_ffn"].reshape(L, D),
            gs["g_final"].reshape(1, D), gs["conv_w"][:, :3].reshape(3 * L, D),
            jnp.pad(gs["attn_sink"].reshape(1, -1), ((0, 0), (0, D - L * N_HEADS))),
            jnp.pad(gs["rel_bias"].reshape(1, -1), ((0, 0), (0, D - N_BUCKETS * N_HEADS)))]
    a = jnp.concatenate(rows, axis=0)
    return jnp.pad(a, ((0, SMALL_ROWS - a.shape[0]), (0, 0)))


def _unpack_small(a, L, D):
    g = {}
    g["pool_scale"] = a[0:L]
    g["g_mix"] = a[L:2 * L]
    g["g_ffn"] = a[2 * L:3 * L]
    g["g_final"] = a[3 * L]
    g["conv_w"] = a[3 * L + 1:6 * L + 1].reshape(L, 3, 1, D)
    g["attn_sink"] = a[6 * L + 1, :L * N_HEADS].reshape(L, N_HEADS)
    g["rel_bias"] = a[6 * L + 2, :N_BUCKETS * N_HEADS].reshape(N_BUCKETS, N_HEADS)
    return g


WEIGHTS = ("w_in", "conv_w", "w_a_out", "w_pool", "pool_scale", "w_attn_out", "attn_sink", "w_o", "g_mix", "g_ffn",
           "w_gu", "w_down", "rel_bias", "g_final")


def kernel(x, w_in, conv_w, w_a_out, w_pool, pool_scale, w_attn_out, attn_sink, w_o, g_mix, g_ffn, w_gu, w_down, rel_bias, g_final, loss_target, m_w_in, m_conv_w, m_w_a_out, m_w_pool, m_pool_scale, m_w_attn_out, m_attn_sink, m_w_o, m_g_mix, m_g_ffn, m_w_gu, m_w_down, m_rel_bias, m_g_final, v_w_in, v_conv_w, v_w_a_out, v_w_pool, v_pool_scale, v_w_attn_out, v_attn_sink, v_w_o, v_g_mix, v_g_ffn, v_w_gu, v_w_down, v_rel_bias, v_g_final):
    w = dict(w_in=w_in, conv_w=conv_w, w_a_out=w_a_out, w_pool=w_pool, pool_scale=pool_scale, w_attn_out=w_attn_out,
             attn_sink=attn_sink, w_o=w_o, g_mix=g_mix, g_ffn=g_ffn, w_gu=w_gu, w_down=w_down, rel_bias=rel_bias,
             g_final=g_final)
    m = dict(w_in=m_w_in, conv_w=m_conv_w, w_a_out=m_w_a_out, w_pool=m_w_pool, pool_scale=m_pool_scale,
             w_attn_out=m_w_attn_out, attn_sink=m_attn_sink, w_o=m_w_o, g_mix=m_g_mix, g_ffn=m_g_ffn, w_gu=m_w_gu,
             w_down=m_w_down, rel_bias=m_rel_bias, g_final=m_g_final)
    v = dict(w_in=v_w_in, conv_w=v_conv_w, w_a_out=v_w_a_out, w_pool=v_w_pool, pool_scale=v_pool_scale,
             w_attn_out=v_w_attn_out, attn_sink=v_attn_sink, w_o=v_w_o, g_mix=v_g_mix, g_ffn=v_g_ffn, w_gu=v_w_gu,
             w_down=v_w_down, rel_bias=v_rel_bias, g_final=v_g_final)
    T, D = x.shape[1], x.shape[2]
    L = w_in.shape[0]
    cx, cy, cc = _place()

    schedule = _Schedule(w, (cx, cy, cc))
    schedule.arrived(0, ("w_in",), _run_job(_GatherJob(_shard_rows(w, 0)[:1]), "gather_w_in_l0"))
    cw = jnp.pad(conv_w.reshape(L * 3, -1), ((0, 16 - L * 3), (0, 0)))
    cw = _all_gather(cw, "gather_conv_w")
    cw = jnp.transpose(cw, (1, 0, 2)).reshape(16, -1)[:L * 3].reshape(L, 3, -1)
    small = dict(conv_w=jnp.pad(cw, ((0, 0), (0, 5), (0, 0))), pool_scale=pool_scale.reshape(L, 1, D),
                 g_mix=g_mix.reshape(L, 1, D), g_ffn=g_ffn.reshape(L, 1, D), attn_sink=attn_sink,
                 rel_bias=rel_bias, g_final=g_final.reshape(1, D))

    loss, dx, _, gs = _local_step(x[0], loss_target[0], [schedule.full[0]], small, schedule)
    loss = lax.psum(loss[0, 0], ("x", "y", "c"))
    grads = _own_grads(schedule.pieces, w)

    small_all = _all_gather(_pack_small(gs, L, D), "gather_small")
    small_sum = _sum_parts(small_all, jnp.int32(0), small_all[1:], F32, "small_sum")
    gsm = _unpack_small(small_sum, L, D)
    W8 = D // N_DEV
    dev = 4 * cx + 2 * cy + cc
    gsm["conv_w"] = lax.dynamic_slice_in_dim(gsm["conv_w"], dev * W8, W8, axis=3)
    grads.update(gsm)

    deltas, new_m, new_v = {}, {}, {}
    for name in WEIGHTS:
        deltas[name], new_m[name], new_v[name] = _adamw(w[name], grads[name], m[name], v[name], "adamw_" + name)

    return (loss, dx[None], *[grads[n] for n in WEIGHTS], *[deltas[n] for n in WEIGHTS],
            *[new_m[n] for n in WEIGHTS], *[new_v[n] for n in WEIGHTS])
```

```python
import functools
import math

import jax
import jax.numpy as jnp
from jax import lax
from jax.experimental import pallas as pl
from jax.experimental.pallas import tpu as pltpu

F32 = jnp.float32
BF16 = jnp.bfloat16
MESH = pl.DeviceIdType.MESH

N_DEV = 8
N_HEADS = 16
N_KV_HEADS = 4
HEAD_DIM = 64
GROUP = N_HEADS // N_KV_HEADS
BLOCK = 128
WINDOW = 128
N_BUCKETS = 32
MAX_DISTANCE = 128
POOL_WINDOWS = (2, 4, 8, 16)
POOL_GROUPS = 4
HALO = 8
EPS = 1e-6
NEG_INF = -1e30

ADAM_LR = 0.001
ADAM_B1 = 0.9
ADAM_B2 = 0.999
ADAM_EPS = 1e-08
ADAM_WD = 0.01
ADAM_STEP = 10

LANES = 1024
VMEM_LIMIT_BYTES = 48 * 1024 * 1024


def _params(*sem):
    return pltpu.CompilerParams(dimension_semantics=sem, vmem_limit_bytes=VMEM_LIMIT_BYTES)


def _tile(n, cap):
    if n <= cap:
        return n
    for t in range(cap - cap % 128, 0, -128):
        if n % t == 0:
            return t
    raise ValueError(f"no tile for {n}")


WIDE_TILE = 2176

_DIMS = {"nn": (((1,), (0,)), ((), ())), "nt": (((1,), (1,)), ((), ())), "tn": (((0,), (0,)), ((), ()))}


HBM_SPEC = pl.BlockSpec(memory_space=pltpu.HBM)
ANY_SPEC = pl.BlockSpec(memory_space=pl.ANY)


def _matmul(a, b, mode, out_dtype, name, res=None, tm_cap=1024, tn_cap=1024, tk_cap=1024, comm=None):
    if mode == "tn":
        K, M = a.shape
    else:
        M, K = a.shape
    N = b.shape[0] if mode == "nt" else b.shape[1]
    tm, tn, tk = _tile(M, tm_cap), _tile(N, tn_cap), _tile(K, tk_cap)
    nk = K // tk
    a_spec = pl.BlockSpec((tk, tm), lambda i, j, k: (k, i)) if mode == "tn" else pl.BlockSpec((tm, tk), lambda i, j, k: (i, k))
    b_spec = pl.BlockSpec((tn, tk), lambda i, j, k: (j, k)) if mode == "nt" else pl.BlockSpec((tk, tn), lambda i, j, k: (k, j))
    o_spec = pl.BlockSpec((tm, tn), lambda i, j, k: (i, j))
    dims = _DIMS[mode]
    has_res = res is not None
    gm, gn = M // tm, N // tn
    steps = gm * gn * nk
    n_in = 2 + has_res
    n_ci = len(comm.ins) if comm is not None else 0
    n_co = len(comm.outs) if comm is not None else 0

    def body(*refs):
        a_ref, b_ref = refs[0], refs[1]
        res_ref = refs[2] if has_res else None
        comm_in = refs[n_in:n_in + n_ci]
        o_ref = refs[n_in + n_ci]
        comm_out = refs[n_in + n_ci + 1:n_in + n_ci + 1 + n_co]
        acc_ref = refs[n_in + n_ci + 1 + n_co]
        sems = refs[n_in + n_ci + 2 + n_co:]
        k = pl.program_id(2)
        step = (pl.program_id(0) * gn + pl.program_id(1)) * nk + k
        if comm is not None:
            @pl.when(step == 0)
            def _():
                comm.start(comm_in, comm_out, sems)

        part = lax.dot_general(a_ref[...], b_ref[...], dims, preferred_element_type=F32)

        @pl.when(k == 0)
        def _():
            acc_ref[...] = part

        @pl.when(k > 0)
        def _():
            acc_ref[...] += part

        @pl.when(k == nk - 1)
        def _():
            out = acc_ref[...]
            if has_res:
                out = out + res_ref[...]
            o_ref[...] = out.astype(out_dtype)

        if comm is not None:
            @pl.when(step == (3 * steps) // 4)
            def _():
                comm.mid(comm_in, comm_out, sems)

            @pl.when(step == steps - 1)
            def _():
                comm.finish(comm_in, comm_out, sems)

    in_specs = [a_spec, b_spec] + ([o_spec] if has_res else [])
    args = (a, b) + ((res,) if has_res else ())
    out_shape = jax.ShapeDtypeStruct((M, N), out_dtype)
    if comm is None:
        return pl.pallas_call(
            body, name=name, grid=(gm, gn, nk), in_specs=in_specs, out_specs=o_spec, out_shape=out_shape,
            scratch_shapes=[pltpu.VMEM((tm, tn), F32)],
            compiler_params=_params("parallel", "parallel", "arbitrary"),
        )(*args)
    outs = pl.pallas_call(
        body, name=name, grid=(gm, gn, nk),
        in_specs=in_specs + [HBM_SPEC] * n_ci, out_specs=[o_spec] + [HBM_SPEC] * n_co,
        out_shape=[out_shape] + list(comm.outs),
        scratch_shapes=[pltpu.VMEM((tm, tn), F32)] + list(comm.sems),
        compiler_params=_params("arbitrary", "arbitrary", "arbitrary"),
    )(*args, *comm.ins)
    return outs[0], outs[1:]


def _pool_mm(a, w, mode, out_dtype, name):
    T = a.shape[0]
    G = POOL_GROUPS
    cg = a.shape[1] // G
    tm = _tile(T, 1024)
    nt = T // tm
    dims = _DIMS[mode]
    if mode == "tn":
        def body(a_ref, d_ref, o_ref):
            part = lax.dot_general(a_ref[...], d_ref[...], dims, preferred_element_type=F32)

            @pl.when(pl.program_id(1) == 0)
            def _():
                o_ref[...] = part

            @pl.when(pl.program_id(1) > 0)
            def _():
                o_ref[...] += part

        return pl.pallas_call(
            body, name=name, grid=(G, nt),
            in_specs=[pl.BlockSpec((tm, cg), lambda g, i: (i, g)), pl.BlockSpec((tm, cg), lambda g, i: (i, g))],
            out_specs=pl.BlockSpec((None, cg, cg), lambda g, i: (g, 0, 0)),
            out_shape=jax.ShapeDtypeStruct((G, cg, cg), F32),
            compiler_params=_params("parallel", "arbitrary"),
        )(a, w)

    def body(a_ref, w_ref, o_ref):
        o_ref[...] = lax.dot_general(a_ref[...], w_ref[...], dims, preferred_element_type=F32).astype(out_dtype)

    return pl.pallas_call(
        body, name=name, grid=(G, nt),
        in_specs=[pl.BlockSpec((tm, cg), lambda g, i: (i, g)), pl.BlockSpec((None, cg, cg), lambda g, i: (g, 0, 0))],
        out_specs=pl.BlockSpec((tm, cg), lambda g, i: (i, g)),
        out_shape=jax.ShapeDtypeStruct((T, G * cg), out_dtype),
        compiler_params=_params("parallel", "parallel"),
    )(a, w)


ROWS = 256
HALO_BLOCK = 16


def _row_spec(d, col=0, rows=ROWS):
    return pl.BlockSpec((rows, d), lambda i, col=col: (i, col))


def _const_spec(shape):
    return pl.BlockSpec(shape, lambda *_: (0,) * len(shape))


def _rms_fwd(x, g, name):
    T, D = x.shape

    def body(x_ref, g_ref, h_ref):
        xv = x_ref[...]
        r = lax.rsqrt(jnp.mean(xv * xv, axis=-1, keepdims=True) + EPS)
        h_ref[...] = (xv * r * g_ref[...]).astype(BF16)

    return pl.pallas_call(
        body, name=name, grid=(T // ROWS,),
        in_specs=[_row_spec(D), _const_spec((1, D))], out_specs=_row_spec(D),
        out_shape=jax.ShapeDtypeStruct((T, D), BF16), compiler_params=_params("parallel"),
    )(x, g)


def _accumulate(ref, part):
    first = pl.program_id(0) == 0

    @pl.when(first)
    def _():
        ref[...] = part

    @pl.when(jnp.logical_not(first))
    def _():
        ref[...] += part


def _rms_bwd(x, g, dh, dres, name):
    T, D = x.shape

    def body(x_ref, g_ref, dh_ref, dres_ref, dx_ref, dxb_ref, dg_ref):
        xv = x_ref[...]
        r = lax.rsqrt(jnp.mean(xv * xv, axis=-1, keepdims=True) + EPS)
        xhat = xv * r
        dh_v = dh_ref[...]
        dxhat = dh_v * g_ref[...]
        dx = dres_ref[...] + r * (dxhat - xhat * jnp.mean(dxhat * xhat, axis=-1, keepdims=True))
        dx_ref[...] = dx
        dxb_ref[...] = dx.astype(BF16)
        _accumulate(dg_ref, jnp.sum(dh_v * xhat, axis=0, keepdims=True))

    return pl.pallas_call(
        body, name=name, grid=(T // ROWS,),
        in_specs=[_row_spec(D), _const_spec((1, D)), _row_spec(D), _row_spec(D)],
        out_specs=[_row_spec(D), _row_spec(D), _const_spec((1, D))],
        out_shape=[jax.ShapeDtypeStruct((T, D), F32), jax.ShapeDtypeStruct((T, D), BF16),
                   jax.ShapeDtypeStruct((1, D), F32)],
        compiler_params=_params("arbitrary"),
    )(x, g, dh, dres)


def _loss_head(x, g, target, name):
    T, D = x.shape

    def body(x_ref, g_ref, t_ref, loss_ref, dx_ref, dxb_ref, dg_ref):
        xv = x_ref[...]
        gv = g_ref[...]
        r = lax.rsqrt(jnp.mean(xv * xv, axis=-1, keepdims=True) + EPS)
        xhat = xv * r
        err = xhat * gv - t_ref[...]
        loss = 0.5 * jnp.sum(jnp.mean(err * err, axis=-1, keepdims=True), axis=0, keepdims=True)
        dy = err * (1.0 / D)
        dxhat = dy * gv
        dx = r * (dxhat - xhat * jnp.mean(dxhat * xhat, axis=-1, keepdims=True))
        dx_ref[...] = dx
        dxb_ref[...] = dx.astype(BF16)
        _accumulate(loss_ref, loss)
        _accumulate(dg_ref, jnp.sum(dy * xhat, axis=0, keepdims=True))

    return pl.pallas_call(
        body, name=name, grid=(T // ROWS,),
        in_specs=[_row_spec(D), _const_spec((1, D)), _row_spec(D)],
        out_specs=[_const_spec((1, 1)), _row_spec(D), _row_spec(D), _const_spec((1, D))],
        out_shape=[jax.ShapeDtypeStruct((1, 1), F32), jax.ShapeDtypeStruct((T, D), F32),
                   jax.ShapeDtypeStruct((T, D), BF16), jax.ShapeDtypeStruct((1, D), F32)],
        compiler_params=_params("arbitrary"),
    )(x, g, target)


def _halo_specs(d, col, n_blocks):
    per = ROWS // HALO_BLOCK
    last = n_blocks * per - 1
    prev = pl.BlockSpec((HALO_BLOCK, d), lambda i, col=col: (jnp.maximum(i * per - 1, 0), col))
    nxt = pl.BlockSpec((HALO_BLOCK, d), lambda i, col=col: (jnp.minimum((i + 1) * per, last), col))
    return prev, nxt


def _with_halo(prev, cur, nxt, n_blocks):
    i = pl.program_id(0)
    prev = jnp.where(i > 0, prev[HALO_BLOCK - HALO:], 0.0)
    nxt = jnp.where(i < n_blocks - 1, nxt[:HALO], 0.0)
    return jnp.concatenate([prev, cur, nxt], axis=0)


def _f32(ref):
    return ref[...].astype(F32)


def _shift(ext, k):
    n = ext.shape[0]
    v = ext if k == 0 else pltpu.roll(ext, (-k) % n, 0)
    return v[HALO:HALO + ROWS]


def _shift_full(ext, k):
    n = ext.shape[0]
    return pltpu.roll(ext, (-k) % n, 0)


def _pool_counts(T):
    n = ROWS + 2 * HALO
    t = pl.program_id(0) * ROWS - HALO + lax.broadcasted_iota(jnp.int32, (n, 1), 0)
    out = []
    for w in POOL_WINDOWS:
        lo = jnp.maximum(t - w // 2, 0)
        hi = jnp.minimum(t + (w - 1 - w // 2), T - 1)
        out.append(jnp.maximum(hi - lo + 1, 1).astype(F32))
    return out


def _window_sums(e, sign):
    s2 = e + _shift_full(e, -sign)
    s4 = _shift_full(s2, -1) + _shift_full(s2, 1)
    s8 = _shift_full(s4, -2) + _shift_full(s4, 2)
    s16 = _shift_full(s8, -4) + _shift_full(s8, 4)
    return s2, s4, s8, s16


def _mixer_fwd(proj, conv_w, name):
    T = proj.shape[0]
    W = conv_w.shape[1]
    nb = T // ROWS
    cg = W // POOL_GROUPS

    def body(b_ref, c_ref, x_ref, u_ref, cp_ref, cn_ref, xp_ref, xn_ref, up_ref, un_ref, w_ref, z_ref, p_ref):
        uc = _with_halo(_f32(cp_ref) * _f32(xp_ref), _f32(c_ref) * _f32(x_ref), _f32(cn_ref) * _f32(xn_ref), nb)
        w0, w1, w2 = w_ref[0:1, :], w_ref[1:2, :], w_ref[2:3, :]
        y = w0 * _shift(uc, -1) + w1 * _shift(uc, 0) + w2 * _shift(uc, 1)
        z_ref[...] = (_f32(b_ref) * y).astype(BF16)
        e = _with_halo(_f32(up_ref), _f32(u_ref), _f32(un_ref), nb)
        counts = _pool_counts(T)
        for gi in range(POOL_GROUPS):
            eg = e[:, gi * cg:(gi + 1) * cg]
            s = _window_sums(eg, 1)[gi]
            p = s[HALO:HALO + ROWS] / counts[gi][HALO:HALO + ROWS] - eg[HALO:HALO + ROWS]
            p_ref[:, gi * cg:(gi + 1) * cg] = p.astype(BF16)

    halo = [s for col in (1, 2, 3) for s in _halo_specs(W, col, nb)]
    return pl.pallas_call(
        body, name=name, grid=(nb,),
        in_specs=[_row_spec(W, 0), _row_spec(W, 1), _row_spec(W, 2), _row_spec(W, 3)] + halo + [_const_spec((8, W))],
        out_specs=[_row_spec(W), _row_spec(W)],
        out_shape=[jax.ShapeDtypeStruct((T, W), BF16), jax.ShapeDtypeStruct((T, W), BF16)],
        compiler_params=_params("parallel"),
    )(proj, proj, proj, proj, proj, proj, proj, proj, proj, proj, conv_w)


def _mixer_bwd(proj, conv_w, dz, dp, dproj, name):
    T = proj.shape[0]
    W = conv_w.shape[1]
    nb = T // ROWS
    cg = W // POOL_GROUPS

    def body(b_ref, c_ref, x_ref, dz_ref, dp_ref,
             bp_ref, bn_ref, cp_ref, cn_ref, xp_ref, xn_ref, dzp_ref, dzn_ref, dpp_ref, dpn_ref, w_ref, _,
             o_ref, dw_ref):
        cv, xv, dzv = _f32(c_ref), _f32(x_ref), _f32(dz_ref)
        uc = _with_halo(_f32(cp_ref) * _f32(xp_ref), cv * xv, _f32(cn_ref) * _f32(xn_ref), nb)
        dy = _with_halo(_f32(dzp_ref) * _f32(bp_ref), dzv * _f32(b_ref), _f32(dzn_ref) * _f32(bn_ref), nb)
        w0, w1, w2 = w_ref[0:1, :], w_ref[1:2, :], w_ref[2:3, :]
        um, u0, up = _shift(uc, -1), _shift(uc, 0), _shift(uc, 1)
        o_ref[:, 0:W] = (dzv * (w0 * um + w1 * u0 + w2 * up)).astype(BF16)
        dy0 = _shift(dy, 0)
        duc = w0 * _shift(dy, 1) + w1 * dy0 + w2 * _shift(dy, -1)
        o_ref[:, W:2 * W] = (duc * xv).astype(BF16)
        o_ref[:, 2 * W:3 * W] = (duc * cv).astype(BF16)
        row = lax.broadcasted_iota(jnp.int32, (8, W), 0)
        dw = jnp.where(row == 0, jnp.sum(dy0 * um, axis=0, keepdims=True),
                       jnp.where(row == 1, jnp.sum(dy0 * u0, axis=0, keepdims=True),
                                 jnp.where(row == 2, jnp.sum(dy0 * up, axis=0, keepdims=True), 0.0)))
        _accumulate(dw_ref, dw)
        d = _with_halo(_f32(dpp_ref), _f32(dp_ref), _f32(dpn_ref), nb)
        counts = _pool_counts(T)
        for gi in range(POOL_GROUPS):
            dg = d[:, gi * cg:(gi + 1) * cg]
            s = _window_sums(dg / counts[gi], -1)[gi]
            o_ref[:, 3 * W + gi * cg:3 * W + (gi + 1) * cg] = (s[HALO:HALO + ROWS] - dg[HALO:HALO + ROWS]).astype(BF16)

    def halo(col):
        return list(_halo_specs(W, col, nb))

    return pl.pallas_call(
        body, name=name, grid=(nb,),
        in_specs=[_row_spec(W, 0), _row_spec(W, 1), _row_spec(W, 2), _row_spec(W), _row_spec(W)]
        + halo(0) + halo(1) + halo(2) + halo(0) + halo(0) + [_const_spec((8, W)), ANY_SPEC],
        out_specs=[_row_spec(4 * W), _const_spec((8, W))],
        out_shape=[jax.ShapeDtypeStruct(dproj.shape, BF16), jax.ShapeDtypeStruct((8, W), F32)],
        input_output_aliases={16: 0}, compiler_params=_params("arbitrary"),
    )(proj, proj, proj, dz, dp, proj, proj, proj, proj, proj, proj, dz, dz, dp, dp, conv_w, dproj)


def _t5_bucket(rel):
    half = N_BUCKETS // 2
    max_exact = half // 2
    ret = jnp.where(rel > 0, half, 0)
    n = jnp.abs(rel)
    nf = jnp.maximum(n, 1).astype(jnp.float32)
    large = max_exact + (jnp.log(nf / max_exact) / math.log(MAX_DISTANCE / max_exact)
                         * (half - max_exact)).astype(jnp.int32)
    large = jnp.minimum(large, half - 1)
    return ret + jnp.where(n < max_exact, n, large)


def _bucket_table():
    qi = jnp.arange(BLOCK)[:, None]
    kj = jnp.arange(3 * BLOCK)[None, :]
    rel = kj - BLOCK - qi
    return jnp.where(jnp.abs(rel) <= WINDOW, _t5_bucket(rel), -1).astype(jnp.int32)


def _bias_table(rel_bias, bucket, name):
    def body(rb_ref, bucket_ref, o_ref):
        h = pl.program_id(0)
        bk = bucket_ref[...]
        acc = jnp.full(bk.shape, NEG_INF, F32)
        for b in range(N_BUCKETS):
            acc = jnp.where(bk == b, rb_ref[b, h], acc)
        o_ref[...] = acc

    return pl.pallas_call(
        body, name=name, grid=(N_HEADS,),
        in_specs=[pl.BlockSpec(memory_space=pltpu.SMEM), _const_spec((BLOCK, 3 * BLOCK))],
        out_specs=pl.BlockSpec((None, BLOCK, 3 * BLOCK), lambda h: (h, 0, 0)),
        out_shape=jax.ShapeDtypeStruct((N_HEADS, BLOCK, 3 * BLOCK), F32),
        compiler_params=_params("parallel"),
    )(rel_bias, bucket)


def _bias_grad(ds_sum, bucket, name):
    def body(ds_ref, bucket_ref, o_ref):
        bk = bucket_ref[...]
        ds = ds_ref[...]
        row = lax.broadcasted_iota(jnp.int32, (N_BUCKETS, 128), 0)
        acc = jnp.zeros((N_BUCKETS, 128), F32)
        for b in range(N_BUCKETS):
            s = jnp.sum(jnp.sum(jnp.where(bk == b, ds, 0.0), axis=1, keepdims=True), axis=0, keepdims=True)
            acc = jnp.where(row == b, s, acc)
        o_ref[...] = acc

    return pl.pallas_call(
        body, name=name, grid=(N_HEADS,),
        in_specs=[pl.BlockSpec((None, BLOCK, 3 * BLOCK), lambda h: (h, 0, 0)), _const_spec((BLOCK, 3 * BLOCK))],
        out_specs=pl.BlockSpec((None, N_BUCKETS, 128), lambda h: (h, 0, 0)),
        out_shape=jax.ShapeDtypeStruct((N_HEADS, N_BUCKETS, 128), F32),
        compiler_params=_params("parallel"),
    )(ds_sum, bucket)


PAIR = 2 * HEAD_DIM
Q_BLOCKS = 2


def _low_half(shape):
    return lax.broadcasted_iota(jnp.int32, shape, len(shape) - 1) % PAIR < HEAD_DIM


def _split_pair(a):
    low = _low_half(a.shape)
    zero = jnp.zeros_like(a)
    return jnp.concatenate([jnp.where(low, a, zero), jnp.where(low, zero, a)], axis=0)


def _kv_expand(proj, kv_off, name):
    T = proj.shape[0]
    kv_w = N_KV_HEADS * HEAD_DIM
    rows = _tile(T, 512)

    def body(k_ref, v_ref, ke_ref, ve_ref):
        for src, dst in ((k_ref, ke_ref), (v_ref, ve_ref)):
            for g in range(N_KV_HEADS // 2):
                x = src[:, g * PAIR:(g + 1) * PAIR].astype(F32)
                swapped = pltpu.roll(x, HEAD_DIM, 1)
                low = _low_half(x.shape)
                dst[:, 2 * g * PAIR:(2 * g + 1) * PAIR] = jnp.where(low, x, swapped).astype(BF16)
                dst[:, (2 * g + 1) * PAIR:(2 * g + 2) * PAIR] = jnp.where(low, swapped, x).astype(BF16)

    out = jax.ShapeDtypeStruct((T, N_KV_HEADS * PAIR), BF16)
    ospec = pl.BlockSpec((rows, N_KV_HEADS * PAIR), lambda i: (i, 0))
    return pl.pallas_call(
        body, name=name, grid=(T // rows,),
        in_specs=[pl.BlockSpec((rows, kv_w), lambda i: (i, kv_off // kv_w)),
                  pl.BlockSpec((rows, kv_w), lambda i: (i, kv_off // kv_w + 1))],
        out_specs=[ospec, ospec], out_shape=[out, out], compiler_params=_params("parallel"),
    )(proj, proj)


def _kv_fold(dke, dve, dproj, kv_off, name):
    T = dke.shape[0]
    kv_w = N_KV_HEADS * HEAD_DIM
    rows = _tile(T, 512)

    def body(dk_ref, dv_ref, _, o_ref):
        for n, src in enumerate((dk_ref, dv_ref)):
            for g in range(N_KV_HEADS // 2):
                a = src[:, 2 * g * PAIR:(2 * g + 1) * PAIR]
                b = src[:, (2 * g + 1) * PAIR:(2 * g + 2) * PAIR]
                a = a + pltpu.roll(a, HEAD_DIM, 1)
                b = b + pltpu.roll(b, HEAD_DIM, 1)
                o_ref[:, n * kv_w + g * PAIR:n * kv_w + (g + 1) * PAIR] = jnp.where(_low_half(a.shape), a, b).astype(BF16)

    ispec = pl.BlockSpec((rows, N_KV_HEADS * PAIR), lambda i: (i, 0))
    return pl.pallas_call(
        body, name=name, grid=(T // rows,), in_specs=[ispec, ispec, ANY_SPEC],
        out_specs=pl.BlockSpec((rows, 2 * kv_w), lambda i: (i, kv_off // (2 * kv_w))),
        out_shape=jax.ShapeDtypeStruct(dproj.shape, BF16), input_output_aliases={2: 0},
        compiler_params=_params("parallel"),
    )(dke, dve, dproj)


def _key_blocks(i, nb):
    return [pl.multiple_of(n * BLOCK, BLOCK) for n in (jnp.maximum(i - 1, 0), i, jnp.minimum(i + 1, nb - 1))]


def _three_blocks(ref, starts):
    return jnp.concatenate([ref[pl.ds(s, BLOCK), :] for s in starts], axis=0)


def _pair_scores(q2, kd, bias_ref, pr, i, nb):
    qq = _split_pair(q2)
    s = lax.dot_general(qq, kd, _DIMS["nt"], preferred_element_type=F32) * (HEAD_DIM ** -0.5)
    s = s + bias_ref[2 * pr:2 * pr + 2].reshape(2 * BLOCK, 3 * BLOCK)
    kj = lax.broadcasted_iota(jnp.int32, (1, 3 * BLOCK), 1)
    outside = jnp.logical_or(jnp.logical_and(i == 0, kj < BLOCK), jnp.logical_and(i == nb - 1, kj >= 2 * BLOCK))
    return qq, jnp.where(outside, NEG_INF, s)


def _attn_specs(T, q_off):
    gw = GROUP * HEAD_DIM
    return dict(
        sink=pl.BlockSpec(memory_space=pltpu.SMEM),
        q=pl.BlockSpec((Q_BLOCKS * BLOCK, gw), lambda j, i: (i, q_off // gw + j)),
        kv=pl.BlockSpec((T, PAIR), lambda j, i: (0, j)),
        bias=pl.BlockSpec((GROUP, BLOCK, 3 * BLOCK), lambda j, i: (j, 0, 0)),
        o=pl.BlockSpec((Q_BLOCKS * BLOCK, gw), lambda j, i: (i, j)))


def _attn_fwd(proj, q_off, kexp, vexp, bias, sink, name, comm=None):
    T = proj.shape[0]
    nb = T // BLOCK
    sp = _attn_specs(T, q_off)
    steps = N_KV_HEADS * (nb // Q_BLOCKS)
    n_ci = len(comm.ins) if comm is not None else 0
    n_co = len(comm.outs) if comm is not None else 0

    def body(*refs):
        sink_ref, q_ref, ke_ref, ve_ref, bias_ref = refs[:5]
        comm_in = refs[5:5 + n_ci]
        o_ref, lse_ref = refs[5 + n_ci:7 + n_ci]
        comm_out = refs[7 + n_ci:7 + n_ci + n_co]
        sems = refs[7 + n_ci + n_co:]
        j, i = pl.program_id(0), pl.program_id(1)
        step = j * (nb // Q_BLOCKS) + i
        if comm is not None:
            @pl.when(step == 0)
            def _():
                comm.start(comm_in, comm_out, sems)

        first_rows = lax.broadcasted_iota(jnp.int32, (2 * BLOCK, 1), 0) < BLOCK
        low = _low_half((BLOCK, PAIR))
        for b in range(Q_BLOCKS):
            blk = i * Q_BLOCKS + b
            rows = slice(b * BLOCK, (b + 1) * BLOCK)
            starts = _key_blocks(blk, nb)
            kd = _three_blocks(ke_ref, starts)
            vv = _split_pair(_three_blocks(ve_ref, starts))
            for pr in range(GROUP // 2):
                lanes = slice(pr * PAIR, (pr + 1) * PAIR)
                _, s = _pair_scores(q_ref[rows, lanes], kd, bias_ref, pr, blk, nb)
                head = GROUP * j + 2 * pr
                sk = jnp.where(first_rows, sink_ref[head], sink_ref[head + 1])
                m = jnp.maximum(jnp.max(s, axis=-1, keepdims=True), sk)
                p = jnp.exp(s - m)
                denom = jnp.sum(p, axis=-1, keepdims=True) + jnp.exp(sk - m)
                p = (p / denom).astype(BF16)
                pp = jnp.concatenate([p[:BLOCK], p[BLOCK:]], axis=1)
                o_ref[rows, lanes] = lax.dot_general(pp, vv, _DIMS["nn"], preferred_element_type=F32).astype(BF16)
                lse = m + jnp.log(denom)
                lse_ref[rows, lanes] = jnp.where(low, lse[:BLOCK], lse[BLOCK:])

        if comm is not None:
            @pl.when(step == (3 * steps) // 4)
            def _():
                comm.mid(comm_in, comm_out, sems)

            @pl.when(step == steps - 1)
            def _():
                comm.finish(comm_in, comm_out, sems)

    out_shape = [jax.ShapeDtypeStruct((T, N_HEADS * HEAD_DIM), BF16), jax.ShapeDtypeStruct((T, N_HEADS * HEAD_DIM), F32)]
    in_specs = [sp["sink"], sp["q"], sp["kv"], sp["kv"], sp["bias"]]
    if comm is None:
        att, lse = pl.pallas_call(
            body, name=name, grid=(N_KV_HEADS, nb // Q_BLOCKS), in_specs=in_specs, out_specs=[sp["o"], sp["o"]],
            out_shape=out_shape, compiler_params=_params("parallel", "parallel"),
        )(sink, proj, kexp, vexp, bias)
        return att, lse, []
    outs = pl.pallas_call(
        body, name=name, grid=(N_KV_HEADS, nb // Q_BLOCKS),
        in_specs=in_specs + [HBM_SPEC] * n_ci, out_specs=[sp["o"], sp["o"]] + [HBM_SPEC] * n_co,
        out_shape=out_shape + list(comm.outs), scratch_shapes=list(comm.sems),
        compiler_params=_params("arbitrary", "arbitrary"),
    )(sink, proj, kexp, vexp, bias, *comm.ins)
    return outs[0], outs[1], outs[2:]


def _attn_bwd(proj, q_off, kexp, vexp, bias, sink, out, lse, dout, dproj, name, comm=None):
    T = proj.shape[0]
    nb = T // BLOCK
    sp = _attn_specs(T, q_off)
    scale = HEAD_DIM ** -0.5
    steps = N_KV_HEADS * (nb // Q_BLOCKS)
    n_ci = len(comm.ins) if comm is not None else 0
    n_co = len(comm.outs) if comm is not None else 0

    def body(*refs):
        sink_ref, q_ref, ke_ref, ve_ref, bias_ref, o_ref, lse_ref, do_ref = refs[:8]
        comm_in = refs[9:9 + n_ci]
        dq_ref, dke_ref, dve_ref, ds_ref, dsink_ref = refs[9 + n_ci:14 + n_ci]
        comm_out = refs[14 + n_ci:14 + n_ci + n_co]
        sems = refs[14 + n_ci + n_co:]
        j, i = pl.program_id(0), pl.program_id(1)
        step = j * (nb // Q_BLOCKS) + i
        if comm is not None:
            @pl.when(step == 0)
            def _():
                comm.start(comm_in, comm_out, sems)

        @pl.when(i == 0)
        def _():
            dke_ref[...] = jnp.zeros(dke_ref.shape, F32)
            dve_ref[...] = jnp.zeros(dve_ref.shape, F32)
            ds_ref[...] = jnp.zeros(ds_ref.shape, F32)
            dsink_ref[...] = jnp.zeros(dsink_ref.shape, F32)

        low = _low_half((BLOCK, PAIR))
        for b in range(Q_BLOCKS):
            blk = i * Q_BLOCKS + b
            rows = slice(b * BLOCK, (b + 1) * BLOCK)
            starts = _key_blocks(blk, nb)
            kd = _three_blocks(ke_ref, starts)
            vd = _three_blocks(ve_ref, starts)
            kk = _split_pair(kd)
            dk_acc = jnp.zeros((3 * BLOCK, PAIR), F32)
            dv_acc = jnp.zeros((3 * BLOCK, PAIR), F32)
            for pr in range(GROUP // 2):
                lanes = slice(pr * PAIR, (pr + 1) * PAIR)
                qq, s = _pair_scores(q_ref[rows, lanes], kd, bias_ref, pr, blk, nb)
                l2 = lse_ref[rows, lanes]
                l2s = pltpu.roll(l2, HEAD_DIM, 1)
                lse_a, lse_b = jnp.where(low, l2, l2s), jnp.where(low, l2s, l2)
                p = jnp.exp(s - jnp.concatenate([jnp.concatenate([lse_a] * 3, axis=1),
                                                 jnp.concatenate([lse_b] * 3, axis=1)], axis=0))
                do2 = do_ref[rows, lanes]
                prod = do2.astype(F32) * o_ref[rows, lanes].astype(F32)
                delta_a = jnp.sum(jnp.where(low, prod, 0.0), axis=-1, keepdims=True)
                delta_b = jnp.sum(jnp.where(low, 0.0, prod), axis=-1, keepdims=True)
                dd = _split_pair(do2)
                dp = lax.dot_general(dd, vd, _DIMS["nt"], preferred_element_type=F32)
                ds = p * (dp - jnp.concatenate([delta_a, delta_b], axis=0))
                dsb = ds.astype(BF16)
                dq = lax.dot_general(jnp.concatenate([dsb[:BLOCK], dsb[BLOCK:]], axis=1), kk, _DIMS["nn"],
                                     preferred_element_type=F32) * scale
                dq_ref[rows, lanes] = dq.astype(BF16)
                dk_acc += lax.dot_general(dsb, qq, _DIMS["tn"], preferred_element_type=F32) * scale
                dv_acc += lax.dot_general(p.astype(BF16), dd, _DIMS["tn"], preferred_element_type=F32)
                ds_ref[2 * pr:2 * pr + 2] += ds.reshape(2, BLOCK, 3 * BLOCK)
                head = GROUP * j + 2 * pr
                p_sink = jnp.exp(jnp.where(low, sink_ref[head], sink_ref[head + 1]) - l2)
                dsink_ref[:, lanes] += jnp.sum(-p_sink * jnp.where(low, delta_a, delta_b), axis=0, keepdims=True)
            for t, start in enumerate(starts):
                dke_ref[pl.ds(start, BLOCK), :] += dk_acc[t * BLOCK:(t + 1) * BLOCK]
                dve_ref[pl.ds(start, BLOCK), :] += dv_acc[t * BLOCK:(t + 1) * BLOCK]

        if comm is not None:
            @pl.when(step == (3 * steps) // 4)
            def _():
                comm.mid(comm_in, comm_out, sems)

            @pl.when(step == steps - 1)
            def _():
                comm.finish(comm_in, comm_out, sems)

    kv_out = jax.ShapeDtypeStruct((T, N_KV_HEADS * PAIR), F32)
    job_ins, job_outs, job_sems = (comm.ins, comm.outs, comm.sems) if comm is not None else ([], [], [])
    outs = pl.pallas_call(
        body, name=name, grid=(N_KV_HEADS, nb // Q_BLOCKS),
        in_specs=[sp["sink"], sp["q"], sp["kv"], sp["kv"], sp["bias"], sp["o"], sp["o"], sp["o"], ANY_SPEC]
        + [HBM_SPEC] * n_ci,
        out_specs=[sp["q"], sp["kv"], sp["kv"], sp["bias"],
                   pl.BlockSpec((1, GROUP * HEAD_DIM), lambda j, i: (0, j))] + [HBM_SPEC] * n_co,
        out_shape=[jax.ShapeDtypeStruct(dproj.shape, BF16), kv_out, kv_out,
                   jax.ShapeDtypeStruct((N_HEADS, BLOCK, 3 * BLOCK), F32),
                   jax.ShapeDtypeStruct((1, N_HEADS * HEAD_DIM), F32)] + list(job_outs),
        scratch_shapes=list(job_sems), input_output_aliases={8: 0},
        compiler_params=_params("arbitrary" if comm is not None else "parallel", "arbitrary"),
    )(sink, proj, kexp, vexp, bias, out, lse, dout, dproj, *job_ins)
    return outs[:5], outs[5:]


GATE_COLS = 512


def _sigmoid(x):
    return 1.0 / (1.0 + jnp.exp(-x))


def _gate_specs(D, gate_off):
    nc = D // GATE_COLS
    base = gate_off // GATE_COLS
    return [pl.BlockSpec((ROWS, GATE_COLS), lambda i, c=base + g * nc + h: (i, c)) for g in range(3) for h in range(nc)]


def _merge_fwd(proj, gate_off, ya, yp, yt, scale, name):
    T, D = ya.shape
    nc = D // GATE_COLS

    def body(*refs):
        gates = refs[:3 * nc]
        ya_ref, yp_ref, yt_ref, s_ref, o_ref = refs[3 * nc:]
        for h in range(nc):
            cols = slice(h * GATE_COLS, (h + 1) * GATE_COLS)
            merged = (_sigmoid(_f32(gates[h])) * ya_ref[:, cols].astype(F32)
                      + _sigmoid(_f32(gates[nc + h])) * (yp_ref[:, cols].astype(F32) * s_ref[:, cols])
                      + _sigmoid(_f32(gates[2 * nc + h])) * yt_ref[:, cols].astype(F32))
            o_ref[:, cols] = merged.astype(BF16)

    yspec = _row_spec(D)
    return pl.pallas_call(
        body, name=name, grid=(T // ROWS,),
        in_specs=_gate_specs(D, gate_off) + [yspec, yspec, yspec, _const_spec((1, D))], out_specs=yspec,
        out_shape=jax.ShapeDtypeStruct((T, D), BF16), compiler_params=_params("parallel"),
    )(*([proj] * (3 * nc)), ya, yp, yt, scale)


def _merge_bwd(proj, gate_off, ya, yp, yt, scale, dm, name):
    T, D = ya.shape
    nc = D // GATE_COLS
    base = gate_off // GATE_COLS

    def body(gate_ref, ya_ref, yp_ref, yt_ref, s_ref, dm_ref, dg_ref, dya_ref, dyp_ref, dyt_ref, ds_ref):
        i, n = pl.program_id(0), pl.program_id(1)
        sg = _sigmoid(_f32(gate_ref))
        for g, (y_ref, dy_ref) in enumerate(((ya_ref, dya_ref), (yp_ref, dyp_ref), (yt_ref, dyt_ref))):
            for h in range(nc):
                @pl.when(n == g * nc + h)
                def _(g=g, h=h, y_ref=y_ref, dy_ref=dy_ref):
                    cols = slice(h * GATE_COLS, (h + 1) * GATE_COLS)
                    dy = dm_ref[:, cols].astype(F32) * sg
                    y = y_ref[:, cols].astype(F32)
                    if g == 1:
                        s_v = s_ref[:, cols]
                        part = jnp.sum(dy * y, axis=0, keepdims=True)

                        @pl.when(i == 0)
                        def _():
                            ds_ref[:, cols] = part

                        @pl.when(i > 0)
                        def _():
                            ds_ref[:, cols] += part

                        y = y * s_v
                        dy_ref[:, cols] = (dy * s_v).astype(BF16)
                    else:
                        dy_ref[:, cols] = dy.astype(BF16)
                    dg_ref[...] = (dy * y * (1.0 - sg)).astype(BF16)

    rows = _tile(T, 4 * ROWS)
    yspec = pl.BlockSpec((rows, D), lambda i, n: (i, 0))
    gspec = pl.BlockSpec((rows, GATE_COLS), lambda i, n: (i, base + n))
    sspec = pl.BlockSpec((1, D), lambda i, n: (0, 0))
    out = jax.ShapeDtypeStruct((T, D), BF16)
    return pl.pallas_call(
        body, name=name, grid=(T // rows, 3 * nc),
        in_specs=[gspec, yspec, yspec, yspec, sspec, yspec],
        out_specs=[gspec, yspec, yspec, yspec, sspec],
        out_shape=[jax.ShapeDtypeStruct(proj.shape, BF16), out, out, out, jax.ShapeDtypeStruct((1, D), F32)],
        compiler_params=_params("arbitrary", "arbitrary"),
    )(proj, ya, yp, yt, scale, dm)


def _swiglu_fwd(gu, name):
    T = gu.shape[0]
    F = gu.shape[1] // 2

    def body(gu_ref, o_ref):
        g = gu_ref[:, 0:F].astype(F32)
        o_ref[...] = (g * _sigmoid(g) * gu_ref[:, F:2 * F].astype(F32)).astype(BF16)

    return pl.pallas_call(
        body, name=name, grid=(T // ROWS,), in_specs=[_row_spec(2 * F)], out_specs=_row_spec(F),
        out_shape=jax.ShapeDtypeStruct((T, F), BF16), compiler_params=_params("parallel"),
    )(gu)


def _swiglu_bwd(gu, dact, name):
    T = gu.shape[0]
    F = gu.shape[1] // 2

    def body(gu_ref, d_ref, o_ref):
        g, d = gu_ref[:, 0:F].astype(F32), d_ref[...].astype(F32)
        sg = _sigmoid(g)
        o_ref[:, 0:F] = (d * gu_ref[:, F:2 * F].astype(F32) * sg * (1.0 + g * (1.0 - sg))).astype(BF16)
        o_ref[:, F:2 * F] = (d * g * sg).astype(BF16)

    return pl.pallas_call(
        body, name=name, grid=(T // ROWS,), in_specs=[_row_spec(2 * F), _row_spec(F)], out_specs=_row_spec(2 * F),
        out_shape=jax.ShapeDtypeStruct((T, 2 * F), BF16), compiler_params=_params("parallel"),
    )(gu, dact)


def _carried(plan, key, *args, **kwargs):
    job = plan.job(key) if plan is not None else None
    if job is None:
        return _matmul(*args, **kwargs)
    out, extra = _matmul(*args, comm=job, **kwargs)
    plan.done(key, extra)
    return out


def _local_step(x, target, wts, small, hooks=None):
    T, D = x.shape
    depth = small["g_mix"].shape[0]
    wts = list(wts) + [None] * (depth - len(wts))
    gate_off = wts[0]["w_inT"].shape[0] - 3 * D
    q_off = 4 * D
    bucket = _bucket_table()
    bias = _bias_table(small["rel_bias"], bucket, "bias_table")

    saved = []
    for l in range(depth):
        n = f"l{l}_"
        if hooks is not None and l > 0:
            wts[l] = hooks.weights(l)
        w = wts[l]
        plan = hooks.plan_fwd(l) if hooks is not None else None
        h = _rms_fwd(x, small["g_mix"][l], n + "rms_mix")
        proj = _carried(plan, "proj", h, w["w_inT"], "nt", BF16, n + "proj", tn_cap=WIDE_TILE)
        z, p = _mixer_fwd(proj, small["conv_w"][l], n + "mixer")
        kexp, vexp = _kv_expand(proj, q_off + D, n + "kv_expand")
        sink = small["attn_sink"][l]
        job = plan.job("attn") if plan is not None else None
        att, lse, extra = _attn_fwd(proj, q_off, kexp, vexp, bias, sink, n + "attn", comm=job)
        if job is not None:
            plan.done("attn", extra)
        ya =_matmul(z, w["w_a_out"], "nn", BF16, n + "ya")
        yp = _pool_mm(p, w["w_pool"], "nn", BF16, n + "yp")
        yt = _matmul(att, w["w_attn_out"], "nn", BF16, n + "yt")
        merged = _merge_fwd(proj, gate_off, ya, yp, yt, small["pool_scale"][l], n + "merge")
        x1 = _matmul(merged, w["w_o"], "nn", F32, n + "x1", res=x)
        h2 = _rms_fwd(x1, small["g_ffn"][l], n + "rms_ffn")
        gu = _carried(plan, "gu", h2, w["w_guT"], "nt", BF16, n + "gu", tn_cap=WIDE_TILE)
        act = _swiglu_fwd(gu, n + "swiglu")
        ff = w["w_down"].shape[0]
        x2 = _carried(plan, "x2", act, w["w_down"], "nn", F32, n + "x2", res=x1, tn_cap=512, tk_cap=ff)
        saved.append(dict(x=x, h=h, proj=proj, z=z, p=p, kexp=kexp, vexp=vexp, sink=sink, lse=lse, att=att,
                          ya=ya, yp=yp, yt=yt, merged=merged, x1=x1, h2=h2, gu=gu, act=act))
        x = x2

    loss, dx, dxb, dg_final = _loss_head(x, small["g_final"], target, "loss_head")

    gw = [None] * depth
    gs = {k_: [None] * depth for k_ in ("conv_w", "pool_scale", "g_mix", "g_ffn", "attn_sink")}
    ds_total = None
    for l in reversed(range(depth)):
        n = f"l{l}_b_"
        s, w, g = saved[l], wts[l], {}
        plan = hooks.plan_bwd(l) if hooks is not None else None
        ff = w["w_down"].shape[0]
        g["w_down"] = _carried(plan, "dw_down", s["act"], dxb, "tn", BF16, n + "dw_down", tm_cap=ff, tk_cap=512)
        dact = _matmul(dxb, w["w_down"], "nt", BF16, n + "dact", tm_cap=512, tn_cap=ff)
        dgu = _swiglu_bwd(s["gu"], dact, n + "swiglu")
        g["w_guT"] = _carried(plan, "dw_gu", dgu, s["h2"], "tn", BF16, n + "dw_gu", tm_cap=WIDE_TILE)
        dh2 = _carried(plan, "dh2", dgu, w["w_guT"], "nn", F32, n + "dh2", tn_cap=512, tk_cap=ff)
        dx1, dx1b, gs["g_ffn"][l] = _rms_bwd(s["x1"], small["g_ffn"][l], dh2, dx, n + "rms_ffn")
        g["w_o"] = _matmul(s["merged"], dx1b, "tn", BF16, n + "dw_o")
        dm = _matmul(dx1b, w["w_o"], "nt", BF16, n + "dmerged")
        dproj, dya, dyp, dyt, gs["pool_scale"][l] = _merge_bwd(
            s["proj"], gate_off, s["ya"], s["yp"], s["yt"], small["pool_scale"][l], dm, n + "merge")
        g["w_a_out"] = _matmul(s["z"], dya, "tn", BF16, n + "dw_a_out")
        dz = _matmul(dya, w["w_a_out"], "nt", BF16, n + "dz")
        g["w_pool"] = _pool_mm(s["p"], dyp, "tn", F32, n + "dw_pool")
        dp = _pool_mm(dyp, w["w_pool"], "nt", BF16, n + "dp")
        g["w_attn_out"] = _matmul(s["att"], dyt, "tn", BF16, n + "dw_attn_out")
        if hooks is not None:
            hooks.early_grads(l, g)
        datt = _carried(plan, "datt", dyt, w["w_attn_out"], "nt", BF16, n + "datt")
        dproj, gs["conv_w"][l] = _mixer_bwd(s["proj"], small["conv_w"][l], dz, dp, dproj, n + "mixer")
        job = plan.job("attn_b") if plan is not None else None
        (dproj, dke, dve, ds_sum, dsink), extra = _attn_bwd(
            s["proj"], q_off, s["kexp"], s["vexp"], bias, s["sink"], s["att"], s["lse"], datt, dproj, n + "attn",
            comm=job)
        if job is not None:
            plan.done("attn_b", extra)
        gs["attn_sink"][l] = dsink.reshape(N_HEADS, HEAD_DIM)[:, 0]
        ds_total = ds_sum if ds_total is None else ds_total + ds_sum
        dproj = _kv_fold(dke, dve, dproj, q_off + D, n + "kv_fold")
        g["w_inT"] = _carried(plan, "dw_in", dproj, s["h"], "tn", BF16, n + "dw_in", tm_cap=WIDE_TILE)
        dh = _matmul(dproj, w["w_inT"], "nn", F32, n + "dh", tk_cap=2816)
        dx, dxb, gs["g_mix"][l] = _rms_bwd(s["x"], small["g_mix"][l], dh, dx1, n + "rms_mix")
        gw[l] = g
        if hooks is not None:
            hooks.grads(l, g)

    d_rel =_bias_grad(ds_total, bucket, "bias_grad")[:, :, 0].T
    gs = {k_: jnp.stack(v_) for k_, v_ in gs.items()}
    gs["rel_bias"] = d_rel
    gs["g_final"] = dg_final
    return loss, dx, gw, gs


def _place():
    return lax.axis_index("x"), lax.axis_index("y"), lax.axis_index("c")


class _GatherJob:
    def __init__(self, parts):
        n = len(parts)
        self.n = n
        self.ins = list(parts)
        self.outs = [jax.ShapeDtypeStruct((N_DEV,) + p.shape, p.dtype) for p in parts]
        self.sems = [pltpu.SemaphoreType.DMA((7 * n,)), pltpu.SemaphoreType.DMA((7 * n,)), pltpu.SemaphoreType.DMA((n,))]

    def _copies(self, ins, outs, sems):
        send_sems, recv_sems, local_sems = sems
        x, y, c = _place()
        me, sibling = (x, y, c), (x, y, 1 - c)
        chips = [(1 - x, y), (x, 1 - y), (1 - x, 1 - y)]

        def rows(t, px, py, pc):
            return outs[t].at[4 * px + 2 * py + pc]

        def copy(t, k, block, to, src=None):
            return pltpu.make_async_remote_copy(
                src_ref=rows(t, *block) if src is None else src, dst_ref=rows(t, *block),
                send_sem=send_sems.at[7 * t + k], recv_sem=recv_sems.at[7 * t + k], device_id=to, device_id_type=MESH)

        ts = range(self.n)
        own = [pltpu.make_async_copy(ins[t], rows(t, *me), local_sems.at[t]) for t in ts]
        first = [copy(t, 0, me, sibling, src=ins[t]) for t in ts]
        first += [copy(t, 1 + j, me, (*chip, c), src=ins[t]) for t in ts for j, chip in enumerate(chips)]
        landed = [copy(t, 1 + j, (*chip, c), me) for j, chip in enumerate(chips) for t in ts]
        passed = [copy(t, 4 + j, (*chip, c), sibling) for j, chip in enumerate(chips) for t in ts]
        last = [copy(t, 0, sibling, me) for t in ts]
        last += [copy(t, 4 + j, (*chip, 1 - c), me) for t in ts for j, chip in enumerate(chips)]
        return own, first, landed, passed, last

    def start(self, ins, outs, sems):
        own, first, _, _, _ = self._copies(ins, outs, sems)
        for cp in own + first:
            cp.start()

    def mid(self, ins, outs, sems):
        _, _, landed, passed, _ = self._copies(ins, outs, sems)
        for arrived, onward in zip(landed, passed):
            arrived.wait_recv()
            onward.start()

    def finish(self, ins, outs, sems):
        own, first, _, passed, last = self._copies(ins, outs, sems)
        for cp in last:
            cp.wait_recv()
        for cp in first + passed:
            cp.wait_send()
        for cp in own:
            cp.wait()


class _SwapJob:
    def __init__(self, g):
        self.ins = [g]
        self.outs = [jax.ShapeDtypeStruct(g.shape[:1] + g.shape[2:], g.dtype)]
        self.sems = [pltpu.SemaphoreType.DMA, pltpu.SemaphoreType.DMA]

    def _copy(self, ins, outs, sems):
        x, y, c = _place()
        return pltpu.make_async_remote_copy(src_ref=ins[0].at[pl.ds(0, ins[0].shape[0]), 1 - c], dst_ref=outs[0],
                                            send_sem=sems[0], recv_sem=sems[1], device_id=(x, y, 1 - c),
                                            device_id_type=MESH)

    def start(self, ins, outs, sems):
        self._copy(ins, outs, sems).start()

    def mid(self, ins, outs, sems):
        pass

    def finish(self, ins, outs, sems):
        self._copy(ins, outs, sems).wait()


class _ExchangeJob:
    def __init__(self, p, row0, rows):
        self.row0, self.rows = row0, rows
        self.ins = [p]
        self.outs = [jax.ShapeDtypeStruct((3, rows) + p.shape[2:], p.dtype)]
        self.sems = [pltpu.SemaphoreType.DMA((3,)), pltpu.SemaphoreType.DMA((3,))]

    def _copies(self, ins, outs, sems):
        x, y, c = _place()
        chips = [(1 - x, y), (x, 1 - y), (1 - x, 1 - y)]
        return [pltpu.make_async_remote_copy(
            src_ref=ins[0].at[2 * px + py, pl.ds(self.row0, self.rows)], dst_ref=outs[0].at[k],
            send_sem=sems[0].at[k], recv_sem=sems[1].at[k], device_id=(px, py, c), device_id_type=MESH)
            for k, (px, py) in enumerate(chips)]

    def start(self, ins, outs, sems):
        for cp in self._copies(ins, outs, sems):
            cp.start()

    def mid(self, ins, outs, sems):
        pass

    def finish(self, ins, outs, sems):
        for cp in self._copies(ins, outs, sems):
            cp.wait()


def _all_gather(v, name):
    def body(x_ref, out_ref, send_sems, recv_sems, local_sem):
        x, y, c = _place()
        me, sibling = (x, y, c), (x, y, 1 - c)
        chips = [(1 - x, y), (x, 1 - y), (1 - x, 1 - y)]

        def rows(px, py, pc):
            return out_ref.at[4 * px + 2 * py + pc]

        def copy(k, block, to, src=None):
            return pltpu.make_async_remote_copy(
                src_ref=rows(*block) if src is None else src, dst_ref=rows(*block),
                send_sem=send_sems.at[k], recv_sem=recv_sems.at[k], device_id=to, device_id_type=MESH)

        mine = pltpu.make_async_copy(x_ref, rows(*me), local_sem)
        mine.start()
        first = [copy(0, me, sibling, src=x_ref)]
        first += [copy(1 + j, me, (*chip, c), src=x_ref) for j, chip in enumerate(chips)]
        for cp in first:
            cp.start()
        passed = [copy(4 + j, (*chip, c), sibling) for j, chip in enumerate(chips)]
        for j, chip in enumerate(chips):
            copy(1 + j, (*chip, c), me).wait_recv()
            passed[j].start()
        copy(0, sibling, me).wait_recv()
        for j, chip in enumerate(chips):
            copy(4 + j, (*chip, 1 - c), me).wait_recv()
        for cp in first + passed:
            cp.wait_send()
        mine.wait()

    return pl.pallas_call(
        body, name=name, in_specs=[HBM_SPEC], out_specs=HBM_SPEC,
        out_shape=jax.ShapeDtypeStruct((N_DEV,) + v.shape, v.dtype),
        scratch_shapes=[pltpu.SemaphoreType.DMA((7,)), pltpu.SemaphoreType.DMA((7,)), pltpu.SemaphoreType.DMA],
    )(v)


def _all_gather_many(parts, name):
    n = len(parts)

    def body(*refs):
        ins, outs = refs[:n], refs[n:2 * n]
        send_sems, recv_sems, local_sems = refs[2 * n:]
        x, y, c = _place()
        me, sibling = (x, y, c), (x, y, 1 - c)
        chips = [(1 - x, y), (x, 1 - y), (1 - x, 1 - y)]

        def rows(t, px, py, pc):
            return outs[t].at[4 * px + 2 * py + pc]

        def copy(t, k, block, to, src=None):
            return pltpu.make_async_remote_copy(
                src_ref=rows(t, *block) if src is None else src, dst_ref=rows(t, *block),
                send_sem=send_sems.at[7 * t + k], recv_sem=recv_sems.at[7 * t + k], device_id=to, device_id_type=MESH)

        mine = [pltpu.make_async_copy(ins[t], rows(t, *me), local_sems.at[t]) for t in range(n)]
        sends = []
        for t in range(n):
            mine[t].start()
            sends.append(copy(t, 0, me, sibling, src=ins[t]))
            sends += [copy(t, 1 + j, me, (*chip, c), src=ins[t]) for j, chip in enumerate(chips)]
        for cp in sends:
            cp.start()
        for j, chip in enumerate(chips):
            for t in range(n):
                copy(t, 1 + j, (*chip, c), me).wait_recv()
                passed = copy(t, 4 + j, (*chip, c), sibling)
                passed.start()
                sends.append(passed)
        for t in range(n):
            copy(t, 0, sibling, me).wait_recv()
            for j, chip in enumerate(chips):
                copy(t, 4 + j, (*chip, 1 - c), me).wait_recv()
        for cp in sends:
            cp.wait_send()
        for cp in mine:
            cp.wait()

    return pl.pallas_call(
        body, name=name, in_specs=[HBM_SPEC] * n, out_specs=[HBM_SPEC] * n,
        out_shape=[jax.ShapeDtypeStruct((N_DEV,) + p.shape, p.dtype) for p in parts],
        scratch_shapes=[pltpu.SemaphoreType.DMA((7 * n,)), pltpu.SemaphoreType.DMA((7 * n,)),
                        pltpu.SemaphoreType.DMA((n,))],
    )(*parts)


def _run_job(job, name):
    n_in, n_out = len(job.ins), len(job.outs)

    def body(*refs):
        ins, outs, sems = refs[:n_in], refs[n_in:n_in + n_out], refs[n_in + n_out:]
        job.start(ins, outs, sems)
        job.mid(ins, outs, sems)
        job.finish(ins, outs, sems)

    return pl.pallas_call(
        body, name=name, in_specs=[HBM_SPEC] * n_in, out_specs=[HBM_SPEC] * n_out, out_shape=list(job.outs),
        scratch_shapes=list(job.sems),
    )(*job.ins)


def _chip_exchange(p, name):
    def body(p_ref, out_ref, send_sems, recv_sems):
        x, y, c = _place()
        chips = [(1 - x, y), (x, 1 - y), (1 - x, 1 - y)]
        copies = [pltpu.make_async_remote_copy(
            src_ref=p_ref.at[2 * px + py], dst_ref=out_ref.at[k], send_sem=send_sems.at[k], recv_sem=recv_sems.at[k],
            device_id=(px, py, c), device_id_type=MESH) for k, (px, py) in enumerate(chips)]
        for cp in copies:
            cp.start()
        for cp in copies:
            cp.wait()

    return pl.pallas_call(
        body, name=name, in_specs=[HBM_SPEC], out_specs=HBM_SPEC,
        out_shape=jax.ShapeDtypeStruct((3,) + p.shape[1:], p.dtype),
        scratch_shapes=[pltpu.SemaphoreType.DMA((3,)), pltpu.SemaphoreType.DMA((3,))],
    )(p)


SUM_ROWS_CAP = 576


def _sum_parts(own, index, others, out_dtype, name, own_row0=0, own_step=0):
    R = others.shape[1]
    common = math.gcd(R, own.shape[1], own_row0 or R)
    rows = next(t for t in range(min(common, SUM_ROWS_CAP) // 16 * 16, 0, -16) if common % t == 0)
    k = others.shape[0]
    assert own_row0 % rows == 0 and own.shape[1] % rows == 0
    blk0 = own_row0 // rows
    per_own = own.shape[1] // rows

    def own_block(i, idx):
        if own_step:
            return (idx[0] + own_step * (i // per_own), i % per_own, 0)
        return (idx[0], blk0 + i, 0)

    def body(idx_ref, own_ref, *refs):
        del idx_ref
        acc = own_ref[...].astype(F32)
        for r in refs[:k]:
            acc = acc + r[...].astype(F32)
        refs[k][...] = acc.astype(out_dtype)

    grid_spec = pltpu.PrefetchScalarGridSpec(
        num_scalar_prefetch=1, grid=(R // rows,),
        in_specs=[pl.BlockSpec((None, rows, LANES), own_block)]
        + [pl.BlockSpec((None, rows, LANES), lambda i, idx, j=j: (j, i, 0)) for j in range(k)],
        out_specs=pl.BlockSpec((rows, LANES), lambda i, idx: (i, 0)))
    return pl.pallas_call(
        body, name=name, grid_spec=grid_spec,
        out_shape=jax.ShapeDtypeStruct((R, LANES), out_dtype), compiler_params=_params("parallel"),
    )(jnp.reshape(index, (1,)).astype(jnp.int32), own, *([others] * k))


def _adamw(w, g, m, v, name):
    shape = w.shape
    cols = shape[-1]
    rows_total = w.size // cols
    w2, g2, m2, v2 = (a.reshape(rows_total, cols) for a in (w, g, m, v))
    rows = rows_total
    if rows_total > ROWS:
        rows = next(r for r in range(ROWS, 0, -8) if rows_total % r == 0)

    def body(w_ref, g_ref, m_ref, v_ref, d_ref, nm_ref, nv_ref):
        gv = g_ref[...]
        nm = ADAM_B1 * m_ref[...] + (1.0 - ADAM_B1) * gv
        nv = ADAM_B2 * v_ref[...] + (1.0 - ADAM_B2) * (gv * gv)
        m_hat = nm / (1.0 - ADAM_B1 ** ADAM_STEP)
        v_hat = nv / (1.0 - ADAM_B2 ** ADAM_STEP)
        d_ref[...] = -ADAM_LR * (m_hat / (jnp.sqrt(v_hat) + ADAM_EPS) + ADAM_WD * w_ref[...])
        nm_ref[...] = nm
        nv_ref[...] = nv

    spec = pl.BlockSpec((rows, cols), lambda i: (i, 0))
    out = jax.ShapeDtypeStruct((rows_total, cols), F32)
    d, nm, nv = pl.pallas_call(
        body, name=name, grid=(rows_total // rows,), in_specs=[spec] * 4, out_specs=[spec] * 3,
        out_shape=[out, out, out], compiler_params=_params("parallel"),
    )(w2, g2, m2, v2)
    return d.reshape(shape), nm.reshape(shape), nv.reshape(shape)


BIG = ("w_in", "w_a_out", "w_pool", "w_attn_out", "w_o", "w_gu", "w_down")


LOCAL = dict(w_in="w_inT", w_a_out="w_a_out", w_pool="w_pool", w_attn_out="w_attn_out", w_o="w_o", w_gu="w_guT",
             w_down="w_down")


def _shard_rows(w, l):
    out = []
    for name in BIG:
        a = w[name][l]
        if name in ("w_in", "w_gu"):
            a = a.T
        elif name == "w_pool":
            a = a.reshape(-1, a.shape[-1])
        out.append(a.astype(BF16))
    return out


def _full_weights(names, gathered, w):
    out = {}
    for name, g in zip(names, gathered):
        if name == "w_pool":
            G, rg, cg = w[name].shape[1:]
            out[name] = jnp.transpose(g.reshape(N_DEV, G, rg, cg), (1, 0, 2, 3)).reshape(G, N_DEV * rg, cg)
        else:
            out[LOCAL[name]] = g.reshape(N_DEV * g.shape[1], g.shape[2])
    return out


def _split_grads(g, w, names=BIG, multiple=1):
    parts, spans, at = [], {}, 0
    for name in names:
        a = g[LOCAL[name]].astype(BF16)
        if name == "w_pool":
            G, rg, cg = w[name].shape[1:]
            a = jnp.transpose(a.reshape(G, N_DEV, rg, cg), (1, 0, 2, 3))
        a = a.reshape(N_DEV, -1, LANES)
        spans[name] = (at, at + a.shape[1])
        at += a.shape[1]
        parts.append(a)
    if at % multiple:
        parts.append(jnp.zeros((N_DEV, multiple - at % multiple, LANES), BF16))
    return jnp.concatenate(parts, axis=1), spans


def _own_grads(pieces, w):
    out = {}
    for name in BIG:
        per_layer = []
        for layer in pieces:
            packed, spans = next((p, s) for p, s in layer if name in s)
            per_layer.append(packed[spans[name][0]:spans[name][1]])
        a = jnp.stack(per_layer)
        if name in ("w_in", "w_gu"):
            sh = w[name].shape
            a = jnp.swapaxes(a.reshape(sh[0], sh[2], sh[1]), 1, 2)
        out[name] = a.reshape(w[name].shape)
    return out


SQUARE = ("w_a_out", "w_pool", "w_attn_out", "w_o")


class _Prefetch:
    def __init__(self, schedule, assign):
        self.schedule, self.assign = schedule, assign

    def job(self, key):
        if key not in self.assign:
            return None
        parts = []
        for layer, names in self.assign[key]:
            shards = dict(zip(BIG, _shard_rows(self.schedule.w, layer)))
            parts += [shards[name] for name in names]
        return _GatherJob(parts)

    def done(self, key, outs):
        for layer, names in self.assign[key]:
            self.schedule.arrived(layer, names, outs[:len(names)])
            outs = outs[len(names):]


class _Reduce:
    def __init__(self, split, spans, place, tag, swap_key="dw_down", carriers=("dw_gu", "dw_in", "dh2"),
                 chunk_rows=1024):
        self.split, self.spans, self.tag = split, spans, tag
        self.swap_key, self.carriers = swap_key, carriers
        self.core, self.chip = place[2], 2 * place[0] + place[1]
        self.rows = split.shape[1]
        self.chunks = [(r, min(chunk_rows, self.rows - r)) for r in range(0, self.rows, chunk_rows)]
        assert len(self.chunks) <= len(carriers)
        self.sums = [None] * len(self.chunks)

    def _swap_job(self):
        return _SwapJob(self.split.reshape(4, 2, self.rows, LANES))

    def _pair_sum(self, from_sibling):
        pair = _sum_parts(self.split, self.core, from_sibling.reshape(1, 4 * self.rows, LANES), BF16,
                          self.tag + "pair_sum", own_step=2)
        self.pair = pair.reshape(4, self.rows, LANES)

    def _chip_sum(self, n, from_chips):
        self.sums[n] = _sum_parts(self.pair, self.chip, from_chips, F32, f"{self.tag}chip_sum{n}",
                                  own_row0=self.chunks[n][0])

    def job(self, key):
        if key == self.swap_key:
            return self._swap_job()
        if key in self.carriers[:len(self.chunks)]:
            return _ExchangeJob(self.pair, *self.chunks[self.carriers.index(key)])
        return None

    def done(self, key, outs):
        if key == self.swap_key:
            self._pair_sum(outs[0])
        else:
            self._chip_sum(self.carriers.index(key), outs[0])

    def run(self):
        self._pair_sum(_run_job(self._swap_job(), self.tag + "reduce_pair")[0])
        self.chunks, self.sums = [(0, self.rows)], [None]
        self._chip_sum(0, _run_job(_ExchangeJob(self.pair, 0, self.rows), self.tag + "reduce_chips")[0])
        return self.result()

    def result(self):
        return (self.sums[0] if len(self.sums) == 1 else jnp.concatenate(self.sums, axis=0)), self.spans


class _Plans:
    def __init__(self, plans):
        self.plans = plans

    def job(self, key):
        self.owner = next((p for p in self.plans if p.job(key) is not None), None)
        return self.owner.job(key) if self.owner is not None else None

    def done(self, key, outs):
        self.owner.done(key, outs)


class _Schedule:
    EARLY = ("w_gu", "w_down") + SQUARE
    EARLY_PAD = 512

    def __init__(self, w, place):
        self.w, self.place = w, place
        self.depth = w["w_in"].shape[0]
        self.full = [{} for _ in range(self.depth)]
        self.reduce = {}
        self.pieces = [None] * self.depth
        self.active = []

    def arrived(self, layer, names, gathered):
        self.full[layer].update(_full_weights(names, gathered, self.w))

    def plan_fwd(self, l):
        nxt = l + 1
        more = nxt < self.depth
        if l == 0:
            assign = dict(proj=[(0, ("w_gu",) + SQUARE)], attn=[(0, ("w_down",))])
            if more:
                assign["attn"].append((nxt, ("w_in",)))
                assign.update(gu=[(nxt, ("w_gu",))], x2=[(nxt, ("w_down",))])
        else:
            assign = dict(proj=[(l, SQUARE)])
            if more:
                assign["proj"].append((nxt, ("w_down",)))
                assign.update(attn=[(nxt, ("w_in",))], gu=[(nxt, ("w_gu",))])
        return _Prefetch(self, assign)

    def weights(self, l):
        return self.full[l]

    def plan_bwd(self, l):
        self.active = [self.reduce[l + 1]] if l + 1 in self.reduce else []
        return _Plans(self.active)

    def early_grads(self, l, g):
        if l == 0:
            split, spans = _split_grads(g, self.w, self.EARLY, self.EARLY_PAD)
            self.early = _Reduce(split, spans, self.place, "l0_early_", swap_key="datt", carriers=("attn_b",),
                                 chunk_rows=split.shape[1])
            self.active.append(self.early)

    def grads(self, l, g):
        if l + 1 in self.reduce:
            self.pieces[l + 1] = [self.reduce[l + 1].result()]
        if l == 0:
            last = _Reduce(*_split_grads(g, self.w, ("w_in",)), self.place, "l0_")
            self.pieces[0] = [self.early.result(), last.run()]
        else:
            self.reduce[l] = _Reduce(*_split_grads(g, self.w), self.place, f"l{l}_")


SMALL_ROWS = 32


def _pack_small(gs, L, D):
    rows = [gs["pool_scale"].reshape(L, D), gs["g_mix"].reshape(L, D), gs["g_ffn"].reshape(L, D),
            gs["g_final"].reshape(1, D), gs["conv_w"][:, :3].reshape(3 * L, D),
            jnp.pad(gs["attn_sink"].reshape(1, -1), ((0, 0), (0, D - L * N_HEADS))),
            jnp.pad(gs["rel_bias"].reshape(1, -1), ((0, 0), (0, D - N_BUCKETS * N_HEADS)))]
    a = jnp.concatenate(rows, axis=0)
    return jnp.pad(a, ((0, SMALL_ROWS - a.shape[0]), (0, 0)))


def _unpack_small(a, L, D):
    g = {}
    g["pool_scale"] = a[0:L]
    g["g_mix"] = a[L:2 * L]
    g["g_ffn"] = a[2 * L:3 * L]
    g["g_final"] = a[3 * L]
    g["conv_w"] = a[3 * L + 1:6 * L + 1].reshape(L, 3, 1, D)
    g["attn_sink"] = a[6 * L + 1, :L * N_HEADS].reshape(L, N_HEADS)
    g["rel_bias"] = a[6 * L + 2, :N_BUCKETS * N_HEADS].reshape(N_BUCKETS, N_HEADS)
    return g


WEIGHTS = ("w_in", "conv_w", "w_a_out", "w_pool", "pool_scale", "w_attn_out", "attn_sink", "w_o", "g_mix", "g_ffn",
           "w_gu", "w_down", "rel_bias", "g_final")


def kernel(x, w_in, conv_w, w_a_out, w_pool, pool_scale, w_attn_out, attn_sink, w_o, g_mix, g_ffn, w_gu, w_down, rel_bias, g_final, loss_target, m_w_in, m_conv_w, m_w_a_out, m_w_pool, m_pool_scale, m_w_attn_out, m_attn_sink, m_w_o, m_g_mix, m_g_ffn, m_w_gu, m_w_down, m_rel_bias, m_g_final, v_w_in, v_conv_w, v_w_a_out, v_w_pool, v_pool_scale, v_w_attn_out, v_attn_sink, v_w_o, v_g_mix, v_g_ffn, v_w_gu, v_w_down, v_rel_bias, v_g_final):
    w = dict(w_in=w_in, conv_w=conv_w, w_a_out=w_a_out, w_pool=w_pool, pool_scale=pool_scale, w_attn_out=w_attn_out,
             attn_sink=attn_sink, w_o=w_o, g_mix=g_mix, g_ffn=g_ffn, w_gu=w_gu, w_down=w_down, rel_bias=rel_bias,
             g_final=g_final)
    m = dict(w_in=m_w_in, conv_w=m_conv_w, w_a_out=m_w_a_out, w_pool=m_w_pool, pool_scale=m_pool_scale,
             w_attn_out=m_w_attn_out, attn_sink=m_attn_sink, w_o=m_w_o, g_mix=m_g_mix, g_ffn=m_g_ffn, w_gu=m_w_gu,
             w_down=m_w_down, rel_bias=m_rel_bias, g_final=m_g_final)
    v = dict(w_in=v_w_in, conv_w=v_conv_w, w_a_out=v_w_a_out, w_pool=v_w_pool, pool_scale=v_pool_scale,
             w_attn_out=v_w_attn_out, attn_sink=v_attn_sink, w_o=v_w_o, g_mix=v_g_mix, g_ffn=v_g_ffn, w_gu=v_w_gu,
             w_down=v_w_down, rel_bias=v_rel_bias, g_final=v_g_final)
    T, D = x.shape[1], x.shape[2]
    L = w_in.shape[0]
    cx, cy, cc = _place()

    schedule = _Schedule(w, (cx, cy, cc))
    schedule.arrived(0, ("w_in",), _run_job(_GatherJob(_shard_rows(w, 0)[:1]), "gather_w_in_l0"))
    cw = jnp.pad(conv_w.reshape(L * 3, -1), ((0, 16 - L * 3), (0, 0)))
    cw = _all_gather(cw, "gather_conv_w")
    cw = jnp.transpose(cw, (1, 0, 2)).reshape(16, -1)[:L * 3].reshape(L, 3, -1)
    small = dict(conv_w=jnp.pad(cw, ((0, 0), (0, 5), (0, 0))), pool_scale=pool_scale.reshape(L, 1, D),
                 g_mix=g_mix.reshape(L, 1, D), g_ffn=g_ffn.reshape(L, 1, D), attn_sink=attn_sink,
                 rel_bias=rel_bias, g_final=g_final.reshape(1, D))

    loss, dx, _, gs = _local_step(x[0], loss_target[0], [schedule.full[0]], small, schedule)
    loss = lax.psum(loss[0, 0], ("x", "y", "c"))
    grads = _own_grads(schedule.pieces, w)

    small_all = _all_gather(_pack_small(gs, L, D), "gather_small")
    small_sum = _sum_parts(small_all, jnp.int32(0), small_all[1:], F32, "small_sum")
    gsm = _unpack_small(small_sum, L, D)
    W8 = D // N_DEV
    dev = 4 * cx + 2 * cy + cc
    gsm["conv_w"] = lax.dynamic_slice_in_dim(gsm["conv_w"], dev * W8, W8, axis=3)
    grads.update(gsm)

    deltas, new_m, new_v = {}, {}, {}
    for name in WEIGHTS:
        deltas[name], new_m[name], new_v[name] = _adamw(w[name], grads[name], m[name], v[name], "adamw_" + name)

    return (loss, dx[None], *[grads[n] for n in WEIGHTS], *[deltas[n] for n in WEIGHTS],
            *[new_m[n] for n in WEIGHTS], *[new_v[n] for n in WEIGHTS])
```

```python
import functools
import math

import jax
import jax.numpy as jnp
from jax import lax
from jax.experimental import pallas as pl
from jax.experimental.pallas import tpu as pltpu

F32 = jnp.float32
BF16 = jnp.bfloat16
MESH = pl.DeviceIdType.MESH

N_DEV = 8
N_HEADS = 16
N_KV_HEADS = 4
HEAD_DIM = 64
GROUP = N_HEADS // N_KV_HEADS
BLOCK = 128
WINDOW = 128
N_BUCKETS = 32
MAX_DISTANCE = 128
POOL_WINDOWS = (2, 4, 8, 16)
POOL_GROUPS = 4
HALO = 8
EPS = 1e-6
NEG_INF = -1e30

ADAM_LR = 0.001
ADAM_B1 = 0.9
ADAM_B2 = 0.999
ADAM_EPS = 1e-08
ADAM_WD = 0.01
ADAM_STEP = 10

LANES = 1024
VMEM_LIMIT_BYTES = 48 * 1024 * 1024


def _params(*sem):
    return pltpu.CompilerParams(dimension_semantics=sem, vmem_limit_bytes=VMEM_LIMIT_BYTES)


def _tile(n, cap):
    if n <= cap:
        return n
    for t in range(cap - cap % 128, 0, -128):
        if n % t == 0:
            return t
    raise ValueError(f"no tile for {n}")


WIDE_TILE = 2176

_DIMS = {"nn": (((1,), (0,)), ((), ())), "nt": (((1,), (1,)), ((), ())), "tn": (((0,), (0,)), ((), ()))}


HBM_SPEC = pl.BlockSpec(memory_space=pltpu.HBM)
ANY_SPEC = pl.BlockSpec(memory_space=pl.ANY)


def _matmul(a, b, mode, out_dtype, name, res=None, tm_cap=1024, tn_cap=1024, tk_cap=1024, comm=None):
    if mode == "tn":
        K, M = a.shape
    else:
        M, K = a.shape
    N = b.shape[0] if mode == "nt" else b.shape[1]
    tm, tn, tk = _tile(M, tm_cap), _tile(N, tn_cap), _tile(K, tk_cap)
    nk = K // tk
    a_spec = pl.BlockSpec((tk, tm), lambda i, j, k: (k, i)) if mode == "tn" else pl.BlockSpec((tm, tk), lambda i, j, k: (i, k))
    b_spec = pl.BlockSpec((tn, tk), lambda i, j, k: (j, k)) if mode == "nt" else pl.BlockSpec((tk, tn), lambda i, j, k: (k, j))
    o_spec = pl.BlockSpec((tm, tn), lambda i, j, k: (i, j))
    dims = _DIMS[mode]
    has_res = res is not None
    gm, gn = M // tm, N // tn
    steps = gm * gn * nk
    n_in = 2 + has_res
    n_ci = len(comm.ins) if comm is not None else 0
    n_co = len(comm.outs) if comm is not None else 0

    def body(*refs):
        a_ref, b_ref = refs[0], refs[1]
        res_ref = refs[2] if has_res else None
        comm_in = refs[n_in:n_in + n_ci]
        o_ref = refs[n_in + n_ci]
        comm_out = refs[n_in + n_ci + 1:n_in + n_ci + 1 + n_co]
        acc_ref = refs[n_in + n_ci + 1 + n_co]
        sems = refs[n_in + n_ci + 2 + n_co:]
        k = pl.program_id(2)
        step = (pl.program_id(0) * gn + pl.program_id(1)) * nk + k
        if comm is not None:
            @pl.when(step == 0)
            def _():
                comm.start(comm_in, comm_out, sems)

        part = lax.dot_general(a_ref[...], b_ref[...], dims, preferred_element_type=F32)

        @pl.when(k == 0)
        def _():
            acc_ref[...] = part

        @pl.when(k > 0)
        def _():
            acc_ref[...] += part

        @pl.when(k == nk - 1)
        def _():
            out = acc_ref[...]
            if has_res:
                out = out + res_ref[...]
            o_ref[...] = out.astype(out_dtype)

        if comm is not None:
            @pl.when(step == (3 * steps) // 4)
            def _():
                comm.mid(comm_in, comm_out, sems)

            @pl.when(step == steps - 1)
            def _():
                comm.finish(comm_in, comm_out, sems)

    in_specs = [a_spec, b_spec] + ([o_spec] if has_res else [])
    args = (a, b) + ((res,) if has_res else ())
    out_shape = jax.ShapeDtypeStruct((M, N), out_dtype)
    if comm is None:
        return pl.pallas_call(
            body, name=name, grid=(gm, gn, nk), in_specs=in_specs, out_specs=o_spec, out_shape=out_shape,
            scratch_shapes=[pltpu.VMEM((tm, tn), F32)],
            compiler_params=_params("parallel", "parallel", "arbitrary"),
        )(*args)
    outs = pl.pallas_call(
        body, name=name, grid=(gm, gn, nk),
        in_specs=in_specs + [HBM_SPEC] * n_ci, out_specs=[o_spec] + [HBM_SPEC] * n_co,
        out_shape=[out_shape] + list(comm.outs),
        scratch_shapes=[pltpu.VMEM((tm, tn), F32)] + list(comm.sems),
        compiler_params=_params("arbitrary", "arbitrary", "arbitrary"),
    )(*args, *comm.ins)
    return outs[0], outs[1:]


def _pool_mm(a, w, mode, out_dtype, name):
    T = a.shape[0]
    G = POOL_GROUPS
    cg = a.shape[1] // G
    tm = _tile(T, 1024)
    nt = T // tm
    dims = _DIMS[mode]
    if mode == "tn":
        def body(a_ref, d_ref, o_ref):
            part = lax.dot_general(a_ref[...], d_ref[...], dims, preferred_element_type=F32)

            @pl.when(pl.program_id(1) == 0)
            def _():
                o_ref[...] = part

            @pl.when(pl.program_id(1) > 0)
            def _():
                o_ref[...] += part

        return pl.pallas_call(
            body, name=name, grid=(G, nt),
            in_specs=[pl.BlockSpec((tm, cg), lambda g, i: (i, g)), pl.BlockSpec((tm, cg), lambda g, i: (i, g))],
            out_specs=pl.BlockSpec((None, cg, cg), lambda g, i: (g, 0, 0)),
            out_shape=jax.ShapeDtypeStruct((G, cg, cg), F32),
            compiler_params=_params("parallel", "arbitrary"),
        )(a, w)

    def body(a_ref, w_ref, o_ref):
        o_ref[...] = lax.dot_general(a_ref[...], w_ref[...], dims, preferred_element_type=F32).astype(out_dtype)

    return pl.pallas_call(
        body, name=name, grid=(G, nt),
        in_specs=[pl.BlockSpec((tm, cg), lambda g, i: (i, g)), pl.BlockSpec((None, cg, cg), lambda g, i: (g, 0, 0))],
        out_specs=pl.BlockSpec((tm, cg), lambda g, i: (i, g)),
        out_shape=jax.ShapeDtypeStruct((T, G * cg), out_dtype),
        compiler_params=_params("parallel", "parallel"),
    )(a, w)


ROWS = 256
HALO_BLOCK = 16


def _row_spec(d, col=0, rows=ROWS):
    return pl.BlockSpec((rows, d), lambda i, col=col: (i, col))


def _const_spec(shape):
    return pl.BlockSpec(shape, lambda *_: (0,) * len(shape))


def _rms_fwd(x, g, name):
    T, D = x.shape

    def body(x_ref, g_ref, h_ref):
        xv = x_ref[...]
        r = lax.rsqrt(jnp.mean(xv * xv, axis=-1, keepdims=True) + EPS)
        h_ref[...] = (xv * r * g_ref[...]).astype(BF16)

    return pl.pallas_call(
        body, name=name, grid=(T // ROWS,),
        in_specs=[_row_spec(D), _const_spec((1, D))], out_specs=_row_spec(D),
        out_shape=jax.ShapeDtypeStruct((T, D), BF16), compiler_params=_params("parallel"),
    )(x, g)


def _accumulate(ref, part):
    first = pl.program_id(0) == 0

    @pl.when(first)
    def _():
        ref[...] = part

    @pl.when(jnp.logical_not(first))
    def _():
        ref[...] += part


def _rms_bwd(x, g, dh, dres, name):
    T, D = x.shape

    def body(x_ref, g_ref, dh_ref, dres_ref, dx_ref, dxb_ref, dg_ref):
        xv = x_ref[...]
        r = lax.rsqrt(jnp.mean(xv * xv, axis=-1, keepdims=True) + EPS)
        xhat = xv * r
        dh_v = dh_ref[...]
        dxhat = dh_v * g_ref[...]
        dx = dres_ref[...] + r * (dxhat - xhat * jnp.mean(dxhat * xhat, axis=-1, keepdims=True))
        dx_ref[...] = dx
        dxb_ref[...] = dx.astype(BF16)
        _accumulate(dg_ref, jnp.sum(dh_v * xhat, axis=0, keepdims=True))

    return pl.pallas_call(
        body, name=name, grid=(T // ROWS,),
        in_specs=[_row_spec(D), _const_spec((1, D)), _row_spec(D), _row_spec(D)],
        out_specs=[_row_spec(D), _row_spec(D), _const_spec((1, D))],
        out_shape=[jax.ShapeDtypeStruct((T, D), F32), jax.ShapeDtypeStruct((T, D), BF16),
                   jax.ShapeDtypeStruct((1, D), F32)],
        compiler_params=_params("arbitrary"),
    )(x, g, dh, dres)


def _loss_head(x, g, target, name):
    T, D = x.shape

    def body(x_ref, g_ref, t_ref, loss_ref, dx_ref, dxb_ref, dg_ref):
        xv = x_ref[...]
        gv = g_ref[...]
        r = lax.rsqrt(jnp.mean(xv * xv, axis=-1, keepdims=True) + EPS)
        xhat = xv * r
        err = xhat * gv - t_ref[...]
        loss = 0.5 * jnp.sum(jnp.mean(err * err, axis=-1, keepdims=True), axis=0, keepdims=True)
        dy = err * (1.0 / D)
        dxhat = dy * gv
        dx = r * (dxhat - xhat * jnp.mean(dxhat * xhat, axis=-1, keepdims=True))
        dx_ref[...] = dx
        dxb_ref[...] = dx.astype(BF16)
        _accumulate(loss_ref, loss)
        _accumulate(dg_ref, jnp.sum(dy * xhat, axis=0, keepdims=True))

    return pl.pallas_call(
        body, name=name, grid=(T // ROWS,),
        in_specs=[_row_spec(D), _const_spec((1, D)), _row_spec(D)],
        out_specs=[_const_spec((1, 1)), _row_spec(D), _row_spec(D), _const_spec((1, D))],
        out_shape=[jax.ShapeDtypeStruct((1, 1), F32), jax.ShapeDtypeStruct((T, D), F32),
                   jax.ShapeDtypeStruct((T, D), BF16), jax.ShapeDtypeStruct((1, D), F32)],
        compiler_params=_params("arbitrary"),
    )(x, g, target)


def _halo_specs(d, col, n_blocks):
    per = ROWS // HALO_BLOCK
    last = n_blocks * per - 1
    prev = pl.BlockSpec((HALO_BLOCK, d), lambda i, col=col: (jnp.maximum(i * per - 1, 0), col))
    nxt = pl.BlockSpec((HALO_BLOCK, d), lambda i, col=col: (jnp.minimum((i + 1) * per, last), col))
    return prev, nxt


def _with_halo(prev, cur, nxt, n_blocks):
    i = pl.program_id(0)
    prev = jnp.where(i > 0, prev[HALO_BLOCK - HALO:], 0.0)
    nxt = jnp.where(i < n_blocks - 1, nxt[:HALO], 0.0)
    return jnp.concatenate([prev, cur, nxt], axis=0)


def _f32(ref):
    return ref[...].astype(F32)


def _shift(ext, k):
    n = ext.shape[0]
    v = ext if k == 0 else pltpu.roll(ext, (-k) % n, 0)
    return v[HALO:HALO + ROWS]


def _shift_full(ext, k):
    n = ext.shape[0]
    return pltpu.roll(ext, (-k) % n, 0)


def _pool_counts(T):
    n = ROWS + 2 * HALO
    t = pl.program_id(0) * ROWS - HALO + lax.broadcasted_iota(jnp.int32, (n, 1), 0)
    out = []
    for w in POOL_WINDOWS:
        lo = jnp.maximum(t - w // 2, 0)
        hi = jnp.minimum(t + (w - 1 - w // 2), T - 1)
        out.append(jnp.maximum(hi - lo + 1, 1).astype(F32))
    return out


def _window_sums(e, sign):
    s2 = e + _shift_full(e, -sign)
    s4 = _shift_full(s2, -1) + _shift_full(s2, 1)
    s8 = _shift_full(s4, -2) + _shift_full(s4, 2)
    s16 = _shift_full(s8, -4) + _shift_full(s8, 4)
    return s2, s4, s8, s16


def _mixer_fwd(proj, conv_w, name):
    T = proj.shape[0]
    W = conv_w.shape[1]
    nb = T // ROWS
    cg = W // POOL_GROUPS

    def body(b_ref, c_ref, x_ref, u_ref, cp_ref, cn_ref, xp_ref, xn_ref, up_ref, un_ref, w_ref, z_ref, p_ref):
        uc = _with_halo(_f32(cp_ref) * _f32(xp_ref), _f32(c_ref) * _f32(x_ref), _f32(cn_ref) * _f32(xn_ref), nb)
        w0, w1, w2 = w_ref[0:1, :], w_ref[1:2, :], w_ref[2:3, :]
        y = w0 * _shift(uc, -1) + w1 * _shift(uc, 0) + w2 * _shift(uc, 1)
        z_ref[...] = (_f32(b_ref) * y).astype(BF16)
        e = _with_halo(_f32(up_ref), _f32(u_ref), _f32(un_ref), nb)
        counts = _pool_counts(T)
        for gi in range(POOL_GROUPS):
            eg = e[:, gi * cg:(gi + 1) * cg]
            s = _window_sums(eg, 1)[gi]
            p = s[HALO:HALO + ROWS] / counts[gi][HALO:HALO + ROWS] - eg[HALO:HALO + ROWS]
            p_ref[:, gi * cg:(gi + 1) * cg] = p.astype(BF16)

    halo = [s for col in (1, 2, 3) for s in _halo_specs(W, col, nb)]
    return pl.pallas_call(
        body, name=name, grid=(nb,),
        in_specs=[_row_spec(W, 0), _row_spec(W, 1), _row_spec(W, 2), _row_spec(W, 3)] + halo + [_const_spec((8, W))],
        out_specs=[_row_spec(W), _row_spec(W)],
        out_shape=[jax.ShapeDtypeStruct((T, W), BF16), jax.ShapeDtypeStruct((T, W), BF16)],
        compiler_params=_params("parallel"),
    )(proj, proj, proj, proj, proj, proj, proj, proj, proj, proj, conv_w)


def _mixer_bwd(proj, conv_w, dz, dp, dproj, name):
    T = proj.shape[0]
    W = conv_w.shape[1]
    nb = T // ROWS
    cg = W // POOL_GROUPS

    def body(b_ref, c_ref, x_ref, dz_ref, dp_ref,
             bp_ref, bn_ref, cp_ref, cn_ref, xp_ref, xn_ref, dzp_ref, dzn_ref, dpp_ref, dpn_ref, w_ref, _,
             o_ref, dw_ref):
        cv, xv, dzv = _f32(c_ref), _f32(x_ref), _f32(dz_ref)
        uc = _with_halo(_f32(cp_ref) * _f32(xp_ref), cv * xv, _f32(cn_ref) * _f32(xn_ref), nb)
        dy = _with_halo(_f32(dzp_ref) * _f32(bp_ref), dzv * _f32(b_ref), _f32(dzn_ref) * _f32(bn_ref), nb)
        w0, w1, w2 = w_ref[0:1, :], w_ref[1:2, :], w_ref[2:3, :]
        um, u0, up = _shift(uc, -1), _shift(uc, 0), _shift(uc, 1)
        o_ref[:, 0:W] = (dzv * (w0 * um + w1 * u0 + w2 * up)).astype(BF16)
        dy0 = _shift(dy, 0)
        duc = w0 * _shift(dy, 1) + w1 * dy0 + w2 * _shift(dy, -1)
        o_ref[:, W:2 * W] = (duc * xv).astype(BF16)
        o_ref[:, 2 * W:3 * W] = (duc * cv).astype(BF16)
        row = lax.broadcasted_iota(jnp.int32, (8, W), 0)
        dw = jnp.where(row == 0, jnp.sum(dy0 * um, axis=0, keepdims=True),
                       jnp.where(row == 1, jnp.sum(dy0 * u0, axis=0, keepdims=True),
                                 jnp.where(row == 2, jnp.sum(dy0 * up, axis=0, keepdims=True), 0.0)))
        _accumulate(dw_ref, dw)
        d = _with_halo(_f32(dpp_ref), _f32(dp_ref), _f32(dpn_ref), nb)
        counts = _pool_counts(T)
        for gi in range(POOL_GROUPS):
            dg = d[:, gi * cg:(gi + 1) * cg]
            s = _window_sums(dg / counts[gi], -1)[gi]
            o_ref[:, 3 * W + gi * cg:3 * W + (gi + 1) * cg] = (s[HALO:HALO + ROWS] - dg[HALO:HALO + ROWS]).astype(BF16)

    def halo(col):
        return list(_halo_specs(W, col, nb))

    return pl.pallas_call(
        body, name=name, grid=(nb,),
        in_specs=[_row_spec(W, 0), _row_spec(W, 1), _row_spec(W, 2), _row_spec(W), _row_spec(W)]
        + halo(0) + halo(1) + halo(2) + halo(0) + halo(0) + [_const_spec((8, W)), ANY_SPEC],
        out_specs=[_row_spec(4 * W), _const_spec((8, W))],
        out_shape=[jax.ShapeDtypeStruct(dproj.shape, BF16), jax.ShapeDtypeStruct((8, W), F32)],
        input_output_aliases={16: 0}, compiler_params=_params("arbitrary"),
    )(proj, proj, proj, dz, dp, proj, proj, proj, proj, proj, proj, dz, dz, dp, dp, conv_w, dproj)


def _t5_bucket(rel):
    half = N_BUCKETS // 2
    max_exact = half // 2
    ret = jnp.where(rel > 0, half, 0)
    n = jnp.abs(rel)
    nf = jnp.maximum(n, 1).astype(jnp.float32)
    large = max_exact + (jnp.log(nf / max_exact) / math.log(MAX_DISTANCE / max_exact)
                         * (half - max_exact)).astype(jnp.int32)
    large = jnp.minimum(large, half - 1)
    return ret + jnp.where(n < max_exact, n, large)


def _bucket_table():
    qi = jnp.arange(BLOCK)[:, None]
    kj = jnp.arange(3 * BLOCK)[None, :]
    rel = kj - BLOCK - qi
    return jnp.where(jnp.abs(rel) <= WINDOW, _t5_bucket(rel), -1).astype(jnp.int32)


def _bias_table(rel_bias, bucket, name):
    def body(rb_ref, bucket_ref, o_ref):
        h = pl.program_id(0)
        bk = bucket_ref[...]
        acc = jnp.full(bk.shape, NEG_INF, F32)
        for b in range(N_BUCKETS):
            acc = jnp.where(bk == b, rb_ref[b, h], acc)
        o_ref[...] = acc

    return pl.pallas_call(
        body, name=name, grid=(N_HEADS,),
        in_specs=[pl.BlockSpec(memory_space=pltpu.SMEM), _const_spec((BLOCK, 3 * BLOCK))],
        out_specs=pl.BlockSpec((None, BLOCK, 3 * BLOCK), lambda h: (h, 0, 0)),
        out_shape=jax.ShapeDtypeStruct((N_HEADS, BLOCK, 3 * BLOCK), F32),
        compiler_params=_params("parallel"),
    )(rel_bias, bucket)


def _bias_grad(ds_sum, bucket, name):
    def body(ds_ref, bucket_ref, o_ref):
        bk = bucket_ref[...]
        ds = ds_ref[...]
        row = lax.broadcasted_iota(jnp.int32, (N_BUCKETS, 128), 0)
        acc = jnp.zeros((N_BUCKETS, 128), F32)
        for b in range(N_BUCKETS):
            s = jnp.sum(jnp.sum(jnp.where(bk == b, ds, 0.0), axis=1, keepdims=True), axis=0, keepdims=True)
            acc = jnp.where(row == b, s, acc)
        o_ref[...] = acc

    return pl.pallas_call(
        body, name=name, grid=(N_HEADS,),
        in_specs=[pl.BlockSpec((None, BLOCK, 3 * BLOCK), lambda h: (h, 0, 0)), _const_spec((BLOCK, 3 * BLOCK))],
        out_specs=pl.BlockSpec((None, N_BUCKETS, 128), lambda h: (h, 0, 0)),
        out_shape=jax.ShapeDtypeStruct((N_HEADS, N_BUCKETS, 128), F32),
        compiler_params=_params("parallel"),
    )(ds_sum, bucket)


PAIR = 2 * HEAD_DIM
Q_BLOCKS = 2


def _low_half(shape):
    return lax.broadcasted_iota(jnp.int32, shape, len(shape) - 1) % PAIR < HEAD_DIM


def _split_pair(a):
    low = _low_half(a.shape)
    zero = jnp.zeros_like(a)
    return jnp.concatenate([jnp.where(low, a, zero), jnp.where(low, zero, a)], axis=0)


def _kv_expand(proj, kv_off, name):
    T = proj.shape[0]
    kv_w = N_KV_HEADS * HEAD_DIM
    rows = _tile(T, 512)

    def body(k_ref, v_ref, ke_ref, ve_ref):
        for src, dst in ((k_ref, ke_ref), (v_ref, ve_ref)):
            for g in range(N_KV_HEADS // 2):
                x = src[:, g * PAIR:(g + 1) * PAIR].astype(F32)
                swapped = pltpu.roll(x, HEAD_DIM, 1)
                low = _low_half(x.shape)
                dst[:, 2 * g * PAIR:(2 * g + 1) * PAIR] = jnp.where(low, x, swapped).astype(BF16)
                dst[:, (2 * g + 1) * PAIR:(2 * g + 2) * PAIR] = jnp.where(low, swapped, x).astype(BF16)

    out = jax.ShapeDtypeStruct((T, N_KV_HEADS * PAIR), BF16)
    ospec = pl.BlockSpec((rows, N_KV_HEADS * PAIR), lambda i: (i, 0))
    return pl.pallas_call(
        body, name=name, grid=(T // rows,),
        in_specs=[pl.BlockSpec((rows, kv_w), lambda i: (i, kv_off // kv_w)),
                  pl.BlockSpec((rows, kv_w), lambda i: (i, kv_off // kv_w + 1))],
        out_specs=[ospec, ospec], out_shape=[out, out], compiler_params=_params("parallel"),
    )(proj, proj)


def _kv_fold(dke, dve, dproj, kv_off, name):
    T = dke.shape[0]
    kv_w = N_KV_HEADS * HEAD_DIM
    rows = _tile(T, 512)

    def body(dk_ref, dv_ref, _, o_ref):
        for n, src in enumerate((dk_ref, dv_ref)):
            for g in range(N_KV_HEADS // 2):
                a = src[:, 2 * g * PAIR:(2 * g + 1) * PAIR]
                b = src[:, (2 * g + 1) * PAIR:(2 * g + 2) * PAIR]
                a = a + pltpu.roll(a, HEAD_DIM, 1)
                b = b + pltpu.roll(b, HEAD_DIM, 1)
                o_ref[:, n * kv_w + g * PAIR:n * kv_w + (g + 1) * PAIR] = jnp.where(_low_half(a.shape), a, b).astype(BF16)

    ispec = pl.BlockSpec((rows, N_KV_HEADS * PAIR), lambda i: (i, 0))
    return pl.pallas_call(
        body, name=name, grid=(T // rows,), in_specs=[ispec, ispec, ANY_SPEC],
        out_specs=pl.BlockSpec((rows, 2 * kv_w), lambda i: (i, kv_off // (2 * kv_w))),
        out_shape=jax.ShapeDtypeStruct(dproj.shape, BF16), input_output_aliases={2: 0},
        compiler_params=_params("parallel"),
    )(dke, dve, dproj)


def _key_blocks(i, nb):
    return [pl.multiple_of(n * BLOCK, BLOCK) for n in (jnp.maximum(i - 1, 0), i, jnp.minimum(i + 1, nb - 1))]


def _three_blocks(ref, starts):
    return jnp.concatenate([ref[pl.ds(s, BLOCK), :] for s in starts], axis=0)


def _group_scores(q_ref, rows, kd, bias_ref, i, nb):
    qq = jnp.concatenate([_split_pair(q_ref[rows, pr * PAIR:(pr + 1) * PAIR]) for pr in range(GROUP // 2)], axis=0)
    s = lax.dot_general(qq, kd, _DIMS["nt"], preferred_element_type=F32) * (HEAD_DIM ** -0.5)
    s = s + bias_ref[...].reshape(GROUP * BLOCK, 3 * BLOCK)
    kj = lax.broadcasted_iota(jnp.int32, (1, 3 * BLOCK), 1)
    outside = jnp.logical_or(jnp.logical_and(i == 0, kj < BLOCK), jnp.logical_and(i == nb - 1, kj >= 2 * BLOCK))
    return qq, jnp.where(outside, NEG_INF, s)


def _per_head_rows(values):
    head = lax.broadcasted_iota(jnp.int32, (GROUP * BLOCK, 1), 0) // BLOCK
    out = jnp.full((GROUP * BLOCK, 1), values[0], F32)
    for g in range(1, GROUP):
        out = jnp.where(head == g, values[g], out)
    return out


def _attn_specs(T, q_off):
    gw = GROUP * HEAD_DIM
    return dict(
        sink=pl.BlockSpec(memory_space=pltpu.SMEM),
        q=pl.BlockSpec((Q_BLOCKS * BLOCK, gw), lambda j, i: (i, q_off // gw + j)),
        kv=pl.BlockSpec((T, PAIR), lambda j, i: (0, j)),
        bias=pl.BlockSpec((GROUP, BLOCK, 3 * BLOCK), lambda j, i: (j, 0, 0)),
        o=pl.BlockSpec((Q_BLOCKS * BLOCK, gw), lambda j, i: (i, j)))


def _attn_fwd(proj, q_off, kexp, vexp, bias, sink, name, comm=None):
    T = proj.shape[0]
    nb = T // BLOCK
    sp = _attn_specs(T, q_off)
    steps = N_KV_HEADS * (nb // Q_BLOCKS)
    n_ci = len(comm.ins) if comm is not None else 0
    n_co = len(comm.outs) if comm is not None else 0

    def body(*refs):
        sink_ref, q_ref, ke_ref, ve_ref, bias_ref = refs[:5]
        comm_in = refs[5:5 + n_ci]
        o_ref, lse_ref = refs[5 + n_ci:7 + n_ci]
        comm_out = refs[7 + n_ci:7 + n_ci + n_co]
        sems = refs[7 + n_ci + n_co:]
        j, i = pl.program_id(0), pl.program_id(1)
        step = j * (nb // Q_BLOCKS) + i
        if comm is not None:
            @pl.when(step == 0)
            def _():
                comm.start(comm_in, comm_out, sems)

        low = _low_half((BLOCK, PAIR))
        sk = _per_head_rows([sink_ref[GROUP * j + g] for g in range(GROUP)])
        for b in range(Q_BLOCKS):
            blk = i * Q_BLOCKS + b
            rows = slice(b * BLOCK, (b + 1) * BLOCK)
            starts = _key_blocks(blk, nb)
            kd = _three_blocks(ke_ref, starts)
            vv = _split_pair(_three_blocks(ve_ref, starts))
            _, s = _group_scores(q_ref, rows, kd, bias_ref, blk, nb)
            m = jnp.maximum(jnp.max(s, axis=-1, keepdims=True), sk)
            p = jnp.exp(s - m)
            denom = jnp.sum(p, axis=-1, keepdims=True) + jnp.exp(sk - m)
            p = (p / denom).astype(BF16)
            lse = m + jnp.log(denom)
            for pr in range(GROUP // 2):
                lanes = slice(pr * PAIR, (pr + 1) * PAIR)
                a, c = slice(2 * pr * BLOCK, (2 * pr + 1) * BLOCK), slice((2 * pr + 1) * BLOCK, (2 * pr + 2) * BLOCK)
                pp = jnp.concatenate([p[a], p[c]], axis=1)
                o_ref[rows, lanes] = lax.dot_general(pp, vv, _DIMS["nn"], preferred_element_type=F32).astype(BF16)
                lse_ref[rows, lanes] = jnp.where(low, lse[a], lse[c])

        if comm is not None:
            @pl.when(step == (3 * steps) // 4)
            def _():
                comm.mid(comm_in, comm_out, sems)

            @pl.when(step == steps - 1)
            def _():
                comm.finish(comm_in, comm_out, sems)

    out_shape = [jax.ShapeDtypeStruct((T, N_HEADS * HEAD_DIM), BF16), jax.ShapeDtypeStruct((T, N_HEADS * HEAD_DIM), F32)]
    in_specs = [sp["sink"], sp["q"], sp["kv"], sp["kv"], sp["bias"]]
    if comm is None:
        att, lse = pl.pallas_call(
            body, name=name, grid=(N_KV_HEADS, nb // Q_BLOCKS), in_specs=in_specs, out_specs=[sp["o"], sp["o"]],
            out_shape=out_shape, compiler_params=_params("parallel", "parallel"),
        )(sink, proj, kexp, vexp, bias)
        return att, lse, []
    outs = pl.pallas_call(
        body, name=name, grid=(N_KV_HEADS, nb // Q_BLOCKS),
        in_specs=in_specs + [HBM_SPEC] * n_ci, out_specs=[sp["o"], sp["o"]] + [HBM_SPEC] * n_co,
        out_shape=out_shape + list(comm.outs), scratch_shapes=list(comm.sems),
        compiler_params=_params("arbitrary", "arbitrary"),
    )(sink, proj, kexp, vexp, bias, *comm.ins)
    return outs[0], outs[1], outs[2:]


def _attn_bwd(proj, q_off, kexp, vexp, bias, sink, out, lse, dout, dproj, name, comm=None):
    T = proj.shape[0]
    nb = T // BLOCK
    sp = _attn_specs(T, q_off)
    scale = HEAD_DIM ** -0.5
    steps = N_KV_HEADS * (nb // Q_BLOCKS)
    n_ci = len(comm.ins) if comm is not None else 0
    n_co = len(comm.outs) if comm is not None else 0

    def body(*refs):
        sink_ref, q_ref, ke_ref, ve_ref, bias_ref, o_ref, lse_ref, do_ref = refs[:8]
        comm_in = refs[9:9 + n_ci]
        dq_ref, dke_ref, dve_ref, ds_ref, dsink_ref = refs[9 + n_ci:14 + n_ci]
        comm_out = refs[14 + n_ci:14 + n_ci + n_co]
        sems = refs[14 + n_ci + n_co:]
        j, i = pl.program_id(0), pl.program_id(1)
        step = j * (nb // Q_BLOCKS) + i
        if comm is not None:
            @pl.when(step == 0)
            def _():
                comm.start(comm_in, comm_out, sems)

        @pl.when(i == 0)
        def _():
            dke_ref[...] = jnp.zeros(dke_ref.shape, F32)
            dve_ref[...] = jnp.zeros(dve_ref.shape, F32)
            ds_ref[...] = jnp.zeros(ds_ref.shape, F32)
            dsink_ref[...] = jnp.zeros(dsink_ref.shape, F32)

        low = _low_half((BLOCK, PAIR))
        for b in range(Q_BLOCKS):
            blk = i * Q_BLOCKS + b
            rows = slice(b * BLOCK, (b + 1) * BLOCK)
            starts = _key_blocks(blk, nb)
            kd = _three_blocks(ke_ref, starts)
            vd = _three_blocks(ve_ref, starts)
            kk = _split_pair(kd)
            qq, s = _group_scores(q_ref, rows, kd, bias_ref, blk, nb)
            lse_rows, deltas, dd = [], [], []
            for pr in range(GROUP // 2):
                lanes = slice(pr * PAIR, (pr + 1) * PAIR)
                l2 = lse_ref[rows, lanes]
                l2s = pltpu.roll(l2, HEAD_DIM, 1)
                lse_rows += [jnp.where(low, l2, l2s), jnp.where(low, l2s, l2)]
                do2 = do_ref[rows, lanes]
                prod = do2.astype(F32) * o_ref[rows, lanes].astype(F32)
                deltas += [jnp.sum(jnp.where(low, prod, 0.0), axis=-1, keepdims=True),
                           jnp.sum(jnp.where(low, 0.0, prod), axis=-1, keepdims=True)]
                dd.append(_split_pair(do2))
                head = GROUP * j + 2 * pr
                p_sink = jnp.exp(jnp.where(low, sink_ref[head], sink_ref[head + 1]) - l2)
                dsink_ref[:, lanes] += jnp.sum(-p_sink * jnp.where(low, deltas[-2], deltas[-1]), axis=0, keepdims=True)
            dd = jnp.concatenate(dd, axis=0)
            p = jnp.exp(s - jnp.concatenate([jnp.concatenate([l] * 3, axis=1) for l in lse_rows], axis=0))
            dp = lax.dot_general(dd, vd, _DIMS["nt"], preferred_element_type=F32)
            ds = p * (dp - jnp.concatenate(deltas, axis=0))
            dsb = ds.astype(BF16)
            for pr in range(GROUP // 2):
                a, c = slice(2 * pr * BLOCK, (2 * pr + 1) * BLOCK), slice((2 * pr + 1) * BLOCK, (2 * pr + 2) * BLOCK)
                dq = lax.dot_general(jnp.concatenate([dsb[a], dsb[c]], axis=1), kk, _DIMS["nn"],
                                     preferred_element_type=F32) * scale
                dq_ref[rows, pr * PAIR:(pr + 1) * PAIR] = dq.astype(BF16)
            dk_acc = lax.dot_general(dsb, qq, _DIMS["tn"], preferred_element_type=F32) * scale
            dv_acc = lax.dot_general(p.astype(BF16), dd, _DIMS["tn"], preferred_element_type=F32)
            ds_ref[...] += ds.reshape(GROUP, BLOCK, 3 * BLOCK)
            for t, start in enumerate(starts):
                dke_ref[pl.ds(start, BLOCK), :] += dk_acc[t * BLOCK:(t + 1) * BLOCK]
                dve_ref[pl.ds(start, BLOCK), :] += dv_acc[t * BLOCK:(t + 1) * BLOCK]

        if comm is not None:
            @pl.when(step == (3 * steps) // 4)
            def _():
                comm.mid(comm_in, comm_out, sems)

            @pl.when(step == steps - 1)
            def _():
                comm.finish(comm_in, comm_out, sems)

    kv_out = jax.ShapeDtypeStruct((T, N_KV_HEADS * PAIR), F32)
    job_ins, job_outs, job_sems = (comm.ins, comm.outs, comm.sems) if comm is not None else ([], [], [])
    outs = pl.pallas_call(
        body, name=name, grid=(N_KV_HEADS, nb // Q_BLOCKS),
        in_specs=[sp["sink"], sp["q"], sp["kv"], sp["kv"], sp["bias"], sp["o"], sp["o"], sp["o"], ANY_SPEC]
        + [HBM_SPEC] * n_ci,
        out_specs=[sp["q"], sp["kv"], sp["kv"], sp["bias"],
                   pl.BlockSpec((1, GROUP * HEAD_DIM), lambda j, i: (0, j))] + [HBM_SPEC] * n_co,
        out_shape=[jax.ShapeDtypeStruct(dproj.shape, BF16), kv_out, kv_out,
                   jax.ShapeDtypeStruct((N_HEADS, BLOCK, 3 * BLOCK), F32),
                   jax.ShapeDtypeStruct((1, N_HEADS * HEAD_DIM), F32)] + list(job_outs),
        scratch_shapes=list(job_sems), input_output_aliases={8: 0},
        compiler_params=_params("arbitrary" if comm is not None else "parallel", "arbitrary"),
    )(sink, proj, kexp, vexp, bias, out, lse, dout, dproj, *job_ins)
    return outs[:5], outs[5:]


GATE_COLS = 512


def _sigmoid(x):
    return 1.0 / (1.0 + jnp.exp(-x))


def _gate_specs(D, gate_off):
    nc = D // GATE_COLS
    base = gate_off // GATE_COLS
    return [pl.BlockSpec((ROWS, GATE_COLS), lambda i, c=base + g * nc + h: (i, c)) for g in range(3) for h in range(nc)]


def _merge_fwd(proj, gate_off, ya, yp, yt, scale, name):
    T, D = ya.shape
    nc = D // GATE_COLS

    def body(*refs):
        gates = refs[:3 * nc]
        ya_ref, yp_ref, yt_ref, s_ref, o_ref = refs[3 * nc:]
        for h in range(nc):
            cols = slice(h * GATE_COLS, (h + 1) * GATE_COLS)
            merged = (_sigmoid(_f32(gates[h])) * ya_ref[:, cols].astype(F32)
                      + _sigmoid(_f32(gates[nc + h])) * (yp_ref[:, cols].astype(F32) * s_ref[:, cols])
                      + _sigmoid(_f32(gates[2 * nc + h])) * yt_ref[:, cols].astype(F32))
            o_ref[:, cols] = merged.astype(BF16)

    yspec = _row_spec(D)
    return pl.pallas_call(
        body, name=name, grid=(T // ROWS,),
        in_specs=_gate_specs(D, gate_off) + [yspec, yspec, yspec, _const_spec((1, D))], out_specs=yspec,
        out_shape=jax.ShapeDtypeStruct((T, D), BF16), compiler_params=_params("parallel"),
    )(*([proj] * (3 * nc)), ya, yp, yt, scale)


def _merge_bwd(proj, gate_off, ya, yp, yt, scale, dm, name):
    T, D = ya.shape
    nc = D // GATE_COLS
    base = gate_off // GATE_COLS

    def body(gate_ref, ya_ref, yp_ref, yt_ref, s_ref, dm_ref, dg_ref, dya_ref, dyp_ref, dyt_ref, ds_ref):
        i, n = pl.program_id(0), pl.program_id(1)
        sg = _sigmoid(_f32(gate_ref))
        for g, (y_ref, dy_ref) in enumerate(((ya_ref, dya_ref), (yp_ref, dyp_ref), (yt_ref, dyt_ref))):
            for h in range(nc):
                @pl.when(n == g * nc + h)
                def _(g=g, h=h, y_ref=y_ref, dy_ref=dy_ref):
                    cols = slice(h * GATE_COLS, (h + 1) * GATE_COLS)
                    dy = dm_ref[:, cols].astype(F32) * sg
                    y = y_ref[:, cols].astype(F32)
                    if g == 1:
                        s_v = s_ref[:, cols]
                        part = jnp.sum(dy * y, axis=0, keepdims=True)

                        @pl.when(i == 0)
                        def _():
                            ds_ref[:, cols] = part

                        @pl.when(i > 0)
                        def _():
                            ds_ref[:, cols] += part

                        y = y * s_v
                        dy_ref[:, cols] = (dy * s_v).astype(BF16)
                    else:
                        dy_ref[:, cols] = dy.astype(BF16)
                    dg_ref[...] = (dy * y * (1.0 - sg)).astype(BF16)

    rows = _tile(T, 4 * ROWS)
    yspec = pl.BlockSpec((rows, D), lambda i, n: (i, 0))
    gspec = pl.BlockSpec((rows, GATE_COLS), lambda i, n: (i, base + n))
    sspec = pl.BlockSpec((1, D), lambda i, n: (0, 0))
    out = jax.ShapeDtypeStruct((T, D), BF16)
    return pl.pallas_call(
        body, name=name, grid=(T // rows, 3 * nc),
        in_specs=[gspec, yspec, yspec, yspec, sspec, yspec],
        out_specs=[gspec, yspec, yspec, yspec, sspec],
        out_shape=[jax.ShapeDtypeStruct(proj.shape, BF16), out, out, out, jax.ShapeDtypeStruct((1, D), F32)],
        compiler_params=_params("arbitrary", "arbitrary"),
    )(proj, ya, yp, yt, scale, dm)


def _swiglu_fwd(gu, name):
    T = gu.shape[0]
    F = gu.shape[1] // 2

    def body(gu_ref, o_ref):
        g = gu_ref[:, 0:F].astype(F32)
        o_ref[...] = (g * _sigmoid(g) * gu_ref[:, F:2 * F].astype(F32)).astype(BF16)

    return pl.pallas_call(
        body, name=name, grid=(T // ROWS,), in_specs=[_row_spec(2 * F)], out_specs=_row_spec(F),
        out_shape=jax.ShapeDtypeStruct((T, F), BF16), compiler_params=_params("parallel"),
    )(gu)


def _swiglu_bwd(gu, dact, name):
    T = gu.shape[0]
    F = gu.shape[1] // 2

    def body(gu_ref, d_ref, o_ref):
        g, d = gu_ref[:, 0:F].astype(F32), d_ref[...].astype(F32)
        sg = _sigmoid(g)
        o_ref[:, 0:F] = (d * gu_ref[:, F:2 * F].astype(F32) * sg * (1.0 + g * (1.0 - sg))).astype(BF16)
        o_ref[:, F:2 * F] = (d * g * sg).astype(BF16)

    return pl.pallas_call(
        body, name=name, grid=(T // ROWS,), in_specs=[_row_spec(2 * F), _row_spec(F)], out_specs=_row_spec(2 * F),
        out_shape=jax.ShapeDtypeStruct((T, 2 * F), BF16), compiler_params=_params("parallel"),
    )(gu, dact)


def _carried(plan, key, *args, **kwargs):
    job = plan.job(key) if plan is not None else None
    if job is None:
        return _matmul(*args, **kwargs)
    out, extra = _matmul(*args, comm=job, **kwargs)
    plan.done(key, extra)
    return out


def _local_step(x, target, wts, small, hooks=None):
    T, D = x.shape
    depth = small["g_mix"].shape[0]
    wts = list(wts) + [None] * (depth - len(wts))
    gate_off = wts[0]["w_inT"].shape[0] - 3 * D
    q_off = 4 * D
    bucket = _bucket_table()
    bias = _bias_table(small["rel_bias"], bucket, "bias_table")

    saved = []
    for l in range(depth):
        n = f"l{l}_"
        if hooks is not None and l > 0:
            wts[l] = hooks.weights(l)
        w = wts[l]
        plan = hooks.plan_fwd(l) if hooks is not None else None
        h = _rms_fwd(x, small["g_mix"][l], n + "rms_mix")
        proj = _carried(plan, "proj", h, w["w_inT"], "nt", BF16, n + "proj", tn_cap=WIDE_TILE)
        z, p = _mixer_fwd(proj, small["conv_w"][l], n + "mixer")
        kexp, vexp = _kv_expand(proj, q_off + D, n + "kv_expand")
        sink = small["attn_sink"][l]
        job = plan.job("attn") if plan is not None else None
        att, lse, extra = _attn_fwd(proj, q_off, kexp, vexp, bias, sink, n + "attn", comm=job)
        if job is not None:
            plan.done("attn", extra)
        ya =_matmul(z, w["w_a_out"], "nn", BF16, n + "ya")
        yp = _pool_mm(p, w["w_pool"], "nn", BF16, n + "yp")
        yt = _matmul(att, w["w_attn_out"], "nn", BF16, n + "yt")
        merged = _merge_fwd(proj, gate_off, ya, yp, yt, small["pool_scale"][l], n + "merge")
        x1 = _matmul(merged, w["w_o"], "nn", F32, n + "x1", res=x)
        h2 = _rms_fwd(x1, small["g_ffn"][l], n + "rms_ffn")
        gu = _carried(plan, "gu", h2, w["w_guT"], "nt", BF16, n + "gu", tn_cap=WIDE_TILE)
        act = _swiglu_fwd(gu, n + "swiglu")
        ff = w["w_down"].shape[0]
        x2 = _carried(plan, "x2", act, w["w_down"], "nn", F32, n + "x2", res=x1, tn_cap=512, tk_cap=ff)
        saved.append(dict(x=x, h=h, proj=proj, z=z, p=p, kexp=kexp, vexp=vexp, sink=sink, lse=lse, att=att,
                          ya=ya, yp=yp, yt=yt, merged=merged, x1=x1, h2=h2, gu=gu, act=act))
        x = x2

    loss, dx, dxb, dg_final = _loss_head(x, small["g_final"], target, "loss_head")

    gw = [None] * depth
    gs = {k_: [None] * depth for k_ in ("conv_w", "pool_scale", "g_mix", "g_ffn", "attn_sink")}
    ds_total = None
    for l in reversed(range(depth)):
        n = f"l{l}_b_"
        s, w, g = saved[l], wts[l], {}
        plan = hooks.plan_bwd(l) if hooks is not None else None
        ff = w["w_down"].shape[0]
        g["w_down"] = _carried(plan, "dw_down", s["act"], dxb, "tn", BF16, n + "dw_down", tm_cap=ff, tk_cap=512)
        dact = _matmul(dxb, w["w_down"], "nt", BF16, n + "dact", tm_cap=512, tn_cap=ff)
        dgu = _swiglu_bwd(s["gu"], dact, n + "swiglu")
        g["w_guT"] = _carried(plan, "dw_gu", dgu, s["h2"], "tn", BF16, n + "dw_gu", tm_cap=WIDE_TILE)
        dh2 = _carried(plan, "dh2", dgu, w["w_guT"], "nn", F32, n + "dh2", tn_cap=512, tk_cap=ff)
        dx1, dx1b, gs["g_ffn"][l] = _rms_bwd(s["x1"], small["g_ffn"][l], dh2, dx, n + "rms_ffn")
        g["w_o"] = _matmul(s["merged"], dx1b, "tn", BF16, n + "dw_o")
        dm = _matmul(dx1b, w["w_o"], "nt", BF16, n + "dmerged")
        dproj, dya, dyp, dyt, gs["pool_scale"][l] = _merge_bwd(
            s["proj"], gate_off, s["ya"], s["yp"], s["yt"], small["pool_scale"][l], dm, n + "merge")
        g["w_a_out"] = _matmul(s["z"], dya, "tn", BF16, n + "dw_a_out")
        dz = _matmul(dya, w["w_a_out"], "nt", BF16, n + "dz")
        g["w_pool"] = _pool_mm(s["p"], dyp, "tn", F32, n + "dw_pool")
        dp = _pool_mm(dyp, w["w_pool"], "nt", BF16, n + "dp")
        g["w_attn_out"] = _matmul(s["att"], dyt, "tn", BF16, n + "dw_attn_out")
        if hooks is not None:
            hooks.early_grads(l, g)
        datt = _carried(plan, "datt", dyt, w["w_attn_out"], "nt", BF16, n + "datt")
        dproj, gs["conv_w"][l] = _mixer_bwd(s["proj"], small["conv_w"][l], dz, dp, dproj, n + "mixer")
        job = plan.job("attn_b") if plan is not None else None
        (dproj, dke, dve, ds_sum, dsink), extra = _attn_bwd(
            s["proj"], q_off, s["kexp"], s["vexp"], bias, s["sink"], s["att"], s["lse"], datt, dproj, n + "attn",
            comm=job)
        if job is not None:
            plan.done("attn_b", extra)
        gs["attn_sink"][l] = dsink.reshape(N_HEADS, HEAD_DIM)[:, 0]
        ds_total = ds_sum if ds_total is None else ds_total + ds_sum
        dproj = _kv_fold(dke, dve, dproj, q_off + D, n + "kv_fold")
        g["w_inT"] = _carried(plan, "dw_in", dproj, s["h"], "tn", BF16, n + "dw_in", tm_cap=WIDE_TILE)
        dh = _carried(plan, "dh", dproj, w["w_inT"], "nn", F32, n + "dh", tk_cap=2816)
        dx, dxb, gs["g_mix"][l] = _rms_bwd(s["x"], small["g_mix"][l], dh, dx1, n + "rms_mix")
        gw[l] = g
        if hooks is not None:
            hooks.grads(l, g)

    d_rel =_bias_grad(ds_total, bucket, "bias_grad")[:, :, 0].T
    gs = {k_: jnp.stack(v_) for k_, v_ in gs.items()}
    gs["rel_bias"] = d_rel
    gs["g_final"] = dg_final
    return loss, dx, gw, gs


def _place():
    return lax.axis_index("x"), lax.axis_index("y"), lax.axis_index("c")


class _GatherJob:
    def __init__(self, parts):
        n = len(parts)
        self.n = n
        self.ins = list(parts)
        self.outs = [jax.ShapeDtypeStruct((N_DEV,) + p.shape, p.dtype) for p in parts]
        self.sems = [pltpu.SemaphoreType.DMA((7 * n,)), pltpu.SemaphoreType.DMA((7 * n,)), pltpu.SemaphoreType.DMA((n,))]

    def _copies(self, ins, outs, sems):
        send_sems, recv_sems, local_sems = sems
        x, y, c = _place()
        me, sibling = (x, y, c), (x, y, 1 - c)
        chips = [(1 - x, y), (x, 1 - y), (1 - x, 1 - y)]

        def rows(t, px, py, pc):
            return outs[t].at[4 * px + 2 * py + pc]

        def copy(t, k, block, to, src=None):
            return pltpu.make_async_remote_copy(
                src_ref=rows(t, *block) if src is None else src, dst_ref=rows(t, *block),
                send_sem=send_sems.at[7 * t + k], recv_sem=recv_sems.at[7 * t + k], device_id=to, device_id_type=MESH)

        ts = range(self.n)
        own = [pltpu.make_async_copy(ins[t], rows(t, *me), local_sems.at[t]) for t in ts]
        first = [copy(t, 0, me, sibling, src=ins[t]) for t in ts]
        first += [copy(t, 1 + j, me, (*chip, c), src=ins[t]) for t in ts for j, chip in enumerate(chips)]
        landed = [copy(t, 1 + j, (*chip, c), me) for j, chip in enumerate(chips) for t in ts]
        passed = [copy(t, 4 + j, (*chip, c), sibling) for j, chip in enumerate(chips) for t in ts]
        last = [copy(t, 0, sibling, me) for t in ts]
        last += [copy(t, 4 + j, (*chip, 1 - c), me) for t in ts for j, chip in enumerate(chips)]
        return own, first, landed, passed, last

    def start(self, ins, outs, sems):
        own, first, _, _, _ = self._copies(ins, outs, sems)
        for cp in own + first:
            cp.start()

    def mid(self, ins, outs, sems):
        _, _, landed, passed, _ = self._copies(ins, outs, sems)
        for arrived, onward in zip(landed, passed):
            arrived.wait_recv()
            onward.start()

    def finish(self, ins, outs, sems):
        own, first, _, passed, last = self._copies(ins, outs, sems)
        for cp in last:
            cp.wait_recv()
        for cp in first + passed:
            cp.wait_send()
        for cp in own:
            cp.wait()


class _SwapJob:
    def __init__(self, g):
        self.ins = [g]
        self.outs = [jax.ShapeDtypeStruct(g.shape[:1] + g.shape[2:], g.dtype)]
        self.sems = [pltpu.SemaphoreType.DMA, pltpu.SemaphoreType.DMA]

    def _copy(self, ins, outs, sems):
        x, y, c = _place()
        return pltpu.make_async_remote_copy(src_ref=ins[0].at[pl.ds(0, ins[0].shape[0]), 1 - c], dst_ref=outs[0],
                                            send_sem=sems[0], recv_sem=sems[1], device_id=(x, y, 1 - c),
                                            device_id_type=MESH)

    def start(self, ins, outs, sems):
        self._copy(ins, outs, sems).start()

    def mid(self, ins, outs, sems):
        pass

    def finish(self, ins, outs, sems):
        self._copy(ins, outs, sems).wait()


class _ExchangeJob:
    def __init__(self, p, row0, rows):
        self.row0, self.rows = row0, rows
        self.ins = [p]
        self.outs = [jax.ShapeDtypeStruct((3, rows) + p.shape[2:], p.dtype)]
        self.sems = [pltpu.SemaphoreType.DMA((3,)), pltpu.SemaphoreType.DMA((3,))]

    def _copies(self, ins, outs, sems):
        x, y, c = _place()
        chips = [(1 - x, y), (x, 1 - y), (1 - x, 1 - y)]
        return [pltpu.make_async_remote_copy(
            src_ref=ins[0].at[2 * px + py, pl.ds(self.row0, self.rows)], dst_ref=outs[0].at[k],
            send_sem=sems[0].at[k], recv_sem=sems[1].at[k], device_id=(px, py, c), device_id_type=MESH)
            for k, (px, py) in enumerate(chips)]

    def start(self, ins, outs, sems):
        for cp in self._copies(ins, outs, sems):
            cp.start()

    def mid(self, ins, outs, sems):
        pass

    def finish(self, ins, outs, sems):
        for cp in self._copies(ins, outs, sems):
            cp.wait()


def _all_gather(v, name):
    def body(x_ref, out_ref, send_sems, recv_sems, local_sem):
        x, y, c = _place()
        me, sibling = (x, y, c), (x, y, 1 - c)
        chips = [(1 - x, y), (x, 1 - y), (1 - x, 1 - y)]

        def rows(px, py, pc):
            return out_ref.at[4 * px + 2 * py + pc]

        def copy(k, block, to, src=None):
            return pltpu.make_async_remote_copy(
                src_ref=rows(*block) if src is None else src, dst_ref=rows(*block),
                send_sem=send_sems.at[k], recv_sem=recv_sems.at[k], device_id=to, device_id_type=MESH)

        mine = pltpu.make_async_copy(x_ref, rows(*me), local_sem)
        mine.start()
        first = [copy(0, me, sibling, src=x_ref)]
        first += [copy(1 + j, me, (*chip, c), src=x_ref) for j, chip in enumerate(chips)]
        for cp in first:
            cp.start()
        passed = [copy(4 + j, (*chip, c), sibling) for j, chip in enumerate(chips)]
        for j, chip in enumerate(chips):
            copy(1 + j, (*chip, c), me).wait_recv()
            passed[j].start()
        copy(0, sibling, me).wait_recv()
        for j, chip in enumerate(chips):
            copy(4 + j, (*chip, 1 - c), me).wait_recv()
        for cp in first + passed:
            cp.wait_send()
        mine.wait()

    return pl.pallas_call(
        body, name=name, in_specs=[HBM_SPEC], out_specs=HBM_SPEC,
        out_shape=jax.ShapeDtypeStruct((N_DEV,) + v.shape, v.dtype),
        scratch_shapes=[pltpu.SemaphoreType.DMA((7,)), pltpu.SemaphoreType.DMA((7,)), pltpu.SemaphoreType.DMA],
    )(v)


def _all_gather_many(parts, name):
    n = len(parts)

    def body(*refs):
        ins, outs = refs[:n], refs[n:2 * n]
        send_sems, recv_sems, local_sems = refs[2 * n:]
        x, y, c = _place()
        me, sibling = (x, y, c), (x, y, 1 - c)
        chips = [(1 - x, y), (x, 1 - y), (1 - x, 1 - y)]

        def rows(t, px, py, pc):
            return outs[t].at[4 * px + 2 * py + pc]

        def copy(t, k, block, to, src=None):
            return pltpu.make_async_remote_copy(
                src_ref=rows(t, *block) if src is None else src, dst_ref=rows(t, *block),
                send_sem=send_sems.at[7 * t + k], recv_sem=recv_sems.at[7 * t + k], device_id=to, device_id_type=MESH)

        mine = [pltpu.make_async_copy(ins[t], rows(t, *me), local_sems.at[t]) for t in range(n)]
        sends = []
        for t in range(n):
            mine[t].start()
            sends.append(copy(t, 0, me, sibling, src=ins[t]))
            sends += [copy(t, 1 + j, me, (*chip, c), src=ins[t]) for j, chip in enumerate(chips)]
        for cp in sends:
            cp.start()
        for j, chip in enumerate(chips):
            for t in range(n):
                copy(t, 1 + j, (*chip, c), me).wait_recv()
                passed = copy(t, 4 + j, (*chip, c), sibling)
                passed.start()
                sends.append(passed)
        for t in range(n):
            copy(t, 0, sibling, me).wait_recv()
            for j, chip in enumerate(chips):
                copy(t, 4 + j, (*chip, 1 - c), me).wait_recv()
        for cp in sends:
            cp.wait_send()
        for cp in mine:
            cp.wait()

    return pl.pallas_call(
        body, name=name, in_specs=[HBM_SPEC] * n, out_specs=[HBM_SPEC] * n,
        out_shape=[jax.ShapeDtypeStruct((N_DEV,) + p.shape, p.dtype) for p in parts],
        scratch_shapes=[pltpu.SemaphoreType.DMA((7 * n,)), pltpu.SemaphoreType.DMA((7 * n,)),
                        pltpu.SemaphoreType.DMA((n,))],
    )(*parts)


def _run_job(job, name):
    n_in, n_out = len(job.ins), len(job.outs)

    def body(*refs):
        ins, outs, sems = refs[:n_in], refs[n_in:n_in + n_out], refs[n_in + n_out:]
        job.start(ins, outs, sems)
        job.mid(ins, outs, sems)
        job.finish(ins, outs, sems)

    return pl.pallas_call(
        body, name=name, in_specs=[HBM_SPEC] * n_in, out_specs=[HBM_SPEC] * n_out, out_shape=list(job.outs),
        scratch_shapes=list(job.sems),
    )(*job.ins)


def _chip_exchange(p, name):
    def body(p_ref, out_ref, send_sems, recv_sems):
        x, y, c = _place()
        chips = [(1 - x, y), (x, 1 - y), (1 - x, 1 - y)]
        copies = [pltpu.make_async_remote_copy(
            src_ref=p_ref.at[2 * px + py], dst_ref=out_ref.at[k], send_sem=send_sems.at[k], recv_sem=recv_sems.at[k],
            device_id=(px, py, c), device_id_type=MESH) for k, (px, py) in enumerate(chips)]
        for cp in copies:
            cp.start()
        for cp in copies:
            cp.wait()

    return pl.pallas_call(
        body, name=name, in_specs=[HBM_SPEC], out_specs=HBM_SPEC,
        out_shape=jax.ShapeDtypeStruct((3,) + p.shape[1:], p.dtype),
        scratch_shapes=[pltpu.SemaphoreType.DMA((3,)), pltpu.SemaphoreType.DMA((3,))],
    )(p)


SUM_ROWS_CAP = 576


def _sum_parts(own, index, others, out_dtype, name, own_row0=0, own_step=0):
    R = others.shape[1]
    common = math.gcd(R, own.shape[1], own_row0 or R)
    rows = next(t for t in range(min(common, SUM_ROWS_CAP) // 16 * 16, 0, -16) if common % t == 0)
    k = others.shape[0]
    assert own_row0 % rows == 0 and own.shape[1] % rows == 0
    blk0 = own_row0 // rows
    per_own = own.shape[1] // rows

    def own_block(i, idx):
        if own_step:
            return (idx[0] + own_step * (i // per_own), i % per_own, 0)
        return (idx[0], blk0 + i, 0)

    def body(idx_ref, own_ref, *refs):
        del idx_ref
        acc = own_ref[...].astype(F32)
        for r in refs[:k]:
            acc = acc + r[...].astype(F32)
        refs[k][...] = acc.astype(out_dtype)

    grid_spec = pltpu.PrefetchScalarGridSpec(
        num_scalar_prefetch=1, grid=(R // rows,),
        in_specs=[pl.BlockSpec((None, rows, LANES), own_block)]
        + [pl.BlockSpec((None, rows, LANES), lambda i, idx, j=j: (j, i, 0)) for j in range(k)],
        out_specs=pl.BlockSpec((rows, LANES), lambda i, idx: (i, 0)))
    return pl.pallas_call(
        body, name=name, grid_spec=grid_spec,
        out_shape=jax.ShapeDtypeStruct((R, LANES), out_dtype), compiler_params=_params("parallel"),
    )(jnp.reshape(index, (1,)).astype(jnp.int32), own, *([others] * k))


def _adamw(w, g, m, v, name):
    shape = w.shape
    cols = shape[-1]
    rows_total = w.size // cols
    w2, g2, m2, v2 = (a.reshape(rows_total, cols) for a in (w, g, m, v))
    rows = rows_total
    if rows_total > ROWS:
        rows = next(r for r in range(ROWS, 0, -8) if rows_total % r == 0)

    def body(w_ref, g_ref, m_ref, v_ref, d_ref, nm_ref, nv_ref):
        gv = g_ref[...]
        nm = ADAM_B1 * m_ref[...] + (1.0 - ADAM_B1) * gv
        nv = ADAM_B2 * v_ref[...] + (1.0 - ADAM_B2) * (gv * gv)
        m_hat = nm / (1.0 - ADAM_B1 ** ADAM_STEP)
        v_hat = nv / (1.0 - ADAM_B2 ** ADAM_STEP)
        d_ref[...] = -ADAM_LR * (m_hat / (jnp.sqrt(v_hat) + ADAM_EPS) + ADAM_WD * w_ref[...])
        nm_ref[...] = nm
        nv_ref[...] = nv

    spec = pl.BlockSpec((rows, cols), lambda i: (i, 0))
    out = jax.ShapeDtypeStruct((rows_total, cols), F32)
    d, nm, nv = pl.pallas_call(
        body, name=name, grid=(rows_total // rows,), in_specs=[spec] * 4, out_specs=[spec] * 3,
        out_shape=[out, out, out], compiler_params=_params("parallel"),
    )(w2, g2, m2, v2)
    return d.reshape(shape), nm.reshape(shape), nv.reshape(shape)


BIG = ("w_in", "w_a_out", "w_pool", "w_attn_out", "w_o", "w_gu", "w_down")


LOCAL = dict(w_in="w_inT", w_a_out="w_a_out", w_pool="w_pool", w_attn_out="w_attn_out", w_o="w_o", w_gu="w_guT",
             w_down="w_down")


def _shard_rows(w, l):
    out = []
    for name in BIG:
        a = w[name][l]
        if name in ("w_in", "w_gu"):
            a = a.T
        elif name == "w_pool":
            a = a.reshape(-1, a.shape[-1])
        out.append(a.astype(BF16))
    return out


def _full_weights(names, gathered, w):
    out = {}
    for name, g in zip(names, gathered):
        if name == "w_pool":
            G, rg, cg = w[name].shape[1:]
            out[name] = jnp.transpose(g.reshape(N_DEV, G, rg, cg), (1, 0, 2, 3)).reshape(G, N_DEV * rg, cg)
        else:
            out[LOCAL[name]] = g.reshape(N_DEV * g.shape[1], g.shape[2])
    return out


def _split_grads(g, w, names=BIG, multiple=1):
    parts, spans, at = [], {}, 0
    for name in names:
        a = g[LOCAL[name]].astype(BF16)
        if name == "w_pool":
            G, rg, cg = w[name].shape[1:]
            a = jnp.transpose(a.reshape(G, N_DEV, rg, cg), (1, 0, 2, 3))
        a = a.reshape(N_DEV, -1, LANES)
        spans[name] = (at, at + a.shape[1])
        at += a.shape[1]
        parts.append(a)
    if at % multiple:
        parts.append(jnp.zeros((N_DEV, multiple - at % multiple, LANES), BF16))
    return jnp.concatenate(parts, axis=1), spans


def _own_grads(pieces, w):
    out = {}
    for name in BIG:
        per_layer = []
        for layer in pieces:
            packed, spans = next((p, s) for p, s in layer if name in s)
            per_layer.append(packed[spans[name][0]:spans[name][1]])
        a = jnp.stack(per_layer)
        if name in ("w_in", "w_gu"):
            sh = w[name].shape
            a = jnp.swapaxes(a.reshape(sh[0], sh[2], sh[1]), 1, 2)
        out[name] = a.reshape(w[name].shape)
    return out


SQUARE = ("w_a_out", "w_pool", "w_attn_out", "w_o")


class _Prefetch:
    def __init__(self, schedule, assign):
        self.schedule, self.assign = schedule, assign

    def job(self, key):
        if key not in self.assign:
            return None
        parts = []
        for layer, names in self.assign[key]:
            shards = dict(zip(BIG, _shard_rows(self.schedule.w, layer)))
            parts += [shards[name] for name in names]
        return _GatherJob(parts)

    def done(self, key, outs):
        for layer, names in self.assign[key]:
            self.schedule.arrived(layer, names, outs[:len(names)])
            outs = outs[len(names):]


class _Reduce:
    def __init__(self, split, spans, place, tag, swap_key="dw_down", carriers=("dw_gu", "dh2", "dw_in", "dh"),
                 chunk_rows=(640, 640, 768, 512)):
        self.split, self.spans, self.tag = split, spans, tag
        self.swap_key, self.carriers = swap_key, carriers
        self.core, self.chip = place[2], 2 * place[0] + place[1]
        self.rows = split.shape[1]
        self.chunks, at = [], 0
        for rows in chunk_rows:
            rows = min(rows, self.rows - at) if len(self.chunks) + 1 < len(chunk_rows) else self.rows - at
            if rows > 0:
                self.chunks.append((at, rows))
                at += rows
        assert at == self.rows and len(self.chunks) <= len(carriers)
        self.sums = [None] * len(self.chunks)

    def _swap_job(self):
        return _SwapJob(self.split.reshape(4, 2, self.rows, LANES))

    def _pair_sum(self, from_sibling):
        pair = _sum_parts(self.split, self.core, from_sibling.reshape(1, 4 * self.rows, LANES), BF16,
                          self.tag + "pair_sum", own_step=2)
        self.pair = pair.reshape(4, self.rows, LANES)

    def _chip_sum(self, n, from_chips):
        self.sums[n] = _sum_parts(self.pair, self.chip, from_chips, F32, f"{self.tag}chip_sum{n}",
                                  own_row0=self.chunks[n][0])

    def job(self, key):
        if key == self.swap_key:
            return self._swap_job()
        if key in self.carriers[:len(self.chunks)]:
            return _ExchangeJob(self.pair, *self.chunks[self.carriers.index(key)])
        return None

    def done(self, key, outs):
        if key == self.swap_key:
            self._pair_sum(outs[0])
        else:
            self._chip_sum(self.carriers.index(key), outs[0])

    def run(self):
        self._pair_sum(_run_job(self._swap_job(), self.tag + "reduce_pair")[0])
        self.chunks, self.sums = [(0, self.rows)], [None]
        self._chip_sum(0, _run_job(_ExchangeJob(self.pair, 0, self.rows), self.tag + "reduce_chips")[0])
        return self.result()

    def result(self):
        return (self.sums[0] if len(self.sums) == 1 else jnp.concatenate(self.sums, axis=0)), self.spans


class _Plans:
    def __init__(self, plans):
        self.plans = plans

    def job(self, key):
        self.owner = next((p for p in self.plans if p.job(key) is not None), None)
        return self.owner.job(key) if self.owner is not None else None

    def done(self, key, outs):
        self.owner.done(key, outs)


class _Schedule:
    EARLY = ("w_gu", "w_down") + SQUARE
    EARLY_PAD = 512

    def __init__(self, w, place):
        self.w, self.place = w, place
        self.depth = w["w_in"].shape[0]
        self.full = [{} for _ in range(self.depth)]
        self.reduce = {}
        self.pieces = [None] * self.depth
        self.active = []

    def arrived(self, layer, names, gathered):
        self.full[layer].update(_full_weights(names, gathered, self.w))

    def plan_fwd(self, l):
        nxt = l + 1
        more = nxt < self.depth
        if l == 0:
            assign = dict(proj=[(0, ("w_gu",) + SQUARE)], attn=[(0, ("w_down",))])
            if more:
                assign["attn"].append((nxt, ("w_in",)))
                assign.update(gu=[(nxt, ("w_gu",))], x2=[(nxt, ("w_down",))])
        else:
            assign = dict(proj=[(l, SQUARE)])
            if more:
                assign["proj"].append((nxt, ("w_down",)))
                assign.update(attn=[(nxt, ("w_in",))], gu=[(nxt, ("w_gu",))])
        return _Prefetch(self, assign)

    def weights(self, l):
        return self.full[l]

    def plan_bwd(self, l):
        self.active = [self.reduce[l + 1]] if l + 1 in self.reduce else []
        return _Plans(self.active)

    def early_grads(self, l, g):
        if l == 0:
            split, spans = _split_grads(g, self.w, self.EARLY, self.EARLY_PAD)
            self.early = _Reduce(split, spans, self.place, "l0_early_", swap_key="datt", carriers=("attn_b",),
                                 chunk_rows=(split.shape[1],))
            self.active.append(self.early)

    def grads(self, l, g):
        if l + 1 in self.reduce:
            self.pieces[l + 1] = [self.reduce[l + 1].result()]
        if l == 0:
            last = _Reduce(*_split_grads(g, self.w, ("w_in",)), self.place, "l0_")
            self.pieces[0] = [self.early.result(), last.run()]
        else:
            self.reduce[l] = _Reduce(*_split_grads(g, self.w), self.place, f"l{l}_")


SMALL_ROWS = 32


def _pack_small(gs, L, D):
    rows = [gs["pool_scale"].reshape(L, D), gs["g_mix"].reshape(L, D), gs["g_ffn"].reshape(L, D),
            gs["g_final"].reshape(1, D), gs["conv_w"][:, :3].reshape(3 * L, D),
            jnp.pad(gs["attn_sink"].reshape(1, -1), ((0, 0), (0, D - L * N_HEADS))),
            jnp.pad(gs["rel_bias"].reshape(1, -1), ((0, 0), (0, D - N_BUCKETS * N_HEADS)))]
    a = jnp.concatenate(rows, axis=0)
    return jnp.pad(a, ((0, SMALL_ROWS - a.shape[0]), (0, 0)))


def _unpack_small(a, L, D):
    g = {}
    g["pool_scale"] = a[0:L]
    g["g_mix"] = a[L:2 * L]
    g["g_ffn"] = a[2 * L:3 * L]
    g["g_final"] = a[3 * L]
    g["conv_w"] = a[3 * L + 1:6 * L + 1].reshape(L, 3, 1, D)
    g["attn_sink"] = a[6 * L + 1, :L * N_HEADS].reshape(L, N_HEADS)
    g["rel_bias"] = a[6 * L + 2, :N_BUCKETS * N_HEADS].reshape(N_BUCKETS, N_HEADS)
    return g


WEIGHTS = ("w_in", "conv_w", "w_a_out", "w_pool", "pool_scale", "w_attn_out", "attn_sink", "w_o", "g_mix", "g_ffn",
           "w_gu", "w_down", "rel_bias", "g_final")


def kernel(x, w_in, conv_w, w_a_out, w_pool, pool_scale, w_attn_out, attn_sink, w_o, g_mix, g_ffn, w_gu, w_down, rel_bias, g_final, loss_target, m_w_in, m_conv_w, m_w_a_out, m_w_pool, m_pool_scale, m_w_attn_out, m_attn_sink, m_w_o, m_g_mix, m_g_ffn, m_w_gu, m_w_down, m_rel_bias, m_g_final, v_w_in, v_conv_w, v_w_a_out, v_w_pool, v_pool_scale, v_w_attn_out, v_attn_sink, v_w_o, v_g_mix, v_g_ffn, v_w_gu, v_w_down, v_rel_bias, v_g_final):
    w = dict(w_in=w_in, conv_w=conv_w, w_a_out=w_a_out, w_pool=w_pool, pool_scale=pool_scale, w_attn_out=w_attn_out,
             attn_sink=attn_sink, w_o=w_o, g_mix=g_mix, g_ffn=g_ffn, w_gu=w_gu, w_down=w_down, rel_bias=rel_bias,
             g_final=g_final)
    m = dict(w_in=m_w_in, conv_w=m_conv_w, w_a_out=m_w_a_out, w_pool=m_w_pool, pool_scale=m_pool_scale,
             w_attn_out=m_w_attn_out, attn_sink=m_attn_sink, w_o=m_w_o, g_mix=m_g_mix, g_ffn=m_g_ffn, w_gu=m_w_gu,
             w_down=m_w_down, rel_bias=m_rel_bias, g_final=m_g_final)
    v = dict(w_in=v_w_in, conv_w=v_conv_w, w_a_out=v_w_a_out, w_pool=v_w_pool, pool_scale=v_pool_scale,
             w_attn_out=v_w_attn_out, attn_sink=v_attn_sink, w_o=v_w_o, g_mix=v_g_mix, g_ffn=v_g_ffn, w_gu=v_w_gu,
             w_down=v_w_down, rel_bias=v_rel_bias, g_final=v_g_final)
    T, D = x.shape[1], x.shape[2]
    L = w_in.shape[0]
    cx, cy, cc = _place()

    schedule = _Schedule(w, (cx, cy, cc))
    schedule.arrived(0, ("w_in",), _run_job(_GatherJob(_shard_rows(w, 0)[:1]), "gather_w_in_l0"))
    cw = jnp.pad(conv_w.reshape(L * 3, -1), ((0, 16 - L * 3), (0, 0)))
    cw = _all_gather(cw, "gather_conv_w")
    cw = jnp.transpose(cw, (1, 0, 2)).reshape(16, -1)[:L * 3].reshape(L, 3, -1)
    small = dict(conv_w=jnp.pad(cw, ((0, 0), (0, 5), (0, 0))), pool_scale=pool_scale.reshape(L, 1, D),
                 g_mix=g_mix.reshape(L, 1, D), g_ffn=g_ffn.reshape(L, 1, D), attn_sink=attn_sink,
                 rel_bias=rel_bias, g_final=g_final.reshape(1, D))

    loss, dx, _, gs = _local_step(x[0], loss_target[0], [schedule.full[0]], small, schedule)
    loss = lax.psum(loss[0, 0], ("x", "y", "c"))
    grads = _own_grads(schedule.pieces, w)

    small_all = _all_gather(_pack_small(gs, L, D), "gather_small")
    small_sum = _sum_parts(small_all, jnp.int32(0), small_all[1:], F32, "small_sum")
    gsm = _unpack_small(small_sum, L, D)
    W8 = D // N_DEV
    dev = 4 * cx + 2 * cy + cc
    gsm["conv_w"] = lax.dynamic_slice_in_dim(gsm["conv_w"], dev * W8, W8, axis=3)
    grads.update(gsm)

    deltas, new_m, new_v = {}, {}, {}
    for name in WEIGHTS:
        deltas[name], new_m[name], new_v[name] = _adamw(w[name], grads[name], m[name], v[name], "adamw_" + name)

    return (loss, dx[None], *[grads[n] for n in WEIGHTS], *[deltas[n] for n in WEIGHTS],
            *[new_m[n] for n in WEIGHTS], *[new_v[n] for n in WEIGHTS])
```

```python
import functools
import math

import jax
import jax.numpy as jnp
from jax import lax
from jax.experimental import pallas as pl
from jax.experimental.pallas import tpu as pltpu

F32 = jnp.float32
BF16 = jnp.bfloat16
MESH = pl.DeviceIdType.MESH

N_DEV = 8
N_HEADS = 16
N_KV_HEADS = 4
HEAD_DIM = 64
GROUP = N_HEADS // N_KV_HEADS
BLOCK = 128
WINDOW = 128
N_BUCKETS = 32
MAX_DISTANCE = 128
POOL_WINDOWS = (2, 4, 8, 16)
POOL_GROUPS = 4
HALO = 8
EPS = 1e-6
NEG_INF = -1e30

ADAM_LR = 0.001
ADAM_B1 = 0.9
ADAM_B2 = 0.999
ADAM_EPS = 1e-08
ADAM_WD = 0.01
ADAM_STEP = 10

LANES = 1024
VMEM_LIMIT_BYTES = 48 * 1024 * 1024


def _params(*sem):
    return pltpu.CompilerParams(dimension_semantics=sem, vmem_limit_bytes=VMEM_LIMIT_BYTES)


def _tile(n, cap):
    if n <= cap:
        return n
    for t in range(cap - cap % 128, 0, -128):
        if n % t == 0:
            return t
    raise ValueError(f"no tile for {n}")


WIDE_TILE = 2176

_DIMS = {"nn": (((1,), (0,)), ((), ())), "nt": (((1,), (1,)), ((), ())), "tn": (((0,), (0,)), ((), ()))}


HBM_SPEC = pl.BlockSpec(memory_space=pltpu.HBM)
ANY_SPEC = pl.BlockSpec(memory_space=pl.ANY)


def _matmul(a, b, mode, out_dtype, name, res=None, tm_cap=1024, tn_cap=1024, tk_cap=1024, comm=None):
    if mode == "tn":
        K, M = a.shape
    else:
        M, K = a.shape
    N = b.shape[0] if mode == "nt" else b.shape[1]
    tm, tn, tk = _tile(M, tm_cap), _tile(N, tn_cap), _tile(K, tk_cap)
    nk = K // tk
    a_spec = pl.BlockSpec((tk, tm), lambda i, j, k: (k, i)) if mode == "tn" else pl.BlockSpec((tm, tk), lambda i, j, k: (i, k))
    b_spec = pl.BlockSpec((tn, tk), lambda i, j, k: (j, k)) if mode == "nt" else pl.BlockSpec((tk, tn), lambda i, j, k: (k, j))
    o_spec = pl.BlockSpec((tm, tn), lambda i, j, k: (i, j))
    dims = _DIMS[mode]
    has_res = res is not None
    gm, gn = M // tm, N // tn
    steps = gm * gn * nk
    n_in = 2 + has_res
    n_ci = len(comm.ins) if comm is not None else 0
    n_co = len(comm.outs) if comm is not None else 0

    def body(*refs):
        a_ref, b_ref = refs[0], refs[1]
        res_ref = refs[2] if has_res else None
        comm_in = refs[n_in:n_in + n_ci]
        o_ref = refs[n_in + n_ci]
        comm_out = refs[n_in + n_ci + 1:n_in + n_ci + 1 + n_co]
        acc_ref = refs[n_in + n_ci + 1 + n_co]
        sems = refs[n_in + n_ci + 2 + n_co:]
        k = pl.program_id(2)
        step = (pl.program_id(0) * gn + pl.program_id(1)) * nk + k
        if comm is not None:
            @pl.when(step == 0)
            def _():
                comm.start(comm_in, comm_out, sems)

        part = lax.dot_general(a_ref[...], b_ref[...], dims, preferred_element_type=F32)

        @pl.when(k == 0)
        def _():
            acc_ref[...] = part

        @pl.when(k > 0)
        def _():
            acc_ref[...] += part

        @pl.when(k == nk - 1)
        def _():
            out = acc_ref[...]
            if has_res:
                out = out + res_ref[...]
            o_ref[...] = out.astype(out_dtype)

        if comm is not None:
            @pl.when(step == (7 * steps) // 8)
            def _():
                comm.mid(comm_in, comm_out, sems)

            @pl.when(step == steps - 1)
            def _():
                comm.finish(comm_in, comm_out, sems)

    in_specs = [a_spec, b_spec] + ([o_spec] if has_res else [])
    args = (a, b) + ((res,) if has_res else ())
    out_shape = jax.ShapeDtypeStruct((M, N), out_dtype)
    if comm is None:
        return pl.pallas_call(
            body, name=name, grid=(gm, gn, nk), in_specs=in_specs, out_specs=o_spec, out_shape=out_shape,
            scratch_shapes=[pltpu.VMEM((tm, tn), F32)],
            compiler_params=_params("parallel", "parallel", "arbitrary"),
        )(*args)
    outs = pl.pallas_call(
        body, name=name, grid=(gm, gn, nk),
        in_specs=in_specs + [HBM_SPEC] * n_ci, out_specs=[o_spec] + [HBM_SPEC] * n_co,
        out_shape=[out_shape] + list(comm.outs),
        scratch_shapes=[pltpu.VMEM((tm, tn), F32)] + list(comm.sems),
        compiler_params=_params("arbitrary", "arbitrary", "arbitrary"),
    )(*args, *comm.ins)
    return outs[0], outs[1:]


def _pool_mm(a, w, mode, out_dtype, name):
    T = a.shape[0]
    G = POOL_GROUPS
    cg = a.shape[1] // G
    tm = _tile(T, 1024)
    nt = T // tm
    dims = _DIMS[mode]
    if mode == "tn":
        def body(a_ref, d_ref, o_ref):
            part = lax.dot_general(a_ref[...], d_ref[...], dims, preferred_element_type=F32)

            @pl.when(pl.program_id(1) == 0)
            def _():
                o_ref[...] = part

            @pl.when(pl.program_id(1) > 0)
            def _():
                o_ref[...] += part

        return pl.pallas_call(
            body, name=name, grid=(G, nt),
            in_specs=[pl.BlockSpec((tm, cg), lambda g, i: (i, g)), pl.BlockSpec((tm, cg), lambda g, i: (i, g))],
            out_specs=pl.BlockSpec((None, cg, cg), lambda g, i: (g, 0, 0)),
            out_shape=jax.ShapeDtypeStruct((G, cg, cg), F32),
            compiler_params=_params("parallel", "arbitrary"),
        )(a, w)

    def body(a_ref, w_ref, o_ref):
        o_ref[...] = lax.dot_general(a_ref[...], w_ref[...], dims, preferred_element_type=F32).astype(out_dtype)

    return pl.pallas_call(
        body, name=name, grid=(G, nt),
        in_specs=[pl.BlockSpec((tm, cg), lambda g, i: (i, g)), pl.BlockSpec((None, cg, cg), lambda g, i: (g, 0, 0))],
        out_specs=pl.BlockSpec((tm, cg), lambda g, i: (i, g)),
        out_shape=jax.ShapeDtypeStruct((T, G * cg), out_dtype),
        compiler_params=_params("parallel", "parallel"),
    )(a, w)


ROWS = 256
HALO_BLOCK = 16


def _row_spec(d, col=0, rows=ROWS):
    return pl.BlockSpec((rows, d), lambda i, col=col: (i, col))


def _const_spec(shape):
    return pl.BlockSpec(shape, lambda *_: (0,) * len(shape))


def _rms_fwd(x, g, name):
    T, D = x.shape

    def body(x_ref, g_ref, h_ref):
        xv = x_ref[...]
        r = lax.rsqrt(jnp.mean(xv * xv, axis=-1, keepdims=True) + EPS)
        h_ref[...] = (xv * r * g_ref[...]).astype(BF16)

    return pl.pallas_call(
        body, name=name, grid=(T // ROWS,),
        in_specs=[_row_spec(D), _const_spec((1, D))], out_specs=_row_spec(D),
        out_shape=jax.ShapeDtypeStruct((T, D), BF16), compiler_params=_params("parallel"),
    )(x, g)


def _accumulate(ref, part):
    first = pl.program_id(0) == 0

    @pl.when(first)
    def _():
        ref[...] = part

    @pl.when(jnp.logical_not(first))
    def _():
        ref[...] += part


def _rms_bwd(x, g, dh, dres, name):
    T, D = x.shape

    def body(x_ref, g_ref, dh_ref, dres_ref, dx_ref, dxb_ref, dg_ref):
        xv = x_ref[...]
        r = lax.rsqrt(jnp.mean(xv * xv, axis=-1, keepdims=True) + EPS)
        xhat = xv * r
        dh_v = dh_ref[...]
        dxhat = dh_v * g_ref[...]
        dx = dres_ref[...] + r * (dxhat - xhat * jnp.mean(dxhat * xhat, axis=-1, keepdims=True))
        dx_ref[...] = dx
        dxb_ref[...] = dx.astype(BF16)
        _accumulate(dg_ref, jnp.sum(dh_v * xhat, axis=0, keepdims=True))

    return pl.pallas_call(
        body, name=name, grid=(T // ROWS,),
        in_specs=[_row_spec(D), _const_spec((1, D)), _row_spec(D), _row_spec(D)],
        out_specs=[_row_spec(D), _row_spec(D), _const_spec((1, D))],
        out_shape=[jax.ShapeDtypeStruct((T, D), F32), jax.ShapeDtypeStruct((T, D), BF16),
                   jax.ShapeDtypeStruct((1, D), F32)],
        compiler_params=_params("arbitrary"),
    )(x, g, dh, dres)


def _loss_head(x, g, target, name):
    T, D = x.shape

    def body(x_ref, g_ref, t_ref, loss_ref, dx_ref, dxb_ref, dg_ref):
        xv = x_ref[...]
        gv = g_ref[...]
        r = lax.rsqrt(jnp.mean(xv * xv, axis=-1, keepdims=True) + EPS)
        xhat = xv * r
        err = xhat * gv - t_ref[...]
        loss = 0.5 * jnp.sum(jnp.mean(err * err, axis=-1, keepdims=True), axis=0, keepdims=True)
        dy = err * (1.0 / D)
        dxhat = dy * gv
        dx = r * (dxhat - xhat * jnp.mean(dxhat * xhat, axis=-1, keepdims=True))
        dx_ref[...] = dx
        dxb_ref[...] = dx.astype(BF16)
        _accumulate(loss_ref, loss)
        _accumulate(dg_ref, jnp.sum(dy * xhat, axis=0, keepdims=True))

    return pl.pallas_call(
        body, name=name, grid=(T // ROWS,),
        in_specs=[_row_spec(D), _const_spec((1, D)), _row_spec(D)],
        out_specs=[_const_spec((1, 1)), _row_spec(D), _row_spec(D), _const_spec((1, D))],
        out_shape=[jax.ShapeDtypeStruct((1, 1), F32), jax.ShapeDtypeStruct((T, D), F32),
                   jax.ShapeDtypeStruct((T, D), BF16), jax.ShapeDtypeStruct((1, D), F32)],
        compiler_params=_params("arbitrary"),
    )(x, g, target)


def _halo_specs(d, col, n_blocks):
    per = ROWS // HALO_BLOCK
    last = n_blocks * per - 1
    prev = pl.BlockSpec((HALO_BLOCK, d), lambda i, col=col: (jnp.maximum(i * per - 1, 0), col))
    nxt = pl.BlockSpec((HALO_BLOCK, d), lambda i, col=col: (jnp.minimum((i + 1) * per, last), col))
    return prev, nxt


def _with_halo(prev, cur, nxt, n_blocks):
    i = pl.program_id(0)
    prev = jnp.where(i > 0, prev[HALO_BLOCK - HALO:], 0.0)
    nxt = jnp.where(i < n_blocks - 1, nxt[:HALO], 0.0)
    return jnp.concatenate([prev, cur, nxt], axis=0)


def _f32(ref):
    return ref[...].astype(F32)


def _shift(ext, k):
    n = ext.shape[0]
    v = ext if k == 0 else pltpu.roll(ext, (-k) % n, 0)
    return v[HALO:HALO + ROWS]


def _shift_full(ext, k):
    n = ext.shape[0]
    return pltpu.roll(ext, (-k) % n, 0)


def _pool_counts(T):
    n = ROWS + 2 * HALO
    t = pl.program_id(0) * ROWS - HALO + lax.broadcasted_iota(jnp.int32, (n, 1), 0)
    out = []
    for w in POOL_WINDOWS:
        lo = jnp.maximum(t - w // 2, 0)
        hi = jnp.minimum(t + (w - 1 - w // 2), T - 1)
        out.append(jnp.maximum(hi - lo + 1, 1).astype(F32))
    return out


def _window_sums(e, sign):
    s2 = e + _shift_full(e, -sign)
    s4 = _shift_full(s2, -1) + _shift_full(s2, 1)
    s8 = _shift_full(s4, -2) + _shift_full(s4, 2)
    s16 = _shift_full(s8, -4) + _shift_full(s8, 4)
    return s2, s4, s8, s16


def _mixer_fwd(proj, conv_w, name):
    T = proj.shape[0]
    W = conv_w.shape[1]
    nb = T // ROWS
    cg = W // POOL_GROUPS

    def body(b_ref, c_ref, x_ref, u_ref, cp_ref, cn_ref, xp_ref, xn_ref, up_ref, un_ref, w_ref, z_ref, p_ref):
        uc = _with_halo(_f32(cp_ref) * _f32(xp_ref), _f32(c_ref) * _f32(x_ref), _f32(cn_ref) * _f32(xn_ref), nb)
        w0, w1, w2 = w_ref[0:1, :], w_ref[1:2, :], w_ref[2:3, :]
        y = w0 * _shift(uc, -1) + w1 * _shift(uc, 0) + w2 * _shift(uc, 1)
        z_ref[...] = (_f32(b_ref) * y).astype(BF16)
        e = _with_halo(_f32(up_ref), _f32(u_ref), _f32(un_ref), nb)
        counts = _pool_counts(T)
        for gi in range(POOL_GROUPS):
            eg = e[:, gi * cg:(gi + 1) * cg]
            s = _window_sums(eg, 1)[gi]
            p = s[HALO:HALO + ROWS] / counts[gi][HALO:HALO + ROWS] - eg[HALO:HALO + ROWS]
            p_ref[:, gi * cg:(gi + 1) * cg] = p.astype(BF16)

    halo = [s for col in (1, 2, 3) for s in _halo_specs(W, col, nb)]
    return pl.pallas_call(
        body, name=name, grid=(nb,),
        in_specs=[_row_spec(W, 0), _row_spec(W, 1), _row_spec(W, 2), _row_spec(W, 3)] + halo + [_const_spec((8, W))],
        out_specs=[_row_spec(W), _row_spec(W)],
        out_shape=[jax.ShapeDtypeStruct((T, W), BF16), jax.ShapeDtypeStruct((T, W), BF16)],
        compiler_params=_params("parallel"),
    )(proj, proj, proj, proj, proj, proj, proj, proj, proj, proj, conv_w)


def _mixer_bwd(proj, conv_w, dz, dp, dproj, name):
    T = proj.shape[0]
    W = conv_w.shape[1]
    nb = T // ROWS
    cg = W // POOL_GROUPS

    def body(b_ref, c_ref, x_ref, dz_ref, dp_ref,
             bp_ref, bn_ref, cp_ref, cn_ref, xp_ref, xn_ref, dzp_ref, dzn_ref, dpp_ref, dpn_ref, w_ref, _,
             o_ref, dw_ref):
        cv, xv, dzv = _f32(c_ref), _f32(x_ref), _f32(dz_ref)
        uc = _with_halo(_f32(cp_ref) * _f32(xp_ref), cv * xv, _f32(cn_ref) * _f32(xn_ref), nb)
        dy = _with_halo(_f32(dzp_ref) * _f32(bp_ref), dzv * _f32(b_ref), _f32(dzn_ref) * _f32(bn_ref), nb)
        w0, w1, w2 = w_ref[0:1, :], w_ref[1:2, :], w_ref[2:3, :]
        um, u0, up = _shift(uc, -1), _shift(uc, 0), _shift(uc, 1)
        o_ref[:, 0:W] = (dzv * (w0 * um + w1 * u0 + w2 * up)).astype(BF16)
        dy0 = _shift(dy, 0)
        duc = w0 * _shift(dy, 1) + w1 * dy0 + w2 * _shift(dy, -1)
        o_ref[:, W:2 * W] = (duc * xv).astype(BF16)
        o_ref[:, 2 * W:3 * W] = (duc * cv).astype(BF16)
        row = lax.broadcasted_iota(jnp.int32, (8, W), 0)
        dw = jnp.where(row == 0, jnp.sum(dy0 * um, axis=0, keepdims=True),
                       jnp.where(row == 1, jnp.sum(dy0 * u0, axis=0, keepdims=True),
                                 jnp.where(row == 2, jnp.sum(dy0 * up, axis=0, keepdims=True), 0.0)))
        _accumulate(dw_ref, dw)
        d = _with_halo(_f32(dpp_ref), _f32(dp_ref), _f32(dpn_ref), nb)
        counts = _pool_counts(T)
        for gi in range(POOL_GROUPS):
            dg = d[:, gi * cg:(gi + 1) * cg]
            s = _window_sums(dg / counts[gi], -1)[gi]
            o_ref[:, 3 * W + gi * cg:3 * W + (gi + 1) * cg] = (s[HALO:HALO + ROWS] - dg[HALO:HALO + ROWS]).astype(BF16)

    def halo(col):
        return list(_halo_specs(W, col, nb))

    return pl.pallas_call(
        body, name=name, grid=(nb,),
        in_specs=[_row_spec(W, 0), _row_spec(W, 1), _row_spec(W, 2), _row_spec(W), _row_spec(W)]
        + halo(0) + halo(1) + halo(2) + halo(0) + halo(0) + [_const_spec((8, W)), ANY_SPEC],
        out_specs=[_row_spec(4 * W), _const_spec((8, W))],
        out_shape=[jax.ShapeDtypeStruct(dproj.shape, BF16), jax.ShapeDtypeStruct((8, W), F32)],
        input_output_aliases={16: 0}, compiler_params=_params("arbitrary"),
    )(proj, proj, proj, dz, dp, proj, proj, proj, proj, proj, proj, dz, dz, dp, dp, conv_w, dproj)


def _t5_bucket(rel):
    half = N_BUCKETS // 2
    max_exact = half // 2
    ret = jnp.where(rel > 0, half, 0)
    n = jnp.abs(rel)
    nf = jnp.maximum(n, 1).astype(jnp.float32)
    large = max_exact + (jnp.log(nf / max_exact) / math.log(MAX_DISTANCE / max_exact)
                         * (half - max_exact)).astype(jnp.int32)
    large = jnp.minimum(large, half - 1)
    return ret + jnp.where(n < max_exact, n, large)


def _bucket_table():
    qi = jnp.arange(BLOCK)[:, None]
    kj = jnp.arange(3 * BLOCK)[None, :]
    rel = kj - BLOCK - qi
    return jnp.where(jnp.abs(rel) <= WINDOW, _t5_bucket(rel), -1).astype(jnp.int32)


def _bias_table(rel_bias, bucket, name):
    def body(rb_ref, bucket_ref, o_ref):
        h = pl.program_id(0)
        bk = bucket_ref[...]
        acc = jnp.full(bk.shape, NEG_INF, F32)
        for b in range(N_BUCKETS):
            acc = jnp.where(bk == b, rb_ref[b, h], acc)
        o_ref[...] = acc

    return pl.pallas_call(
        body, name=name, grid=(N_HEADS,),
        in_specs=[pl.BlockSpec(memory_space=pltpu.SMEM), _const_spec((BLOCK, 3 * BLOCK))],
        out_specs=pl.BlockSpec((None, BLOCK, 3 * BLOCK), lambda h: (h, 0, 0)),
        out_shape=jax.ShapeDtypeStruct((N_HEADS, BLOCK, 3 * BLOCK), F32),
        compiler_params=_params("parallel"),
    )(rel_bias, bucket)


def _bias_grad(ds_sum, bucket, name):
    def body(ds_ref, bucket_ref, o_ref):
        bk = bucket_ref[...]
        ds = ds_ref[...]
        row = lax.broadcasted_iota(jnp.int32, (N_BUCKETS, 128), 0)
        acc = jnp.zeros((N_BUCKETS, 128), F32)
        for b in range(N_BUCKETS):
            s = jnp.sum(jnp.sum(jnp.where(bk == b, ds, 0.0), axis=1, keepdims=True), axis=0, keepdims=True)
            acc = jnp.where(row == b, s, acc)
        o_ref[...] = acc

    return pl.pallas_call(
        body, name=name, grid=(N_HEADS,),
        in_specs=[pl.BlockSpec((None, BLOCK, 3 * BLOCK), lambda h: (h, 0, 0)), _const_spec((BLOCK, 3 * BLOCK))],
        out_specs=pl.BlockSpec((None, N_BUCKETS, 128), lambda h: (h, 0, 0)),
        out_shape=jax.ShapeDtypeStruct((N_HEADS, N_BUCKETS, 128), F32),
        compiler_params=_params("parallel"),
    )(ds_sum, bucket)


PAIR = 2 * HEAD_DIM
Q_BLOCKS_FWD = 4
Q_BLOCKS_BWD = 2


def _low_half(shape):
    return lax.broadcasted_iota(jnp.int32, shape, len(shape) - 1) % PAIR < HEAD_DIM


def _split_pair(a):
    low = _low_half(a.shape)
    zero = jnp.zeros_like(a)
    return jnp.concatenate([jnp.where(low, a, zero), jnp.where(low, zero, a)], axis=0)


def _kv_expand(proj, kv_off, name):
    T = proj.shape[0]
    kv_w = N_KV_HEADS * HEAD_DIM
    rows = _tile(T, 512)

    def body(k_ref, v_ref, ke_ref, ve_ref):
        for src, dst in ((k_ref, ke_ref), (v_ref, ve_ref)):
            for g in range(N_KV_HEADS // 2):
                x = src[:, g * PAIR:(g + 1) * PAIR].astype(F32)
                swapped = pltpu.roll(x, HEAD_DIM, 1)
                low = _low_half(x.shape)
                dst[:, 2 * g * PAIR:(2 * g + 1) * PAIR] = jnp.where(low, x, swapped).astype(BF16)
                dst[:, (2 * g + 1) * PAIR:(2 * g + 2) * PAIR] = jnp.where(low, swapped, x).astype(BF16)

    out = jax.ShapeDtypeStruct((T, N_KV_HEADS * PAIR), BF16)
    ospec = pl.BlockSpec((rows, N_KV_HEADS * PAIR), lambda i: (i, 0))
    return pl.pallas_call(
        body, name=name, grid=(T // rows,),
        in_specs=[pl.BlockSpec((rows, kv_w), lambda i: (i, kv_off // kv_w)),
                  pl.BlockSpec((rows, kv_w), lambda i: (i, kv_off // kv_w + 1))],
        out_specs=[ospec, ospec], out_shape=[out, out], compiler_params=_params("parallel"),
    )(proj, proj)


def _kv_fold(dke, dve, dproj, kv_off, name):
    T = dke.shape[0]
    kv_w = N_KV_HEADS * HEAD_DIM
    rows = _tile(T, 512)

    def body(dk_ref, dv_ref, _, o_ref):
        for n, src in enumerate((dk_ref, dv_ref)):
            for g in range(N_KV_HEADS // 2):
                a = src[:, 2 * g * PAIR:(2 * g + 1) * PAIR]
                b = src[:, (2 * g + 1) * PAIR:(2 * g + 2) * PAIR]
                a = a + pltpu.roll(a, HEAD_DIM, 1)
                b = b + pltpu.roll(b, HEAD_DIM, 1)
                o_ref[:, n * kv_w + g * PAIR:n * kv_w + (g + 1) * PAIR] = jnp.where(_low_half(a.shape), a, b).astype(BF16)

    ispec = pl.BlockSpec((rows, N_KV_HEADS * PAIR), lambda i: (i, 0))
    return pl.pallas_call(
        body, name=name, grid=(T // rows,), in_specs=[ispec, ispec, ANY_SPEC],
        out_specs=pl.BlockSpec((rows, 2 * kv_w), lambda i: (i, kv_off // (2 * kv_w))),
        out_shape=jax.ShapeDtypeStruct(dproj.shape, BF16), input_output_aliases={2: 0},
        compiler_params=_params("parallel"),
    )(dke, dve, dproj)


def _key_blocks(i, nb):
    return [pl.multiple_of(n * BLOCK, BLOCK) for n in (jnp.maximum(i - 1, 0), i, jnp.minimum(i + 1, nb - 1))]


def _three_blocks(ref, starts):
    return jnp.concatenate([ref[pl.ds(s, BLOCK), :] for s in starts], axis=0)


def _group_scores(q_ref, rows, kd, bias_ref, i, nb):
    qq = jnp.concatenate([_split_pair(q_ref[rows, pr * PAIR:(pr + 1) * PAIR]) for pr in range(GROUP // 2)], axis=0)
    s = lax.dot_general(qq, kd, _DIMS["nt"], preferred_element_type=F32) * (HEAD_DIM ** -0.5)
    s = s + bias_ref[...].reshape(GROUP * BLOCK, 3 * BLOCK)
    kj = lax.broadcasted_iota(jnp.int32, (1, 3 * BLOCK), 1)
    outside = jnp.logical_or(jnp.logical_and(i == 0, kj < BLOCK), jnp.logical_and(i == nb - 1, kj >= 2 * BLOCK))
    return qq, jnp.where(outside, NEG_INF, s)


def _per_head_rows(values):
    head = lax.broadcasted_iota(jnp.int32, (GROUP * BLOCK, 1), 0) // BLOCK
    out = jnp.full((GROUP * BLOCK, 1), values[0], F32)
    for g in range(1, GROUP):
        out = jnp.where(head == g, values[g], out)
    return out


def _attn_specs(T, q_off, Q_BLOCKS):
    gw = GROUP * HEAD_DIM
    return dict(
        sink=pl.BlockSpec(memory_space=pltpu.SMEM),
        q=pl.BlockSpec((Q_BLOCKS * BLOCK, gw), lambda j, i: (i, q_off // gw + j)),
        kv=pl.BlockSpec((T, PAIR), lambda j, i: (0, j)),
        bias=pl.BlockSpec((GROUP, BLOCK, 3 * BLOCK), lambda j, i: (j, 0, 0)),
        o=pl.BlockSpec((Q_BLOCKS * BLOCK, gw), lambda j, i: (i, j)))


def _attn_fwd(proj, q_off, kexp, vexp, bias, sink, name, comm=None):
    T = proj.shape[0]
    nb = T // BLOCK
    Q_BLOCKS = min(Q_BLOCKS_FWD, nb)
    sp = _attn_specs(T, q_off, Q_BLOCKS)
    steps = N_KV_HEADS * (nb // Q_BLOCKS)
    n_ci = len(comm.ins) if comm is not None else 0
    n_co = len(comm.outs) if comm is not None else 0

    def body(*refs):
        sink_ref, q_ref, ke_ref, ve_ref, bias_ref = refs[:5]
        comm_in = refs[5:5 + n_ci]
        o_ref, lse_ref = refs[5 + n_ci:7 + n_ci]
        comm_out = refs[7 + n_ci:7 + n_ci + n_co]
        sems = refs[7 + n_ci + n_co:]
        j, i = pl.program_id(0), pl.program_id(1)
        step = j * (nb // Q_BLOCKS) + i
        if comm is not None:
            @pl.when(step == 0)
            def _():
                comm.start(comm_in, comm_out, sems)

        low = _low_half((BLOCK, PAIR))
        sk = _per_head_rows([sink_ref[GROUP * j + g] for g in range(GROUP)])
        for b in range(Q_BLOCKS):
            blk = i * Q_BLOCKS + b
            rows = slice(b * BLOCK, (b + 1) * BLOCK)
            starts = _key_blocks(blk, nb)
            kd = _three_blocks(ke_ref, starts)
            vv = _split_pair(_three_blocks(ve_ref, starts))
            _, s = _group_scores(q_ref, rows, kd, bias_ref, blk, nb)
            m = jnp.maximum(jnp.max(s, axis=-1, keepdims=True), sk)
            p = jnp.exp(s - m)
            denom = jnp.sum(p, axis=-1, keepdims=True) + jnp.exp(sk - m)
            p = (p / denom).astype(BF16)
            lse = m + jnp.log(denom)
            for pr in range(GROUP // 2):
                lanes = slice(pr * PAIR, (pr + 1) * PAIR)
                a, c = slice(2 * pr * BLOCK, (2 * pr + 1) * BLOCK), slice((2 * pr + 1) * BLOCK, (2 * pr + 2) * BLOCK)
                pp = jnp.concatenate([p[a], p[c]], axis=1)
                o_ref[rows, lanes] = lax.dot_general(pp, vv, _DIMS["nn"], preferred_element_type=F32).astype(BF16)
                lse_ref[rows, lanes] = jnp.where(low, lse[a], lse[c])

        if comm is not None:
            @pl.when(step == (7 * steps) // 8)
            def _():
                comm.mid(comm_in, comm_out, sems)

            @pl.when(step == steps - 1)
            def _():
                comm.finish(comm_in, comm_out, sems)

    out_shape = [jax.ShapeDtypeStruct((T, N_HEADS * HEAD_DIM), BF16), jax.ShapeDtypeStruct((T, N_HEADS * HEAD_DIM), F32)]
    in_specs = [sp["sink"], sp["q"], sp["kv"], sp["kv"], sp["bias"]]
    if comm is None:
        att, lse = pl.pallas_call(
            body, name=name, grid=(N_KV_HEADS, nb // Q_BLOCKS), in_specs=in_specs, out_specs=[sp["o"], sp["o"]],
            out_shape=out_shape, compiler_params=_params("parallel", "parallel"),
        )(sink, proj, kexp, vexp, bias)
        return att, lse, []
    outs = pl.pallas_call(
        body, name=name, grid=(N_KV_HEADS, nb // Q_BLOCKS),
        in_specs=in_specs + [HBM_SPEC] * n_ci, out_specs=[sp["o"], sp["o"]] + [HBM_SPEC] * n_co,
        out_shape=out_shape + list(comm.outs), scratch_shapes=list(comm.sems),
        compiler_params=_params("arbitrary", "arbitrary"),
    )(sink, proj, kexp, vexp, bias, *comm.ins)
    return outs[0], outs[1], outs[2:]


def _attn_bwd(proj, q_off, kexp, vexp, bias, sink, out, lse, dout, dproj, name, comm=None):
    T = proj.shape[0]
    nb = T // BLOCK
    Q_BLOCKS = min(Q_BLOCKS_BWD, nb)
    sp = _attn_specs(T, q_off, Q_BLOCKS)
    scale = HEAD_DIM ** -0.5
    steps = N_KV_HEADS * (nb // Q_BLOCKS)
    n_ci = len(comm.ins) if comm is not None else 0
    n_co = len(comm.outs) if comm is not None else 0

    def body(*refs):
        sink_ref, q_ref, ke_ref, ve_ref, bias_ref, o_ref, lse_ref, do_ref = refs[:8]
        comm_in = refs[9:9 + n_ci]
        dq_ref, dke_ref, dve_ref, ds_ref, dsink_ref = refs[9 + n_ci:14 + n_ci]
        comm_out = refs[14 + n_ci:14 + n_ci + n_co]
        sems = refs[14 + n_ci + n_co:]
        j, i = pl.program_id(0), pl.program_id(1)
        step = j * (nb // Q_BLOCKS) + i
        if comm is not None:
            @pl.when(step == 0)
            def _():
                comm.start(comm_in, comm_out, sems)

        @pl.when(i == 0)
        def _():
            dke_ref[...] = jnp.zeros(dke_ref.shape, F32)
            dve_ref[...] = jnp.zeros(dve_ref.shape, F32)
            ds_ref[...] = jnp.zeros(ds_ref.shape, F32)
            dsink_ref[...] = jnp.zeros(dsink_ref.shape, F32)

        low = _low_half((BLOCK, PAIR))
        for b in range(Q_BLOCKS):
            blk = i * Q_BLOCKS + b
            rows = slice(b * BLOCK, (b + 1) * BLOCK)
            starts = _key_blocks(blk, nb)
            kd = _three_blocks(ke_ref, starts)
            vd = _three_blocks(ve_ref, starts)
            kk = _split_pair(kd)
            qq, s = _group_scores(q_ref, rows, kd, bias_ref, blk, nb)
            lse_rows, deltas, dd = [], [], []
            for pr in range(GROUP // 2):
                lanes = slice(pr * PAIR, (pr + 1) * PAIR)
                l2 = lse_ref[rows, lanes]
                l2s = pltpu.roll(l2, HEAD_DIM, 1)
                lse_rows += [jnp.where(low, l2, l2s), jnp.where(low, l2s, l2)]
                do2 = do_ref[rows, lanes]
                prod = do2.astype(F32) * o_ref[rows, lanes].astype(F32)
                deltas += [jnp.sum(jnp.where(low, prod, 0.0), axis=-1, keepdims=True),
                           jnp.sum(jnp.where(low, 0.0, prod), axis=-1, keepdims=True)]
                dd.append(_split_pair(do2))
                head = GROUP * j + 2 * pr
                p_sink = jnp.exp(jnp.where(low, sink_ref[head], sink_ref[head + 1]) - l2)
                dsink_ref[:, lanes] += jnp.sum(-p_sink * jnp.where(low, deltas[-2], deltas[-1]), axis=0, keepdims=True)
            dd = jnp.concatenate(dd, axis=0)
            p = jnp.exp(s - jnp.concatenate([jnp.concatenate([l] * 3, axis=1) for l in lse_rows], axis=0))
            dp = lax.dot_general(dd, vd, _DIMS["nt"], preferred_element_type=F32)
            ds = p * (dp - jnp.concatenate(deltas, axis=0))
            dsb = ds.astype(BF16)
            for pr in range(GROUP // 2):
                a, c = slice(2 * pr * BLOCK, (2 * pr + 1) * BLOCK), slice((2 * pr + 1) * BLOCK, (2 * pr + 2) * BLOCK)
                dq = lax.dot_general(jnp.concatenate([dsb[a], dsb[c]], axis=1), kk, _DIMS["nn"],
                                     preferred_element_type=F32) * scale
                dq_ref[rows, pr * PAIR:(pr + 1) * PAIR] = dq.astype(BF16)
            dk_acc = lax.dot_general(dsb, qq, _DIMS["tn"], preferred_element_type=F32) * scale
            dv_acc = lax.dot_general(p.astype(BF16), dd, _DIMS["tn"], preferred_element_type=F32)
            ds_ref[...] += ds.reshape(GROUP, BLOCK, 3 * BLOCK)
            for t, start in enumerate(starts):
                dke_ref[pl.ds(start, BLOCK), :] += dk_acc[t * BLOCK:(t + 1) * BLOCK]
                dve_ref[pl.ds(start, BLOCK), :] += dv_acc[t * BLOCK:(t + 1) * BLOCK]

        if comm is not None:
            @pl.when(step == (7 * steps) // 8)
            def _():
                comm.mid(comm_in, comm_out, sems)

            @pl.when(step == steps - 1)
            def _():
                comm.finish(comm_in, comm_out, sems)

    kv_out = jax.ShapeDtypeStruct((T, N_KV_HEADS * PAIR), F32)
    job_ins, job_outs, job_sems = (comm.ins, comm.outs, comm.sems) if comm is not None else ([], [], [])
    outs = pl.pallas_call(
        body, name=name, grid=(N_KV_HEADS, nb // Q_BLOCKS),
        in_specs=[sp["sink"], sp["q"], sp["kv"], sp["kv"], sp["bias"], sp["o"], sp["o"], sp["o"], ANY_SPEC]
        + [HBM_SPEC] * n_ci,
        out_specs=[sp["q"], sp["kv"], sp["kv"], sp["bias"],
                   pl.BlockSpec((1, GROUP * HEAD_DIM), lambda j, i: (0, j))] + [HBM_SPEC] * n_co,
        out_shape=[jax.ShapeDtypeStruct(dproj.shape, BF16), kv_out, kv_out,
                   jax.ShapeDtypeStruct((N_HEADS, BLOCK, 3 * BLOCK), F32),
                   jax.ShapeDtypeStruct((1, N_HEADS * HEAD_DIM), F32)] + list(job_outs),
        scratch_shapes=list(job_sems), input_output_aliases={8: 0},
        compiler_params=_params("arbitrary" if comm is not None else "parallel", "arbitrary"),
    )(sink, proj, kexp, vexp, bias, out, lse, dout, dproj, *job_ins)
    return outs[:5], outs[5:]


GATE_COLS = 512


def _sigmoid(x):
    return 1.0 / (1.0 + jnp.exp(-x))


def _gate_specs(D, gate_off):
    nc = D // GATE_COLS
    base = gate_off // GATE_COLS
    return [pl.BlockSpec((ROWS, GATE_COLS), lambda i, c=base + g * nc + h: (i, c)) for g in range(3) for h in range(nc)]


def _merge_fwd(proj, gate_off, ya, yp, yt, scale, name):
    T, D = ya.shape
    nc = D // GATE_COLS

    def body(*refs):
        gates = refs[:3 * nc]
        ya_ref, yp_ref, yt_ref, s_ref, o_ref = refs[3 * nc:]
        for h in range(nc):
            cols = slice(h * GATE_COLS, (h + 1) * GATE_COLS)
            merged = (_sigmoid(_f32(gates[h])) * ya_ref[:, cols].astype(F32)
                      + _sigmoid(_f32(gates[nc + h])) * (yp_ref[:, cols].astype(F32) * s_ref[:, cols])
                      + _sigmoid(_f32(gates[2 * nc + h])) * yt_ref[:, cols].astype(F32))
            o_ref[:, cols] = merged.astype(BF16)

    yspec = _row_spec(D)
    return pl.pallas_call(
        body, name=name, grid=(T // ROWS,),
        in_specs=_gate_specs(D, gate_off) + [yspec, yspec, yspec, _const_spec((1, D))], out_specs=yspec,
        out_shape=jax.ShapeDtypeStruct((T, D), BF16), compiler_params=_params("parallel"),
    )(*([proj] * (3 * nc)), ya, yp, yt, scale)


def _merge_bwd(proj, gate_off, ya, yp, yt, scale, dm, name):
    T, D = ya.shape
    nc = D // GATE_COLS
    base = gate_off // GATE_COLS

    def body(gate_ref, ya_ref, yp_ref, yt_ref, s_ref, dm_ref, dg_ref, dya_ref, dyp_ref, dyt_ref, ds_ref):
        i, n = pl.program_id(0), pl.program_id(1)
        sg = _sigmoid(_f32(gate_ref))
        for g, (y_ref, dy_ref) in enumerate(((ya_ref, dya_ref), (yp_ref, dyp_ref), (yt_ref, dyt_ref))):
            for h in range(nc):
                @pl.when(n == g * nc + h)
                def _(g=g, h=h, y_ref=y_ref, dy_ref=dy_ref):
                    cols = slice(h * GATE_COLS, (h + 1) * GATE_COLS)
                    dy = dm_ref[:, cols].astype(F32) * sg
                    y = y_ref[:, cols].astype(F32)
                    if g == 1:
                        s_v = s_ref[:, cols]
                        part = jnp.sum(dy * y, axis=0, keepdims=True)

                        @pl.when(i == 0)
                        def _():
                            ds_ref[:, cols] = part

                        @pl.when(i > 0)
                        def _():
                            ds_ref[:, cols] += part

                        y = y * s_v
                        dy_ref[:, cols] = (dy * s_v).astype(BF16)
                    else:
                        dy_ref[:, cols] = dy.astype(BF16)
                    dg_ref[...] = (dy * y * (1.0 - sg)).astype(BF16)

    rows = _tile(T, 4 * ROWS)
    yspec = pl.BlockSpec((rows, D), lambda i, n: (i, 0))
    gspec = pl.BlockSpec((rows, GATE_COLS), lambda i, n: (i, base + n))
    sspec = pl.BlockSpec((1, D), lambda i, n: (0, 0))
    out = jax.ShapeDtypeStruct((T, D), BF16)
    return pl.pallas_call(
        body, name=name, grid=(T // rows, 3 * nc),
        in_specs=[gspec, yspec, yspec, yspec, sspec, yspec],
        out_specs=[gspec, yspec, yspec, yspec, sspec],
        out_shape=[jax.ShapeDtypeStruct(proj.shape, BF16), out, out, out, jax.ShapeDtypeStruct((1, D), F32)],
        compiler_params=_params("arbitrary", "arbitrary"),
    )(proj, ya, yp, yt, scale, dm)


def _swiglu_fwd(gu, name):
    T = gu.shape[0]
    F = gu.shape[1] // 2

    def body(gu_ref, o_ref):
        g = gu_ref[:, 0:F].astype(F32)
        o_ref[...] = (g * _sigmoid(g) * gu_ref[:, F:2 * F].astype(F32)).astype(BF16)

    return pl.pallas_call(
        body, name=name, grid=(T // ROWS,), in_specs=[_row_spec(2 * F)], out_specs=_row_spec(F),
        out_shape=jax.ShapeDtypeStruct((T, F), BF16), compiler_params=_params("parallel"),
    )(gu)


def _swiglu_bwd(gu, dact, name):
    T = gu.shape[0]
    F = gu.shape[1] // 2

    def body(gu_ref, d_ref, o_ref):
        g, d = gu_ref[:, 0:F].astype(F32), d_ref[...].astype(F32)
        sg = _sigmoid(g)
        o_ref[:, 0:F] = (d * gu_ref[:, F:2 * F].astype(F32) * sg * (1.0 + g * (1.0 - sg))).astype(BF16)
        o_ref[:, F:2 * F] = (d * g * sg).astype(BF16)

    return pl.pallas_call(
        body, name=name, grid=(T // ROWS,), in_specs=[_row_spec(2 * F), _row_spec(F)], out_specs=_row_spec(2 * F),
        out_shape=jax.ShapeDtypeStruct((T, 2 * F), BF16), compiler_params=_params("parallel"),
    )(gu, dact)


def _carried(plan, key, *args, **kwargs):
    job = plan.job(key) if plan is not None else None
    if job is None:
        return _matmul(*args, **kwargs)
    out, extra = _matmul(*args, comm=job, **kwargs)
    plan.done(key, extra)
    return out


def _local_step(x, target, wts, small, hooks=None):
    T, D = x.shape
    depth = small["g_mix"].shape[0]
    wts = list(wts) + [None] * (depth - len(wts))
    gate_off = wts[0]["w_inT"].shape[0] - 3 * D
    q_off = 4 * D
    bucket = _bucket_table()
    bias = _bias_table(small["rel_bias"], bucket, "bias_table")

    saved = []
    for l in range(depth):
        n = f"l{l}_"
        if hooks is not None and l > 0:
            wts[l] = hooks.weights(l)
        w = wts[l]
        plan = hooks.plan_fwd(l) if hooks is not None else None
        h = _rms_fwd(x, small["g_mix"][l], n + "rms_mix")
        proj = _carried(plan, "proj", h, w["w_inT"], "nt", BF16, n + "proj", tn_cap=WIDE_TILE)
        z, p = _mixer_fwd(proj, small["conv_w"][l], n + "mixer")
        kexp, vexp = _kv_expand(proj, q_off + D, n + "kv_expand")
        sink = small["attn_sink"][l]
        job = plan.job("attn") if plan is not None else None
        att, lse, extra = _attn_fwd(proj, q_off, kexp, vexp, bias, sink, n + "attn", comm=job)
        if job is not None:
            plan.done("attn", extra)
        ya =_matmul(z, w["w_a_out"], "nn", BF16, n + "ya")
        yp = _pool_mm(p, w["w_pool"], "nn", BF16, n + "yp")
        yt = _matmul(att, w["w_attn_out"], "nn", BF16, n + "yt")
        merged = _merge_fwd(proj, gate_off, ya, yp, yt, small["pool_scale"][l], n + "merge")
        x1 = _matmul(merged, w["w_o"], "nn", F32, n + "x1", res=x)
        h2 = _rms_fwd(x1, small["g_ffn"][l], n + "rms_ffn")
        gu = _carried(plan, "gu", h2, w["w_guT"], "nt", BF16, n + "gu", tn_cap=WIDE_TILE)
        act = _swiglu_fwd(gu, n + "swiglu")
        ff = w["w_down"].shape[0]
        x2 = _carried(plan, "x2", act, w["w_down"], "nn", F32, n + "x2", res=x1, tn_cap=512, tk_cap=ff)
        saved.append(dict(x=x, h=h, proj=proj, z=z, p=p, kexp=kexp, vexp=vexp, sink=sink, lse=lse, att=att,
                          ya=ya, yp=yp, yt=yt, merged=merged, x1=x1, h2=h2, gu=gu, act=act))
        x = x2

    loss, dx, dxb, dg_final = _loss_head(x, small["g_final"], target, "loss_head")

    gw = [None] * depth
    gs = {k_: [None] * depth for k_ in ("conv_w", "pool_scale", "g_mix", "g_ffn", "attn_sink")}
    ds_total = None
    for l in reversed(range(depth)):
        n = f"l{l}_b_"
        s, w, g = saved[l], wts[l], {}
        plan = hooks.plan_bwd(l) if hooks is not None else None
        ff = w["w_down"].shape[0]
        g["w_down"] = _carried(plan, "dw_down", s["act"], dxb, "tn", BF16, n + "dw_down", tm_cap=ff, tk_cap=512)
        dact = _matmul(dxb, w["w_down"], "nt", BF16, n + "dact", tm_cap=512, tn_cap=ff)
        dgu = _swiglu_bwd(s["gu"], dact, n + "swiglu")
        g["w_guT"] = _carried(plan, "dw_gu", dgu, s["h2"], "tn", BF16, n + "dw_gu", tm_cap=WIDE_TILE)
        dh2 = _carried(plan, "dh2", dgu, w["w_guT"], "nn", F32, n + "dh2", tn_cap=512, tk_cap=ff)
        dx1, dx1b, gs["g_ffn"][l] = _rms_bwd(s["x1"], small["g_ffn"][l], dh2, dx, n + "rms_ffn")
        g["w_o"] = _matmul(s["merged"], dx1b, "tn", BF16, n + "dw_o")
        dm = _matmul(dx1b, w["w_o"], "nt", BF16, n + "dmerged")
        dproj, dya, dyp, dyt, gs["pool_scale"][l] = _merge_bwd(
            s["proj"], gate_off, s["ya"], s["yp"], s["yt"], small["pool_scale"][l], dm, n + "merge")
        g["w_a_out"] = _matmul(s["z"], dya, "tn", BF16, n + "dw_a_out")
        dz = _matmul(dya, w["w_a_out"], "nt", BF16, n + "dz")
        g["w_pool"] = _pool_mm(s["p"], dyp, "tn", F32, n + "dw_pool")
        dp = _pool_mm(dyp, w["w_pool"], "nt", BF16, n + "dp")
        g["w_attn_out"] = _matmul(s["att"], dyt, "tn", BF16, n + "dw_attn_out")
        if hooks is not None:
            hooks.early_grads(l, g)
        datt = _carried(plan, "datt", dyt, w["w_attn_out"], "nt", BF16, n + "datt")
        dproj, gs["conv_w"][l] = _mixer_bwd(s["proj"], small["conv_w"][l], dz, dp, dproj, n + "mixer")
        job = plan.job("attn_b") if plan is not None else None
        (dproj, dke, dve, ds_sum, dsink), extra = _attn_bwd(
            s["proj"], q_off, s["kexp"], s["vexp"], bias, s["sink"], s["att"], s["lse"], datt, dproj, n + "attn",
            comm=job)
        if job is not None:
            plan.done("attn_b", extra)
        gs["attn_sink"][l] = dsink.reshape(N_HEADS, HEAD_DIM)[:, 0]
        ds_total = ds_sum if ds_total is None else ds_total + ds_sum
        dproj = _kv_fold(dke, dve, dproj, q_off + D, n + "kv_fold")
        g["w_inT"] = _carried(plan, "dw_in", dproj, s["h"], "tn", BF16, n + "dw_in", tm_cap=WIDE_TILE)
        dh = _carried(plan, "dh", dproj, w["w_inT"], "nn", F32, n + "dh", tk_cap=2816)
        dx, dxb, gs["g_mix"][l] = _rms_bwd(s["x"], small["g_mix"][l], dh, dx1, n + "rms_mix")
        gw[l] = g
        if hooks is not None:
            hooks.grads(l, g)

    d_rel =_bias_grad(ds_total, bucket, "bias_grad")[:, :, 0].T
    gs = {k_: jnp.stack(v_) for k_, v_ in gs.items()}
    gs["rel_bias"] = d_rel
    gs["g_final"] = dg_final
    return loss, dx, gw, gs


def _place():
    return lax.axis_index("x"), lax.axis_index("y"), lax.axis_index("c")


class _GatherJob:
    def __init__(self, parts):
        n = len(parts)
        self.n = n
        self.ins = list(parts)
        self.outs = [jax.ShapeDtypeStruct((N_DEV,) + p.shape, p.dtype) for p in parts]
        self.sems = [pltpu.SemaphoreType.DMA((7 * n,)), pltpu.SemaphoreType.DMA((7 * n,)), pltpu.SemaphoreType.DMA((n,))]

    def _copies(self, ins, outs, sems):
        send_sems, recv_sems, local_sems = sems
        x, y, c = _place()
        me, sibling = (x, y, c), (x, y, 1 - c)
        chips = [(1 - x, y), (x, 1 - y), (1 - x, 1 - y)]

        def rows(t, px, py, pc):
            return outs[t].at[4 * px + 2 * py + pc]

        def copy(t, k, block, to, src=None):
            return pltpu.make_async_remote_copy(
                src_ref=rows(t, *block) if src is None else src, dst_ref=rows(t, *block),
                send_sem=send_sems.at[7 * t + k], recv_sem=recv_sems.at[7 * t + k], device_id=to, device_id_type=MESH)

        ts = range(self.n)
        own = [pltpu.make_async_copy(ins[t], rows(t, *me), local_sems.at[t]) for t in ts]
        first = [copy(t, 0, me, sibling, src=ins[t]) for t in ts]
        first += [copy(t, 1 + j, me, (*chip, c), src=ins[t]) for t in ts for j, chip in enumerate(chips)]
        landed = [copy(t, 1 + j, (*chip, c), me) for j, chip in enumerate(chips) for t in ts]
        passed = [copy(t, 4 + j, (*chip, c), sibling) for j, chip in enumerate(chips) for t in ts]
        last = [copy(t, 0, sibling, me) for t in ts]
        last += [copy(t, 4 + j, (*chip, 1 - c), me) for t in ts for j, chip in enumerate(chips)]
        return own, first, landed, passed, last

    def start(self, ins, outs, sems):
        own, first, _, _, _ = self._copies(ins, outs, sems)
        for cp in own + first:
            cp.start()

    def mid(self, ins, outs, sems):
        _, _, landed, passed, _ = self._copies(ins, outs, sems)
        for arrived, onward in zip(landed, passed):
            arrived.wait_recv()
            onward.start()

    def finish(self, ins, outs, sems):
        own, first, _, passed, last = self._copies(ins, outs, sems)
        for cp in last:
            cp.wait_recv()
        for cp in first + passed:
            cp.wait_send()
        for cp in own:
            cp.wait()


class _SwapJob:
    def __init__(self, g):
        self.ins = [g]
        self.outs = [jax.ShapeDtypeStruct(g.shape[:1] + g.shape[2:], g.dtype)]
        self.sems = [pltpu.SemaphoreType.DMA, pltpu.SemaphoreType.DMA]

    def _copy(self, ins, outs, sems):
        x, y, c = _place()
        return pltpu.make_async_remote_copy(src_ref=ins[0].at[pl.ds(0, ins[0].shape[0]), 1 - c], dst_ref=outs[0],
                                            send_sem=sems[0], recv_sem=sems[1], device_id=(x, y, 1 - c),
                                            device_id_type=MESH)

    def start(self, ins, outs, sems):
        self._copy(ins, outs, sems).start()

    def mid(self, ins, outs, sems):
        pass

    def finish(self, ins, outs, sems):
        self._copy(ins, outs, sems).wait()


class _ExchangeJob:
    def __init__(self, p, row0, rows):
        self.row0, self.rows = row0, rows
        self.ins = [p]
        self.outs = [jax.ShapeDtypeStruct((3, rows) + p.shape[2:], p.dtype)]
        self.sems = [pltpu.SemaphoreType.DMA((3,)), pltpu.SemaphoreType.DMA((3,))]

    def _copies(self, ins, outs, sems):
        x, y, c = _place()
        chips = [(1 - x, y), (x, 1 - y), (1 - x, 1 - y)]
        return [pltpu.make_async_remote_copy(
            src_ref=ins[0].at[2 * px + py, pl.ds(self.row0, self.rows)], dst_ref=outs[0].at[k],
            send_sem=sems[0].at[k], recv_sem=sems[1].at[k], device_id=(px, py, c), device_id_type=MESH)
            for k, (px, py) in enumerate(chips)]

    def start(self, ins, outs, sems):
        for cp in self._copies(ins, outs, sems):
            cp.start()

    def mid(self, ins, outs, sems):
        pass

    def finish(self, ins, outs, sems):
        for cp in self._copies(ins, outs, sems):
            cp.wait()


def _all_gather(v, name):
    def body(x_ref, out_ref, send_sems, recv_sems, local_sem):
        x, y, c = _place()
        me, sibling = (x, y, c), (x, y, 1 - c)
        chips = [(1 - x, y), (x, 1 - y), (1 - x, 1 - y)]

        def rows(px, py, pc):
            return out_ref.at[4 * px + 2 * py + pc]

        def copy(k, block, to, src=None):
            return pltpu.make_async_remote_copy(
                src_ref=rows(*block) if src is None else src, dst_ref=rows(*block),
                send_sem=send_sems.at[k], recv_sem=recv_sems.at[k], device_id=to, device_id_type=MESH)

        mine = pltpu.make_async_copy(x_ref, rows(*me), local_sem)
        mine.start()
        first = [copy(0, me, sibling, src=x_ref)]
        first += [copy(1 + j, me, (*chip, c), src=x_ref) for j, chip in enumerate(chips)]
        for cp in first:
            cp.start()
        passed = [copy(4 + j, (*chip, c), sibling) for j, chip in enumerate(chips)]
        for j, chip in enumerate(chips):
            copy(1 + j, (*chip, c), me).wait_recv()
            passed[j].start()
        copy(0, sibling, me).wait_recv()
        for j, chip in enumerate(chips):
            copy(4 + j, (*chip, 1 - c), me).wait_recv()
        for cp in first + passed:
            cp.wait_send()
        mine.wait()

    return pl.pallas_call(
        body, name=name, in_specs=[HBM_SPEC], out_specs=HBM_SPEC,
        out_shape=jax.ShapeDtypeStruct((N_DEV,) + v.shape, v.dtype),
        scratch_shapes=[pltpu.SemaphoreType.DMA((7,)), pltpu.SemaphoreType.DMA((7,)), pltpu.SemaphoreType.DMA],
    )(v)


def _all_gather_many(parts, name):
    n = len(parts)

    def body(*refs):
        ins, outs = refs[:n], refs[n:2 * n]
        send_sems, recv_sems, local_sems = refs[2 * n:]
        x, y, c = _place()
        me, sibling = (x, y, c), (x, y, 1 - c)
        chips = [(1 - x, y), (x, 1 - y), (1 - x, 1 - y)]

        def rows(t, px, py, pc):
            return outs[t].at[4 * px + 2 * py + pc]

        def copy(t, k, block, to, src=None):
            return pltpu.make_async_remote_copy(
                src_ref=rows(t, *block) if src is None else src, dst_ref=rows(t, *block),
                send_sem=send_sems.at[7 * t + k], recv_sem=recv_sems.at[7 * t + k], device_id=to, device_id_type=MESH)

        mine = [pltpu.make_async_copy(ins[t], rows(t, *me), local_sems.at[t]) for t in range(n)]
        sends = []
        for t in range(n):
            mine[t].start()
            sends.append(copy(t, 0, me, sibling, src=ins[t]))
            sends += [copy(t, 1 + j, me, (*chip, c), src=ins[t]) for j, chip in enumerate(chips)]
        for cp in sends:
            cp.start()
        for j, chip in enumerate(chips):
            for t in range(n):
                copy(t, 1 + j, (*chip, c), me).wait_recv()
                passed = copy(t, 4 + j, (*chip, c), sibling)
                passed.start()
                sends.append(passed)
        for t in range(n):
            copy(t, 0, sibling, me).wait_recv()
            for j, chip in enumerate(chips):
                copy(t, 4 + j, (*chip, 1 - c), me).wait_recv()
        for cp in sends:
            cp.wait_send()
        for cp in mine:
            cp.wait()

    return pl.pallas_call(
        body, name=name, in_specs=[HBM_SPEC] * n, out_specs=[HBM_SPEC] * n,
        out_shape=[jax.ShapeDtypeStruct((N_DEV,) + p.shape, p.dtype) for p in parts],
        scratch_shapes=[pltpu.SemaphoreType.DMA((7 * n,)), pltpu.SemaphoreType.DMA((7 * n,)),
                        pltpu.SemaphoreType.DMA((n,))],
    )(*parts)


def _run_job(job, name):
    n_in, n_out = len(job.ins), len(job.outs)

    def body(*refs):
        ins, outs, sems = refs[:n_in], refs[n_in:n_in + n_out], refs[n_in + n_out:]
        job.start(ins, outs, sems)
        job.mid(ins, outs, sems)
        job.finish(ins, outs, sems)

    return pl.pallas_call(
        body, name=name, in_specs=[HBM_SPEC] * n_in, out_specs=[HBM_SPEC] * n_out, out_shape=list(job.outs),
        scratch_shapes=list(job.sems),
    )(*job.ins)


def _chip_exchange(p, name):
    def body(p_ref, out_ref, send_sems, recv_sems):
        x, y, c = _place()
        chips = [(1 - x, y), (x, 1 - y), (1 - x, 1 - y)]
        copies = [pltpu.make_async_remote_copy(
            src_ref=p_ref.at[2 * px + py], dst_ref=out_ref.at[k], send_sem=send_sems.at[k], recv_sem=recv_sems.at[k],
            device_id=(px, py, c), device_id_type=MESH) for k, (px, py) in enumerate(chips)]
        for cp in copies:
            cp.start()
        for cp in copies:
            cp.wait()

    return pl.pallas_call(
        body, name=name, in_specs=[HBM_SPEC], out_specs=HBM_SPEC,
        out_shape=jax.ShapeDtypeStruct((3,) + p.shape[1:], p.dtype),
        scratch_shapes=[pltpu.SemaphoreType.DMA((3,)), pltpu.SemaphoreType.DMA((3,))],
    )(p)


SUM_ROWS_CAP = 576


def _sum_parts(own, index, others, out_dtype, name, own_row0=0, own_step=0):
    R = others.shape[1]
    common = math.gcd(R, own.shape[1], own_row0 or R)
    rows = next(t for t in range(min(common, SUM_ROWS_CAP) // 16 * 16, 0, -16) if common % t == 0)
    k = others.shape[0]
    assert own_row0 % rows == 0 and own.shape[1] % rows == 0
    blk0 = own_row0 // rows
    per_own = own.shape[1] // rows

    def own_block(i, idx):
        if own_step:
            return (idx[0] + own_step * (i // per_own), i % per_own, 0)
        return (idx[0], blk0 + i, 0)

    def body(idx_ref, own_ref, *refs):
        del idx_ref
        acc = own_ref[...].astype(F32)
        for r in refs[:k]:
            acc = acc + r[...].astype(F32)
        refs[k][...] = acc.astype(out_dtype)

    grid_spec = pltpu.PrefetchScalarGridSpec(
        num_scalar_prefetch=1, grid=(R // rows,),
        in_specs=[pl.BlockSpec((None, rows, LANES), own_block)]
        + [pl.BlockSpec((None, rows, LANES), lambda i, idx, j=j: (j, i, 0)) for j in range(k)],
        out_specs=pl.BlockSpec((rows, LANES), lambda i, idx: (i, 0)))
    return pl.pallas_call(
        body, name=name, grid_spec=grid_spec,
        out_shape=jax.ShapeDtypeStruct((R, LANES), out_dtype), compiler_params=_params("parallel"),
    )(jnp.reshape(index, (1,)).astype(jnp.int32), own, *([others] * k))


def _adamw(w, g, m, v, name):
    shape = w.shape
    cols = shape[-1]
    rows_total = w.size // cols
    w2, g2, m2, v2 = (a.reshape(rows_total, cols) for a in (w, g, m, v))
    rows = rows_total
    if rows_total > ROWS:
        rows = next(r for r in range(ROWS, 0, -8) if rows_total % r == 0)

    def body(w_ref, g_ref, m_ref, v_ref, d_ref, nm_ref, nv_ref):
        gv = g_ref[...]
        nm = ADAM_B1 * m_ref[...] + (1.0 - ADAM_B1) * gv
        nv = ADAM_B2 * v_ref[...] + (1.0 - ADAM_B2) * (gv * gv)
        m_hat = nm / (1.0 - ADAM_B1 ** ADAM_STEP)
        v_hat = nv / (1.0 - ADAM_B2 ** ADAM_STEP)
        d_ref[...] = -ADAM_LR * (m_hat / (jnp.sqrt(v_hat) + ADAM_EPS) + ADAM_WD * w_ref[...])
        nm_ref[...] = nm
        nv_ref[...] = nv

    spec = pl.BlockSpec((rows, cols), lambda i: (i, 0))
    out = jax.ShapeDtypeStruct((rows_total, cols), F32)
    d, nm, nv = pl.pallas_call(
        body, name=name, grid=(rows_total // rows,), in_specs=[spec] * 4, out_specs=[spec] * 3,
        out_shape=[out, out, out], compiler_params=_params("parallel"),
    )(w2, g2, m2, v2)
    return d.reshape(shape), nm.reshape(shape), nv.reshape(shape)


BIG = ("w_in", "w_a_out", "w_pool", "w_attn_out", "w_o", "w_gu", "w_down")


LOCAL = dict(w_in="w_inT", w_a_out="w_a_out", w_pool="w_pool", w_attn_out="w_attn_out", w_o="w_o", w_gu="w_guT",
             w_down="w_down")


def _shard_rows(w, l):
    out = []
    for name in BIG:
        a = w[name][l]
        if name in ("w_in", "w_gu"):
            a = a.T
        elif name == "w_pool":
            a = a.reshape(-1, a.shape[-1])
        out.append(a.astype(BF16))
    return out


def _full_weights(names, gathered, w):
    out = {}
    for name, g in zip(names, gathered):
        if name == "w_pool":
            G, rg, cg = w[name].shape[1:]
            out[name] = jnp.transpose(g.reshape(N_DEV, G, rg, cg), (1, 0, 2, 3)).reshape(G, N_DEV * rg, cg)
        else:
            out[LOCAL[name]] = g.reshape(N_DEV * g.shape[1], g.shape[2])
    return out


def _split_grads(g, w, names=BIG, multiple=1):
    parts, spans, at = [], {}, 0
    for name in names:
        a = g[LOCAL[name]].astype(BF16)
        if name == "w_pool":
            G, rg, cg = w[name].shape[1:]
            a = jnp.transpose(a.reshape(G, N_DEV, rg, cg), (1, 0, 2, 3))
        a = a.reshape(N_DEV, -1, LANES)
        spans[name] = (at, at + a.shape[1])
        at += a.shape[1]
        parts.append(a)
    if at % multiple:
        parts.append(jnp.zeros((N_DEV, multiple - at % multiple, LANES), BF16))
    return jnp.concatenate(parts, axis=1), spans


def _own_grads(pieces, w):
    out = {}
    for name in BIG:
        per_layer = []
        for layer in pieces:
            packed, spans = next((p, s) for p, s in layer if name in s)
            per_layer.append(packed[spans[name][0]:spans[name][1]])
        a = jnp.stack(per_layer)
        if name in ("w_in", "w_gu"):
            sh = w[name].shape
            a = jnp.swapaxes(a.reshape(sh[0], sh[2], sh[1]), 1, 2)
        out[name] = a.reshape(w[name].shape)
    return out


SQUARE = ("w_a_out", "w_pool", "w_attn_out", "w_o")


class _Prefetch:
    def __init__(self, schedule, assign):
        self.schedule, self.assign = schedule, assign

    def job(self, key):
        if key not in self.assign:
            return None
        parts = []
        for layer, names in self.assign[key]:
            shards = dict(zip(BIG, _shard_rows(self.schedule.w, layer)))
            parts += [shards[name] for name in names]
        return _GatherJob(parts)

    def done(self, key, outs):
        for layer, names in self.assign[key]:
            self.schedule.arrived(layer, names, outs[:len(names)])
            outs = outs[len(names):]


class _Reduce:
    def __init__(self, split, spans, place, tag, swap_key="dw_down", carriers=("dw_gu", "dh2", "dw_in", "dh"),
                 chunk_rows=(640, 640, 768, 512)):
        self.split, self.spans, self.tag = split, spans, tag
        self.swap_key, self.carriers = swap_key, carriers
        self.core, self.chip = place[2], 2 * place[0] + place[1]
        self.rows = split.shape[1]
        self.chunks, at = [], 0
        for rows in chunk_rows:
            rows = min(rows, self.rows - at) if len(self.chunks) + 1 < len(chunk_rows) else self.rows - at
            if rows > 0:
                self.chunks.append((at, rows))
                at += rows
        assert at == self.rows and len(self.chunks) <= len(carriers)
        self.sums = [None] * len(self.chunks)

    def _swap_job(self):
        return _SwapJob(self.split.reshape(4, 2, self.rows, LANES))

    def _pair_sum(self, from_sibling):
        pair = _sum_parts(self.split, self.core, from_sibling.reshape(1, 4 * self.rows, LANES), BF16,
                          self.tag + "pair_sum", own_step=2)
        self.pair = pair.reshape(4, self.rows, LANES)

    def _chip_sum(self, n, from_chips):
        self.sums[n] = _sum_parts(self.pair, self.chip, from_chips, F32, f"{self.tag}chip_sum{n}",
                                  own_row0=self.chunks[n][0])

    def job(self, key):
        if key == self.swap_key:
            return self._swap_job()
        if key in self.carriers[:len(self.chunks)]:
            return _ExchangeJob(self.pair, *self.chunks[self.carriers.index(key)])
        return None

    def done(self, key, outs):
        if key == self.swap_key:
            self._pair_sum(outs[0])
        else:
            self._chip_sum(self.carriers.index(key), outs[0])

    def run(self):
        self._pair_sum(_run_job(self._swap_job(), self.tag + "reduce_pair")[0])
        self.chunks, self.sums = [(0, self.rows)], [None]
        self._chip_sum(0, _run_job(_ExchangeJob(self.pair, 0, self.rows), self.tag + "reduce_chips")[0])
        return self.result()

    def result(self):
        return (self.sums[0] if len(self.sums) == 1 else jnp.concatenate(self.sums, axis=0)), self.spans


class _Plans:
    def __init__(self, plans):
        self.plans = plans

    def job(self, key):
        self.owner = next((p for p in self.plans if p.job(key) is not None), None)
        return self.owner.job(key) if self.owner is not None else None

    def done(self, key, outs):
        self.owner.done(key, outs)


class _Schedule:
    EARLY = ("w_gu", "w_down") + SQUARE
    EARLY_PAD = 512

    def __init__(self, w, place):
        self.w, self.place = w, place
        self.depth = w["w_in"].shape[0]
        self.full = [{} for _ in range(self.depth)]
        self.reduce = {}
        self.pieces = [None] * self.depth
        self.active = []

    def arrived(self, layer, names, gathered):
        self.full[layer].update(_full_weights(names, gathered, self.w))

    def plan_fwd(self, l):
        nxt = l + 1
        more = nxt < self.depth
        if l == 0:
            assign = dict(proj=[(0, ("w_gu",) + SQUARE)], attn=[(0, ("w_down",))])
            if more:
                assign["attn"].append((nxt, ("w_in",)))
                assign.update(gu=[(nxt, ("w_gu",))], x2=[(nxt, ("w_down",))])
        else:
            late = [(l, SQUARE)] if l == 1 else []
            assign = dict(attn=late)
            if more:
                assign.update(proj=[(nxt, ("w_in",))], gu=[(nxt, ("w_down", "w_attn_out", "w_o"))])
                assign["attn"] = late + [(nxt, ("w_gu",) if late else ("w_gu", "w_a_out"))]
                assign["x2"] = [(nxt, ("w_pool", "w_a_out") if late else ("w_pool",))]
            if not assign["attn"]:
                del assign["attn"]
        return _Prefetch(self, assign)

    def weights(self, l):
        return self.full[l]

    def plan_bwd(self, l):
        self.active = [self.reduce[l + 1]] if l + 1 in self.reduce else []
        return _Plans(self.active)

    def early_grads(self, l, g):
        if l == 0:
            split, spans = _split_grads(g, self.w, self.EARLY, self.EARLY_PAD)
            self.early = _Reduce(split, spans, self.place, "l0_early_", swap_key="datt", carriers=("attn_b",),
                                 chunk_rows=(split.shape[1],))
            self.active.append(self.early)

    def grads(self, l, g):
        if l + 1 in self.reduce:
            self.pieces[l + 1] = [self.reduce[l + 1].result()]
        if l == 0:
            last = _Reduce(*_split_grads(g, self.w, ("w_in",)), self.place, "l0_")
            self.pieces[0] = [self.early.result(), last.run()]
        else:
            self.reduce[l] = _Reduce(*_split_grads(g, self.w), self.place, f"l{l}_")


SMALL_ROWS = 32


def _pack_small(gs, L, D):
    rows = [gs["pool_scale"].reshape(L, D), gs["g_mix"].reshape(L, D), gs["g_ffn"].reshape(L, D),
            gs["g_final"].reshape(1, D), gs["conv_w"][:, :3].reshape(3 * L, D),
            jnp.pad(gs["attn_sink"].reshape(1, -1), ((0, 0), (0, D - L * N_HEADS))),
            jnp.pad(gs["rel_bias"].reshape(1, -1), ((0, 0), (0, D - N_BUCKETS * N_HEADS)))]
    a = jnp.concatenate(rows, axis=0)
    return jnp.pad(a, ((0, SMALL_ROWS - a.shape[0]), (0, 0)))


def _unpack_small(a, L, D):
    g = {}
    g["pool_scale"] = a[0:L]
    g["g_mix"] = a[L:2 * L]
    g["g_ffn"] = a[2 * L:3 * L]
    g["g_final"] = a[3 * L]
    g["conv_w"] = a[3 * L + 1:6 * L + 1].reshape(L, 3, 1, D)
    g["attn_sink"] = a[6 * L + 1, :L * N_HEADS].reshape(L, N_HEADS)
    g["rel_bias"] = a[6 * L + 2, :N_BUCKETS * N_HEADS].reshape(N_BUCKETS, N_HEADS)
    return g


WEIGHTS = ("w_in", "conv_w", "w_a_out", "w_pool", "pool_scale", "w_attn_out", "attn_sink", "w_o", "g_mix", "g_ffn",
           "w_gu", "w_down", "rel_bias", "g_final")


def kernel(x, w_in, conv_w, w_a_out, w_pool, pool_scale, w_attn_out, attn_sink, w_o, g_mix, g_ffn, w_gu, w_down, rel_bias, g_final, loss_target, m_w_in, m_conv_w, m_w_a_out, m_w_pool, m_pool_scale, m_w_attn_out, m_attn_sink, m_w_o, m_g_mix, m_g_ffn, m_w_gu, m_w_down, m_rel_bias, m_g_final, v_w_in, v_conv_w, v_w_a_out, v_w_pool, v_pool_scale, v_w_attn_out, v_attn_sink, v_w_o, v_g_mix, v_g_ffn, v_w_gu, v_w_down, v_rel_bias, v_g_final):
    w = dict(w_in=w_in, conv_w=conv_w, w_a_out=w_a_out, w_pool=w_pool, pool_scale=pool_scale, w_attn_out=w_attn_out,
             attn_sink=attn_sink, w_o=w_o, g_mix=g_mix, g_ffn=g_ffn, w_gu=w_gu, w_down=w_down, rel_bias=rel_bias,
             g_final=g_final)
    m = dict(w_in=m_w_in, conv_w=m_conv_w, w_a_out=m_w_a_out, w_pool=m_w_pool, pool_scale=m_pool_scale,
             w_attn_out=m_w_attn_out, attn_sink=m_attn_sink, w_o=m_w_o, g_mix=m_g_mix, g_ffn=m_g_ffn, w_gu=m_w_gu,
             w_down=m_w_down, rel_bias=m_rel_bias, g_final=m_g_final)
    v = dict(w_in=v_w_in, conv_w=v_conv_w, w_a_out=v_w_a_out, w_pool=v_w_pool, pool_scale=v_pool_scale,
             w_attn_out=v_w_attn_out, attn_sink=v_attn_sink, w_o=v_w_o, g_mix=v_g_mix, g_ffn=v_g_ffn, w_gu=v_w_gu,
             w_down=v_w_down, rel_bias=v_rel_bias, g_final=v_g_final)
    T, D = x.shape[1], x.shape[2]
    L = w_in.shape[0]
    cx, cy, cc = _place()

    schedule = _Schedule(w, (cx, cy, cc))
    schedule.arrived(0, ("w_in",), _run_job(_GatherJob(_shard_rows(w, 0)[:1]), "gather_w_in_l0"))
    cw = jnp.pad(conv_w.reshape(L * 3, -1), ((0, 16 - L * 3), (0, 0)))
    cw = _all_gather(cw, "gather_conv_w")
    cw = jnp.transpose(cw, (1, 0, 2)).reshape(16, -1)[:L * 3].reshape(L, 3, -1)
    small = dict(conv_w=jnp.pad(cw, ((0, 0), (0, 5), (0, 0))), pool_scale=pool_scale.reshape(L, 1, D),
                 g_mix=g_mix.reshape(L, 1, D), g_ffn=g_ffn.reshape(L, 1, D), attn_sink=attn_sink,
                 rel_bias=rel_bias, g_final=g_final.reshape(1, D))

    loss, dx, _, gs = _local_step(x[0], loss_target[0], [schedule.full[0]], small, schedule)
    loss = lax.psum(loss[0, 0], ("x", "y", "c"))
    grads = _own_grads(schedule.pieces, w)

    small_all = _all_gather(_pack_small(gs, L, D), "gather_small")
    small_sum = _sum_parts(small_all, jnp.int32(0), small_all[1:], F32, "small_sum")
    gsm = _unpack_small(small_sum, L, D)
    W8 = D // N_DEV
    dev = 4 * cx + 2 * cy + cc
    gsm["conv_w"] = lax.dynamic_slice_in_dim(gsm["conv_w"], dev * W8, W8, axis=3)
    grads.update(gsm)

    deltas, new_m, new_v = {}, {}, {}
    for name in WEIGHTS:
        deltas[name], new_m[name], new_v[name] = _adamw(w[name], grads[name], m[name], v[name], "adamw_" + name)

    return (loss, dx[None], *[grads[n] for n in WEIGHTS], *[deltas[n] for n in WEIGHTS],
            *[new_m[n] for n in WEIGHTS], *[new_v[n] for n in WEIGHTS])
```

```python
import functools
import math

import jax
import jax.numpy as jnp
from jax import lax
from jax.experimental import pallas as pl
from jax.experimental.pallas import tpu as pltpu

F32 = jnp.float32
BF16 = jnp.bfloat16
MESH = pl.DeviceIdType.MESH

N_DEV = 8
N_HEADS = 16
N_KV_HEADS = 4
HEAD_DIM = 64
GROUP = N_HEADS // N_KV_HEADS
BLOCK = 128
WINDOW = 128
N_BUCKETS = 32
MAX_DISTANCE = 128
POOL_WINDOWS = (2, 4, 8, 16)
POOL_GROUPS = 4
HALO = 8
EPS = 1e-6
NEG_INF = -1e30

ADAM_LR = 0.001
ADAM_B1 = 0.9
ADAM_B2 = 0.999
ADAM_EPS = 1e-08
ADAM_WD = 0.01
ADAM_STEP = 10

LANES = 1024
VMEM_LIMIT_BYTES = 48 * 1024 * 1024


def _params(*sem):
    return pltpu.CompilerParams(dimension_semantics=sem, vmem_limit_bytes=VMEM_LIMIT_BYTES)


def _tile(n, cap):
    if n <= cap:
        return n
    for t in range(cap - cap % 128, 0, -128):
        if n % t == 0:
            return t
    raise ValueError(f"no tile for {n}")


WIDE_TILE = 2176

_DIMS = {"nn": (((1,), (0,)), ((), ())), "nt": (((1,), (1,)), ((), ())), "tn": (((0,), (0,)), ((), ()))}


HBM_SPEC = pl.BlockSpec(memory_space=pltpu.HBM)
ANY_SPEC = pl.BlockSpec(memory_space=pl.ANY)


def _matmul(a, b, mode, out_dtype, name, res=None, tm_cap=1024, tn_cap=1024, tk_cap=1024, comm=None):
    if mode == "tn":
        K, M = a.shape
    else:
        M, K = a.shape
    N = b.shape[0] if mode == "nt" else b.shape[1]
    tm, tn, tk = _tile(M, tm_cap), _tile(N, tn_cap), _tile(K, tk_cap)
    nk = K // tk
    a_spec = pl.BlockSpec((tk, tm), lambda i, j, k: (k, i)) if mode == "tn" else pl.BlockSpec((tm, tk), lambda i, j, k: (i, k))
    b_spec = pl.BlockSpec((tn, tk), lambda i, j, k: (j, k)) if mode == "nt" else pl.BlockSpec((tk, tn), lambda i, j, k: (k, j))
    o_spec = pl.BlockSpec((tm, tn), lambda i, j, k: (i, j))
    dims = _DIMS[mode]
    has_res = res is not None
    gm, gn = M // tm, N // tn
    steps = gm * gn * nk
    n_in = 2 + has_res
    n_ci = len(comm.ins) if comm is not None else 0
    n_co = len(comm.outs) if comm is not None else 0

    def body(*refs):
        a_ref, b_ref = refs[0], refs[1]
        res_ref = refs[2] if has_res else None
        comm_in = refs[n_in:n_in + n_ci]
        o_ref = refs[n_in + n_ci]
        comm_out = refs[n_in + n_ci + 1:n_in + n_ci + 1 + n_co]
        acc_ref = refs[n_in + n_ci + 1 + n_co]
        sems = refs[n_in + n_ci + 2 + n_co:]
        k = pl.program_id(2)
        step = (pl.program_id(0) * gn + pl.program_id(1)) * nk + k
        if comm is not None:
            @pl.when(step == 0)
            def _():
                comm.start(comm_in, comm_out, sems)

        part = lax.dot_general(a_ref[...], b_ref[...], dims, preferred_element_type=F32)

        @pl.when(k == 0)
        def _():
            acc_ref[...] = part

        @pl.when(k > 0)
        def _():
            acc_ref[...] += part

        @pl.when(k == nk - 1)
        def _():
            out = acc_ref[...]
            if has_res:
                out = out + res_ref[...]
            o_ref[...] = out.astype(out_dtype)

        if comm is not None:
            @pl.when(step == (7 * steps) // 8)
            def _():
                comm.mid(comm_in, comm_out, sems)

            @pl.when(step == steps - 1)
            def _():
                comm.finish(comm_in, comm_out, sems)

    in_specs = [a_spec, b_spec] + ([o_spec] if has_res else [])
    args = (a, b) + ((res,) if has_res else ())
    out_shape = jax.ShapeDtypeStruct((M, N), out_dtype)
    if comm is None:
        return pl.pallas_call(
            body, name=name, grid=(gm, gn, nk), in_specs=in_specs, out_specs=o_spec, out_shape=out_shape,
            scratch_shapes=[pltpu.VMEM((tm, tn), F32)],
            compiler_params=_params("parallel", "parallel", "arbitrary"),
        )(*args)
    outs = pl.pallas_call(
        body, name=name, grid=(gm, gn, nk),
        in_specs=in_specs + [HBM_SPEC] * n_ci, out_specs=[o_spec] + [HBM_SPEC] * n_co,
        out_shape=[out_shape] + list(comm.outs),
        scratch_shapes=[pltpu.VMEM((tm, tn), F32)] + list(comm.sems),
        compiler_params=_params("arbitrary", "arbitrary", "arbitrary"),
    )(*args, *comm.ins)
    return outs[0], outs[1:]


def _pool_mm(a, w, mode, out_dtype, name):
    T = a.shape[0]
    G = POOL_GROUPS
    cg = a.shape[1] // G
    tm = _tile(T, 1024)
    nt = T // tm
    dims = _DIMS[mode]
    if mode == "tn":
        def body(a_ref, d_ref, o_ref):
            part = lax.dot_general(a_ref[...], d_ref[...], dims, preferred_element_type=F32)

            @pl.when(pl.program_id(1) == 0)
            def _():
                o_ref[...] = part

            @pl.when(pl.program_id(1) > 0)
            def _():
                o_ref[...] += part

        return pl.pallas_call(
            body, name=name, grid=(G, nt),
            in_specs=[pl.BlockSpec((tm, cg), lambda g, i: (i, g)), pl.BlockSpec((tm, cg), lambda g, i: (i, g))],
            out_specs=pl.BlockSpec((None, cg, cg), lambda g, i: (g, 0, 0)),
            out_shape=jax.ShapeDtypeStruct((G, cg, cg), F32),
            compiler_params=_params("parallel", "arbitrary"),
        )(a, w)

    def body(a_ref, w_ref, o_ref):
        o_ref[...] = lax.dot_general(a_ref[...], w_ref[...], dims, preferred_element_type=F32).astype(out_dtype)

    return pl.pallas_call(
        body, name=name, grid=(G, nt),
        in_specs=[pl.BlockSpec((tm, cg), lambda g, i: (i, g)), pl.BlockSpec((None, cg, cg), lambda g, i: (g, 0, 0))],
        out_specs=pl.BlockSpec((tm, cg), lambda g, i: (i, g)),
        out_shape=jax.ShapeDtypeStruct((T, G * cg), out_dtype),
        compiler_params=_params("parallel", "parallel"),
    )(a, w)


ROWS = 256
HALO_BLOCK = 16


def _row_spec(d, col=0, rows=ROWS):
    return pl.BlockSpec((rows, d), lambda i, col=col: (i, col))


def _const_spec(shape):
    return pl.BlockSpec(shape, lambda *_: (0,) * len(shape))


def _rms_fwd(x, g, name):
    T, D = x.shape

    def body(x_ref, g_ref, h_ref):
        xv = x_ref[...]
        r = lax.rsqrt(jnp.mean(xv * xv, axis=-1, keepdims=True) + EPS)
        h_ref[...] = (xv * r * g_ref[...]).astype(BF16)

    return pl.pallas_call(
        body, name=name, grid=(T // ROWS,),
        in_specs=[_row_spec(D), _const_spec((1, D))], out_specs=_row_spec(D),
        out_shape=jax.ShapeDtypeStruct((T, D), BF16), compiler_params=_params("parallel"),
    )(x, g)


def _accumulate(ref, part):
    first = pl.program_id(0) == 0

    @pl.when(first)
    def _():
        ref[...] = part

    @pl.when(jnp.logical_not(first))
    def _():
        ref[...] += part


def _rms_bwd(x, g, dh, dres, name):
    T, D = x.shape

    def body(x_ref, g_ref, dh_ref, dres_ref, dx_ref, dxb_ref, dg_ref):
        xv = x_ref[...]
        r = lax.rsqrt(jnp.mean(xv * xv, axis=-1, keepdims=True) + EPS)
        xhat = xv * r
        dh_v = dh_ref[...]
        dxhat = dh_v * g_ref[...]
        dx = dres_ref[...] + r * (dxhat - xhat * jnp.mean(dxhat * xhat, axis=-1, keepdims=True))
        dx_ref[...] = dx
        dxb_ref[...] = dx.astype(BF16)
        _accumulate(dg_ref, jnp.sum(dh_v * xhat, axis=0, keepdims=True))

    return pl.pallas_call(
        body, name=name, grid=(T // ROWS,),
        in_specs=[_row_spec(D), _const_spec((1, D)), _row_spec(D), _row_spec(D)],
        out_specs=[_row_spec(D), _row_spec(D), _const_spec((1, D))],
        out_shape=[jax.ShapeDtypeStruct((T, D), F32), jax.ShapeDtypeStruct((T, D), BF16),
                   jax.ShapeDtypeStruct((1, D), F32)],
        compiler_params=_params("arbitrary"),
    )(x, g, dh, dres)


def _loss_head(x, g, target, name):
    T, D = x.shape

    def body(x_ref, g_ref, t_ref, loss_ref, dx_ref, dxb_ref, dg_ref):
        xv = x_ref[...]
        gv = g_ref[...]
        r = lax.rsqrt(jnp.mean(xv * xv, axis=-1, keepdims=True) + EPS)
        xhat = xv * r
        err = xhat * gv - t_ref[...]
        loss = 0.5 * jnp.sum(jnp.mean(err * err, axis=-1, keepdims=True), axis=0, keepdims=True)
        dy = err * (1.0 / D)
        dxhat = dy * gv
        dx = r * (dxhat - xhat * jnp.mean(dxhat * xhat, axis=-1, keepdims=True))
        dx_ref[...] = dx
        dxb_ref[...] = dx.astype(BF16)
        _accumulate(loss_ref, loss)
        _accumulate(dg_ref, jnp.sum(dy * xhat, axis=0, keepdims=True))

    return pl.pallas_call(
        body, name=name, grid=(T // ROWS,),
        in_specs=[_row_spec(D), _const_spec((1, D)), _row_spec(D)],
        out_specs=[_const_spec((1, 1)), _row_spec(D), _row_spec(D), _const_spec((1, D))],
        out_shape=[jax.ShapeDtypeStruct((1, 1), F32), jax.ShapeDtypeStruct((T, D), F32),
                   jax.ShapeDtypeStruct((T, D), BF16), jax.ShapeDtypeStruct((1, D), F32)],
        compiler_params=_params("arbitrary"),
    )(x, g, target)


def _halo_specs(d, col, n_blocks):
    per = ROWS // HALO_BLOCK
    last = n_blocks * per - 1
    prev = pl.BlockSpec((HALO_BLOCK, d), lambda i, col=col: (jnp.maximum(i * per - 1, 0), col))
    nxt = pl.BlockSpec((HALO_BLOCK, d), lambda i, col=col: (jnp.minimum((i + 1) * per, last), col))
    return prev, nxt


def _with_halo(prev, cur, nxt, n_blocks):
    i = pl.program_id(0)
    prev = jnp.where(i > 0, prev[HALO_BLOCK - HALO:], 0.0)
    nxt = jnp.where(i < n_blocks - 1, nxt[:HALO], 0.0)
    return jnp.concatenate([prev, cur, nxt], axis=0)


def _f32(ref):
    return ref[...].astype(F32)


def _shift(ext, k):
    n = ext.shape[0]
    v = ext if k == 0 else pltpu.roll(ext, (-k) % n, 0)
    return v[HALO:HALO + ROWS]


def _shift_full(ext, k):
    n = ext.shape[0]
    return pltpu.roll(ext, (-k) % n, 0)


def _pool_counts(T):
    n = ROWS + 2 * HALO
    t = pl.program_id(0) * ROWS - HALO + lax.broadcasted_iota(jnp.int32, (n, 1), 0)
    out = []
    for w in POOL_WINDOWS:
        lo = jnp.maximum(t - w // 2, 0)
        hi = jnp.minimum(t + (w - 1 - w // 2), T - 1)
        out.append(jnp.maximum(hi - lo + 1, 1).astype(F32))
    return out


def _window_sums(e, sign):
    s2 = e + _shift_full(e, -sign)
    s4 = _shift_full(s2, -1) + _shift_full(s2, 1)
    s8 = _shift_full(s4, -2) + _shift_full(s4, 2)
    s16 = _shift_full(s8, -4) + _shift_full(s8, 4)
    return s2, s4, s8, s16


def _mixer_fwd(proj, conv_w, name):
    T = proj.shape[0]
    W = conv_w.shape[1]
    nb = T // ROWS
    cg = W // POOL_GROUPS

    def body(b_ref, c_ref, x_ref, u_ref, cp_ref, cn_ref, xp_ref, xn_ref, up_ref, un_ref, w_ref, z_ref, p_ref):
        uc = _with_halo(_f32(cp_ref) * _f32(xp_ref), _f32(c_ref) * _f32(x_ref), _f32(cn_ref) * _f32(xn_ref), nb)
        w0, w1, w2 = w_ref[0:1, :], w_ref[1:2, :], w_ref[2:3, :]
        y = w0 * _shift(uc, -1) + w1 * _shift(uc, 0) + w2 * _shift(uc, 1)
        z_ref[...] = (_f32(b_ref) * y).astype(BF16)
        e = _with_halo(_f32(up_ref), _f32(u_ref), _f32(un_ref), nb)
        counts = _pool_counts(T)
        for gi in range(POOL_GROUPS):
            eg = e[:, gi * cg:(gi + 1) * cg]
            s = _window_sums(eg, 1)[gi]
            p = s[HALO:HALO + ROWS] / counts[gi][HALO:HALO + ROWS] - eg[HALO:HALO + ROWS]
            p_ref[:, gi * cg:(gi + 1) * cg] = p.astype(BF16)

    halo = [s for col in (1, 2, 3) for s in _halo_specs(W, col, nb)]
    return pl.pallas_call(
        body, name=name, grid=(nb,),
        in_specs=[_row_spec(W, 0), _row_spec(W, 1), _row_spec(W, 2), _row_spec(W, 3)] + halo + [_const_spec((8, W))],
        out_specs=[_row_spec(W), _row_spec(W)],
        out_shape=[jax.ShapeDtypeStruct((T, W), BF16), jax.ShapeDtypeStruct((T, W), BF16)],
        compiler_params=_params("parallel"),
    )(proj, proj, proj, proj, proj, proj, proj, proj, proj, proj, conv_w)


def _mixer_bwd(proj, conv_w, dz, dp, dproj, name):
    T = proj.shape[0]
    W = conv_w.shape[1]
    nb = T // ROWS
    cg = W // POOL_GROUPS

    def body(b_ref, c_ref, x_ref, dz_ref, dp_ref,
             bp_ref, bn_ref, cp_ref, cn_ref, xp_ref, xn_ref, dzp_ref, dzn_ref, dpp_ref, dpn_ref, w_ref, _,
             o_ref, dw_ref):
        cv, xv, dzv = _f32(c_ref), _f32(x_ref), _f32(dz_ref)
        uc = _with_halo(_f32(cp_ref) * _f32(xp_ref), cv * xv, _f32(cn_ref) * _f32(xn_ref), nb)
        dy = _with_halo(_f32(dzp_ref) * _f32(bp_ref), dzv * _f32(b_ref), _f32(dzn_ref) * _f32(bn_ref), nb)
        w0, w1, w2 = w_ref[0:1, :], w_ref[1:2, :], w_ref[2:3, :]
        um, u0, up = _shift(uc, -1), _shift(uc, 0), _shift(uc, 1)
        o_ref[:, 0:W] = (dzv * (w0 * um + w1 * u0 + w2 * up)).astype(BF16)
        dy0 = _shift(dy, 0)
        duc = w0 * _shift(dy, 1) + w1 * dy0 + w2 * _shift(dy, -1)
        o_ref[:, W:2 * W] = (duc * xv).astype(BF16)
        o_ref[:, 2 * W:3 * W] = (duc * cv).astype(BF16)
        row = lax.broadcasted_iota(jnp.int32, (8, W), 0)
        dw = jnp.where(row == 0, jnp.sum(dy0 * um, axis=0, keepdims=True),
                       jnp.where(row == 1, jnp.sum(dy0 * u0, axis=0, keepdims=True),
                                 jnp.where(row == 2, jnp.sum(dy0 * up, axis=0, keepdims=True), 0.0)))
        _accumulate(dw_ref, dw)
        d = _with_halo(_f32(dpp_ref), _f32(dp_ref), _f32(dpn_ref), nb)
        counts = _pool_counts(T)
        for gi in range(POOL_GROUPS):
            dg = d[:, gi * cg:(gi + 1) * cg]
            s = _window_sums(dg / counts[gi], -1)[gi]
            o_ref[:, 3 * W + gi * cg:3 * W + (gi + 1) * cg] = (s[HALO:HALO + ROWS] - dg[HALO:HALO + ROWS]).astype(BF16)

    def halo(col):
        return list(_halo_specs(W, col, nb))

    return pl.pallas_call(
        body, name=name, grid=(nb,),
        in_specs=[_row_spec(W, 0), _row_spec(W, 1), _row_spec(W, 2), _row_spec(W), _row_spec(W)]
        + halo(0) + halo(1) + halo(2) + halo(0) + halo(0) + [_const_spec((8, W)), ANY_SPEC],
        out_specs=[_row_spec(4 * W), _const_spec((8, W))],
        out_shape=[jax.ShapeDtypeStruct(dproj.shape, BF16), jax.ShapeDtypeStruct((8, W), F32)],
        input_output_aliases={16: 0}, compiler_params=_params("arbitrary"),
    )(proj, proj, proj, dz, dp, proj, proj, proj, proj, proj, proj, dz, dz, dp, dp, conv_w, dproj)


def _t5_bucket(rel):
    half = N_BUCKETS // 2
    max_exact = half // 2
    ret = jnp.where(rel > 0, half, 0)
    n = jnp.abs(rel)
    nf = jnp.maximum(n, 1).astype(jnp.float32)
    large = max_exact + (jnp.log(nf / max_exact) / math.log(MAX_DISTANCE / max_exact)
                         * (half - max_exact)).astype(jnp.int32)
    large = jnp.minimum(large, half - 1)
    return ret + jnp.where(n < max_exact, n, large)


def _bucket_table():
    qi = jnp.arange(BLOCK)[:, None]
    kj = jnp.arange(3 * BLOCK)[None, :]
    rel = kj - BLOCK - qi
    return jnp.where(jnp.abs(rel) <= WINDOW, _t5_bucket(rel), -1).astype(jnp.int32)


def _bias_table(rel_bias, bucket, name):
    def body(rb_ref, bucket_ref, o_ref):
        h = pl.program_id(0)
        bk = bucket_ref[...]
        acc = jnp.full(bk.shape, NEG_INF, F32)
        for b in range(N_BUCKETS):
            acc = jnp.where(bk == b, rb_ref[b, h], acc)
        o_ref[...] = acc

    return pl.pallas_call(
        body, name=name, grid=(N_HEADS,),
        in_specs=[pl.BlockSpec(memory_space=pltpu.SMEM), _const_spec((BLOCK, 3 * BLOCK))],
        out_specs=pl.BlockSpec((None, BLOCK, 3 * BLOCK), lambda h: (h, 0, 0)),
        out_shape=jax.ShapeDtypeStruct((N_HEADS, BLOCK, 3 * BLOCK), F32),
        compiler_params=_params("parallel"),
    )(rel_bias, bucket)


def _bias_grad(ds_sum, bucket, name):
    def body(ds_ref, bucket_ref, o_ref):
        bk = bucket_ref[...]
        ds = ds_ref[...]
        row = lax.broadcasted_iota(jnp.int32, (N_BUCKETS, 128), 0)
        acc = jnp.zeros((N_BUCKETS, 128), F32)
        for b in range(N_BUCKETS):
            s = jnp.sum(jnp.sum(jnp.where(bk == b, ds, 0.0), axis=1, keepdims=True), axis=0, keepdims=True)
            acc = jnp.where(row == b, s, acc)
        o_ref[...] = acc

    return pl.pallas_call(
        body, name=name, grid=(N_HEADS,),
        in_specs=[pl.BlockSpec((None, BLOCK, 3 * BLOCK), lambda h: (h, 0, 0)), _const_spec((BLOCK, 3 * BLOCK))],
        out_specs=pl.BlockSpec((None, N_BUCKETS, 128), lambda h: (h, 0, 0)),
        out_shape=jax.ShapeDtypeStruct((N_HEADS, N_BUCKETS, 128), F32),
        compiler_params=_params("parallel"),
    )(ds_sum, bucket)


PAIR = 2 * HEAD_DIM
Q_BLOCKS_FWD = 4
Q_BLOCKS_BWD = 2


def _low_half(shape):
    return lax.broadcasted_iota(jnp.int32, shape, len(shape) - 1) % PAIR < HEAD_DIM


def _split_pair(a):
    low = _low_half(a.shape)
    zero = jnp.zeros_like(a)
    return jnp.concatenate([jnp.where(low, a, zero), jnp.where(low, zero, a)], axis=0)


def _kv_expand(proj, kv_off, name):
    T = proj.shape[0]
    kv_w = N_KV_HEADS * HEAD_DIM
    rows = _tile(T, 512)

    def body(k_ref, v_ref, ke_ref, ve_ref):
        for src, dst in ((k_ref, ke_ref), (v_ref, ve_ref)):
            for g in range(N_KV_HEADS // 2):
                x = src[:, g * PAIR:(g + 1) * PAIR].astype(F32)
                swapped = pltpu.roll(x, HEAD_DIM, 1)
                low = _low_half(x.shape)
                dst[:, 2 * g * PAIR:(2 * g + 1) * PAIR] = jnp.where(low, x, swapped).astype(BF16)
                dst[:, (2 * g + 1) * PAIR:(2 * g + 2) * PAIR] = jnp.where(low, swapped, x).astype(BF16)

    out = jax.ShapeDtypeStruct((T, N_KV_HEADS * PAIR), BF16)
    ospec = pl.BlockSpec((rows, N_KV_HEADS * PAIR), lambda i: (i, 0))
    return pl.pallas_call(
        body, name=name, grid=(T // rows,),
        in_specs=[pl.BlockSpec((rows, kv_w), lambda i: (i, kv_off // kv_w)),
                  pl.BlockSpec((rows, kv_w), lambda i: (i, kv_off // kv_w + 1))],
        out_specs=[ospec, ospec], out_shape=[out, out], compiler_params=_params("parallel"),
    )(proj, proj)


def _kv_fold(dke, dve, dproj, kv_off, name):
    T = dke.shape[1]
    kv_w = N_KV_HEADS * HEAD_DIM
    rows = _tile(T, 512)

    def body(dk_ref, dv_ref, _, o_ref):
        for n, src in enumerate((dk_ref, dv_ref)):
            for g in range(N_KV_HEADS // 2):
                a = src[2 * g * PAIR:(2 * g + 1) * PAIR, :].T
                b = src[(2 * g + 1) * PAIR:(2 * g + 2) * PAIR, :].T
                a = a + pltpu.roll(a, HEAD_DIM, 1)
                b = b + pltpu.roll(b, HEAD_DIM, 1)
                o_ref[:, n * kv_w + g * PAIR:n * kv_w + (g + 1) * PAIR] = jnp.where(_low_half(a.shape), a, b).astype(BF16)

    ispec = pl.BlockSpec((N_KV_HEADS * PAIR, rows), lambda i: (0, i))
    return pl.pallas_call(
        body, name=name, grid=(T // rows,), in_specs=[ispec, ispec, ANY_SPEC],
        out_specs=pl.BlockSpec((rows, 2 * kv_w), lambda i: (i, kv_off // (2 * kv_w))),
        out_shape=jax.ShapeDtypeStruct(dproj.shape, BF16), input_output_aliases={2: 0},
        compiler_params=_params("parallel"),
    )(dke, dve, dproj)


def _key_blocks(i, nb):
    return [pl.multiple_of(n * BLOCK, BLOCK) for n in (jnp.maximum(i - 1, 0), i, jnp.minimum(i + 1, nb - 1))]


def _three_blocks(ref, starts):
    return jnp.concatenate([ref[pl.ds(s, BLOCK), :] for s in starts], axis=0)


def _group_scores(q_ref, rows, kd, bias_ref, i, nb):
    qq = jnp.concatenate([_split_pair(q_ref[rows, pr * PAIR:(pr + 1) * PAIR]) for pr in range(GROUP // 2)], axis=0)
    qq = qq * (HEAD_DIM ** -0.5)
    s = lax.dot_general(qq, kd, _DIMS["nt"], preferred_element_type=F32)
    s = s + bias_ref[...].reshape(GROUP * BLOCK, 3 * BLOCK)
    kj = lax.broadcasted_iota(jnp.int32, (1, 3 * BLOCK), 1)
    outside = jnp.logical_or(jnp.logical_and(i == 0, kj < BLOCK), jnp.logical_and(i == nb - 1, kj >= 2 * BLOCK))
    return qq, jnp.where(outside, NEG_INF, s)


def _per_head_rows(values):
    head = lax.broadcasted_iota(jnp.int32, (GROUP * BLOCK, 1), 0) // BLOCK
    out = jnp.full((GROUP * BLOCK, 1), values[0], F32)
    for g in range(1, GROUP):
        out = jnp.where(head == g, values[g], out)
    return out


def _attn_specs(T, q_off, Q_BLOCKS):
    gw = GROUP * HEAD_DIM
    return dict(
        sink=pl.BlockSpec(memory_space=pltpu.SMEM),
        q=pl.BlockSpec((Q_BLOCKS * BLOCK, gw), lambda j, i: (i, q_off // gw + j)),
        kv=pl.BlockSpec((T, PAIR), lambda j, i: (0, j)),
        bias=pl.BlockSpec((GROUP, BLOCK, 3 * BLOCK), lambda j, i: (j, 0, 0)),
        o=pl.BlockSpec((Q_BLOCKS * BLOCK, gw), lambda j, i: (i, j)))


def _attn_fwd(proj, q_off, kexp, vexp, bias, sink, name, comm=None):
    T = proj.shape[0]
    nb = T // BLOCK
    Q_BLOCKS = min(Q_BLOCKS_FWD, nb)
    sp = _attn_specs(T, q_off, Q_BLOCKS)
    steps = N_KV_HEADS * (nb // Q_BLOCKS)
    n_ci = len(comm.ins) if comm is not None else 0
    n_co = len(comm.outs) if comm is not None else 0

    def body(*refs):
        sink_ref, q_ref, ke_ref, ve_ref, bias_ref = refs[:5]
        comm_in = refs[5:5 + n_ci]
        o_ref, lse_ref = refs[5 + n_ci:7 + n_ci]
        comm_out = refs[7 + n_ci:7 + n_ci + n_co]
        sems = refs[7 + n_ci + n_co:]
        j, i = pl.program_id(0), pl.program_id(1)
        step = j * (nb // Q_BLOCKS) + i
        if comm is not None:
            @pl.when(step == 0)
            def _():
                comm.start(comm_in, comm_out, sems)

        low = _low_half((BLOCK, PAIR))
        sk = _per_head_rows([sink_ref[GROUP * j + g] for g in range(GROUP)])
        for b in range(Q_BLOCKS):
            blk = i * Q_BLOCKS + b
            rows = slice(b * BLOCK, (b + 1) * BLOCK)
            starts = _key_blocks(blk, nb)
            kd = _three_blocks(ke_ref, starts)
            vv = _split_pair(_three_blocks(ve_ref, starts))
            _, s = _group_scores(q_ref, rows, kd, bias_ref, blk, nb)
            m = jnp.maximum(jnp.max(s, axis=-1, keepdims=True), sk)
            p = jnp.exp(s - m)
            denom = jnp.sum(p, axis=-1, keepdims=True) + jnp.exp(sk - m)
            p = (p / denom).astype(BF16)
            lse = m + jnp.log(denom)
            for pr in range(GROUP // 2):
                lanes = slice(pr * PAIR, (pr + 1) * PAIR)
                a, c = slice(2 * pr * BLOCK, (2 * pr + 1) * BLOCK), slice((2 * pr + 1) * BLOCK, (2 * pr + 2) * BLOCK)
                pp = jnp.concatenate([p[a], p[c]], axis=1)
                o_ref[rows, lanes] = lax.dot_general(pp, vv, _DIMS["nn"], preferred_element_type=F32).astype(BF16)
                lse_ref[rows, lanes] = jnp.where(low, lse[a], lse[c])

        if comm is not None:
            @pl.when(step == (7 * steps) // 8)
            def _():
                comm.mid(comm_in, comm_out, sems)

            @pl.when(step == steps - 1)
            def _():
                comm.finish(comm_in, comm_out, sems)

    out_shape = [jax.ShapeDtypeStruct((T, N_HEADS * HEAD_DIM), BF16), jax.ShapeDtypeStruct((T, N_HEADS * HEAD_DIM), F32)]
    in_specs = [sp["sink"], sp["q"], sp["kv"], sp["kv"], sp["bias"]]
    if comm is None:
        att, lse = pl.pallas_call(
            body, name=name, grid=(N_KV_HEADS, nb // Q_BLOCKS), in_specs=in_specs, out_specs=[sp["o"], sp["o"]],
            out_shape=out_shape, compiler_params=_params("parallel", "parallel"),
        )(sink, proj, kexp, vexp, bias)
        return att, lse, []
    outs = pl.pallas_call(
        body, name=name, grid=(N_KV_HEADS, nb // Q_BLOCKS),
        in_specs=in_specs + [HBM_SPEC] * n_ci, out_specs=[sp["o"], sp["o"]] + [HBM_SPEC] * n_co,
        out_shape=out_shape + list(comm.outs), scratch_shapes=list(comm.sems),
        compiler_params=_params("arbitrary", "arbitrary"),
    )(sink, proj, kexp, vexp, bias, *comm.ins)
    return outs[0], outs[1], outs[2:]


def _attn_bwd(proj, q_off, kexp, vexp, bias, sink, out, lse, dout, dproj, name, comm=None):
    T = proj.shape[0]
    nb = T // BLOCK
    Q_BLOCKS = min(Q_BLOCKS_BWD, nb)
    sp = _attn_specs(T, q_off, Q_BLOCKS)
    scale = HEAD_DIM ** -0.5
    steps = N_KV_HEADS * (nb // Q_BLOCKS)
    n_ci = len(comm.ins) if comm is not None else 0
    n_co = len(comm.outs) if comm is not None else 0

    def body(*refs):
        sink_ref, q_ref, ke_ref, ve_ref, bias_ref, o_ref, lse_ref, do_ref = refs[:8]
        comm_in = refs[9:9 + n_ci]
        dq_ref, dke_ref, dve_ref, ds_ref, dsink_ref = refs[9 + n_ci:14 + n_ci]
        comm_out = refs[14 + n_ci:14 + n_ci + n_co]
        sems = refs[14 + n_ci + n_co:]
        j, i = pl.program_id(0), pl.program_id(1)
        step = j * (nb // Q_BLOCKS) + i
        if comm is not None:
            @pl.when(step == 0)
            def _():
                comm.start(comm_in, comm_out, sems)

        @pl.when(i == 0)
        def _():
            dke_ref[...] = jnp.zeros(dke_ref.shape, F32)
            dve_ref[...] = jnp.zeros(dve_ref.shape, F32)
            ds_ref[...] = jnp.zeros(ds_ref.shape, F32)
            dsink_ref[...] = jnp.zeros(dsink_ref.shape, F32)

        low = _low_half((BLOCK, PAIR))
        for b in range(Q_BLOCKS):
            blk = i * Q_BLOCKS + b
            rows = slice(b * BLOCK, (b + 1) * BLOCK)
            starts = _key_blocks(blk, nb)
            kd = _three_blocks(ke_ref, starts)
            vd = _three_blocks(ve_ref, starts)
            kk = _split_pair(kd)
            qq, s = _group_scores(q_ref, rows, kd, bias_ref, blk, nb)
            lse_rows, deltas, dd = [], [], []
            for pr in range(GROUP // 2):
                lanes = slice(pr * PAIR, (pr + 1) * PAIR)
                l2 = lse_ref[rows, lanes]
                lse_rows += [jnp.max(jnp.where(low, l2, NEG_INF), axis=-1, keepdims=True),
                             jnp.max(jnp.where(low, NEG_INF, l2), axis=-1, keepdims=True)]
                do2 = do_ref[rows, lanes]
                prod = do2.astype(F32) * o_ref[rows, lanes].astype(F32)
                deltas += [jnp.sum(jnp.where(low, prod, 0.0), axis=-1, keepdims=True),
                           jnp.sum(jnp.where(low, 0.0, prod), axis=-1, keepdims=True)]
                dd.append(_split_pair(do2))
                head = GROUP * j + 2 * pr
                p_sink = jnp.exp(jnp.where(low, sink_ref[head], sink_ref[head + 1]) - l2)
                dsink_ref[:, lanes] += jnp.sum(-p_sink * jnp.where(low, deltas[-2], deltas[-1]), axis=0, keepdims=True)
            dd = jnp.concatenate(dd, axis=0)
            p = jnp.exp(s - jnp.concatenate(lse_rows, axis=0))
            dp = lax.dot_general(dd, vd, _DIMS["nt"], preferred_element_type=F32)
            ds = p * (dp - jnp.concatenate(deltas, axis=0))
            dsb = ds.astype(BF16)
            for pr in range(GROUP // 2):
                a, c = slice(2 * pr * BLOCK, (2 * pr + 1) * BLOCK), slice((2 * pr + 1) * BLOCK, (2 * pr + 2) * BLOCK)
                dq = lax.dot_general(jnp.concatenate([dsb[a], dsb[c]], axis=1), kk, _DIMS["nn"],
                                     preferred_element_type=F32) * scale
                dq_ref[rows, pr * PAIR:(pr + 1) * PAIR] = dq.astype(BF16)
            dk_acc = lax.dot_general(qq, dsb, _DIMS["tn"], preferred_element_type=F32)
            dv_acc = lax.dot_general(dd, p.astype(BF16), _DIMS["tn"], preferred_element_type=F32)
            ds_ref[...] += ds.reshape(GROUP, BLOCK, 3 * BLOCK)
            for t, start in enumerate(starts):
                dke_ref[:, pl.ds(start, BLOCK)] += dk_acc[:, t * BLOCK:(t + 1) * BLOCK]
                dve_ref[:, pl.ds(start, BLOCK)] += dv_acc[:, t * BLOCK:(t + 1) * BLOCK]

        if comm is not None:
            @pl.when(step == (7 * steps) // 8)
            def _():
                comm.mid(comm_in, comm_out, sems)

            @pl.when(step == steps - 1)
            def _():
                comm.finish(comm_in, comm_out, sems)

    kv_out = jax.ShapeDtypeStruct((N_KV_HEADS * PAIR, T), F32)
    kvt_spec = pl.BlockSpec((PAIR, T), lambda j, i: (j, 0))
    job_ins, job_outs, job_sems = (comm.ins, comm.outs, comm.sems) if comm is not None else ([], [], [])
    outs = pl.pallas_call(
        body, name=name, grid=(N_KV_HEADS, nb // Q_BLOCKS),
        in_specs=[sp["sink"], sp["q"], sp["kv"], sp["kv"], sp["bias"], sp["o"], sp["o"], sp["o"], ANY_SPEC]
        + [HBM_SPEC] * n_ci,
        out_specs=[sp["q"], kvt_spec, kvt_spec, sp["bias"],
                   pl.BlockSpec((1, GROUP * HEAD_DIM), lambda j, i: (0, j))] + [HBM_SPEC] * n_co,
        out_shape=[jax.ShapeDtypeStruct(dproj.shape, BF16), kv_out, kv_out,
                   jax.ShapeDtypeStruct((N_HEADS, BLOCK, 3 * BLOCK), F32),
                   jax.ShapeDtypeStruct((1, N_HEADS * HEAD_DIM), F32)] + list(job_outs),
        scratch_shapes=list(job_sems), input_output_aliases={8: 0},
        compiler_params=_params("arbitrary" if comm is not None else "parallel", "arbitrary"),
    )(sink, proj, kexp, vexp, bias, out, lse, dout, dproj, *job_ins)
    return outs[:5], outs[5:]


GATE_COLS = 512


def _sigmoid(x):
    return 1.0 / (1.0 + jnp.exp(-x))


def _gate_specs(D, gate_off):
    nc = D // GATE_COLS
    base = gate_off // GATE_COLS
    return [pl.BlockSpec((ROWS, GATE_COLS), lambda i, c=base + g * nc + h: (i, c)) for g in range(3) for h in range(nc)]


def _merge_fwd(proj, gate_off, ya, yp, yt, scale, name):
    T, D = ya.shape
    nc = D // GATE_COLS

    def body(*refs):
        gates = refs[:3 * nc]
        ya_ref, yp_ref, yt_ref, s_ref, o_ref = refs[3 * nc:]
        for h in range(nc):
            cols = slice(h * GATE_COLS, (h + 1) * GATE_COLS)
            merged = (_sigmoid(_f32(gates[h])) * ya_ref[:, cols].astype(F32)
                      + _sigmoid(_f32(gates[nc + h])) * (yp_ref[:, cols].astype(F32) * s_ref[:, cols])
                      + _sigmoid(_f32(gates[2 * nc + h])) * yt_ref[:, cols].astype(F32))
            o_ref[:, cols] = merged.astype(BF16)

    yspec = _row_spec(D)
    return pl.pallas_call(
        body, name=name, grid=(T // ROWS,),
        in_specs=_gate_specs(D, gate_off) + [yspec, yspec, yspec, _const_spec((1, D))], out_specs=yspec,
        out_shape=jax.ShapeDtypeStruct((T, D), BF16), compiler_params=_params("parallel"),
    )(*([proj] * (3 * nc)), ya, yp, yt, scale)


def _merge_bwd(proj, gate_off, ya, yp, yt, scale, dm, name):
    T, D = ya.shape
    nc = D // GATE_COLS
    base = gate_off // GATE_COLS

    def body(gate_ref, ya_ref, yp_ref, yt_ref, s_ref, dm_ref, dg_ref, dya_ref, dyp_ref, dyt_ref, ds_ref):
        i, n = pl.program_id(0), pl.program_id(1)
        sg = _sigmoid(_f32(gate_ref))
        for g, (y_ref, dy_ref) in enumerate(((ya_ref, dya_ref), (yp_ref, dyp_ref), (yt_ref, dyt_ref))):
            for h in range(nc):
                @pl.when(n == g * nc + h)
                def _(g=g, h=h, y_ref=y_ref, dy_ref=dy_ref):
                    cols = slice(h * GATE_COLS, (h + 1) * GATE_COLS)
                    dy = dm_ref[:, cols].astype(F32) * sg
                    y = y_ref[:, cols].astype(F32)
                    if g == 1:
                        s_v = s_ref[:, cols]
                        part = jnp.sum(dy * y, axis=0, keepdims=True)

                        @pl.when(i == 0)
                        def _():
                            ds_ref[:, cols] = part

                        @pl.when(i > 0)
                        def _():
                            ds_ref[:, cols] += part

                        y = y * s_v
                        dy_ref[:, cols] = (dy * s_v).astype(BF16)
                    else:
                        dy_ref[:, cols] = dy.astype(BF16)
                    dg_ref[...] = (dy * y * (1.0 - sg)).astype(BF16)

    rows = _tile(T, 4 * ROWS)
    yspec = pl.BlockSpec((rows, D), lambda i, n: (i, 0))
    gspec = pl.BlockSpec((rows, GATE_COLS), lambda i, n: (i, base + n))
    sspec = pl.BlockSpec((1, D), lambda i, n: (0, 0))
    out = jax.ShapeDtypeStruct((T, D), BF16)
    return pl.pallas_call(
        body, name=name, grid=(T // rows, 3 * nc),
        in_specs=[gspec, yspec, yspec, yspec, sspec, yspec],
        out_specs=[gspec, yspec, yspec, yspec, sspec],
        out_shape=[jax.ShapeDtypeStruct(proj.shape, BF16), out, out, out, jax.ShapeDtypeStruct((1, D), F32)],
        compiler_params=_params("arbitrary", "arbitrary"),
    )(proj, ya, yp, yt, scale, dm)


def _swiglu_fwd(gu, name):
    T = gu.shape[0]
    F = gu.shape[1] // 2

    def body(gu_ref, o_ref):
        g = gu_ref[:, 0:F].astype(F32)
        o_ref[...] = (g * _sigmoid(g) * gu_ref[:, F:2 * F].astype(F32)).astype(BF16)

    return pl.pallas_call(
        body, name=name, grid=(T // ROWS,), in_specs=[_row_spec(2 * F)], out_specs=_row_spec(F),
        out_shape=jax.ShapeDtypeStruct((T, F), BF16), compiler_params=_params("parallel"),
    )(gu)


def _swiglu_bwd(gu, dact, name):
    T = gu.shape[0]
    F = gu.shape[1] // 2

    def body(gu_ref, d_ref, o_ref):
        g, d = gu_ref[:, 0:F].astype(F32), d_ref[...].astype(F32)
        sg = _sigmoid(g)
        o_ref[:, 0:F] = (d * gu_ref[:, F:2 * F].astype(F32) * sg * (1.0 + g * (1.0 - sg))).astype(BF16)
        o_ref[:, F:2 * F] = (d * g * sg).astype(BF16)

    return pl.pallas_call(
        body, name=name, grid=(T // ROWS,), in_specs=[_row_spec(2 * F), _row_spec(F)], out_specs=_row_spec(2 * F),
        out_shape=jax.ShapeDtypeStruct((T, 2 * F), BF16), compiler_params=_params("parallel"),
    )(gu, dact)


def _carried(plan, key, *args, **kwargs):
    job = plan.job(key) if plan is not None else None
    if job is None:
        return _matmul(*args, **kwargs)
    out, extra = _matmul(*args, comm=job, **kwargs)
    plan.done(key, extra)
    return out


def _local_step(x, target, wts, small, hooks=None):
    T, D = x.shape
    depth = small["g_mix"].shape[0]
    wts = list(wts) + [None] * (depth - len(wts))
    gate_off = wts[0]["w_inT"].shape[0] - 3 * D
    q_off = 4 * D
    bucket = _bucket_table()
    bias = _bias_table(small["rel_bias"], bucket, "bias_table")

    saved = []
    for l in range(depth):
        n = f"l{l}_"
        if hooks is not None and l > 0:
            wts[l] = hooks.weights(l)
        w = wts[l]
        plan = hooks.plan_fwd(l) if hooks is not None else None
        h = _rms_fwd(x, small["g_mix"][l], n + "rms_mix")
        proj = _carried(plan, "proj", h, w["w_inT"], "nt", BF16, n + "proj", tn_cap=WIDE_TILE)
        z, p = _mixer_fwd(proj, small["conv_w"][l], n + "mixer")
        kexp, vexp = _kv_expand(proj, q_off + D, n + "kv_expand")
        sink = small["attn_sink"][l]
        job = plan.job("attn") if plan is not None else None
        att, lse, extra = _attn_fwd(proj, q_off, kexp, vexp, bias, sink, n + "attn", comm=job)
        if job is not None:
            plan.done("attn", extra)
        ya =_matmul(z, w["w_a_out"], "nn", BF16, n + "ya")
        yp = _pool_mm(p, w["w_pool"], "nn", BF16, n + "yp")
        yt = _matmul(att, w["w_attn_out"], "nn", BF16, n + "yt")
        merged = _merge_fwd(proj, gate_off, ya, yp, yt, small["pool_scale"][l], n + "merge")
        x1 = _matmul(merged, w["w_o"], "nn", F32, n + "x1", res=x)
        h2 = _rms_fwd(x1, small["g_ffn"][l], n + "rms_ffn")
        gu = _carried(plan, "gu", h2, w["w_guT"], "nt", BF16, n + "gu", tn_cap=WIDE_TILE)
        act = _swiglu_fwd(gu, n + "swiglu")
        ff = w["w_down"].shape[0]
        x2 = _carried(plan, "x2", act, w["w_down"], "nn", F32, n + "x2", res=x1, tn_cap=512, tk_cap=ff)
        saved.append(dict(x=x, h=h, proj=proj, z=z, p=p, kexp=kexp, vexp=vexp, sink=sink, lse=lse, att=att,
                          ya=ya, yp=yp, yt=yt, merged=merged, x1=x1, h2=h2, gu=gu, act=act))
        x = x2

    loss, dx, dxb, dg_final = _loss_head(x, small["g_final"], target, "loss_head")

    gw = [None] * depth
    gs = {k_: [None] * depth for k_ in ("conv_w", "pool_scale", "g_mix", "g_ffn", "attn_sink")}
    ds_total = None
    for l in reversed(range(depth)):
        n = f"l{l}_b_"
        s, w, g = saved[l], wts[l], {}
        plan = hooks.plan_bwd(l) if hooks is not None else None
        ff = w["w_down"].shape[0]
        g["w_down"] = _carried(plan, "dw_down", s["act"], dxb, "tn", BF16, n + "dw_down", tm_cap=ff, tk_cap=512)
        dact = _matmul(dxb, w["w_down"], "nt", BF16, n + "dact", tm_cap=512, tn_cap=ff)
        dgu = _swiglu_bwd(s["gu"], dact, n + "swiglu")
        g["w_guT"] = _carried(plan, "dw_gu", dgu, s["h2"], "tn", BF16, n + "dw_gu", tm_cap=WIDE_TILE)
        dh2 = _carried(plan, "dh2", dgu, w["w_guT"], "nn", F32, n + "dh2", tn_cap=512, tk_cap=ff)
        dx1, dx1b, gs["g_ffn"][l] = _rms_bwd(s["x1"], small["g_ffn"][l], dh2, dx, n + "rms_ffn")
        g["w_o"] = _matmul(s["merged"], dx1b, "tn", BF16, n + "dw_o")
        dm = _matmul(dx1b, w["w_o"], "nt", BF16, n + "dmerged")
        dproj, dya, dyp, dyt, gs["pool_scale"][l] = _merge_bwd(
            s["proj"], gate_off, s["ya"], s["yp"], s["yt"], small["pool_scale"][l], dm, n + "merge")
        g["w_a_out"] = _matmul(s["z"], dya, "tn", BF16, n + "dw_a_out")
        dz = _matmul(dya, w["w_a_out"], "nt", BF16, n + "dz")
        g["w_pool"] = _pool_mm(s["p"], dyp, "tn", F32, n + "dw_pool")
        dp = _pool_mm(dyp, w["w_pool"], "nt", BF16, n + "dp")
        g["w_attn_out"] = _matmul(s["att"], dyt, "tn", BF16, n + "dw_attn_out")
        if hooks is not None:
            hooks.early_grads(l, g)
        datt = _carried(plan, "datt", dyt, w["w_attn_out"], "nt", BF16, n + "datt")
        dproj, gs["conv_w"][l] = _mixer_bwd(s["proj"], small["conv_w"][l], dz, dp, dproj, n + "mixer")
        job = plan.job("attn_b") if plan is not None else None
        (dproj, dke, dve, ds_sum, dsink), extra = _attn_bwd(
            s["proj"], q_off, s["kexp"], s["vexp"], bias, s["sink"], s["att"], s["lse"], datt, dproj, n + "attn",
            comm=job)
        if job is not None:
            plan.done("attn_b", extra)
        gs["attn_sink"][l] = dsink.reshape(N_HEADS, HEAD_DIM)[:, 0]
        ds_total = ds_sum if ds_total is None else ds_total + ds_sum
        dproj = _kv_fold(dke, dve, dproj, q_off + D, n + "kv_fold")
        g["w_inT"] = _carried(plan, "dw_in", dproj, s["h"], "tn", BF16, n + "dw_in", tm_cap=WIDE_TILE)
        dh = _carried(plan, "dh", dproj, w["w_inT"], "nn", F32, n + "dh", tk_cap=2816)
        dx, dxb, gs["g_mix"][l] = _rms_bwd(s["x"], small["g_mix"][l], dh, dx1, n + "rms_mix")
        gw[l] = g
        if hooks is not None:
            hooks.grads(l, g)

    d_rel =_bias_grad(ds_total, bucket, "bias_grad")[:, :, 0].T
    gs = {k_: jnp.stack(v_) for k_, v_ in gs.items()}
    gs["rel_bias"] = d_rel
    gs["g_final"] = dg_final
    return loss, dx, gw, gs


def _place():
    return lax.axis_index("x"), lax.axis_index("y"), lax.axis_index("c")


class _GatherJob:
    def __init__(self, parts):
        n = len(parts)
        self.n = n
        self.ins = list(parts)
        self.outs = [jax.ShapeDtypeStruct((N_DEV,) + p.shape, p.dtype) for p in parts]
        self.sems = [pltpu.SemaphoreType.DMA((7 * n,)), pltpu.SemaphoreType.DMA((7 * n,)), pltpu.SemaphoreType.DMA((n,))]

    def _copies(self, ins, outs, sems):
        send_sems, recv_sems, local_sems = sems
        x, y, c = _place()
        me, sibling = (x, y, c), (x, y, 1 - c)
        chips = [(1 - x, y), (x, 1 - y), (1 - x, 1 - y)]

        def rows(t, px, py, pc):
            return outs[t].at[4 * px + 2 * py + pc]

        def copy(t, k, block, to, src=None):
            return pltpu.make_async_remote_copy(
                src_ref=rows(t, *block) if src is None else src, dst_ref=rows(t, *block),
                send_sem=send_sems.at[7 * t + k], recv_sem=recv_sems.at[7 * t + k], device_id=to, device_id_type=MESH)

        ts = range(self.n)
        own = [pltpu.make_async_copy(ins[t], rows(t, *me), local_sems.at[t]) for t in ts]
        first = [copy(t, 0, me, sibling, src=ins[t]) for t in ts]
        first += [copy(t, 1 + j, me, (*chip, c), src=ins[t]) for t in ts for j, chip in enumerate(chips)]
        landed = [copy(t, 1 + j, (*chip, c), me) for j, chip in enumerate(chips) for t in ts]
        passed = [copy(t, 4 + j, (*chip, c), sibling) for j, chip in enumerate(chips) for t in ts]
        last = [copy(t, 0, sibling, me) for t in ts]
        last += [copy(t, 4 + j, (*chip, 1 - c), me) for t in ts for j, chip in enumerate(chips)]
        return own, first, landed, passed, last

    def start(self, ins, outs, sems):
        own, first, _, _, _ = self._copies(ins, outs, sems)
        for cp in own + first:
            cp.start()

    def mid(self, ins, outs, sems):
        _, _, landed, passed, _ = self._copies(ins, outs, sems)
        for arrived, onward in zip(landed, passed):
            arrived.wait_recv()
            onward.start()

    def finish(self, ins, outs, sems):
        own, first, _, passed, last = self._copies(ins, outs, sems)
        for cp in last:
            cp.wait_recv()
        for cp in first + passed:
            cp.wait_send()
        for cp in own:
            cp.wait()


class _SwapJob:
    def __init__(self, g):
        self.ins = [g]
        self.outs = [jax.ShapeDtypeStruct(g.shape[:1] + g.shape[2:], g.dtype)]
        self.sems = [pltpu.SemaphoreType.DMA, pltpu.SemaphoreType.DMA]

    def _copy(self, ins, outs, sems):
        x, y, c = _place()
        return pltpu.make_async_remote_copy(src_ref=ins[0].at[pl.ds(0, ins[0].shape[0]), 1 - c], dst_ref=outs[0],
                                            send_sem=sems[0], recv_sem=sems[1], device_id=(x, y, 1 - c),
                                            device_id_type=MESH)

    def start(self, ins, outs, sems):
        self._copy(ins, outs, sems).start()

    def mid(self, ins, outs, sems):
        pass

    def finish(self, ins, outs, sems):
        self._copy(ins, outs, sems).wait()


class _ExchangeJob:
    def __init__(self, p, row0, rows):
        self.row0, self.rows = row0, rows
        self.ins = [p]
        self.outs = [jax.ShapeDtypeStruct((3, rows) + p.shape[2:], p.dtype)]
        self.sems = [pltpu.SemaphoreType.DMA((3,)), pltpu.SemaphoreType.DMA((3,))]

    def _copies(self, ins, outs, sems):
        x, y, c = _place()
        chips = [(1 - x, y), (x, 1 - y), (1 - x, 1 - y)]
        return [pltpu.make_async_remote_copy(
            src_ref=ins[0].at[2 * px + py, pl.ds(self.row0, self.rows)], dst_ref=outs[0].at[k],
            send_sem=sems[0].at[k], recv_sem=sems[1].at[k], device_id=(px, py, c), device_id_type=MESH)
            for k, (px, py) in enumerate(chips)]

    def start(self, ins, outs, sems):
        for cp in self._copies(ins, outs, sems):
            cp.start()

    def mid(self, ins, outs, sems):
        pass

    def finish(self, ins, outs, sems):
        for cp in self._copies(ins, outs, sems):
            cp.wait()


def _all_gather(v, name):
    def body(x_ref, out_ref, send_sems, recv_sems, local_sem):
        x, y, c = _place()
        me, sibling = (x, y, c), (x, y, 1 - c)
        chips = [(1 - x, y), (x, 1 - y), (1 - x, 1 - y)]

        def rows(px, py, pc):
            return out_ref.at[4 * px + 2 * py + pc]

        def copy(k, block, to, src=None):
            return pltpu.make_async_remote_copy(
                src_ref=rows(*block) if src is None else src, dst_ref=rows(*block),
                send_sem=send_sems.at[k], recv_sem=recv_sems.at[k], device_id=to, device_id_type=MESH)

        mine = pltpu.make_async_copy(x_ref, rows(*me), local_sem)
        mine.start()
        first = [copy(0, me, sibling, src=x_ref)]
        first += [copy(1 + j, me, (*chip, c), src=x_ref) for j, chip in enumerate(chips)]
        for cp in first:
            cp.start()
        passed = [copy(4 + j, (*chip, c), sibling) for j, chip in enumerate(chips)]
        for j, chip in enumerate(chips):
            copy(1 + j, (*chip, c), me).wait_recv()
            passed[j].start()
        copy(0, sibling, me).wait_recv()
        for j, chip in enumerate(chips):
            copy(4 + j, (*chip, 1 - c), me).wait_recv()
        for cp in first + passed:
            cp.wait_send()
        mine.wait()

    return pl.pallas_call(
        body, name=name, in_specs=[HBM_SPEC], out_specs=HBM_SPEC,
        out_shape=jax.ShapeDtypeStruct((N_DEV,) + v.shape, v.dtype),
        scratch_shapes=[pltpu.SemaphoreType.DMA((7,)), pltpu.SemaphoreType.DMA((7,)), pltpu.SemaphoreType.DMA],
    )(v)


def _all_gather_many(parts, name):
    n = len(parts)

    def body(*refs):
        ins, outs = refs[:n], refs[n:2 * n]
        send_sems, recv_sems, local_sems = refs[2 * n:]
        x, y, c = _place()
        me, sibling = (x, y, c), (x, y, 1 - c)
        chips = [(1 - x, y), (x, 1 - y), (1 - x, 1 - y)]

        def rows(t, px, py, pc):
            return outs[t].at[4 * px + 2 * py + pc]

        def copy(t, k, block, to, src=None):
            return pltpu.make_async_remote_copy(
                src_ref=rows(t, *block) if src is None else src, dst_ref=rows(t, *block),
                send_sem=send_sems.at[7 * t + k], recv_sem=recv_sems.at[7 * t + k], device_id=to, device_id_type=MESH)

        mine = [pltpu.make_async_copy(ins[t], rows(t, *me), local_sems.at[t]) for t in range(n)]
        sends = []
        for t in range(n):
            mine[t].start()
            sends.append(copy(t, 0, me, sibling, src=ins[t]))
            sends += [copy(t, 1 + j, me, (*chip, c), src=ins[t]) for j, chip in enumerate(chips)]
        for cp in sends:
            cp.start()
        for j, chip in enumerate(chips):
            for t in range(n):
                copy(t, 1 + j, (*chip, c), me).wait_recv()
                passed = copy(t, 4 + j, (*chip, c), sibling)
                passed.start()
                sends.append(passed)
        for t in range(n):
            copy(t, 0, sibling, me).wait_recv()
            for j, chip in enumerate(chips):
                copy(t, 4 + j, (*chip, 1 - c), me).wait_recv()
        for cp in sends:
            cp.wait_send()
        for cp in mine:
            cp.wait()

    return pl.pallas_call(
        body, name=name, in_specs=[HBM_SPEC] * n, out_specs=[HBM_SPEC] * n,
        out_shape=[jax.ShapeDtypeStruct((N_DEV,) + p.shape, p.dtype) for p in parts],
        scratch_shapes=[pltpu.SemaphoreType.DMA((7 * n,)), pltpu.SemaphoreType.DMA((7 * n,)),
                        pltpu.SemaphoreType.DMA((n,))],
    )(*parts)


def _run_job(job, name):
    n_in, n_out = len(job.ins), len(job.outs)

    def body(*refs):
        ins, outs, sems = refs[:n_in], refs[n_in:n_in + n_out], refs[n_in + n_out:]
        job.start(ins, outs, sems)
        job.mid(ins, outs, sems)
        job.finish(ins, outs, sems)

    return pl.pallas_call(
        body, name=name, in_specs=[HBM_SPEC] * n_in, out_specs=[HBM_SPEC] * n_out, out_shape=list(job.outs),
        scratch_shapes=list(job.sems),
    )(*job.ins)


def _chip_exchange(p, name):
    def body(p_ref, out_ref, send_sems, recv_sems):
        x, y, c = _place()
        chips = [(1 - x, y), (x, 1 - y), (1 - x, 1 - y)]
        copies = [pltpu.make_async_remote_copy(
            src_ref=p_ref.at[2 * px + py], dst_ref=out_ref.at[k], send_sem=send_sems.at[k], recv_sem=recv_sems.at[k],
            device_id=(px, py, c), device_id_type=MESH) for k, (px, py) in enumerate(chips)]
        for cp in copies:
            cp.start()
        for cp in copies:
            cp.wait()

    return pl.pallas_call(
        body, name=name, in_specs=[HBM_SPEC], out_specs=HBM_SPEC,
        out_shape=jax.ShapeDtypeStruct((3,) + p.shape[1:], p.dtype),
        scratch_shapes=[pltpu.SemaphoreType.DMA((3,)), pltpu.SemaphoreType.DMA((3,))],
    )(p)


SUM_ROWS_CAP = 576


def _sum_parts(own, index, others, out_dtype, name, own_row0=0, own_step=0):
    R = others.shape[1]
    common = math.gcd(R, own.shape[1], own_row0 or R)
    rows = next(t for t in range(min(common, SUM_ROWS_CAP) // 16 * 16, 0, -16) if common % t == 0)
    k = others.shape[0]
    assert own_row0 % rows == 0 and own.shape[1] % rows == 0
    blk0 = own_row0 // rows
    per_own = own.shape[1] // rows

    def own_block(i, idx):
        if own_step:
            return (idx[0] + own_step * (i // per_own), i % per_own, 0)
        return (idx[0], blk0 + i, 0)

    def body(idx_ref, own_ref, *refs):
        del idx_ref
        acc = own_ref[...].astype(F32)
        for r in refs[:k]:
            acc = acc + r[...].astype(F32)
        refs[k][...] = acc.astype(out_dtype)

    grid_spec = pltpu.PrefetchScalarGridSpec(
        num_scalar_prefetch=1, grid=(R // rows,),
        in_specs=[pl.BlockSpec((None, rows, LANES), own_block)]
        + [pl.BlockSpec((None, rows, LANES), lambda i, idx, j=j: (j, i, 0)) for j in range(k)],
        out_specs=pl.BlockSpec((rows, LANES), lambda i, idx: (i, 0)))
    return pl.pallas_call(
        body, name=name, grid_spec=grid_spec,
        out_shape=jax.ShapeDtypeStruct((R, LANES), out_dtype), compiler_params=_params("parallel"),
    )(jnp.reshape(index, (1,)).astype(jnp.int32), own, *([others] * k))


def _adamw(w, g, m, v, name):
    shape = w.shape
    cols = shape[-1]
    rows_total = w.size // cols
    w2, g2, m2, v2 = (a.reshape(rows_total, cols) for a in (w, g, m, v))
    rows = rows_total
    if rows_total > ROWS:
        rows = next(r for r in range(ROWS, 0, -8) if rows_total % r == 0)

    def body(w_ref, g_ref, m_ref, v_ref, d_ref, nm_ref, nv_ref):
        gv = g_ref[...]
        nm = ADAM_B1 * m_ref[...] + (1.0 - ADAM_B1) * gv
        nv = ADAM_B2 * v_ref[...] + (1.0 - ADAM_B2) * (gv * gv)
        m_hat = nm / (1.0 - ADAM_B1 ** ADAM_STEP)
        v_hat = nv / (1.0 - ADAM_B2 ** ADAM_STEP)
        d_ref[...] = -ADAM_LR * (m_hat / (jnp.sqrt(v_hat) + ADAM_EPS) + ADAM_WD * w_ref[...])
        nm_ref[...] = nm
        nv_ref[...] = nv

    spec = pl.BlockSpec((rows, cols), lambda i: (i, 0))
    out = jax.ShapeDtypeStruct((rows_total, cols), F32)
    d, nm, nv = pl.pallas_call(
        body, name=name, grid=(rows_total // rows,), in_specs=[spec] * 4, out_specs=[spec] * 3,
        out_shape=[out, out, out], compiler_params=_params("parallel"),
    )(w2, g2, m2, v2)
    return d.reshape(shape), nm.reshape(shape), nv.reshape(shape)


BIG = ("w_in", "w_a_out", "w_pool", "w_attn_out", "w_o", "w_gu", "w_down")


LOCAL = dict(w_in="w_inT", w_a_out="w_a_out", w_pool="w_pool", w_attn_out="w_attn_out", w_o="w_o", w_gu="w_guT",
             w_down="w_down")


def _shard_rows(w, l):
    out = []
    for name in BIG:
        a = w[name][l]
        if name in ("w_in", "w_gu"):
            a = a.T
        elif name == "w_pool":
            a = a.reshape(-1, a.shape[-1])
        out.append(a.astype(BF16))
    return out


def _full_weights(names, gathered, w):
    out = {}
    for name, g in zip(names, gathered):
        if name == "w_pool":
            G, rg, cg = w[name].shape[1:]
            out[name] = jnp.transpose(g.reshape(N_DEV, G, rg, cg), (1, 0, 2, 3)).reshape(G, N_DEV * rg, cg)
        else:
            out[LOCAL[name]] = g.reshape(N_DEV * g.shape[1], g.shape[2])
    return out


def _split_grads(g, w, names=BIG, multiple=1):
    parts, spans, at = [], {}, 0
    for name in names:
        a = g[LOCAL[name]].astype(BF16)
        if name == "w_pool":
            G, rg, cg = w[name].shape[1:]
            a = jnp.transpose(a.reshape(G, N_DEV, rg, cg), (1, 0, 2, 3))
        a = a.reshape(N_DEV, -1, LANES)
        spans[name] = (at, at + a.shape[1])
        at += a.shape[1]
        parts.append(a)
    if at % multiple:
        parts.append(jnp.zeros((N_DEV, multiple - at % multiple, LANES), BF16))
    return jnp.concatenate(parts, axis=1), spans


def _own_grads(pieces, w):
    out = {}
    for name in BIG:
        per_layer = []
        for layer in pieces:
            packed, spans = next((p, s) for p, s in layer if name in s)
            per_layer.append(packed[spans[name][0]:spans[name][1]])
        a = jnp.stack(per_layer)
        if name in ("w_in", "w_gu"):
            sh = w[name].shape
            a = jnp.swapaxes(a.reshape(sh[0], sh[2], sh[1]), 1, 2)
        out[name] = a.reshape(w[name].shape)
    return out


SQUARE = ("w_a_out", "w_pool", "w_attn_out", "w_o")


class _Prefetch:
    def __init__(self, schedule, assign):
        self.schedule, self.assign = schedule, assign

    def job(self, key):
        if key not in self.assign:
            return None
        parts = []
        for layer, names in self.assign[key]:
            shards = dict(zip(BIG, _shard_rows(self.schedule.w, layer)))
            parts += [shards[name] for name in names]
        return _GatherJob(parts)

    def done(self, key, outs):
        for layer, names in self.assign[key]:
            self.schedule.arrived(layer, names, outs[:len(names)])
            outs = outs[len(names):]


class _Reduce:
    def __init__(self, split, spans, place, tag, swap_key="dw_down", carriers=("dw_gu", "dh2", "dw_in", "dh"),
                 chunk_rows=(640, 640, 768, 512)):
        self.split, self.spans, self.tag = split, spans, tag
        self.swap_key, self.carriers = swap_key, carriers
        self.core, self.chip = place[2], 2 * place[0] + place[1]
        self.rows = split.shape[1]
        self.chunks, at = [], 0
        for rows in chunk_rows:
            rows = min(rows, self.rows - at) if len(self.chunks) + 1 < len(chunk_rows) else self.rows - at
            if rows > 0:
                self.chunks.append((at, rows))
                at += rows
        assert at == self.rows and len(self.chunks) <= len(carriers)
        self.sums = [None] * len(self.chunks)

    def _swap_job(self):
        return _SwapJob(self.split.reshape(4, 2, self.rows, LANES))

    def _pair_sum(self, from_sibling):
        pair = _sum_parts(self.split, self.core, from_sibling.reshape(1, 4 * self.rows, LANES), BF16,
                          self.tag + "pair_sum", own_step=2)
        self.pair = pair.reshape(4, self.rows, LANES)

    def _chip_sum(self, n, from_chips):
        self.sums[n] = _sum_parts(self.pair, self.chip, from_chips, F32, f"{self.tag}chip_sum{n}",
                                  own_row0=self.chunks[n][0])

    def job(self, key):
        if key == self.swap_key:
            return self._swap_job()
        if key in self.carriers[:len(self.chunks)]:
            return _ExchangeJob(self.pair, *self.chunks[self.carriers.index(key)])
        return None

    def done(self, key, outs):
        if key == self.swap_key:
            self._pair_sum(outs[0])
        else:
            self._chip_sum(self.carriers.index(key), outs[0])

    def run(self):
        self._pair_sum(_run_job(self._swap_job(), self.tag + "reduce_pair")[0])
        self.chunks, self.sums = [(0, self.rows)], [None]
        self._chip_sum(0, _run_job(_ExchangeJob(self.pair, 0, self.rows), self.tag + "reduce_chips")[0])
        return self.result()

    def result(self):
        return (self.sums[0] if len(self.sums) == 1 else jnp.concatenate(self.sums, axis=0)), self.spans


class _Plans:
    def __init__(self, plans):
        self.plans = plans

    def job(self, key):
        self.owner = next((p for p in self.plans if p.job(key) is not None), None)
        return self.owner.job(key) if self.owner is not None else None

    def done(self, key, outs):
        self.owner.done(key, outs)


class _Schedule:
    EARLY = ("w_gu", "w_down") + SQUARE
    EARLY_PAD = 512

    def __init__(self, w, place):
        self.w, self.place = w, place
        self.depth = w["w_in"].shape[0]
        self.full = [{} for _ in range(self.depth)]
        self.reduce = {}
        self.pieces = [None] * self.depth
        self.active = []

    def arrived(self, layer, names, gathered):
        self.full[layer].update(_full_weights(names, gathered, self.w))

    def plan_fwd(self, l):
        nxt = l + 1
        more = nxt < self.depth
        if l == 0:
            assign = dict(proj=[(0, ("w_gu",) + SQUARE)], attn=[(0, ("w_down",))])
            if more:
                assign["attn"].append((nxt, ("w_in",)))
                assign.update(gu=[(nxt, ("w_gu",))], x2=[(nxt, ("w_down",))])
        else:
            late = [(l, SQUARE)] if l == 1 else []
            assign = dict(attn=late)
            if more:
                assign.update(proj=[(nxt, ("w_in",))], gu=[(nxt, ("w_down", "w_attn_out", "w_o"))])
                assign["attn"] = late + [(nxt, ("w_gu",) if late else ("w_gu", "w_a_out"))]
                assign["x2"] = [(nxt, ("w_pool", "w_a_out") if late else ("w_pool",))]
            if not assign["attn"]:
                del assign["attn"]
        return _Prefetch(self, assign)

    def weights(self, l):
        return self.full[l]

    def plan_bwd(self, l):
        self.active = [self.reduce[l + 1]] if l + 1 in self.reduce else []
        return _Plans(self.active)

    def early_grads(self, l, g):
        if l == 0:
            split, spans = _split_grads(g, self.w, self.EARLY, self.EARLY_PAD)
            self.early = _Reduce(split, spans, self.place, "l0_early_", swap_key="datt", carriers=("attn_b",),
                                 chunk_rows=(split.shape[1],))
            self.active.append(self.early)

    def grads(self, l, g):
        if l + 1 in self.reduce:
            self.pieces[l + 1] = [self.reduce[l + 1].result()]
        if l == 0:
            last = _Reduce(*_split_grads(g, self.w, ("w_in",)), self.place, "l0_")
            self.pieces[0] = [self.early.result(), last.run()]
        else:
            self.reduce[l] = _Reduce(*_split_grads(g, self.w), self.place, f"l{l}_")


SMALL_ROWS = 32


def _pack_small(gs, L, D):
    rows = [gs["pool_scale"].reshape(L, D), gs["g_mix"].reshape(L, D), gs["g_ffn"].reshape(L, D),
            gs["g_final"].reshape(1, D), gs["conv_w"][:, :3].reshape(3 * L, D),
            jnp.pad(gs["attn_sink"].reshape(1, -1), ((0, 0), (0, D - L * N_HEADS))),
            jnp.pad(gs["rel_bias"].reshape(1, -1), ((0, 0), (0, D - N_BUCKETS * N_HEADS)))]
    a = jnp.concatenate(rows, axis=0)
    return jnp.pad(a, ((0, SMALL_ROWS - a.shape[0]), (0, 0)))


def _unpack_small(a, L, D):
    g = {}
    g["pool_scale"] = a[0:L]
    g["g_mix"] = a[L:2 * L]
    g["g_ffn"] = a[2 * L:3 * L]
    g["g_final"] = a[3 * L]
    g["conv_w"] = a[3 * L + 1:6 * L + 1].reshape(L, 3, 1, D)
    g["attn_sink"] = a[6 * L + 1, :L * N_HEADS].reshape(L, N_HEADS)
    g["rel_bias"] = a[6 * L + 2, :N_BUCKETS * N_HEADS].reshape(N_BUCKETS, N_HEADS)
    return g


WEIGHTS = ("w_in", "conv_w", "w_a_out", "w_pool", "pool_scale", "w_attn_out", "attn_sink", "w_o", "g_mix", "g_ffn",
           "w_gu", "w_down", "rel_bias", "g_final")


def kernel(x, w_in, conv_w, w_a_out, w_pool, pool_scale, w_attn_out, attn_sink, w_o, g_mix, g_ffn, w_gu, w_down, rel_bias, g_final, loss_target, m_w_in, m_conv_w, m_w_a_out, m_w_pool, m_pool_scale, m_w_attn_out, m_attn_sink, m_w_o, m_g_mix, m_g_ffn, m_w_gu, m_w_down, m_rel_bias, m_g_final, v_w_in, v_conv_w, v_w_a_out, v_w_pool, v_pool_scale, v_w_attn_out, v_attn_sink, v_w_o, v_g_mix, v_g_ffn, v_w_gu, v_w_down, v_rel_bias, v_g_final):
    w = dict(w_in=w_in, conv_w=conv_w, w_a_out=w_a_out, w_pool=w_pool, pool_scale=pool_scale, w_attn_out=w_attn_out,
             attn_sink=attn_sink, w_o=w_o, g_mix=g_mix, g_ffn=g_ffn, w_gu=w_gu, w_down=w_down, rel_bias=rel_bias,
             g_final=g_final)
    m = dict(w_in=m_w_in, conv_w=m_conv_w, w_a_out=m_w_a_out, w_pool=m_w_pool, pool_scale=m_pool_scale,
             w_attn_out=m_w_attn_out, attn_sink=m_attn_sink, w_o=m_w_o, g_mix=m_g_mix, g_ffn=m_g_ffn, w_gu=m_w_gu,
             w_down=m_w_down, rel_bias=m_rel_bias, g_final=m_g_final)
    v = dict(w_in=v_w_in, conv_w=v_conv_w, w_a_out=v_w_a_out, w_pool=v_w_pool, pool_scale=v_pool_scale,
             w_attn_out=v_w_attn_out, attn_sink=v_attn_sink, w_o=v_w_o, g_mix=v_g_mix, g_ffn=v_g_ffn, w_gu=v_w_gu,
             w_down=v_w_down, rel_bias=v_rel_bias, g_final=v_g_final)
    T, D = x.shape[1], x.shape[2]
    L = w_in.shape[0]
    cx, cy, cc = _place()

    schedule = _Schedule(w, (cx, cy, cc))
    schedule.arrived(0, ("w_in",), _run_job(_GatherJob(_shard_rows(w, 0)[:1]), "gather_w_in_l0"))
    cw = jnp.pad(conv_w.reshape(L * 3, -1), ((0, 16 - L * 3), (0, 0)))
    cw = _all_gather(cw, "gather_conv_w")
    cw = jnp.transpose(cw, (1, 0, 2)).reshape(16, -1)[:L * 3].reshape(L, 3, -1)
    small = dict(conv_w=jnp.pad(cw, ((0, 0), (0, 5), (0, 0))), pool_scale=pool_scale.reshape(L, 1, D),
                 g_mix=g_mix.reshape(L, 1, D), g_ffn=g_ffn.reshape(L, 1, D), attn_sink=attn_sink,
                 rel_bias=rel_bias, g_final=g_final.reshape(1, D))

    loss, dx, _, gs = _local_step(x[0], loss_target[0], [schedule.full[0]], small, schedule)
    loss = lax.psum(loss[0, 0], ("x", "y", "c"))
    grads = _own_grads(schedule.pieces, w)

    small_all = _all_gather(_pack_small(gs, L, D), "gather_small")
    small_sum = _sum_parts(small_all, jnp.int32(0), small_all[1:], F32, "small_sum")
    gsm = _unpack_small(small_sum, L, D)
    W8 = D // N_DEV
    dev = 4 * cx + 2 * cy + cc
    gsm["conv_w"] = lax.dynamic_slice_in_dim(gsm["conv_w"], dev * W8, W8, axis=3)
    grads.update(gsm)

    deltas, new_m, new_v = {}, {}, {}
    for name in WEIGHTS:
        deltas[name], new_m[name], new_v[name] = _adamw(w[name], grads[name], m[name], v[name], "adamw_" + name)

    return (loss, dx[None], *[grads[n] for n in WEIGHTS], *[deltas[n] for n in WEIGHTS],
            *[new_m[n] for n in WEIGHTS], *[new_v[n] for n in WEIGHTS])
```

```python
import math

import jax
import jax.numpy as jnp
from jax import lax
from jax.experimental import pallas as pl
from jax.experimental.pallas import tpu as pltpu

F32 = jnp.float32
BF16 = jnp.bfloat16
MESH = pl.DeviceIdType.MESH

N_DEV = 8
N_HEADS = 16
N_KV_HEADS = 4
HEAD_DIM = 64
GROUP = N_HEADS // N_KV_HEADS
BLOCK = 128
WINDOW = 128
N_BUCKETS = 32
MAX_DISTANCE = 128
POOL_WINDOWS = (2, 4, 8, 16)
POOL_GROUPS = 4
HALO = 8
EPS = 1e-6
NEG_INF = -1e30

ADAM_LR = 0.001
ADAM_B1 = 0.9
ADAM_B2 = 0.999
ADAM_EPS = 1e-08
ADAM_WD = 0.01
ADAM_STEP = 10

LANES = 1024
VMEM_LIMIT_BYTES = 48 * 1024 * 1024


def _params(*sem):
    return pltpu.CompilerParams(dimension_semantics=sem, vmem_limit_bytes=VMEM_LIMIT_BYTES)


def _tile(n, cap):
    if n <= cap:
        return n
    for t in range(cap - cap % 128, 0, -128):
        if n % t == 0:
            return t
    raise ValueError(f"no tile for {n}")


WIDE_TILE = 2176

_DIMS = {"nn": (((1,), (0,)), ((), ())), "nt": (((1,), (1,)), ((), ())), "tn": (((0,), (0,)), ((), ()))}


HBM_SPEC = pl.BlockSpec(memory_space=pltpu.HBM)
ANY_SPEC = pl.BlockSpec(memory_space=pl.ANY)


def _matmul(a, b, mode, out_dtype, name, res=None, tm_cap=1024, tn_cap=1024, tk_cap=1024, comm=None):
    if mode == "tn":
        K, M = a.shape
    else:
        M, K = a.shape
    N = b.shape[0] if mode == "nt" else b.shape[1]
    tm, tn, tk = _tile(M, tm_cap), _tile(N, tn_cap), _tile(K, tk_cap)
    nk = K // tk
    a_spec = pl.BlockSpec((tk, tm), lambda i, j, k: (k, i)) if mode == "tn" else pl.BlockSpec((tm, tk), lambda i, j, k: (i, k))
    b_spec = pl.BlockSpec((tn, tk), lambda i, j, k: (j, k)) if mode == "nt" else pl.BlockSpec((tk, tn), lambda i, j, k: (k, j))
    o_spec = pl.BlockSpec((tm, tn), lambda i, j, k: (i, j))
    dims = _DIMS[mode]
    has_res = res is not None
    gm, gn = M // tm, N // tn
    steps = gm * gn * nk
    n_in = 2 + has_res
    n_ci = len(comm.ins) if comm is not None else 0
    n_co = len(comm.outs) if comm is not None else 0

    def body(*refs):
        a_ref, b_ref = refs[0], refs[1]
        res_ref = refs[2] if has_res else None
        comm_in = refs[n_in:n_in + n_ci]
        o_ref = refs[n_in + n_ci]
        comm_out = refs[n_in + n_ci + 1:n_in + n_ci + 1 + n_co]
        acc_ref = refs[n_in + n_ci + 1 + n_co]
        sems = refs[n_in + n_ci + 2 + n_co:]
        k = pl.program_id(2)
        step = (pl.program_id(0) * gn + pl.program_id(1)) * nk + k
        if comm is not None:
            @pl.when(step == 0)
            def _():
                comm.start(comm_in, comm_out, sems)

        part = lax.dot_general(a_ref[...], b_ref[...], dims, preferred_element_type=F32)

        @pl.when(k == 0)
        def _():
            acc_ref[...] = part

        @pl.when(k > 0)
        def _():
            acc_ref[...] += part

        @pl.when(k == nk - 1)
        def _():
            out = acc_ref[...]
            if has_res:
                out = out + res_ref[...]
            o_ref[...] = out.astype(out_dtype)

        if comm is not None:
            @pl.when(step == (7 * steps) // 8)
            def _():
                comm.mid(comm_in, comm_out, sems)

            @pl.when(step == steps - 1)
            def _():
                comm.finish(comm_in, comm_out, sems)

    in_specs = [a_spec, b_spec] + ([o_spec] if has_res else [])
    args = (a, b) + ((res,) if has_res else ())
    out_shape = jax.ShapeDtypeStruct((M, N), out_dtype)
    if comm is None:
        return pl.pallas_call(
            body, name=name, grid=(gm, gn, nk), in_specs=in_specs, out_specs=o_spec, out_shape=out_shape,
            scratch_shapes=[pltpu.VMEM((tm, tn), F32)],
            compiler_params=_params("parallel", "parallel", "arbitrary"),
        )(*args)
    outs = pl.pallas_call(
        body, name=name, grid=(gm, gn, nk),
        in_specs=in_specs + [HBM_SPEC] * n_ci, out_specs=[o_spec] + [HBM_SPEC] * n_co,
        out_shape=[out_shape] + list(comm.outs),
        scratch_shapes=[pltpu.VMEM((tm, tn), F32)] + list(comm.sems),
        compiler_params=_params("arbitrary", "arbitrary", "arbitrary"),
    )(*args, *comm.ins)
    return outs[0], outs[1:]


def _pool_mm(a, w, mode, out_dtype, name):
    T = a.shape[0]
    G = POOL_GROUPS
    cg = a.shape[1] // G
    tm = _tile(T, 1024)
    nt = T // tm
    dims = _DIMS[mode]
    if mode == "tn":
        def body(a_ref, d_ref, o_ref):
            part = lax.dot_general(a_ref[...], d_ref[...], dims, preferred_element_type=F32)

            @pl.when(pl.program_id(1) == 0)
            def _():
                o_ref[...] = part

            @pl.when(pl.program_id(1) > 0)
            def _():
                o_ref[...] += part

        return pl.pallas_call(
            body, name=name, grid=(G, nt),
            in_specs=[pl.BlockSpec((tm, cg), lambda g, i: (i, g)), pl.BlockSpec((tm, cg), lambda g, i: (i, g))],
            out_specs=pl.BlockSpec((None, cg, cg), lambda g, i: (g, 0, 0)),
            out_shape=jax.ShapeDtypeStruct((G, cg, cg), F32),
            compiler_params=_params("parallel", "arbitrary"),
        )(a, w)

    def body(a_ref, w_ref, o_ref):
        o_ref[...] = lax.dot_general(a_ref[...], w_ref[...], dims, preferred_element_type=F32).astype(out_dtype)

    return pl.pallas_call(
        body, name=name, grid=(G, nt),
        in_specs=[pl.BlockSpec((tm, cg), lambda g, i: (i, g)), pl.BlockSpec((None, cg, cg), lambda g, i: (g, 0, 0))],
        out_specs=pl.BlockSpec((tm, cg), lambda g, i: (i, g)),
        out_shape=jax.ShapeDtypeStruct((T, G * cg), out_dtype),
        compiler_params=_params("parallel", "parallel"),
    )(a, w)


ROWS = 256
HALO_BLOCK = 16


def _row_spec(d, col=0, rows=ROWS):
    return pl.BlockSpec((rows, d), lambda i, col=col: (i, col))


def _const_spec(shape):
    return pl.BlockSpec(shape, lambda *_: (0,) * len(shape))


def _rms_fwd(x, g, name):
    T, D = x.shape

    def body(x_ref, g_ref, h_ref):
        xv = x_ref[...]
        r = lax.rsqrt(jnp.mean(xv * xv, axis=-1, keepdims=True) + EPS)
        h_ref[...] = (xv * r * g_ref[...]).astype(BF16)

    return pl.pallas_call(
        body, name=name, grid=(T // ROWS,),
        in_specs=[_row_spec(D), _const_spec((1, D))], out_specs=_row_spec(D),
        out_shape=jax.ShapeDtypeStruct((T, D), BF16), compiler_params=_params("parallel"),
    )(x, g)


def _accumulate(ref, part):
    first = pl.program_id(0) == 0

    @pl.when(first)
    def _():
        ref[...] = part

    @pl.when(jnp.logical_not(first))
    def _():
        ref[...] += part


def _rms_bwd(x, g, dh, dres, name):
    T, D = x.shape

    def body(x_ref, g_ref, dh_ref, dres_ref, dx_ref, dxb_ref, dg_ref):
        xv = x_ref[...]
        r = lax.rsqrt(jnp.mean(xv * xv, axis=-1, keepdims=True) + EPS)
        xhat = xv * r
        dh_v = dh_ref[...]
        dxhat = dh_v * g_ref[...]
        dx = dres_ref[...] + r * (dxhat - xhat * jnp.mean(dxhat * xhat, axis=-1, keepdims=True))
        dx_ref[...] = dx
        dxb_ref[...] = dx.astype(BF16)
        _accumulate(dg_ref, jnp.sum(dh_v * xhat, axis=0, keepdims=True))

    return pl.pallas_call(
        body, name=name, grid=(T // ROWS,),
        in_specs=[_row_spec(D), _const_spec((1, D)), _row_spec(D), _row_spec(D)],
        out_specs=[_row_spec(D), _row_spec(D), _const_spec((1, D))],
        out_shape=[jax.ShapeDtypeStruct((T, D), F32), jax.ShapeDtypeStruct((T, D), BF16),
                   jax.ShapeDtypeStruct((1, D), F32)],
        compiler_params=_params("arbitrary"),
    )(x, g, dh, dres)


def _loss_head(x, g, target, name):
    T, D = x.shape

    def body(x_ref, g_ref, t_ref, loss_ref, dx_ref, dxb_ref, dg_ref):
        xv = x_ref[...]
        gv = g_ref[...]
        r = lax.rsqrt(jnp.mean(xv * xv, axis=-1, keepdims=True) + EPS)
        xhat = xv * r
        err = xhat * gv - t_ref[...]
        loss = 0.5 * jnp.sum(jnp.mean(err * err, axis=-1, keepdims=True), axis=0, keepdims=True)
        dy = err * (1.0 / D)
        dxhat = dy * gv
        dx = r * (dxhat - xhat * jnp.mean(dxhat * xhat, axis=-1, keepdims=True))
        dx_ref[...] = dx
        dxb_ref[...] = dx.astype(BF16)
        _accumulate(loss_ref, loss)
        _accumulate(dg_ref, jnp.sum(dy * xhat, axis=0, keepdims=True))

    return pl.pallas_call(
        body, name=name, grid=(T // ROWS,),
        in_specs=[_row_spec(D), _const_spec((1, D)), _row_spec(D)],
        out_specs=[_const_spec((1, 1)), _row_spec(D), _row_spec(D), _const_spec((1, D))],
        out_shape=[jax.ShapeDtypeStruct((1, 1), F32), jax.ShapeDtypeStruct((T, D), F32),
                   jax.ShapeDtypeStruct((T, D), BF16), jax.ShapeDtypeStruct((1, D), F32)],
        compiler_params=_params("arbitrary"),
    )(x, g, target)


def _halo_specs(d, col, n_blocks):
    per = ROWS // HALO_BLOCK
    last = n_blocks * per - 1
    prev = pl.BlockSpec((HALO_BLOCK, d), lambda i, col=col: (jnp.maximum(i * per - 1, 0), col))
    nxt = pl.BlockSpec((HALO_BLOCK, d), lambda i, col=col: (jnp.minimum((i + 1) * per, last), col))
    return prev, nxt


def _with_halo(prev, cur, nxt, n_blocks):
    i = pl.program_id(0)
    prev = jnp.where(i > 0, prev[HALO_BLOCK - HALO:], 0.0)
    nxt = jnp.where(i < n_blocks - 1, nxt[:HALO], 0.0)
    return jnp.concatenate([prev, cur, nxt], axis=0)


def _f32(ref):
    return ref[...].astype(F32)


def _shift(ext, k):
    n = ext.shape[0]
    v = ext if k == 0 else pltpu.roll(ext, (-k) % n, 0)
    return v[HALO:HALO + ROWS]


def _shift_full(ext, k):
    n = ext.shape[0]
    return pltpu.roll(ext, (-k) % n, 0)


def _pool_counts(T):
    n = ROWS + 2 * HALO
    t = pl.program_id(0) * ROWS - HALO + lax.broadcasted_iota(jnp.int32, (n, 1), 0)
    out = []
    for w in POOL_WINDOWS:
        lo = jnp.maximum(t - w // 2, 0)
        hi = jnp.minimum(t + (w - 1 - w // 2), T - 1)
        out.append(jnp.maximum(hi - lo + 1, 1).astype(F32))
    return out


def _window_sums(e, sign):
    s2 = e + _shift_full(e, -sign)
    s4 = _shift_full(s2, -1) + _shift_full(s2, 1)
    s8 = _shift_full(s4, -2) + _shift_full(s4, 2)
    s16 = _shift_full(s8, -4) + _shift_full(s8, 4)
    return s2, s4, s8, s16


def _mixer_fwd(proj, conv_w, name):
    T = proj.shape[0]
    W = conv_w.shape[1]
    nb = T // ROWS
    cg = W // POOL_GROUPS

    def body(b_ref, c_ref, x_ref, u_ref, cp_ref, cn_ref, xp_ref, xn_ref, up_ref, un_ref, w_ref, z_ref, p_ref):
        uc = _with_halo(_f32(cp_ref) * _f32(xp_ref), _f32(c_ref) * _f32(x_ref), _f32(cn_ref) * _f32(xn_ref), nb)
        w0, w1, w2 = w_ref[0:1, :], w_ref[1:2, :], w_ref[2:3, :]
        y = w0 * _shift(uc, -1) + w1 * _shift(uc, 0) + w2 * _shift(uc, 1)
        z_ref[...] = (_f32(b_ref) * y).astype(BF16)
        e = _with_halo(_f32(up_ref), _f32(u_ref), _f32(un_ref), nb)
        counts = _pool_counts(T)
        for gi in range(POOL_GROUPS):
            eg = e[:, gi * cg:(gi + 1) * cg]
            s = _window_sums(eg, 1)[gi]
            p = s[HALO:HALO + ROWS] / counts[gi][HALO:HALO + ROWS] - eg[HALO:HALO + ROWS]
            p_ref[:, gi * cg:(gi + 1) * cg] = p.astype(BF16)

    halo = [s for col in (1, 2, 3) for s in _halo_specs(W, col, nb)]
    return pl.pallas_call(
        body, name=name, grid=(nb,),
        in_specs=[_row_spec(W, 0), _row_spec(W, 1), _row_spec(W, 2), _row_spec(W, 3)] + halo + [_const_spec((8, W))],
        out_specs=[_row_spec(W), _row_spec(W)],
        out_shape=[jax.ShapeDtypeStruct((T, W), BF16), jax.ShapeDtypeStruct((T, W), BF16)],
        compiler_params=_params("parallel"),
    )(proj, proj, proj, proj, proj, proj, proj, proj, proj, proj, conv_w)


def _mixer_bwd(proj, conv_w, dz, dp, dproj, name):
    T = proj.shape[0]
    W = conv_w.shape[1]
    nb = T // ROWS
    cg = W // POOL_GROUPS

    def body(b_ref, c_ref, x_ref, dz_ref, dp_ref,
             bp_ref, bn_ref, cp_ref, cn_ref, xp_ref, xn_ref, dzp_ref, dzn_ref, dpp_ref, dpn_ref, w_ref, _,
             o_ref, dw_ref):
        cv, xv, dzv = _f32(c_ref), _f32(x_ref), _f32(dz_ref)
        uc = _with_halo(_f32(cp_ref) * _f32(xp_ref), cv * xv, _f32(cn_ref) * _f32(xn_ref), nb)
        dy = _with_halo(_f32(dzp_ref) * _f32(bp_ref), dzv * _f32(b_ref), _f32(dzn_ref) * _f32(bn_ref), nb)
        w0, w1, w2 = w_ref[0:1, :], w_ref[1:2, :], w_ref[2:3, :]
        um, u0, up = _shift(uc, -1), _shift(uc, 0), _shift(uc, 1)
        o_ref[:, 0:W] = (dzv * (w0 * um + w1 * u0 + w2 * up)).astype(BF16)
        dy0 = _shift(dy, 0)
        duc = w0 * _shift(dy, 1) + w1 * dy0 + w2 * _shift(dy, -1)
        o_ref[:, W:2 * W] = (duc * xv).astype(BF16)
        o_ref[:, 2 * W:3 * W] = (duc * cv).astype(BF16)
        row = lax.broadcasted_iota(jnp.int32, (8, W), 0)
        dw = jnp.where(row == 0, jnp.sum(dy0 * um, axis=0, keepdims=True),
                       jnp.where(row == 1, jnp.sum(dy0 * u0, axis=0, keepdims=True),
                                 jnp.where(row == 2, jnp.sum(dy0 * up, axis=0, keepdims=True), 0.0)))
        _accumulate(dw_ref, dw)
        d = _with_halo(_f32(dpp_ref), _f32(dp_ref), _f32(dpn_ref), nb)
        counts = _pool_counts(T)
        for gi in range(POOL_GROUPS):
            dg = d[:, gi * cg:(gi + 1) * cg]
            s = _window_sums(dg / counts[gi], -1)[gi]
            o_ref[:, 3 * W + gi * cg:3 * W + (gi + 1) * cg] = (s[HALO:HALO + ROWS] - dg[HALO:HALO + ROWS]).astype(BF16)

    def halo(col):
        return list(_halo_specs(W, col, nb))

    return pl.pallas_call(
        body, name=name, grid=(nb,),
        in_specs=[_row_spec(W, 0), _row_spec(W, 1), _row_spec(W, 2), _row_spec(W), _row_spec(W)]
        + halo(0) + halo(1) + halo(2) + halo(0) + halo(0) + [_const_spec((8, W)), ANY_SPEC],
        out_specs=[_row_spec(4 * W), _const_spec((8, W))],
        out_shape=[jax.ShapeDtypeStruct(dproj.shape, BF16), jax.ShapeDtypeStruct((8, W), F32)],
        input_output_aliases={16: 0}, compiler_params=_params("arbitrary"),
    )(proj, proj, proj, dz, dp, proj, proj, proj, proj, proj, proj, dz, dz, dp, dp, conv_w, dproj)


def _t5_bucket(rel):
    half = N_BUCKETS // 2
    max_exact = half // 2
    ret = jnp.where(rel > 0, half, 0)
    n = jnp.abs(rel)
    nf = jnp.maximum(n, 1).astype(jnp.float32)
    large = max_exact + (jnp.log(nf / max_exact) / math.log(MAX_DISTANCE / max_exact)
                         * (half - max_exact)).astype(jnp.int32)
    large = jnp.minimum(large, half - 1)
    return ret + jnp.where(n < max_exact, n, large)


def _bucket_table():
    qi = jnp.arange(BLOCK)[:, None]
    kj = jnp.arange(3 * BLOCK)[None, :]
    rel = kj - BLOCK - qi
    return jnp.where(jnp.abs(rel) <= WINDOW, _t5_bucket(rel), -1).astype(jnp.int32)


def _bias_table(rel_bias, bucket, name):
    def body(rb_ref, bucket_ref, o_ref):
        h = pl.program_id(0)
        bk = bucket_ref[...]
        acc = jnp.full(bk.shape, NEG_INF, F32)
        for b in range(N_BUCKETS):
            acc = jnp.where(bk == b, rb_ref[b, h], acc)
        o_ref[...] = acc

    return pl.pallas_call(
        body, name=name, grid=(N_HEADS,),
        in_specs=[pl.BlockSpec(memory_space=pltpu.SMEM), _const_spec((BLOCK, 3 * BLOCK))],
        out_specs=pl.BlockSpec((None, BLOCK, 3 * BLOCK), lambda h: (h, 0, 0)),
        out_shape=jax.ShapeDtypeStruct((N_HEADS, BLOCK, 3 * BLOCK), F32),
        compiler_params=_params("parallel"),
    )(rel_bias, bucket)


def _bias_grad(ds_sum, bucket, name):
    def body(ds_ref, bucket_ref, o_ref):
        bk = bucket_ref[...]
        ds = ds_ref[...]
        row = lax.broadcasted_iota(jnp.int32, (N_BUCKETS, 128), 0)
        acc = jnp.zeros((N_BUCKETS, 128), F32)
        for b in range(N_BUCKETS):
            s = jnp.sum(jnp.sum(jnp.where(bk == b, ds, 0.0), axis=1, keepdims=True), axis=0, keepdims=True)
            acc = jnp.where(row == b, s, acc)
        o_ref[...] = acc

    return pl.pallas_call(
        body, name=name, grid=(N_HEADS,),
        in_specs=[pl.BlockSpec((None, BLOCK, 3 * BLOCK), lambda h: (h, 0, 0)), _const_spec((BLOCK, 3 * BLOCK))],
        out_specs=pl.BlockSpec((None, N_BUCKETS, 128), lambda h: (h, 0, 0)),
        out_shape=jax.ShapeDtypeStruct((N_HEADS, N_BUCKETS, 128), F32),
        compiler_params=_params("parallel"),
    )(ds_sum, bucket)


PAIR = 2 * HEAD_DIM
Q_BLOCKS_FWD = 4
Q_BLOCKS_BWD = 2


def _low_half(shape):
    return lax.broadcasted_iota(jnp.int32, shape, len(shape) - 1) % PAIR < HEAD_DIM


def _split_pair(a):
    low = _low_half(a.shape)
    zero = jnp.zeros_like(a)
    return jnp.concatenate([jnp.where(low, a, zero), jnp.where(low, zero, a)], axis=0)


def _kv_expand(proj, kv_off, name):
    T = proj.shape[0]
    kv_w = N_KV_HEADS * HEAD_DIM
    rows = _tile(T, 512)

    def body(k_ref, v_ref, ke_ref, ve_ref):
        for src, dst in ((k_ref, ke_ref), (v_ref, ve_ref)):
            for g in range(N_KV_HEADS // 2):
                x = src[:, g * PAIR:(g + 1) * PAIR].astype(F32)
                swapped = pltpu.roll(x, HEAD_DIM, 1)
                low = _low_half(x.shape)
                dst[:, 2 * g * PAIR:(2 * g + 1) * PAIR] = jnp.where(low, x, swapped).astype(BF16)
                dst[:, (2 * g + 1) * PAIR:(2 * g + 2) * PAIR] = jnp.where(low, swapped, x).astype(BF16)

    out = jax.ShapeDtypeStruct((T, N_KV_HEADS * PAIR), BF16)
    ospec = pl.BlockSpec((rows, N_KV_HEADS * PAIR), lambda i: (i, 0))
    return pl.pallas_call(
        body, name=name, grid=(T // rows,),
        in_specs=[pl.BlockSpec((rows, kv_w), lambda i: (i, kv_off // kv_w)),
                  pl.BlockSpec((rows, kv_w), lambda i: (i, kv_off // kv_w + 1))],
        out_specs=[ospec, ospec], out_shape=[out, out], compiler_params=_params("parallel"),
    )(proj, proj)


def _kv_fold(dke, dve, dproj, kv_off, name):
    T = dke.shape[1]
    kv_w = N_KV_HEADS * HEAD_DIM
    rows = _tile(T, 512)

    def body(dk_ref, dv_ref, _, o_ref):
        for n, src in enumerate((dk_ref, dv_ref)):
            for g in range(N_KV_HEADS // 2):
                a = src[2 * g * PAIR:(2 * g + 1) * PAIR, :].T
                b = src[(2 * g + 1) * PAIR:(2 * g + 2) * PAIR, :].T
                a = a + pltpu.roll(a, HEAD_DIM, 1)
                b = b + pltpu.roll(b, HEAD_DIM, 1)
                o_ref[:, n * kv_w + g * PAIR:n * kv_w + (g + 1) * PAIR] = jnp.where(_low_half(a.shape), a, b).astype(BF16)

    ispec = pl.BlockSpec((N_KV_HEADS * PAIR, rows), lambda i: (0, i))
    return pl.pallas_call(
        body, name=name, grid=(T // rows,), in_specs=[ispec, ispec, ANY_SPEC],
        out_specs=pl.BlockSpec((rows, 2 * kv_w), lambda i: (i, kv_off // (2 * kv_w))),
        out_shape=jax.ShapeDtypeStruct(dproj.shape, BF16), input_output_aliases={2: 0},
        compiler_params=_params("parallel"),
    )(dke, dve, dproj)


def _key_blocks(i, nb):
    return [pl.multiple_of(n * BLOCK, BLOCK) for n in (jnp.maximum(i - 1, 0), i, jnp.minimum(i + 1, nb - 1))]


def _three_blocks(ref, starts):
    return jnp.concatenate([ref[pl.ds(s, BLOCK), :] for s in starts], axis=0)


def _group_scores(q_ref, rows, kd, bias_ref, i, nb):
    qq = jnp.concatenate([_split_pair(q_ref[rows, pr * PAIR:(pr + 1) * PAIR]) for pr in range(GROUP // 2)], axis=0)
    qq = qq * (HEAD_DIM ** -0.5)
    s = lax.dot_general(qq, kd, _DIMS["nt"], preferred_element_type=F32)
    s = s + bias_ref[...].reshape(GROUP * BLOCK, 3 * BLOCK)
    kj = lax.broadcasted_iota(jnp.int32, (1, 3 * BLOCK), 1)
    outside = jnp.logical_or(jnp.logical_and(i == 0, kj < BLOCK), jnp.logical_and(i == nb - 1, kj >= 2 * BLOCK))
    return qq, jnp.where(outside, NEG_INF, s)


def _per_head_rows(values):
    head = lax.broadcasted_iota(jnp.int32, (GROUP * BLOCK, 1), 0) // BLOCK
    out = jnp.full((GROUP * BLOCK, 1), values[0], F32)
    for g in range(1, GROUP):
        out = jnp.where(head == g, values[g], out)
    return out


def _attn_specs(T, q_off, Q_BLOCKS):
    gw = GROUP * HEAD_DIM
    return dict(
        sink=pl.BlockSpec(memory_space=pltpu.SMEM),
        q=pl.BlockSpec((Q_BLOCKS * BLOCK, gw), lambda j, i: (i, q_off // gw + j)),
        kv=pl.BlockSpec((T, PAIR), lambda j, i: (0, j)),
        bias=pl.BlockSpec((GROUP, BLOCK, 3 * BLOCK), lambda j, i: (j, 0, 0)),
        o=pl.BlockSpec((Q_BLOCKS * BLOCK, gw), lambda j, i: (i, j)))


def _attn_fwd(proj, q_off, kexp, vexp, bias, sink, name, comm=None):
    T = proj.shape[0]
    nb = T // BLOCK
    Q_BLOCKS = min(Q_BLOCKS_FWD, nb)
    sp = _attn_specs(T, q_off, Q_BLOCKS)
    steps = N_KV_HEADS * (nb // Q_BLOCKS)
    n_ci = len(comm.ins) if comm is not None else 0
    n_co = len(comm.outs) if comm is not None else 0

    def body(*refs):
        sink_ref, q_ref, ke_ref, ve_ref, bias_ref = refs[:5]
        comm_in = refs[5:5 + n_ci]
        o_ref, lse_ref = refs[5 + n_ci:7 + n_ci]
        comm_out = refs[7 + n_ci:7 + n_ci + n_co]
        sems = refs[7 + n_ci + n_co:]
        j, i = pl.program_id(0), pl.program_id(1)
        step = j * (nb // Q_BLOCKS) + i
        if comm is not None:
            @pl.when(step == 0)
            def _():
                comm.start(comm_in, comm_out, sems)

        low = _low_half((BLOCK, PAIR))
        sk = _per_head_rows([sink_ref[GROUP * j + g] for g in range(GROUP)])
        for b in range(Q_BLOCKS):
            blk = i * Q_BLOCKS + b
            rows = slice(b * BLOCK, (b + 1) * BLOCK)
            starts = _key_blocks(blk, nb)
            kd = _three_blocks(ke_ref, starts)
            vv = _split_pair(_three_blocks(ve_ref, starts))
            _, s = _group_scores(q_ref, rows, kd, bias_ref, blk, nb)
            m = jnp.maximum(jnp.max(s, axis=-1, keepdims=True), sk)
            p = jnp.exp(s - m)
            denom = jnp.sum(p, axis=-1, keepdims=True) + jnp.exp(sk - m)
            p = (p / denom).astype(BF16)
            lse = m + jnp.log(denom)
            for pr in range(GROUP // 2):
                lanes = slice(pr * PAIR, (pr + 1) * PAIR)
                a, c = slice(2 * pr * BLOCK, (2 * pr + 1) * BLOCK), slice((2 * pr + 1) * BLOCK, (2 * pr + 2) * BLOCK)
                pp = jnp.concatenate([p[a], p[c]], axis=1)
                o_ref[rows, lanes] = lax.dot_general(pp, vv, _DIMS["nn"], preferred_element_type=F32).astype(BF16)
                lse_ref[rows, lanes] = jnp.where(low, lse[a], lse[c])

        if comm is not None:
            @pl.when(step == (7 * steps) // 8)
            def _():
                comm.mid(comm_in, comm_out, sems)

            @pl.when(step == steps - 1)
            def _():
                comm.finish(comm_in, comm_out, sems)

    out_shape = [jax.ShapeDtypeStruct((T, N_HEADS * HEAD_DIM), BF16), jax.ShapeDtypeStruct((T, N_HEADS * HEAD_DIM), F32)]
    in_specs = [sp["sink"], sp["q"], sp["kv"], sp["kv"], sp["bias"]]
    if comm is None:
        att, lse = pl.pallas_call(
            body, name=name, grid=(N_KV_HEADS, nb // Q_BLOCKS), in_specs=in_specs, out_specs=[sp["o"], sp["o"]],
            out_shape=out_shape, compiler_params=_params("parallel", "parallel"),
        )(sink, proj, kexp, vexp, bias)
        return att, lse, []
    outs = pl.pallas_call(
        body, name=name, grid=(N_KV_HEADS, nb // Q_BLOCKS),
        in_specs=in_specs + [HBM_SPEC] * n_ci, out_specs=[sp["o"], sp["o"]] + [HBM_SPEC] * n_co,
        out_shape=out_shape + list(comm.outs), scratch_shapes=list(comm.sems),
        compiler_params=_params("arbitrary", "arbitrary"),
    )(sink, proj, kexp, vexp, bias, *comm.ins)
    return outs[0], outs[1], outs[2:]


def _attn_bwd(proj, q_off, kexp, vexp, bias, sink, out, lse, dout, dproj, name, comm=None):
    T = proj.shape[0]
    nb = T // BLOCK
    Q_BLOCKS = min(Q_BLOCKS_BWD, nb)
    sp = _attn_specs(T, q_off, Q_BLOCKS)
    scale = HEAD_DIM ** -0.5
    steps = N_KV_HEADS * (nb // Q_BLOCKS)
    n_ci = len(comm.ins) if comm is not None else 0
    n_co = len(comm.outs) if comm is not None else 0

    def body(*refs):
        sink_ref, q_ref, ke_ref, ve_ref, bias_ref, o_ref, lse_ref, do_ref = refs[:8]
        comm_in = refs[9:9 + n_ci]
        dq_ref, dke_ref, dve_ref, ds_ref, dsink_ref = refs[9 + n_ci:14 + n_ci]
        comm_out = refs[14 + n_ci:14 + n_ci + n_co]
        sems = refs[14 + n_ci + n_co:]
        j, i = pl.program_id(0), pl.program_id(1)
        step = j * (nb // Q_BLOCKS) + i
        if comm is not None:
            @pl.when(step == 0)
            def _():
                comm.start(comm_in, comm_out, sems)

        @pl.when(i == 0)
        def _():
            dke_ref[...] = jnp.zeros(dke_ref.shape, F32)
            dve_ref[...] = jnp.zeros(dve_ref.shape, F32)
            ds_ref[...] = jnp.zeros(ds_ref.shape, F32)
            dsink_ref[...] = jnp.zeros(dsink_ref.shape, F32)

        low = _low_half((BLOCK, PAIR))
        for b in range(Q_BLOCKS):
            blk = i * Q_BLOCKS + b
            rows = slice(b * BLOCK, (b + 1) * BLOCK)
            starts = _key_blocks(blk, nb)
            kd = _three_blocks(ke_ref, starts)
            vd = _three_blocks(ve_ref, starts)
            kk = _split_pair(kd)
            qq, s = _group_scores(q_ref, rows, kd, bias_ref, blk, nb)
            lse_rows, deltas, dd = [], [], []
            for pr in range(GROUP // 2):
                lanes = slice(pr * PAIR, (pr + 1) * PAIR)
                l2 = lse_ref[rows, lanes]
                lse_rows += [jnp.max(jnp.where(low, l2, NEG_INF), axis=-1, keepdims=True),
                             jnp.max(jnp.where(low, NEG_INF, l2), axis=-1, keepdims=True)]
                do2 = do_ref[rows, lanes]
                prod = do2.astype(F32) * o_ref[rows, lanes].astype(F32)
                deltas += [jnp.sum(jnp.where(low, prod, 0.0), axis=-1, keepdims=True),
                           jnp.sum(jnp.where(low, 0.0, prod), axis=-1, keepdims=True)]
                dd.append(_split_pair(do2))
                head = GROUP * j + 2 * pr
                p_sink = jnp.exp(jnp.where(low, sink_ref[head], sink_ref[head + 1]) - l2)
                dsink_ref[:, lanes] += jnp.sum(-p_sink * jnp.where(low, deltas[-2], deltas[-1]), axis=0, keepdims=True)
            dd = jnp.concatenate(dd, axis=0)
            p = jnp.exp(s - jnp.concatenate(lse_rows, axis=0))
            dp = lax.dot_general(dd, vd, _DIMS["nt"], preferred_element_type=F32)
            ds = p * (dp - jnp.concatenate(deltas, axis=0))
            dsb = ds.astype(BF16)
            for pr in range(GROUP // 2):
                a, c = slice(2 * pr * BLOCK, (2 * pr + 1) * BLOCK), slice((2 * pr + 1) * BLOCK, (2 * pr + 2) * BLOCK)
                dq = lax.dot_general(jnp.concatenate([dsb[a], dsb[c]], axis=1), kk, _DIMS["nn"],
                                     preferred_element_type=F32) * scale
                dq_ref[rows, pr * PAIR:(pr + 1) * PAIR] = dq.astype(BF16)
            dk_acc = lax.dot_general(qq, dsb, _DIMS["tn"], preferred_element_type=F32)
            dv_acc = lax.dot_general(dd, p.astype(BF16), _DIMS["tn"], preferred_element_type=F32)
            ds_ref[...] += ds.reshape(GROUP, BLOCK, 3 * BLOCK)
            for t, start in enumerate(starts):
                dke_ref[:, pl.ds(start, BLOCK)] += dk_acc[:, t * BLOCK:(t + 1) * BLOCK]
                dve_ref[:, pl.ds(start, BLOCK)] += dv_acc[:, t * BLOCK:(t + 1) * BLOCK]

        if comm is not None:
            @pl.when(step == (7 * steps) // 8)
            def _():
                comm.mid(comm_in, comm_out, sems)

            @pl.when(step == steps - 1)
            def _():
                comm.finish(comm_in, comm_out, sems)

    kv_out = jax.ShapeDtypeStruct((N_KV_HEADS * PAIR, T), F32)
    kvt_spec = pl.BlockSpec((PAIR, T), lambda j, i: (j, 0))
    job_ins, job_outs, job_sems = (comm.ins, comm.outs, comm.sems) if comm is not None else ([], [], [])
    outs = pl.pallas_call(
        body, name=name, grid=(N_KV_HEADS, nb // Q_BLOCKS),
        in_specs=[sp["sink"], sp["q"], sp["kv"], sp["kv"], sp["bias"], sp["o"], sp["o"], sp["o"], ANY_SPEC]
        + [HBM_SPEC] * n_ci,
        out_specs=[sp["q"], kvt_spec, kvt_spec, sp["bias"],
                   pl.BlockSpec((1, GROUP * HEAD_DIM), lambda j, i: (0, j))] + [HBM_SPEC] * n_co,
        out_shape=[jax.ShapeDtypeStruct(dproj.shape, BF16), kv_out, kv_out,
                   jax.ShapeDtypeStruct((N_HEADS, BLOCK, 3 * BLOCK), F32),
                   jax.ShapeDtypeStruct((1, N_HEADS * HEAD_DIM), F32)] + list(job_outs),
        scratch_shapes=list(job_sems), input_output_aliases={8: 0},
        compiler_params=_params("arbitrary" if comm is not None else "parallel", "arbitrary"),
    )(sink, proj, kexp, vexp, bias, out, lse, dout, dproj, *job_ins)
    return outs[:5], outs[5:]


GATE_COLS = 512


def _sigmoid(x):
    return 1.0 / (1.0 + jnp.exp(-x))


def _gate_specs(D, gate_off):
    nc = D // GATE_COLS
    base = gate_off // GATE_COLS
    return [pl.BlockSpec((ROWS, GATE_COLS), lambda i, c=base + g * nc + h: (i, c)) for g in range(3) for h in range(nc)]


def _merge_fwd(proj, gate_off, ya, yp, yt, scale, name):
    T, D = ya.shape
    nc = D // GATE_COLS

    def body(*refs):
        gates = refs[:3 * nc]
        ya_ref, yp_ref, yt_ref, s_ref, o_ref = refs[3 * nc:]
        for h in range(nc):
            cols = slice(h * GATE_COLS, (h + 1) * GATE_COLS)
            merged = (_sigmoid(_f32(gates[h])) * ya_ref[:, cols].astype(F32)
                      + _sigmoid(_f32(gates[nc + h])) * (yp_ref[:, cols].astype(F32) * s_ref[:, cols])
                      + _sigmoid(_f32(gates[2 * nc + h])) * yt_ref[:, cols].astype(F32))
            o_ref[:, cols] = merged.astype(BF16)

    yspec = _row_spec(D)
    return pl.pallas_call(
        body, name=name, grid=(T // ROWS,),
        in_specs=_gate_specs(D, gate_off) + [yspec, yspec, yspec, _const_spec((1, D))], out_specs=yspec,
        out_shape=jax.ShapeDtypeStruct((T, D), BF16), compiler_params=_params("parallel"),
    )(*([proj] * (3 * nc)), ya, yp, yt, scale)


def _merge_bwd(proj, gate_off, ya, yp, yt, scale, dm, name):
    T, D = ya.shape
    nc = D // GATE_COLS
    base = gate_off // GATE_COLS

    def body(gate_ref, ya_ref, yp_ref, yt_ref, s_ref, dm_ref, dg_ref, dya_ref, dyp_ref, dyt_ref, ds_ref):
        i, n = pl.program_id(0), pl.program_id(1)
        sg = _sigmoid(_f32(gate_ref))
        for g, (y_ref, dy_ref) in enumerate(((ya_ref, dya_ref), (yp_ref, dyp_ref), (yt_ref, dyt_ref))):
            for h in range(nc):
                @pl.when(n == g * nc + h)
                def _(g=g, h=h, y_ref=y_ref, dy_ref=dy_ref):
                    cols = slice(h * GATE_COLS, (h + 1) * GATE_COLS)
                    dy = dm_ref[:, cols].astype(F32) * sg
                    y = y_ref[:, cols].astype(F32)
                    if g == 1:
                        s_v = s_ref[:, cols]
                        part = jnp.sum(dy * y, axis=0, keepdims=True)

                        @pl.when(i == 0)
                        def _():
                            ds_ref[:, cols] = part

                        @pl.when(i > 0)
                        def _():
                            ds_ref[:, cols] += part

                        y = y * s_v
                        dy_ref[:, cols] = (dy * s_v).astype(BF16)
                    else:
                        dy_ref[:, cols] = dy.astype(BF16)
                    dg_ref[...] = (dy * y * (1.0 - sg)).astype(BF16)

    rows = _tile(T, 4 * ROWS)
    yspec = pl.BlockSpec((rows, D), lambda i, n: (i, 0))
    gspec = pl.BlockSpec((rows, GATE_COLS), lambda i, n: (i, base + n))
    sspec = pl.BlockSpec((1, D), lambda i, n: (0, 0))
    out = jax.ShapeDtypeStruct((T, D), BF16)
    return pl.pallas_call(
        body, name=name, grid=(T // rows, 3 * nc),
        in_specs=[gspec, yspec, yspec, yspec, sspec, yspec],
        out_specs=[gspec, yspec, yspec, yspec, sspec],
        out_shape=[jax.ShapeDtypeStruct(proj.shape, BF16), out, out, out, jax.ShapeDtypeStruct((1, D), F32)],
        compiler_params=_params("arbitrary", "arbitrary"),
    )(proj, ya, yp, yt, scale, dm)


def _swiglu_fwd(gu, name):
    T = gu.shape[0]
    F = gu.shape[1] // 2

    def body(gu_ref, o_ref):
        g = gu_ref[:, 0:F].astype(F32)
        o_ref[...] = (g * _sigmoid(g) * gu_ref[:, F:2 * F].astype(F32)).astype(BF16)

    return pl.pallas_call(
        body, name=name, grid=(T // ROWS,), in_specs=[_row_spec(2 * F)], out_specs=_row_spec(F),
        out_shape=jax.ShapeDtypeStruct((T, F), BF16), compiler_params=_params("parallel"),
    )(gu)


def _swiglu_bwd(gu, dact, name):
    T = gu.shape[0]
    F = gu.shape[1] // 2

    def body(gu_ref, d_ref, o_ref):
        g, d = gu_ref[:, 0:F].astype(F32), d_ref[...].astype(F32)
        sg = _sigmoid(g)
        o_ref[:, 0:F] = (d * gu_ref[:, F:2 * F].astype(F32) * sg * (1.0 + g * (1.0 - sg))).astype(BF16)
        o_ref[:, F:2 * F] = (d * g * sg).astype(BF16)

    return pl.pallas_call(
        body, name=name, grid=(T // ROWS,), in_specs=[_row_spec(2 * F), _row_spec(F)], out_specs=_row_spec(2 * F),
        out_shape=jax.ShapeDtypeStruct((T, 2 * F), BF16), compiler_params=_params("parallel"),
    )(gu, dact)


def _carried(plan, key, *args, **kwargs):
    job = plan.job(key) if plan is not None else None
    if job is None:
        return _matmul(*args, **kwargs)
    out, extra = _matmul(*args, comm=job, **kwargs)
    plan.done(key, extra)
    return out


def _local_step(x, target, wts, small, hooks=None):
    T, D = x.shape
    depth = small["g_mix"].shape[0]
    wts = list(wts) + [None] * (depth - len(wts))
    gate_off = wts[0]["w_inT"].shape[0] - 3 * D
    q_off = 4 * D
    bucket = _bucket_table()
    bias = _bias_table(small["rel_bias"], bucket, "bias_table")

    saved = []
    for l in range(depth):
        n = f"l{l}_"
        if hooks is not None and l > 0:
            wts[l] = hooks.weights(l)
        w = wts[l]
        plan = hooks.plan_fwd(l) if hooks is not None else None
        h = _rms_fwd(x, small["g_mix"][l], n + "rms_mix")
        proj = _carried(plan, "proj", h, w["w_inT"], "nt", BF16, n + "proj", tn_cap=WIDE_TILE)
        z, p = _mixer_fwd(proj, small["conv_w"][l], n + "mixer")
        kexp, vexp = _kv_expand(proj, q_off + D, n + "kv_expand")
        sink = small["attn_sink"][l]
        job = plan.job("attn") if plan is not None else None
        att, lse, extra = _attn_fwd(proj, q_off, kexp, vexp, bias, sink, n + "attn", comm=job)
        if job is not None:
            plan.done("attn", extra)
        ya =_matmul(z, w["w_a_out"], "nn", BF16, n + "ya")
        yp = _pool_mm(p, w["w_pool"], "nn", BF16, n + "yp")
        yt = _matmul(att, w["w_attn_out"], "nn", BF16, n + "yt")
        merged = _merge_fwd(proj, gate_off, ya, yp, yt, small["pool_scale"][l], n + "merge")
        x1 = _matmul(merged, w["w_o"], "nn", F32, n + "x1", res=x)
        h2 = _rms_fwd(x1, small["g_ffn"][l], n + "rms_ffn")
        gu = _carried(plan, "gu", h2, w["w_guT"], "nt", BF16, n + "gu", tn_cap=WIDE_TILE)
        act = _swiglu_fwd(gu, n + "swiglu")
        ff = w["w_down"].shape[0]
        x2 = _carried(plan, "x2", act, w["w_down"], "nn", F32, n + "x2", res=x1, tn_cap=512, tk_cap=ff)
        saved.append(dict(x=x, h=h, proj=proj, z=z, p=p, kexp=kexp, vexp=vexp, sink=sink, lse=lse, att=att,
                          ya=ya, yp=yp, yt=yt, merged=merged, x1=x1, h2=h2, gu=gu, act=act))
        x = x2

    loss, dx, dxb, dg_final = _loss_head(x, small["g_final"], target, "loss_head")

    gw = [None] * depth
    gs = {k_: [None] * depth for k_ in ("conv_w", "pool_scale", "g_mix", "g_ffn", "attn_sink")}
    ds_total = None
    for l in reversed(range(depth)):
        n = f"l{l}_b_"
        s, w, g = saved[l], wts[l], {}
        plan = hooks.plan_bwd(l) if hooks is not None else None
        ff = w["w_down"].shape[0]
        g["w_down"] = _carried(plan, "dw_down", s["act"], dxb, "tn", BF16, n + "dw_down", tm_cap=ff, tk_cap=512)
        dact = _matmul(dxb, w["w_down"], "nt", BF16, n + "dact", tm_cap=512, tn_cap=ff)
        dgu = _swiglu_bwd(s["gu"], dact, n + "swiglu")
        g["w_guT"] = _carried(plan, "dw_gu", dgu, s["h2"], "tn", BF16, n + "dw_gu", tm_cap=WIDE_TILE)
        dh2 = _carried(plan, "dh2", dgu, w["w_guT"], "nn", F32, n + "dh2", tn_cap=512, tk_cap=ff)
        dx1, dx1b, gs["g_ffn"][l] = _rms_bwd(s["x1"], small["g_ffn"][l], dh2, dx, n + "rms_ffn")
        g["w_o"] = _matmul(s["merged"], dx1b, "tn", BF16, n + "dw_o")
        dm = _matmul(dx1b, w["w_o"], "nt", BF16, n + "dmerged")
        dproj, dya, dyp, dyt, gs["pool_scale"][l] = _merge_bwd(
            s["proj"], gate_off, s["ya"], s["yp"], s["yt"], small["pool_scale"][l], dm, n + "merge")
        g["w_a_out"] = _matmul(s["z"], dya, "tn", BF16, n + "dw_a_out")
        dz = _matmul(dya, w["w_a_out"], "nt", BF16, n + "dz")
        g["w_pool"] = _pool_mm(s["p"], dyp, "tn", F32, n + "dw_pool")
        dp = _pool_mm(dyp, w["w_pool"], "nt", BF16, n + "dp")
        g["w_attn_out"] = _matmul(s["att"], dyt, "tn", BF16, n + "dw_attn_out")
        if hooks is not None:
            hooks.early_grads(l, g)
        datt = _carried(plan, "datt", dyt, w["w_attn_out"], "nt", BF16, n + "datt")
        dproj, gs["conv_w"][l] = _mixer_bwd(s["proj"], small["conv_w"][l], dz, dp, dproj, n + "mixer")
        job = plan.job("attn_b") if plan is not None else None
        (dproj, dke, dve, ds_sum, dsink), extra = _attn_bwd(
            s["proj"], q_off, s["kexp"], s["vexp"], bias, s["sink"], s["att"], s["lse"], datt, dproj, n + "attn",
            comm=job)
        if job is not None:
            plan.done("attn_b", extra)
        gs["attn_sink"][l] = dsink.reshape(N_HEADS, HEAD_DIM)[:, 0]
        ds_total = ds_sum if ds_total is None else ds_total + ds_sum
        dproj = _kv_fold(dke, dve, dproj, q_off + D, n + "kv_fold")
        g["w_inT"] = _carried(plan, "dw_in", dproj, s["h"], "tn", BF16, n + "dw_in", tm_cap=WIDE_TILE)
        dh = _carried(plan, "dh", dproj, w["w_inT"], "nn", F32, n + "dh", tk_cap=2816)
        dx, dxb, gs["g_mix"][l] = _rms_bwd(s["x"], small["g_mix"][l], dh, dx1, n + "rms_mix")
        gw[l] = g
        if hooks is not None:
            hooks.grads(l, g)

    d_rel =_bias_grad(ds_total, bucket, "bias_grad")[:, :, 0].T
    gs = {k_: jnp.stack(v_) for k_, v_ in gs.items()}
    gs["rel_bias"] = d_rel
    gs["g_final"] = dg_final
    return loss, dx, gw, gs


def _place():
    return lax.axis_index("x"), lax.axis_index("y"), lax.axis_index("c")


class _GatherJob:
    def __init__(self, parts):
        n = len(parts)
        self.n = n
        self.ins = list(parts)
        self.outs = [jax.ShapeDtypeStruct((N_DEV,) + p.shape, p.dtype) for p in parts]
        self.sems = [pltpu.SemaphoreType.DMA((7 * n,)), pltpu.SemaphoreType.DMA((7 * n,)), pltpu.SemaphoreType.DMA((n,))]

    def _copies(self, ins, outs, sems):
        send_sems, recv_sems, local_sems = sems
        x, y, c = _place()
        me, sibling = (x, y, c), (x, y, 1 - c)
        chips = [(1 - x, y), (x, 1 - y), (1 - x, 1 - y)]

        def rows(t, px, py, pc):
            return outs[t].at[4 * px + 2 * py + pc]

        def copy(t, k, block, to, src=None):
            return pltpu.make_async_remote_copy(
                src_ref=rows(t, *block) if src is None else src, dst_ref=rows(t, *block),
                send_sem=send_sems.at[7 * t + k], recv_sem=recv_sems.at[7 * t + k], device_id=to, device_id_type=MESH)

        ts = range(self.n)
        own = [pltpu.make_async_copy(ins[t], rows(t, *me), local_sems.at[t]) for t in ts]
        first = [copy(t, 0, me, sibling, src=ins[t]) for t in ts]
        first += [copy(t, 1 + j, me, (*chip, c), src=ins[t]) for t in ts for j, chip in enumerate(chips)]
        landed = [copy(t, 1 + j, (*chip, c), me) for j, chip in enumerate(chips) for t in ts]
        passed = [copy(t, 4 + j, (*chip, c), sibling) for j, chip in enumerate(chips) for t in ts]
        last = [copy(t, 0, sibling, me) for t in ts]
        last += [copy(t, 4 + j, (*chip, 1 - c), me) for t in ts for j, chip in enumerate(chips)]
        return own, first, landed, passed, last

    def start(self, ins, outs, sems):
        own, first, _, _, _ = self._copies(ins, outs, sems)
        for cp in own + first:
            cp.start()

    def mid(self, ins, outs, sems):
        _, _, landed, passed, _ = self._copies(ins, outs, sems)
        for arrived, onward in zip(landed, passed):
            arrived.wait_recv()
            onward.start()

    def finish(self, ins, outs, sems):
        own, first, _, passed, last = self._copies(ins, outs, sems)
        for cp in last:
            cp.wait_recv()
        for cp in first + passed:
            cp.wait_send()
        for cp in own:
            cp.wait()


class _SwapJob:
    def __init__(self, g):
        self.ins = [g]
        self.outs = [jax.ShapeDtypeStruct(g.shape[:1] + g.shape[2:], g.dtype)]
        self.sems = [pltpu.SemaphoreType.DMA, pltpu.SemaphoreType.DMA]

    def _copy(self, ins, outs, sems):
        x, y, c = _place()
        return pltpu.make_async_remote_copy(src_ref=ins[0].at[pl.ds(0, ins[0].shape[0]), 1 - c], dst_ref=outs[0],
                                            send_sem=sems[0], recv_sem=sems[1], device_id=(x, y, 1 - c),
                                            device_id_type=MESH)

    def start(self, ins, outs, sems):
        self._copy(ins, outs, sems).start()

    def mid(self, ins, outs, sems):
        pass

    def finish(self, ins, outs, sems):
        self._copy(ins, outs, sems).wait()


class _ExchangeJob:
    def __init__(self, p, row0, rows):
        self.row0, self.rows = row0, rows
        self.ins = [p]
        self.outs = [jax.ShapeDtypeStruct((3, rows) + p.shape[2:], p.dtype)]
        self.sems = [pltpu.SemaphoreType.DMA((3,)), pltpu.SemaphoreType.DMA((3,))]

    def _copies(self, ins, outs, sems):
        x, y, c = _place()
        chips = [(1 - x, y), (x, 1 - y), (1 - x, 1 - y)]
        return [pltpu.make_async_remote_copy(
            src_ref=ins[0].at[2 * px + py, pl.ds(self.row0, self.rows)], dst_ref=outs[0].at[k],
            send_sem=sems[0].at[k], recv_sem=sems[1].at[k], device_id=(px, py, c), device_id_type=MESH)
            for k, (px, py) in enumerate(chips)]

    def start(self, ins, outs, sems):
        for cp in self._copies(ins, outs, sems):
            cp.start()

    def mid(self, ins, outs, sems):
        pass

    def finish(self, ins, outs, sems):
        for cp in self._copies(ins, outs, sems):
            cp.wait()


def _run_job(job, name):
    n_in, n_out = len(job.ins), len(job.outs)

    def body(*refs):
        ins, outs, sems = refs[:n_in], refs[n_in:n_in + n_out], refs[n_in + n_out:]
        job.start(ins, outs, sems)
        job.mid(ins, outs, sems)
        job.finish(ins, outs, sems)

    return pl.pallas_call(
        body, name=name, in_specs=[HBM_SPEC] * n_in, out_specs=[HBM_SPEC] * n_out, out_shape=list(job.outs),
        scratch_shapes=list(job.sems),
    )(*job.ins)


SUM_ROWS_CAP = 576


def _sum_parts(own, index, others, out_dtype, name, own_row0=0, own_step=0):
    R = others.shape[1]
    common = math.gcd(R, own.shape[1], own_row0 or R)
    rows = next(t for t in range(min(common, SUM_ROWS_CAP) // 16 * 16, 0, -16) if common % t == 0)
    k = others.shape[0]
    assert own_row0 % rows == 0 and own.shape[1] % rows == 0
    blk0 = own_row0 // rows
    per_own = own.shape[1] // rows

    def own_block(i, idx):
        if own_step:
            return (idx[0] + own_step * (i // per_own), i % per_own, 0)
        return (idx[0], blk0 + i, 0)

    def body(idx_ref, own_ref, *refs):
        del idx_ref
        acc = own_ref[...].astype(F32)
        for r in refs[:k]:
            acc = acc + r[...].astype(F32)
        refs[k][...] = acc.astype(out_dtype)

    grid_spec = pltpu.PrefetchScalarGridSpec(
        num_scalar_prefetch=1, grid=(R // rows,),
        in_specs=[pl.BlockSpec((None, rows, LANES), own_block)]
        + [pl.BlockSpec((None, rows, LANES), lambda i, idx, j=j: (j, i, 0)) for j in range(k)],
        out_specs=pl.BlockSpec((rows, LANES), lambda i, idx: (i, 0)))
    return pl.pallas_call(
        body, name=name, grid_spec=grid_spec,
        out_shape=jax.ShapeDtypeStruct((R, LANES), out_dtype), compiler_params=_params("parallel"),
    )(jnp.reshape(index, (1,)).astype(jnp.int32), own, *([others] * k))


def _adamw(w, g, m, v, name):
    shape = w.shape
    cols = shape[-1]
    rows_total = w.size // cols
    w2, g2, m2, v2 = (a.reshape(rows_total, cols) for a in (w, g, m, v))
    rows = rows_total
    if rows_total > ROWS:
        rows = next(r for r in range(ROWS, 0, -8) if rows_total % r == 0)

    def body(w_ref, g_ref, m_ref, v_ref, d_ref, nm_ref, nv_ref):
        gv = g_ref[...]
        nm = ADAM_B1 * m_ref[...] + (1.0 - ADAM_B1) * gv
        nv = ADAM_B2 * v_ref[...] + (1.0 - ADAM_B2) * (gv * gv)
        m_hat = nm / (1.0 - ADAM_B1 ** ADAM_STEP)
        v_hat = nv / (1.0 - ADAM_B2 ** ADAM_STEP)
        d_ref[...] = -ADAM_LR * (m_hat / (jnp.sqrt(v_hat) + ADAM_EPS) + ADAM_WD * w_ref[...])
        nm_ref[...] = nm
        nv_ref[...] = nv

    spec = pl.BlockSpec((rows, cols), lambda i: (i, 0))
    out = jax.ShapeDtypeStruct((rows_total, cols), F32)
    d, nm, nv = pl.pallas_call(
        body, name=name, grid=(rows_total // rows,), in_specs=[spec] * 4, out_specs=[spec] * 3,
        out_shape=[out, out, out], compiler_params=_params("parallel"),
    )(w2, g2, m2, v2)
    return d.reshape(shape), nm.reshape(shape), nv.reshape(shape)


BIG = ("w_in", "w_a_out", "w_pool", "w_attn_out", "w_o", "w_gu", "w_down")


LOCAL = dict(w_in="w_inT", w_a_out="w_a_out", w_pool="w_pool", w_attn_out="w_attn_out", w_o="w_o", w_gu="w_guT",
             w_down="w_down")


def _shard_rows(w, l):
    out = []
    for name in BIG:
        a = w[name][l]
        if name in ("w_in", "w_gu"):
            a = a.T
        elif name == "w_pool":
            a = a.reshape(-1, a.shape[-1])
        out.append(a.astype(BF16))
    return out


def _full_weights(names, gathered, w):
    out = {}
    for name, g in zip(names, gathered):
        if name == "w_pool":
            G, rg, cg = w[name].shape[1:]
            out[name] = jnp.transpose(g.reshape(N_DEV, G, rg, cg), (1, 0, 2, 3)).reshape(G, N_DEV * rg, cg)
        else:
            out[LOCAL[name]] = g.reshape(N_DEV * g.shape[1], g.shape[2])
    return out


def _split_grads(g, w, names=BIG, multiple=1):
    parts, spans, at = [], {}, 0
    for name in names:
        a = g[LOCAL[name]].astype(BF16)
        if name == "w_pool":
            G, rg, cg = w[name].shape[1:]
            a = jnp.transpose(a.reshape(G, N_DEV, rg, cg), (1, 0, 2, 3))
        a = a.reshape(N_DEV, -1, LANES)
        spans[name] = (at, at + a.shape[1])
        at += a.shape[1]
        parts.append(a)
    if at % multiple:
        parts.append(jnp.zeros((N_DEV, multiple - at % multiple, LANES), BF16))
    return jnp.concatenate(parts, axis=1), spans


def _own_grads(pieces, w):
    out = {}
    for name in BIG:
        per_layer = []
        for layer in pieces:
            packed, spans = next((p, s) for p, s in layer if name in s)
            per_layer.append(packed[spans[name][0]:spans[name][1]])
        a = jnp.stack(per_layer)
        if name in ("w_in", "w_gu"):
            sh = w[name].shape
            a = jnp.swapaxes(a.reshape(sh[0], sh[2], sh[1]), 1, 2)
        out[name] = a.reshape(w[name].shape)
    return out


SQUARE = ("w_a_out", "w_pool", "w_attn_out", "w_o")


class _Prefetch:
    def __init__(self, schedule, assign):
        self.schedule, self.assign = schedule, assign

    def job(self, key):
        if key not in self.assign:
            return None
        parts = []
        for layer, names in self.assign[key]:
            shards = dict(zip(BIG, _shard_rows(self.schedule.w, layer)))
            parts += [shards[name] for name in names]
        return _GatherJob(parts)

    def done(self, key, outs):
        for layer, names in self.assign[key]:
            self.schedule.arrived(layer, names, outs[:len(names)])
            outs = outs[len(names):]


class _Reduce:
    def __init__(self, split, spans, place, tag, swap_key="dw_down", carriers=("dw_gu", "dh2", "dw_in", "dh"),
                 chunk_rows=(640, 640, 768, 512)):
        self.split, self.spans, self.tag = split, spans, tag
        self.swap_key, self.carriers = swap_key, carriers
        self.core, self.chip = place[2], 2 * place[0] + place[1]
        self.rows = split.shape[1]
        self.chunks, at = [], 0
        for rows in chunk_rows:
            rows = min(rows, self.rows - at) if len(self.chunks) + 1 < len(chunk_rows) else self.rows - at
            if rows > 0:
                self.chunks.append((at, rows))
                at += rows
        assert at == self.rows and len(self.chunks) <= len(carriers)
        self.sums = [None] * len(self.chunks)

    def _swap_job(self):
        return _SwapJob(self.split.reshape(4, 2, self.rows, LANES))

    def _pair_sum(self, from_sibling):
        pair = _sum_parts(self.split, self.core, from_sibling.reshape(1, 4 * self.rows, LANES), BF16,
                          self.tag + "pair_sum", own_step=2)
        self.pair = pair.reshape(4, self.rows, LANES)

    def _chip_sum(self, n, from_chips):
        self.sums[n] = _sum_parts(self.pair, self.chip, from_chips, F32, f"{self.tag}chip_sum{n}",
                                  own_row0=self.chunks[n][0])

    def job(self, key):
        if key == self.swap_key:
            return self._swap_job()
        if key in self.carriers[:len(self.chunks)]:
            return _ExchangeJob(self.pair, *self.chunks[self.carriers.index(key)])
        return None

    def done(self, key, outs):
        if key == self.swap_key:
            self._pair_sum(outs[0])
        else:
            self._chip_sum(self.carriers.index(key), outs[0])

    def run(self):
        self._pair_sum(_run_job(self._swap_job(), self.tag + "reduce_pair")[0])
        self.chunks, self.sums = [(0, self.rows)], [None]
        self._chip_sum(0, _run_job(_ExchangeJob(self.pair, 0, self.rows), self.tag + "reduce_chips")[0])
        return self.result()

    def result(self):
        return (self.sums[0] if len(self.sums) == 1 else jnp.concatenate(self.sums, axis=0)), self.spans


class _Plans:
    def __init__(self, plans):
        self.plans = plans

    def job(self, key):
        self.owner = next((p for p in self.plans if p.job(key) is not None), None)
        return self.owner.job(key) if self.owner is not None else None

    def done(self, key, outs):
        self.owner.done(key, outs)


class _Schedule:
    EARLY = ("w_gu", "w_down") + SQUARE
    EARLY_PAD = 512

    def __init__(self, w, place):
        self.w, self.place = w, place
        self.depth = w["w_in"].shape[0]
        self.full = [{} for _ in range(self.depth)]
        self.reduce = {}
        self.pieces = [None] * self.depth
        self.active = []

    def arrived(self, layer, names, gathered):
        self.full[layer].update(_full_weights(names, gathered, self.w))

    def plan_fwd(self, l):
        nxt = l + 1
        more = nxt < self.depth
        if l == 0:
            assign = dict(proj=[(0, ("w_gu",) + SQUARE)], attn=[(0, ("w_down",))])
            if more:
                assign["attn"].append((nxt, ("w_in",)))
                assign.update(gu=[(nxt, ("w_gu",))], x2=[(nxt, ("w_down",))])
        else:
            late = [(l, SQUARE)] if l == 1 else []
            assign = dict(attn=late)
            if more:
                assign.update(proj=[(nxt, ("w_in",))], gu=[(nxt, ("w_down", "w_attn_out", "w_o", "w_pool"))])
                assign["attn"] = late + [(nxt, ("w_gu",) if late else ("w_gu", "w_a_out"))]
                if late:
                    assign["x2"] = [(nxt, ("w_a_out",))]
            if not assign["attn"]:
                del assign["attn"]
        return _Prefetch(self, assign)

    def weights(self, l):
        return self.full[l]

    def plan_bwd(self, l):
        self.active = [self.reduce[l + 1]] if l + 1 in self.reduce else []
        return _Plans(self.active)

    def early_grads(self, l, g):
        if l == 0:
            split, spans = _split_grads(g, self.w, self.EARLY, self.EARLY_PAD)
            self.early = _Reduce(split, spans, self.place, "l0_early_", swap_key="datt", carriers=("attn_b",),
                                 chunk_rows=(split.shape[1],))
            self.active.append(self.early)

    def grads(self, l, g):
        if l + 1 in self.reduce:
            self.pieces[l + 1] = [self.reduce[l + 1].result()]
        if l == 0:
            last = _Reduce(*_split_grads(g, self.w, ("w_in",)), self.place, "l0_")
            self.pieces[0] = [self.early.result(), last.run()]
        else:
            self.reduce[l] = _Reduce(*_split_grads(g, self.w), self.place, f"l{l}_")


SMALL_ROWS = 32


def _pack_small(gs, L, D):
    rows = [gs["pool_scale"].reshape(L, D), gs["g_mix"].reshape(L, D), gs["g_ffn"].reshape(L, D),
            gs["g_final"].reshape(1, D), gs["conv_w"][:, :3].reshape(3 * L, D),
            jnp.pad(gs["attn_sink"].reshape(1, -1), ((0, 0), (0, D - L * N_HEADS))),
            jnp.pad(gs["rel_bias"].reshape(1, -1), ((0, 0), (0, D - N_BUCKETS * N_HEADS)))]
    a = jnp.concatenate(rows, axis=0)
    return jnp.pad(a, ((0, SMALL_ROWS - a.shape[0]), (0, 0)))


def _unpack_small(a, L, D):
    g = {}
    g["pool_scale"] = a[0:L]
    g["g_mix"] = a[L:2 * L]
    g["g_ffn"] = a[2 * L:3 * L]
    g["g_final"] = a[3 * L]
    g["conv_w"] = a[3 * L + 1:6 * L + 1].reshape(L, 3, 1, D)
    g["attn_sink"] = a[6 * L + 1, :L * N_HEADS].reshape(L, N_HEADS)
    g["rel_bias"] = a[6 * L + 2, :N_BUCKETS * N_HEADS].reshape(N_BUCKETS, N_HEADS)
    return g


WEIGHTS = ("w_in", "conv_w", "w_a_out", "w_pool", "pool_scale", "w_attn_out", "attn_sink", "w_o", "g_mix", "g_ffn",
           "w_gu", "w_down", "rel_bias", "g_final")


def kernel(x, w_in, conv_w, w_a_out, w_pool, pool_scale, w_attn_out, attn_sink, w_o, g_mix, g_ffn, w_gu, w_down, rel_bias, g_final, loss_target, m_w_in, m_conv_w, m_w_a_out, m_w_pool, m_pool_scale, m_w_attn_out, m_attn_sink, m_w_o, m_g_mix, m_g_ffn, m_w_gu, m_w_down, m_rel_bias, m_g_final, v_w_in, v_conv_w, v_w_a_out, v_w_pool, v_pool_scale, v_w_attn_out, v_attn_sink, v_w_o, v_g_mix, v_g_ffn, v_w_gu, v_w_down, v_rel_bias, v_g_final):
    w = dict(w_in=w_in, conv_w=conv_w, w_a_out=w_a_out, w_pool=w_pool, pool_scale=pool_scale, w_attn_out=w_attn_out,
             attn_sink=attn_sink, w_o=w_o, g_mix=g_mix, g_ffn=g_ffn, w_gu=w_gu, w_down=w_down, rel_bias=rel_bias,
             g_final=g_final)
    m = dict(w_in=m_w_in, conv_w=m_conv_w, w_a_out=m_w_a_out, w_pool=m_w_pool, pool_scale=m_pool_scale,
             w_attn_out=m_w_attn_out, attn_sink=m_attn_sink, w_o=m_w_o, g_mix=m_g_mix, g_ffn=m_g_ffn, w_gu=m_w_gu,
             w_down=m_w_down, rel_bias=m_rel_bias, g_final=m_g_final)
    v = dict(w_in=v_w_in, conv_w=v_conv_w, w_a_out=v_w_a_out, w_pool=v_w_pool, pool_scale=v_pool_scale,
             w_attn_out=v_w_attn_out, attn_sink=v_attn_sink, w_o=v_w_o, g_mix=v_g_mix, g_ffn=v_g_ffn, w_gu=v_w_gu,
             w_down=v_w_down, rel_bias=v_rel_bias, g_final=v_g_final)
    T, D = x.shape[1], x.shape[2]
    L = w_in.shape[0]
    cx, cy, cc = _place()

    schedule = _Schedule(w, (cx, cy, cc))
    schedule.arrived(0, ("w_in",), _run_job(_GatherJob(_shard_rows(w, 0)[:1]), "gather_w_in_l0"))
    cw = jnp.pad(conv_w.reshape(L * 3, -1), ((0, 16 - L * 3), (0, 0)))
    cw = _run_job(_GatherJob([cw]), "gather_conv_w")[0]
    cw = jnp.transpose(cw, (1, 0, 2)).reshape(16, -1)[:L * 3].reshape(L, 3, -1)
    small = dict(conv_w=jnp.pad(cw, ((0, 0), (0, 5), (0, 0))), pool_scale=pool_scale.reshape(L, 1, D),
                 g_mix=g_mix.reshape(L, 1, D), g_ffn=g_ffn.reshape(L, 1, D), attn_sink=attn_sink,
                 rel_bias=rel_bias, g_final=g_final.reshape(1, D))

    loss, dx, _, gs = _local_step(x[0], loss_target[0], [schedule.full[0]], small, schedule)
    loss = lax.psum(loss[0, 0], ("x", "y", "c"))
    grads = _own_grads(schedule.pieces, w)

    small_all = _run_job(_GatherJob([_pack_small(gs, L, D)]), "gather_small")[0]
    small_sum = _sum_parts(small_all, jnp.int32(0), small_all[1:], F32, "small_sum")
    gsm = _unpack_small(small_sum, L, D)
    W8 = D // N_DEV
    dev = 4 * cx + 2 * cy + cc
    gsm["conv_w"] = lax.dynamic_slice_in_dim(gsm["conv_w"], dev * W8, W8, axis=3)
    grads.update(gsm)

    deltas, new_m, new_v = {}, {}, {}
    for name in WEIGHTS:
        deltas[name], new_m[name], new_v[name] = _adamw(w[name], grads[name], m[name], v[name], "adamw_" + name)

    return (loss, dx[None], *[grads[n] for n in WEIGHTS], *[deltas[n] for n in WEIGHTS],
            *[new_m[n] for n in WEIGHTS], *[new_v[n] for n in WEIGHTS])
```

```python
import math

import jax
import jax.numpy as jnp
from jax import lax
from jax.experimental import pallas as pl
from jax.experimental.pallas import tpu as pltpu

F32 = jnp.float32
BF16 = jnp.bfloat16
MESH = pl.DeviceIdType.MESH

N_DEV = 8
N_HEADS = 16
N_KV_HEADS = 4
HEAD_DIM = 64
GROUP = N_HEADS // N_KV_HEADS
BLOCK = 128
WINDOW = 128
N_BUCKETS = 32
MAX_DISTANCE = 128
POOL_WINDOWS = (2, 4, 8, 16)
POOL_GROUPS = 4
HALO = 8
EPS = 1e-6
NEG_INF = -1e30

ADAM_LR = 0.001
ADAM_B1 = 0.9
ADAM_B2 = 0.999
ADAM_EPS = 1e-08
ADAM_WD = 0.01
ADAM_STEP = 10

LANES = 1024
VMEM_LIMIT_BYTES = 48 * 1024 * 1024


def _params(*sem):
    return pltpu.CompilerParams(dimension_semantics=sem, vmem_limit_bytes=VMEM_LIMIT_BYTES)


def _tile(n, cap):
    if n <= cap:
        return n
    for t in range(cap - cap % 128, 0, -128):
        if n % t == 0:
            return t
    raise ValueError(f"no tile for {n}")


TALL_TILE = 2048
WIDE_TILE = 2176

_DIMS = {"nn": (((1,), (0,)), ((), ())), "nt": (((1,), (1,)), ((), ())), "tn": (((0,), (0,)), ((), ()))}


HBM_SPEC = pl.BlockSpec(memory_space=pltpu.HBM)
ANY_SPEC = pl.BlockSpec(memory_space=pl.ANY)


def _matmul(a, b, mode, out_dtype, name, res=None, tm_cap=1024, tn_cap=1024, tk_cap=1024, comm=None):
    if mode == "tn":
        K, M = a.shape
    else:
        M, K = a.shape
    N = b.shape[0] if mode == "nt" else b.shape[1]
    tm, tn, tk = _tile(M, tm_cap), _tile(N, tn_cap), _tile(K, tk_cap)
    nk = K // tk
    a_spec = pl.BlockSpec((tk, tm), lambda i, j, k: (k, i)) if mode == "tn" else pl.BlockSpec((tm, tk), lambda i, j, k: (i, k))
    b_spec = pl.BlockSpec((tn, tk), lambda i, j, k: (j, k)) if mode == "nt" else pl.BlockSpec((tk, tn), lambda i, j, k: (k, j))
    o_spec = pl.BlockSpec((tm, tn), lambda i, j, k: (i, j))
    dims = _DIMS[mode]
    has_res = res is not None
    gm, gn = M // tm, N // tn
    steps = gm * gn * nk
    n_in = 2 + has_res
    n_ci = len(comm.ins) if comm is not None else 0
    n_co = len(comm.outs) if comm is not None else 0

    def body(*refs):
        a_ref, b_ref = refs[0], refs[1]
        res_ref = refs[2] if has_res else None
        comm_in = refs[n_in:n_in + n_ci]
        o_ref = refs[n_in + n_ci]
        comm_out = refs[n_in + n_ci + 1:n_in + n_ci + 1 + n_co]
        acc_ref = refs[n_in + n_ci + 1 + n_co]
        sems = refs[n_in + n_ci + 2 + n_co:]
        k = pl.program_id(2)
        step = (pl.program_id(0) * gn + pl.program_id(1)) * nk + k
        if comm is not None:
            @pl.when(step == 0)
            def _():
                comm.start(comm_in, comm_out, sems)

        part = lax.dot_general(a_ref[...], b_ref[...], dims, preferred_element_type=F32)

        @pl.when(k == 0)
        def _():
            acc_ref[...] = part

        @pl.when(k > 0)
        def _():
            acc_ref[...] += part

        @pl.when(k == nk - 1)
        def _():
            out = acc_ref[...]
            if has_res:
                out = out + res_ref[...]
            o_ref[...] = out.astype(out_dtype)

        if comm is not None:
            @pl.when(step == (7 * steps) // 8)
            def _():
                comm.mid(comm_in, comm_out, sems)

            @pl.when(step == steps - 1)
            def _():
                comm.finish(comm_in, comm_out, sems)

    in_specs = [a_spec, b_spec] + ([o_spec] if has_res else [])
    args = (a, b) + ((res,) if has_res else ())
    out_shape = jax.ShapeDtypeStruct((M, N), out_dtype)
    if comm is None:
        return pl.pallas_call(
            body, name=name, grid=(gm, gn, nk), in_specs=in_specs, out_specs=o_spec, out_shape=out_shape,
            scratch_shapes=[pltpu.VMEM((tm, tn), F32)],
            compiler_params=_params("parallel", "parallel", "arbitrary"),
        )(*args)
    outs = pl.pallas_call(
        body, name=name, grid=(gm, gn, nk),
        in_specs=in_specs + [HBM_SPEC] * n_ci, out_specs=[o_spec] + [HBM_SPEC] * n_co,
        out_shape=[out_shape] + list(comm.outs),
        scratch_shapes=[pltpu.VMEM((tm, tn), F32)] + list(comm.sems),
        compiler_params=_params("arbitrary", "arbitrary", "arbitrary"),
    )(*args, *comm.ins)
    return outs[0], outs[1:]


def _pool_mm(a, w, mode, out_dtype, name):
    T = a.shape[0]
    G = POOL_GROUPS
    cg = a.shape[1] // G
    tm = _tile(T, 1024)
    nt = T // tm
    dims = _DIMS[mode]
    if mode == "tn":
        def body(a_ref, d_ref, o_ref):
            part = lax.dot_general(a_ref[...], d_ref[...], dims, preferred_element_type=F32)

            @pl.when(pl.program_id(1) == 0)
            def _():
                o_ref[...] = part

            @pl.when(pl.program_id(1) > 0)
            def _():
                o_ref[...] += part

        return pl.pallas_call(
            body, name=name, grid=(G, nt),
            in_specs=[pl.BlockSpec((tm, cg), lambda g, i: (i, g)), pl.BlockSpec((tm, cg), lambda g, i: (i, g))],
            out_specs=pl.BlockSpec((None, cg, cg), lambda g, i: (g, 0, 0)),
            out_shape=jax.ShapeDtypeStruct((G, cg, cg), F32),
            compiler_params=_params("parallel", "arbitrary"),
        )(a, w)

    def body(a_ref, w_ref, o_ref):
        o_ref[...] = lax.dot_general(a_ref[...], w_ref[...], dims, preferred_element_type=F32).astype(out_dtype)

    return pl.pallas_call(
        body, name=name, grid=(G, nt),
        in_specs=[pl.BlockSpec((tm, cg), lambda g, i: (i, g)), pl.BlockSpec((None, cg, cg), lambda g, i: (g, 0, 0))],
        out_specs=pl.BlockSpec((tm, cg), lambda g, i: (i, g)),
        out_shape=jax.ShapeDtypeStruct((T, G * cg), out_dtype),
        compiler_params=_params("parallel", "parallel"),
    )(a, w)


ROWS = 256
HALO_BLOCK = 16


def _row_spec(d, col=0, rows=ROWS):
    return pl.BlockSpec((rows, d), lambda i, col=col: (i, col))


def _const_spec(shape):
    return pl.BlockSpec(shape, lambda *_: (0,) * len(shape))


def _rms_fwd(x, g, name):
    T, D = x.shape

    def body(x_ref, g_ref, h_ref):
        xv = x_ref[...]
        r = lax.rsqrt(jnp.mean(xv * xv, axis=-1, keepdims=True) + EPS)
        h_ref[...] = (xv * r * g_ref[...]).astype(BF16)

    return pl.pallas_call(
        body, name=name, grid=(T // ROWS,),
        in_specs=[_row_spec(D), _const_spec((1, D))], out_specs=_row_spec(D),
        out_shape=jax.ShapeDtypeStruct((T, D), BF16), compiler_params=_params("parallel"),
    )(x, g)


def _accumulate(ref, part):
    first = pl.program_id(0) == 0

    @pl.when(first)
    def _():
        ref[...] = part

    @pl.when(jnp.logical_not(first))
    def _():
        ref[...] += part


def _rms_bwd(x, g, dh, dres, name):
    T, D = x.shape

    def body(x_ref, g_ref, dh_ref, dres_ref, dx_ref, dxb_ref, dg_ref):
        xv = x_ref[...]
        r = lax.rsqrt(jnp.mean(xv * xv, axis=-1, keepdims=True) + EPS)
        xhat = xv * r
        dh_v = dh_ref[...]
        dxhat = dh_v * g_ref[...]
        dx = dres_ref[...] + r * (dxhat - xhat * jnp.mean(dxhat * xhat, axis=-1, keepdims=True))
        dx_ref[...] = dx
        dxb_ref[...] = dx.astype(BF16)
        _accumulate(dg_ref, jnp.sum(dh_v * xhat, axis=0, keepdims=True))

    return pl.pallas_call(
        body, name=name, grid=(T // ROWS,),
        in_specs=[_row_spec(D), _const_spec((1, D)), _row_spec(D), _row_spec(D)],
        out_specs=[_row_spec(D), _row_spec(D), _const_spec((1, D))],
        out_shape=[jax.ShapeDtypeStruct((T, D), F32), jax.ShapeDtypeStruct((T, D), BF16),
                   jax.ShapeDtypeStruct((1, D), F32)],
        compiler_params=_params("arbitrary"),
    )(x, g, dh, dres)


def _loss_head(x, g, target, name):
    T, D = x.shape

    def body(x_ref, g_ref, t_ref, loss_ref, dx_ref, dxb_ref, dg_ref):
        xv = x_ref[...]
        gv = g_ref[...]
        r = lax.rsqrt(jnp.mean(xv * xv, axis=-1, keepdims=True) + EPS)
        xhat = xv * r
        err = xhat * gv - t_ref[...]
        loss = 0.5 * jnp.sum(jnp.mean(err * err, axis=-1, keepdims=True), axis=0, keepdims=True)
        dy = err * (1.0 / D)
        dxhat = dy * gv
        dx = r * (dxhat - xhat * jnp.mean(dxhat * xhat, axis=-1, keepdims=True))
        dx_ref[...] = dx
        dxb_ref[...] = dx.astype(BF16)
        _accumulate(loss_ref, loss)
        _accumulate(dg_ref, jnp.sum(dy * xhat, axis=0, keepdims=True))

    return pl.pallas_call(
        body, name=name, grid=(T // ROWS,),
        in_specs=[_row_spec(D), _const_spec((1, D)), _row_spec(D)],
        out_specs=[_const_spec((1, 1)), _row_spec(D), _row_spec(D), _const_spec((1, D))],
        out_shape=[jax.ShapeDtypeStruct((1, 1), F32), jax.ShapeDtypeStruct((T, D), F32),
                   jax.ShapeDtypeStruct((T, D), BF16), jax.ShapeDtypeStruct((1, D), F32)],
        compiler_params=_params("arbitrary"),
    )(x, g, target)


def _halo_specs(d, col, n_blocks):
    per = ROWS // HALO_BLOCK
    last = n_blocks * per - 1
    prev = pl.BlockSpec((HALO_BLOCK, d), lambda i, col=col: (jnp.maximum(i * per - 1, 0), col))
    nxt = pl.BlockSpec((HALO_BLOCK, d), lambda i, col=col: (jnp.minimum((i + 1) * per, last), col))
    return prev, nxt


def _with_halo(prev, cur, nxt, n_blocks):
    i = pl.program_id(0)
    prev = jnp.where(i > 0, prev[HALO_BLOCK - HALO:], 0.0)
    nxt = jnp.where(i < n_blocks - 1, nxt[:HALO], 0.0)
    return jnp.concatenate([prev, cur, nxt], axis=0)


def _f32(ref):
    return ref[...].astype(F32)


def _shift(ext, k):
    n = ext.shape[0]
    v = ext if k == 0 else pltpu.roll(ext, (-k) % n, 0)
    return v[HALO:HALO + ROWS]


def _shift_full(ext, k):
    n = ext.shape[0]
    return pltpu.roll(ext, (-k) % n, 0)


def _pool_counts(T):
    n = ROWS + 2 * HALO
    t = pl.program_id(0) * ROWS - HALO + lax.broadcasted_iota(jnp.int32, (n, 1), 0)
    out = []
    for w in POOL_WINDOWS:
        lo = jnp.maximum(t - w // 2, 0)
        hi = jnp.minimum(t + (w - 1 - w // 2), T - 1)
        out.append(jnp.maximum(hi - lo + 1, 1).astype(F32))
    return out


def _window_sums(e, sign):
    s2 = e + _shift_full(e, -sign)
    s4 = _shift_full(s2, -1) + _shift_full(s2, 1)
    s8 = _shift_full(s4, -2) + _shift_full(s4, 2)
    s16 = _shift_full(s8, -4) + _shift_full(s8, 4)
    return s2, s4, s8, s16


def _mixer_fwd(proj, conv_w, name):
    T = proj.shape[0]
    W = conv_w.shape[1]
    nb = T // ROWS
    cg = W // POOL_GROUPS

    def body(b_ref, c_ref, x_ref, u_ref, cp_ref, cn_ref, xp_ref, xn_ref, up_ref, un_ref, w_ref, z_ref, p_ref):
        uc = _with_halo(_f32(cp_ref) * _f32(xp_ref), _f32(c_ref) * _f32(x_ref), _f32(cn_ref) * _f32(xn_ref), nb)
        w0, w1, w2 = w_ref[0:1, :], w_ref[1:2, :], w_ref[2:3, :]
        y = w0 * _shift(uc, -1) + w1 * _shift(uc, 0) + w2 * _shift(uc, 1)
        z_ref[...] = (_f32(b_ref) * y).astype(BF16)
        e = _with_halo(_f32(up_ref), _f32(u_ref), _f32(un_ref), nb)
        counts = _pool_counts(T)
        for gi in range(POOL_GROUPS):
            eg = e[:, gi * cg:(gi + 1) * cg]
            s = _window_sums(eg, 1)[gi]
            p = s[HALO:HALO + ROWS] / counts[gi][HALO:HALO + ROWS] - eg[HALO:HALO + ROWS]
            p_ref[:, gi * cg:(gi + 1) * cg] = p.astype(BF16)

    halo = [s for col in (1, 2, 3) for s in _halo_specs(W, col, nb)]
    return pl.pallas_call(
        body, name=name, grid=(nb,),
        in_specs=[_row_spec(W, 0), _row_spec(W, 1), _row_spec(W, 2), _row_spec(W, 3)] + halo + [_const_spec((8, W))],
        out_specs=[_row_spec(W), _row_spec(W)],
        out_shape=[jax.ShapeDtypeStruct((T, W), BF16), jax.ShapeDtypeStruct((T, W), BF16)],
        compiler_params=_params("parallel"),
    )(proj, proj, proj, proj, proj, proj, proj, proj, proj, proj, conv_w)


def _mixer_bwd(proj, conv_w, dz, dp, dproj, name):
    T = proj.shape[0]
    W = conv_w.shape[1]
    nb = T // ROWS
    cg = W // POOL_GROUPS

    def body(b_ref, c_ref, x_ref, dz_ref, dp_ref,
             bp_ref, bn_ref, cp_ref, cn_ref, xp_ref, xn_ref, dzp_ref, dzn_ref, dpp_ref, dpn_ref, w_ref, _,
             o_ref, dw_ref):
        cv, xv, dzv = _f32(c_ref), _f32(x_ref), _f32(dz_ref)
        uc = _with_halo(_f32(cp_ref) * _f32(xp_ref), cv * xv, _f32(cn_ref) * _f32(xn_ref), nb)
        dy = _with_halo(_f32(dzp_ref) * _f32(bp_ref), dzv * _f32(b_ref), _f32(dzn_ref) * _f32(bn_ref), nb)
        w0, w1, w2 = w_ref[0:1, :], w_ref[1:2, :], w_ref[2:3, :]
        um, u0, up = _shift(uc, -1), _shift(uc, 0), _shift(uc, 1)
        o_ref[:, 0:W] = (dzv * (w0 * um + w1 * u0 + w2 * up)).astype(BF16)
        dy0 = _shift(dy, 0)
        duc = w0 * _shift(dy, 1) + w1 * dy0 + w2 * _shift(dy, -1)
        o_ref[:, W:2 * W] = (duc * xv).astype(BF16)
        o_ref[:, 2 * W:3 * W] = (duc * cv).astype(BF16)
        row = lax.broadcasted_iota(jnp.int32, (8, W), 0)
        dw = jnp.where(row == 0, jnp.sum(dy0 * um, axis=0, keepdims=True),
                       jnp.where(row == 1, jnp.sum(dy0 * u0, axis=0, keepdims=True),
                                 jnp.where(row == 2, jnp.sum(dy0 * up, axis=0, keepdims=True), 0.0)))
        _accumulate(dw_ref, dw)
        d = _with_halo(_f32(dpp_ref), _f32(dp_ref), _f32(dpn_ref), nb)
        counts = _pool_counts(T)
        for gi in range(POOL_GROUPS):
            dg = d[:, gi * cg:(gi + 1) * cg]
            s = _window_sums(dg / counts[gi], -1)[gi]
            o_ref[:, 3 * W + gi * cg:3 * W + (gi + 1) * cg] = (s[HALO:HALO + ROWS] - dg[HALO:HALO + ROWS]).astype(BF16)

    def halo(col):
        return list(_halo_specs(W, col, nb))

    return pl.pallas_call(
        body, name=name, grid=(nb,),
        in_specs=[_row_spec(W, 0), _row_spec(W, 1), _row_spec(W, 2), _row_spec(W), _row_spec(W)]
        + halo(0) + halo(1) + halo(2) + halo(0) + halo(0) + [_const_spec((8, W)), ANY_SPEC],
        out_specs=[_row_spec(4 * W), _const_spec((8, W))],
        out_shape=[jax.ShapeDtypeStruct(dproj.shape, BF16), jax.ShapeDtypeStruct((8, W), F32)],
        input_output_aliases={16: 0}, compiler_params=_params("arbitrary"),
    )(proj, proj, proj, dz, dp, proj, proj, proj, proj, proj, proj, dz, dz, dp, dp, conv_w, dproj)


def _t5_bucket(rel):
    half = N_BUCKETS // 2
    max_exact = half // 2
    ret = jnp.where(rel > 0, half, 0)
    n = jnp.abs(rel)
    nf = jnp.maximum(n, 1).astype(jnp.float32)
    large = max_exact + (jnp.log(nf / max_exact) / math.log(MAX_DISTANCE / max_exact)
                         * (half - max_exact)).astype(jnp.int32)
    large = jnp.minimum(large, half - 1)
    return ret + jnp.where(n < max_exact, n, large)


def _bucket_table():
    qi = jnp.arange(BLOCK)[:, None]
    kj = jnp.arange(3 * BLOCK)[None, :]
    rel = kj - BLOCK - qi
    return jnp.where(jnp.abs(rel) <= WINDOW, _t5_bucket(rel), -1).astype(jnp.int32)


def _bias_table(rel_bias, bucket, name):
    def body(rb_ref, bucket_ref, o_ref):
        h = pl.program_id(0)
        bk = bucket_ref[...]
        acc = jnp.full(bk.shape, NEG_INF, F32)
        for b in range(N_BUCKETS):
            acc = jnp.where(bk == b, rb_ref[b, h], acc)
        o_ref[...] = acc

    return pl.pallas_call(
        body, name=name, grid=(N_HEADS,),
        in_specs=[pl.BlockSpec(memory_space=pltpu.SMEM), _const_spec((BLOCK, 3 * BLOCK))],
        out_specs=pl.BlockSpec((None, BLOCK, 3 * BLOCK), lambda h: (h, 0, 0)),
        out_shape=jax.ShapeDtypeStruct((N_HEADS, BLOCK, 3 * BLOCK), F32),
        compiler_params=_params("parallel"),
    )(rel_bias, bucket)


def _bias_grad(ds_sum, bucket, name):
    def body(ds_ref, bucket_ref, o_ref):
        bk = bucket_ref[...]
        ds = ds_ref[...]
        row = lax.broadcasted_iota(jnp.int32, (N_BUCKETS, 128), 0)
        acc = jnp.zeros((N_BUCKETS, 128), F32)
        for b in range(N_BUCKETS):
            s = jnp.sum(jnp.sum(jnp.where(bk == b, ds, 0.0), axis=1, keepdims=True), axis=0, keepdims=True)
            acc = jnp.where(row == b, s, acc)
        o_ref[...] = acc

    return pl.pallas_call(
        body, name=name, grid=(N_HEADS,),
        in_specs=[pl.BlockSpec((None, BLOCK, 3 * BLOCK), lambda h: (h, 0, 0)), _const_spec((BLOCK, 3 * BLOCK))],
        out_specs=pl.BlockSpec((None, N_BUCKETS, 128), lambda h: (h, 0, 0)),
        out_shape=jax.ShapeDtypeStruct((N_HEADS, N_BUCKETS, 128), F32),
        compiler_params=_params("parallel"),
    )(ds_sum, bucket)


PAIR = 2 * HEAD_DIM
Q_BLOCKS_FWD = 4
Q_BLOCKS_BWD = 4


def _low_half(shape):
    return lax.broadcasted_iota(jnp.int32, shape, len(shape) - 1) % PAIR < HEAD_DIM


def _split_pair(a):
    low = _low_half(a.shape)
    zero = jnp.zeros_like(a)
    return jnp.concatenate([jnp.where(low, a, zero), jnp.where(low, zero, a)], axis=0)


def _kv_expand(proj, kv_off, name):
    T = proj.shape[0]
    kv_w = N_KV_HEADS * HEAD_DIM
    rows = _tile(T, 512)

    def body(k_ref, v_ref, ke_ref, ve_ref):
        for src, dst in ((k_ref, ke_ref), (v_ref, ve_ref)):
            for g in range(N_KV_HEADS // 2):
                x = src[:, g * PAIR:(g + 1) * PAIR].astype(F32)
                swapped = pltpu.roll(x, HEAD_DIM, 1)
                low = _low_half(x.shape)
                dst[:, 2 * g * PAIR:(2 * g + 1) * PAIR] = jnp.where(low, x, swapped).astype(BF16)
                dst[:, (2 * g + 1) * PAIR:(2 * g + 2) * PAIR] = jnp.where(low, swapped, x).astype(BF16)

    out = jax.ShapeDtypeStruct((T, N_KV_HEADS * PAIR), BF16)
    ospec = pl.BlockSpec((rows, N_KV_HEADS * PAIR), lambda i: (i, 0))
    return pl.pallas_call(
        body, name=name, grid=(T // rows,),
        in_specs=[pl.BlockSpec((rows, kv_w), lambda i: (i, kv_off // kv_w)),
                  pl.BlockSpec((rows, kv_w), lambda i: (i, kv_off // kv_w + 1))],
        out_specs=[ospec, ospec], out_shape=[out, out], compiler_params=_params("parallel"),
    )(proj, proj)


def _kv_fold(dke, dve, dproj, kv_off, name):
    T = dke.shape[1]
    kv_w = N_KV_HEADS * HEAD_DIM
    rows = _tile(T, 512)

    def body(dk_ref, dv_ref, _, o_ref):
        for n, src in enumerate((dk_ref, dv_ref)):
            for g in range(N_KV_HEADS // 2):
                a = src[2 * g * PAIR:(2 * g + 1) * PAIR, :].T
                b = src[(2 * g + 1) * PAIR:(2 * g + 2) * PAIR, :].T
                a = a + pltpu.roll(a, HEAD_DIM, 1)
                b = b + pltpu.roll(b, HEAD_DIM, 1)
                o_ref[:, n * kv_w + g * PAIR:n * kv_w + (g + 1) * PAIR] = jnp.where(_low_half(a.shape), a, b).astype(BF16)

    ispec = pl.BlockSpec((N_KV_HEADS * PAIR, rows), lambda i: (0, i))
    return pl.pallas_call(
        body, name=name, grid=(T // rows,), in_specs=[ispec, ispec, ANY_SPEC],
        out_specs=pl.BlockSpec((rows, 2 * kv_w), lambda i: (i, kv_off // (2 * kv_w))),
        out_shape=jax.ShapeDtypeStruct(dproj.shape, BF16), input_output_aliases={2: 0},
        compiler_params=_params("parallel"),
    )(dke, dve, dproj)


def _key_blocks(i, nb):
    return [pl.multiple_of(n * BLOCK, BLOCK) for n in (jnp.maximum(i - 1, 0), i, jnp.minimum(i + 1, nb - 1))]


def _three_blocks(ref, starts):
    return jnp.concatenate([ref[pl.ds(s, BLOCK), :] for s in starts], axis=0)


def _group_scores(q_ref, rows, kd, bias_ref, i, nb):
    qq = jnp.concatenate([_split_pair(q_ref[rows, pr * PAIR:(pr + 1) * PAIR]) for pr in range(GROUP // 2)], axis=0)
    qq = qq * (HEAD_DIM ** -0.5)
    s = lax.dot_general(qq, kd, _DIMS["nt"], preferred_element_type=F32)
    s = s + bias_ref[...].reshape(GROUP * BLOCK, 3 * BLOCK)
    kj = lax.broadcasted_iota(jnp.int32, (1, 3 * BLOCK), 1)
    outside = jnp.logical_or(jnp.logical_and(i == 0, kj < BLOCK), jnp.logical_and(i == nb - 1, kj >= 2 * BLOCK))
    return qq, jnp.where(outside, NEG_INF, s)


def _per_head_rows(values):
    head = lax.broadcasted_iota(jnp.int32, (GROUP * BLOCK, 1), 0) // BLOCK
    out = jnp.full((GROUP * BLOCK, 1), values[0], F32)
    for g in range(1, GROUP):
        out = jnp.where(head == g, values[g], out)
    return out


def _attn_specs(T, q_off, Q_BLOCKS):
    gw = GROUP * HEAD_DIM
    return dict(
        sink=pl.BlockSpec(memory_space=pltpu.SMEM),
        q=pl.BlockSpec((Q_BLOCKS * BLOCK, gw), lambda j, i: (i, q_off // gw + j)),
        kv=pl.BlockSpec((T, PAIR), lambda j, i: (0, j)),
        bias=pl.BlockSpec((GROUP, BLOCK, 3 * BLOCK), lambda j, i: (j, 0, 0)),
        o=pl.BlockSpec((Q_BLOCKS * BLOCK, gw), lambda j, i: (i, j)))


def _attn_fwd(proj, q_off, kexp, vexp, bias, sink, name, comm=None):
    T = proj.shape[0]
    nb = T // BLOCK
    Q_BLOCKS = min(Q_BLOCKS_FWD, nb)
    sp = _attn_specs(T, q_off, Q_BLOCKS)
    steps = N_KV_HEADS * (nb // Q_BLOCKS)
    n_ci = len(comm.ins) if comm is not None else 0
    n_co = len(comm.outs) if comm is not None else 0

    def body(*refs):
        sink_ref, q_ref, ke_ref, ve_ref, bias_ref = refs[:5]
        comm_in = refs[5:5 + n_ci]
        o_ref, lse_ref = refs[5 + n_ci:7 + n_ci]
        comm_out = refs[7 + n_ci:7 + n_ci + n_co]
        sems = refs[7 + n_ci + n_co:]
        j, i = pl.program_id(0), pl.program_id(1)
        step = j * (nb // Q_BLOCKS) + i
        if comm is not None:
            @pl.when(step == 0)
            def _():
                comm.start(comm_in, comm_out, sems)

        low = _low_half((BLOCK, PAIR))
        sk = _per_head_rows([sink_ref[GROUP * j + g] for g in range(GROUP)])
        for b in range(Q_BLOCKS):
            blk = i * Q_BLOCKS + b
            rows = slice(b * BLOCK, (b + 1) * BLOCK)
            starts = _key_blocks(blk, nb)
            kd = _three_blocks(ke_ref, starts)
            vv = _split_pair(_three_blocks(ve_ref, starts))
            _, s = _group_scores(q_ref, rows, kd, bias_ref, blk, nb)
            m = jnp.maximum(jnp.max(s, axis=-1, keepdims=True), sk)
            p = jnp.exp(s - m)
            denom = jnp.sum(p, axis=-1, keepdims=True) + jnp.exp(sk - m)
            p = (p / denom).astype(BF16)
            lse = m + jnp.log(denom)
            for pr in range(GROUP // 2):
                lanes = slice(pr * PAIR, (pr + 1) * PAIR)
                a, c = slice(2 * pr * BLOCK, (2 * pr + 1) * BLOCK), slice((2 * pr + 1) * BLOCK, (2 * pr + 2) * BLOCK)
                pp = jnp.concatenate([p[a], p[c]], axis=1)
                o_ref[rows, lanes] = lax.dot_general(pp, vv, _DIMS["nn"], preferred_element_type=F32).astype(BF16)
                lse_ref[rows, lanes] = jnp.where(low, lse[a], lse[c])

        if comm is not None:
            @pl.when(step == (7 * steps) // 8)
            def _():
                comm.mid(comm_in, comm_out, sems)

            @pl.when(step == steps - 1)
            def _():
                comm.finish(comm_in, comm_out, sems)

    out_shape = [jax.ShapeDtypeStruct((T, N_HEADS * HEAD_DIM), BF16), jax.ShapeDtypeStruct((T, N_HEADS * HEAD_DIM), F32)]
    in_specs = [sp["sink"], sp["q"], sp["kv"], sp["kv"], sp["bias"]]
    if comm is None:
        att, lse = pl.pallas_call(
            body, name=name, grid=(N_KV_HEADS, nb // Q_BLOCKS), in_specs=in_specs, out_specs=[sp["o"], sp["o"]],
            out_shape=out_shape, compiler_params=_params("parallel", "parallel"),
        )(sink, proj, kexp, vexp, bias)
        return att, lse, []
    outs = pl.pallas_call(
        body, name=name, grid=(N_KV_HEADS, nb // Q_BLOCKS),
        in_specs=in_specs + [HBM_SPEC] * n_ci, out_specs=[sp["o"], sp["o"]] + [HBM_SPEC] * n_co,
        out_shape=out_shape + list(comm.outs), scratch_shapes=list(comm.sems),
        compiler_params=_params("arbitrary", "arbitrary"),
    )(sink, proj, kexp, vexp, bias, *comm.ins)
    return outs[0], outs[1], outs[2:]


def _attn_bwd(proj, q_off, kexp, vexp, bias, sink, out, lse, dout, dproj, name, comm=None):
    T = proj.shape[0]
    nb = T // BLOCK
    Q_BLOCKS = min(Q_BLOCKS_BWD, nb)
    sp = _attn_specs(T, q_off, Q_BLOCKS)
    scale = HEAD_DIM ** -0.5
    steps = N_KV_HEADS * (nb // Q_BLOCKS)
    n_ci = len(comm.ins) if comm is not None else 0
    n_co = len(comm.outs) if comm is not None else 0

    def body(*refs):
        sink_ref, q_ref, ke_ref, ve_ref, bias_ref, o_ref, lse_ref, do_ref = refs[:8]
        comm_in = refs[9:9 + n_ci]
        dq_ref, dke_ref, dve_ref, ds_ref, dsink_ref = refs[9 + n_ci:14 + n_ci]
        comm_out = refs[14 + n_ci:14 + n_ci + n_co]
        sems = refs[14 + n_ci + n_co:]
        j, i = pl.program_id(0), pl.program_id(1)
        step = j * (nb // Q_BLOCKS) + i
        if comm is not None:
            @pl.when(step == 0)
            def _():
                comm.start(comm_in, comm_out, sems)

        @pl.when(i == 0)
        def _():
            dke_ref[...] = jnp.zeros(dke_ref.shape, F32)
            dve_ref[...] = jnp.zeros(dve_ref.shape, F32)
            ds_ref[...] = jnp.zeros(ds_ref.shape, F32)
            dsink_ref[...] = jnp.zeros(dsink_ref.shape, F32)

        low = _low_half((BLOCK, PAIR))
        for b in range(Q_BLOCKS):
            blk = i * Q_BLOCKS + b
            rows = slice(b * BLOCK, (b + 1) * BLOCK)
            starts = _key_blocks(blk, nb)
            kd = _three_blocks(ke_ref, starts)
            vd = _three_blocks(ve_ref, starts)
            kk = _split_pair(kd)
            qq, s = _group_scores(q_ref, rows, kd, bias_ref, blk, nb)
            lse_rows, deltas, dd = [], [], []
            for pr in range(GROUP // 2):
                lanes = slice(pr * PAIR, (pr + 1) * PAIR)
                l2 = lse_ref[rows, lanes]
                lse_rows += [jnp.max(jnp.where(low, l2, NEG_INF), axis=-1, keepdims=True),
                             jnp.max(jnp.where(low, NEG_INF, l2), axis=-1, keepdims=True)]
                do2 = do_ref[rows, lanes]
                prod = do2.astype(F32) * o_ref[rows, lanes].astype(F32)
                deltas += [jnp.sum(jnp.where(low, prod, 0.0), axis=-1, keepdims=True),
                           jnp.sum(jnp.where(low, 0.0, prod), axis=-1, keepdims=True)]
                dd.append(_split_pair(do2))
                head = GROUP * j + 2 * pr
                p_sink = jnp.exp(jnp.where(low, sink_ref[head], sink_ref[head + 1]) - l2)
                dsink_ref[:, lanes] += jnp.sum(-p_sink * jnp.where(low, deltas[-2], deltas[-1]), axis=0, keepdims=True)
            dd = jnp.concatenate(dd, axis=0)
            p = jnp.exp(s - jnp.concatenate(lse_rows, axis=0))
            dp = lax.dot_general(dd, vd, _DIMS["nt"], preferred_element_type=F32)
            ds = p * (dp - jnp.concatenate(deltas, axis=0))
            dsb = ds.astype(BF16)
            for pr in range(GROUP // 2):
                a, c = slice(2 * pr * BLOCK, (2 * pr + 1) * BLOCK), slice((2 * pr + 1) * BLOCK, (2 * pr + 2) * BLOCK)
                dq = lax.dot_general(jnp.concatenate([dsb[a], dsb[c]], axis=1), kk, _DIMS["nn"],
                                     preferred_element_type=F32) * scale
                dq_ref[rows, pr * PAIR:(pr + 1) * PAIR] = dq.astype(BF16)
            dk_acc = lax.dot_general(qq, dsb, _DIMS["tn"], preferred_element_type=F32)
            dv_acc = lax.dot_general(dd, p.astype(BF16), _DIMS["tn"], preferred_element_type=F32)
            ds_ref[...] += ds.reshape(GROUP, BLOCK, 3 * BLOCK)
            for t, start in enumerate(starts):
                dke_ref[:, pl.ds(start, BLOCK)] += dk_acc[:, t * BLOCK:(t + 1) * BLOCK]
                dve_ref[:, pl.ds(start, BLOCK)] += dv_acc[:, t * BLOCK:(t + 1) * BLOCK]

        if comm is not None:
            @pl.when(step == (7 * steps) // 8)
            def _():
                comm.mid(comm_in, comm_out, sems)

            @pl.when(step == steps - 1)
            def _():
                comm.finish(comm_in, comm_out, sems)

    kv_out = jax.ShapeDtypeStruct((N_KV_HEADS * PAIR, T), F32)
    kvt_spec = pl.BlockSpec((PAIR, T), lambda j, i: (j, 0))
    job_ins, job_outs, job_sems = (comm.ins, comm.outs, comm.sems) if comm is not None else ([], [], [])
    outs = pl.pallas_call(
        body, name=name, grid=(N_KV_HEADS, nb // Q_BLOCKS),
        in_specs=[sp["sink"], sp["q"], sp["kv"], sp["kv"], sp["bias"], sp["o"], sp["o"], sp["o"], ANY_SPEC]
        + [HBM_SPEC] * n_ci,
        out_specs=[sp["q"], kvt_spec, kvt_spec, sp["bias"],
                   pl.BlockSpec((1, GROUP * HEAD_DIM), lambda j, i: (0, j))] + [HBM_SPEC] * n_co,
        out_shape=[jax.ShapeDtypeStruct(dproj.shape, BF16), kv_out, kv_out,
                   jax.ShapeDtypeStruct((N_HEADS, BLOCK, 3 * BLOCK), F32),
                   jax.ShapeDtypeStruct((1, N_HEADS * HEAD_DIM), F32)] + list(job_outs),
        scratch_shapes=list(job_sems), input_output_aliases={8: 0},
        compiler_params=_params("arbitrary" if comm is not None else "parallel", "arbitrary"),
    )(sink, proj, kexp, vexp, bias, out, lse, dout, dproj, *job_ins)
    return outs[:5], outs[5:]


GATE_COLS = 512


def _sigmoid(x):
    return 1.0 / (1.0 + jnp.exp(-x))


def _gate_specs(D, gate_off):
    nc = D // GATE_COLS
    base = gate_off // GATE_COLS
    return [pl.BlockSpec((ROWS, GATE_COLS), lambda i, c=base + g * nc + h: (i, c)) for g in range(3) for h in range(nc)]


def _merge_fwd(proj, gate_off, ya, yp, yt, scale, name):
    T, D = ya.shape
    nc = D // GATE_COLS

    def body(*refs):
        gates = refs[:3 * nc]
        ya_ref, yp_ref, yt_ref, s_ref, o_ref = refs[3 * nc:]
        for h in range(nc):
            cols = slice(h * GATE_COLS, (h + 1) * GATE_COLS)
            merged = (_sigmoid(_f32(gates[h])) * ya_ref[:, cols].astype(F32)
                      + _sigmoid(_f32(gates[nc + h])) * (yp_ref[:, cols].astype(F32) * s_ref[:, cols])
                      + _sigmoid(_f32(gates[2 * nc + h])) * yt_ref[:, cols].astype(F32))
            o_ref[:, cols] = merged.astype(BF16)

    yspec = _row_spec(D)
    return pl.pallas_call(
        body, name=name, grid=(T // ROWS,),
        in_specs=_gate_specs(D, gate_off) + [yspec, yspec, yspec, _const_spec((1, D))], out_specs=yspec,
        out_shape=jax.ShapeDtypeStruct((T, D), BF16), compiler_params=_params("parallel"),
    )(*([proj] * (3 * nc)), ya, yp, yt, scale)


def _merge_bwd(proj, gate_off, ya, yp, yt, scale, dm, name):
    T, D = ya.shape
    nc = D // GATE_COLS
    base = gate_off // GATE_COLS

    def body(gate_ref, ya_ref, yp_ref, yt_ref, s_ref, dm_ref, dg_ref, dya_ref, dyp_ref, dyt_ref, ds_ref):
        i, n = pl.program_id(0), pl.program_id(1)
        sg = _sigmoid(_f32(gate_ref))
        for g, (y_ref, dy_ref) in enumerate(((ya_ref, dya_ref), (yp_ref, dyp_ref), (yt_ref, dyt_ref))):
            for h in range(nc):
                @pl.when(n == g * nc + h)
                def _(g=g, h=h, y_ref=y_ref, dy_ref=dy_ref):
                    cols = slice(h * GATE_COLS, (h + 1) * GATE_COLS)
                    dy = dm_ref[:, cols].astype(F32) * sg
                    y = y_ref[:, cols].astype(F32)
                    if g == 1:
                        s_v = s_ref[:, cols]
                        part = jnp.sum(dy * y, axis=0, keepdims=True)

                        @pl.when(i == 0)
                        def _():
                            ds_ref[:, cols] = part

                        @pl.when(i > 0)
                        def _():
                            ds_ref[:, cols] += part

                        y = y * s_v
                        dy_ref[:, cols] = (dy * s_v).astype(BF16)
                    else:
                        dy_ref[:, cols] = dy.astype(BF16)
                    dg_ref[...] = (dy * y * (1.0 - sg)).astype(BF16)

    rows = _tile(T, 4 * ROWS)
    yspec = pl.BlockSpec((rows, D), lambda i, n: (i, 0))
    gspec = pl.BlockSpec((rows, GATE_COLS), lambda i, n: (i, base + n))
    sspec = pl.BlockSpec((1, D), lambda i, n: (0, 0))
    out = jax.ShapeDtypeStruct((T, D), BF16)
    return pl.pallas_call(
        body, name=name, grid=(T // rows, 3 * nc),
        in_specs=[gspec, yspec, yspec, yspec, sspec, yspec],
        out_specs=[gspec, yspec, yspec, yspec, sspec],
        out_shape=[jax.ShapeDtypeStruct(proj.shape, BF16), out, out, out, jax.ShapeDtypeStruct((1, D), F32)],
        compiler_params=_params("arbitrary", "arbitrary"),
    )(proj, ya, yp, yt, scale, dm)


def _swiglu_fwd(gu, name):
    T = gu.shape[0]
    F = gu.shape[1] // 2

    def body(gu_ref, o_ref):
        g = gu_ref[:, 0:F].astype(F32)
        o_ref[...] = (g * _sigmoid(g) * gu_ref[:, F:2 * F].astype(F32)).astype(BF16)

    return pl.pallas_call(
        body, name=name, grid=(T // ROWS,), in_specs=[_row_spec(2 * F)], out_specs=_row_spec(F),
        out_shape=jax.ShapeDtypeStruct((T, F), BF16), compiler_params=_params("parallel"),
    )(gu)


def _swiglu_bwd(gu, dact, name):
    T = gu.shape[0]
    F = gu.shape[1] // 2

    def body(gu_ref, d_ref, o_ref):
        g, d = gu_ref[:, 0:F].astype(F32), d_ref[...].astype(F32)
        sg = _sigmoid(g)
        o_ref[:, 0:F] = (d * gu_ref[:, F:2 * F].astype(F32) * sg * (1.0 + g * (1.0 - sg))).astype(BF16)
        o_ref[:, F:2 * F] = (d * g * sg).astype(BF16)

    return pl.pallas_call(
        body, name=name, grid=(T // ROWS,), in_specs=[_row_spec(2 * F), _row_spec(F)], out_specs=_row_spec(2 * F),
        out_shape=jax.ShapeDtypeStruct((T, 2 * F), BF16), compiler_params=_params("parallel"),
    )(gu, dact)


def _carried(plan, key, *args, **kwargs):
    job = plan.job(key) if plan is not None else None
    if job is None:
        return _matmul(*args, **kwargs)
    out, extra = _matmul(*args, comm=job, **kwargs)
    plan.done(key, extra)
    return out


def _local_step(x, target, wts, small, hooks=None):
    T, D = x.shape
    depth = small["g_mix"].shape[0]
    wts = list(wts) + [None] * (depth - len(wts))
    gate_off = wts[0]["w_inT"].shape[0] - 3 * D
    q_off = 4 * D
    bucket = _bucket_table()
    bias = _bias_table(small["rel_bias"], bucket, "bias_table")

    saved = []
    for l in range(depth):
        n = f"l{l}_"
        if hooks is not None and l > 0:
            wts[l] = hooks.weights(l)
        w = wts[l]
        plan = hooks.plan_fwd(l) if hooks is not None else None
        h = _rms_fwd(x, small["g_mix"][l], n + "rms_mix")
        proj = _carried(plan, "proj", h, w["w_inT"], "nt", BF16, n + "proj", tn_cap=WIDE_TILE)
        z, p = _mixer_fwd(proj, small["conv_w"][l], n + "mixer")
        kexp, vexp = _kv_expand(proj, q_off + D, n + "kv_expand")
        sink = small["attn_sink"][l]
        job = plan.job("attn") if plan is not None else None
        att, lse, extra = _attn_fwd(proj, q_off, kexp, vexp, bias, sink, n + "attn", comm=job)
        if job is not None:
            plan.done("attn", extra)
        ya = _matmul(z, w["w_a_out"], "nn", BF16, n + "ya", tm_cap=TALL_TILE)
        yp = _pool_mm(p, w["w_pool"], "nn", BF16, n + "yp")
        yt = _matmul(att, w["w_attn_out"], "nn", BF16, n + "yt", tm_cap=TALL_TILE)
        merged = _merge_fwd(proj, gate_off, ya, yp, yt, small["pool_scale"][l], n + "merge")
        x1 = _matmul(merged, w["w_o"], "nn", F32, n + "x1", res=x)
        h2 = _rms_fwd(x1, small["g_ffn"][l], n + "rms_ffn")
        gu = _carried(plan, "gu", h2, w["w_guT"], "nt", BF16, n + "gu", tn_cap=WIDE_TILE)
        act = _swiglu_fwd(gu, n + "swiglu")
        ff = w["w_down"].shape[0]
        x2 = _carried(plan, "x2", act, w["w_down"], "nn", F32, n + "x2", res=x1, tn_cap=512, tk_cap=ff)
        saved.append(dict(x=x, h=h, proj=proj, z=z, p=p, kexp=kexp, vexp=vexp, sink=sink, lse=lse, att=att,
                          ya=ya, yp=yp, yt=yt, merged=merged, x1=x1, h2=h2, gu=gu, act=act))
        x = x2

    loss, dx, dxb, dg_final = _loss_head(x, small["g_final"], target, "loss_head")

    gw = [None] * depth
    gs = {k_: [None] * depth for k_ in ("conv_w", "pool_scale", "g_mix", "g_ffn", "attn_sink")}
    ds_total = None
    for l in reversed(range(depth)):
        n = f"l{l}_b_"
        s, w, g = saved[l], wts[l], {}
        plan = hooks.plan_bwd(l) if hooks is not None else None
        ff = w["w_down"].shape[0]
        g["w_down"] = _carried(plan, "dw_down", s["act"], dxb, "tn", BF16, n + "dw_down", tm_cap=ff, tk_cap=512)
        dact = _matmul(dxb, w["w_down"], "nt", BF16, n + "dact", tm_cap=512, tn_cap=ff)
        dgu = _swiglu_bwd(s["gu"], dact, n + "swiglu")
        g["w_guT"] = _carried(plan, "dw_gu", dgu, s["h2"], "tn", BF16, n + "dw_gu", tm_cap=WIDE_TILE)
        dh2 = _carried(plan, "dh2", dgu, w["w_guT"], "nn", F32, n + "dh2", tn_cap=512, tk_cap=ff)
        dx1, dx1b, gs["g_ffn"][l] = _rms_bwd(s["x1"], small["g_ffn"][l], dh2, dx, n + "rms_ffn")
        g["w_o"] = _matmul(s["merged"], dx1b, "tn", BF16, n + "dw_o")
        dm = _matmul(dx1b, w["w_o"], "nt", BF16, n + "dmerged", tm_cap=TALL_TILE)
        dproj, dya, dyp, dyt, gs["pool_scale"][l] = _merge_bwd(
            s["proj"], gate_off, s["ya"], s["yp"], s["yt"], small["pool_scale"][l], dm, n + "merge")
        g["w_a_out"] = _matmul(s["z"], dya, "tn", BF16, n + "dw_a_out")
        dz = _matmul(dya, w["w_a_out"], "nt", BF16, n + "dz", tm_cap=TALL_TILE)
        g["w_pool"] = _pool_mm(s["p"], dyp, "tn", F32, n + "dw_pool")
        dp = _pool_mm(dyp, w["w_pool"], "nt", BF16, n + "dp")
        g["w_attn_out"] = _matmul(s["att"], dyt, "tn", BF16, n + "dw_attn_out")
        if hooks is not None:
            hooks.early_grads(l, g)
        datt = _carried(plan, "datt", dyt, w["w_attn_out"], "nt", BF16, n + "datt", tm_cap=TALL_TILE)
        dproj, gs["conv_w"][l] = _mixer_bwd(s["proj"], small["conv_w"][l], dz, dp, dproj, n + "mixer")
        job = plan.job("attn_b") if plan is not None else None
        (dproj, dke, dve, ds_sum, dsink), extra = _attn_bwd(
            s["proj"], q_off, s["kexp"], s["vexp"], bias, s["sink"], s["att"], s["lse"], datt, dproj, n + "attn",
            comm=job)
        if job is not None:
            plan.done("attn_b", extra)
        gs["attn_sink"][l] = dsink.reshape(N_HEADS, HEAD_DIM)[:, 0]
        ds_total = ds_sum if ds_total is None else ds_total + ds_sum
        dproj = _kv_fold(dke, dve, dproj, q_off + D, n + "kv_fold")
        g["w_inT"] = _carried(plan, "dw_in", dproj, s["h"], "tn", BF16, n + "dw_in", tm_cap=WIDE_TILE)
        dh = _carried(plan, "dh", dproj, w["w_inT"], "nn", F32, n + "dh", tk_cap=2816)
        dx, dxb, gs["g_mix"][l] = _rms_bwd(s["x"], small["g_mix"][l], dh, dx1, n + "rms_mix")
        gw[l] = g
        if hooks is not None:
            hooks.grads(l, g)

    d_rel =_bias_grad(ds_total, bucket, "bias_grad")[:, :, 0].T
    gs = {k_: jnp.stack(v_) for k_, v_ in gs.items()}
    gs["rel_bias"] = d_rel
    gs["g_final"] = dg_final
    return loss, dx, gw, gs


def _place():
    return lax.axis_index("x"), lax.axis_index("y"), lax.axis_index("c")


class _GatherJob:
    def __init__(self, parts):
        n = len(parts)
        self.n = n
        self.ins = list(parts)
        self.outs = [jax.ShapeDtypeStruct((N_DEV,) + p.shape, p.dtype) for p in parts]
        self.sems = [pltpu.SemaphoreType.DMA((7 * n,)), pltpu.SemaphoreType.DMA((7 * n,)), pltpu.SemaphoreType.DMA((n,))]

    def _copies(self, ins, outs, sems):
        send_sems, recv_sems, local_sems = sems
        x, y, c = _place()
        me, sibling = (x, y, c), (x, y, 1 - c)
        chips = [(1 - x, y), (x, 1 - y), (1 - x, 1 - y)]

        def rows(t, px, py, pc):
            return outs[t].at[4 * px + 2 * py + pc]

        def copy(t, k, block, to, src=None):
            return pltpu.make_async_remote_copy(
                src_ref=rows(t, *block) if src is None else src, dst_ref=rows(t, *block),
                send_sem=send_sems.at[7 * t + k], recv_sem=recv_sems.at[7 * t + k], device_id=to, device_id_type=MESH)

        ts = range(self.n)
        own = [pltpu.make_async_copy(ins[t], rows(t, *me), local_sems.at[t]) for t in ts]
        first = [copy(t, 0, me, sibling, src=ins[t]) for t in ts]
        first += [copy(t, 1 + j, me, (*chip, c), src=ins[t]) for t in ts for j, chip in enumerate(chips)]
        landed = [copy(t, 1 + j, (*chip, c), me) for j, chip in enumerate(chips) for t in ts]
        passed = [copy(t, 4 + j, (*chip, c), sibling) for j, chip in enumerate(chips) for t in ts]
        last = [copy(t, 0, sibling, me) for t in ts]
        last += [copy(t, 4 + j, (*chip, 1 - c), me) for t in ts for j, chip in enumerate(chips)]
        return own, first, landed, passed, last

    def start(self, ins, outs, sems):
        own, first, _, _, _ = self._copies(ins, outs, sems)
        for cp in own + first:
            cp.start()

    def mid(self, ins, outs, sems):
        _, _, landed, passed, _ = self._copies(ins, outs, sems)
        for arrived, onward in zip(landed, passed):
            arrived.wait_recv()
            onward.start()

    def finish(self, ins, outs, sems):
        own, first, _, passed, last = self._copies(ins, outs, sems)
        for cp in last:
            cp.wait_recv()
        for cp in first + passed:
            cp.wait_send()
        for cp in own:
            cp.wait()


class _SwapJob:
    def __init__(self, g):
        self.ins = [g]
        self.outs = [jax.ShapeDtypeStruct(g.shape[:1] + g.shape[2:], g.dtype)]
        self.sems = [pltpu.SemaphoreType.DMA, pltpu.SemaphoreType.DMA]

    def _copy(self, ins, outs, sems):
        x, y, c = _place()
        return pltpu.make_async_remote_copy(src_ref=ins[0].at[pl.ds(0, ins[0].shape[0]), 1 - c], dst_ref=outs[0],
                                            send_sem=sems[0], recv_sem=sems[1], device_id=(x, y, 1 - c),
                                            device_id_type=MESH)

    def start(self, ins, outs, sems):
        self._copy(ins, outs, sems).start()

    def mid(self, ins, outs, sems):
        pass

    def finish(self, ins, outs, sems):
        self._copy(ins, outs, sems).wait()


class _ExchangeJob:
    def __init__(self, p, row0, rows):
        self.row0, self.rows = row0, rows
        self.ins = [p]
        self.outs = [jax.ShapeDtypeStruct((3, rows) + p.shape[2:], p.dtype)]
        self.sems = [pltpu.SemaphoreType.DMA((3,)), pltpu.SemaphoreType.DMA((3,))]

    def _copies(self, ins, outs, sems):
        x, y, c = _place()
        chips = [(1 - x, y), (x, 1 - y), (1 - x, 1 - y)]
        return [pltpu.make_async_remote_copy(
            src_ref=ins[0].at[2 * px + py, pl.ds(self.row0, self.rows)], dst_ref=outs[0].at[k],
            send_sem=sems[0].at[k], recv_sem=sems[1].at[k], device_id=(px, py, c), device_id_type=MESH)
            for k, (px, py) in enumerate(chips)]

    def start(self, ins, outs, sems):
        for cp in self._copies(ins, outs, sems):
            cp.start()

    def mid(self, ins, outs, sems):
        pass

    def finish(self, ins, outs, sems):
        for cp in self._copies(ins, outs, sems):
            cp.wait()


def _run_job(job, name):
    n_in, n_out = len(job.ins), len(job.outs)

    def body(*refs):
        ins, outs, sems = refs[:n_in], refs[n_in:n_in + n_out], refs[n_in + n_out:]
        job.start(ins, outs, sems)
        job.mid(ins, outs, sems)
        job.finish(ins, outs, sems)

    return pl.pallas_call(
        body, name=name, in_specs=[HBM_SPEC] * n_in, out_specs=[HBM_SPEC] * n_out, out_shape=list(job.outs),
        scratch_shapes=list(job.sems),
    )(*job.ins)


SUM_ROWS_CAP = 576


def _sum_parts(own, index, others, out_dtype, name, own_row0=0, own_step=0):
    R = others.shape[1]
    common = math.gcd(R, own.shape[1], own_row0 or R)
    rows = next(t for t in range(min(common, SUM_ROWS_CAP) // 16 * 16, 0, -16) if common % t == 0)
    k = others.shape[0]
    assert own_row0 % rows == 0 and own.shape[1] % rows == 0
    blk0 = own_row0 // rows
    per_own = own.shape[1] // rows

    def own_block(i, idx):
        if own_step:
            return (idx[0] + own_step * (i // per_own), i % per_own, 0)
        return (idx[0], blk0 + i, 0)

    def body(idx_ref, own_ref, *refs):
        del idx_ref
        acc = own_ref[...].astype(F32)
        for r in refs[:k]:
            acc = acc + r[...].astype(F32)
        refs[k][...] = acc.astype(out_dtype)

    grid_spec = pltpu.PrefetchScalarGridSpec(
        num_scalar_prefetch=1, grid=(R // rows,),
        in_specs=[pl.BlockSpec((None, rows, LANES), own_block)]
        + [pl.BlockSpec((None, rows, LANES), lambda i, idx, j=j: (j, i, 0)) for j in range(k)],
        out_specs=pl.BlockSpec((rows, LANES), lambda i, idx: (i, 0)))
    return pl.pallas_call(
        body, name=name, grid_spec=grid_spec,
        out_shape=jax.ShapeDtypeStruct((R, LANES), out_dtype), compiler_params=_params("parallel"),
    )(jnp.reshape(index, (1,)).astype(jnp.int32), own, *([others] * k))


def _adamw(w, g, m, v, name):
    shape = w.shape
    cols = shape[-1]
    rows_total = w.size // cols
    w2, g2, m2, v2 = (a.reshape(rows_total, cols) for a in (w, g, m, v))
    rows = rows_total
    if rows_total > ROWS:
        rows = next(r for r in range(ROWS, 0, -8) if rows_total % r == 0)

    def body(w_ref, g_ref, m_ref, v_ref, d_ref, nm_ref, nv_ref):
        gv = g_ref[...]
        nm = ADAM_B1 * m_ref[...] + (1.0 - ADAM_B1) * gv
        nv = ADAM_B2 * v_ref[...] + (1.0 - ADAM_B2) * (gv * gv)
        m_hat = nm / (1.0 - ADAM_B1 ** ADAM_STEP)
        v_hat = nv / (1.0 - ADAM_B2 ** ADAM_STEP)
        d_ref[...] = -ADAM_LR * (m_hat / (jnp.sqrt(v_hat) + ADAM_EPS) + ADAM_WD * w_ref[...])
        nm_ref[...] = nm
        nv_ref[...] = nv

    spec = pl.BlockSpec((rows, cols), lambda i: (i, 0))
    out = jax.ShapeDtypeStruct((rows_total, cols), F32)
    d, nm, nv = pl.pallas_call(
        body, name=name, grid=(rows_total // rows,), in_specs=[spec] * 4, out_specs=[spec] * 3,
        out_shape=[out, out, out], compiler_params=_params("parallel"),
    )(w2, g2, m2, v2)
    return d.reshape(shape), nm.reshape(shape), nv.reshape(shape)


BIG = ("w_in", "w_a_out", "w_pool", "w_attn_out", "w_o", "w_gu", "w_down")


LOCAL = dict(w_in="w_inT", w_a_out="w_a_out", w_pool="w_pool", w_attn_out="w_attn_out", w_o="w_o", w_gu="w_guT",
             w_down="w_down")


def _shard_rows(w, l):
    out = []
    for name in BIG:
        a = w[name][l]
        if name in ("w_in", "w_gu"):
            a = a.T
        elif name == "w_pool":
            a = a.reshape(-1, a.shape[-1])
        out.append(a.astype(BF16))
    return out


def _full_weights(names, gathered, w):
    out = {}
    for name, g in zip(names, gathered):
        if name == "w_pool":
            G, rg, cg = w[name].shape[1:]
            out[name] = jnp.transpose(g.reshape(N_DEV, G, rg, cg), (1, 0, 2, 3)).reshape(G, N_DEV * rg, cg)
        else:
            out[LOCAL[name]] = g.reshape(N_DEV * g.shape[1], g.shape[2])
    return out


def _split_grads(g, w, names=BIG, multiple=1):
    parts, spans, at = [], {}, 0
    for name in names:
        a = g[LOCAL[name]].astype(BF16)
        if name == "w_pool":
            G, rg, cg = w[name].shape[1:]
            a = jnp.transpose(a.reshape(G, N_DEV, rg, cg), (1, 0, 2, 3))
        a = a.reshape(N_DEV, -1, LANES)
        spans[name] = (at, at + a.shape[1])
        at += a.shape[1]
        parts.append(a)
    if at % multiple:
        parts.append(jnp.zeros((N_DEV, multiple - at % multiple, LANES), BF16))
    return jnp.concatenate(parts, axis=1), spans


def _own_grads(pieces, w):
    out = {}
    for name in BIG:
        per_layer = []
        for layer in pieces:
            packed, spans = next((p, s) for p, s in layer if name in s)
            per_layer.append(packed[spans[name][0]:spans[name][1]])
        a = jnp.stack(per_layer)
        if name in ("w_in", "w_gu"):
            sh = w[name].shape
            a = jnp.swapaxes(a.reshape(sh[0], sh[2], sh[1]), 1, 2)
        out[name] = a.reshape(w[name].shape)
    return out


SQUARE = ("w_a_out", "w_pool", "w_attn_out", "w_o")


class _Prefetch:
    def __init__(self, schedule, assign):
        self.schedule, self.assign = schedule, assign

    def job(self, key):
        if key not in self.assign:
            return None
        parts = []
        for layer, names in self.assign[key]:
            shards = dict(zip(BIG, _shard_rows(self.schedule.w, layer)))
            parts += [shards[name] for name in names]
        return _GatherJob(parts)

    def done(self, key, outs):
        for layer, names in self.assign[key]:
            self.schedule.arrived(layer, names, outs[:len(names)])
            outs = outs[len(names):]


class _Reduce:
    def __init__(self, split, spans, place, tag, swap_key="dw_down", carriers=("dw_gu", "dh2", "dw_in", "dh"),
                 chunk_rows=(640, 640, 768, 512)):
        self.split, self.spans, self.tag = split, spans, tag
        self.swap_key, self.carriers = swap_key, carriers
        self.core, self.chip = place[2], 2 * place[0] + place[1]
        self.rows = split.shape[1]
        self.chunks, at = [], 0
        for rows in chunk_rows:
            rows = min(rows, self.rows - at) if len(self.chunks) + 1 < len(chunk_rows) else self.rows - at
            if rows > 0:
                self.chunks.append((at, rows))
                at += rows
        assert at == self.rows and len(self.chunks) <= len(carriers)
        self.sums = [None] * len(self.chunks)

    def _swap_job(self):
        return _SwapJob(self.split.reshape(4, 2, self.rows, LANES))

    def _pair_sum(self, from_sibling):
        pair = _sum_parts(self.split, self.core, from_sibling.reshape(1, 4 * self.rows, LANES), BF16,
                          self.tag + "pair_sum", own_step=2)
        self.pair = pair.reshape(4, self.rows, LANES)

    def _chip_sum(self, n, from_chips):
        self.sums[n] = _sum_parts(self.pair, self.chip, from_chips, F32, f"{self.tag}chip_sum{n}",
                                  own_row0=self.chunks[n][0])

    def job(self, key):
        if key == self.swap_key:
            return self._swap_job()
        if key in self.carriers[:len(self.chunks)]:
            return _ExchangeJob(self.pair, *self.chunks[self.carriers.index(key)])
        return None

    def done(self, key, outs):
        if key == self.swap_key:
            self._pair_sum(outs[0])
        else:
            self._chip_sum(self.carriers.index(key), outs[0])

    def run(self):
        self._pair_sum(_run_job(self._swap_job(), self.tag + "reduce_pair")[0])
        self.chunks, self.sums = [(0, self.rows)], [None]
        self._chip_sum(0, _run_job(_ExchangeJob(self.pair, 0, self.rows), self.tag + "reduce_chips")[0])
        return self.result()

    def result(self):
        return (self.sums[0] if len(self.sums) == 1 else jnp.concatenate(self.sums, axis=0)), self.spans


class _Plans:
    def __init__(self, plans):
        self.plans = plans

    def job(self, key):
        self.owner = next((p for p in self.plans if p.job(key) is not None), None)
        return self.owner.job(key) if self.owner is not None else None

    def done(self, key, outs):
        self.owner.done(key, outs)


class _Schedule:
    EARLY = ("w_gu", "w_down") + SQUARE
    EARLY_PAD = 512

    def __init__(self, w, place):
        self.w, self.place = w, place
        self.depth = w["w_in"].shape[0]
        self.full = [{} for _ in range(self.depth)]
        self.reduce = {}
        self.pieces = [None] * self.depth
        self.active = []

    def arrived(self, layer, names, gathered):
        self.full[layer].update(_full_weights(names, gathered, self.w))

    def plan_fwd(self, l):
        nxt = l + 1
        more = nxt < self.depth
        if l == 0:
            assign = dict(proj=[(0, ("w_gu",) + SQUARE)], attn=[(0, ("w_down",))])
            if more:
                assign["attn"].append((nxt, ("w_in",)))
                assign.update(gu=[(nxt, ("w_gu",))], x2=[(nxt, ("w_down",))])
        else:
            late = [(l, SQUARE)] if l == 1 else []
            assign = dict(attn=late)
            if more:
                assign.update(proj=[(nxt, ("w_in",))], gu=[(nxt, ("w_down", "w_attn_out", "w_o", "w_pool"))])
                assign["attn"] = late + [(nxt, ("w_gu",) if late else ("w_gu", "w_a_out"))]
                if late:
                    assign["x2"] = [(nxt, ("w_a_out",))]
            if not assign["attn"]:
                del assign["attn"]
        return _Prefetch(self, assign)

    def weights(self, l):
        return self.full[l]

    def plan_bwd(self, l):
        self.active = [self.reduce[l + 1]] if l + 1 in self.reduce else []
        return _Plans(self.active)

    def early_grads(self, l, g):
        if l == 0:
            split, spans = _split_grads(g, self.w, self.EARLY, self.EARLY_PAD)
            self.early = _Reduce(split, spans, self.place, "l0_early_", swap_key="datt", carriers=("attn_b",),
                                 chunk_rows=(split.shape[1],))
            self.active.append(self.early)

    def grads(self, l, g):
        if l + 1 in self.reduce:
            self.pieces[l + 1] = [self.reduce[l + 1].result()]
        if l == 0:
            last = _Reduce(*_split_grads(g, self.w, ("w_in",)), self.place, "l0_")
            self.pieces[0] = [self.early.result(), last.run()]
        else:
            self.reduce[l] = _Reduce(*_split_grads(g, self.w), self.place, f"l{l}_")


SMALL_ROWS = 32


def _pack_small(gs, L, D):
    rows = [gs["pool_scale"].reshape(L, D), gs["g_mix"].reshape(L, D), gs["g_ffn"].reshape(L, D),
            gs["g_final"].reshape(1, D), gs["conv_w"][:, :3].reshape(3 * L, D),
            jnp.pad(gs["attn_sink"].reshape(1, -1), ((0, 0), (0, D - L * N_HEADS))),
            jnp.pad(gs["rel_bias"].reshape(1, -1), ((0, 0), (0, D - N_BUCKETS * N_HEADS)))]
    a = jnp.concatenate(rows, axis=0)
    return jnp.pad(a, ((0, SMALL_ROWS - a.shape[0]), (0, 0)))


def _unpack_small(a, L, D):
    g = {}
    g["pool_scale"] = a[0:L]
    g["g_mix"] = a[L:2 * L]
    g["g_ffn"] = a[2 * L:3 * L]
    g["g_final"] = a[3 * L]
    g["conv_w"] = a[3 * L + 1:6 * L + 1].reshape(L, 3, 1, D)
    g["attn_sink"] = a[6 * L + 1, :L * N_HEADS].reshape(L, N_HEADS)
    g["rel_bias"] = a[6 * L + 2, :N_BUCKETS * N_HEADS].reshape(N_BUCKETS, N_HEADS)
    return g


WEIGHTS = ("w_in", "conv_w", "w_a_out", "w_pool", "pool_scale", "w_attn_out", "attn_sink", "w_o", "g_mix", "g_ffn",
           "w_gu", "w_down", "rel_bias", "g_final")


def kernel(x, w_in, conv_w, w_a_out, w_pool, pool_scale, w_attn_out, attn_sink, w_o, g_mix, g_ffn, w_gu, w_down, rel_bias, g_final, loss_target, m_w_in, m_conv_w, m_w_a_out, m_w_pool, m_pool_scale, m_w_attn_out, m_attn_sink, m_w_o, m_g_mix, m_g_ffn, m_w_gu, m_w_down, m_rel_bias, m_g_final, v_w_in, v_conv_w, v_w_a_out, v_w_pool, v_pool_scale, v_w_attn_out, v_attn_sink, v_w_o, v_g_mix, v_g_ffn, v_w_gu, v_w_down, v_rel_bias, v_g_final):
    w = dict(w_in=w_in, conv_w=conv_w, w_a_out=w_a_out, w_pool=w_pool, pool_scale=pool_scale, w_attn_out=w_attn_out,
             attn_sink=attn_sink, w_o=w_o, g_mix=g_mix, g_ffn=g_ffn, w_gu=w_gu, w_down=w_down, rel_bias=rel_bias,
             g_final=g_final)
    m = dict(w_in=m_w_in, conv_w=m_conv_w, w_a_out=m_w_a_out, w_pool=m_w_pool, pool_scale=m_pool_scale,
             w_attn_out=m_w_attn_out, attn_sink=m_attn_sink, w_o=m_w_o, g_mix=m_g_mix, g_ffn=m_g_ffn, w_gu=m_w_gu,
             w_down=m_w_down, rel_bias=m_rel_bias, g_final=m_g_final)
    v = dict(w_in=v_w_in, conv_w=v_conv_w, w_a_out=v_w_a_out, w_pool=v_w_pool, pool_scale=v_pool_scale,
             w_attn_out=v_w_attn_out, attn_sink=v_attn_sink, w_o=v_w_o, g_mix=v_g_mix, g_ffn=v_g_ffn, w_gu=v_w_gu,
             w_down=v_w_down, rel_bias=v_rel_bias, g_final=v_g_final)
    T, D = x.shape[1], x.shape[2]
    L = w_in.shape[0]
    cx, cy, cc = _place()

    schedule = _Schedule(w, (cx, cy, cc))
    schedule.arrived(0, ("w_in",), _run_job(_GatherJob(_shard_rows(w, 0)[:1]), "gather_w_in_l0"))
    cw = jnp.pad(conv_w.reshape(L * 3, -1), ((0, 16 - L * 3), (0, 0)))
    cw = _run_job(_GatherJob([cw]), "gather_conv_w")[0]
    cw = jnp.transpose(cw, (1, 0, 2)).reshape(16, -1)[:L * 3].reshape(L, 3, -1)
    small = dict(conv_w=jnp.pad(cw, ((0, 0), (0, 5), (0, 0))), pool_scale=pool_scale.reshape(L, 1, D),
                 g_mix=g_mix.reshape(L, 1, D), g_ffn=g_ffn.reshape(L, 1, D), attn_sink=attn_sink,
                 rel_bias=rel_bias, g_final=g_final.reshape(1, D))

    loss, dx, _, gs = _local_step(x[0], loss_target[0], [schedule.full[0]], small, schedule)
    loss = lax.psum(loss[0, 0], ("x", "y", "c"))
    grads = _own_grads(schedule.pieces, w)

    small_all = _run_job(_GatherJob([_pack_small(gs, L, D)]), "gather_small")[0]
    small_sum = _sum_parts(small_all, jnp.int32(0), small_all[1:], F32, "small_sum")
    gsm = _unpack_small(small_sum, L, D)
    W8 = D // N_DEV
    dev = 4 * cx + 2 * cy + cc
    gsm["conv_w"] = lax.dynamic_slice_in_dim(gsm["conv_w"], dev * W8, W8, axis=3)
    grads.update(gsm)

    deltas, new_m, new_v = {}, {}, {}
    for name in WEIGHTS:
        deltas[name], new_m[name], new_v[name] = _adamw(w[name], grads[name], m[name], v[name], "adamw_" + name)

    return (loss, dx[None], *[grads[n] for n in WEIGHTS], *[deltas[n] for n in WEIGHTS],
            *[new_m[n] for n in WEIGHTS], *[new_v[n] for n in WEIGHTS])
```

```python
import math

import jax
import jax.numpy as jnp
from jax import lax
from jax.experimental import pallas as pl
from jax.experimental.pallas import tpu as pltpu

F32 = jnp.float32
BF16 = jnp.bfloat16
MESH = pl.DeviceIdType.MESH

N_DEV = 8
N_HEADS = 16
N_KV_HEADS = 4
HEAD_DIM = 64
GROUP = N_HEADS // N_KV_HEADS
BLOCK = 128
WINDOW = 128
N_BUCKETS = 32
MAX_DISTANCE = 128
POOL_WINDOWS = (2, 4, 8, 16)
POOL_GROUPS = 4
HALO = 8
EPS = 1e-6
NEG_INF = -1e30

ADAM_LR = 0.001
ADAM_B1 = 0.9
ADAM_B2 = 0.999
ADAM_EPS = 1e-08
ADAM_WD = 0.01
ADAM_STEP = 10

LANES = 1024
VMEM_LIMIT_BYTES = 48 * 1024 * 1024


def _params(*sem):
    return pltpu.CompilerParams(dimension_semantics=sem, vmem_limit_bytes=VMEM_LIMIT_BYTES)


def _tile(n, cap):
    if n <= cap:
        return n
    for t in range(cap - cap % 128, 0, -128):
        if n % t == 0:
            return t
    raise ValueError(f"no tile for {n}")


TALL_TILE = 2048
WIDE_TILE = 2176

_DIMS = {"nn": (((1,), (0,)), ((), ())), "nt": (((1,), (1,)), ((), ())), "tn": (((0,), (0,)), ((), ()))}


HBM_SPEC = pl.BlockSpec(memory_space=pltpu.HBM)
ANY_SPEC = pl.BlockSpec(memory_space=pl.ANY)


def _matmul(a, b, mode, out_dtype, name, res=None, tm_cap=1024, tn_cap=1024, tk_cap=1024, comm=None, norm_g=None):
    if mode == "tn":
        K, M = a.shape
    else:
        M, K = a.shape
    N = b.shape[0] if mode == "nt" else b.shape[1]
    tm, tn, tk = _tile(M, tm_cap), _tile(N, tn_cap), _tile(K, tk_cap)
    nk = K // tk
    a_spec = pl.BlockSpec((tk, tm), lambda i, j, k: (k, i)) if mode == "tn" else pl.BlockSpec((tm, tk), lambda i, j, k: (i, k))
    b_spec = pl.BlockSpec((tn, tk), lambda i, j, k: (j, k)) if mode == "nt" else pl.BlockSpec((tk, tn), lambda i, j, k: (k, j))
    o_spec = pl.BlockSpec((tm, tn), lambda i, j, k: (i, j))
    dims = _DIMS[mode]
    has_res = res is not None
    has_norm = norm_g is not None
    assert not has_norm or (comm is None and tn == N)
    gm, gn = M // tm, N // tn
    steps = gm * gn * nk
    n_in = 2 + has_res + has_norm
    n_ci = len(comm.ins) if comm is not None else 0
    n_co = len(comm.outs) if comm is not None else int(has_norm)

    def body(*refs):
        a_ref, b_ref = refs[0], refs[1]
        res_ref = refs[2] if has_res else None
        g_ref = refs[2 + has_res] if has_norm else None
        comm_in = refs[n_in:n_in + n_ci]
        o_ref = refs[n_in + n_ci]
        comm_out = refs[n_in + n_ci + 1:n_in + n_ci + 1 + n_co]
        acc_ref = refs[n_in + n_ci + 1 + n_co]
        sems = refs[n_in + n_ci + 2 + n_co:]
        k = pl.program_id(2)
        step = (pl.program_id(0) * gn + pl.program_id(1)) * nk + k
        if comm is not None:
            @pl.when(step == 0)
            def _():
                comm.start(comm_in, comm_out, sems)

        part = lax.dot_general(a_ref[...], b_ref[...], dims, preferred_element_type=F32)

        @pl.when(k == 0)
        def _():
            acc_ref[...] = part

        @pl.when(k > 0)
        def _():
            acc_ref[...] += part

        @pl.when(k == nk - 1)
        def _():
            out = acc_ref[...]
            if has_res:
                out = out + res_ref[...]
            o_ref[...] = out.astype(out_dtype)
            if has_norm:
                r = lax.rsqrt(jnp.mean(out * out, axis=-1, keepdims=True) + EPS)
                comm_out[0][...] = (out * r * g_ref[...]).astype(BF16)

        if comm is not None:
            @pl.when(step == (7 * steps) // 8)
            def _():
                comm.mid(comm_in, comm_out, sems)

            @pl.when(step == steps - 1)
            def _():
                comm.finish(comm_in, comm_out, sems)

    in_specs = [a_spec, b_spec] + ([o_spec] if has_res else [])
    args = (a, b) + ((res,) if has_res else ())
    out_shape = jax.ShapeDtypeStruct((M, N), out_dtype)
    if has_norm:
        return pl.pallas_call(
            body, name=name, grid=(gm, gn, nk), in_specs=in_specs + [_const_spec((1, N))], out_specs=[o_spec, o_spec],
            out_shape=[out_shape, jax.ShapeDtypeStruct((M, N), BF16)], scratch_shapes=[pltpu.VMEM((tm, tn), F32)],
            compiler_params=_params("parallel", "parallel", "arbitrary"),
        )(*args, norm_g)
    if comm is None:
        return pl.pallas_call(
            body, name=name, grid=(gm, gn, nk), in_specs=in_specs, out_specs=o_spec, out_shape=out_shape,
            scratch_shapes=[pltpu.VMEM((tm, tn), F32)],
            compiler_params=_params("parallel", "parallel", "arbitrary"),
        )(*args)
    outs = pl.pallas_call(
        body, name=name, grid=(gm, gn, nk),
        in_specs=in_specs + [HBM_SPEC] * n_ci, out_specs=[o_spec] + [HBM_SPEC] * n_co,
        out_shape=[out_shape] + list(comm.outs),
        scratch_shapes=[pltpu.VMEM((tm, tn), F32)] + list(comm.sems),
        compiler_params=_params("arbitrary", "arbitrary", "arbitrary"),
    )(*args, *comm.ins)
    return outs[0], outs[1:]


def _pool_mm(a, w, mode, out_dtype, name):
    T = a.shape[0]
    G = POOL_GROUPS
    cg = a.shape[1] // G
    tm = _tile(T, 1024)
    nt = T // tm
    dims = _DIMS[mode]
    if mode == "tn":
        def body(a_ref, d_ref, o_ref):
            part = lax.dot_general(a_ref[...], d_ref[...], dims, preferred_element_type=F32)

            @pl.when(pl.program_id(1) == 0)
            def _():
                o_ref[...] = part

            @pl.when(pl.program_id(1) > 0)
            def _():
                o_ref[...] += part

        return pl.pallas_call(
            body, name=name, grid=(G, nt),
            in_specs=[pl.BlockSpec((tm, cg), lambda g, i: (i, g)), pl.BlockSpec((tm, cg), lambda g, i: (i, g))],
            out_specs=pl.BlockSpec((None, cg, cg), lambda g, i: (g, 0, 0)),
            out_shape=jax.ShapeDtypeStruct((G, cg, cg), F32),
            compiler_params=_params("parallel", "arbitrary"),
        )(a, w)

    def body(a_ref, w_ref, o_ref):
        o_ref[...] = lax.dot_general(a_ref[...], w_ref[...], dims, preferred_element_type=F32).astype(out_dtype)

    return pl.pallas_call(
        body, name=name, grid=(G, nt),
        in_specs=[pl.BlockSpec((tm, cg), lambda g, i: (i, g)), pl.BlockSpec((None, cg, cg), lambda g, i: (g, 0, 0))],
        out_specs=pl.BlockSpec((tm, cg), lambda g, i: (i, g)),
        out_shape=jax.ShapeDtypeStruct((T, G * cg), out_dtype),
        compiler_params=_params("parallel", "parallel"),
    )(a, w)


ROWS = 256
HALO_BLOCK = 16


def _row_spec(d, col=0, rows=ROWS):
    return pl.BlockSpec((rows, d), lambda i, col=col: (i, col))


def _const_spec(shape):
    return pl.BlockSpec(shape, lambda *_: (0,) * len(shape))


def _rms_fwd(x, g, name):
    T, D = x.shape

    def body(x_ref, g_ref, h_ref):
        xv = x_ref[...]
        r = lax.rsqrt(jnp.mean(xv * xv, axis=-1, keepdims=True) + EPS)
        h_ref[...] = (xv * r * g_ref[...]).astype(BF16)

    return pl.pallas_call(
        body, name=name, grid=(T // ROWS,),
        in_specs=[_row_spec(D), _const_spec((1, D))], out_specs=_row_spec(D),
        out_shape=jax.ShapeDtypeStruct((T, D), BF16), compiler_params=_params("parallel"),
    )(x, g)


def _accumulate(ref, part):
    first = pl.program_id(0) == 0

    @pl.when(first)
    def _():
        ref[...] = part

    @pl.when(jnp.logical_not(first))
    def _():
        ref[...] += part


def _rms_bwd(x, g, dh, dres, name):
    T, D = x.shape

    def body(x_ref, g_ref, dh_ref, dres_ref, dx_ref, dxb_ref, dg_ref):
        xv = x_ref[...]
        r = lax.rsqrt(jnp.mean(xv * xv, axis=-1, keepdims=True) + EPS)
        xhat = xv * r
        dh_v = dh_ref[...]
        dxhat = dh_v * g_ref[...]
        dx = dres_ref[...] + r * (dxhat - xhat * jnp.mean(dxhat * xhat, axis=-1, keepdims=True))
        dx_ref[...] = dx
        dxb_ref[...] = dx.astype(BF16)
        _accumulate(dg_ref, jnp.sum(dh_v * xhat, axis=0, keepdims=True))

    return pl.pallas_call(
        body, name=name, grid=(T // ROWS,),
        in_specs=[_row_spec(D), _const_spec((1, D)), _row_spec(D), _row_spec(D)],
        out_specs=[_row_spec(D), _row_spec(D), _const_spec((1, D))],
        out_shape=[jax.ShapeDtypeStruct((T, D), F32), jax.ShapeDtypeStruct((T, D), BF16),
                   jax.ShapeDtypeStruct((1, D), F32)],
        compiler_params=_params("arbitrary"),
    )(x, g, dh, dres)


def _loss_head(x, g, target, name):
    T, D = x.shape

    def body(x_ref, g_ref, t_ref, loss_ref, dx_ref, dxb_ref, dg_ref):
        xv = x_ref[...]
        gv = g_ref[...]
        r = lax.rsqrt(jnp.mean(xv * xv, axis=-1, keepdims=True) + EPS)
        xhat = xv * r
        err = xhat * gv - t_ref[...]
        loss = 0.5 * jnp.sum(jnp.mean(err * err, axis=-1, keepdims=True), axis=0, keepdims=True)
        dy = err * (1.0 / D)
        dxhat = dy * gv
        dx = r * (dxhat - xhat * jnp.mean(dxhat * xhat, axis=-1, keepdims=True))
        dx_ref[...] = dx
        dxb_ref[...] = dx.astype(BF16)
        _accumulate(loss_ref, loss)
        _accumulate(dg_ref, jnp.sum(dy * xhat, axis=0, keepdims=True))

    return pl.pallas_call(
        body, name=name, grid=(T // ROWS,),
        in_specs=[_row_spec(D), _const_spec((1, D)), _row_spec(D)],
        out_specs=[_const_spec((1, 1)), _row_spec(D), _row_spec(D), _const_spec((1, D))],
        out_shape=[jax.ShapeDtypeStruct((1, 1), F32), jax.ShapeDtypeStruct((T, D), F32),
                   jax.ShapeDtypeStruct((T, D), BF16), jax.ShapeDtypeStruct((1, D), F32)],
        compiler_params=_params("arbitrary"),
    )(x, g, target)


def _halo_specs(d, col, n_blocks):
    per = ROWS // HALO_BLOCK
    last = n_blocks * per - 1
    prev = pl.BlockSpec((HALO_BLOCK, d), lambda i, col=col: (jnp.maximum(i * per - 1, 0), col))
    nxt = pl.BlockSpec((HALO_BLOCK, d), lambda i, col=col: (jnp.minimum((i + 1) * per, last), col))
    return prev, nxt


def _with_halo(prev, cur, nxt, n_blocks):
    i = pl.program_id(0)
    prev = jnp.where(i > 0, prev[HALO_BLOCK - HALO:], 0.0)
    nxt = jnp.where(i < n_blocks - 1, nxt[:HALO], 0.0)
    return jnp.concatenate([prev, cur, nxt], axis=0)


def _f32(ref):
    return ref[...].astype(F32)


def _shift(ext, k):
    n = ext.shape[0]
    v = ext if k == 0 else pltpu.roll(ext, (-k) % n, 0)
    return v[HALO:HALO + ROWS]


def _shift_full(ext, k):
    n = ext.shape[0]
    return pltpu.roll(ext, (-k) % n, 0)


def _pool_counts(T):
    n = ROWS + 2 * HALO
    t = pl.program_id(0) * ROWS - HALO + lax.broadcasted_iota(jnp.int32, (n, 1), 0)
    out = []
    for w in POOL_WINDOWS:
        lo = jnp.maximum(t - w // 2, 0)
        hi = jnp.minimum(t + (w - 1 - w // 2), T - 1)
        out.append(jnp.maximum(hi - lo + 1, 1).astype(F32))
    return out


def _window_sums(e, sign):
    s2 = e + _shift_full(e, -sign)
    s4 = _shift_full(s2, -1) + _shift_full(s2, 1)
    s8 = _shift_full(s4, -2) + _shift_full(s4, 2)
    s16 = _shift_full(s8, -4) + _shift_full(s8, 4)
    return s2, s4, s8, s16


def _mixer_fwd(proj, conv_w, name):
    T = proj.shape[0]
    W = conv_w.shape[1]
    nb = T // ROWS
    cg = W // POOL_GROUPS

    def body(b_ref, c_ref, x_ref, u_ref, cp_ref, cn_ref, xp_ref, xn_ref, up_ref, un_ref, w_ref, z_ref, p_ref):
        uc = _with_halo(_f32(cp_ref) * _f32(xp_ref), _f32(c_ref) * _f32(x_ref), _f32(cn_ref) * _f32(xn_ref), nb)
        w0, w1, w2 = w_ref[0:1, :], w_ref[1:2, :], w_ref[2:3, :]
        y = w0 * _shift(uc, -1) + w1 * _shift(uc, 0) + w2 * _shift(uc, 1)
        z_ref[...] = (_f32(b_ref) * y).astype(BF16)
        e = _with_halo(_f32(up_ref), _f32(u_ref), _f32(un_ref), nb)
        counts = _pool_counts(T)
        for gi in range(POOL_GROUPS):
            eg = e[:, gi * cg:(gi + 1) * cg]
            s = _window_sums(eg, 1)[gi]
            p = s[HALO:HALO + ROWS] / counts[gi][HALO:HALO + ROWS] - eg[HALO:HALO + ROWS]
            p_ref[:, gi * cg:(gi + 1) * cg] = p.astype(BF16)

    halo = [s for col in (1, 2, 3) for s in _halo_specs(W, col, nb)]
    return pl.pallas_call(
        body, name=name, grid=(nb,),
        in_specs=[_row_spec(W, 0), _row_spec(W, 1), _row_spec(W, 2), _row_spec(W, 3)] + halo + [_const_spec((8, W))],
        out_specs=[_row_spec(W), _row_spec(W)],
        out_shape=[jax.ShapeDtypeStruct((T, W), BF16), jax.ShapeDtypeStruct((T, W), BF16)],
        compiler_params=_params("parallel"),
    )(proj, proj, proj, proj, proj, proj, proj, proj, proj, proj, conv_w)


def _mixer_bwd(proj, conv_w, dz, dp, dproj, name):
    T = proj.shape[0]
    W = conv_w.shape[1]
    nb = T // ROWS
    cg = W // POOL_GROUPS

    def body(b_ref, c_ref, x_ref, dz_ref, dp_ref,
             bp_ref, bn_ref, cp_ref, cn_ref, xp_ref, xn_ref, dzp_ref, dzn_ref, dpp_ref, dpn_ref, w_ref, _,
             o_ref, dw_ref):
        cv, xv, dzv = _f32(c_ref), _f32(x_ref), _f32(dz_ref)
        uc = _with_halo(_f32(cp_ref) * _f32(xp_ref), cv * xv, _f32(cn_ref) * _f32(xn_ref), nb)
        dy = _with_halo(_f32(dzp_ref) * _f32(bp_ref), dzv * _f32(b_ref), _f32(dzn_ref) * _f32(bn_ref), nb)
        w0, w1, w2 = w_ref[0:1, :], w_ref[1:2, :], w_ref[2:3, :]
        um, u0, up = _shift(uc, -1), _shift(uc, 0), _shift(uc, 1)
        o_ref[:, 0:W] = (dzv * (w0 * um + w1 * u0 + w2 * up)).astype(BF16)
        dy0 = _shift(dy, 0)
        duc = w0 * _shift(dy, 1) + w1 * dy0 + w2 * _shift(dy, -1)
        o_ref[:, W:2 * W] = (duc * xv).astype(BF16)
        o_ref[:, 2 * W:3 * W] = (duc * cv).astype(BF16)
        row = lax.broadcasted_iota(jnp.int32, (8, W), 0)
        dw = jnp.where(row == 0, jnp.sum(dy0 * um, axis=0, keepdims=True),
                       jnp.where(row == 1, jnp.sum(dy0 * u0, axis=0, keepdims=True),
                                 jnp.where(row == 2, jnp.sum(dy0 * up, axis=0, keepdims=True), 0.0)))
        _accumulate(dw_ref, dw)
        d = _with_halo(_f32(dpp_ref), _f32(dp_ref), _f32(dpn_ref), nb)
        counts = _pool_counts(T)
        for gi in range(POOL_GROUPS):
            dg = d[:, gi * cg:(gi + 1) * cg]
            s = _window_sums(dg / counts[gi], -1)[gi]
            o_ref[:, 3 * W + gi * cg:3 * W + (gi + 1) * cg] = (s[HALO:HALO + ROWS] - dg[HALO:HALO + ROWS]).astype(BF16)

    def halo(col):
        return list(_halo_specs(W, col, nb))

    return pl.pallas_call(
        body, name=name, grid=(nb,),
        in_specs=[_row_spec(W, 0), _row_spec(W, 1), _row_spec(W, 2), _row_spec(W), _row_spec(W)]
        + halo(0) + halo(1) + halo(2) + halo(0) + halo(0) + [_const_spec((8, W)), ANY_SPEC],
        out_specs=[_row_spec(4 * W), _const_spec((8, W))],
        out_shape=[jax.ShapeDtypeStruct(dproj.shape, BF16), jax.ShapeDtypeStruct((8, W), F32)],
        input_output_aliases={16: 0}, compiler_params=_params("arbitrary"),
    )(proj, proj, proj, dz, dp, proj, proj, proj, proj, proj, proj, dz, dz, dp, dp, conv_w, dproj)


def _t5_bucket(rel):
    half = N_BUCKETS // 2
    max_exact = half // 2
    ret = jnp.where(rel > 0, half, 0)
    n = jnp.abs(rel)
    nf = jnp.maximum(n, 1).astype(jnp.float32)
    large = max_exact + (jnp.log(nf / max_exact) / math.log(MAX_DISTANCE / max_exact)
                         * (half - max_exact)).astype(jnp.int32)
    large = jnp.minimum(large, half - 1)
    return ret + jnp.where(n < max_exact, n, large)


def _bucket_table():
    qi = jnp.arange(BLOCK)[:, None]
    kj = jnp.arange(3 * BLOCK)[None, :]
    rel = kj - BLOCK - qi
    return jnp.where(jnp.abs(rel) <= WINDOW, _t5_bucket(rel), -1).astype(jnp.int32)


def _bias_table(rel_bias, bucket, name):
    def body(rb_ref, bucket_ref, o_ref):
        h = pl.program_id(0)
        bk = bucket_ref[...]
        acc = jnp.full(bk.shape, NEG_INF, F32)
        for b in range(N_BUCKETS):
            acc = jnp.where(bk == b, rb_ref[b, h], acc)
        o_ref[...] = acc

    return pl.pallas_call(
        body, name=name, grid=(N_HEADS,),
        in_specs=[pl.BlockSpec(memory_space=pltpu.SMEM), _const_spec((BLOCK, 3 * BLOCK))],
        out_specs=pl.BlockSpec((None, BLOCK, 3 * BLOCK), lambda h: (h, 0, 0)),
        out_shape=jax.ShapeDtypeStruct((N_HEADS, BLOCK, 3 * BLOCK), F32),
        compiler_params=_params("parallel"),
    )(rel_bias, bucket)


def _bias_grad(ds_sum, bucket, name):
    def body(ds_ref, bucket_ref, o_ref):
        bk = bucket_ref[...]
        ds = ds_ref[...]
        row = lax.broadcasted_iota(jnp.int32, (N_BUCKETS, 128), 0)
        acc = jnp.zeros((N_BUCKETS, 128), F32)
        for b in range(N_BUCKETS):
            s = jnp.sum(jnp.sum(jnp.where(bk == b, ds, 0.0), axis=1, keepdims=True), axis=0, keepdims=True)
            acc = jnp.where(row == b, s, acc)
        o_ref[...] = acc

    return pl.pallas_call(
        body, name=name, grid=(N_HEADS,),
        in_specs=[pl.BlockSpec((None, BLOCK, 3 * BLOCK), lambda h: (h, 0, 0)), _const_spec((BLOCK, 3 * BLOCK))],
        out_specs=pl.BlockSpec((None, N_BUCKETS, 128), lambda h: (h, 0, 0)),
        out_shape=jax.ShapeDtypeStruct((N_HEADS, N_BUCKETS, 128), F32),
        compiler_params=_params("parallel"),
    )(ds_sum, bucket)


PAIR = 2 * HEAD_DIM
Q_BLOCKS_FWD = 4
Q_BLOCKS_BWD = 4


def _low_half(shape):
    return lax.broadcasted_iota(jnp.int32, shape, len(shape) - 1) % PAIR < HEAD_DIM


def _split_pair(a):
    low = _low_half(a.shape)
    zero = jnp.zeros_like(a)
    return jnp.concatenate([jnp.where(low, a, zero), jnp.where(low, zero, a)], axis=0)


def _kv_expand(proj, kv_off, name):
    T = proj.shape[0]
    kv_w = N_KV_HEADS * HEAD_DIM
    rows = _tile(T, 512)

    def body(k_ref, v_ref, ke_ref, ve_ref):
        for src, dst in ((k_ref, ke_ref), (v_ref, ve_ref)):
            for g in range(N_KV_HEADS // 2):
                x = src[:, g * PAIR:(g + 1) * PAIR].astype(F32)
                swapped = pltpu.roll(x, HEAD_DIM, 1)
                low = _low_half(x.shape)
                dst[:, 2 * g * PAIR:(2 * g + 1) * PAIR] = jnp.where(low, x, swapped).astype(BF16)
                dst[:, (2 * g + 1) * PAIR:(2 * g + 2) * PAIR] = jnp.where(low, swapped, x).astype(BF16)

    out = jax.ShapeDtypeStruct((T, N_KV_HEADS * PAIR), BF16)
    ospec = pl.BlockSpec((rows, N_KV_HEADS * PAIR), lambda i: (i, 0))
    return pl.pallas_call(
        body, name=name, grid=(T // rows,),
        in_specs=[pl.BlockSpec((rows, kv_w), lambda i: (i, kv_off // kv_w)),
                  pl.BlockSpec((rows, kv_w), lambda i: (i, kv_off // kv_w + 1))],
        out_specs=[ospec, ospec], out_shape=[out, out], compiler_params=_params("parallel"),
    )(proj, proj)


def _kv_fold(dke, dve, dproj, kv_off, name):
    T = dke.shape[1]
    kv_w = N_KV_HEADS * HEAD_DIM
    rows = _tile(T, 512)

    def body(dk_ref, dv_ref, _, o_ref):
        for n, src in enumerate((dk_ref, dv_ref)):
            for g in range(N_KV_HEADS // 2):
                a = src[2 * g * PAIR:(2 * g + 1) * PAIR, :].T
                b = src[(2 * g + 1) * PAIR:(2 * g + 2) * PAIR, :].T
                a = a + pltpu.roll(a, HEAD_DIM, 1)
                b = b + pltpu.roll(b, HEAD_DIM, 1)
                o_ref[:, n * kv_w + g * PAIR:n * kv_w + (g + 1) * PAIR] = jnp.where(_low_half(a.shape), a, b).astype(BF16)

    ispec = pl.BlockSpec((N_KV_HEADS * PAIR, rows), lambda i: (0, i))
    return pl.pallas_call(
        body, name=name, grid=(T // rows,), in_specs=[ispec, ispec, ANY_SPEC],
        out_specs=pl.BlockSpec((rows, 2 * kv_w), lambda i: (i, kv_off // (2 * kv_w))),
        out_shape=jax.ShapeDtypeStruct(dproj.shape, BF16), input_output_aliases={2: 0},
        compiler_params=_params("parallel"),
    )(dke, dve, dproj)


def _key_blocks(i, nb):
    return [pl.multiple_of(n * BLOCK, BLOCK) for n in (jnp.maximum(i - 1, 0), i, jnp.minimum(i + 1, nb - 1))]


def _three_blocks(ref, starts):
    return jnp.concatenate([ref[pl.ds(s, BLOCK), :] for s in starts], axis=0)


def _group_scores(q_ref, rows, kd, bias_ref, i, nb):
    qq = jnp.concatenate([_split_pair(q_ref[rows, pr * PAIR:(pr + 1) * PAIR]) for pr in range(GROUP // 2)], axis=0)
    qq = qq * (HEAD_DIM ** -0.5)
    s = lax.dot_general(qq, kd, _DIMS["nt"], preferred_element_type=F32)
    s = s + bias_ref[...].reshape(GROUP * BLOCK, 3 * BLOCK)
    kj = lax.broadcasted_iota(jnp.int32, (1, 3 * BLOCK), 1)
    outside = jnp.logical_or(jnp.logical_and(i == 0, kj < BLOCK), jnp.logical_and(i == nb - 1, kj >= 2 * BLOCK))
    return qq, jnp.where(outside, NEG_INF, s)


def _per_head_rows(values):
    head = lax.broadcasted_iota(jnp.int32, (GROUP * BLOCK, 1), 0) // BLOCK
    out = jnp.full((GROUP * BLOCK, 1), values[0], F32)
    for g in range(1, GROUP):
        out = jnp.where(head == g, values[g], out)
    return out


def _attn_specs(T, q_off, Q_BLOCKS):
    gw = GROUP * HEAD_DIM
    return dict(
        sink=pl.BlockSpec(memory_space=pltpu.SMEM),
        q=pl.BlockSpec((Q_BLOCKS * BLOCK, gw), lambda j, i: (i, q_off // gw + j)),
        kv=pl.BlockSpec((T, PAIR), lambda j, i: (0, j)),
        bias=pl.BlockSpec((GROUP, BLOCK, 3 * BLOCK), lambda j, i: (j, 0, 0)),
        o=pl.BlockSpec((Q_BLOCKS * BLOCK, gw), lambda j, i: (i, j)))


def _attn_fwd(proj, q_off, kexp, vexp, bias, sink, name, comm=None):
    T = proj.shape[0]
    nb = T // BLOCK
    Q_BLOCKS = min(Q_BLOCKS_FWD, nb)
    sp = _attn_specs(T, q_off, Q_BLOCKS)
    steps = N_KV_HEADS * (nb // Q_BLOCKS)
    n_ci = len(comm.ins) if comm is not None else 0
    n_co = len(comm.outs) if comm is not None else 0

    def body(*refs):
        sink_ref, q_ref, ke_ref, ve_ref, bias_ref = refs[:5]
        comm_in = refs[5:5 + n_ci]
        o_ref, lse_ref = refs[5 + n_ci:7 + n_ci]
        comm_out = refs[7 + n_ci:7 + n_ci + n_co]
        sems = refs[7 + n_ci + n_co:]
        j, i = pl.program_id(0), pl.program_id(1)
        step = j * (nb // Q_BLOCKS) + i
        if comm is not None:
            @pl.when(step == 0)
            def _():
                comm.start(comm_in, comm_out, sems)

        low = _low_half((BLOCK, PAIR))
        sk = _per_head_rows([sink_ref[GROUP * j + g] for g in range(GROUP)])
        for b in range(Q_BLOCKS):
            blk = i * Q_BLOCKS + b
            rows = slice(b * BLOCK, (b + 1) * BLOCK)
            starts = _key_blocks(blk, nb)
            kd = _three_blocks(ke_ref, starts)
            vv = _split_pair(_three_blocks(ve_ref, starts))
            _, s = _group_scores(q_ref, rows, kd, bias_ref, blk, nb)
            m = jnp.maximum(jnp.max(s, axis=-1, keepdims=True), sk)
            p = jnp.exp(s - m)
            denom = jnp.sum(p, axis=-1, keepdims=True) + jnp.exp(sk - m)
            p = (p / denom).astype(BF16)
            lse = m + jnp.log(denom)
            for pr in range(GROUP // 2):
                lanes = slice(pr * PAIR, (pr + 1) * PAIR)
                a, c = slice(2 * pr * BLOCK, (2 * pr + 1) * BLOCK), slice((2 * pr + 1) * BLOCK, (2 * pr + 2) * BLOCK)
                pp = jnp.concatenate([p[a], p[c]], axis=1)
                o_ref[rows, lanes] = lax.dot_general(pp, vv, _DIMS["nn"], preferred_element_type=F32).astype(BF16)
                lse_ref[rows, lanes] = jnp.where(low, lse[a], lse[c])

        if comm is not None:
            @pl.when(step == (7 * steps) // 8)
            def _():
                comm.mid(comm_in, comm_out, sems)

            @pl.when(step == steps - 1)
            def _():
                comm.finish(comm_in, comm_out, sems)

    out_shape = [jax.ShapeDtypeStruct((T, N_HEADS * HEAD_DIM), BF16), jax.ShapeDtypeStruct((T, N_HEADS * HEAD_DIM), F32)]
    in_specs = [sp["sink"], sp["q"], sp["kv"], sp["kv"], sp["bias"]]
    if comm is None:
        att, lse = pl.pallas_call(
            body, name=name, grid=(N_KV_HEADS, nb // Q_BLOCKS), in_specs=in_specs, out_specs=[sp["o"], sp["o"]],
            out_shape=out_shape, compiler_params=_params("parallel", "parallel"),
        )(sink, proj, kexp, vexp, bias)
        return att, lse, []
    outs = pl.pallas_call(
        body, name=name, grid=(N_KV_HEADS, nb // Q_BLOCKS),
        in_specs=in_specs + [HBM_SPEC] * n_ci, out_specs=[sp["o"], sp["o"]] + [HBM_SPEC] * n_co,
        out_shape=out_shape + list(comm.outs), scratch_shapes=list(comm.sems),
        compiler_params=_params("arbitrary", "arbitrary"),
    )(sink, proj, kexp, vexp, bias, *comm.ins)
    return outs[0], outs[1], outs[2:]


def _attn_bwd(proj, q_off, kexp, vexp, bias, sink, out, lse, dout, dproj, name, comm=None):
    T = proj.shape[0]
    nb = T // BLOCK
    Q_BLOCKS = min(Q_BLOCKS_BWD, nb)
    sp = _attn_specs(T, q_off, Q_BLOCKS)
    scale = HEAD_DIM ** -0.5
    steps = N_KV_HEADS * (nb // Q_BLOCKS)
    n_ci = len(comm.ins) if comm is not None else 0
    n_co = len(comm.outs) if comm is not None else 0

    def body(*refs):
        sink_ref, q_ref, ke_ref, ve_ref, bias_ref, o_ref, lse_ref, do_ref = refs[:8]
        comm_in = refs[9:9 + n_ci]
        dq_ref, dke_ref, dve_ref, ds_ref, dsink_ref = refs[9 + n_ci:14 + n_ci]
        comm_out = refs[14 + n_ci:14 + n_ci + n_co]
        sems = refs[14 + n_ci + n_co:]
        j, i = pl.program_id(0), pl.program_id(1)
        step = j * (nb // Q_BLOCKS) + i
        if comm is not None:
            @pl.when(step == 0)
            def _():
                comm.start(comm_in, comm_out, sems)

        @pl.when(i == 0)
        def _():
            dke_ref[...] = jnp.zeros(dke_ref.shape, F32)
            dve_ref[...] = jnp.zeros(dve_ref.shape, F32)
            ds_ref[...] = jnp.zeros(ds_ref.shape, F32)
            dsink_ref[...] = jnp.zeros(dsink_ref.shape, F32)

        low = _low_half((BLOCK, PAIR))
        for b in range(Q_BLOCKS):
            blk = i * Q_BLOCKS + b
            rows = slice(b * BLOCK, (b + 1) * BLOCK)
            starts = _key_blocks(blk, nb)
            kd = _three_blocks(ke_ref, starts)
            vd = _three_blocks(ve_ref, starts)
            kk = _split_pair(kd)
            qq, s = _group_scores(q_ref, rows, kd, bias_ref, blk, nb)
            lse_rows, deltas, dd = [], [], []
            for pr in range(GROUP // 2):
                lanes = slice(pr * PAIR, (pr + 1) * PAIR)
                l2 = lse_ref[rows, lanes]
                lse_rows += [jnp.max(jnp.where(low, l2, NEG_INF), axis=-1, keepdims=True),
                             jnp.max(jnp.where(low, NEG_INF, l2), axis=-1, keepdims=True)]
                do2 = do_ref[rows, lanes]
                prod = do2.astype(F32) * o_ref[rows, lanes].astype(F32)
                deltas += [jnp.sum(jnp.where(low, prod, 0.0), axis=-1, keepdims=True),
                           jnp.sum(jnp.where(low, 0.0, prod), axis=-1, keepdims=True)]
                dd.append(_split_pair(do2))
                head = GROUP * j + 2 * pr
                p_sink = jnp.exp(jnp.where(low, sink_ref[head], sink_ref[head + 1]) - l2)
                dsink_ref[:, lanes] += jnp.sum(-p_sink * jnp.where(low, deltas[-2], deltas[-1]), axis=0, keepdims=True)
            dd = jnp.concatenate(dd, axis=0)
            p = jnp.exp(s - jnp.concatenate(lse_rows, axis=0))
            dp = lax.dot_general(dd, vd, _DIMS["nt"], preferred_element_type=F32)
            ds = p * (dp - jnp.concatenate(deltas, axis=0))
            dsb = ds.astype(BF16)
            for pr in range(GROUP // 2):
                a, c = slice(2 * pr * BLOCK, (2 * pr + 1) * BLOCK), slice((2 * pr + 1) * BLOCK, (2 * pr + 2) * BLOCK)
                dq = lax.dot_general(jnp.concatenate([dsb[a], dsb[c]], axis=1), kk, _DIMS["nn"],
                                     preferred_element_type=F32) * scale
                dq_ref[rows, pr * PAIR:(pr + 1) * PAIR] = dq.astype(BF16)
            dk_acc = lax.dot_general(qq, dsb, _DIMS["tn"], preferred_element_type=F32)
            dv_acc = lax.dot_general(dd, p.astype(BF16), _DIMS["tn"], preferred_element_type=F32)
            ds_ref[...] += ds.reshape(GROUP, BLOCK, 3 * BLOCK)
            for t, start in enumerate(starts):
                dke_ref[:, pl.ds(start, BLOCK)] += dk_acc[:, t * BLOCK:(t + 1) * BLOCK]
                dve_ref[:, pl.ds(start, BLOCK)] += dv_acc[:, t * BLOCK:(t + 1) * BLOCK]

        if comm is not None:
            @pl.when(step == (7 * steps) // 8)
            def _():
                comm.mid(comm_in, comm_out, sems)

            @pl.when(step == steps - 1)
            def _():
                comm.finish(comm_in, comm_out, sems)

    kv_out = jax.ShapeDtypeStruct((N_KV_HEADS * PAIR, T), F32)
    kvt_spec = pl.BlockSpec((PAIR, T), lambda j, i: (j, 0))
    job_ins, job_outs, job_sems = (comm.ins, comm.outs, comm.sems) if comm is not None else ([], [], [])
    outs = pl.pallas_call(
        body, name=name, grid=(N_KV_HEADS, nb // Q_BLOCKS),
        in_specs=[sp["sink"], sp["q"], sp["kv"], sp["kv"], sp["bias"], sp["o"], sp["o"], sp["o"], ANY_SPEC]
        + [HBM_SPEC] * n_ci,
        out_specs=[sp["q"], kvt_spec, kvt_spec, sp["bias"],
                   pl.BlockSpec((1, GROUP * HEAD_DIM), lambda j, i: (0, j))] + [HBM_SPEC] * n_co,
        out_shape=[jax.ShapeDtypeStruct(dproj.shape, BF16), kv_out, kv_out,
                   jax.ShapeDtypeStruct((N_HEADS, BLOCK, 3 * BLOCK), F32),
                   jax.ShapeDtypeStruct((1, N_HEADS * HEAD_DIM), F32)] + list(job_outs),
        scratch_shapes=list(job_sems), input_output_aliases={8: 0},
        compiler_params=_params("arbitrary" if comm is not None else "parallel", "arbitrary"),
    )(sink, proj, kexp, vexp, bias, out, lse, dout, dproj, *job_ins)
    return outs[:5], outs[5:]


GATE_COLS = 512


def _sigmoid(x):
    return 1.0 / (1.0 + jnp.exp(-x))


def _gate_specs(D, gate_off):
    nc = D // GATE_COLS
    base = gate_off // GATE_COLS
    return [pl.BlockSpec((ROWS, GATE_COLS), lambda i, c=base + g * nc + h: (i, c)) for g in range(3) for h in range(nc)]


def _merge_fwd(proj, gate_off, ya, yp, yt, scale, name):
    T, D = ya.shape
    nc = D // GATE_COLS

    def body(*refs):
        gates = refs[:3 * nc]
        ya_ref, yp_ref, yt_ref, s_ref, o_ref = refs[3 * nc:]
        for h in range(nc):
            cols = slice(h * GATE_COLS, (h + 1) * GATE_COLS)
            merged = (_sigmoid(_f32(gates[h])) * ya_ref[:, cols].astype(F32)
                      + _sigmoid(_f32(gates[nc + h])) * (yp_ref[:, cols].astype(F32) * s_ref[:, cols])
                      + _sigmoid(_f32(gates[2 * nc + h])) * yt_ref[:, cols].astype(F32))
            o_ref[:, cols] = merged.astype(BF16)

    yspec = _row_spec(D)
    return pl.pallas_call(
        body, name=name, grid=(T // ROWS,),
        in_specs=_gate_specs(D, gate_off) + [yspec, yspec, yspec, _const_spec((1, D))], out_specs=yspec,
        out_shape=jax.ShapeDtypeStruct((T, D), BF16), compiler_params=_params("parallel"),
    )(*([proj] * (3 * nc)), ya, yp, yt, scale)


def _merge_bwd(proj, gate_off, ya, yp, yt, scale, dm, name):
    T, D = ya.shape
    nc = D // GATE_COLS
    base = gate_off // GATE_COLS

    def body(gate_ref, ya_ref, yp_ref, yt_ref, s_ref, dm_ref, dg_ref, dya_ref, dyp_ref, dyt_ref, ds_ref):
        i, n = pl.program_id(0), pl.program_id(1)
        sg = _sigmoid(_f32(gate_ref))
        for g, (y_ref, dy_ref) in enumerate(((ya_ref, dya_ref), (yp_ref, dyp_ref), (yt_ref, dyt_ref))):
            for h in range(nc):
                @pl.when(n == g * nc + h)
                def _(g=g, h=h, y_ref=y_ref, dy_ref=dy_ref):
                    cols = slice(h * GATE_COLS, (h + 1) * GATE_COLS)
                    dy = dm_ref[:, cols].astype(F32) * sg
                    y = y_ref[:, cols].astype(F32)
                    if g == 1:
                        s_v = s_ref[:, cols]
                        part = jnp.sum(dy * y, axis=0, keepdims=True)

                        @pl.when(i == 0)
                        def _():
                            ds_ref[:, cols] = part

                        @pl.when(i > 0)
                        def _():
                            ds_ref[:, cols] += part

                        y = y * s_v
                        dy_ref[:, cols] = (dy * s_v).astype(BF16)
                    else:
                        dy_ref[:, cols] = dy.astype(BF16)
                    dg_ref[...] = (dy * y * (1.0 - sg)).astype(BF16)

    rows = _tile(T, 4 * ROWS)
    yspec = pl.BlockSpec((rows, D), lambda i, n: (i, 0))
    gspec = pl.BlockSpec((rows, GATE_COLS), lambda i, n: (i, base + n))
    sspec = pl.BlockSpec((1, D), lambda i, n: (0, 0))
    out = jax.ShapeDtypeStruct((T, D), BF16)
    return pl.pallas_call(
        body, name=name, grid=(T // rows, 3 * nc),
        in_specs=[gspec, yspec, yspec, yspec, sspec, yspec],
        out_specs=[gspec, yspec, yspec, yspec, sspec],
        out_shape=[jax.ShapeDtypeStruct(proj.shape, BF16), out, out, out, jax.ShapeDtypeStruct((1, D), F32)],
        compiler_params=_params("arbitrary", "arbitrary"),
    )(proj, ya, yp, yt, scale, dm)


def _swiglu_fwd(gu, name):
    T = gu.shape[0]
    F = gu.shape[1] // 2

    def body(gu_ref, o_ref):
        g = gu_ref[:, 0:F].astype(F32)
        o_ref[...] = (g * _sigmoid(g) * gu_ref[:, F:2 * F].astype(F32)).astype(BF16)

    return pl.pallas_call(
        body, name=name, grid=(T // ROWS,), in_specs=[_row_spec(2 * F)], out_specs=_row_spec(F),
        out_shape=jax.ShapeDtypeStruct((T, F), BF16), compiler_params=_params("parallel"),
    )(gu)


def _swiglu_bwd(gu, dact, name):
    T = gu.shape[0]
    F = gu.shape[1] // 2

    def body(gu_ref, d_ref, o_ref):
        g, d = gu_ref[:, 0:F].astype(F32), d_ref[...].astype(F32)
        sg = _sigmoid(g)
        o_ref[:, 0:F] = (d * gu_ref[:, F:2 * F].astype(F32) * sg * (1.0 + g * (1.0 - sg))).astype(BF16)
        o_ref[:, F:2 * F] = (d * g * sg).astype(BF16)

    return pl.pallas_call(
        body, name=name, grid=(T // ROWS,), in_specs=[_row_spec(2 * F), _row_spec(F)], out_specs=_row_spec(2 * F),
        out_shape=jax.ShapeDtypeStruct((T, 2 * F), BF16), compiler_params=_params("parallel"),
    )(gu, dact)


def _carried(plan, key, *args, **kwargs):
    job = plan.job(key) if plan is not None else None
    if job is None:
        return _matmul(*args, **kwargs)
    out, extra = _matmul(*args, comm=job, **kwargs)
    plan.done(key, extra)
    return out


def _local_step(x, target, wts, small, hooks=None):
    T, D = x.shape
    depth = small["g_mix"].shape[0]
    wts = list(wts) + [None] * (depth - len(wts))
    gate_off = wts[0]["w_inT"].shape[0] - 3 * D
    q_off = 4 * D
    bucket = _bucket_table()
    bias = _bias_table(small["rel_bias"], bucket, "bias_table")

    saved = []
    for l in range(depth):
        n = f"l{l}_"
        if hooks is not None and l > 0:
            wts[l] = hooks.weights(l)
        w = wts[l]
        plan = hooks.plan_fwd(l) if hooks is not None else None
        h = _rms_fwd(x, small["g_mix"][l], n + "rms_mix")
        proj = _carried(plan, "proj", h, w["w_inT"], "nt", BF16, n + "proj", tn_cap=WIDE_TILE)
        z, p = _mixer_fwd(proj, small["conv_w"][l], n + "mixer")
        kexp, vexp = _kv_expand(proj, q_off + D, n + "kv_expand")
        sink = small["attn_sink"][l]
        job = plan.job("attn") if plan is not None else None
        att, lse, extra = _attn_fwd(proj, q_off, kexp, vexp, bias, sink, n + "attn", comm=job)
        if job is not None:
            plan.done("attn", extra)
        ya = _matmul(z, w["w_a_out"], "nn", BF16, n + "ya", tm_cap=TALL_TILE)
        yp = _pool_mm(p, w["w_pool"], "nn", BF16, n + "yp")
        yt = _matmul(att, w["w_attn_out"], "nn", BF16, n + "yt", tm_cap=TALL_TILE)
        merged = _merge_fwd(proj, gate_off, ya, yp, yt, small["pool_scale"][l], n + "merge")
        x1, h2 = _matmul(merged, w["w_o"], "nn", F32, n + "x1", res=x, norm_g=small["g_ffn"][l])
        gu = _carried(plan, "gu", h2, w["w_guT"], "nt", BF16, n + "gu", tn_cap=WIDE_TILE)
        act = _swiglu_fwd(gu, n + "swiglu")
        ff = w["w_down"].shape[0]
        x2 = _carried(plan, "x2", act, w["w_down"], "nn", F32, n + "x2", res=x1, tn_cap=512, tk_cap=ff)
        saved.append(dict(x=x, h=h, proj=proj, z=z, p=p, kexp=kexp, vexp=vexp, sink=sink, lse=lse, att=att,
                          ya=ya, yp=yp, yt=yt, merged=merged, x1=x1, h2=h2, gu=gu, act=act))
        x = x2

    loss, dx, dxb, dg_final = _loss_head(x, small["g_final"], target, "loss_head")

    gw = [None] * depth
    gs = {k_: [None] * depth for k_ in ("conv_w", "pool_scale", "g_mix", "g_ffn", "attn_sink")}
    ds_total = None
    for l in reversed(range(depth)):
        n = f"l{l}_b_"
        s, w, g = saved[l], wts[l], {}
        plan = hooks.plan_bwd(l) if hooks is not None else None
        ff = w["w_down"].shape[0]
        g["w_down"] = _carried(plan, "dw_down", s["act"], dxb, "tn", BF16, n + "dw_down", tm_cap=ff, tk_cap=512)
        dact = _matmul(dxb, w["w_down"], "nt", BF16, n + "dact", tm_cap=512, tn_cap=ff)
        dgu = _swiglu_bwd(s["gu"], dact, n + "swiglu")
        g["w_guT"] = _carried(plan, "dw_gu", dgu, s["h2"], "tn", BF16, n + "dw_gu", tm_cap=WIDE_TILE)
        dh2 = _carried(plan, "dh2", dgu, w["w_guT"], "nn", F32, n + "dh2", tn_cap=512, tk_cap=ff)
        dx1, dx1b, gs["g_ffn"][l] = _rms_bwd(s["x1"], small["g_ffn"][l], dh2, dx, n + "rms_ffn")
        g["w_o"] = _matmul(s["merged"], dx1b, "tn", BF16, n + "dw_o")
        dm = _matmul(dx1b, w["w_o"], "nt", BF16, n + "dmerged", tm_cap=TALL_TILE)
        dproj, dya, dyp, dyt, gs["pool_scale"][l] = _merge_bwd(
            s["proj"], gate_off, s["ya"], s["yp"], s["yt"], small["pool_scale"][l], dm, n + "merge")
        g["w_a_out"] = _matmul(s["z"], dya, "tn", BF16, n + "dw_a_out")
        dz = _matmul(dya, w["w_a_out"], "nt", BF16, n + "dz", tm_cap=TALL_TILE)
        g["w_pool"] = _pool_mm(s["p"], dyp, "tn", F32, n + "dw_pool")
        dp = _pool_mm(dyp, w["w_pool"], "nt", BF16, n + "dp")
        g["w_attn_out"] = _matmul(s["att"], dyt, "tn", BF16, n + "dw_attn_out")
        if hooks is not None:
            hooks.early_grads(l, g)
        datt = _carried(plan, "datt", dyt, w["w_attn_out"], "nt", BF16, n + "datt", tm_cap=TALL_TILE)
        dproj, gs["conv_w"][l] = _mixer_bwd(s["proj"], small["conv_w"][l], dz, dp, dproj, n + "mixer")
        job = plan.job("attn_b") if plan is not None else None
        (dproj, dke, dve, ds_sum, dsink), extra = _attn_bwd(
            s["proj"], q_off, s["kexp"], s["vexp"], bias, s["sink"], s["att"], s["lse"], datt, dproj, n + "attn",
            comm=job)
        if job is not None:
            plan.done("attn_b", extra)
        gs["attn_sink"][l] = dsink.reshape(N_HEADS, HEAD_DIM)[:, 0]
        ds_total = ds_sum if ds_total is None else ds_total + ds_sum
        dproj = _kv_fold(dke, dve, dproj, q_off + D, n + "kv_fold")
        g["w_inT"] = _carried(plan, "dw_in", dproj, s["h"], "tn", BF16, n + "dw_in", tm_cap=WIDE_TILE)
        dh = _carried(plan, "dh", dproj, w["w_inT"], "nn", F32, n + "dh", tk_cap=2816)
        dx, dxb, gs["g_mix"][l] = _rms_bwd(s["x"], small["g_mix"][l], dh, dx1, n + "rms_mix")
        gw[l] = g
        if hooks is not None:
            hooks.grads(l, g)

    d_rel =_bias_grad(ds_total, bucket, "bias_grad")[:, :, 0].T
    gs = {k_: jnp.stack(v_) for k_, v_ in gs.items()}
    gs["rel_bias"] = d_rel
    gs["g_final"] = dg_final
    return loss, dx, gw, gs


def _place():
    return lax.axis_index("x"), lax.axis_index("y"), lax.axis_index("c")


class _GatherJob:
    def __init__(self, parts):
        n = len(parts)
        self.n = n
        self.ins = list(parts)
        self.outs = [jax.ShapeDtypeStruct((N_DEV,) + p.shape, p.dtype) for p in parts]
        self.sems = [pltpu.SemaphoreType.DMA((7 * n,)), pltpu.SemaphoreType.DMA((7 * n,)), pltpu.SemaphoreType.DMA((n,))]

    def _copies(self, ins, outs, sems):
        send_sems, recv_sems, local_sems = sems
        x, y, c = _place()
        me, sibling = (x, y, c), (x, y, 1 - c)
        chips = [(1 - x, y), (x, 1 - y), (1 - x, 1 - y)]

        def rows(t, px, py, pc):
            return outs[t].at[4 * px + 2 * py + pc]

        def copy(t, k, block, to, src=None):
            return pltpu.make_async_remote_copy(
                src_ref=rows(t, *block) if src is None else src, dst_ref=rows(t, *block),
                send_sem=send_sems.at[7 * t + k], recv_sem=recv_sems.at[7 * t + k], device_id=to, device_id_type=MESH)

        ts = range(self.n)
        own = [pltpu.make_async_copy(ins[t], rows(t, *me), local_sems.at[t]) for t in ts]
        first = [copy(t, 0, me, sibling, src=ins[t]) for t in ts]
        first += [copy(t, 1 + j, me, (*chip, c), src=ins[t]) for t in ts for j, chip in enumerate(chips)]
        landed = [copy(t, 1 + j, (*chip, c), me) for j, chip in enumerate(chips) for t in ts]
        passed = [copy(t, 4 + j, (*chip, c), sibling) for j, chip in enumerate(chips) for t in ts]
        last = [copy(t, 0, sibling, me) for t in ts]
        last += [copy(t, 4 + j, (*chip, 1 - c), me) for t in ts for j, chip in enumerate(chips)]
        return own, first, landed, passed, last

    def start(self, ins, outs, sems):
        own, first, _, _, _ = self._copies(ins, outs, sems)
        for cp in own + first:
            cp.start()

    def mid(self, ins, outs, sems):
        _, _, landed, passed, _ = self._copies(ins, outs, sems)
        for arrived, onward in zip(landed, passed):
            arrived.wait_recv()
            onward.start()

    def finish(self, ins, outs, sems):
        own, first, _, passed, last = self._copies(ins, outs, sems)
        for cp in last:
            cp.wait_recv()
        for cp in first + passed:
            cp.wait_send()
        for cp in own:
            cp.wait()


class _SwapJob:
    def __init__(self, g):
        self.ins = [g]
        self.outs = [jax.ShapeDtypeStruct(g.shape[:1] + g.shape[2:], g.dtype)]
        self.sems = [pltpu.SemaphoreType.DMA, pltpu.SemaphoreType.DMA]

    def _copy(self, ins, outs, sems):
        x, y, c = _place()
        return pltpu.make_async_remote_copy(src_ref=ins[0].at[pl.ds(0, ins[0].shape[0]), 1 - c], dst_ref=outs[0],
                                            send_sem=sems[0], recv_sem=sems[1], device_id=(x, y, 1 - c),
                                            device_id_type=MESH)

    def start(self, ins, outs, sems):
        self._copy(ins, outs, sems).start()

    def mid(self, ins, outs, sems):
        pass

    def finish(self, ins, outs, sems):
        self._copy(ins, outs, sems).wait()


class _ExchangeJob:
    def __init__(self, p, row0, rows):
        self.row0, self.rows = row0, rows
        self.ins = [p]
        self.outs = [jax.ShapeDtypeStruct((3, rows) + p.shape[2:], p.dtype)]
        self.sems = [pltpu.SemaphoreType.DMA((3,)), pltpu.SemaphoreType.DMA((3,))]

    def _copies(self, ins, outs, sems):
        x, y, c = _place()
        chips = [(1 - x, y), (x, 1 - y), (1 - x, 1 - y)]
        return [pltpu.make_async_remote_copy(
            src_ref=ins[0].at[2 * px + py, pl.ds(self.row0, self.rows)], dst_ref=outs[0].at[k],
            send_sem=sems[0].at[k], recv_sem=sems[1].at[k], device_id=(px, py, c), device_id_type=MESH)
            for k, (px, py) in enumerate(chips)]

    def start(self, ins, outs, sems):
        for cp in self._copies(ins, outs, sems):
            cp.start()

    def mid(self, ins, outs, sems):
        pass

    def finish(self, ins, outs, sems):
        for cp in self._copies(ins, outs, sems):
            cp.wait()


def _run_job(job, name):
    n_in, n_out = len(job.ins), len(job.outs)

    def body(*refs):
        ins, outs, sems = refs[:n_in], refs[n_in:n_in + n_out], refs[n_in + n_out:]
        job.start(ins, outs, sems)
        job.mid(ins, outs, sems)
        job.finish(ins, outs, sems)

    return pl.pallas_call(
        body, name=name, in_specs=[HBM_SPEC] * n_in, out_specs=[HBM_SPEC] * n_out, out_shape=list(job.outs),
        scratch_shapes=list(job.sems),
    )(*job.ins)


SUM_ROWS_CAP = 576


def _sum_parts(own, index, others, out_dtype, name, own_row0=0, own_step=0):
    R = others.shape[1]
    common = math.gcd(R, own.shape[1], own_row0 or R)
    rows = next(t for t in range(min(common, SUM_ROWS_CAP) // 16 * 16, 0, -16) if common % t == 0)
    k = others.shape[0]
    assert own_row0 % rows == 0 and own.shape[1] % rows == 0
    blk0 = own_row0 // rows
    per_own = own.shape[1] // rows

    def own_block(i, idx):
        if own_step:
            return (idx[0] + own_step * (i // per_own), i % per_own, 0)
        return (idx[0], blk0 + i, 0)

    def body(idx_ref, own_ref, *refs):
        del idx_ref
        acc = own_ref[...].astype(F32)
        for r in refs[:k]:
            acc = acc + r[...].astype(F32)
        refs[k][...] = acc.astype(out_dtype)

    grid_spec = pltpu.PrefetchScalarGridSpec(
        num_scalar_prefetch=1, grid=(R // rows,),
        in_specs=[pl.BlockSpec((None, rows, LANES), own_block)]
        + [pl.BlockSpec((None, rows, LANES), lambda i, idx, j=j: (j, i, 0)) for j in range(k)],
        out_specs=pl.BlockSpec((rows, LANES), lambda i, idx: (i, 0)))
    return pl.pallas_call(
        body, name=name, grid_spec=grid_spec,
        out_shape=jax.ShapeDtypeStruct((R, LANES), out_dtype), compiler_params=_params("parallel"),
    )(jnp.reshape(index, (1,)).astype(jnp.int32), own, *([others] * k))


def _adamw(w, g, m, v, name):
    shape = w.shape
    cols = shape[-1]
    rows_total = w.size // cols
    w2, g2, m2, v2 = (a.reshape(rows_total, cols) for a in (w, g, m, v))
    rows = rows_total
    if rows_total > ROWS:
        rows = next(r for r in range(ROWS, 0, -8) if rows_total % r == 0)

    def body(w_ref, g_ref, m_ref, v_ref, d_ref, nm_ref, nv_ref):
        gv = g_ref[...]
        nm = ADAM_B1 * m_ref[...] + (1.0 - ADAM_B1) * gv
        nv = ADAM_B2 * v_ref[...] + (1.0 - ADAM_B2) * (gv * gv)
        m_hat = nm / (1.0 - ADAM_B1 ** ADAM_STEP)
        v_hat = nv / (1.0 - ADAM_B2 ** ADAM_STEP)
        d_ref[...] = -ADAM_LR * (m_hat / (jnp.sqrt(v_hat) + ADAM_EPS) + ADAM_WD * w_ref[...])
        nm_ref[...] = nm
        nv_ref[...] = nv

    spec = pl.BlockSpec((rows, cols), lambda i: (i, 0))
    out = jax.ShapeDtypeStruct((rows_total, cols), F32)
    d, nm, nv = pl.pallas_call(
        body, name=name, grid=(rows_total // rows,), in_specs=[spec] * 4, out_specs=[spec] * 3,
        out_shape=[out, out, out], compiler_params=_params("parallel"),
    )(w2, g2, m2, v2)
    return d.reshape(shape), nm.reshape(shape), nv.reshape(shape)


BIG = ("w_in", "w_a_out", "w_pool", "w_attn_out", "w_o", "w_gu", "w_down")


LOCAL = dict(w_in="w_inT", w_a_out="w_a_out", w_pool="w_pool", w_attn_out="w_attn_out", w_o="w_o", w_gu="w_guT",
             w_down="w_down")


def _shard_rows(w, l):
    out = []
    for name in BIG:
        a = w[name][l]
        if name in ("w_in", "w_gu"):
            a = a.T
        elif name == "w_pool":
            a = a.reshape(-1, a.shape[-1])
        out.append(a.astype(BF16))
    return out


def _full_weights(names, gathered, w):
    out = {}
    for name, g in zip(names, gathered):
        if name == "w_pool":
            G, rg, cg = w[name].shape[1:]
            out[name] = jnp.transpose(g.reshape(N_DEV, G, rg, cg), (1, 0, 2, 3)).reshape(G, N_DEV * rg, cg)
        else:
            out[LOCAL[name]] = g.reshape(N_DEV * g.shape[1], g.shape[2])
    return out


def _split_grads(g, w, names=BIG, multiple=1):
    parts, spans, at = [], {}, 0
    for name in names:
        a = g[LOCAL[name]].astype(BF16)
        if name == "w_pool":
            G, rg, cg = w[name].shape[1:]
            a = jnp.transpose(a.reshape(G, N_DEV, rg, cg), (1, 0, 2, 3))
        a = a.reshape(N_DEV, -1, LANES)
        spans[name] = (at, at + a.shape[1])
        at += a.shape[1]
        parts.append(a)
    if at % multiple:
        parts.append(jnp.zeros((N_DEV, multiple - at % multiple, LANES), BF16))
    return jnp.concatenate(parts, axis=1), spans


def _own_grads(pieces, w):
    out = {}
    for name in BIG:
        per_layer = []
        for layer in pieces:
            packed, spans = next((p, s) for p, s in layer if name in s)
            per_layer.append(packed[spans[name][0]:spans[name][1]])
        a = jnp.stack(per_layer)
        if name in ("w_in", "w_gu"):
            sh = w[name].shape
            a = jnp.swapaxes(a.reshape(sh[0], sh[2], sh[1]), 1, 2)
        out[name] = a.reshape(w[name].shape)
    return out


SQUARE = ("w_a_out", "w_pool", "w_attn_out", "w_o")


class _Prefetch:
    def __init__(self, schedule, assign):
        self.schedule, self.assign = schedule, assign

    def job(self, key):
        if key not in self.assign:
            return None
        parts = []
        for layer, names in self.assign[key]:
            shards = dict(zip(BIG, _shard_rows(self.schedule.w, layer)))
            parts += [shards[name] for name in names]
        return _GatherJob(parts)

    def done(self, key, outs):
        for layer, names in self.assign[key]:
            self.schedule.arrived(layer, names, outs[:len(names)])
            outs = outs[len(names):]


class _Reduce:
    def __init__(self, split, spans, place, tag, swap_key="dw_down", carriers=("dw_gu", "dh2", "dw_in", "dh"),
                 chunk_rows=(640, 640, 768, 512)):
        self.split, self.spans, self.tag = split, spans, tag
        self.swap_key, self.carriers = swap_key, carriers
        self.core, self.chip = place[2], 2 * place[0] + place[1]
        self.rows = split.shape[1]
        self.chunks, at = [], 0
        for rows in chunk_rows:
            rows = min(rows, self.rows - at) if len(self.chunks) + 1 < len(chunk_rows) else self.rows - at
            if rows > 0:
                self.chunks.append((at, rows))
                at += rows
        assert at == self.rows and len(self.chunks) <= len(carriers)
        self.sums = [None] * len(self.chunks)

    def _swap_job(self):
        return _SwapJob(self.split.reshape(4, 2, self.rows, LANES))

    def _pair_sum(self, from_sibling):
        pair = _sum_parts(self.split, self.core, from_sibling.reshape(1, 4 * self.rows, LANES), BF16,
                          self.tag + "pair_sum", own_step=2)
        self.pair = pair.reshape(4, self.rows, LANES)

    def _chip_sum(self, n, from_chips):
        self.sums[n] = _sum_parts(self.pair, self.chip, from_chips, F32, f"{self.tag}chip_sum{n}",
                                  own_row0=self.chunks[n][0])

    def job(self, key):
        if key == self.swap_key:
            return self._swap_job()
        if key in self.carriers[:len(self.chunks)]:
            return _ExchangeJob(self.pair, *self.chunks[self.carriers.index(key)])
        return None

    def done(self, key, outs):
        if key == self.swap_key:
            self._pair_sum(outs[0])
        else:
            self._chip_sum(self.carriers.index(key), outs[0])

    def run(self):
        self._pair_sum(_run_job(self._swap_job(), self.tag + "reduce_pair")[0])
        self.chunks, self.sums = [(0, self.rows)], [None]
        self._chip_sum(0, _run_job(_ExchangeJob(self.pair, 0, self.rows), self.tag + "reduce_chips")[0])
        return self.result()

    def result(self):
        return (self.sums[0] if len(self.sums) == 1 else jnp.concatenate(self.sums, axis=0)), self.spans


class _Plans:
    def __init__(self, plans):
        self.plans = plans

    def job(self, key):
        self.owner = next((p for p in self.plans if p.job(key) is not None), None)
        return self.owner.job(key) if self.owner is not None else None

    def done(self, key, outs):
        self.owner.done(key, outs)


class _Schedule:
    EARLY = ("w_gu", "w_down") + SQUARE
    EARLY_PAD = 512

    def __init__(self, w, place):
        self.w, self.place = w, place
        self.depth = w["w_in"].shape[0]
        self.full = [{} for _ in range(self.depth)]
        self.reduce = {}
        self.pieces = [None] * self.depth
        self.active = []

    def arrived(self, layer, names, gathered):
        self.full[layer].update(_full_weights(names, gathered, self.w))

    def plan_fwd(self, l):
        nxt = l + 1
        more = nxt < self.depth
        if l == 0:
            assign = dict(proj=[(0, ("w_gu",) + SQUARE)], attn=[(0, ("w_down",))])
            if more:
                assign["attn"].append((nxt, ("w_in",)))
                assign.update(gu=[(nxt, ("w_gu",))], x2=[(nxt, ("w_down",))])
        else:
            late = [(l, SQUARE)] if l == 1 else []
            assign = dict(attn=late)
            if more:
                assign.update(proj=[(nxt, ("w_in",))], gu=[(nxt, ("w_down", "w_attn_out", "w_o", "w_pool"))])
                assign["attn"] = late + [(nxt, ("w_gu",) if late else ("w_gu", "w_a_out"))]
                if late:
                    assign["x2"] = [(nxt, ("w_a_out",))]
            if not assign["attn"]:
                del assign["attn"]
        return _Prefetch(self, assign)

    def weights(self, l):
        return self.full[l]

    def plan_bwd(self, l):
        self.active = [self.reduce[l + 1]] if l + 1 in self.reduce else []
        return _Plans(self.active)

    def early_grads(self, l, g):
        if l == 0:
            split, spans = _split_grads(g, self.w, self.EARLY, self.EARLY_PAD)
            self.early = _Reduce(split, spans, self.place, "l0_early_", swap_key="datt", carriers=("attn_b",),
                                 chunk_rows=(split.shape[1],))
            self.active.append(self.early)

    def grads(self, l, g):
        if l + 1 in self.reduce:
            self.pieces[l + 1] = [self.reduce[l + 1].result()]
        if l == 0:
            last = _Reduce(*_split_grads(g, self.w, ("w_in",)), self.place, "l0_")
            self.pieces[0] = [self.early.result(), last.run()]
        else:
            self.reduce[l] = _Reduce(*_split_grads(g, self.w), self.place, f"l{l}_")


SMALL_ROWS = 32


def _pack_small(gs, L, D):
    rows = [gs["pool_scale"].reshape(L, D), gs["g_mix"].reshape(L, D), gs["g_ffn"].reshape(L, D),
            gs["g_final"].reshape(1, D), gs["conv_w"][:, :3].reshape(3 * L, D),
            jnp.pad(gs["attn_sink"].reshape(1, -1), ((0, 0), (0, D - L * N_HEADS))),
            jnp.pad(gs["rel_bias"].reshape(1, -1), ((0, 0), (0, D - N_BUCKETS * N_HEADS)))]
    a = jnp.concatenate(rows, axis=0)
    return jnp.pad(a, ((0, SMALL_ROWS - a.shape[0]), (0, 0)))


def _unpack_small(a, L, D):
    g = {}
    g["pool_scale"] = a[0:L]
    g["g_mix"] = a[L:2 * L]
    g["g_ffn"] = a[2 * L:3 * L]
    g["g_final"] = a[3 * L]
    g["conv_w"] = a[3 * L + 1:6 * L + 1].reshape(L, 3, 1, D)
    g["attn_sink"] = a[6 * L + 1, :L * N_HEADS].reshape(L, N_HEADS)
    g["rel_bias"] = a[6 * L + 2, :N_BUCKETS * N_HEADS].reshape(N_BUCKETS, N_HEADS)
    return g


WEIGHTS = ("w_in", "conv_w", "w_a_out", "w_pool", "pool_scale", "w_attn_out", "attn_sink", "w_o", "g_mix", "g_ffn",
           "w_gu", "w_down", "rel_bias", "g_final")


def kernel(x, w_in, conv_w, w_a_out, w_pool, pool_scale, w_attn_out, attn_sink, w_o, g_mix, g_ffn, w_gu, w_down, rel_bias, g_final, loss_target, m_w_in, m_conv_w, m_w_a_out, m_w_pool, m_pool_scale, m_w_attn_out, m_attn_sink, m_w_o, m_g_mix, m_g_ffn, m_w_gu, m_w_down, m_rel_bias, m_g_final, v_w_in, v_conv_w, v_w_a_out, v_w_pool, v_pool_scale, v_w_attn_out, v_attn_sink, v_w_o, v_g_mix, v_g_ffn, v_w_gu, v_w_down, v_rel_bias, v_g_final):
    w = dict(w_in=w_in, conv_w=conv_w, w_a_out=w_a_out, w_pool=w_pool, pool_scale=pool_scale, w_attn_out=w_attn_out,
             attn_sink=attn_sink, w_o=w_o, g_mix=g_mix, g_ffn=g_ffn, w_gu=w_gu, w_down=w_down, rel_bias=rel_bias,
             g_final=g_final)
    m = dict(w_in=m_w_in, conv_w=m_conv_w, w_a_out=m_w_a_out, w_pool=m_w_pool, pool_scale=m_pool_scale,
             w_attn_out=m_w_attn_out, attn_sink=m_attn_sink, w_o=m_w_o, g_mix=m_g_mix, g_ffn=m_g_ffn, w_gu=m_w_gu,
             w_down=m_w_down, rel_bias=m_rel_bias, g_final=m_g_final)
    v = dict(w_in=v_w_in, conv_w=v_conv_w, w_a_out=v_w_a_out, w_pool=v_w_pool, pool_scale=v_pool_scale,
             w_attn_out=v_w_attn_out, attn_sink=v_attn_sink, w_o=v_w_o, g_mix=v_g_mix, g_ffn=v_g_ffn, w_gu=v_w_gu,
             w_down=v_w_down, rel_bias=v_rel_bias, g_final=v_g_final)
    T, D = x.shape[1], x.shape[2]
    L = w_in.shape[0]
    cx, cy, cc = _place()

    schedule = _Schedule(w, (cx, cy, cc))
    schedule.arrived(0, ("w_in",), _run_job(_GatherJob(_shard_rows(w, 0)[:1]), "gather_w_in_l0"))
    cw = jnp.pad(conv_w.reshape(L * 3, -1), ((0, 16 - L * 3), (0, 0)))
    cw = _run_job(_GatherJob([cw]), "gather_conv_w")[0]
    cw = jnp.transpose(cw, (1, 0, 2)).reshape(16, -1)[:L * 3].reshape(L, 3, -1)
    small = dict(conv_w=jnp.pad(cw, ((0, 0), (0, 5), (0, 0))), pool_scale=pool_scale.reshape(L, 1, D),
                 g_mix=g_mix.reshape(L, 1, D), g_ffn=g_ffn.reshape(L, 1, D), attn_sink=attn_sink,
                 rel_bias=rel_bias, g_final=g_final.reshape(1, D))

    loss, dx, _, gs = _local_step(x[0], loss_target[0], [schedule.full[0]], small, schedule)
    loss = lax.psum(loss[0, 0], ("x", "y", "c"))
    grads = _own_grads(schedule.pieces, w)

    small_all = _run_job(_GatherJob([_pack_small(gs, L, D)]), "gather_small")[0]
    small_sum = _sum_parts(small_all, jnp.int32(0), small_all[1:], F32, "small_sum")
    gsm = _unpack_small(small_sum, L, D)
    W8 = D // N_DEV
    dev = 4 * cx + 2 * cy + cc
    gsm["conv_w"] = lax.dynamic_slice_in_dim(gsm["conv_w"], dev * W8, W8, axis=3)
    grads.update(gsm)

    deltas, new_m, new_v = {}, {}, {}
    for name in WEIGHTS:
        deltas[name], new_m[name], new_v[name] = _adamw(w[name], grads[name], m[name], v[name], "adamw_" + name)

    return (loss, dx[None], *[grads[n] for n in WEIGHTS], *[deltas[n] for n in WEIGHTS],
            *[new_m[n] for n in WEIGHTS], *[new_v[n] for n in WEIGHTS])
```
